```python
import jax
import jax.numpy as jnp
from jax import lax
import numpy as np


D_MODEL = 1024
BATCH = 2
SEQ = 8192
DEPTH = 1

GRID_W = 64
CTX_LEN = 256
GLA_HEADS = 4
GLA_DK = 64
GLA_DV = 128
GLA_KEY = GLA_HEADS * GLA_DK
GLA_VAL = GLA_HEADS * GLA_DV
GLA_RANK = 16
GLA_CHUNK = 64
GATE_NORMALIZER = 16.0
POOL_WINDOWS = (2, 4, 8, 16)
POOL_GROUPS = 4
POOL_CH = 128
POOL_WIDTH = POOL_GROUPS * POOL_CH
N_BRANCHES = 2
N_EXPERTS = 16
D_EXPERT = 1024
EC_CAPACITY = 2
EPS = 1e-6
CTX_COLS = (GLA_KEY, GLA_VAL, GLA_RANK, GLA_RANK)
N_CTX_COLS = GLA_KEY + GLA_VAL + 2 * GLA_RANK
ALL_COLS = (N_CTX_COLS, GLA_KEY, GLA_VAL, POOL_WIDTH, N_BRANCHES * D_MODEL)
D_IN = N_CTX_COLS + GLA_KEY + GLA_VAL + POOL_WIDTH + N_BRANCHES * D_MODEL

kernel_name = 'hybrid_gla_pool_ecmoe_diffusion_block'


def split_cols(u, sizes):
    points, acc = [], 0
    for s in sizes[:-1]:
        acc += s
        points.append(acc)
    return jnp.split(u, points, axis=-1)


def rms_norm(x, g):
    xf = x.astype(jnp.float32)
    y = xf * lax.rsqrt(jnp.mean(xf * xf, axis=-1, keepdims=True) + EPS)
    return (y * g.astype(jnp.float32)).astype(x.dtype)


def modulate(x, g, shift, scale):
    return rms_norm(x, g) * (1 + scale) + shift


def to_heads(u, d):
    b, l, _ = u.shape
    return u.reshape(b, l, GLA_HEADS, d).transpose(0, 2, 1, 3)


def chunked(u):
    b, h, l, d = u.shape
    return u.reshape(b, h, l // GLA_CHUNK, GLA_CHUNK, d)


def gla_scan(q, k, v, log_a, h0, with_output):
    kc = chunked(k.astype(jnp.float32))
    vc = chunked(v.astype(jnp.float32))
    b = jnp.cumsum(chunked(log_a.astype(jnp.float32)), axis=3)
    b_last = b[:, :, :, -1:, :]
    kv = jnp.einsum('bhncd,bhnce->bhnde', kc * jnp.exp(b_last - b), vc)
    decay = jnp.exp(b_last[:, :, :, 0, :])

    def step(state, inp):
        dec_n, kv_n = inp
        return dec_n[..., None] * state + kv_n, state

    final, starts = lax.scan(step, h0, (jnp.moveaxis(decay, 2, 0), jnp.moveaxis(kv, 2, 0)))
    if not with_output:
        return None, final
    starts = jnp.moveaxis(starts, 0, 2)
    qc = chunked(q.astype(jnp.float32))
    inter = jnp.einsum('bhncd,bhnde->bhnce', qc * jnp.exp(b), starts)
    mid = b[:, :, :, GLA_CHUNK // 2 - 1:GLA_CHUNK // 2, :]
    scores = jnp.einsum('bhnid,bhnjd->bhnij', qc * jnp.exp(b - mid), kc * jnp.exp(mid - b))
    lower = jnp.tril(jnp.ones((GLA_CHUNK, GLA_CHUNK), dtype=bool))
    intra = jnp.einsum('bhnij,bhnje->bhnie', jnp.where(lower, scores, 0.0), vc)
    bsz, h, n, cl, dv = intra.shape
    return (inter + intra).reshape(bsz, h, n * cl, dv), final


def gla_bidirectional(q, k, v, la_f, la_b, h0_f, h0_b, with_output):
    rev = lambda u: jnp.flip(u, axis=2)
    o_f, h_f = gla_scan(q, k, v, la_f, h0_f, with_output)
    q_rev = rev(q) if with_output else None
    o_b, h_b = gla_scan(q_rev, rev(k), rev(v), rev(la_b), h0_b, with_output)
    if not with_output:
        return None, h_f, h_b
    return o_f + rev(o_b), h_f, h_b


def kv_and_decay(proj_kvd, w_decay_up, b_decay):
    k, v, r_f, r_b = split_cols(proj_kvd, CTX_COLS)
    la_f = jax.nn.log_sigmoid((r_f @ w_decay_up[0] + b_decay[0]).astype(jnp.float32)) / GATE_NORMALIZER
    la_b = jax.nn.log_sigmoid((r_b @ w_decay_up[1] + b_decay[1]).astype(jnp.float32)) / GATE_NORMALIZER
    return to_heads(k, GLA_DK), to_heads(v, GLA_DV), to_heads(la_f, GLA_DK), to_heads(la_b, GLA_DK)


def context_states(ctx, g1, shift, scale, w_in, w_decay_up, b_decay):
    h = modulate(ctx, g1, shift, scale)
    k, v, la_f, la_b = kv_and_decay(h @ w_in[:, :N_CTX_COLS], w_decay_up, b_decay)
    zero = jnp.zeros((ctx.shape[0], GLA_HEADS, GLA_DK, GLA_DV), jnp.float32)
    _, h_f, h_b = gla_bidirectional(None, k, v, la_f, la_b, zero, zero, False)
    return h_f, h_b


def box_sum(u, w, axis):
    n = u.shape[axis]
    pad = [(0, 0)] * u.ndim
    pad[axis] = (1, 0)
    cs = jnp.pad(jnp.cumsum(u, axis=axis), pad)
    pos = jnp.arange(n)
    lo = jnp.clip(pos - w // 2, 0, n)
    hi = jnp.clip(pos + w // 2, 0, n)
    s = jnp.take(cs, hi, axis=axis) - jnp.take(cs, lo, axis=axis)
    return s, (hi - lo).astype(jnp.float32)


def pool_latent(u):
    bsz, l, _ = u.shape
    rows = l // GRID_W
    ug = u.astype(jnp.float32).reshape(bsz, rows, GRID_W, POOL_GROUPS, POOL_CH)
    outs = []
    for gi, w in enumerate(POOL_WINDOWS):
        ugi = ug[:, :, :, gi, :]
        s, cnt_c = box_sum(ugi, w, 2)
        s, cnt_r = box_sum(s, w, 1)
        outs.append(s / (cnt_r[:, None] * cnt_c[None, :])[None, :, :, None] - ugi)
    return jnp.stack(outs, axis=3).reshape(bsz, l, POOL_GROUPS, POOL_CH)


def pool_context(u):
    bsz, l, _ = u.shape
    ug = u.astype(jnp.float32).reshape(bsz, l, POOL_GROUPS, POOL_CH)
    outs = []
    for gi, w in enumerate(POOL_WINDOWS):
        ugi = ug[:, :, gi, :]
        s, cnt = box_sum(ugi, w, 1)
        outs.append(s / cnt[None, :, None] - ugi)
    return jnp.stack(outs, axis=2)


def mixer_sublayer(xs, g1, shift, scale, gate, h0_f, h0_b, pool_fn, w_in, w_decay_up, b_decay,
                   gla_norm_g, w_gla_proj, pool_w, pool_scale, w_pool_proj, w_out):
    bsz, l, _ = xs.shape
    h = modulate(xs, g1, shift, scale)
    proj_kvd, q, g, pool_in, merge = split_cols(h @ w_in, ALL_COLS)
    k, v, la_f, la_b = kv_and_decay(proj_kvd, w_decay_up, b_decay)
    q = to_heads(q, GLA_DK) * (GLA_DK ** -0.5)
    o, h_f, h_b = gla_bidirectional(q, k, v, la_f, la_b, h0_f, h0_b, True)
    o = rms_norm(o, gla_norm_g).transpose(0, 2, 1, 3).reshape(bsz, l, GLA_VAL).astype(xs.dtype)
    branch_gla = (o * jax.nn.silu(g)) @ w_gla_proj
    pooled = pool_fn(pool_in)
    mixed = jnp.einsum('blgc,gcd->blgd', pooled, pool_w.astype(jnp.float32)).reshape(bsz, l, POOL_WIDTH)
    branch_pool = (mixed * pool_scale.astype(jnp.float32)).astype(xs.dtype) @ w_pool_proj
    gate_gla, gate_pool = split_cols(jax.nn.sigmoid(merge), (D_MODEL, D_MODEL))
    y = (gate_gla * branch_gla + gate_pool * branch_pool) @ w_out
    return xs + gate * y, h_f, h_b


def ec_moe(h, w_router, w_gate_e, w_up_e, w_down_e):
    l = h.shape[1]
    cap = EC_CAPACITY * l // N_EXPERTS
    aff = jax.nn.softmax((h @ w_router).astype(jnp.float32), axis=-1)
    top_aff, top_idx = lax.top_k(jnp.swapaxes(aff, 1, 2), cap)

    def per_set(hs, idx, wgt):
        xs = hs[idx]
        a = jax.nn.silu(jnp.einsum('ecd,edf->ecf', xs, w_gate_e)) * jnp.einsum('ecd,edf->ecf', xs, w_up_e)
        y = jnp.einsum('ecf,efd->ecd', a, w_down_e) * wgt[..., None].astype(hs.dtype)
        return jnp.zeros_like(hs).at[idx.reshape(-1)].add(y.reshape(-1, hs.shape[-1]))

    return jax.vmap(per_set)(h, top_idx, top_aff)


def ffn_sublayer(xs, g2, shift, scale, gate, w_router, w_gate_e, w_up_e, w_down_e):
    return xs + gate * ec_moe(modulate(xs, g2, shift, scale), w_router, w_gate_e, w_up_e, w_down_e)


def setup_inputs(seed: int = 0) -> dict:
    key = jax.random.key(seed)
    ks = jax.random.split(key, 24)
    nrm = lambda k, shape, s: jax.random.normal(k, shape, jnp.float32) * s
    return {
        'x': nrm(ks[0], (BATCH, SEQ, D_MODEL), 1.0),
        'c': nrm(ks[1], (BATCH, D_MODEL), 1.0),
        'ctx': nrm(ks[2], (BATCH, CTX_LEN, D_MODEL), 1.0),
        'c_ctx': nrm(ks[3], (D_MODEL,), 1.0),
        'ada_w': nrm(ks[4], (DEPTH, D_MODEL, 6 * D_MODEL), 0.5 * D_MODEL ** -0.5),
        'ada_b': nrm(ks[5], (DEPTH, 6 * D_MODEL), 0.02),
        'norm1_g': 1.0 + nrm(ks[6], (DEPTH, D_MODEL), 0.02),
        'norm2_g': 1.0 + nrm(ks[7], (DEPTH, D_MODEL), 0.02),
        'w_in': nrm(ks[8], (DEPTH, D_MODEL, D_IN), D_MODEL ** -0.5),
        'w_decay_up': nrm(ks[9], (DEPTH, 2, GLA_RANK, GLA_KEY), GLA_RANK ** -0.5),
        'b_decay': nrm(ks[10], (DEPTH, 2, GLA_KEY), 0.1),
        'gla_norm_g': 1.0 + nrm(ks[11], (DEPTH, GLA_DV), 0.02),
        'w_gla_proj': nrm(ks[12], (DEPTH, GLA_VAL, D_MODEL), GLA_VAL ** -0.5),
        'pool_w': nrm(ks[13], (DEPTH, POOL_GROUPS, POOL_CH, POOL_CH), POOL_CH ** -0.5),
        'pool_scale': 1.0 + nrm(ks[14], (DEPTH, POOL_WIDTH), 0.1),
        'w_pool_proj': nrm(ks[15], (DEPTH, POOL_WIDTH, D_MODEL), POOL_WIDTH ** -0.5),
        'w_out': nrm(ks[16], (DEPTH, D_MODEL, D_MODEL), D_MODEL ** -0.5),
        'w_router': nrm(ks[17], (DEPTH, D_MODEL, N_EXPERTS), D_MODEL ** -0.5),
        'w_gate_e': nrm(ks[18], (DEPTH, N_EXPERTS, D_MODEL, D_EXPERT), D_MODEL ** -0.5),
        'w_up_e': nrm(ks[19], (DEPTH, N_EXPERTS, D_MODEL, D_EXPERT), D_MODEL ** -0.5),
        'w_down_e': nrm(ks[20], (DEPTH, N_EXPERTS, D_EXPERT, D_MODEL), D_EXPERT ** -0.5),
        'final_norm_g': 1.0 + nrm(ks[21], (D_MODEL,), 0.02),
    }


def reference(x, c, ctx, c_ctx, ada_w, ada_b, norm1_g, norm2_g, w_in, w_decay_up, b_decay,
              gla_norm_g, w_gla_proj, pool_w, pool_scale, w_pool_proj, w_out, w_router,
              w_gate_e, w_up_e, w_down_e, final_norm_g):
    zero_state = jnp.zeros((ctx.shape[0], GLA_HEADS, GLA_DK, GLA_DV), jnp.float32)
    for i in range(DEPTH):
        sh1, sc1, gt1, sh2, sc2, gt2 = jnp.split((jax.nn.silu(c) @ ada_w[i] + ada_b[i])[:, None, :], 6, axis=-1)
        csh1, csc1, cgt1, csh2, csc2, cgt2 = jnp.split(jax.nn.silu(c_ctx) @ ada_w[i] + ada_b[i], 6, axis=-1)
        mix_w = (w_in[i], w_decay_up[i], b_decay[i], gla_norm_g[i], w_gla_proj[i],
                 pool_w[i], pool_scale[i], w_pool_proj[i], w_out[i])
        moe_w = (w_router[i], w_gate_e[i], w_up_e[i], w_down_e[i])
        if i == DEPTH - 1:
            h_f, h_b = context_states(ctx, norm1_g[i], csh1, csc1, w_in[i], w_decay_up[i], b_decay[i])
        else:
            ctx_mixed, h_f, h_b = mixer_sublayer(ctx, norm1_g[i], csh1, csc1, cgt1, zero_state, zero_state,
                                                 pool_context, *mix_w)
            ctx = ffn_sublayer(ctx_mixed, norm2_g[i], csh2, csc2, cgt2, *moe_w)
        x, _, _ = mixer_sublayer(x, norm1_g[i], sh1, sc1, gt1, h_f, h_b, pool_latent, *mix_w)
        x = ffn_sublayer(x, norm2_g[i], sh2, sc2, gt2, *moe_w)
    return rms_norm(x, final_norm_g)
```

```python
import functools

import jax
import jax.numpy as jnp
from jax import lax
from jax.experimental import pallas as pl
from jax.experimental.pallas import tpu as pltpu

F32 = jnp.float32
BF16 = jnp.bfloat16
I32 = jnp.int32
HIGHEST = lax.Precision.HIGHEST

EPS = 1e-6
GRID_W = 64
GLA_CHUNK = 64
GLA_STAGES = 6
GATE_NORMALIZER = 16.0
POOL_WINDOWS = (2, 4, 8, 16)
EC_CAPACITY = 2
LANES = 128
ROUTE_BLK = 256
SLOT_WIN = 64
VMEM_LIMIT = 56 * 1024 * 1024

_NT = (((1,), (1,)), ((), ()))
_TN = (((0,), (0,)), ((), ()))


def _cparams(sem):
    return pltpu.CompilerParams(dimension_semantics=sem, vmem_limit_bytes=VMEM_LIMIT)


def _silu(x):
    return x * jax.nn.sigmoid(x)


def _ada_body(c_ref, w_ref, b_ref, o_ref):
    s = _silu(c_ref[...])
    o_ref[...] = jnp.dot(s, w_ref[...], precision=HIGHEST, preferred_element_type=F32) + b_ref[...]


def _ada(cin, w, b):
    rows, d = cin.shape
    n = w.shape[1]
    nb = n // 4
    return pl.pallas_call(
        _ada_body,
        grid=(n // nb,),
        in_specs=[pl.BlockSpec((rows, d), lambda j: (0, 0)),
                  pl.BlockSpec((d, nb), lambda j: (0, j)),
                  pl.BlockSpec((1, nb), lambda j: (0, j))],
        out_specs=pl.BlockSpec((rows, nb), lambda j: (0, j)),
        out_shape=jax.ShapeDtypeStruct((rows, n), F32),
        compiler_params=_cparams(("arbitrary",)),
        name="ada",
    )(cin, w, b)


def _fold_body(wr_ref, up_ref, o_ref):
    o_ref[...] = jnp.dot(wr_ref[...], up_ref[...], precision=HIGHEST, preferred_element_type=F32)


def _fold(wr, up):
    return pl.pallas_call(
        _fold_body,
        out_shape=jax.ShapeDtypeStruct((wr.shape[0], up.shape[1]), F32),
        name="fold",
    )(wr, up)


def _modulated_norm(x, mult, shift):
    ms = jnp.mean(x * x, axis=-1, keepdims=True)
    return (x * lax.rsqrt(ms + EPS)) * mult + shift


def _log_sigmoid(z):
    return jnp.minimum(z, 0.0) - jnp.log1p(jnp.exp(-jnp.abs(z)))


def _inproj_body(x_ref, mult_ref, shift_ref, w_ref, bz_ref,
                 k_ref, v_ref, q_ref, laf_ref, lab_ref, sg_ref, p_ref, *, dk, dv, pw):
    h = _modulated_norm(x_ref[...], mult_ref[0], shift_ref[0])
    u = jnp.dot(h.astype(BF16), w_ref[...], preferred_element_type=F32)
    o = 0
    k_ref[...] = u[:, o:o + dk].astype(BF16); o += dk
    v_ref[...] = u[:, o:o + dv].astype(BF16); o += dv
    q_ref[...] = u[:, o:o + dk].astype(BF16); o += dk
    zf = u[:, o:o + dk] + bz_ref[0:1, :]; o += dk
    zb = u[:, o:o + dk] + bz_ref[1:2, :]; o += dk
    laf_ref[...] = _log_sigmoid(zf) * (1.0 / GATE_NORMALIZER)
    lab_ref[...] = _log_sigmoid(zb) * (1.0 / GATE_NORMALIZER)
    sg_ref[...] = _silu(u[:, o:o + dv]).astype(BF16); o += dv
    p_ref[...] = u[:, o:o + pw].astype(BF16)


def _inproj(x2, mult, shift, w, bz, rows_per_sample, dk, dv, pw, tm):
    n, d = x2.shape
    tps = rows_per_sample // tm
    row = lambda i: (i, 0)
    vec = lambda i: (i // tps, 0, 0)
    outs = [(dk, BF16), (dv, BF16), (dk, BF16), (dk, F32), (dk, F32), (dv, BF16), (pw, BF16)]
    return pl.pallas_call(
        functools.partial(_inproj_body, dk=dk, dv=dv, pw=pw),
        grid=(n // tm,),
        in_specs=[pl.BlockSpec((tm, d), row),
                  pl.BlockSpec((1, 1, d), vec),
                  pl.BlockSpec((1, 1, d), vec),
                  pl.BlockSpec(w.shape, lambda i: (0, 0)),
                  pl.BlockSpec(bz.shape, lambda i: (0, 0))],
        out_specs=[pl.BlockSpec((tm, c), row) for c, _ in outs],
        out_shape=[jax.ShapeDtypeStruct((n, c), t) for c, t in outs],
        compiler_params=_cparams(("parallel",)),
        name="inproj",
    )(x2, mult, shift, w, bz)


def _gla_direction(k_ref, q_ref, v_ref, la_ref, o_ref, s_ref, reverse, nchunk):
    C = GLA_CHUNK
    hd = C
    cb = 4 * C
    lt = nchunk * C
    la = la_ref[0]
    r = lax.broadcasted_iota(I32, (cb, cb), 0)
    c = lax.broadcasted_iota(I32, (cb, cb), 1)
    same = (r // C) == (c // C)
    cum = jnp.where(same & ((c >= r) if reverse else (c <= r)), 1.0, 0.0).astype(BF16)
    la_hi = la.astype(BF16)
    la_lo = (la - la_hi.astype(F32)).astype(BF16)
    la2 = jnp.concatenate([la_hi, la_lo], axis=1)
    bcs = []
    for blk in range(lt // cb):
        part = jnp.dot(cum, la2[blk * cb:(blk + 1) * cb], preferred_element_type=F32)
        bcs.append(part[:, :LANES] + part[:, LANES:])
    ri = lax.broadcasted_iota(I32, (2 * C, C), 0) % C
    ci = lax.broadcasted_iota(I32, (2 * C, C), 1)
    tri = (ci >= ri) if reverse else (ci <= ri)
    lane = lax.broadcasted_iota(I32, (1, LANES), 1)
    m0 = (lane < hd).astype(F32)
    m1 = (lane >= hd).astype(F32)
    sr = lax.broadcasted_iota(I32, (2 * LANES, LANES), 0)
    sl = lax.broadcasted_iota(I32, (2 * LANES, LANES), 1)
    smask = ((sr < LANES) == (sl < hd)).astype(F32)
    kt = k_ref[0].astype(F32)
    qt = q_ref[0].astype(F32)
    zero_v = jnp.zeros((C, LANES), BF16)
    order = list(range(nchunk - 1, -1, -1) if reverse else range(nchunk))
    intra, qds, kvs, decs, q2s, kss, kws, scs = {}, {}, {}, {}, {}, {}, {}, {}
    for ch in order:
        lo = ch * C
        b = bcs[lo // cb][lo % cb:lo % cb + C]
        last = b[0:1] if reverse else b[C - 1:C]
        mid = b[C // 2:C // 2 + 1] if reverse else b[C // 2 - 1:C // 2]
        kc = kt[lo:lo + C]
        qc = qt[lo:lo + C]
        qs = qc * jnp.exp(b - mid)
        kss[ch] = (kc * jnp.exp(mid - b)).astype(BF16)
        qds[ch] = (qc * jnp.exp(b)).astype(BF16)
        kws[ch] = (kc * jnp.exp(last - b)).astype(BF16)
        decs[ch] = jnp.exp(last)
        q2s[ch] = jnp.concatenate([qs * m0, qs * m1], axis=0).astype(BF16)
    yield
    for ch in order:
        scs[ch] = lax.dot_general(q2s[ch], kss[ch], _NT, preferred_element_type=F32)
    yield
    for ch in order:
        v2 = v_ref[0, ch * C:(ch + 1) * C, :]
        kvs[ch] = lax.dot_general(v2, kws[ch], _TN, preferred_element_type=F32) * smask
    yield
    for ch in order:
        sc = jnp.where(tri, scs[ch], 0.0).astype(BF16)
        sc2 = jnp.concatenate([sc[:C], sc[C:]], axis=1)
        v2 = v_ref[0, ch * C:(ch + 1) * C, :]
        vbd = jnp.concatenate([jnp.concatenate([v2[:, :LANES], zero_v], axis=1),
                               jnp.concatenate([zero_v, v2[:, LANES:]], axis=1)], axis=0)
        intra[ch] = jnp.dot(sc2, vbd, preferred_element_type=F32)
    yield
    st = s_ref[...]
    starts = {}
    for ch in order:
        starts[ch] = st.astype(BF16)
        st = st * decs[ch] + kvs[ch]
    s_ref[...] = st
    yield
    for ch in order:
        inter = lax.dot_general(qds[ch], starts[ch], _NT, preferred_element_type=F32)
        o_ref[0, ch * C:(ch + 1) * C, :] = inter + intra[ch]
    yield


def _gla_body(kf, qf, vf, laf, kb, qb, vb, lab, h0f, h0b, of, ob, hf_out, hb_out, sf, sb, *, nchunk):
    i = pl.program_id(2)

    @pl.when(i == 0)
    def _():
        sf[...] = h0f[0, 0]
        sb[...] = h0b[0, 0]

    sweeps = [_gla_direction(kf, qf, vf, laf, of, sf, False, nchunk),
              _gla_direction(kb, qb, vb, lab, ob, sb, True, nchunk)]
    for _ in range(GLA_STAGES):
        for sweep in sweeps:
            next(sweep)

    @pl.when(i == pl.num_programs(2) - 1)
    def _():
        hf_out[0, 0] = sf[...]
        hb_out[0, 0] = sb[...]


def _gla(k, q, v, laf, lab, h0f, h0b, lt):
    bsz, l, _ = k.shape
    pairs = h0f.shape[1]
    nt = l // lt
    fwd = lambda b, hp, i: (b, i, hp)
    bwd = lambda b, hp, i: (b, nt - 1 - i, hp)
    st = lambda b, hp, i: (b, hp, 0, 0)
    kq = lambda m: pl.BlockSpec((1, lt, LANES), m)
    vv = lambda m: pl.BlockSpec((1, lt, 2 * LANES), m)
    sspec = pl.BlockSpec((1, 1, 2 * LANES, LANES), st)
    return pl.pallas_call(
        functools.partial(_gla_body, nchunk=lt // GLA_CHUNK),
        grid=(bsz, pairs, nt),
        in_specs=[kq(fwd), kq(fwd), vv(fwd), kq(fwd), kq(bwd), kq(bwd), vv(bwd), kq(bwd), sspec, sspec],
        out_specs=[vv(fwd), vv(bwd), sspec, sspec],
        out_shape=[jax.ShapeDtypeStruct(v.shape, F32), jax.ShapeDtypeStruct(v.shape, F32),
                   jax.ShapeDtypeStruct(h0f.shape, F32), jax.ShapeDtypeStruct(h0b.shape, F32)],
        scratch_shapes=[pltpu.VMEM((2 * LANES, LANES), F32), pltpu.VMEM((2 * LANES, LANES), F32)],
        compiler_params=_cparams(("parallel", "parallel", "arbitrary")),
        name="gla",
    )(k, q, v, laf, k, q, v, lab, h0f, h0b)


def _pool_body(p_ref, w_ref, sc_ref, o_ref, *, half, rows):
    x = p_ref[0].astype(F32)
    l = x.shape[0]
    t = lax.broadcasted_iota(I32, (l, 1), 0)
    col = t % GRID_W
    row = t // GRID_W

    def box(a, pos, n, unit):
        fwd = a
        bwd = jnp.where(pos >= 1, pltpu.roll(a, unit, axis=0), 0.0)
        s = 1
        while s < half:
            fwd = fwd + jnp.where(pos + s <= n - 1, pltpu.roll(fwd, l - s * unit, axis=0), 0.0)
            bwd = bwd + jnp.where(pos >= s, pltpu.roll(bwd, s * unit, axis=0), 0.0)
            s *= 2
        cnt = jnp.minimum(pos + half, n) - jnp.maximum(pos - half, 0)
        return fwd + bwd, cnt.astype(F32)

    s1, cnt_c = box(x, col, GRID_W, 1)
    s2, cnt_r = box(s1, row, rows, GRID_W)
    pooled = s2 / (cnt_r * cnt_c) - x
    mixed = jnp.dot(pooled.astype(BF16), w_ref[0].astype(BF16), preferred_element_type=F32)
    o_ref[0] = (mixed * sc_ref[0]).astype(BF16)


def _pool(pin, pool_w, pool_scale3, g):
    bsz, l, pw = pin.shape
    ch = pool_w.shape[-1]
    half = POOL_WINDOWS[g] // 2
    return pl.pallas_call(
        functools.partial(_pool_body, half=half, rows=l // GRID_W),
        grid=(bsz,),
        in_specs=[pl.BlockSpec((1, l, ch), lambda b: (b, 0, g)),
                  pl.BlockSpec((1, ch, ch), lambda b: (g, 0, 0)),
                  pl.BlockSpec((1, 1, ch), lambda b: (g, 0, 0))],
        out_specs=pl.BlockSpec((1, l, ch), lambda b: (b, 0, 0)),
        out_shape=jax.ShapeDtypeStruct((bsz, l, ch), BF16),
        compiler_params=_cparams(("parallel",)),
        name=f"pool{g}",
    )(pin, pool_w, pool_scale3)


def _merge_body(x_ref, m1_ref, s1_ref, g1_ref, m2_ref, s2_ref,
                of_ref, ob_ref, sg_ref, mx0, mx1, mx2, mx3,
                wm_ref, wgla_ref, wpool_ref, wout_ref, gn_ref, wr_ref,
                x1_ref, h2_ref, aff_ref, *, heads, ne):
    x = x_ref[...]
    d = x.shape[1]
    h = _modulated_norm(x, m1_ref[0], s1_ref[0]).astype(BF16)
    gates = jax.nn.sigmoid(jnp.dot(h, wm_ref[...], preferred_element_type=F32))
    o = of_ref[...] + ob_ref[...]
    sg = sg_ref[...].astype(F32)
    og = []
    for j in range(heads):
        oj = o[:, j * LANES:(j + 1) * LANES]
        oj = oj * lax.rsqrt(jnp.mean(oj * oj, axis=-1, keepdims=True) + EPS) * gn_ref[...]
        og.append((oj * sg[:, j * LANES:(j + 1) * LANES]).astype(BF16))
    bg = jnp.dot(jnp.concatenate(og, axis=1), wgla_ref[...], preferred_element_type=F32)
    mixed = jnp.concatenate([mx0[0], mx1[0], mx2[0], mx3[0]], axis=1)
    bp = jnp.dot(mixed, wpool_ref[...], preferred_element_type=F32)
    z = gates[:, :d] * bg + gates[:, d:] * bp
    y = jnp.dot(z.astype(BF16), wout_ref[...], preferred_element_type=F32)
    x1 = x + g1_ref[0] * y
    x1_ref[...] = x1
    h2 = _modulated_norm(x1, m2_ref[0], s2_ref[0])
    hi = h2.astype(BF16)
    h2_ref[...] = hi
    lo = (h2 - hi.astype(F32)).astype(BF16)
    lg = (jnp.dot(hi, wr_ref[...], preferred_element_type=F32)
          + jnp.dot(lo, wr_ref[...], preferred_element_type=F32))
    lgt = lg.T
    logit = lgt[0:ne] + lgt[ne:2 * ne]
    mx = jnp.max(logit, axis=0, keepdims=True)
    ex = jnp.exp(logit - mx)
    aff_ref[0] = ex / jnp.sum(ex, axis=0, keepdims=True)


def _merge(x2, vecs, of, ob, sg, mixed, wm, wgla, wpool, wout, gn, wr, rows_per_sample, heads, ne, tm):
    n, d = x2.shape
    bsz = n // rows_per_sample
    tps = rows_per_sample // tm
    row = lambda i: (i, 0)
    vec = lambda i: (i // tps, 0, 0)
    full = lambda a: pl.BlockSpec(a.shape, lambda i: (0,) * a.ndim)
    ch = mixed[0].shape[-1]
    mspec = pl.BlockSpec((1, tm, ch), lambda i: (i // tps, i % tps, 0))
    gv = of.shape[-1]
    return pl.pallas_call(
        functools.partial(_merge_body, heads=heads, ne=ne),
        grid=(n // tm,),
        in_specs=[pl.BlockSpec((tm, d), row)] + [pl.BlockSpec((1, 1, d), vec)] * 5
                 + [pl.BlockSpec((tm, gv), row)] * 3 + [mspec] * 4
                 + [full(wm), full(wgla), full(wpool), full(wout), full(gn), full(wr)],
        out_specs=[pl.BlockSpec((tm, d), row),
                   pl.BlockSpec((tm, d), row),
                   pl.BlockSpec((1, ne, tm), lambda i: (i // tps, 0, i % tps))],
        out_shape=[jax.ShapeDtypeStruct((n, d), F32),
                   jax.ShapeDtypeStruct((n, d), BF16),
                   jax.ShapeDtypeStruct((bsz, ne, rows_per_sample), F32)],
        compiler_params=_cparams(("parallel",)),
        name="merge",
    )(x2, *vecs, of, ob, sg, *mixed, wm, wgla, wpool, wout, gn, wr)


def _route_body(aff_ref, pos_ref, off_ref, cnt_ref, *, cap, ntb):
    a = aff_ref[0]
    ne, l = a.shape
    blk = ROUTE_BLK

    def bisect(i, v):
        cand = v | jnp.left_shift(jnp.int32(1), 30 - i)
        cnt = jnp.sum((a >= lax.bitcast_convert_type(cand, F32)).astype(F32), axis=1, keepdims=True)
        return jnp.where(cnt >= cap, cand, v)

    thr = lax.bitcast_convert_type(lax.fori_loop(0, 31, bisect, jnp.zeros((ne, 1), I32)), F32)
    gt = a > thr
    tie = a == thr
    need = cap - jnp.sum(gt.astype(F32), axis=1, keepdims=True)

    r = lax.broadcasted_iota(I32, (blk, blk), 0)
    c = lax.broadcasted_iota(I32, (blk, blk), 1)
    upper = (r <= c).astype(BF16)
    lane = lax.broadcasted_iota(I32, (1, LANES), 1)

    def prefix(mask_f):
        run = jnp.zeros((ne, 1), F32)
        offs = jnp.zeros((ne, LANES), F32)
        for tb in range(ntb):
            m = mask_f[:, tb * blk:(tb + 1) * blk].astype(BF16)
            loc = jnp.dot(m, upper, preferred_element_type=F32)
            cnt_ref[:, tb * blk:(tb + 1) * blk] = loc + run
            offs = jnp.where(lane == tb, run, offs)
            run = run + loc[:, blk - 1:blk]
        return jnp.where(lane >= ntb, run, offs)

    tie_f = tie.astype(F32)
    prefix(tie_f)
    tie_excl = cnt_ref[...] - tie_f
    sel = gt | (tie & (tie_excl < need))
    offs = prefix(sel.astype(F32))
    pos_ref[0] = jnp.where(sel, cnt_ref[...] - 1.0, -1.0).astype(I32)
    off_ref[0] = offs.astype(I32)


def _route(aff, cap):
    bsz, ne, l = aff.shape
    spec = lambda s: pl.BlockSpec((1,) + s, lambda b: (b, 0, 0))
    return pl.pallas_call(
        functools.partial(_route_body, cap=cap, ntb=l // ROUTE_BLK),
        grid=(bsz,),
        in_specs=[spec((ne, l))],
        out_specs=[spec((ne, l)), spec((ne, LANES))],
        out_shape=[jax.ShapeDtypeStruct((bsz, ne, l), I32),
                   jax.ShapeDtypeStruct((bsz, ne, LANES), I32)],
        scratch_shapes=[pltpu.VMEM((ne, l), F32)],
        compiler_params=_cparams(("parallel",)),
        name="route",
    )(aff)


def _window_plan(off_ref, bb, tt, experts, ne):
    lows = [(off_ref[(bb * ne + e) * LANES + tt] // 16) * 16 for e in experts]
    ends = [off_ref[(bb * ne + e) * LANES + tt + 1] for e in experts]
    return lows, ends


def _gatherx_body(off_ref, h_ref, pos_ref, xs_ref, *, ne, cap, eg):
    b = pl.program_id(0)
    g = pl.program_id(1)
    tb = pl.program_id(2)
    win = SLOT_WIN

    @pl.when(tb == 0)
    def _():
        xs_ref[...] = jnp.zeros(xs_ref.shape, BF16)

    lows, ends = _window_plan(off_ref, b, tb, [g * eg + k for k in range(eg)], ne)
    rounds = jnp.int32(0)
    for k in range(eg):
        rounds = jnp.maximum(rounds, (ends[k] - lows[k] + win - 1) // win)
    j_col = lax.broadcasted_iota(I32, (win, 1), 0)

    def one_round(r, carry):
        starts = [pl.multiple_of(jnp.minimum(lows[k] + r * win, cap - win), 16) for k in range(eg)]
        pieces = []
        for k in range(eg):
            p = pos_ref[0, k:k + 1, :]
            hit = (p - starts[k] == j_col) & (p >= lows[k] + r * win)
            pieces.append(jnp.where(hit, 1.0, 0.0).astype(BF16))
        sel = jnp.concatenate(pieces, axis=0)
        rows = jnp.dot(sel, h_ref[...], preferred_element_type=F32).astype(BF16)
        for k in range(eg):
            dst = (0, k, pl.ds(starts[k], win), slice(None))
            xs_ref[dst] = xs_ref[dst] + rows[k * win:(k + 1) * win]
        return carry

    lax.fori_loop(0, rounds, one_round, 0)


def _gatherx(off_flat, h2, pos, cap):
    n, d = h2.shape
    bsz, ne, l = pos.shape
    t = ROUTE_BLK
    ntb = l // t
    eg = 8
    grid_spec = pltpu.PrefetchScalarGridSpec(
        num_scalar_prefetch=1,
        grid=(bsz, ne // eg, ntb),
        in_specs=[pl.BlockSpec((t, d), lambda b, g, i, off: (b * ntb + i, 0)),
                  pl.BlockSpec((1, eg, t), lambda b, g, i, off: (b, g, i))],
        out_specs=pl.BlockSpec((1, eg, cap, d), lambda b, g, i, off: (b, g, 0, 0)),
    )
    return pl.pallas_call(
        functools.partial(_gatherx_body, ne=ne, cap=cap, eg=eg),
        grid_spec=grid_spec,
        out_shape=jax.ShapeDtypeStruct((bsz, ne, cap, d), BF16),
        compiler_params=_cparams(("parallel", "parallel", "arbitrary")),
        name="gatherx",
    )(off_flat, h2, pos)


def _moe_body(xs_ref, wg_ref, wu_ref, wd_ref, y_ref, *, rc):
    wg = wg_ref[0].astype(BF16)
    wu = wu_ref[0].astype(BF16)
    wd = wd_ref[0].astype(BF16)
    cap = xs_ref.shape[2]
    acts = []
    for ch in range(cap // rc):
        xs = xs_ref[0, 0, ch * rc:(ch + 1) * rc, :]
        gate = jnp.dot(xs, wg, preferred_element_type=F32)
        up = jnp.dot(xs, wu, preferred_element_type=F32)
        acts.append((_silu(gate) * up).astype(BF16))
    for ch in range(cap // rc):
        y_ref[0, 0, ch * rc:(ch + 1) * rc, :] = jnp.dot(acts[ch], wd, preferred_element_type=F32).astype(BF16)


def _moe(xs, wg, wu, wd):
    bsz, ne, cap, d = xs.shape
    de = wg.shape[2]
    slot = pl.BlockSpec((1, 1, cap, d), lambda b, e: (b, e, 0, 0))
    return pl.pallas_call(
        functools.partial(_moe_body, rc=min(cap, 512)),
        grid=(bsz, ne),
        in_specs=[slot,
                  pl.BlockSpec((1, d, de), lambda b, e: (e, 0, 0)),
                  pl.BlockSpec((1, d, de), lambda b, e: (e, 0, 0)),
                  pl.BlockSpec((1, de, d), lambda b, e: (e, 0, 0))],
        out_specs=slot,
        out_shape=jax.ShapeDtypeStruct((bsz, ne, cap, d), BF16),
        compiler_params=_cparams(("parallel", "arbitrary")),
        name="moe",
    )(xs, wg, wu, wd)


def _combine_body(off_ref, x1_ref, g2_ref, fg_ref, pos_ref, aff_ref, y_hbm, o_ref, ybuf, sem,
                  *, ne, cap, ntb):
    b = pl.program_id(0)
    tb = pl.program_id(1)
    step = b * ntb + tb
    nsteps = pl.num_programs(0) * ntb
    slot = step % 2
    spare = 2
    win = SLOT_WIN

    def block_lows(bb, tt):
        return _window_plan(off_ref, bb, tt, range(ne), ne)[0]

    def window_copy(bb, e, start, buf):
        return pltpu.make_async_copy(y_hbm.at[bb, e, pl.ds(start, win), :],
                                     ybuf.at[buf, pl.ds(e * win, win), :], sem.at[buf, e])

    def window_starts(lows, r):
        return [pl.multiple_of(jnp.minimum(lows[e] + r * win, cap - win), 16) for e in range(ne)]

    def start_first_round(bb, tt, buf):
        starts = window_starts(block_lows(bb, tt), 0)
        for e in range(ne):
            window_copy(bb, e, starts[e], buf).start()

    @pl.when(step == 0)
    def _():
        start_first_round(b, tb, 0)

    @pl.when(step + 1 < nsteps)
    def _():
        nxt = step + 1
        start_first_round(nxt // ntb, nxt % ntb, 1 - slot)

    lows, ends = _window_plan(off_ref, b, tb, range(ne), ne)
    rounds = jnp.int32(0)
    for e in range(ne):
        rounds = jnp.maximum(rounds, (ends[e] - lows[e] + win - 1) // win)
    j_col = lax.broadcasted_iota(I32, (win, 1), 0)

    def expand(r, starts, buf):
        pieces = []
        for e in range(ne):
            p = pos_ref[0, e:e + 1, :]
            valid = p >= lows[e] + r * win
            hit = (p - starts[e] == j_col) & valid
            pieces.append(jnp.where(hit, aff_ref[0, e:e + 1, :], 0.0).astype(BF16))
        pmat = jnp.concatenate(pieces, axis=0)
        return lax.dot_general(pmat, ybuf[buf], _TN, preferred_element_type=F32)

    starts0 = window_starts(lows, 0)
    for e in range(ne):
        window_copy(b, e, starts0[e], slot).wait()
    moe = expand(0, starts0, slot)

    def extra_round(r, acc):
        starts = window_starts(lows, r)
        for e in range(ne):
            window_copy(b, e, starts[e], spare).start()
        for e in range(ne):
            window_copy(b, e, starts[e], spare).wait()
        return acc + expand(r, starts, spare)

    moe = lax.fori_loop(1, rounds, extra_round, moe)
    x2 = x1_ref[...] + g2_ref[0] * moe
    ms = jnp.mean(x2 * x2, axis=-1, keepdims=True)
    o_ref[...] = x2 * lax.rsqrt(ms + EPS) * fg_ref[...]


def _combine(off_flat, x1, g2, fg, pos, aff, y, rows_per_sample):
    n, d = x1.shape
    bsz, ne, cap, _ = y.shape
    t = ROUTE_BLK
    ntb = rows_per_sample // t
    grid_spec = pltpu.PrefetchScalarGridSpec(
        num_scalar_prefetch=1,
        grid=(bsz, ntb),
        in_specs=[pl.BlockSpec((t, d), lambda b, i, off: (b * ntb + i, 0)),
                  pl.BlockSpec((1, 1, d), lambda b, i, off: (b, 0, 0)),
                  pl.BlockSpec((1, d), lambda b, i, off: (0, 0)),
                  pl.BlockSpec((1, ne, t), lambda b, i, off: (b, 0, i)),
                  pl.BlockSpec((1, ne, t), lambda b, i, off: (b, 0, i)),
                  pl.BlockSpec(memory_space=pl.ANY)],
        out_specs=pl.BlockSpec((t, d), lambda b, i, off: (b * ntb + i, 0)),
        scratch_shapes=[pltpu.VMEM((3, ne * SLOT_WIN, d), BF16), pltpu.SemaphoreType.DMA((3, ne))],
    )
    return pl.pallas_call(
        functools.partial(_combine_body, ne=ne, cap=cap, ntb=ntb),
        grid_spec=grid_spec,
        out_shape=jax.ShapeDtypeStruct((n, d), F32),
        compiler_params=_cparams(("arbitrary", "arbitrary")),
        name="combine",
    )(off_flat, x1, g2, fg, pos, aff, y)


def kernel(x, c, ctx, c_ctx, ada_w, ada_b, norm1_g, norm2_g, w_in, w_decay_up, b_decay, gla_norm_g,
           w_gla_proj, pool_w, pool_scale, w_pool_proj, w_out, w_router, w_gate_e, w_up_e, w_down_e,
           final_norm_g):
    assert ada_w.shape[0] == 1, "single-layer block"
    bsz, l, d = x.shape
    lc = ctx.shape[1]
    rank, dk = w_decay_up.shape[2], w_decay_up.shape[3]
    dvh = gla_norm_g.shape[1]
    dv = w_gla_proj.shape[1]
    heads = dv // dvh
    groups, ch = pool_w.shape[1], pool_w.shape[2]
    pw = groups * ch
    ne = w_router.shape[2]
    cap = EC_CAPACITY * l // ne
    assert dk // heads == 64 and dvh == LANES and ch == LANES and heads % 2 == 0
    assert l % ROUTE_BLK == 0 and l // ROUTE_BLK < LANES and cap >= SLOT_WIN and cap % 16 == 0

    cin = jnp.zeros((8, d), F32).at[:bsz].set(c).at[bsz].set(c_ctx)
    mods = _ada(cin, ada_w[0], ada_b[0][None, :])
    sh1, sc1, gt1, sh2, sc2, gt2 = [mods[:, i * d:(i + 1) * d] for i in range(6)]
    vec3 = lambda a: a[:, None, :]
    mult1 = norm1_g[0][None, :] * (1.0 + sc1)
    mult2 = norm2_g[0][None, :] * (1.0 + sc2)

    wi = w_in[0]
    o_r = dk + dv
    o_q = o_r + 2 * rank
    o_g = o_q + dk
    o_p = o_g + dv
    o_m = o_p + pw
    up = jnp.zeros((2 * rank, 2 * dk), F32)
    up = up.at[:rank, :dk].set(w_decay_up[0, 0]).at[rank:, dk:].set(w_decay_up[0, 1])
    wz = _fold(wi[:, o_r:o_q], up)
    w1 = jnp.concatenate([wi[:, :o_r], wi[:, o_q:o_g] * (float(dk // heads) ** -0.5), wz, wi[:, o_g:o_m]],
                         axis=1).astype(BF16)
    wm = wi[:, o_m:].astype(BF16)
    bz = b_decay[0]

    zero_state = jnp.zeros((bsz, heads // 2, 2 * LANES, LANES), F32)
    cm = jnp.broadcast_to(vec3(mult1[bsz:bsz + 1]), (bsz, 1, d))
    cs = jnp.broadcast_to(vec3(sh1[bsz:bsz + 1]), (bsz, 1, d))
    ck, cv, _, claf, clab, _, _ = _inproj(ctx.reshape(bsz * lc, d), cm, cs, w1, bz, lc, dk, dv, pw, min(lc, 256))
    r3 = lambda a, n: a.reshape(bsz, n, a.shape[-1])
    _, _, h_f, h_b = _gla(r3(ck, lc), r3(ck, lc), r3(cv, lc), r3(claf, lc), r3(clab, lc),
                          zero_state, zero_state, min(lc, 256))

    x2 = x.reshape(bsz * l, d)
    tm = min(l, 512)
    k, v, q, laf, lab, sg, pin = _inproj(x2, vec3(mult1[:bsz]), vec3(sh1[:bsz]), w1, bz, l, dk, dv, pw, tm)
    of, ob, _, _ = _gla(r3(k, l), r3(q, l), r3(v, l), r3(laf, l), r3(lab, l), h_f, h_b, min(l, 512))
    pin3 = r3(pin, l)
    psc = pool_scale[0].reshape(groups, 1, ch)
    mixed = [_pool(pin3, pool_w[0], psc, g) for g in range(groups)]

    wr_hi = w_router[0].astype(BF16)
    wr_lo = (w_router[0] - wr_hi.astype(F32)).astype(BF16)
    wr = jnp.zeros((d, LANES), BF16).at[:, :ne].set(wr_hi).at[:, ne:2 * ne].set(wr_lo)
    vecs = [vec3(mult1[:bsz]), vec3(sh1[:bsz]), vec3(gt1[:bsz]), vec3(mult2[:bsz]), vec3(sh2[:bsz])]
    x1, h2, aff = _merge(x2, vecs, of.reshape(bsz * l, dv), ob.reshape(bsz * l, dv), sg, mixed,
                          wm, w_gla_proj[0].astype(BF16), w_pool_proj[0].astype(BF16), w_out[0].astype(BF16),
                          gla_norm_g[0][None, :], wr, l, heads, ne, tm)

    pos, offs = _route(aff, cap)
    off_flat = offs.reshape(-1)
    xs = _gatherx(off_flat, h2, pos, cap)
    y = _moe(xs, w_gate_e[0], w_up_e[0], w_down_e[0])
    out = _combine(off_flat, x1, vec3(gt2[:bsz]), final_norm_g[None, :], pos, aff, y, l)
    return out.reshape(bsz, l, d)
```

```python
import functools

import jax
import jax.numpy as jnp
from jax import lax
from jax.experimental import pallas as pl
from jax.experimental.pallas import tpu as pltpu

F32 = jnp.float32
BF16 = jnp.bfloat16
I32 = jnp.int32
HIGHEST = lax.Precision.HIGHEST

EPS = 1e-6
GRID_W = 64
GLA_CHUNK = 64
GLA_STAGES = 6
GATE_NORMALIZER = 16.0
POOL_WINDOWS = (2, 4, 8, 16)
EC_CAPACITY = 2
LANES = 128
ROUTE_BLK = 256
SLOT_WIN = 64
SUB_ROWS = 256
VMEM_LIMIT = 56 * 1024 * 1024

_NT = (((1,), (1,)), ((), ()))
_TN = (((0,), (0,)), ((), ()))


def _cparams(sem):
    return pltpu.CompilerParams(dimension_semantics=sem, vmem_limit_bytes=VMEM_LIMIT)


def _silu(x):
    return x * jax.nn.sigmoid(x)


def _row_groups(n):
    step = min(n, SUB_ROWS)
    return [pl.ds(i, step) for i in range(0, n, step)]


def _staggered(gens, nstages):
    for t in range(nstages + len(gens) - 1):
        for g in reversed(range(len(gens))):
            if 0 <= t - g < nstages:
                next(gens[g])


def _ada_body(c_ref, w_ref, b_ref, o_ref):
    s = _silu(c_ref[...])
    rows = s.shape[0]
    s_hi = s.astype(BF16)
    s_lo = (s - s_hi.astype(F32)).astype(BF16)
    w = w_ref[...]
    w_hi = w.astype(BF16)
    w_lo = (w - w_hi.astype(F32)).astype(BF16)
    both = jnp.dot(jnp.concatenate([s_hi, s_lo], axis=0), w_hi, preferred_element_type=F32)
    o_ref[...] = (both[:rows] + both[rows:] + jnp.dot(s_hi, w_lo, preferred_element_type=F32)) + b_ref[...]


def _ada(cin, w, b):
    rows, d = cin.shape
    n = w.shape[1]
    nb = n // 4
    return pl.pallas_call(
        _ada_body,
        grid=(n // nb,),
        in_specs=[pl.BlockSpec((rows, d), lambda j: (0, 0)),
                  pl.BlockSpec((d, nb), lambda j: (0, j)),
                  pl.BlockSpec((1, nb), lambda j: (0, j))],
        out_specs=pl.BlockSpec((rows, nb), lambda j: (0, j)),
        out_shape=jax.ShapeDtypeStruct((rows, n), F32),
        compiler_params=_cparams(("arbitrary",)),
        name="ada",
    )(cin, w, b)


def _fold_body(wr_ref, up_ref, o_ref):
    o_ref[...] = jnp.dot(wr_ref[...], up_ref[...], precision=HIGHEST, preferred_element_type=F32)


def _fold(wr, up):
    return pl.pallas_call(
        _fold_body,
        out_shape=jax.ShapeDtypeStruct((wr.shape[0], up.shape[1]), F32),
        name="fold",
    )(wr, up)


def _modulated_norm(x, mult, shift):
    ms = jnp.mean(x * x, axis=-1, keepdims=True)
    return (x * lax.rsqrt(ms + EPS)) * mult + shift


def _log_sigmoid(z):
    return jnp.minimum(z, 0.0) - jnp.log1p(jnp.exp(-jnp.abs(z)))


def _inproj_body(x_ref, mult_ref, shift_ref, w_ref, bz_ref,
                 k_ref, v_ref, q_ref, laf_ref, lab_ref, sg_ref, p_ref, *, dk, dv, pw):
    def sub_tile(rows):
        h = _modulated_norm(x_ref[rows, :], mult_ref[0], shift_ref[0]).astype(BF16)
        yield
        u = jnp.dot(h, w_ref[...], preferred_element_type=F32)
        yield
        o = 0
        k_ref[rows, :] = u[:, o:o + dk].astype(BF16); o += dk
        v_ref[rows, :] = u[:, o:o + dv].astype(BF16); o += dv
        q_ref[rows, :] = u[:, o:o + dk].astype(BF16); o += dk
        zf = u[:, o:o + dk] + bz_ref[0:1, :]; o += dk
        zb = u[:, o:o + dk] + bz_ref[1:2, :]; o += dk
        laf_ref[rows, :] = _log_sigmoid(zf) * (1.0 / GATE_NORMALIZER)
        lab_ref[rows, :] = _log_sigmoid(zb) * (1.0 / GATE_NORMALIZER)
        sg_ref[rows, :] = _silu(u[:, o:o + dv]).astype(BF16); o += dv
        p_ref[rows, :] = u[:, o:o + pw].astype(BF16)
        yield

    _staggered([sub_tile(r) for r in _row_groups(x_ref.shape[0])], 3)


def _inproj(x2, mult, shift, w, bz, rows_per_sample, dk, dv, pw, tm):
    n, d = x2.shape
    tps = rows_per_sample // tm
    row = lambda i: (i, 0)
    vec = lambda i: (i // tps, 0, 0)
    outs = [(dk, BF16), (dv, BF16), (dk, BF16), (dk, F32), (dk, F32), (dv, BF16), (pw, BF16)]
    return pl.pallas_call(
        functools.partial(_inproj_body, dk=dk, dv=dv, pw=pw),
        grid=(n // tm,),
        in_specs=[pl.BlockSpec((tm, d), row),
                  pl.BlockSpec((1, 1, d), vec),
                  pl.BlockSpec((1, 1, d), vec),
                  pl.BlockSpec(w.shape, lambda i: (0, 0)),
                  pl.BlockSpec(bz.shape, lambda i: (0, 0))],
        out_specs=[pl.BlockSpec((tm, c), row) for c, _ in outs],
        out_shape=[jax.ShapeDtypeStruct((n, c), t) for c, t in outs],
        compiler_params=_cparams(("parallel",)),
        name="inproj",
    )(x2, mult, shift, w, bz)


def _gla_direction(k_ref, q_ref, v_ref, la_ref, o_ref, s_ref, reverse, nchunk):
    C = GLA_CHUNK
    hd = C
    cb = 4 * C
    lt = nchunk * C
    la = la_ref[0]
    r = lax.broadcasted_iota(I32, (cb, cb), 0)
    c = lax.broadcasted_iota(I32, (cb, cb), 1)
    same = (r // C) == (c // C)
    cum = jnp.where(same & ((c >= r) if reverse else (c <= r)), 1.0, 0.0).astype(BF16)
    la_hi = la.astype(BF16)
    la_lo = (la - la_hi.astype(F32)).astype(BF16)
    la2 = jnp.concatenate([la_hi, la_lo], axis=1)
    bcs = []
    for blk in range(lt // cb):
        part = jnp.dot(cum, la2[blk * cb:(blk + 1) * cb], preferred_element_type=F32)
        bcs.append(part[:, :LANES] + part[:, LANES:])
    ri = lax.broadcasted_iota(I32, (2 * C, C), 0) % C
    ci = lax.broadcasted_iota(I32, (2 * C, C), 1)
    tri = (ci >= ri) if reverse else (ci <= ri)
    lane = lax.broadcasted_iota(I32, (1, LANES), 1)
    m0 = (lane < hd).astype(F32)
    m1 = (lane >= hd).astype(F32)
    sr = lax.broadcasted_iota(I32, (2 * LANES, LANES), 0)
    sl = lax.broadcasted_iota(I32, (2 * LANES, LANES), 1)
    smask = ((sr < LANES) == (sl < hd)).astype(F32)
    kt = k_ref[0].astype(F32)
    qt = q_ref[0].astype(F32)
    zero_v = jnp.zeros((C, LANES), BF16)
    order = list(range(nchunk - 1, -1, -1) if reverse else range(nchunk))
    intra, qds, kvs, decs, q2s, kss, kws, scs = {}, {}, {}, {}, {}, {}, {}, {}
    for ch in order:
        lo = ch * C
        b = bcs[lo // cb][lo % cb:lo % cb + C]
        last = b[0:1] if reverse else b[C - 1:C]
        mid = b[C // 2:C // 2 + 1] if reverse else b[C // 2 - 1:C // 2]
        kc = kt[lo:lo + C]
        qc = qt[lo:lo + C]
        qs = qc * jnp.exp(b - mid)
        kss[ch] = (kc * jnp.exp(mid - b)).astype(BF16)
        qds[ch] = (qc * jnp.exp(b)).astype(BF16)
        kws[ch] = (kc * jnp.exp(last - b)).astype(BF16)
        decs[ch] = jnp.exp(last)
        q2s[ch] = jnp.concatenate([qs * m0, qs * m1], axis=0).astype(BF16)
    yield
    for ch in order:
        scs[ch] = lax.dot_general(q2s[ch], kss[ch], _NT, preferred_element_type=F32)
    yield
    for ch in order:
        v2 = v_ref[0, ch * C:(ch + 1) * C, :]
        kvs[ch] = lax.dot_general(v2, kws[ch], _TN, preferred_element_type=F32) * smask
    yield
    for ch in order:
        sc = jnp.where(tri, scs[ch], 0.0).astype(BF16)
        sc2 = jnp.concatenate([sc[:C], sc[C:]], axis=1)
        v2 = v_ref[0, ch * C:(ch + 1) * C, :]
        vbd = jnp.concatenate([jnp.concatenate([v2[:, :LANES], zero_v], axis=1),
                               jnp.concatenate([zero_v, v2[:, LANES:]], axis=1)], axis=0)
        intra[ch] = jnp.dot(sc2, vbd, preferred_element_type=F32)
    yield
    st = s_ref[...]
    starts = {}
    for ch in order:
        starts[ch] = st.astype(BF16)
        st = st * decs[ch] + kvs[ch]
    s_ref[...] = st
    yield
    for ch in order:
        inter = lax.dot_general(qds[ch], starts[ch], _NT, preferred_element_type=F32)
        o_ref[0, ch * C:(ch + 1) * C, :] = inter + intra[ch]
    yield


def _gla_body(kf, qf, vf, laf, kb, qb, vb, lab, h0f, h0b, of, ob, hf_out, hb_out, sf, sb, *, nchunk):
    i = pl.program_id(2)

    @pl.when(i == 0)
    def _():
        sf[...] = h0f[0, 0]
        sb[...] = h0b[0, 0]

    sweeps = [_gla_direction(kf, qf, vf, laf, of, sf, False, nchunk),
              _gla_direction(kb, qb, vb, lab, ob, sb, True, nchunk)]
    for _ in range(GLA_STAGES):
        for sweep in sweeps:
            next(sweep)

    @pl.when(i == pl.num_programs(2) - 1)
    def _():
        hf_out[0, 0] = sf[...]
        hb_out[0, 0] = sb[...]


def _gla(k, q, v, laf, lab, h0f, h0b, lt):
    bsz, l, _ = k.shape
    pairs = h0f.shape[1]
    nt = l // lt
    fwd = lambda b, hp, i: (b, i, hp)
    bwd = lambda b, hp, i: (b, nt - 1 - i, hp)
    st = lambda b, hp, i: (b, hp, 0, 0)
    kq = lambda m: pl.BlockSpec((1, lt, LANES), m)
    vv = lambda m: pl.BlockSpec((1, lt, 2 * LANES), m)
    sspec = pl.BlockSpec((1, 1, 2 * LANES, LANES), st)
    return pl.pallas_call(
        functools.partial(_gla_body, nchunk=lt // GLA_CHUNK),
        grid=(bsz, pairs, nt),
        in_specs=[kq(fwd), kq(fwd), vv(fwd), kq(fwd), kq(bwd), kq(bwd), vv(bwd), kq(bwd), sspec, sspec],
        out_specs=[vv(fwd), vv(bwd), sspec, sspec],
        out_shape=[jax.ShapeDtypeStruct(v.shape, F32), jax.ShapeDtypeStruct(v.shape, F32),
                   jax.ShapeDtypeStruct(h0f.shape, F32), jax.ShapeDtypeStruct(h0b.shape, F32)],
        scratch_shapes=[pltpu.VMEM((2 * LANES, LANES), F32), pltpu.VMEM((2 * LANES, LANES), F32)],
        compiler_params=_cparams(("parallel", "parallel", "arbitrary")),
        name="gla",
    )(k, q, v, laf, k, q, v, lab, h0f, h0b)


def _pool_body(p_ref, w_ref, sc_ref, o_ref, *, half, rows):
    x = p_ref[0].astype(F32)
    l = x.shape[0]
    t = lax.broadcasted_iota(I32, (l, 1), 0)
    col = t % GRID_W
    row = t // GRID_W

    def box(a, pos, n, unit):
        fwd = a
        bwd = jnp.where(pos >= 1, pltpu.roll(a, unit, axis=0), 0.0)
        s = 1
        while s < half:
            fwd = fwd + jnp.where(pos + s <= n - 1, pltpu.roll(fwd, l - s * unit, axis=0), 0.0)
            bwd = bwd + jnp.where(pos >= s, pltpu.roll(bwd, s * unit, axis=0), 0.0)
            s *= 2
        cnt = jnp.minimum(pos + half, n) - jnp.maximum(pos - half, 0)
        return fwd + bwd, cnt.astype(F32)

    s1, cnt_c = box(x, col, GRID_W, 1)
    s2, cnt_r = box(s1, row, rows, GRID_W)
    pooled = s2 / (cnt_r * cnt_c) - x
    mixed = jnp.dot(pooled.astype(BF16), w_ref[0].astype(BF16), preferred_element_type=F32)
    o_ref[0] = (mixed * sc_ref[0]).astype(BF16)


def _pool(pin, pool_w, pool_scale3, g):
    bsz, l, pw = pin.shape
    ch = pool_w.shape[-1]
    half = POOL_WINDOWS[g] // 2
    return pl.pallas_call(
        functools.partial(_pool_body, half=half, rows=l // GRID_W),
        grid=(bsz,),
        in_specs=[pl.BlockSpec((1, l, ch), lambda b: (b, 0, g)),
                  pl.BlockSpec((1, ch, ch), lambda b: (g, 0, 0)),
                  pl.BlockSpec((1, 1, ch), lambda b: (g, 0, 0))],
        out_specs=pl.BlockSpec((1, l, ch), lambda b: (b, 0, 0)),
        out_shape=jax.ShapeDtypeStruct((bsz, l, ch), BF16),
        compiler_params=_cparams(("parallel",)),
        name=f"pool{g}",
    )(pin, pool_w, pool_scale3)


def _merge_body(x_ref, m1_ref, s1_ref, g1_ref, m2_ref, s2_ref,
                of_ref, ob_ref, sg_ref, mx0, mx1, mx2, mx3,
                wm_ref, wgla_ref, wpool_ref, wout_ref, gn_ref, wr_ref,
                x1_ref, h2_ref, aff_ref, *, heads, ne):
    d = x_ref.shape[1]

    def sub_tile(rows):
        x = x_ref[rows, :]
        h = _modulated_norm(x, m1_ref[0], s1_ref[0]).astype(BF16)
        o = of_ref[rows, :] + ob_ref[rows, :]
        sg = sg_ref[rows, :].astype(F32)
        og = []
        for j in range(heads):
            oj = o[:, j * LANES:(j + 1) * LANES]
            oj = oj * lax.rsqrt(jnp.mean(oj * oj, axis=-1, keepdims=True) + EPS) * gn_ref[...]
            og.append((oj * sg[:, j * LANES:(j + 1) * LANES]).astype(BF16))
        og = jnp.concatenate(og, axis=1)
        mixed = jnp.concatenate([mx0[0, rows, :], mx1[0, rows, :], mx2[0, rows, :], mx3[0, rows, :]], axis=1)
        yield
        gates = jnp.dot(h, wm_ref[...], preferred_element_type=F32)
        bg = jnp.dot(og, wgla_ref[...], preferred_element_type=F32)
        bp = jnp.dot(mixed, wpool_ref[...], preferred_element_type=F32)
        yield
        gates = jax.nn.sigmoid(gates)
        z = (gates[:, :d] * bg + gates[:, d:] * bp).astype(BF16)
        yield
        y = jnp.dot(z, wout_ref[...], preferred_element_type=F32)
        yield
        x1 = x + g1_ref[0] * y
        x1_ref[rows, :] = x1
        h2 = _modulated_norm(x1, m2_ref[0], s2_ref[0])
        hi = h2.astype(BF16)
        h2_ref[rows, :] = hi
        lo = (h2 - hi.astype(F32)).astype(BF16)
        yield
        lg = (jnp.dot(hi, wr_ref[...], preferred_element_type=F32)
              + jnp.dot(lo, wr_ref[...], preferred_element_type=F32))
        yield
        lgt = lg.T
        logit = lgt[0:ne] + lgt[ne:2 * ne]
        mx = jnp.max(logit, axis=0, keepdims=True)
        ex = jnp.exp(logit - mx)
        aff_ref[0, :, rows] = ex / jnp.sum(ex, axis=0, keepdims=True)
        yield

    _staggered([sub_tile(r) for r in _row_groups(x_ref.shape[0])], 7)


def _merge(x2, vecs, of, ob, sg, mixed, wm, wgla, wpool, wout, gn, wr, rows_per_sample, heads, ne, tm):
    n, d = x2.shape
    bsz = n // rows_per_sample
    tps = rows_per_sample // tm
    row = lambda i: (i, 0)
    vec = lambda i: (i // tps, 0, 0)
    full = lambda a: pl.BlockSpec(a.shape, lambda i: (0,) * a.ndim)
    ch = mixed[0].shape[-1]
    mspec = pl.BlockSpec((1, tm, ch), lambda i: (i // tps, i % tps, 0))
    gv = of.shape[-1]
    return pl.pallas_call(
        functools.partial(_merge_body, heads=heads, ne=ne),
        grid=(n // tm,),
        in_specs=[pl.BlockSpec((tm, d), row)] + [pl.BlockSpec((1, 1, d), vec)] * 5
                 + [pl.BlockSpec((tm, gv), row)] * 3 + [mspec] * 4
                 + [full(wm), full(wgla), full(wpool), full(wout), full(gn), full(wr)],
        out_specs=[pl.BlockSpec((tm, d), row),
                   pl.BlockSpec((tm, d), row),
                   pl.BlockSpec((1, ne, tm), lambda i: (i // tps, 0, i % tps))],
        out_shape=[jax.ShapeDtypeStruct((n, d), F32),
                   jax.ShapeDtypeStruct((n, d), BF16),
                   jax.ShapeDtypeStruct((bsz, ne, rows_per_sample), F32)],
        compiler_params=_cparams(("parallel",)),
        name="merge",
    )(x2, *vecs, of, ob, sg, *mixed, wm, wgla, wpool, wout, gn, wr)


def _route_body(aff_ref, pos_ref, off_ref, cnt_ref, *, cap, ntb):
    a = aff_ref[0]
    ne, l = a.shape
    blk = ROUTE_BLK

    def bisect(i, v):
        cand = v | jnp.left_shift(jnp.int32(1), 30 - i)
        cnt = jnp.sum((a >= lax.bitcast_convert_type(cand, F32)).astype(F32), axis=1, keepdims=True)
        return jnp.where(cnt >= cap, cand, v)

    thr = lax.bitcast_convert_type(lax.fori_loop(0, 31, bisect, jnp.zeros((ne, 1), I32)), F32)
    gt = a > thr
    tie = a == thr
    need = cap - jnp.sum(gt.astype(F32), axis=1, keepdims=True)

    r = lax.broadcasted_iota(I32, (blk, blk), 0)
    c = lax.broadcasted_iota(I32, (blk, blk), 1)
    upper = (r <= c).astype(BF16)
    lane = lax.broadcasted_iota(I32, (1, LANES), 1)

    def prefix(mask_f):
        run = jnp.zeros((ne, 1), F32)
        offs = jnp.zeros((ne, LANES), F32)
        for tb in range(ntb):
            m = mask_f[:, tb * blk:(tb + 1) * blk].astype(BF16)
            loc = jnp.dot(m, upper, preferred_element_type=F32)
            cnt_ref[:, tb * blk:(tb + 1) * blk] = loc + run
            offs = jnp.where(lane == tb, run, offs)
            run = run + loc[:, blk - 1:blk]
        return jnp.where(lane >= ntb, run, offs)

    tie_f = tie.astype(F32)
    prefix(tie_f)
    tie_excl = cnt_ref[...] - tie_f
    sel = gt | (tie & (tie_excl < need))
    offs = prefix(sel.astype(F32))
    pos_ref[0] = jnp.where(sel, cnt_ref[...] - 1.0, -1.0).astype(I32)
    off_ref[0] = offs.astype(I32)


def _route(aff, cap):
    bsz, ne, l = aff.shape
    spec = lambda s: pl.BlockSpec((1,) + s, lambda b: (b, 0, 0))
    return pl.pallas_call(
        functools.partial(_route_body, cap=cap, ntb=l // ROUTE_BLK),
        grid=(bsz,),
        in_specs=[spec((ne, l))],
        out_specs=[spec((ne, l)), spec((ne, LANES))],
        out_shape=[jax.ShapeDtypeStruct((bsz, ne, l), I32),
                   jax.ShapeDtypeStruct((bsz, ne, LANES), I32)],
        scratch_shapes=[pltpu.VMEM((ne, l), F32)],
        compiler_params=_cparams(("parallel",)),
        name="route",
    )(aff)


def _window_plan(off_ref, bb, tt, experts, ne):
    lows = [(off_ref[(bb * ne + e) * LANES + tt] // 16) * 16 for e in experts]
    ends = [off_ref[(bb * ne + e) * LANES + tt + 1] for e in experts]
    return lows, ends


def _gatherx_body(off_ref, h_ref, pos_ref, xs_ref, *, ne, cap, eg):
    b = pl.program_id(0)
    g = pl.program_id(1)
    tb = pl.program_id(2)
    win = SLOT_WIN

    @pl.when(tb == 0)
    def _():
        xs_ref[...] = jnp.zeros(xs_ref.shape, BF16)

    j_col = lax.broadcasted_iota(I32, (win, 1), 0)
    t = ROUTE_BLK
    for sub in range(h_ref.shape[0] // t):
        lows, ends = _window_plan(off_ref, b, tb * (h_ref.shape[0] // t) + sub, [g * eg + k for k in range(eg)], ne)
        rounds = jnp.int32(0)
        for k in range(eg):
            rounds = jnp.maximum(rounds, (ends[k] - lows[k] + win - 1) // win)

        def one_round(r, carry, lows=lows, sub=sub):
            starts = [pl.multiple_of(jnp.minimum(lows[k] + r * win, cap - win), 16) for k in range(eg)]
            pieces = []
            for k in range(eg):
                p = pos_ref[0, k:k + 1, sub * t:(sub + 1) * t]
                hit = (p - starts[k] == j_col) & (p >= lows[k] + r * win)
                pieces.append(jnp.where(hit, 1.0, 0.0).astype(BF16))
            sel = jnp.concatenate(pieces, axis=0)
            rows = jnp.dot(sel, h_ref[sub * t:(sub + 1) * t, :],
                           preferred_element_type=F32).astype(BF16)
            for k in range(eg):
                dst = (0, k, pl.ds(starts[k], win), slice(None))
                xs_ref[dst] = xs_ref[dst] + rows[k * win:(k + 1) * win]
            return carry

        lax.fori_loop(0, rounds, one_round, 0)


def _gatherx(off_flat, h2, pos, cap):
    n, d = h2.shape
    bsz, ne, l = pos.shape
    t = min(l, 4 * ROUTE_BLK)
    ntb = l // t
    eg = 8
    grid_spec = pltpu.PrefetchScalarGridSpec(
        num_scalar_prefetch=1,
        grid=(bsz, ne // eg, ntb),
        in_specs=[pl.BlockSpec((t, d), lambda b, g, i, off: (b * ntb + i, 0)),
                  pl.BlockSpec((1, eg, t), lambda b, g, i, off: (b, g, i))],
        out_specs=pl.BlockSpec((1, eg, cap, d), lambda b, g, i, off: (b, g, 0, 0)),
    )
    return pl.pallas_call(
        functools.partial(_gatherx_body, ne=ne, cap=cap, eg=eg),
        grid_spec=grid_spec,
        out_shape=jax.ShapeDtypeStruct((bsz, ne, cap, d), BF16),
        compiler_params=_cparams(("parallel", "parallel", "arbitrary")),
        name="gatherx",
    )(off_flat, h2, pos)


def _moe_body(xs_ref, wg_ref, wu_ref, wd_ref, y_ref, *, rc):
    wg = wg_ref[0].astype(BF16)
    wu = wu_ref[0].astype(BF16)
    wd = wd_ref[0].astype(BF16)
    cap = xs_ref.shape[2]
    acts = []
    for ch in range(cap // rc):
        xs = xs_ref[0, 0, ch * rc:(ch + 1) * rc, :]
        gate = jnp.dot(xs, wg, preferred_element_type=F32)
        up = jnp.dot(xs, wu, preferred_element_type=F32)
        acts.append((_silu(gate) * up).astype(BF16))
    for ch in range(cap // rc):
        y_ref[0, 0, ch * rc:(ch + 1) * rc, :] = jnp.dot(acts[ch], wd, preferred_element_type=F32).astype(BF16)


def _moe(xs, wg, wu, wd):
    bsz, ne, cap, d = xs.shape
    de = wg.shape[2]
    slot = pl.BlockSpec((1, 1, cap, d), lambda b, e: (b, e, 0, 0))
    return pl.pallas_call(
        functools.partial(_moe_body, rc=min(cap, 512)),
        grid=(bsz, ne),
        in_specs=[slot,
                  pl.BlockSpec((1, d, de), lambda b, e: (e, 0, 0)),
                  pl.BlockSpec((1, d, de), lambda b, e: (e, 0, 0)),
                  pl.BlockSpec((1, de, d), lambda b, e: (e, 0, 0))],
        out_specs=slot,
        out_shape=jax.ShapeDtypeStruct((bsz, ne, cap, d), BF16),
        compiler_params=_cparams(("parallel", "arbitrary")),
        name="moe",
    )(xs, wg, wu, wd)


def _combine_body(off_ref, x1_ref, g2_ref, fg_ref, pos_ref, aff_ref, y_hbm, o_ref, ybuf, sem,
                  *, ne, cap, ntb):
    b = pl.program_id(0)
    tb = pl.program_id(1)
    step = b * ntb + tb
    nsteps = pl.num_programs(0) * ntb
    slot = step % 2
    spare = 2
    win = SLOT_WIN

    def block_lows(bb, tt):
        return _window_plan(off_ref, bb, tt, range(ne), ne)[0]

    def window_copy(bb, e, start, buf):
        return pltpu.make_async_copy(y_hbm.at[bb, e, pl.ds(start, win), :],
                                     ybuf.at[buf, pl.ds(e * win, win), :], sem.at[buf, e])

    def window_starts(lows, r):
        return [pl.multiple_of(jnp.minimum(lows[e] + r * win, cap - win), 16) for e in range(ne)]

    def start_first_round(bb, tt, buf):
        starts = window_starts(block_lows(bb, tt), 0)
        for e in range(ne):
            window_copy(bb, e, starts[e], buf).start()

    @pl.when(step == 0)
    def _():
        start_first_round(b, tb, 0)

    @pl.when(step + 1 < nsteps)
    def _():
        nxt = step + 1
        start_first_round(nxt // ntb, nxt % ntb, 1 - slot)

    lows, ends = _window_plan(off_ref, b, tb, range(ne), ne)
    rounds = jnp.int32(0)
    for e in range(ne):
        rounds = jnp.maximum(rounds, (ends[e] - lows[e] + win - 1) // win)
    j_col = lax.broadcasted_iota(I32, (win, 1), 0)

    def expand(r, starts, buf):
        pieces = []
        for e in range(ne):
            p = pos_ref[0, e:e + 1, :]
            valid = p >= lows[e] + r * win
            hit = (p - starts[e] == j_col) & valid
            pieces.append(jnp.where(hit, aff_ref[0, e:e + 1, :], 0.0).astype(BF16))
        pmat = jnp.concatenate(pieces, axis=0)
        return lax.dot_general(pmat, ybuf[buf], _TN, preferred_element_type=F32)

    starts0 = window_starts(lows, 0)
    for e in range(ne):
        window_copy(b, e, starts0[e], slot).wait()
    moe = expand(0, starts0, slot)

    def extra_round(r, acc):
        starts = window_starts(lows, r)
        for e in range(ne):
            window_copy(b, e, starts[e], spare).start()
        for e in range(ne):
            window_copy(b, e, starts[e], spare).wait()
        return acc + expand(r, starts, spare)

    moe = lax.fori_loop(1, rounds, extra_round, moe)
    x2 = x1_ref[...] + g2_ref[0] * moe
    ms = jnp.mean(x2 * x2, axis=-1, keepdims=True)
    o_ref[...] = x2 * lax.rsqrt(ms + EPS) * fg_ref[...]


def _combine(off_flat, x1, g2, fg, pos, aff, y, rows_per_sample):
    n, d = x1.shape
    bsz, ne, cap, _ = y.shape
    t = ROUTE_BLK
    ntb = rows_per_sample // t
    grid_spec = pltpu.PrefetchScalarGridSpec(
        num_scalar_prefetch=1,
        grid=(bsz, ntb),
        in_specs=[pl.BlockSpec((t, d), lambda b, i, off: (b * ntb + i, 0)),
                  pl.BlockSpec((1, 1, d), lambda b, i, off: (b, 0, 0)),
                  pl.BlockSpec((1, d), lambda b, i, off: (0, 0)),
                  pl.BlockSpec((1, ne, t), lambda b, i, off: (b, 0, i)),
                  pl.BlockSpec((1, ne, t), lambda b, i, off: (b, 0, i)),
                  pl.BlockSpec(memory_space=pl.ANY)],
        out_specs=pl.BlockSpec((t, d), lambda b, i, off: (b * ntb + i, 0)),
        scratch_shapes=[pltpu.VMEM((3, ne * SLOT_WIN, d), BF16), pltpu.SemaphoreType.DMA((3, ne))],
    )
    return pl.pallas_call(
        functools.partial(_combine_body, ne=ne, cap=cap, ntb=ntb),
        grid_spec=grid_spec,
        out_shape=jax.ShapeDtypeStruct((n, d), F32),
        compiler_params=_cparams(("arbitrary", "arbitrary")),
        name="combine",
    )(off_flat, x1, g2, fg, pos, aff, y)


def kernel(x, c, ctx, c_ctx, ada_w, ada_b, norm1_g, norm2_g, w_in, w_decay_up, b_decay, gla_norm_g,
           w_gla_proj, pool_w, pool_scale, w_pool_proj, w_out, w_router, w_gate_e, w_up_e, w_down_e,
           final_norm_g):
    assert ada_w.shape[0] == 1, "single-layer block"
    bsz, l, d = x.shape
    lc = ctx.shape[1]
    rank, dk = w_decay_up.shape[2], w_decay_up.shape[3]
    dvh = gla_norm_g.shape[1]
    dv = w_gla_proj.shape[1]
    heads = dv // dvh
    groups, ch = pool_w.shape[1], pool_w.shape[2]
    pw = groups * ch
    ne = w_router.shape[2]
    cap = EC_CAPACITY * l // ne
    assert dk // heads == 64 and dvh == LANES and ch == LANES and heads % 2 == 0
    assert l % ROUTE_BLK == 0 and l // ROUTE_BLK < LANES and cap >= SLOT_WIN and cap % 16 == 0

    cin = jnp.zeros((8, d), F32).at[:bsz].set(c).at[bsz].set(c_ctx)
    mods = _ada(cin, ada_w[0], ada_b[0][None, :])
    sh1, sc1, gt1, sh2, sc2, gt2 = [mods[:, i * d:(i + 1) * d] for i in range(6)]
    vec3 = lambda a: a[:, None, :]
    mult1 = norm1_g[0][None, :] * (1.0 + sc1)
    mult2 = norm2_g[0][None, :] * (1.0 + sc2)

    wi = w_in[0]
    o_r = dk + dv
    o_q = o_r + 2 * rank
    o_g = o_q + dk
    o_p = o_g + dv
    o_m = o_p + pw
    up = jnp.zeros((2 * rank, 2 * dk), F32)
    up = up.at[:rank, :dk].set(w_decay_up[0, 0]).at[rank:, dk:].set(w_decay_up[0, 1])
    wz = _fold(wi[:, o_r:o_q], up)
    w1 = jnp.concatenate([wi[:, :o_r], wi[:, o_q:o_g] * (float(dk // heads) ** -0.5), wz, wi[:, o_g:o_m]],
                         axis=1).astype(BF16)
    wm = wi[:, o_m:].astype(BF16)
    bz = b_decay[0]

    zero_state = jnp.zeros((bsz, heads // 2, 2 * LANES, LANES), F32)
    cm = jnp.broadcast_to(vec3(mult1[bsz:bsz + 1]), (bsz, 1, d))
    cs = jnp.broadcast_to(vec3(sh1[bsz:bsz + 1]), (bsz, 1, d))
    ck, cv, _, claf, clab, _, _ = _inproj(ctx.reshape(bsz * lc, d), cm, cs, w1, bz, lc, dk, dv, pw, min(lc, 256))
    r3 = lambda a, n: a.reshape(bsz, n, a.shape[-1])
    _, _, h_f, h_b = _gla(r3(ck, lc), r3(ck, lc), r3(cv, lc), r3(claf, lc), r3(clab, lc),
                          zero_state, zero_state, min(lc, 256))

    x2 = x.reshape(bsz * l, d)
    tm = min(l, 512)
    k, v, q, laf, lab, sg, pin = _inproj(x2, vec3(mult1[:bsz]), vec3(sh1[:bsz]), w1, bz, l, dk, dv, pw, tm)
    of, ob, _, _ = _gla(r3(k, l), r3(q, l), r3(v, l), r3(laf, l), r3(lab, l), h_f, h_b, min(l, 512))
    pin3 = r3(pin, l)
    psc = pool_scale[0].reshape(groups, 1, ch)
    mixed = [_pool(pin3, pool_w[0], psc, g) for g in range(groups)]

    wr_hi = w_router[0].astype(BF16)
    wr_lo = (w_router[0] - wr_hi.astype(F32)).astype(BF16)
    wr = jnp.zeros((d, LANES), BF16).at[:, :ne].set(wr_hi).at[:, ne:2 * ne].set(wr_lo)
    vecs = [vec3(mult1[:bsz]), vec3(sh1[:bsz]), vec3(gt1[:bsz]), vec3(mult2[:bsz]), vec3(sh2[:bsz])]
    x1, h2, aff = _merge(x2, vecs, of.reshape(bsz * l, dv), ob.reshape(bsz * l, dv), sg, mixed,
                          wm, w_gla_proj[0].astype(BF16), w_pool_proj[0].astype(BF16), w_out[0].astype(BF16),
                          gla_norm_g[0][None, :], wr, l, heads, ne, tm)

    pos, offs = _route(aff, cap)
    off_flat = offs.reshape(-1)
    xs = _gatherx(off_flat, h2, pos, cap)
    y = _moe(xs, w_gate_e[0], w_up_e[0], w_down_e[0])
    out = _combine(off_flat, x1, vec3(gt2[:bsz]), final_norm_g[None, :], pos, aff, y, l)
    return out.reshape(bsz, l, d)
```

```python
import functools

import jax
import jax.numpy as jnp
from jax import lax
from jax.experimental import pallas as pl
from jax.experimental.pallas import tpu as pltpu

F32 = jnp.float32
BF16 = jnp.bfloat16
I32 = jnp.int32
HIGHEST = lax.Precision.HIGHEST

EPS = 1e-6
GRID_W = 64
GLA_CHUNK = 64
GLA_STAGES = 6
GATE_NORMALIZER = 16.0
POOL_WINDOWS = (2, 4, 8, 16)
EC_CAPACITY = 2
LANES = 128
ROUTE_BLK = 256
SLOT_WIN = 64
SUB_ROWS = 256
VMEM_LIMIT = 56 * 1024 * 1024

_NT = (((1,), (1,)), ((), ()))
_TN = (((0,), (0,)), ((), ()))


def _cparams(sem):
    return pltpu.CompilerParams(dimension_semantics=sem, vmem_limit_bytes=VMEM_LIMIT)


def _silu(x):
    return x * jax.nn.sigmoid(x)


def _row_groups(n):
    step = min(n, SUB_ROWS)
    return [pl.ds(i, step) for i in range(0, n, step)]


def _staggered(gens, nstages):
    for t in range(nstages + len(gens) - 1):
        for g in reversed(range(len(gens))):
            if 0 <= t - g < nstages:
                next(gens[g])


def _ada_body(c_ref, w_ref, b_ref, o_ref):
    s = _silu(c_ref[...])
    rows = s.shape[0]
    s_hi = s.astype(BF16)
    s_lo = (s - s_hi.astype(F32)).astype(BF16)
    w = w_ref[...]
    w_hi = w.astype(BF16)
    w_lo = (w - w_hi.astype(F32)).astype(BF16)
    both = jnp.dot(jnp.concatenate([s_hi, s_lo], axis=0), w_hi, preferred_element_type=F32)
    o_ref[...] = (both[:rows] + both[rows:] + jnp.dot(s_hi, w_lo, preferred_element_type=F32)) + b_ref[...]


def _ada(cin, w, b):
    rows, d = cin.shape
    n = w.shape[1]
    nb = n // 4
    return pl.pallas_call(
        _ada_body,
        grid=(n // nb,),
        in_specs=[pl.BlockSpec((rows, d), lambda j: (0, 0)),
                  pl.BlockSpec((d, nb), lambda j: (0, j)),
                  pl.BlockSpec((1, nb), lambda j: (0, j))],
        out_specs=pl.BlockSpec((rows, nb), lambda j: (0, j)),
        out_shape=jax.ShapeDtypeStruct((rows, n), F32),
        compiler_params=_cparams(("arbitrary",)),
        name="ada",
    )(cin, w, b)


def _prep_body(w_ref, up_ref, w1_ref, wm_ref, *, o_r, o_q, o_g, o_m, qscale):
    w = w_ref[0]
    dk = o_g - o_q
    wz = jnp.dot(w[:, o_r:o_q], up_ref[...], precision=HIGHEST, preferred_element_type=F32)
    nz = wz.shape[1]
    w1_ref[:, :o_r] = w[:, :o_r].astype(BF16)
    w1_ref[:, o_r:o_r + dk] = (w[:, o_q:o_g] * qscale).astype(BF16)
    w1_ref[:, o_r + dk:o_r + dk + nz] = wz.astype(BF16)
    w1_ref[:, o_r + dk + nz:] = w[:, o_g:o_m].astype(BF16)
    wm_ref[...] = w[:, o_m:].astype(BF16)


def _prep(w_in, up, o_r, o_q, o_g, o_m, qscale):
    _, d, n = w_in.shape
    n1 = o_r + (o_g - o_q) + up.shape[1] + (o_m - o_g)
    tr = d // 4
    return pl.pallas_call(
        functools.partial(_prep_body, o_r=o_r, o_q=o_q, o_g=o_g, o_m=o_m, qscale=qscale),
        grid=(d // tr,),
        in_specs=[pl.BlockSpec((1, tr, n), lambda i: (0, i, 0)),
                  pl.BlockSpec(up.shape, lambda i: (0, 0))],
        out_specs=[pl.BlockSpec((tr, n1), lambda i: (i, 0)),
                   pl.BlockSpec((tr, n - o_m), lambda i: (i, 0))],
        out_shape=[jax.ShapeDtypeStruct((d, n1), BF16), jax.ShapeDtypeStruct((d, n - o_m), BF16)],
        compiler_params=_cparams(("parallel",)),
        name="prep",
    )(w_in, up)


def _modulated_norm(x, mult, shift):
    ms = jnp.mean(x * x, axis=-1, keepdims=True)
    return (x * lax.rsqrt(ms + EPS)) * mult + shift


def _log_sigmoid(z):
    return jnp.minimum(z, 0.0) - jnp.log1p(jnp.exp(-jnp.abs(z)))


def _inproj_body(x_ref, mult_ref, shift_ref, w_ref, bz_ref,
                 k_ref, v_ref, q_ref, laf_ref, lab_ref, sg_ref, p_ref, *, dk, dv, pw):
    def sub_tile(rows):
        h = _modulated_norm(x_ref[rows, :], mult_ref[0], shift_ref[0]).astype(BF16)
        yield
        u = jnp.dot(h, w_ref[...], preferred_element_type=F32)
        yield
        o = 0
        k_ref[rows, :] = u[:, o:o + dk].astype(BF16); o += dk
        v_ref[rows, :] = u[:, o:o + dv].astype(BF16); o += dv
        q_ref[rows, :] = u[:, o:o + dk].astype(BF16); o += dk
        zf = u[:, o:o + dk] + bz_ref[0:1, :]; o += dk
        zb = u[:, o:o + dk] + bz_ref[1:2, :]; o += dk
        laf_ref[rows, :] = _log_sigmoid(zf) * (1.0 / GATE_NORMALIZER)
        lab_ref[rows, :] = _log_sigmoid(zb) * (1.0 / GATE_NORMALIZER)
        sg_ref[rows, :] = _silu(u[:, o:o + dv]).astype(BF16); o += dv
        p_ref[rows, :] = u[:, o:o + pw].astype(BF16)
        yield

    _staggered([sub_tile(r) for r in _row_groups(x_ref.shape[0])], 3)


def _inproj(x2, mult, shift, w, bz, rows_per_sample, dk, dv, pw, tm):
    n, d = x2.shape
    tps = rows_per_sample // tm
    row = lambda i: (i, 0)
    vec = lambda i: (i // tps, 0, 0)
    outs = [(dk, BF16), (dv, BF16), (dk, BF16), (dk, F32), (dk, F32), (dv, BF16), (pw, BF16)]
    return pl.pallas_call(
        functools.partial(_inproj_body, dk=dk, dv=dv, pw=pw),
        grid=(n // tm,),
        in_specs=[pl.BlockSpec((tm, d), row),
                  pl.BlockSpec((1, 1, d), vec),
                  pl.BlockSpec((1, 1, d), vec),
                  pl.BlockSpec(w.shape, lambda i: (0, 0)),
                  pl.BlockSpec(bz.shape, lambda i: (0, 0))],
        out_specs=[pl.BlockSpec((tm, c), row) for c, _ in outs],
        out_shape=[jax.ShapeDtypeStruct((n, c), t) for c, t in outs],
        compiler_params=_cparams(("parallel",)),
        name="inproj",
    )(x2, mult, shift, w, bz)


def _gla_direction(k_ref, q_ref, v_ref, la_ref, o_ref, s_ref, reverse, nchunk):
    C = GLA_CHUNK
    hd = C
    cb = 4 * C
    lt = nchunk * C
    la = la_ref[0]
    r = lax.broadcasted_iota(I32, (cb, cb), 0)
    c = lax.broadcasted_iota(I32, (cb, cb), 1)
    same = (r // C) == (c // C)
    cum = jnp.where(same & ((c >= r) if reverse else (c <= r)), 1.0, 0.0).astype(BF16)
    la_hi = la.astype(BF16)
    la_lo = (la - la_hi.astype(F32)).astype(BF16)
    la2 = jnp.concatenate([la_hi, la_lo], axis=1)
    bcs = []
    for blk in range(lt // cb):
        part = jnp.dot(cum, la2[blk * cb:(blk + 1) * cb], preferred_element_type=F32)
        bcs.append(part[:, :LANES] + part[:, LANES:])
    ri = lax.broadcasted_iota(I32, (2 * C, C), 0) % C
    ci = lax.broadcasted_iota(I32, (2 * C, C), 1)
    tri = (ci >= ri) if reverse else (ci <= ri)
    lane = lax.broadcasted_iota(I32, (1, LANES), 1)
    m0 = (lane < hd).astype(F32)
    m1 = (lane >= hd).astype(F32)
    sr = lax.broadcasted_iota(I32, (2 * LANES, LANES), 0)
    sl = lax.broadcasted_iota(I32, (2 * LANES, LANES), 1)
    smask = ((sr < LANES) == (sl < hd)).astype(F32)
    kt = k_ref[0].astype(F32)
    qt = q_ref[0].astype(F32)
    zero_v = jnp.zeros((C, LANES), BF16)
    order = list(range(nchunk - 1, -1, -1) if reverse else range(nchunk))
    intra, qds, kvs, decs, q2s, kss, kws, scs = {}, {}, {}, {}, {}, {}, {}, {}
    for ch in order:
        lo = ch * C
        b = bcs[lo // cb][lo % cb:lo % cb + C]
        last = b[0:1] if reverse else b[C - 1:C]
        mid = b[C // 2:C // 2 + 1] if reverse else b[C // 2 - 1:C // 2]
        kc = kt[lo:lo + C]
        qc = qt[lo:lo + C]
        qs = qc * jnp.exp(b - mid)
        kss[ch] = (kc * jnp.exp(mid - b)).astype(BF16)
        qds[ch] = (qc * jnp.exp(b)).astype(BF16)
        kws[ch] = (kc * jnp.exp(last - b)).astype(BF16)
        decs[ch] = jnp.exp(last)
        q2s[ch] = jnp.concatenate([qs * m0, qs * m1], axis=0).astype(BF16)
    yield
    for ch in order:
        scs[ch] = lax.dot_general(q2s[ch], kss[ch], _NT, preferred_element_type=F32)
    yield
    for ch in order:
        v2 = v_ref[0, ch * C:(ch + 1) * C, :]
        kvs[ch] = lax.dot_general(v2, kws[ch], _TN, preferred_element_type=F32) * smask
    yield
    for ch in order:
        sc = jnp.where(tri, scs[ch], 0.0).astype(BF16)
        sc2 = jnp.concatenate([sc[:C], sc[C:]], axis=1)
        v2 = v_ref[0, ch * C:(ch + 1) * C, :]
        vbd = jnp.concatenate([jnp.concatenate([v2[:, :LANES], zero_v], axis=1),
                               jnp.concatenate([zero_v, v2[:, LANES:]], axis=1)], axis=0)
        intra[ch] = jnp.dot(sc2, vbd, preferred_element_type=F32)
    yield
    st = s_ref[...]
    starts = {}
    for ch in order:
        starts[ch] = st.astype(BF16)
        st = st * decs[ch] + kvs[ch]
    s_ref[...] = st
    yield
    for ch in order:
        inter = lax.dot_general(qds[ch], starts[ch], _NT, preferred_element_type=F32)
        o_ref[0, ch * C:(ch + 1) * C, :] = inter + intra[ch]
    yield


def _gla_body(kf, qf, vf, laf, kb, qb, vb, lab, h0f, h0b, of, ob, hf_out, hb_out, sf, sb, *, nchunk):
    i = pl.program_id(2)

    @pl.when(i == 0)
    def _():
        sf[...] = h0f[0, 0]
        sb[...] = h0b[0, 0]

    sweeps = [_gla_direction(kf, qf, vf, laf, of, sf, False, nchunk),
              _gla_direction(kb, qb, vb, lab, ob, sb, True, nchunk)]
    for _ in range(GLA_STAGES):
        for sweep in sweeps:
            next(sweep)

    @pl.when(i == pl.num_programs(2) - 1)
    def _():
        hf_out[0, 0] = sf[...]
        hb_out[0, 0] = sb[...]


def _gla(k, q, v, laf, lab, h0f, h0b, lt):
    bsz, l, _ = k.shape
    pairs = h0f.shape[1]
    nt = l // lt
    fwd = lambda b, hp, i: (b, i, hp)
    bwd = lambda b, hp, i: (b, nt - 1 - i, hp)
    st = lambda b, hp, i: (b, hp, 0, 0)
    kq = lambda m: pl.BlockSpec((1, lt, LANES), m)
    vv = lambda m: pl.BlockSpec((1, lt, 2 * LANES), m)
    sspec = pl.BlockSpec((1, 1, 2 * LANES, LANES), st)
    return pl.pallas_call(
        functools.partial(_gla_body, nchunk=lt // GLA_CHUNK),
        grid=(bsz, pairs, nt),
        in_specs=[kq(fwd), kq(fwd), vv(fwd), kq(fwd), kq(bwd), kq(bwd), vv(bwd), kq(bwd), sspec, sspec],
        out_specs=[vv(fwd), vv(bwd), sspec, sspec],
        out_shape=[jax.ShapeDtypeStruct(v.shape, F32), jax.ShapeDtypeStruct(v.shape, F32),
                   jax.ShapeDtypeStruct(h0f.shape, F32), jax.ShapeDtypeStruct(h0b.shape, F32)],
        scratch_shapes=[pltpu.VMEM((2 * LANES, LANES), F32), pltpu.VMEM((2 * LANES, LANES), F32)],
        compiler_params=_cparams(("parallel", "parallel", "arbitrary")),
        name="gla",
    )(k, q, v, laf, k, q, v, lab, h0f, h0b)


def _pool_body(p_ref, w_ref, sc_ref, o_ref, s1_ref, *, half, rows):
    xb = p_ref[0]
    l, ch = xb.shape
    t = lax.broadcasted_iota(I32, (l, 1), 0)
    col = t % GRID_W
    row = t // GRID_W

    def counts(pos, n):
        return (jnp.minimum(pos + half, n) - jnp.maximum(pos - half, 0)).astype(F32)

    blk = 4 * GRID_W
    r = lax.broadcasted_iota(I32, (blk, blk), 0)
    c = lax.broadcasted_iota(I32, (blk, blk), 1)
    band = ((r // GRID_W == c // GRID_W) & (c - r >= -half) & (c - r <= half - 1)).astype(BF16)
    for i in range(0, l // blk, 2):
        pair = jnp.concatenate([xb[i * blk:(i + 1) * blk], xb[(i + 1) * blk:(i + 2) * blk]], axis=1)
        sums = jnp.dot(band, pair, preferred_element_type=F32)
        s1_ref[i * blk:(i + 1) * blk, :] = sums[:, :ch]
        s1_ref[(i + 1) * blk:(i + 2) * blk, :] = sums[:, ch:]
    s1 = s1_ref[...]

    fwd = s1
    bwd = jnp.where(row >= 1, pltpu.roll(s1, GRID_W, axis=0), 0.0)
    s = 1
    while s < half:
        fwd = fwd + jnp.where(row + s <= rows - 1, pltpu.roll(fwd, l - s * GRID_W, axis=0), 0.0)
        bwd = bwd + jnp.where(row >= s, pltpu.roll(bwd, s * GRID_W, axis=0), 0.0)
        s *= 2
    pooled = (fwd + bwd) / (counts(row, rows) * counts(col, GRID_W)) - xb.astype(F32)
    mixed = jnp.dot(pooled.astype(BF16), w_ref[0].astype(BF16), preferred_element_type=F32)
    o_ref[0] = (mixed * sc_ref[0]).astype(BF16)


def _pool(pin, pool_w, pool_scale3, g):
    bsz, l, pw = pin.shape
    ch = pool_w.shape[-1]
    half = POOL_WINDOWS[g] // 2
    return pl.pallas_call(
        functools.partial(_pool_body, half=half, rows=l // GRID_W),
        grid=(bsz,),
        in_specs=[pl.BlockSpec((1, l, ch), lambda b: (b, 0, g)),
                  pl.BlockSpec((1, ch, ch), lambda b: (g, 0, 0)),
                  pl.BlockSpec((1, 1, ch), lambda b: (g, 0, 0))],
        out_specs=pl.BlockSpec((1, l, ch), lambda b: (b, 0, 0)),
        out_shape=jax.ShapeDtypeStruct((bsz, l, ch), BF16),
        scratch_shapes=[pltpu.VMEM((l, ch), F32)],
        compiler_params=_cparams(("parallel",)),
        name=f"pool{g}",
    )(pin, pool_w, pool_scale3)


def _merge_body(x_ref, m1_ref, s1_ref, g1_ref, m2_ref, s2_ref,
                of_ref, ob_ref, sg_ref, mx0, mx1, mx2, mx3,
                wm_ref, wgla_ref, wpool_ref, wout_ref, gn_ref, wr_ref,
                x1_ref, h2_ref, aff_ref, *, heads, ne):
    d = x_ref.shape[1]

    def sub_tile(rows):
        x = x_ref[rows, :]
        h = _modulated_norm(x, m1_ref[0], s1_ref[0]).astype(BF16)
        o = of_ref[rows, :] + ob_ref[rows, :]
        sg = sg_ref[rows, :].astype(F32)
        og = []
        for j in range(heads):
            oj = o[:, j * LANES:(j + 1) * LANES]
            oj = oj * lax.rsqrt(jnp.mean(oj * oj, axis=-1, keepdims=True) + EPS) * gn_ref[...]
            og.append((oj * sg[:, j * LANES:(j + 1) * LANES]).astype(BF16))
        og = jnp.concatenate(og, axis=1)
        mixed = jnp.concatenate([mx0[0, rows, :], mx1[0, rows, :], mx2[0, rows, :], mx3[0, rows, :]], axis=1)
        yield
        gates = jnp.dot(h, wm_ref[...], preferred_element_type=F32)
        bg = jnp.dot(og, wgla_ref[...], preferred_element_type=F32)
        bp = jnp.dot(mixed, wpool_ref[...], preferred_element_type=F32)
        yield
        gates = jax.nn.sigmoid(gates)
        z = (gates[:, :d] * bg + gates[:, d:] * bp).astype(BF16)
        yield
        y = jnp.dot(z, wout_ref[...], preferred_element_type=F32)
        yield
        x1 = x + g1_ref[0] * y
        x1_ref[rows, :] = x1
        h2 = _modulated_norm(x1, m2_ref[0], s2_ref[0])
        hi = h2.astype(BF16)
        h2_ref[rows, :] = hi
        lo = (h2 - hi.astype(F32)).astype(BF16)
        yield
        lg = (jnp.dot(hi, wr_ref[...], preferred_element_type=F32)
              + jnp.dot(lo, wr_ref[...], preferred_element_type=F32))
        yield
        lgt = lg.T
        logit = lgt[0:ne] + lgt[ne:2 * ne]
        mx = jnp.max(logit, axis=0, keepdims=True)
        ex = jnp.exp(logit - mx)
        aff_ref[0, :, rows] = ex / jnp.sum(ex, axis=0, keepdims=True)
        yield

    _staggered([sub_tile(r) for r in _row_groups(x_ref.shape[0])], 7)


def _merge(x2, vecs, of, ob, sg, mixed, wm, wgla, wpool, wout, gn, wr, rows_per_sample, heads, ne, tm):
    n, d = x2.shape
    bsz = n // rows_per_sample
    tps = rows_per_sample // tm
    row = lambda i: (i, 0)
    vec = lambda i: (i // tps, 0, 0)
    full = lambda a: pl.BlockSpec(a.shape, lambda i: (0,) * a.ndim)
    ch = mixed[0].shape[-1]
    mspec = pl.BlockSpec((1, tm, ch), lambda i: (i // tps, i % tps, 0))
    gv = of.shape[-1]
    return pl.pallas_call(
        functools.partial(_merge_body, heads=heads, ne=ne),
        grid=(n // tm,),
        in_specs=[pl.BlockSpec((tm, d), row)] + [pl.BlockSpec((1, 1, d), vec)] * 5
                 + [pl.BlockSpec((tm, gv), row)] * 3 + [mspec] * 4
                 + [full(wm), full(wgla), full(wpool), full(wout), full(gn), full(wr)],
        out_specs=[pl.BlockSpec((tm, d), row),
                   pl.BlockSpec((tm, d), row),
                   pl.BlockSpec((1, ne, tm), lambda i: (i // tps, 0, i % tps))],
        out_shape=[jax.ShapeDtypeStruct((n, d), F32),
                   jax.ShapeDtypeStruct((n, d), BF16),
                   jax.ShapeDtypeStruct((bsz, ne, rows_per_sample), F32)],
        compiler_params=_cparams(("parallel",)),
        name="merge",
    )(x2, *vecs, of, ob, sg, *mixed, wm, wgla, wpool, wout, gn, wr)


def _route_body(aff_ref, pos_ref, off_ref, cnt_ref, *, cap, ntb):
    a = aff_ref[0]
    ne, l = a.shape
    blk = ROUTE_BLK

    def bisect(i, v):
        cand = v | jnp.left_shift(jnp.int32(1), 30 - i)
        cnt = jnp.sum((a >= lax.bitcast_convert_type(cand, F32)).astype(F32), axis=1, keepdims=True)
        return jnp.where(cnt >= cap, cand, v)

    thr = lax.bitcast_convert_type(lax.fori_loop(0, 31, bisect, jnp.zeros((ne, 1), I32)), F32)
    gt = a > thr
    tie = a == thr
    need = cap - jnp.sum(gt.astype(F32), axis=1, keepdims=True)

    r = lax.broadcasted_iota(I32, (blk, blk), 0)
    c = lax.broadcasted_iota(I32, (blk, blk), 1)
    upper = (r <= c).astype(BF16)
    lane = lax.broadcasted_iota(I32, (1, LANES), 1)

    def prefix(mask_f):
        run = jnp.zeros((ne, 1), F32)
        offs = jnp.zeros((ne, LANES), F32)
        for tb in range(ntb):
            m = mask_f[:, tb * blk:(tb + 1) * blk].astype(BF16)
            loc = jnp.dot(m, upper, preferred_element_type=F32)
            cnt_ref[:, tb * blk:(tb + 1) * blk] = loc + run
            offs = jnp.where(lane == tb, run, offs)
            run = run + loc[:, blk - 1:blk]
        return jnp.where(lane >= ntb, run, offs)

    tie_f = tie.astype(F32)
    prefix(tie_f)
    tie_excl = cnt_ref[...] - tie_f
    sel = gt | (tie & (tie_excl < need))
    offs = prefix(sel.astype(F32))
    pos_ref[0] = jnp.where(sel, cnt_ref[...] - 1.0, -1.0).astype(I32)
    off_ref[0] = offs.astype(I32)


def _route(aff, cap):
    bsz, ne, l = aff.shape
    spec = lambda s: pl.BlockSpec((1,) + s, lambda b: (b, 0, 0))
    return pl.pallas_call(
        functools.partial(_route_body, cap=cap, ntb=l // ROUTE_BLK),
        grid=(bsz,),
        in_specs=[spec((ne, l))],
        out_specs=[spec((ne, l)), spec((ne, LANES))],
        out_shape=[jax.ShapeDtypeStruct((bsz, ne, l), I32),
                   jax.ShapeDtypeStruct((bsz, ne, LANES), I32)],
        scratch_shapes=[pltpu.VMEM((ne, l), F32)],
        compiler_params=_cparams(("parallel",)),
        name="route",
    )(aff)


def _window_plan(off_ref, bb, tt, experts, ne):
    lows = [(off_ref[(bb * ne + e) * LANES + tt] // 16) * 16 for e in experts]
    ends = [off_ref[(bb * ne + e) * LANES + tt + 1] for e in experts]
    return lows, ends


def _gatherx_body(off_ref, h_ref, pos_ref, xs_ref, *, ne, cap, eg):
    b = pl.program_id(0)
    g = pl.program_id(1)
    tb = pl.program_id(2)
    win = SLOT_WIN

    @pl.when(tb == 0)
    def _():
        xs_ref[...] = jnp.zeros(xs_ref.shape, BF16)

    j_col = lax.broadcasted_iota(I32, (win, 1), 0)
    t = ROUTE_BLK
    for sub in range(h_ref.shape[0] // t):
        lows, ends = _window_plan(off_ref, b, tb * (h_ref.shape[0] // t) + sub, [g * eg + k for k in range(eg)], ne)
        rounds = jnp.int32(0)
        for k in range(eg):
            rounds = jnp.maximum(rounds, (ends[k] - lows[k] + win - 1) // win)

        def one_round(r, carry, lows=lows, sub=sub):
            starts = [pl.multiple_of(jnp.minimum(lows[k] + r * win, cap - win), 16) for k in range(eg)]
            pieces = []
            for k in range(eg):
                p = pos_ref[0, k:k + 1, sub * t:(sub + 1) * t]
                hit = (p - starts[k] == j_col) & (p >= lows[k] + r * win)
                pieces.append(jnp.where(hit, 1.0, 0.0).astype(BF16))
            sel = jnp.concatenate(pieces, axis=0)
            rows = jnp.dot(sel, h_ref[sub * t:(sub + 1) * t, :],
                           preferred_element_type=F32).astype(BF16)
            for k in range(eg):
                dst = (0, k, pl.ds(starts[k], win), slice(None))
                xs_ref[dst] = xs_ref[dst] + rows[k * win:(k + 1) * win]
            return carry

        lax.fori_loop(0, rounds, one_round, 0)


def _gatherx(off_flat, h2, pos, cap):
    n, d = h2.shape
    bsz, ne, l = pos.shape
    t = min(l, 4 * ROUTE_BLK)
    ntb = l // t
    eg = 8
    grid_spec = pltpu.PrefetchScalarGridSpec(
        num_scalar_prefetch=1,
        grid=(bsz, ne // eg, ntb),
        in_specs=[pl.BlockSpec((t, d), lambda b, g, i, off: (b * ntb + i, 0)),
                  pl.BlockSpec((1, eg, t), lambda b, g, i, off: (b, g, i))],
        out_specs=pl.BlockSpec((1, eg, cap, d), lambda b, g, i, off: (b, g, 0, 0)),
    )
    return pl.pallas_call(
        functools.partial(_gatherx_body, ne=ne, cap=cap, eg=eg),
        grid_spec=grid_spec,
        out_shape=jax.ShapeDtypeStruct((bsz, ne, cap, d), BF16),
        compiler_params=_cparams(("parallel", "parallel", "arbitrary")),
        name="gatherx",
    )(off_flat, h2, pos)


def _moe_body(xs_ref, wg_ref, wu_ref, wd_ref, y_ref, *, rc):
    wg = wg_ref[0].astype(BF16)
    wu = wu_ref[0].astype(BF16)
    wd = wd_ref[0].astype(BF16)
    cap = xs_ref.shape[2]
    acts = []
    for ch in range(cap // rc):
        xs = xs_ref[0, 0, ch * rc:(ch + 1) * rc, :]
        gate = jnp.dot(xs, wg, preferred_element_type=F32)
        up = jnp.dot(xs, wu, preferred_element_type=F32)
        acts.append((_silu(gate) * up).astype(BF16))
    for ch in range(cap // rc):
        y_ref[0, 0, ch * rc:(ch + 1) * rc, :] = jnp.dot(acts[ch], wd, preferred_element_type=F32).astype(BF16)


def _moe(xs, wg, wu, wd):
    bsz, ne, cap, d = xs.shape
    de = wg.shape[2]
    slot = pl.BlockSpec((1, 1, cap, d), lambda b, e: (b, e, 0, 0))
    return pl.pallas_call(
        functools.partial(_moe_body, rc=min(cap, 512)),
        grid=(bsz, ne),
        in_specs=[slot,
                  pl.BlockSpec((1, d, de), lambda b, e: (e, 0, 0)),
                  pl.BlockSpec((1, d, de), lambda b, e: (e, 0, 0)),
                  pl.BlockSpec((1, de, d), lambda b, e: (e, 0, 0))],
        out_specs=slot,
        out_shape=jax.ShapeDtypeStruct((bsz, ne, cap, d), BF16),
        compiler_params=_cparams(("parallel", "arbitrary")),
        name="moe",
    )(xs, wg, wu, wd)


def _combine_body(off_ref, x1_ref, g2_ref, fg_ref, pos_ref, aff_ref, y_hbm, o_ref, ybuf, sem,
                  *, ne, cap, ntb):
    b = pl.program_id(0)
    tb = pl.program_id(1)
    step = b * ntb + tb
    nsteps = pl.num_programs(0) * ntb
    slot = step % 2
    spare = 2
    win = SLOT_WIN

    def block_lows(bb, tt):
        return _window_plan(off_ref, bb, tt, range(ne), ne)[0]

    def window_copy(bb, e, start, buf):
        return pltpu.make_async_copy(y_hbm.at[bb, e, pl.ds(start, win), :],
                                     ybuf.at[buf, pl.ds(e * win, win), :], sem.at[buf, e])

    def window_starts(lows, r):
        return [pl.multiple_of(jnp.minimum(lows[e] + r * win, cap - win), 16) for e in range(ne)]

    def start_first_round(bb, tt, buf):
        starts = window_starts(block_lows(bb, tt), 0)
        for e in range(ne):
            window_copy(bb, e, starts[e], buf).start()

    @pl.when(step == 0)
    def _():
        start_first_round(b, tb, 0)

    @pl.when(step + 1 < nsteps)
    def _():
        nxt = step + 1
        start_first_round(nxt // ntb, nxt % ntb, 1 - slot)

    lows, ends = _window_plan(off_ref, b, tb, range(ne), ne)
    rounds = jnp.int32(0)
    for e in range(ne):
        rounds = jnp.maximum(rounds, (ends[e] - lows[e] + win - 1) // win)
    j_col = lax.broadcasted_iota(I32, (win, 1), 0)

    def expand(r, starts, buf):
        pieces = []
        for e in range(ne):
            p = pos_ref[0, e:e + 1, :]
            valid = p >= lows[e] + r * win
            hit = (p - starts[e] == j_col) & valid
            pieces.append(jnp.where(hit, aff_ref[0, e:e + 1, :], 0.0).astype(BF16))
        pmat = jnp.concatenate(pieces, axis=0)
        return lax.dot_general(pmat, ybuf[buf], _TN, preferred_element_type=F32)

    starts0 = window_starts(lows, 0)
    for e in range(ne):
        window_copy(b, e, starts0[e], slot).wait()
    moe = expand(0, starts0, slot)

    def extra_round(r, acc):
        starts = window_starts(lows, r)
        for e in range(ne):
            window_copy(b, e, starts[e], spare).start()
        for e in range(ne):
            window_copy(b, e, starts[e], spare).wait()
        return acc + expand(r, starts, spare)

    moe = lax.fori_loop(1, rounds, extra_round, moe)
    x2 = x1_ref[...] + g2_ref[0] * moe
    ms = jnp.mean(x2 * x2, axis=-1, keepdims=True)
    o_ref[...] = x2 * lax.rsqrt(ms + EPS) * fg_ref[...]


def _combine(off_flat, x1, g2, fg, pos, aff, y, rows_per_sample):
    n, d = x1.shape
    bsz, ne, cap, _ = y.shape
    t = ROUTE_BLK
    ntb = rows_per_sample // t
    grid_spec = pltpu.PrefetchScalarGridSpec(
        num_scalar_prefetch=1,
        grid=(bsz, ntb),
        in_specs=[pl.BlockSpec((t, d), lambda b, i, off: (b * ntb + i, 0)),
                  pl.BlockSpec((1, 1, d), lambda b, i, off: (b, 0, 0)),
                  pl.BlockSpec((1, d), lambda b, i, off: (0, 0)),
                  pl.BlockSpec((1, ne, t), lambda b, i, off: (b, 0, i)),
                  pl.BlockSpec((1, ne, t), lambda b, i, off: (b, 0, i)),
                  pl.BlockSpec(memory_space=pl.ANY)],
        out_specs=pl.BlockSpec((t, d), lambda b, i, off: (b * ntb + i, 0)),
        scratch_shapes=[pltpu.VMEM((3, ne * SLOT_WIN, d), BF16), pltpu.SemaphoreType.DMA((3, ne))],
    )
    return pl.pallas_call(
        functools.partial(_combine_body, ne=ne, cap=cap, ntb=ntb),
        grid_spec=grid_spec,
        out_shape=jax.ShapeDtypeStruct((n, d), F32),
        compiler_params=_cparams(("arbitrary", "arbitrary")),
        name="combine",
    )(off_flat, x1, g2, fg, pos, aff, y)


def kernel(x, c, ctx, c_ctx, ada_w, ada_b, norm1_g, norm2_g, w_in, w_decay_up, b_decay, gla_norm_g,
           w_gla_proj, pool_w, pool_scale, w_pool_proj, w_out, w_router, w_gate_e, w_up_e, w_down_e,
           final_norm_g):
    assert ada_w.shape[0] == 1, "single-layer block"
    bsz, l, d = x.shape
    lc = ctx.shape[1]
    rank, dk = w_decay_up.shape[2], w_decay_up.shape[3]
    dvh = gla_norm_g.shape[1]
    dv = w_gla_proj.shape[1]
    heads = dv // dvh
    groups, ch = pool_w.shape[1], pool_w.shape[2]
    pw = groups * ch
    ne = w_router.shape[2]
    cap = EC_CAPACITY * l // ne
    assert dk // heads == 64 and dvh == LANES and ch == LANES and heads % 2 == 0
    assert l % ROUTE_BLK == 0 and l // ROUTE_BLK < LANES and cap >= SLOT_WIN and cap % 16 == 0

    cin = jnp.zeros((8, d), F32).at[:bsz].set(c).at[bsz].set(c_ctx)
    mods = _ada(cin, ada_w[0], ada_b[0][None, :])
    sh1, sc1, gt1, sh2, sc2, gt2 = [mods[:, i * d:(i + 1) * d] for i in range(6)]
    vec3 = lambda a: a[:, None, :]
    mult1 = norm1_g[0][None, :] * (1.0 + sc1)
    mult2 = norm2_g[0][None, :] * (1.0 + sc2)

    o_r = dk + dv
    o_q = o_r + 2 * rank
    o_g = o_q + dk
    o_p = o_g + dv
    o_m = o_p + pw
    up = jnp.zeros((2 * rank, 2 * dk), F32)
    up = up.at[:rank, :dk].set(w_decay_up[0, 0]).at[rank:, dk:].set(w_decay_up[0, 1])
    w1, wm = _prep(w_in, up, o_r, o_q, o_g, o_m, float(dk // heads) ** -0.5)
    bz = b_decay[0]

    zero_state = jnp.zeros((bsz, heads // 2, 2 * LANES, LANES), F32)
    cm = jnp.broadcast_to(vec3(mult1[bsz:bsz + 1]), (bsz, 1, d))
    cs = jnp.broadcast_to(vec3(sh1[bsz:bsz + 1]), (bsz, 1, d))
    ck, cv, _, claf, clab, _, _ = _inproj(ctx.reshape(bsz * lc, d), cm, cs, w1, bz, lc, dk, dv, pw, min(lc, 256))
    r3 = lambda a, n: a.reshape(bsz, n, a.shape[-1])
    _, _, h_f, h_b = _gla(r3(ck, lc), r3(ck, lc), r3(cv, lc), r3(claf, lc), r3(clab, lc),
                          zero_state, zero_state, min(lc, 256))

    x2 = x.reshape(bsz * l, d)
    tm = min(l, 512)
    k, v, q, laf, lab, sg, pin = _inproj(x2, vec3(mult1[:bsz]), vec3(sh1[:bsz]), w1, bz, l, dk, dv, pw, tm)
    of, ob, _, _ = _gla(r3(k, l), r3(q, l), r3(v, l), r3(laf, l), r3(lab, l), h_f, h_b, min(l, 512))
    pin3 = r3(pin, l)
    psc = pool_scale[0].reshape(groups, 1, ch)
    mixed = [_pool(pin3, pool_w[0], psc, g) for g in range(groups)]

    wr_hi = w_router[0].astype(BF16)
    wr_lo = (w_router[0] - wr_hi.astype(F32)).astype(BF16)
    wr = jnp.zeros((d, LANES), BF16).at[:, :ne].set(wr_hi).at[:, ne:2 * ne].set(wr_lo)
    vecs = [vec3(mult1[:bsz]), vec3(sh1[:bsz]), vec3(gt1[:bsz]), vec3(mult2[:bsz]), vec3(sh2[:bsz])]
    x1, h2, aff = _merge(x2, vecs, of.reshape(bsz * l, dv), ob.reshape(bsz * l, dv), sg, mixed,
                          wm, w_gla_proj[0].astype(BF16), w_pool_proj[0].astype(BF16), w_out[0].astype(BF16),
                          gla_norm_g[0][None, :], wr, l, heads, ne, tm)

    pos, offs = _route(aff, cap)
    off_flat = offs.reshape(-1)
    xs = _gatherx(off_flat, h2, pos, cap)
    y = _moe(xs, w_gate_e[0], w_up_e[0], w_down_e[0])
    out = _combine(off_flat, x1, vec3(gt2[:bsz]), final_norm_g[None, :], pos, aff, y, l)
    return out.reshape(bsz, l, d)
```

```python
import functools

import jax
import jax.numpy as jnp
from jax import lax
from jax.experimental import pallas as pl
from jax.experimental.pallas import tpu as pltpu

F32 = jnp.float32
BF16 = jnp.bfloat16
I32 = jnp.int32
HIGHEST = lax.Precision.HIGHEST

EPS = 1e-6
GRID_W = 64
GLA_CHUNK = 64
GLA_STAGES = 6
GATE_NORMALIZER = 16.0
POOL_WINDOWS = (2, 4, 8, 16)
EC_CAPACITY = 2
LANES = 128
ROUTE_BLK = 256
SLOT_WIN = 64
SUB_ROWS = 256
VMEM_LIMIT = 56 * 1024 * 1024

_NT = (((1,), (1,)), ((), ()))
_TN = (((0,), (0,)), ((), ()))


def _cparams(sem):
    return pltpu.CompilerParams(dimension_semantics=sem, vmem_limit_bytes=VMEM_LIMIT)


def _silu(x):
    return x * jax.nn.sigmoid(x)


def _row_groups(n):
    step = min(n, SUB_ROWS)
    return [pl.ds(i, step) for i in range(0, n, step)]


def _staggered(gens, nstages):
    for t in range(nstages + len(gens) - 1):
        for g in reversed(range(len(gens))):
            if 0 <= t - g < nstages:
                next(gens[g])


def _ada_body(c_ref, w_ref, b_ref, o_ref):
    s = _silu(c_ref[...])
    rows = s.shape[0]
    s_hi = s.astype(BF16)
    s_lo = (s - s_hi.astype(F32)).astype(BF16)
    w = w_ref[...]
    w_hi = w.astype(BF16)
    w_lo = (w - w_hi.astype(F32)).astype(BF16)
    both = jnp.dot(jnp.concatenate([s_hi, s_lo], axis=0), w_hi, preferred_element_type=F32)
    o_ref[...] = (both[:rows] + both[rows:] + jnp.dot(s_hi, w_lo, preferred_element_type=F32)) + b_ref[...]


def _ada(cin, w, b):
    rows, d = cin.shape
    n = w.shape[1]
    nb = n // 4
    return pl.pallas_call(
        _ada_body,
        grid=(n // nb,),
        in_specs=[pl.BlockSpec((rows, d), lambda j: (0, 0)),
                  pl.BlockSpec((d, nb), lambda j: (0, j)),
                  pl.BlockSpec((1, nb), lambda j: (0, j))],
        out_specs=pl.BlockSpec((rows, nb), lambda j: (0, j)),
        out_shape=jax.ShapeDtypeStruct((rows, n), F32),
        compiler_params=_cparams(("arbitrary",)),
        name="ada",
    )(cin, w, b)


def _prep_body(w_ref, up_ref, w1_ref, wm_ref, *, o_r, o_q, o_g, o_m, qscale):
    w = w_ref[0]
    dk = o_g - o_q
    wz = jnp.dot(w[:, o_r:o_q], up_ref[...], precision=HIGHEST, preferred_element_type=F32)
    nz = wz.shape[1]
    w1_ref[:, :o_r] = w[:, :o_r].astype(BF16)
    w1_ref[:, o_r:o_r + dk] = (w[:, o_q:o_g] * qscale).astype(BF16)
    w1_ref[:, o_r + dk:o_r + dk + nz] = wz.astype(BF16)
    w1_ref[:, o_r + dk + nz:] = w[:, o_g:o_m].astype(BF16)
    wm_ref[...] = w[:, o_m:].astype(BF16)


def _prep(w_in, up, o_r, o_q, o_g, o_m, qscale):
    _, d, n = w_in.shape
    n1 = o_r + (o_g - o_q) + up.shape[1] + (o_m - o_g)
    tr = d // 4
    return pl.pallas_call(
        functools.partial(_prep_body, o_r=o_r, o_q=o_q, o_g=o_g, o_m=o_m, qscale=qscale),
        grid=(d // tr,),
        in_specs=[pl.BlockSpec((1, tr, n), lambda i: (0, i, 0)),
                  pl.BlockSpec(up.shape, lambda i: (0, 0))],
        out_specs=[pl.BlockSpec((tr, n1), lambda i: (i, 0)),
                   pl.BlockSpec((tr, n - o_m), lambda i: (i, 0))],
        out_shape=[jax.ShapeDtypeStruct((d, n1), BF16), jax.ShapeDtypeStruct((d, n - o_m), BF16)],
        compiler_params=_cparams(("parallel",)),
        name="prep",
    )(w_in, up)


def _modulated_norm(x, mult, shift):
    ms = jnp.mean(x * x, axis=-1, keepdims=True)
    return (x * lax.rsqrt(ms + EPS)) * mult + shift


def _log_sigmoid(z):
    return jnp.minimum(z, 0.0) - jnp.log1p(jnp.exp(-jnp.abs(z)))


def _inproj_body(x_ref, mult_ref, shift_ref, w_ref, bz_ref,
                 k_ref, v_ref, q_ref, laf_ref, lab_ref, sg_ref, p_ref, *, dk, dv, pw):
    def sub_tile(rows):
        h = _modulated_norm(x_ref[rows, :], mult_ref[0], shift_ref[0]).astype(BF16)
        yield
        u = jnp.dot(h, w_ref[...], preferred_element_type=F32)
        yield
        o = 0
        k_ref[rows, :] = u[:, o:o + dk].astype(BF16); o += dk
        v_ref[rows, :] = u[:, o:o + dv].astype(BF16); o += dv
        q_ref[rows, :] = u[:, o:o + dk].astype(BF16); o += dk
        zf = u[:, o:o + dk] + bz_ref[0:1, :]; o += dk
        zb = u[:, o:o + dk] + bz_ref[1:2, :]; o += dk
        laf_ref[rows, :] = _log_sigmoid(zf) * (1.0 / GATE_NORMALIZER)
        lab_ref[rows, :] = _log_sigmoid(zb) * (1.0 / GATE_NORMALIZER)
        sg_ref[rows, :] = _silu(u[:, o:o + dv]).astype(BF16); o += dv
        p_ref[rows, :] = u[:, o:o + pw].astype(BF16)
        yield

    _staggered([sub_tile(r) for r in _row_groups(x_ref.shape[0])], 3)


def _inproj(x2, mult, shift, w, bz, rows_per_sample, dk, dv, pw, tm):
    n, d = x2.shape
    tps = rows_per_sample // tm
    row = lambda i: (i, 0)
    vec = lambda i: (i // tps, 0, 0)
    outs = [(dk, BF16), (dv, BF16), (dk, BF16), (dk, F32), (dk, F32), (dv, BF16), (pw, BF16)]
    return pl.pallas_call(
        functools.partial(_inproj_body, dk=dk, dv=dv, pw=pw),
        grid=(n // tm,),
        in_specs=[pl.BlockSpec((tm, d), row),
                  pl.BlockSpec((1, 1, d), vec),
                  pl.BlockSpec((1, 1, d), vec),
                  pl.BlockSpec(w.shape, lambda i: (0, 0)),
                  pl.BlockSpec(bz.shape, lambda i: (0, 0))],
        out_specs=[pl.BlockSpec((tm, c), row) for c, _ in outs],
        out_shape=[jax.ShapeDtypeStruct((n, c), t) for c, t in outs],
        compiler_params=_cparams(("parallel",)),
        name="inproj",
    )(x2, mult, shift, w, bz)


def _gla_direction(k_ref, q_ref, v_ref, la_ref, o_ref, s_ref, reverse, nchunk):
    C = GLA_CHUNK
    hd = C
    cb = 4 * C
    lt = nchunk * C
    la = la_ref[0]
    r = lax.broadcasted_iota(I32, (cb, cb), 0)
    c = lax.broadcasted_iota(I32, (cb, cb), 1)
    same = (r // C) == (c // C)
    cum = jnp.where(same & ((c >= r) if reverse else (c <= r)), 1.0, 0.0).astype(BF16)
    la_hi = la.astype(BF16)
    la_lo = (la - la_hi.astype(F32)).astype(BF16)
    la2 = jnp.concatenate([la_hi, la_lo], axis=1)
    bcs = []
    for blk in range(lt // cb):
        part = jnp.dot(cum, la2[blk * cb:(blk + 1) * cb], preferred_element_type=F32)
        bcs.append(part[:, :LANES] + part[:, LANES:])
    ri = lax.broadcasted_iota(I32, (2 * C, C), 0) % C
    ci = lax.broadcasted_iota(I32, (2 * C, C), 1)
    tri = (ci >= ri) if reverse else (ci <= ri)
    lane = lax.broadcasted_iota(I32, (1, LANES), 1)
    m0 = (lane < hd).astype(F32)
    m1 = (lane >= hd).astype(F32)
    sr = lax.broadcasted_iota(I32, (2 * LANES, LANES), 0)
    sl = lax.broadcasted_iota(I32, (2 * LANES, LANES), 1)
    smask = ((sr < LANES) == (sl < hd)).astype(F32)
    kt = k_ref[0].astype(F32)
    qt = q_ref[0].astype(F32)
    zero_v = jnp.zeros((C, LANES), BF16)
    order = list(range(nchunk - 1, -1, -1) if reverse else range(nchunk))
    intra, qds, kvs, decs, q2s, kss, kws, scs = {}, {}, {}, {}, {}, {}, {}, {}
    for ch in order:
        lo = ch * C
        b = bcs[lo // cb][lo % cb:lo % cb + C]
        last = b[0:1] if reverse else b[C - 1:C]
        mid = b[C // 2:C // 2 + 1] if reverse else b[C // 2 - 1:C // 2]
        kc = kt[lo:lo + C]
        qc = qt[lo:lo + C]
        qs = qc * jnp.exp(b - mid)
        kss[ch] = (kc * jnp.exp(mid - b)).astype(BF16)
        qds[ch] = (qc * jnp.exp(b)).astype(BF16)
        kws[ch] = (kc * jnp.exp(last - b)).astype(BF16)
        decs[ch] = jnp.exp(last)
        q2s[ch] = jnp.concatenate([qs * m0, qs * m1], axis=0).astype(BF16)
    yield
    for ch in order:
        scs[ch] = lax.dot_general(q2s[ch], kss[ch], _NT, preferred_element_type=F32)
    yield
    for ch in order:
        v2 = v_ref[0, ch * C:(ch + 1) * C, :]
        kvs[ch] = lax.dot_general(v2, kws[ch], _TN, preferred_element_type=F32) * smask
    yield
    for ch in order:
        sc = jnp.where(tri, scs[ch], 0.0).astype(BF16)
        sc2 = jnp.concatenate([sc[:C], sc[C:]], axis=1)
        v2 = v_ref[0, ch * C:(ch + 1) * C, :]
        vbd = jnp.concatenate([jnp.concatenate([v2[:, :LANES], zero_v], axis=1),
                               jnp.concatenate([zero_v, v2[:, LANES:]], axis=1)], axis=0)
        intra[ch] = jnp.dot(sc2, vbd, preferred_element_type=F32)
    yield
    st = s_ref[...]
    starts = {}
    for ch in order:
        starts[ch] = st.astype(BF16)
        st = st * decs[ch] + kvs[ch]
    s_ref[...] = st
    yield
    for ch in order:
        inter = lax.dot_general(qds[ch], starts[ch], _NT, preferred_element_type=F32)
        o_ref[0, ch * C:(ch + 1) * C, :] = inter + intra[ch]
    yield


def _gla_body(kf, qf, vf, laf, kb, qb, vb, lab, h0f, h0b, of, ob, hf_out, hb_out, sf, sb, *, nchunk):
    i = pl.program_id(2)

    @pl.when(i == 0)
    def _():
        sf[...] = h0f[0, 0]
        sb[...] = h0b[0, 0]

    sweeps = [_gla_direction(kf, qf, vf, laf, of, sf, False, nchunk),
              _gla_direction(kb, qb, vb, lab, ob, sb, True, nchunk)]
    for _ in range(GLA_STAGES):
        for sweep in sweeps:
            next(sweep)

    @pl.when(i == pl.num_programs(2) - 1)
    def _():
        hf_out[0, 0] = sf[...]
        hb_out[0, 0] = sb[...]


def _gla(k, q, v, laf, lab, h0f, h0b, lt):
    bsz, l, _ = k.shape
    pairs = h0f.shape[1]
    nt = l // lt
    fwd = lambda b, hp, i: (b, i, hp)
    bwd = lambda b, hp, i: (b, nt - 1 - i, hp)
    st = lambda b, hp, i: (b, hp, 0, 0)
    kq = lambda m: pl.BlockSpec((1, lt, LANES), m)
    vv = lambda m: pl.BlockSpec((1, lt, 2 * LANES), m)
    sspec = pl.BlockSpec((1, 1, 2 * LANES, LANES), st)
    return pl.pallas_call(
        functools.partial(_gla_body, nchunk=lt // GLA_CHUNK),
        grid=(bsz, pairs, nt),
        in_specs=[kq(fwd), kq(fwd), vv(fwd), kq(fwd), kq(bwd), kq(bwd), vv(bwd), kq(bwd), sspec, sspec],
        out_specs=[vv(fwd), vv(bwd), sspec, sspec],
        out_shape=[jax.ShapeDtypeStruct(v.shape, F32), jax.ShapeDtypeStruct(v.shape, F32),
                   jax.ShapeDtypeStruct(h0f.shape, F32), jax.ShapeDtypeStruct(h0b.shape, F32)],
        scratch_shapes=[pltpu.VMEM((2 * LANES, LANES), F32), pltpu.VMEM((2 * LANES, LANES), F32)],
        compiler_params=_cparams(("parallel", "parallel", "arbitrary")),
        name="gla",
    )(k, q, v, laf, k, q, v, lab, h0f, h0b)


def _pool_body(p_ref, w_ref, sc_ref, o_ref, s1_ref, *, half, rows):
    xb = p_ref[0]
    l, ch = xb.shape
    t = lax.broadcasted_iota(I32, (l, 1), 0)
    col = t % GRID_W
    row = t // GRID_W

    def counts(pos, n):
        return (jnp.minimum(pos + half, n) - jnp.maximum(pos - half, 0)).astype(F32)

    blk = 4 * GRID_W
    r = lax.broadcasted_iota(I32, (blk, blk), 0)
    c = lax.broadcasted_iota(I32, (blk, blk), 1)
    band = ((r // GRID_W == c // GRID_W) & (c - r >= -half) & (c - r <= half - 1)).astype(BF16)
    for i in range(0, l // blk, 2):
        pair = jnp.concatenate([xb[i * blk:(i + 1) * blk], xb[(i + 1) * blk:(i + 2) * blk]], axis=1)
        sums = jnp.dot(band, pair, preferred_element_type=F32)
        s1_ref[i * blk:(i + 1) * blk, :] = sums[:, :ch]
        s1_ref[(i + 1) * blk:(i + 2) * blk, :] = sums[:, ch:]
    s1 = s1_ref[...]

    fwd = s1
    bwd = jnp.where(row >= 1, pltpu.roll(s1, GRID_W, axis=0), 0.0)
    s = 1
    while s < half:
        fwd = fwd + jnp.where(row + s <= rows - 1, pltpu.roll(fwd, l - s * GRID_W, axis=0), 0.0)
        bwd = bwd + jnp.where(row >= s, pltpu.roll(bwd, s * GRID_W, axis=0), 0.0)
        s *= 2
    pooled = (fwd + bwd) / (counts(row, rows) * counts(col, GRID_W)) - xb.astype(F32)
    mixed = jnp.dot(pooled.astype(BF16), w_ref[0].astype(BF16), preferred_element_type=F32)
    o_ref[0] = (mixed * sc_ref[0]).astype(BF16)


def _pool_groups_body(p_ref, w_ref, sc_ref, o_ref, s1_ref, *, rows):
    for gi, window in enumerate(POOL_WINDOWS):
        @pl.when(pl.program_id(1) == gi)
        def _(half=window // 2):
            _pool_body(p_ref, w_ref, sc_ref, o_ref, s1_ref, half=half, rows=rows)


def _pool(pin, pool_w, pool_scale3):
    bsz, l, pw = pin.shape
    groups, ch = pool_w.shape[0], pool_w.shape[-1]
    return pl.pallas_call(
        functools.partial(_pool_groups_body, rows=l // GRID_W),
        grid=(bsz, groups),
        in_specs=[pl.BlockSpec((1, l, ch), lambda b, g: (b, 0, g)),
                  pl.BlockSpec((1, ch, ch), lambda b, g: (g, 0, 0)),
                  pl.BlockSpec((1, 1, ch), lambda b, g: (g, 0, 0))],
        out_specs=pl.BlockSpec((1, l, ch), lambda b, g: (b, 0, g)),
        out_shape=jax.ShapeDtypeStruct((bsz, l, pw), BF16),
        scratch_shapes=[pltpu.VMEM((l, ch), F32)],
        compiler_params=_cparams(("parallel", "parallel")),
        name="pool",
    )(pin, pool_w, pool_scale3)


def _merge_body(x_ref, m1_ref, s1_ref, g1_ref, m2_ref, s2_ref,
                of_ref, ob_ref, sg_ref, mx_ref,
                wm_ref, wgla_ref, wpool_ref, wout_ref, gn_ref, wr_ref,
                x1_ref, h2_ref, aff_ref, *, heads, ne):
    d = x_ref.shape[1]

    def sub_tile(rows):
        x = x_ref[rows, :]
        h = _modulated_norm(x, m1_ref[0], s1_ref[0]).astype(BF16)
        o = of_ref[rows, :] + ob_ref[rows, :]
        sg = sg_ref[rows, :].astype(F32)
        og = []
        for j in range(heads):
            oj = o[:, j * LANES:(j + 1) * LANES]
            oj = oj * lax.rsqrt(jnp.mean(oj * oj, axis=-1, keepdims=True) + EPS) * gn_ref[...]
            og.append((oj * sg[:, j * LANES:(j + 1) * LANES]).astype(BF16))
        og = jnp.concatenate(og, axis=1)
        mixed = mx_ref[rows, :]
        yield
        gates = jnp.dot(h, wm_ref[...], preferred_element_type=F32)
        bg = jnp.dot(og, wgla_ref[...], preferred_element_type=F32)
        bp = jnp.dot(mixed, wpool_ref[...], preferred_element_type=F32)
        yield
        gates = jax.nn.sigmoid(gates)
        z = (gates[:, :d] * bg + gates[:, d:] * bp).astype(BF16)
        yield
        y = jnp.dot(z, wout_ref[...], preferred_element_type=F32)
        yield
        x1 = x + g1_ref[0] * y
        x1_ref[rows, :] = x1
        h2 = _modulated_norm(x1, m2_ref[0], s2_ref[0])
        hi = h2.astype(BF16)
        h2_ref[rows, :] = hi
        lo = (h2 - hi.astype(F32)).astype(BF16)
        yield
        lg = (jnp.dot(hi, wr_ref[...], preferred_element_type=F32)
              + jnp.dot(lo, wr_ref[...], preferred_element_type=F32))
        yield
        lgt = lg.T
        logit = lgt[0:ne] + lgt[ne:2 * ne]
        mx = jnp.max(logit, axis=0, keepdims=True)
        ex = jnp.exp(logit - mx)
        aff_ref[0, :, rows] = ex / jnp.sum(ex, axis=0, keepdims=True)
        yield

    _staggered([sub_tile(r) for r in _row_groups(x_ref.shape[0])], 7)


def _merge(x2, vecs, of, ob, sg, mixed, wm, wgla, wpool, wout, gn, wr, rows_per_sample, heads, ne, tm):
    n, d = x2.shape
    bsz = n // rows_per_sample
    tps = rows_per_sample // tm
    row = lambda i: (i, 0)
    vec = lambda i: (i // tps, 0, 0)
    full = lambda a: pl.BlockSpec(a.shape, lambda i: (0,) * a.ndim)
    gv = of.shape[-1]
    return pl.pallas_call(
        functools.partial(_merge_body, heads=heads, ne=ne),
        grid=(n // tm,),
        in_specs=[pl.BlockSpec((tm, d), row)] + [pl.BlockSpec((1, 1, d), vec)] * 5
                 + [pl.BlockSpec((tm, gv), row)] * 4
                 + [full(wm), full(wgla), full(wpool), full(wout), full(gn), full(wr)],
        out_specs=[pl.BlockSpec((tm, d), row),
                   pl.BlockSpec((tm, d), row),
                   pl.BlockSpec((1, ne, tm), lambda i: (i // tps, 0, i % tps))],
        out_shape=[jax.ShapeDtypeStruct((n, d), F32),
                   jax.ShapeDtypeStruct((n, d), BF16),
                   jax.ShapeDtypeStruct((bsz, ne, rows_per_sample), F32)],
        compiler_params=_cparams(("parallel",)),
        name="merge",
    )(x2, *vecs, of, ob, sg, mixed, wm, wgla, wpool, wout, gn, wr)


def _route_body(aff_ref, pos_ref, off_ref, cnt_ref, *, cap, ntb):
    a = aff_ref[0]
    ne, l = a.shape
    blk = ROUTE_BLK

    def bisect(i, v):
        cand = v | jnp.left_shift(jnp.int32(1), 30 - i)
        cnt = jnp.sum((a >= lax.bitcast_convert_type(cand, F32)).astype(F32), axis=1, keepdims=True)
        return jnp.where(cnt >= cap, cand, v)

    thr = lax.bitcast_convert_type(lax.fori_loop(0, 31, bisect, jnp.zeros((ne, 1), I32)), F32)
    gt = a > thr
    tie = a == thr
    need = cap - jnp.sum(gt.astype(F32), axis=1, keepdims=True)

    r = lax.broadcasted_iota(I32, (blk, blk), 0)
    c = lax.broadcasted_iota(I32, (blk, blk), 1)
    upper = (r <= c).astype(BF16)
    lane = lax.broadcasted_iota(I32, (1, LANES), 1)

    def prefix(mask_f):
        run = jnp.zeros((ne, 1), F32)
        offs = jnp.zeros((ne, LANES), F32)
        for tb in range(ntb):
            m = mask_f[:, tb * blk:(tb + 1) * blk].astype(BF16)
            loc = jnp.dot(m, upper, preferred_element_type=F32)
            cnt_ref[:, tb * blk:(tb + 1) * blk] = loc + run
            offs = jnp.where(lane == tb, run, offs)
            run = run + loc[:, blk - 1:blk]
        return jnp.where(lane >= ntb, run, offs)

    tie_f = tie.astype(F32)
    prefix(tie_f)
    tie_excl = cnt_ref[...] - tie_f
    sel = gt | (tie & (tie_excl < need))
    offs = prefix(sel.astype(F32))
    pos_ref[0] = jnp.where(sel, cnt_ref[...] - 1.0, -1.0).astype(I32)
    off_ref[0] = offs.astype(I32)


def _route(aff, cap):
    bsz, ne, l = aff.shape
    spec = lambda s: pl.BlockSpec((1,) + s, lambda b: (b, 0, 0))
    return pl.pallas_call(
        functools.partial(_route_body, cap=cap, ntb=l // ROUTE_BLK),
        grid=(bsz,),
        in_specs=[spec((ne, l))],
        out_specs=[spec((ne, l)), spec((ne, LANES))],
        out_shape=[jax.ShapeDtypeStruct((bsz, ne, l), I32),
                   jax.ShapeDtypeStruct((bsz, ne, LANES), I32)],
        scratch_shapes=[pltpu.VMEM((ne, l), F32)],
        compiler_params=_cparams(("parallel",)),
        name="route",
    )(aff)


def _window_plan(off_ref, bb, tt, experts, ne):
    lows = [off_ref[(bb * ne + e) * LANES + tt] & -16 for e in experts]
    ends = [off_ref[(bb * ne + e) * LANES + tt + 1] for e in experts]
    return lows, ends


def _window_rounds(lows, ends):
    rounds = jnp.int32(0)
    for lo, hi in zip(lows, ends):
        rounds = jnp.maximum(rounds, lax.div(hi - lo + (SLOT_WIN - 1), jnp.int32(SLOT_WIN)))
    return rounds


def _gatherx_body(off_ref, h_ref, pos_ref, xs_ref, *, ne, cap, eg):
    b = pl.program_id(0)
    g = pl.program_id(1)
    tb = pl.program_id(2)
    win = SLOT_WIN

    @pl.when(tb == 0)
    def _():
        xs_ref[...] = jnp.zeros(xs_ref.shape, BF16)

    j_col = lax.broadcasted_iota(I32, (win, 1), 0)
    t = ROUTE_BLK
    for sub in range(h_ref.shape[0] // t):
        lows, ends = _window_plan(off_ref, b, tb * (h_ref.shape[0] // t) + sub, [g * eg + k for k in range(eg)], ne)
        rounds = _window_rounds(lows, ends)

        def one_round(r, carry, lows=lows, sub=sub):
            starts = [pl.multiple_of(jnp.minimum(lows[k] + r * win, cap - win), 16) for k in range(eg)]
            pieces = []
            for k in range(eg):
                p = pos_ref[0, k:k + 1, sub * t:(sub + 1) * t]
                hit = (p - starts[k] == j_col) & (p >= lows[k] + r * win)
                pieces.append(jnp.where(hit, 1.0, 0.0).astype(BF16))
            sel = jnp.concatenate(pieces, axis=0)
            rows = jnp.dot(sel, h_ref[sub * t:(sub + 1) * t, :],
                           preferred_element_type=F32).astype(BF16)
            for k in range(eg):
                dst = (0, k, pl.ds(starts[k], win), slice(None))
                xs_ref[dst] = xs_ref[dst] + rows[k * win:(k + 1) * win]
            return carry

        lax.fori_loop(0, rounds, one_round, 0)


def _gatherx(off_flat, h2, pos, cap):
    n, d = h2.shape
    bsz, ne, l = pos.shape
    t = min(l, 4 * ROUTE_BLK)
    ntb = l // t
    eg = 8
    grid_spec = pltpu.PrefetchScalarGridSpec(
        num_scalar_prefetch=1,
        grid=(bsz, ne // eg, ntb),
        in_specs=[pl.BlockSpec((t, d), lambda b, g, i, off: (b * ntb + i, 0)),
                  pl.BlockSpec((1, eg, t), lambda b, g, i, off: (b, g, i))],
        out_specs=pl.BlockSpec((1, eg, cap, d), lambda b, g, i, off: (b, g, 0, 0)),
    )
    return pl.pallas_call(
        functools.partial(_gatherx_body, ne=ne, cap=cap, eg=eg),
        grid_spec=grid_spec,
        out_shape=jax.ShapeDtypeStruct((bsz, ne, cap, d), BF16),
        compiler_params=_cparams(("parallel", "parallel", "arbitrary")),
        name="gatherx",
    )(off_flat, h2, pos)


def _moe_body(xs_ref, wg_ref, wu_ref, wd_ref, y_ref, *, rc):
    wg = wg_ref[0].astype(BF16)
    wu = wu_ref[0].astype(BF16)
    wd = wd_ref[0].astype(BF16)
    cap = xs_ref.shape[2]
    acts = []
    for ch in range(cap // rc):
        xs = xs_ref[0, 0, ch * rc:(ch + 1) * rc, :]
        gate = jnp.dot(xs, wg, preferred_element_type=F32)
        up = jnp.dot(xs, wu, preferred_element_type=F32)
        acts.append((_silu(gate) * up).astype(BF16))
    for ch in range(cap // rc):
        y_ref[0, 0, ch * rc:(ch + 1) * rc, :] = jnp.dot(acts[ch], wd, preferred_element_type=F32).astype(BF16)


def _moe(xs, wg, wu, wd):
    bsz, ne, cap, d = xs.shape
    de = wg.shape[2]
    slot = pl.BlockSpec((1, 1, cap, d), lambda b, e: (b, e, 0, 0))
    return pl.pallas_call(
        functools.partial(_moe_body, rc=min(cap, 512)),
        grid=(bsz, ne),
        in_specs=[slot,
                  pl.BlockSpec((1, d, de), lambda b, e: (e, 0, 0)),
                  pl.BlockSpec((1, d, de), lambda b, e: (e, 0, 0)),
                  pl.BlockSpec((1, de, d), lambda b, e: (e, 0, 0))],
        out_specs=slot,
        out_shape=jax.ShapeDtypeStruct((bsz, ne, cap, d), BF16),
        compiler_params=_cparams(("parallel", "arbitrary")),
        name="moe",
    )(xs, wg, wu, wd)


def _combine_body(off_ref, x1_ref, g2_ref, fg_ref, pos_ref, aff_ref, y_hbm, o_ref, ybuf, sem,
                  *, ne, cap, ntb):
    b = pl.program_id(0)
    tb = pl.program_id(1)
    step = b * ntb + tb
    nsteps = pl.num_programs(0) * ntb
    slot = step % 2
    spare = 2
    win = SLOT_WIN

    def block_lows(bb, tt):
        return _window_plan(off_ref, bb, tt, range(ne), ne)[0]

    def window_copy(bb, e, start, buf):
        return pltpu.make_async_copy(y_hbm.at[bb, e, pl.ds(start, win), :],
                                     ybuf.at[buf, pl.ds(e * win, win), :], sem.at[buf, e])

    def window_starts(lows, r):
        return [pl.multiple_of(jnp.minimum(lows[e] + r * win, cap - win), 16) for e in range(ne)]

    def start_first_round(bb, tt, buf):
        starts = window_starts(block_lows(bb, tt), 0)
        for e in range(ne):
            window_copy(bb, e, starts[e], buf).start()

    @pl.when(step == 0)
    def _():
        start_first_round(b, tb, 0)

    lows, ends = _window_plan(off_ref, b, tb, range(ne), ne)
    rounds = _window_rounds(lows, ends)
    j_col = lax.broadcasted_iota(I32, (win, 1), 0)

    def expand(r, starts, buf):
        pieces = []
        for e in range(ne):
            p = pos_ref[0, e:e + 1, :]
            valid = p >= lows[e] + r * win
            hit = (p - starts[e] == j_col) & valid
            pieces.append(jnp.where(hit, aff_ref[0, e:e + 1, :], 0.0).astype(BF16))
        pmat = jnp.concatenate(pieces, axis=0)
        return lax.dot_general(pmat, ybuf[buf], _TN, preferred_element_type=F32)

    starts0 = window_starts(lows, 0)
    for e in range(ne):
        window_copy(b, e, starts0[e], slot).wait()
    nxt = jnp.minimum(step + 1, nsteps - 1)
    start_first_round(lax.div(nxt, jnp.int32(ntb)), lax.rem(nxt, jnp.int32(ntb)), 1 - slot)
    moe = expand(0, starts0, slot)

    def extra_round(r, acc):
        starts = window_starts(lows, r)
        for e in range(ne):
            window_copy(b, e, starts[e], spare).start()
        for e in range(ne):
            window_copy(b, e, starts[e], spare).wait()
        return acc + expand(r, starts, spare)

    moe = lax.fori_loop(1, rounds, extra_round, moe)
    x2 = x1_ref[...] + g2_ref[0] * moe
    ms = jnp.mean(x2 * x2, axis=-1, keepdims=True)
    o_ref[...] = x2 * lax.rsqrt(ms + EPS) * fg_ref[...]

    @pl.when(step == nsteps - 1)
    def _():
        for e in range(ne):
            window_copy(b, e, starts0[e], 1 - slot).wait()


def _combine(off_flat, x1, g2, fg, pos, aff, y, rows_per_sample):
    n, d = x1.shape
    bsz, ne, cap, _ = y.shape
    t = ROUTE_BLK
    ntb = rows_per_sample // t
    grid_spec = pltpu.PrefetchScalarGridSpec(
        num_scalar_prefetch=1,
        grid=(bsz, ntb),
        in_specs=[pl.BlockSpec((t, d), lambda b, i, off: (b * ntb + i, 0)),
                  pl.BlockSpec((1, 1, d), lambda b, i, off: (b, 0, 0)),
                  pl.BlockSpec((1, d), lambda b, i, off: (0, 0)),
                  pl.BlockSpec((1, ne, t), lambda b, i, off: (b, 0, i)),
                  pl.BlockSpec((1, ne, t), lambda b, i, off: (b, 0, i)),
                  pl.BlockSpec(memory_space=pl.ANY)],
        out_specs=pl.BlockSpec((t, d), lambda b, i, off: (b * ntb + i, 0)),
        scratch_shapes=[pltpu.VMEM((3, ne * SLOT_WIN, d), BF16), pltpu.SemaphoreType.DMA((3, ne))],
    )
    return pl.pallas_call(
        functools.partial(_combine_body, ne=ne, cap=cap, ntb=ntb),
        grid_spec=grid_spec,
        out_shape=jax.ShapeDtypeStruct((n, d), F32),
        compiler_params=_cparams(("arbitrary", "arbitrary")),
        name="combine",
    )(off_flat, x1, g2, fg, pos, aff, y)


def kernel(x, c, ctx, c_ctx, ada_w, ada_b, norm1_g, norm2_g, w_in, w_decay_up, b_decay, gla_norm_g,
           w_gla_proj, pool_w, pool_scale, w_pool_proj, w_out, w_router, w_gate_e, w_up_e, w_down_e,
           final_norm_g):
    assert ada_w.shape[0] == 1, "single-layer block"
    bsz, l, d = x.shape
    lc = ctx.shape[1]
    rank, dk = w_decay_up.shape[2], w_decay_up.shape[3]
    dvh = gla_norm_g.shape[1]
    dv = w_gla_proj.shape[1]
    heads = dv // dvh
    groups, ch = pool_w.shape[1], pool_w.shape[2]
    pw = groups * ch
    ne = w_router.shape[2]
    cap = EC_CAPACITY * l // ne
    assert dk // heads == 64 and dvh == LANES and ch == LANES and heads % 2 == 0
    assert l % ROUTE_BLK == 0 and l // ROUTE_BLK < LANES and cap >= SLOT_WIN and cap % 16 == 0

    cin = jnp.zeros((8, d), F32).at[:bsz].set(c).at[bsz].set(c_ctx)
    mods = _ada(cin, ada_w[0], ada_b[0][None, :])
    sh1, sc1, gt1, sh2, sc2, gt2 = [mods[:, i * d:(i + 1) * d] for i in range(6)]
    vec3 = lambda a: a[:, None, :]
    mult1 = norm1_g[0][None, :] * (1.0 + sc1)
    mult2 = norm2_g[0][None, :] * (1.0 + sc2)

    o_r = dk + dv
    o_q = o_r + 2 * rank
    o_g = o_q + dk
    o_p = o_g + dv
    o_m = o_p + pw
    up = jnp.zeros((2 * rank, 2 * dk), F32)
    up = up.at[:rank, :dk].set(w_decay_up[0, 0]).at[rank:, dk:].set(w_decay_up[0, 1])
    w1, wm = _prep(w_in, up, o_r, o_q, o_g, o_m, float(dk // heads) ** -0.5)
    bz = b_decay[0]

    zero_state = jnp.zeros((bsz, heads // 2, 2 * LANES, LANES), F32)
    cm = jnp.broadcast_to(vec3(mult1[bsz:bsz + 1]), (bsz, 1, d))
    cs = jnp.broadcast_to(vec3(sh1[bsz:bsz + 1]), (bsz, 1, d))
    ck, cv, _, claf, clab, _, _ = _inproj(ctx.reshape(bsz * lc, d), cm, cs, w1, bz, lc, dk, dv, pw, min(lc, 256))
    r3 = lambda a, n: a.reshape(bsz, n, a.shape[-1])
    _, _, h_f, h_b = _gla(r3(ck, lc), r3(ck, lc), r3(cv, lc), r3(claf, lc), r3(clab, lc),
                          zero_state, zero_state, min(lc, 256))

    x2 = x.reshape(bsz * l, d)
    tm = min(l, 512)
    k, v, q, laf, lab, sg, pin = _inproj(x2, vec3(mult1[:bsz]), vec3(sh1[:bsz]), w1, bz, l, dk, dv, pw, tm)
    of, ob, _, _ = _gla(r3(k, l), r3(q, l), r3(v, l), r3(laf, l), r3(lab, l), h_f, h_b, min(l, 512))
    pin3 = r3(pin, l)
    psc = pool_scale[0].reshape(groups, 1, ch)
    mixed = _pool(pin3, pool_w[0], psc).reshape(bsz * l, pw)

    wr_hi = w_router[0].astype(BF16)
    wr_lo = (w_router[0] - wr_hi.astype(F32)).astype(BF16)
    wr = jnp.zeros((d, LANES), BF16).at[:, :ne].set(wr_hi).at[:, ne:2 * ne].set(wr_lo)
    vecs = [vec3(mult1[:bsz]), vec3(sh1[:bsz]), vec3(gt1[:bsz]), vec3(mult2[:bsz]), vec3(sh2[:bsz])]
    x1, h2, aff = _merge(x2, vecs, of.reshape(bsz * l, dv), ob.reshape(bsz * l, dv), sg, mixed,
                          wm, w_gla_proj[0].astype(BF16), w_pool_proj[0].astype(BF16), w_out[0].astype(BF16),
                          gla_norm_g[0][None, :], wr, l, heads, ne, tm)

    pos, offs = _route(aff, cap)
    off_flat = offs.reshape(-1)
    xs = _gatherx(off_flat, h2, pos, cap)
    y = _moe(xs, w_gate_e[0], w_up_e[0], w_down_e[0])
    out = _combine(off_flat, x1, vec3(gt2[:bsz]), final_norm_g[None, :], pos, aff, y, l)
    return out.reshape(bsz, l, d)
```

```python
import functools

import jax
import jax.numpy as jnp
from jax import lax
from jax.experimental import pallas as pl
from jax.experimental.pallas import tpu as pltpu

F32 = jnp.float32
BF16 = jnp.bfloat16
I32 = jnp.int32
HIGHEST = lax.Precision.HIGHEST

EPS = 1e-6
GRID_W = 64
GLA_CHUNK = 64
GLA_STAGES = 6
GATE_NORMALIZER = 16.0
POOL_WINDOWS = (2, 4, 8, 16)
EC_CAPACITY = 2
LANES = 128
ROUTE_BLK = 256
SLOT_WIN = 64
SUB_ROWS = 256
VMEM_LIMIT = 56 * 1024 * 1024

_NT = (((1,), (1,)), ((), ()))
_TN = (((0,), (0,)), ((), ()))


def _cparams(sem):
    return pltpu.CompilerParams(dimension_semantics=sem, vmem_limit_bytes=VMEM_LIMIT)


def _silu(x):
    return x * jax.nn.sigmoid(x)


def _row_groups(n):
    step = min(n, SUB_ROWS)
    return [pl.ds(i, step) for i in range(0, n, step)]


def _staggered(gens, nstages):
    for t in range(nstages + len(gens) - 1):
        for g in reversed(range(len(gens))):
            if 0 <= t - g < nstages:
                next(gens[g])


def _ada_body(c_ref, w_ref, b_ref, o_ref):
    s = _silu(c_ref[...])
    rows = s.shape[0]
    s_hi = s.astype(BF16)
    s_lo = (s - s_hi.astype(F32)).astype(BF16)
    w = w_ref[...]
    w_hi = w.astype(BF16)
    w_lo = (w - w_hi.astype(F32)).astype(BF16)
    both = jnp.dot(jnp.concatenate([s_hi, s_lo], axis=0), w_hi, preferred_element_type=F32)
    o_ref[...] = (both[:rows] + both[rows:] + jnp.dot(s_hi, w_lo, preferred_element_type=F32)) + b_ref[...]


def _ada(cin, w, b):
    rows, d = cin.shape
    n = w.shape[1]
    nb = n // 4
    return pl.pallas_call(
        _ada_body,
        grid=(n // nb,),
        in_specs=[pl.BlockSpec((rows, d), lambda j: (0, 0)),
                  pl.BlockSpec((d, nb), lambda j: (0, j)),
                  pl.BlockSpec((1, nb), lambda j: (0, j))],
        out_specs=pl.BlockSpec((rows, nb), lambda j: (0, j)),
        out_shape=jax.ShapeDtypeStruct((rows, n), F32),
        compiler_params=_cparams(("arbitrary",)),
        name="ada",
    )(cin, w, b)


def _prep_body(w_ref, up_ref, w1_ref, wm_ref, *, o_r, o_q, o_g, o_m, qscale):
    w = w_ref[0]
    dk = o_g - o_q
    wz = jnp.dot(w[:, o_r:o_q], up_ref[...], precision=HIGHEST, preferred_element_type=F32)
    nz = wz.shape[1]
    w1_ref[:, :o_r] = w[:, :o_r].astype(BF16)
    w1_ref[:, o_r:o_r + dk] = (w[:, o_q:o_g] * qscale).astype(BF16)
    w1_ref[:, o_r + dk:o_r + dk + nz] = wz.astype(BF16)
    w1_ref[:, o_r + dk + nz:] = w[:, o_g:o_m].astype(BF16)
    wm_ref[...] = w[:, o_m:].astype(BF16)


def _prep(w_in, up, o_r, o_q, o_g, o_m, qscale):
    _, d, n = w_in.shape
    n1 = o_r + (o_g - o_q) + up.shape[1] + (o_m - o_g)
    tr = d // 4
    return pl.pallas_call(
        functools.partial(_prep_body, o_r=o_r, o_q=o_q, o_g=o_g, o_m=o_m, qscale=qscale),
        grid=(d // tr,),
        in_specs=[pl.BlockSpec((1, tr, n), lambda i: (0, i, 0)),
                  pl.BlockSpec(up.shape, lambda i: (0, 0))],
        out_specs=[pl.BlockSpec((tr, n1), lambda i: (i, 0)),
                   pl.BlockSpec((tr, n - o_m), lambda i: (i, 0))],
        out_shape=[jax.ShapeDtypeStruct((d, n1), BF16), jax.ShapeDtypeStruct((d, n - o_m), BF16)],
        compiler_params=_cparams(("parallel",)),
        name="prep",
    )(w_in, up)


def _modulated_norm(x, mult, shift):
    ms = jnp.mean(x * x, axis=-1, keepdims=True)
    return (x * lax.rsqrt(ms + EPS)) * mult + shift


def _log_sigmoid(z):
    return jnp.minimum(z, 0.0) - jnp.log1p(jnp.exp(-jnp.abs(z)))


def _inproj_body(x_ref, mult_ref, shift_ref, w_ref, bz_ref,
                 k_ref, v_ref, q_ref, laf_ref, lab_ref, sg_ref, p_ref, *, dk, dv, pw):
    def sub_tile(rows):
        h = _modulated_norm(x_ref[rows, :], mult_ref[0], shift_ref[0]).astype(BF16)
        yield
        u = jnp.dot(h, w_ref[...], preferred_element_type=F32)
        yield
        o = 0
        k_ref[rows, :] = u[:, o:o + dk].astype(BF16); o += dk
        v_ref[rows, :] = u[:, o:o + dv].astype(BF16); o += dv
        q_ref[rows, :] = u[:, o:o + dk].astype(BF16); o += dk
        zf = u[:, o:o + dk] + bz_ref[0:1, :]; o += dk
        zb = u[:, o:o + dk] + bz_ref[1:2, :]; o += dk
        laf_ref[rows, :] = _log_sigmoid(zf) * (1.0 / GATE_NORMALIZER)
        lab_ref[rows, :] = _log_sigmoid(zb) * (1.0 / GATE_NORMALIZER)
        sg_ref[rows, :] = _silu(u[:, o:o + dv]).astype(BF16); o += dv
        p_ref[rows, :] = u[:, o:o + pw].astype(BF16)
        yield

    _staggered([sub_tile(r) for r in _row_groups(x_ref.shape[0])], 3)


def _inproj(x2, mult, shift, w, bz, rows_per_sample, dk, dv, pw, tm):
    n, d = x2.shape
    tps = rows_per_sample // tm
    row = lambda i: (i, 0)
    vec = lambda i: (i // tps, 0, 0)
    outs = [(dk, BF16), (dv, BF16), (dk, BF16), (dk, F32), (dk, F32), (dv, BF16), (pw, BF16)]
    return pl.pallas_call(
        functools.partial(_inproj_body, dk=dk, dv=dv, pw=pw),
        grid=(n // tm,),
        in_specs=[pl.BlockSpec((tm, d), row),
                  pl.BlockSpec((1, 1, d), vec),
                  pl.BlockSpec((1, 1, d), vec),
                  pl.BlockSpec(w.shape, lambda i: (0, 0), pipeline_mode=pl.Buffered(1)),
                  pl.BlockSpec(bz.shape, lambda i: (0, 0))],
        out_specs=[pl.BlockSpec((tm, c), row) for c, _ in outs],
        out_shape=[jax.ShapeDtypeStruct((n, c), t) for c, t in outs],
        compiler_params=_cparams(("parallel",)),
        name="inproj",
    )(x2, mult, shift, w, bz)


def _gla_direction(k_ref, q_ref, v_ref, la_ref, o_ref, s_ref, reverse, nchunk):
    C = GLA_CHUNK
    hd = C
    cb = 4 * C
    lt = nchunk * C
    la = la_ref[0]
    r = lax.broadcasted_iota(I32, (cb, cb), 0)
    c = lax.broadcasted_iota(I32, (cb, cb), 1)
    same = (r // C) == (c // C)
    cum = jnp.where(same & ((c >= r) if reverse else (c <= r)), 1.0, 0.0).astype(BF16)
    la_hi = la.astype(BF16)
    la_lo = (la - la_hi.astype(F32)).astype(BF16)
    la2 = jnp.concatenate([la_hi, la_lo], axis=1)
    bcs = []
    for blk in range(lt // cb):
        part = jnp.dot(cum, la2[blk * cb:(blk + 1) * cb], preferred_element_type=F32)
        bcs.append(part[:, :LANES] + part[:, LANES:])
    ri = lax.broadcasted_iota(I32, (2 * C, C), 0) % C
    ci = lax.broadcasted_iota(I32, (2 * C, C), 1)
    tri = (ci >= ri) if reverse else (ci <= ri)
    lane = lax.broadcasted_iota(I32, (1, LANES), 1)
    m0 = (lane < hd).astype(F32)
    m1 = (lane >= hd).astype(F32)
    sr = lax.broadcasted_iota(I32, (2 * LANES, LANES), 0)
    sl = lax.broadcasted_iota(I32, (2 * LANES, LANES), 1)
    smask = ((sr < LANES) == (sl < hd)).astype(F32)
    kt = k_ref[0].astype(F32)
    qt = q_ref[0].astype(F32)
    zero_v = jnp.zeros((C, LANES), BF16)
    order = list(range(nchunk - 1, -1, -1) if reverse else range(nchunk))
    intra, qds, kvs, decs, q2s, kss, kws, scs = {}, {}, {}, {}, {}, {}, {}, {}
    for ch in order:
        lo = ch * C
        b = bcs[lo // cb][lo % cb:lo % cb + C]
        last = b[0:1] if reverse else b[C - 1:C]
        mid = b[C // 2:C // 2 + 1] if reverse else b[C // 2 - 1:C // 2]
        kc = kt[lo:lo + C]
        qc = qt[lo:lo + C]
        qs = qc * jnp.exp(b - mid)
        kss[ch] = (kc * jnp.exp(mid - b)).astype(BF16)
        qds[ch] = (qc * jnp.exp(b)).astype(BF16)
        kws[ch] = (kc * jnp.exp(last - b)).astype(BF16)
        decs[ch] = jnp.exp(last)
        q2s[ch] = jnp.concatenate([qs * m0, qs * m1], axis=0).astype(BF16)
    yield
    for ch in order:
        scs[ch] = lax.dot_general(q2s[ch], kss[ch], _NT, preferred_element_type=F32)
    yield
    for ch in order:
        v2 = v_ref[0, ch * C:(ch + 1) * C, :]
        kvs[ch] = lax.dot_general(v2, kws[ch], _TN, preferred_element_type=F32) * smask
    yield
    for ch in order:
        sc = jnp.where(tri, scs[ch], 0.0).astype(BF16)
        sc2 = jnp.concatenate([sc[:C], sc[C:]], axis=1)
        v2 = v_ref[0, ch * C:(ch + 1) * C, :]
        vbd = jnp.concatenate([jnp.concatenate([v2[:, :LANES], zero_v], axis=1),
                               jnp.concatenate([zero_v, v2[:, LANES:]], axis=1)], axis=0)
        intra[ch] = jnp.dot(sc2, vbd, preferred_element_type=F32)
    yield
    st = s_ref[...]
    starts = {}
    for ch in order:
        starts[ch] = st.astype(BF16)
        st = st * decs[ch] + kvs[ch]
    s_ref[...] = st
    yield
    for ch in order:
        inter = lax.dot_general(qds[ch], starts[ch], _NT, preferred_element_type=F32)
        o_ref[0, ch * C:(ch + 1) * C, :] = inter + intra[ch]
    yield


def _gla_body(kf, qf, vf, laf, kb, qb, vb, lab, h0f, h0b, of, ob, hf_out, hb_out, sf, sb, *, nchunk):
    i = pl.program_id(2)

    @pl.when(i == 0)
    def _():
        sf[...] = h0f[0, 0]
        sb[...] = h0b[0, 0]

    sweeps = [_gla_direction(kf, qf, vf, laf, of, sf, False, nchunk),
              _gla_direction(kb, qb, vb, lab, ob, sb, True, nchunk)]
    for _ in range(GLA_STAGES):
        for sweep in sweeps:
            next(sweep)

    @pl.when(i == pl.num_programs(2) - 1)
    def _():
        hf_out[0, 0] = sf[...]
        hb_out[0, 0] = sb[...]


def _gla(k, q, v, laf, lab, h0f, h0b, lt):
    bsz, l, _ = k.shape
    pairs = h0f.shape[1]
    nt = l // lt
    fwd = lambda b, hp, i: (b, i, hp)
    bwd = lambda b, hp, i: (b, nt - 1 - i, hp)
    st = lambda b, hp, i: (b, hp, 0, 0)
    kq = lambda m: pl.BlockSpec((1, lt, LANES), m)
    vv = lambda m: pl.BlockSpec((1, lt, 2 * LANES), m)
    sspec = pl.BlockSpec((1, 1, 2 * LANES, LANES), st)
    return pl.pallas_call(
        functools.partial(_gla_body, nchunk=lt // GLA_CHUNK),
        grid=(bsz, pairs, nt),
        in_specs=[kq(fwd), kq(fwd), vv(fwd), kq(fwd), kq(bwd), kq(bwd), vv(bwd), kq(bwd), sspec, sspec],
        out_specs=[vv(fwd), vv(bwd), sspec, sspec],
        out_shape=[jax.ShapeDtypeStruct(v.shape, F32), jax.ShapeDtypeStruct(v.shape, F32),
                   jax.ShapeDtypeStruct(h0f.shape, F32), jax.ShapeDtypeStruct(h0b.shape, F32)],
        scratch_shapes=[pltpu.VMEM((2 * LANES, LANES), F32), pltpu.VMEM((2 * LANES, LANES), F32)],
        compiler_params=_cparams(("parallel", "parallel", "arbitrary")),
        name="gla",
    )(k, q, v, laf, k, q, v, lab, h0f, h0b)


def _pool_body(p_ref, w_ref, sc_ref, o_ref, s1_ref, *, half, rows):
    xb = p_ref[0]
    l, ch = xb.shape
    t = lax.broadcasted_iota(I32, (l, 1), 0)
    col = t % GRID_W
    row = t // GRID_W

    def counts(pos, n):
        return (jnp.minimum(pos + half, n) - jnp.maximum(pos - half, 0)).astype(F32)

    blk = 4 * GRID_W
    r = lax.broadcasted_iota(I32, (blk, blk), 0)
    c = lax.broadcasted_iota(I32, (blk, blk), 1)
    band = ((r // GRID_W == c // GRID_W) & (c - r >= -half) & (c - r <= half - 1)).astype(BF16)
    for i in range(0, l // blk, 2):
        pair = jnp.concatenate([xb[i * blk:(i + 1) * blk], xb[(i + 1) * blk:(i + 2) * blk]], axis=1)
        sums = jnp.dot(band, pair, preferred_element_type=F32)
        s1_ref[i * blk:(i + 1) * blk, :] = sums[:, :ch]
        s1_ref[(i + 1) * blk:(i + 2) * blk, :] = sums[:, ch:]
    s1 = s1_ref[...]

    fwd = s1
    bwd = jnp.where(row >= 1, pltpu.roll(s1, GRID_W, axis=0), 0.0)
    s = 1
    while s < half:
        fwd = fwd + jnp.where(row + s <= rows - 1, pltpu.roll(fwd, l - s * GRID_W, axis=0), 0.0)
        bwd = bwd + jnp.where(row >= s, pltpu.roll(bwd, s * GRID_W, axis=0), 0.0)
        s *= 2
    pooled = (fwd + bwd) / (counts(row, rows) * counts(col, GRID_W)) - xb.astype(F32)
    mixed = jnp.dot(pooled.astype(BF16), w_ref[0].astype(BF16), preferred_element_type=F32)
    o_ref[0] = (mixed * sc_ref[0]).astype(BF16)


def _pool_groups_body(p_ref, w_ref, sc_ref, o_ref, s1_ref, *, rows):
    for gi, window in enumerate(POOL_WINDOWS):
        @pl.when(pl.program_id(1) == gi)
        def _(half=window // 2):
            _pool_body(p_ref, w_ref, sc_ref, o_ref, s1_ref, half=half, rows=rows)


def _pool(pin, pool_w, pool_scale3):
    bsz, l, pw = pin.shape
    groups, ch = pool_w.shape[0], pool_w.shape[-1]
    return pl.pallas_call(
        functools.partial(_pool_groups_body, rows=l // GRID_W),
        grid=(bsz, groups),
        in_specs=[pl.BlockSpec((1, l, ch), lambda b, g: (b, 0, g)),
                  pl.BlockSpec((1, ch, ch), lambda b, g: (g, 0, 0)),
                  pl.BlockSpec((1, 1, ch), lambda b, g: (g, 0, 0))],
        out_specs=pl.BlockSpec((1, l, ch), lambda b, g: (b, 0, g)),
        out_shape=jax.ShapeDtypeStruct((bsz, l, pw), BF16),
        scratch_shapes=[pltpu.VMEM((l, ch), F32)],
        compiler_params=_cparams(("parallel", "parallel")),
        name="pool",
    )(pin, pool_w, pool_scale3)


def _merge_body(x_ref, m1_ref, s1_ref, g1_ref, m2_ref, s2_ref,
                of_ref, ob_ref, sg_ref, mx_ref,
                wm_ref, wgla_ref, wpool_ref, wout_ref, gn_ref, wr_ref,
                x1_ref, h2_ref, aff_ref, *, heads, ne):
    d = x_ref.shape[1]

    def sub_tile(rows):
        x = x_ref[rows, :]
        h = _modulated_norm(x, m1_ref[0], s1_ref[0]).astype(BF16)
        o = of_ref[rows, :] + ob_ref[rows, :]
        sg = sg_ref[rows, :].astype(F32)
        og = []
        for j in range(heads):
            oj = o[:, j * LANES:(j + 1) * LANES]
            oj = oj * lax.rsqrt(jnp.mean(oj * oj, axis=-1, keepdims=True) + EPS) * gn_ref[...]
            og.append((oj * sg[:, j * LANES:(j + 1) * LANES]).astype(BF16))
        og = jnp.concatenate(og, axis=1)
        mixed = mx_ref[rows, :]
        yield
        gates = jnp.dot(h, wm_ref[...], preferred_element_type=F32)
        bg = jnp.dot(og, wgla_ref[...], preferred_element_type=F32)
        bp = jnp.dot(mixed, wpool_ref[...], preferred_element_type=F32)
        yield
        gates = jax.nn.sigmoid(gates)
        z = (gates[:, :d] * bg + gates[:, d:] * bp).astype(BF16)
        yield
        y = jnp.dot(z, wout_ref[...], preferred_element_type=F32)
        yield
        x1 = x + g1_ref[0] * y
        x1_ref[rows, :] = x1
        h2 = _modulated_norm(x1, m2_ref[0], s2_ref[0])
        hi = h2.astype(BF16)
        h2_ref[rows, :] = hi
        lo = (h2 - hi.astype(F32)).astype(BF16)
        yield
        lg = (jnp.dot(hi, wr_ref[...], preferred_element_type=F32)
              + jnp.dot(lo, wr_ref[...], preferred_element_type=F32))
        yield
        lgt = lg.T
        logit = lgt[0:ne] + lgt[ne:2 * ne]
        mx = jnp.max(logit, axis=0, keepdims=True)
        ex = jnp.exp(logit - mx)
        aff_ref[0, :, rows] = ex / jnp.sum(ex, axis=0, keepdims=True)
        yield

    _staggered([sub_tile(r) for r in _row_groups(x_ref.shape[0])], 7)


def _merge(x2, vecs, of, ob, sg, mixed, wm, wgla, wpool, wout, gn, wr, rows_per_sample, heads, ne, tm):
    n, d = x2.shape
    bsz = n // rows_per_sample
    tps = rows_per_sample // tm
    row = lambda i: (i, 0)
    vec = lambda i: (i // tps, 0, 0)
    full = lambda a: pl.BlockSpec(a.shape, lambda i: (0,) * a.ndim, pipeline_mode=pl.Buffered(1))
    gv = of.shape[-1]
    return pl.pallas_call(
        functools.partial(_merge_body, heads=heads, ne=ne),
        grid=(n // tm,),
        in_specs=[pl.BlockSpec((tm, d), row)] + [pl.BlockSpec((1, 1, d), vec)] * 5
                 + [pl.BlockSpec((tm, gv), row)] * 4
                 + [full(wm), full(wgla), full(wpool), full(wout), full(gn), full(wr)],
        out_specs=[pl.BlockSpec((tm, d), row),
                   pl.BlockSpec((tm, d), row),
                   pl.BlockSpec((1, ne, tm), lambda i: (i // tps, 0, i % tps))],
        out_shape=[jax.ShapeDtypeStruct((n, d), F32),
                   jax.ShapeDtypeStruct((n, d), BF16),
                   jax.ShapeDtypeStruct((bsz, ne, rows_per_sample), F32)],
        compiler_params=_cparams(("parallel",)),
        name="merge",
    )(x2, *vecs, of, ob, sg, mixed, wm, wgla, wpool, wout, gn, wr)


def _route_body(aff_ref, pos_ref, off_ref, cnt_ref, *, cap, ntb):
    a = aff_ref[0]
    ne, l = a.shape
    blk = ROUTE_BLK

    def bisect(i, v):
        cand = v | jnp.left_shift(jnp.int32(1), 30 - i)
        cnt = jnp.sum((a >= lax.bitcast_convert_type(cand, F32)).astype(F32), axis=1, keepdims=True)
        return jnp.where(cnt >= cap, cand, v)

    thr = lax.bitcast_convert_type(lax.fori_loop(0, 31, bisect, jnp.zeros((ne, 1), I32)), F32)
    gt = a > thr
    tie = a == thr
    need = cap - jnp.sum(gt.astype(F32), axis=1, keepdims=True)

    r = lax.broadcasted_iota(I32, (blk, blk), 0)
    c = lax.broadcasted_iota(I32, (blk, blk), 1)
    upper = (r <= c).astype(BF16)
    lane = lax.broadcasted_iota(I32, (1, LANES), 1)

    def prefix(mask_f):
        run = jnp.zeros((ne, 1), F32)
        offs = jnp.zeros((ne, LANES), F32)
        for tb in range(ntb):
            m = mask_f[:, tb * blk:(tb + 1) * blk].astype(BF16)
            loc = jnp.dot(m, upper, preferred_element_type=F32)
            cnt_ref[:, tb * blk:(tb + 1) * blk] = loc + run
            offs = jnp.where(lane == tb, run, offs)
            run = run + loc[:, blk - 1:blk]
        return jnp.where(lane >= ntb, run, offs)

    tie_f = tie.astype(F32)
    prefix(tie_f)
    tie_excl = cnt_ref[...] - tie_f
    sel = gt | (tie & (tie_excl < need))
    offs = prefix(sel.astype(F32))
    pos_ref[0] = jnp.where(sel, cnt_ref[...] - 1.0, -1.0).astype(I32)
    off_ref[0] = offs.astype(I32)


def _route(aff, cap):
    bsz, ne, l = aff.shape
    spec = lambda s: pl.BlockSpec((1,) + s, lambda b: (b, 0, 0))
    return pl.pallas_call(
        functools.partial(_route_body, cap=cap, ntb=l // ROUTE_BLK),
        grid=(bsz,),
        in_specs=[spec((ne, l))],
        out_specs=[spec((ne, l)), spec((ne, LANES))],
        out_shape=[jax.ShapeDtypeStruct((bsz, ne, l), I32),
                   jax.ShapeDtypeStruct((bsz, ne, LANES), I32)],
        scratch_shapes=[pltpu.VMEM((ne, l), F32)],
        compiler_params=_cparams(("parallel",)),
        name="route",
    )(aff)


def _window_plan(off_ref, bb, tt, experts, ne):
    lows = [off_ref[(bb * ne + e) * LANES + tt] & -16 for e in experts]
    ends = [off_ref[(bb * ne + e) * LANES + tt + 1] for e in experts]
    return lows, ends


def _window_rounds(lows, ends):
    rounds = jnp.int32(0)
    for lo, hi in zip(lows, ends):
        rounds = jnp.maximum(rounds, lax.div(hi - lo + (SLOT_WIN - 1), jnp.int32(SLOT_WIN)))
    return rounds


def _gatherx_body(off_ref, h_ref, pos_ref, xs_ref, *, ne, cap, eg):
    b = pl.program_id(0)
    g = pl.program_id(1)
    tb = pl.program_id(2)
    win = SLOT_WIN

    @pl.when(tb == 0)
    def _():
        xs_ref[...] = jnp.zeros(xs_ref.shape, BF16)

    j_col = lax.broadcasted_iota(I32, (win, 1), 0)
    t = ROUTE_BLK
    for sub in range(h_ref.shape[0] // t):
        lows, ends = _window_plan(off_ref, b, tb * (h_ref.shape[0] // t) + sub, [g * eg + k for k in range(eg)], ne)
        rounds = _window_rounds(lows, ends)

        def one_round(r, carry, lows=lows, sub=sub):
            starts = [pl.multiple_of(jnp.minimum(lows[k] + r * win, cap - win), 16) for k in range(eg)]
            pieces = []
            for k in range(eg):
                p = pos_ref[0, k:k + 1, sub * t:(sub + 1) * t]
                hit = (p - starts[k] == j_col) & (p >= lows[k] + r * win)
                pieces.append(jnp.where(hit, 1.0, 0.0).astype(BF16))
            sel = jnp.concatenate(pieces, axis=0)
            rows = jnp.dot(sel, h_ref[sub * t:(sub + 1) * t, :],
                           preferred_element_type=F32).astype(BF16)
            for k in range(eg):
                dst = (0, k, pl.ds(starts[k], win), slice(None))
                xs_ref[dst] = xs_ref[dst] + rows[k * win:(k + 1) * win]
            return carry

        lax.fori_loop(0, rounds, one_round, 0)


def _gatherx(off_flat, h2, pos, cap):
    n, d = h2.shape
    bsz, ne, l = pos.shape
    t = min(l, 4 * ROUTE_BLK)
    ntb = l // t
    eg = 8
    grid_spec = pltpu.PrefetchScalarGridSpec(
        num_scalar_prefetch=1,
        grid=(bsz, ne // eg, ntb),
        in_specs=[pl.BlockSpec((t, d), lambda b, g, i, off: (b * ntb + i, 0)),
                  pl.BlockSpec((1, eg, t), lambda b, g, i, off: (b, g, i))],
        out_specs=pl.BlockSpec((1, eg, cap, d), lambda b, g, i, off: (b, g, 0, 0)),
    )
    return pl.pallas_call(
        functools.partial(_gatherx_body, ne=ne, cap=cap, eg=eg),
        grid_spec=grid_spec,
        out_shape=jax.ShapeDtypeStruct((bsz, ne, cap, d), BF16),
        compiler_params=_cparams(("parallel", "parallel", "arbitrary")),
        name="gatherx",
    )(off_flat, h2, pos)


def _moe_body(xs_ref, wg_ref, wu_ref, wd_ref, y_ref, *, rc):
    wg = wg_ref[0].astype(BF16)
    wu = wu_ref[0].astype(BF16)
    wd = wd_ref[0].astype(BF16)
    cap = xs_ref.shape[2]
    acts = []
    for ch in range(cap // rc):
        xs = xs_ref[0, 0, ch * rc:(ch + 1) * rc, :]
        gate = jnp.dot(xs, wg, preferred_element_type=F32)
        up = jnp.dot(xs, wu, preferred_element_type=F32)
        acts.append((_silu(gate) * up).astype(BF16))
    for ch in range(cap // rc):
        y_ref[0, 0, ch * rc:(ch + 1) * rc, :] = jnp.dot(acts[ch], wd, preferred_element_type=F32).astype(BF16)


def _moe(xs, wg, wu, wd):
    bsz, ne, cap, d = xs.shape
    de = wg.shape[2]
    slot = pl.BlockSpec((1, 1, cap, d), lambda b, e: (b, e, 0, 0))
    return pl.pallas_call(
        functools.partial(_moe_body, rc=min(cap, 512)),
        grid=(bsz, ne),
        in_specs=[slot,
                  pl.BlockSpec((1, d, de), lambda b, e: (e, 0, 0)),
                  pl.BlockSpec((1, d, de), lambda b, e: (e, 0, 0)),
                  pl.BlockSpec((1, de, d), lambda b, e: (e, 0, 0))],
        out_specs=slot,
        out_shape=jax.ShapeDtypeStruct((bsz, ne, cap, d), BF16),
        compiler_params=_cparams(("parallel", "arbitrary")),
        name="moe",
    )(xs, wg, wu, wd)


def _combine_body(off_ref, x1_ref, g2_ref, fg_ref, pos_ref, aff_ref, y_hbm, o_ref, ybuf, sem,
                  *, ne, cap, ntb):
    b = pl.program_id(0)
    tb = pl.program_id(1)
    step = b * ntb + tb
    nsteps = pl.num_programs(0) * ntb
    slot = step % 2
    spare = 2
    win = SLOT_WIN

    def block_lows(bb, tt):
        return _window_plan(off_ref, bb, tt, range(ne), ne)[0]

    def window_copy(bb, e, start, buf):
        return pltpu.make_async_copy(y_hbm.at[bb, e, pl.ds(start, win), :],
                                     ybuf.at[buf, pl.ds(e * win, win), :], sem.at[buf, e])

    def window_starts(lows, r):
        return [pl.multiple_of(jnp.minimum(lows[e] + r * win, cap - win), 16) for e in range(ne)]

    def start_first_round(bb, tt, buf):
        starts = window_starts(block_lows(bb, tt), 0)
        for e in range(ne):
            window_copy(bb, e, starts[e], buf).start()

    @pl.when(step == 0)
    def _():
        start_first_round(b, tb, 0)

    lows, ends = _window_plan(off_ref, b, tb, range(ne), ne)
    rounds = _window_rounds(lows, ends)
    j_col = lax.broadcasted_iota(I32, (win, 1), 0)

    def expand(r, starts, buf):
        pieces = []
        for e in range(ne):
            p = pos_ref[0, e:e + 1, :]
            valid = p >= lows[e] + r * win
            hit = (p - starts[e] == j_col) & valid
            pieces.append(jnp.where(hit, aff_ref[0, e:e + 1, :], 0.0).astype(BF16))
        pmat = jnp.concatenate(pieces, axis=0)
        return lax.dot_general(pmat, ybuf[buf], _TN, preferred_element_type=F32)

    starts0 = window_starts(lows, 0)
    for e in range(ne):
        window_copy(b, e, starts0[e], slot).wait()
    moe = expand(0, starts0, slot)
    nxt = jnp.minimum(step + 1, nsteps - 1)
    start_first_round(lax.div(nxt, jnp.int32(ntb)), lax.rem(nxt, jnp.int32(ntb)), 1 - slot)

    def extra_round(r, acc):
        starts = window_starts(lows, r)
        for e in range(ne):
            window_copy(b, e, starts[e], spare).start()
        for e in range(ne):
            window_copy(b, e, starts[e], spare).wait()
        return acc + expand(r, starts, spare)

    moe = lax.fori_loop(1, rounds, extra_round, moe)
    x2 = x1_ref[...] + g2_ref[0] * moe
    ms = jnp.mean(x2 * x2, axis=-1, keepdims=True)
    o_ref[...] = x2 * lax.rsqrt(ms + EPS) * fg_ref[...]

    @pl.when(step == nsteps - 1)
    def _():
        for e in range(ne):
            window_copy(b, e, starts0[e], 1 - slot).wait()


def _combine(off_flat, x1, g2, fg, pos, aff, y, rows_per_sample):
    n, d = x1.shape
    bsz, ne, cap, _ = y.shape
    t = ROUTE_BLK
    ntb = rows_per_sample // t
    grid_spec = pltpu.PrefetchScalarGridSpec(
        num_scalar_prefetch=1,
        grid=(bsz, ntb),
        in_specs=[pl.BlockSpec((t, d), lambda b, i, off: (b * ntb + i, 0)),
                  pl.BlockSpec((1, 1, d), lambda b, i, off: (b, 0, 0)),
                  pl.BlockSpec((1, d), lambda b, i, off: (0, 0)),
                  pl.BlockSpec((1, ne, t), lambda b, i, off: (b, 0, i)),
                  pl.BlockSpec((1, ne, t), lambda b, i, off: (b, 0, i)),
                  pl.BlockSpec(memory_space=pl.ANY)],
        out_specs=pl.BlockSpec((t, d), lambda b, i, off: (b * ntb + i, 0)),
        scratch_shapes=[pltpu.VMEM((3, ne * SLOT_WIN, d), BF16), pltpu.SemaphoreType.DMA((3, ne))],
    )
    return pl.pallas_call(
        functools.partial(_combine_body, ne=ne, cap=cap, ntb=ntb),
        grid_spec=grid_spec,
        out_shape=jax.ShapeDtypeStruct((n, d), F32),
        compiler_params=_cparams(("arbitrary", "arbitrary")),
        name="combine",
    )(off_flat, x1, g2, fg, pos, aff, y)


def kernel(x, c, ctx, c_ctx, ada_w, ada_b, norm1_g, norm2_g, w_in, w_decay_up, b_decay, gla_norm_g,
           w_gla_proj, pool_w, pool_scale, w_pool_proj, w_out, w_router, w_gate_e, w_up_e, w_down_e,
           final_norm_g):
    assert ada_w.shape[0] == 1, "single-layer block"
    bsz, l, d = x.shape
    lc = ctx.shape[1]
    rank, dk = w_decay_up.shape[2], w_decay_up.shape[3]
    dvh = gla_norm_g.shape[1]
    dv = w_gla_proj.shape[1]
    heads = dv // dvh
    groups, ch = pool_w.shape[1], pool_w.shape[2]
    pw = groups * ch
    ne = w_router.shape[2]
    cap = EC_CAPACITY * l // ne
    assert dk // heads == 64 and dvh == LANES and ch == LANES and heads % 2 == 0
    assert l % ROUTE_BLK == 0 and l // ROUTE_BLK < LANES and cap >= SLOT_WIN and cap % 16 == 0

    cin = jnp.zeros((8, d), F32).at[:bsz].set(c).at[bsz].set(c_ctx)
    mods = _ada(cin, ada_w[0], ada_b[0][None, :])
    sh1, sc1, gt1, sh2, sc2, gt2 = [mods[:, i * d:(i + 1) * d] for i in range(6)]
    vec3 = lambda a: a[:, None, :]
    mult1 = norm1_g[0][None, :] * (1.0 + sc1)
    mult2 = norm2_g[0][None, :] * (1.0 + sc2)

    o_r = dk + dv
    o_q = o_r + 2 * rank
    o_g = o_q + dk
    o_p = o_g + dv
    o_m = o_p + pw
    up = jnp.zeros((2 * rank, 2 * dk), F32)
    up = up.at[:rank, :dk].set(w_decay_up[0, 0]).at[rank:, dk:].set(w_decay_up[0, 1])
    w1, wm = _prep(w_in, up, o_r, o_q, o_g, o_m, float(dk // heads) ** -0.5)
    bz = b_decay[0]

    zero_state = jnp.zeros((bsz, heads // 2, 2 * LANES, LANES), F32)
    cm = jnp.broadcast_to(vec3(mult1[bsz:bsz + 1]), (bsz, 1, d))
    cs = jnp.broadcast_to(vec3(sh1[bsz:bsz + 1]), (bsz, 1, d))
    ck, cv, _, claf, clab, _, _ = _inproj(ctx.reshape(bsz * lc, d), cm, cs, w1, bz, lc, dk, dv, pw, min(lc, 256))
    r3 = lambda a, n: a.reshape(bsz, n, a.shape[-1])
    _, _, h_f, h_b = _gla(r3(ck, lc), r3(ck, lc), r3(cv, lc), r3(claf, lc), r3(clab, lc),
                          zero_state, zero_state, min(lc, 256))

    x2 = x.reshape(bsz * l, d)
    tm = min(l, 1024)
    k, v, q, laf, lab, sg, pin = _inproj(x2, vec3(mult1[:bsz]), vec3(sh1[:bsz]), w1, bz, l, dk, dv, pw, tm)
    of, ob, _, _ = _gla(r3(k, l), r3(q, l), r3(v, l), r3(laf, l), r3(lab, l), h_f, h_b, min(l, 512))
    pin3 = r3(pin, l)
    psc = pool_scale[0].reshape(groups, 1, ch)
    mixed = _pool(pin3, pool_w[0], psc).reshape(bsz * l, pw)

    wr_hi = w_router[0].astype(BF16)
    wr_lo = (w_router[0] - wr_hi.astype(F32)).astype(BF16)
    wr = jnp.zeros((d, LANES), BF16).at[:, :ne].set(wr_hi).at[:, ne:2 * ne].set(wr_lo)
    vecs = [vec3(mult1[:bsz]), vec3(sh1[:bsz]), vec3(gt1[:bsz]), vec3(mult2[:bsz]), vec3(sh2[:bsz])]
    x1, h2, aff = _merge(x2, vecs, of.reshape(bsz * l, dv), ob.reshape(bsz * l, dv), sg, mixed,
                          wm, w_gla_proj[0].astype(BF16), w_pool_proj[0].astype(BF16), w_out[0].astype(BF16),
                          gla_norm_g[0][None, :], wr, l, heads, ne, tm)

    pos, offs = _route(aff, cap)
    off_flat = offs.reshape(-1)
    xs = _gatherx(off_flat, h2, pos, cap)
    y = _moe(xs, w_gate_e[0], w_up_e[0], w_down_e[0])
    out = _combine(off_flat, x1, vec3(gt2[:bsz]), final_norm_g[None, :], pos, aff, y, l)
    return out.reshape(bsz, l, d)
```

```python
import functools

import jax
import jax.numpy as jnp
from jax import lax
from jax.experimental import pallas as pl
from jax.experimental.pallas import tpu as pltpu

F32 = jnp.float32
BF16 = jnp.bfloat16
I32 = jnp.int32
HIGHEST = lax.Precision.HIGHEST

EPS = 1e-6
GRID_W = 64
GLA_CHUNK = 64
GLA_STAGES = 6
GATE_NORMALIZER = 16.0
POOL_WINDOWS = (2, 4, 8, 16)
EC_CAPACITY = 2
LANES = 128
ROUTE_BLK = 256
SLOT_WIN = 64
SUB_ROWS = 256
VMEM_LIMIT = 56 * 1024 * 1024

_NT = (((1,), (1,)), ((), ()))
_TN = (((0,), (0,)), ((), ()))


def _cparams(sem):
    return pltpu.CompilerParams(dimension_semantics=sem, vmem_limit_bytes=VMEM_LIMIT)


def _silu(x):
    return x * jax.nn.sigmoid(x)


def _row_groups(n):
    step = min(n, SUB_ROWS)
    return [pl.ds(i, step) for i in range(0, n, step)]


def _staggered(gens, nstages):
    for t in range(nstages + len(gens) - 1):
        for g in reversed(range(len(gens))):
            if 0 <= t - g < nstages:
                next(gens[g])


def _ada_body(c_ref, w_ref, b_ref, o_ref):
    s = _silu(c_ref[...])
    rows = s.shape[0]
    s_hi = s.astype(BF16)
    s_lo = (s - s_hi.astype(F32)).astype(BF16)
    w = w_ref[...]
    w_hi = w.astype(BF16)
    w_lo = (w - w_hi.astype(F32)).astype(BF16)
    both = jnp.dot(jnp.concatenate([s_hi, s_lo], axis=0), w_hi, preferred_element_type=F32)
    o_ref[...] = (both[:rows] + both[rows:] + jnp.dot(s_hi, w_lo, preferred_element_type=F32)) + b_ref[...]


def _ada(cin, w, b):
    rows, d = cin.shape
    n = w.shape[1]
    nb = n // 4
    return pl.pallas_call(
        _ada_body,
        grid=(n // nb,),
        in_specs=[pl.BlockSpec((rows, d), lambda j: (0, 0)),
                  pl.BlockSpec((d, nb), lambda j: (0, j)),
                  pl.BlockSpec((1, nb), lambda j: (0, j))],
        out_specs=pl.BlockSpec((rows, nb), lambda j: (0, j)),
        out_shape=jax.ShapeDtypeStruct((rows, n), F32),
        compiler_params=_cparams(("arbitrary",)),
        name="ada",
    )(cin, w, b)


def _prep_body(w_ref, up_ref, w1_ref, wm_ref, *, o_r, o_q, o_g, o_m, qscale):
    w = w_ref[0]
    dk = o_g - o_q
    wz = jnp.dot(w[:, o_r:o_q], up_ref[...], precision=HIGHEST, preferred_element_type=F32)
    nz = wz.shape[1]
    w1_ref[:, :o_r] = w[:, :o_r].astype(BF16)
    w1_ref[:, o_r:o_r + dk] = (w[:, o_q:o_g] * qscale).astype(BF16)
    w1_ref[:, o_r + dk:o_r + dk + nz] = wz.astype(BF16)
    w1_ref[:, o_r + dk + nz:] = w[:, o_g:o_m].astype(BF16)
    wm_ref[...] = w[:, o_m:].astype(BF16)


def _prep(w_in, up, o_r, o_q, o_g, o_m, qscale):
    _, d, n = w_in.shape
    n1 = o_r + (o_g - o_q) + up.shape[1] + (o_m - o_g)
    tr = d // 4
    return pl.pallas_call(
        functools.partial(_prep_body, o_r=o_r, o_q=o_q, o_g=o_g, o_m=o_m, qscale=qscale),
        grid=(d // tr,),
        in_specs=[pl.BlockSpec((1, tr, n), lambda i: (0, i, 0)),
                  pl.BlockSpec(up.shape, lambda i: (0, 0))],
        out_specs=[pl.BlockSpec((tr, n1), lambda i: (i, 0)),
                   pl.BlockSpec((tr, n - o_m), lambda i: (i, 0))],
        out_shape=[jax.ShapeDtypeStruct((d, n1), BF16), jax.ShapeDtypeStruct((d, n - o_m), BF16)],
        compiler_params=_cparams(("parallel",)),
        name="prep",
    )(w_in, up)


def _modulated_norm(x, mult, shift):
    ms = jnp.mean(x * x, axis=-1, keepdims=True)
    return (x * lax.rsqrt(ms + EPS)) * mult + shift


def _log_sigmoid(z):
    return jnp.minimum(z, 0.0) - jnp.log1p(jnp.exp(-jnp.abs(z)))


def _inproj_body(x_ref, mult_ref, shift_ref, w_ref, bz_ref,
                 k_ref, v_ref, q_ref, laf_ref, lab_ref, sg_ref, p_ref, *, dk, dv, pw):
    def sub_tile(rows):
        h = _modulated_norm(x_ref[rows, :], mult_ref[0], shift_ref[0]).astype(BF16)
        yield
        u = jnp.dot(h, w_ref[...], preferred_element_type=F32)
        yield
        o = 0
        k_ref[rows, :] = u[:, o:o + dk].astype(BF16); o += dk
        v_ref[rows, :] = u[:, o:o + dv].astype(BF16); o += dv
        q_ref[rows, :] = u[:, o:o + dk].astype(BF16); o += dk
        zf = u[:, o:o + dk] + bz_ref[0:1, :]; o += dk
        zb = u[:, o:o + dk] + bz_ref[1:2, :]; o += dk
        laf_ref[rows, :] = _log_sigmoid(zf) * (1.0 / GATE_NORMALIZER)
        lab_ref[rows, :] = _log_sigmoid(zb) * (1.0 / GATE_NORMALIZER)
        sg_ref[rows, :] = _silu(u[:, o:o + dv]).astype(BF16); o += dv
        p_ref[rows, :] = u[:, o:o + pw].astype(BF16)
        yield

    _staggered([sub_tile(r) for r in _row_groups(x_ref.shape[0])], 3)


def _inproj(x2, mult, shift, w, bz, rows_per_sample, dk, dv, pw, tm):
    n, d = x2.shape
    tps = rows_per_sample // tm
    row = lambda i: (i, 0)
    vec = lambda i: (i // tps, 0, 0)
    outs = [(dk, BF16), (dv, BF16), (dk, BF16), (dk, F32), (dk, F32), (dv, BF16), (pw, BF16)]
    return pl.pallas_call(
        functools.partial(_inproj_body, dk=dk, dv=dv, pw=pw),
        grid=(n // tm,),
        in_specs=[pl.BlockSpec((tm, d), row),
                  pl.BlockSpec((1, 1, d), vec),
                  pl.BlockSpec((1, 1, d), vec),
                  pl.BlockSpec(w.shape, lambda i: (0, 0), pipeline_mode=pl.Buffered(1)),
                  pl.BlockSpec(bz.shape, lambda i: (0, 0))],
        out_specs=[pl.BlockSpec((tm, c), row) for c, _ in outs],
        out_shape=[jax.ShapeDtypeStruct((n, c), t) for c, t in outs],
        compiler_params=_cparams(("parallel",)),
        name="inproj",
    )(x2, mult, shift, w, bz)


def _gla_direction(k_ref, q_ref, v_ref, la_ref, o_ref, s_ref, reverse, nchunk):
    C = GLA_CHUNK
    hd = C
    cb = 4 * C
    lt = nchunk * C
    la = la_ref[0]
    r = lax.broadcasted_iota(I32, (cb, cb), 0)
    c = lax.broadcasted_iota(I32, (cb, cb), 1)
    same = (r // C) == (c // C)
    cum = jnp.where(same & ((c >= r) if reverse else (c <= r)), 1.0, 0.0).astype(BF16)
    la_hi = la.astype(BF16)
    la_lo = (la - la_hi.astype(F32)).astype(BF16)
    la2 = jnp.concatenate([la_hi, la_lo], axis=1)
    bcs = []
    for blk in range(lt // cb):
        part = jnp.dot(cum, la2[blk * cb:(blk + 1) * cb], preferred_element_type=F32)
        bcs.append(part[:, :LANES] + part[:, LANES:])
    ri = lax.broadcasted_iota(I32, (2 * C, C), 0) % C
    ci = lax.broadcasted_iota(I32, (2 * C, C), 1)
    tri = (ci >= ri) if reverse else (ci <= ri)
    lane = lax.broadcasted_iota(I32, (1, LANES), 1)
    m0 = (lane < hd).astype(F32)
    m1 = (lane >= hd).astype(F32)
    sr = lax.broadcasted_iota(I32, (2 * LANES, LANES), 0)
    sl = lax.broadcasted_iota(I32, (2 * LANES, LANES), 1)
    smask = ((sr < LANES) == (sl < hd)).astype(F32)
    kt = k_ref[0].astype(F32)
    qt = q_ref[0].astype(F32)
    zero_v = jnp.zeros((C, LANES), BF16)
    order = list(range(nchunk - 1, -1, -1) if reverse else range(nchunk))
    intra, qds, kvs, decs, q2s, kss, kws, scs = {}, {}, {}, {}, {}, {}, {}, {}
    for ch in order:
        lo = ch * C
        b = bcs[lo // cb][lo % cb:lo % cb + C]
        last = b[0:1] if reverse else b[C - 1:C]
        mid = b[C // 2:C // 2 + 1] if reverse else b[C // 2 - 1:C // 2]
        kc = kt[lo:lo + C]
        qc = qt[lo:lo + C]
        qs = qc * jnp.exp(b - mid)
        kss[ch] = (kc * jnp.exp(mid - b)).astype(BF16)
        qds[ch] = (qc * jnp.exp(b)).astype(BF16)
        kws[ch] = (kc * jnp.exp(last - b)).astype(BF16)
        decs[ch] = jnp.exp(last)
        q2s[ch] = jnp.concatenate([qs * m0, qs * m1], axis=0).astype(BF16)
    yield
    for ch in order:
        scs[ch] = lax.dot_general(q2s[ch], kss[ch], _NT, preferred_element_type=F32)
    yield
    for ch in order:
        v2 = v_ref[0, ch * C:(ch + 1) * C, :]
        kvs[ch] = lax.dot_general(v2, kws[ch], _TN, preferred_element_type=F32) * smask
    yield
    for ch in order:
        sc = jnp.where(tri, scs[ch], 0.0).astype(BF16)
        sc2 = jnp.concatenate([sc[:C], sc[C:]], axis=1)
        v2 = v_ref[0, ch * C:(ch + 1) * C, :]
        vbd = jnp.concatenate([jnp.concatenate([v2[:, :LANES], zero_v], axis=1),
                               jnp.concatenate([zero_v, v2[:, LANES:]], axis=1)], axis=0)
        intra[ch] = jnp.dot(sc2, vbd, preferred_element_type=F32)
    yield
    st = s_ref[...]
    starts = {}
    for ch in order:
        starts[ch] = st.astype(BF16)
        st = st * decs[ch] + kvs[ch]
    s_ref[...] = st
    yield
    for ch in order:
        inter = lax.dot_general(qds[ch], starts[ch], _NT, preferred_element_type=F32)
        o_ref[0, ch * C:(ch + 1) * C, :] = inter + intra[ch]
    yield


def _gla_body(kf, qf, vf, laf, kb, qb, vb, lab, h0f, h0b, of, ob, hf_out, hb_out, sf, sb, *, nchunk):
    i = pl.program_id(2)

    @pl.when(i == 0)
    def _():
        sf[...] = h0f[0, 0]
        sb[...] = h0b[0, 0]

    sweeps = [_gla_direction(kf, qf, vf, laf, of, sf, False, nchunk),
              _gla_direction(kb, qb, vb, lab, ob, sb, True, nchunk)]
    for _ in range(GLA_STAGES):
        for sweep in sweeps:
            next(sweep)

    @pl.when(i == pl.num_programs(2) - 1)
    def _():
        hf_out[0, 0] = sf[...]
        hb_out[0, 0] = sb[...]


def _gla(k, q, v, laf, lab, h0f, h0b, lt):
    bsz, l, _ = k.shape
    pairs = h0f.shape[1]
    nt = l // lt
    fwd = lambda b, hp, i: (b, i, hp)
    bwd = lambda b, hp, i: (b, nt - 1 - i, hp)
    st = lambda b, hp, i: (b, hp, 0, 0)
    kq = lambda m: pl.BlockSpec((1, lt, LANES), m)
    vv = lambda m: pl.BlockSpec((1, lt, 2 * LANES), m)
    sspec = pl.BlockSpec((1, 1, 2 * LANES, LANES), st)
    return pl.pallas_call(
        functools.partial(_gla_body, nchunk=lt // GLA_CHUNK),
        grid=(bsz, pairs, nt),
        in_specs=[kq(fwd), kq(fwd), vv(fwd), kq(fwd), kq(bwd), kq(bwd), vv(bwd), kq(bwd), sspec, sspec],
        out_specs=[vv(fwd), vv(bwd), sspec, sspec],
        out_shape=[jax.ShapeDtypeStruct(v.shape, F32), jax.ShapeDtypeStruct(v.shape, F32),
                   jax.ShapeDtypeStruct(h0f.shape, F32), jax.ShapeDtypeStruct(h0b.shape, F32)],
        scratch_shapes=[pltpu.VMEM((2 * LANES, LANES), F32), pltpu.VMEM((2 * LANES, LANES), F32)],
        compiler_params=_cparams(("parallel", "parallel", "arbitrary")),
        name="gla",
    )(k, q, v, laf, k, q, v, lab, h0f, h0b)


def _pool_body(p_ref, w_ref, sc_ref, o_ref, s1_ref, *, half, rows):
    xb = p_ref[0]
    l, ch = xb.shape
    t = lax.broadcasted_iota(I32, (l, 1), 0)
    col = t % GRID_W
    row = t // GRID_W

    def counts(pos, n):
        return (jnp.minimum(pos + half, n) - jnp.maximum(pos - half, 0)).astype(F32)

    blk = 4 * GRID_W
    r = lax.broadcasted_iota(I32, (blk, blk), 0)
    c = lax.broadcasted_iota(I32, (blk, blk), 1)
    band = ((r // GRID_W == c // GRID_W) & (c - r >= -half) & (c - r <= half - 1)).astype(BF16)
    for i in range(0, l // blk, 2):
        pair = jnp.concatenate([xb[i * blk:(i + 1) * blk], xb[(i + 1) * blk:(i + 2) * blk]], axis=1)
        sums = jnp.dot(band, pair, preferred_element_type=F32)
        s1_ref[i * blk:(i + 1) * blk, :] = sums[:, :ch]
        s1_ref[(i + 1) * blk:(i + 2) * blk, :] = sums[:, ch:]
    s1 = s1_ref[...]

    fwd = s1
    bwd = jnp.where(row >= 1, pltpu.roll(s1, GRID_W, axis=0), 0.0)
    s = 1
    while s < half:
        fwd = fwd + jnp.where(row + s <= rows - 1, pltpu.roll(fwd, l - s * GRID_W, axis=0), 0.0)
        bwd = bwd + jnp.where(row >= s, pltpu.roll(bwd, s * GRID_W, axis=0), 0.0)
        s *= 2
    pooled = (fwd + bwd) / (counts(row, rows) * counts(col, GRID_W)) - xb.astype(F32)
    mixed = jnp.dot(pooled.astype(BF16), w_ref[0].astype(BF16), preferred_element_type=F32)
    o_ref[0] = (mixed * sc_ref[0]).astype(BF16)


def _pool_groups_body(p_ref, w_ref, sc_ref, o_ref, s1_ref, *, rows):
    for gi, window in enumerate(POOL_WINDOWS):
        @pl.when(pl.program_id(1) == gi)
        def _(half=window // 2):
            _pool_body(p_ref, w_ref, sc_ref, o_ref, s1_ref, half=half, rows=rows)


def _pool(pin, pool_w, pool_scale3):
    bsz, l, pw = pin.shape
    groups, ch = pool_w.shape[0], pool_w.shape[-1]
    return pl.pallas_call(
        functools.partial(_pool_groups_body, rows=l // GRID_W),
        grid=(bsz, groups),
        in_specs=[pl.BlockSpec((1, l, ch), lambda b, g: (b, 0, g)),
                  pl.BlockSpec((1, ch, ch), lambda b, g: (g, 0, 0)),
                  pl.BlockSpec((1, 1, ch), lambda b, g: (g, 0, 0))],
        out_specs=pl.BlockSpec((1, l, ch), lambda b, g: (b, 0, g)),
        out_shape=jax.ShapeDtypeStruct((bsz, l, pw), BF16),
        scratch_shapes=[pltpu.VMEM((l, ch), F32)],
        compiler_params=_cparams(("parallel", "parallel")),
        name="pool",
    )(pin, pool_w, pool_scale3)


def _merge_body(x_ref, m1_ref, s1_ref, g1_ref, m2_ref, s2_ref,
                of_ref, ob_ref, sg_ref, mx_ref,
                wm_ref, wgla_ref, wpool_ref, wout_ref, gn_ref, wr_ref,
                x1_ref, h2_ref, aff_ref, *, heads, ne):
    d = x_ref.shape[1]

    def sub_tile(rows):
        x = x_ref[rows, :]
        h = _modulated_norm(x, m1_ref[0], s1_ref[0]).astype(BF16)
        o = of_ref[rows, :] + ob_ref[rows, :]
        sg = sg_ref[rows, :].astype(F32)
        og = []
        for j in range(heads):
            oj = o[:, j * LANES:(j + 1) * LANES]
            oj = oj * lax.rsqrt(jnp.mean(oj * oj, axis=-1, keepdims=True) + EPS) * gn_ref[...]
            og.append((oj * sg[:, j * LANES:(j + 1) * LANES]).astype(BF16))
        og = jnp.concatenate(og, axis=1)
        mixed = mx_ref[rows, :]
        yield
        gates = jnp.dot(h, wm_ref[...], preferred_element_type=F32)
        bg = jnp.dot(og, wgla_ref[...], preferred_element_type=F32)
        bp = jnp.dot(mixed, wpool_ref[...], preferred_element_type=F32)
        yield
        gates = jax.nn.sigmoid(gates)
        z = (gates[:, :d] * bg + gates[:, d:] * bp).astype(BF16)
        yield
        y = jnp.dot(z, wout_ref[...], preferred_element_type=F32)
        yield
        x1 = x + g1_ref[0] * y
        x1_ref[rows, :] = x1
        h2 = _modulated_norm(x1, m2_ref[0], s2_ref[0])
        hi = h2.astype(BF16)
        h2_ref[rows, :] = hi
        lo = (h2 - hi.astype(F32)).astype(BF16)
        yield
        lg = (jnp.dot(hi, wr_ref[...], preferred_element_type=F32)
              + jnp.dot(lo, wr_ref[...], preferred_element_type=F32))
        yield
        lgt = lg.T
        logit = lgt[0:ne] + lgt[ne:2 * ne]
        mx = jnp.max(logit, axis=0, keepdims=True)
        ex = jnp.exp(logit - mx)
        aff_ref[0, :, rows] = ex / jnp.sum(ex, axis=0, keepdims=True)
        yield

    _staggered([sub_tile(r) for r in _row_groups(x_ref.shape[0])], 7)


def _merge(x2, vecs, of, ob, sg, mixed, wm, wgla, wpool, wout, gn, wr, rows_per_sample, heads, ne, tm):
    n, d = x2.shape
    bsz = n // rows_per_sample
    tps = rows_per_sample // tm
    row = lambda i: (i, 0)
    vec = lambda i: (i // tps, 0, 0)
    full = lambda a: pl.BlockSpec(a.shape, lambda i: (0,) * a.ndim, pipeline_mode=pl.Buffered(1))
    gv = of.shape[-1]
    return pl.pallas_call(
        functools.partial(_merge_body, heads=heads, ne=ne),
        grid=(n // tm,),
        in_specs=[pl.BlockSpec((tm, d), row)] + [pl.BlockSpec((1, 1, d), vec)] * 5
                 + [pl.BlockSpec((tm, gv), row)] * 4
                 + [full(wm), full(wgla), full(wpool), full(wout), full(gn), full(wr)],
        out_specs=[pl.BlockSpec((tm, d), row),
                   pl.BlockSpec((tm, d), row),
                   pl.BlockSpec((1, ne, tm), lambda i: (i // tps, 0, i % tps))],
        out_shape=[jax.ShapeDtypeStruct((n, d), F32),
                   jax.ShapeDtypeStruct((n, d), BF16),
                   jax.ShapeDtypeStruct((bsz, ne, rows_per_sample), F32)],
        compiler_params=_cparams(("parallel",)),
        name="merge",
    )(x2, *vecs, of, ob, sg, mixed, wm, wgla, wpool, wout, gn, wr)


def _route_body(aff_ref, pos_ref, off_ref, cnt_ref, *, cap, ntb):
    a = aff_ref[0]
    ne, l = a.shape
    blk = ROUTE_BLK

    def bisect(i, v):
        cand = v | jnp.left_shift(jnp.int32(1), 30 - i)
        cnt = jnp.sum((a >= lax.bitcast_convert_type(cand, F32)).astype(F32), axis=1, keepdims=True)
        return jnp.where(cnt >= cap, cand, v)

    thr = lax.bitcast_convert_type(lax.fori_loop(0, 31, bisect, jnp.zeros((ne, 1), I32)), F32)
    gt = a > thr
    tie = a == thr
    need = cap - jnp.sum(gt.astype(F32), axis=1, keepdims=True)

    r = lax.broadcasted_iota(I32, (blk, blk), 0)
    c = lax.broadcasted_iota(I32, (blk, blk), 1)
    upper = (r <= c).astype(BF16)
    lane = lax.broadcasted_iota(I32, (1, LANES), 1)

    def prefix(mask_f):
        run = jnp.zeros((ne, 1), F32)
        offs = jnp.zeros((ne, LANES), F32)
        for tb in range(ntb):
            m = mask_f[:, tb * blk:(tb + 1) * blk].astype(BF16)
            loc = jnp.dot(m, upper, preferred_element_type=F32)
            cnt_ref[:, tb * blk:(tb + 1) * blk] = loc + run
            offs = jnp.where(lane == tb, run, offs)
            run = run + loc[:, blk - 1:blk]
        return jnp.where(lane >= ntb, run, offs)

    tie_f = tie.astype(F32)
    prefix(tie_f)
    tie_excl = cnt_ref[...] - tie_f
    sel = gt | (tie & (tie_excl < need))
    offs = prefix(sel.astype(F32))
    pos_ref[0] = jnp.where(sel, cnt_ref[...] - 1.0, -1.0).astype(I32)
    off_ref[0] = offs.astype(I32)


def _route(aff, cap):
    bsz, ne, l = aff.shape
    spec = lambda s: pl.BlockSpec((1,) + s, lambda b: (b, 0, 0))
    return pl.pallas_call(
        functools.partial(_route_body, cap=cap, ntb=l // ROUTE_BLK),
        grid=(bsz,),
        in_specs=[spec((ne, l))],
        out_specs=[spec((ne, l)), spec((ne, LANES))],
        out_shape=[jax.ShapeDtypeStruct((bsz, ne, l), I32),
                   jax.ShapeDtypeStruct((bsz, ne, LANES), I32)],
        scratch_shapes=[pltpu.VMEM((ne, l), F32)],
        compiler_params=_cparams(("parallel",)),
        name="route",
    )(aff)


def _window_plan(off_ref, bb, tt, experts, ne):
    lows = [off_ref[(bb * ne + e) * LANES + tt] & -16 for e in experts]
    ends = [off_ref[(bb * ne + e) * LANES + tt + 1] for e in experts]
    return lows, ends


def _window_rounds(lows, ends):
    rounds = jnp.int32(0)
    for lo, hi in zip(lows, ends):
        rounds = jnp.maximum(rounds, lax.div(hi - lo + (SLOT_WIN - 1), jnp.int32(SLOT_WIN)))
    return rounds


def _gatherx_body(off_ref, h_ref, pos_ref, xs_ref, *, ne, cap, eg):
    b = pl.program_id(0)
    g = pl.program_id(1)
    tb = pl.program_id(2)
    win = SLOT_WIN

    @pl.when(tb == 0)
    def _():
        xs_ref[...] = jnp.zeros(xs_ref.shape, BF16)

    j_col = lax.broadcasted_iota(I32, (win, 1), 0)
    t = ROUTE_BLK
    for sub in range(h_ref.shape[0] // t):
        lows, ends = _window_plan(off_ref, b, tb * (h_ref.shape[0] // t) + sub, [g * eg + k for k in range(eg)], ne)
        rounds = _window_rounds(lows, ends)

        def one_round(r, carry, lows=lows, sub=sub):
            starts = [pl.multiple_of(jnp.minimum(lows[k] + r * win, cap - win), 16) for k in range(eg)]
            pieces = []
            for k in range(eg):
                p = pos_ref[0, k:k + 1, sub * t:(sub + 1) * t]
                hit = (p - starts[k] == j_col) & (p >= lows[k] + r * win)
                pieces.append(jnp.where(hit, 1.0, 0.0).astype(BF16))
            sel = jnp.concatenate(pieces, axis=0)
            rows = jnp.dot(sel, h_ref[sub * t:(sub + 1) * t, :],
                           preferred_element_type=F32).astype(BF16)
            for k in range(eg):
                dst = (0, k, pl.ds(starts[k], win), slice(None))
                xs_ref[dst] = xs_ref[dst] + rows[k * win:(k + 1) * win]
            return carry

        lax.fori_loop(0, rounds, one_round, 0)


def _gatherx(off_flat, h2, pos, cap):
    n, d = h2.shape
    bsz, ne, l = pos.shape
    t = min(l, 4 * ROUTE_BLK)
    ntb = l // t
    eg = 8
    grid_spec = pltpu.PrefetchScalarGridSpec(
        num_scalar_prefetch=1,
        grid=(bsz, ne // eg, ntb),
        in_specs=[pl.BlockSpec((t, d), lambda b, g, i, off: (b * ntb + i, 0)),
                  pl.BlockSpec((1, eg, t), lambda b, g, i, off: (b, g, i))],
        out_specs=pl.BlockSpec((1, eg, cap, d), lambda b, g, i, off: (b, g, 0, 0)),
    )
    return pl.pallas_call(
        functools.partial(_gatherx_body, ne=ne, cap=cap, eg=eg),
        grid_spec=grid_spec,
        out_shape=jax.ShapeDtypeStruct((bsz, ne, cap, d), BF16),
        compiler_params=_cparams(("parallel", "parallel", "arbitrary")),
        name="gatherx",
    )(off_flat, h2, pos)


def _moe_body(xs_ref, wg_ref, wu_ref, wd_ref, y_ref, *, rc):
    wg = wg_ref[0].astype(BF16)
    wu = wu_ref[0].astype(BF16)
    wd = wd_ref[0].astype(BF16)
    cap = xs_ref.shape[2]
    acts = []
    for ch in range(cap // rc):
        xs = xs_ref[0, 0, ch * rc:(ch + 1) * rc, :]
        gate = jnp.dot(xs, wg, preferred_element_type=F32)
        up = jnp.dot(xs, wu, preferred_element_type=F32)
        acts.append((_silu(gate) * up).astype(BF16))
    for ch in range(cap // rc):
        y_ref[0, 0, ch * rc:(ch + 1) * rc, :] = jnp.dot(acts[ch], wd, preferred_element_type=F32).astype(BF16)


def _moe(xs, wg, wu, wd):
    bsz, ne, cap, d = xs.shape
    de = wg.shape[2]
    slot = pl.BlockSpec((1, 1, cap, d), lambda b, e: (b, e, 0, 0))
    return pl.pallas_call(
        functools.partial(_moe_body, rc=min(cap, 512)),
        grid=(bsz, ne),
        in_specs=[slot,
                  pl.BlockSpec((1, d, de), lambda b, e: (e, 0, 0)),
                  pl.BlockSpec((1, d, de), lambda b, e: (e, 0, 0)),
                  pl.BlockSpec((1, de, d), lambda b, e: (e, 0, 0))],
        out_specs=slot,
        out_shape=jax.ShapeDtypeStruct((bsz, ne, cap, d), BF16),
        compiler_params=_cparams(("parallel", "arbitrary")),
        name="moe",
    )(xs, wg, wu, wd)


def _combine_body(off_ref, x1_ref, g2_ref, fg_ref, pos_ref, aff_ref, y_hbm, o_ref, ybuf, sem,
                  *, ne, cap, ntb):
    b = pl.program_id(0)
    tb = pl.program_id(1)
    step = b * ntb + tb
    nsteps = pl.num_programs(0) * ntb
    slot = step % 2
    spare = 2
    win = SLOT_WIN

    def block_lows(bb, tt):
        return _window_plan(off_ref, bb, tt, range(ne), ne)[0]

    def window_copy(bb, e, start, buf):
        return pltpu.make_async_copy(y_hbm.at[bb, e, pl.ds(start, win), :],
                                     ybuf.at[buf, pl.ds(e * win, win), :], sem.at[buf, e])

    def window_starts(lows, r):
        return [pl.multiple_of(jnp.minimum(lows[e] + r * win, cap - win), 16) for e in range(ne)]

    def start_first_round(bb, tt, buf):
        starts = window_starts(block_lows(bb, tt), 0)
        for e in range(ne):
            window_copy(bb, e, starts[e], buf).start()

    @pl.when(step == 0)
    def _():
        start_first_round(b, tb, 0)

    lows, ends = _window_plan(off_ref, b, tb, range(ne), ne)
    rounds = _window_rounds(lows, ends)
    j_col = lax.broadcasted_iota(I32, (win, 1), 0)

    def expand(r, starts, buf):
        pieces = []
        for e in range(ne):
            p = pos_ref[0, e:e + 1, :]
            valid = p >= lows[e] + r * win
            hit = (p - starts[e] == j_col) & valid
            pieces.append(jnp.where(hit, aff_ref[0, e:e + 1, :], 0.0).astype(BF16))
        pmat = jnp.concatenate(pieces, axis=0)
        return lax.dot_general(pmat, ybuf[buf], _TN, preferred_element_type=F32)

    nxt = jnp.minimum(step + 1, nsteps - 1)
    start_first_round(lax.div(nxt, jnp.int32(ntb)), lax.rem(nxt, jnp.int32(ntb)), 1 - slot)
    starts0 = window_starts(lows, 0)
    for e in range(ne):
        window_copy(b, e, starts0[e], slot).wait()
    moe = expand(0, starts0, slot)

    def extra_round(r, acc):
        starts = window_starts(lows, r)
        for e in range(ne):
            window_copy(b, e, starts[e], spare).start()
        for e in range(ne):
            window_copy(b, e, starts[e], spare).wait()
        return acc + expand(r, starts, spare)

    moe = lax.fori_loop(1, rounds, extra_round, moe)
    x2 = x1_ref[...] + g2_ref[0] * moe
    ms = jnp.mean(x2 * x2, axis=-1, keepdims=True)
    o_ref[...] = x2 * lax.rsqrt(ms + EPS) * fg_ref[...]

    @pl.when(step == nsteps - 1)
    def _():
        for e in range(ne):
            window_copy(b, e, starts0[e], 1 - slot).wait()


def _combine(off_flat, x1, g2, fg, pos, aff, y, rows_per_sample):
    n, d = x1.shape
    bsz, ne, cap, _ = y.shape
    t = ROUTE_BLK
    ntb = rows_per_sample // t
    grid_spec = pltpu.PrefetchScalarGridSpec(
        num_scalar_prefetch=1,
        grid=(bsz, ntb),
        in_specs=[pl.BlockSpec((t, d), lambda b, i, off: (b * ntb + i, 0)),
                  pl.BlockSpec((1, 1, d), lambda b, i, off: (b, 0, 0)),
                  pl.BlockSpec((1, d), lambda b, i, off: (0, 0)),
                  pl.BlockSpec((1, ne, t), lambda b, i, off: (b, 0, i)),
                  pl.BlockSpec((1, ne, t), lambda b, i, off: (b, 0, i)),
                  pl.BlockSpec(memory_space=pl.ANY)],
        out_specs=pl.BlockSpec((t, d), lambda b, i, off: (b * ntb + i, 0)),
        scratch_shapes=[pltpu.VMEM((3, ne * SLOT_WIN, d), BF16), pltpu.SemaphoreType.DMA((3, ne))],
    )
    return pl.pallas_call(
        functools.partial(_combine_body, ne=ne, cap=cap, ntb=ntb),
        grid_spec=grid_spec,
        out_shape=jax.ShapeDtypeStruct((n, d), F32),
        compiler_params=_cparams(("arbitrary", "arbitrary")),
        name="combine",
    )(off_flat, x1, g2, fg, pos, aff, y)


def kernel(x, c, ctx, c_ctx, ada_w, ada_b, norm1_g, norm2_g, w_in, w_decay_up, b_decay, gla_norm_g,
           w_gla_proj, pool_w, pool_scale, w_pool_proj, w_out, w_router, w_gate_e, w_up_e, w_down_e,
           final_norm_g):
    assert ada_w.shape[0] == 1, "single-layer block"
    bsz, l, d = x.shape
    lc = ctx.shape[1]
    rank, dk = w_decay_up.shape[2], w_decay_up.shape[3]
    dvh = gla_norm_g.shape[1]
    dv = w_gla_proj.shape[1]
    heads = dv // dvh
    groups, ch = pool_w.shape[1], pool_w.shape[2]
    pw = groups * ch
    ne = w_router.shape[2]
    cap = EC_CAPACITY * l // ne
    assert dk // heads == 64 and dvh == LANES and ch == LANES and heads % 2 == 0
    assert l % ROUTE_BLK == 0 and l // ROUTE_BLK < LANES and cap >= SLOT_WIN and cap % 16 == 0

    cin = jnp.zeros((8, d), F32).at[:bsz].set(c).at[bsz].set(c_ctx)
    mods = _ada(cin, ada_w[0], ada_b[0][None, :])
    sh1, sc1, gt1, sh2, sc2, gt2 = [mods[:, i * d:(i + 1) * d] for i in range(6)]
    vec3 = lambda a: a[:, None, :]
    mult1 = norm1_g[0][None, :] * (1.0 + sc1)
    mult2 = norm2_g[0][None, :] * (1.0 + sc2)

    o_r = dk + dv
    o_q = o_r + 2 * rank
    o_g = o_q + dk
    o_p = o_g + dv
    o_m = o_p + pw
    up = jnp.zeros((2 * rank, 2 * dk), F32)
    up = up.at[:rank, :dk].set(w_decay_up[0, 0]).at[rank:, dk:].set(w_decay_up[0, 1])
    w1, wm = _prep(w_in, up, o_r, o_q, o_g, o_m, float(dk // heads) ** -0.5)
    bz = b_decay[0]

    zero_state = jnp.zeros((bsz, heads // 2, 2 * LANES, LANES), F32)
    cm = jnp.broadcast_to(vec3(mult1[bsz:bsz + 1]), (bsz, 1, d))
    cs = jnp.broadcast_to(vec3(sh1[bsz:bsz + 1]), (bsz, 1, d))
    ck, cv, _, claf, clab, _, _ = _inproj(ctx.reshape(bsz * lc, d), cm, cs, w1, bz, lc, dk, dv, pw, min(lc, 256))
    r3 = lambda a, n: a.reshape(bsz, n, a.shape[-1])
    _, _, h_f, h_b = _gla(r3(ck, lc), r3(ck, lc), r3(cv, lc), r3(claf, lc), r3(clab, lc),
                          zero_state, zero_state, min(lc, 256))

    x2 = x.reshape(bsz * l, d)
    tm = min(l, 1024)
    k, v, q, laf, lab, sg, pin = _inproj(x2, vec3(mult1[:bsz]), vec3(sh1[:bsz]), w1, bz, l, dk, dv, pw, tm)
    of, ob, _, _ = _gla(r3(k, l), r3(q, l), r3(v, l), r3(laf, l), r3(lab, l), h_f, h_b, min(l, 512))
    pin3 = r3(pin, l)
    psc = pool_scale[0].reshape(groups, 1, ch)
    mixed = _pool(pin3, pool_w[0], psc).reshape(bsz * l, pw)

    wr_hi = w_router[0].astype(BF16)
    wr_lo = (w_router[0] - wr_hi.astype(F32)).astype(BF16)
    wr = jnp.zeros((d, LANES), BF16).at[:, :ne].set(wr_hi).at[:, ne:2 * ne].set(wr_lo)
    vecs = [vec3(mult1[:bsz]), vec3(sh1[:bsz]), vec3(gt1[:bsz]), vec3(mult2[:bsz]), vec3(sh2[:bsz])]
    x1, h2, aff = _merge(x2, vecs, of.reshape(bsz * l, dv), ob.reshape(bsz * l, dv), sg, mixed,
                          wm, w_gla_proj[0].astype(BF16), w_pool_proj[0].astype(BF16), w_out[0].astype(BF16),
                          gla_norm_g[0][None, :], wr, l, heads, ne, tm)

    pos, offs = _route(aff, cap)
    off_flat = offs.reshape(-1)
    xs = _gatherx(off_flat, h2, pos, cap)
    y = _moe(xs, w_gate_e[0], w_up_e[0], w_down_e[0])
    out = _combine(off_flat, x1, vec3(gt2[:bsz]), final_norm_g[None, :], pos, aff, y, l)
    return out.reshape(bsz, l, d)
```

```python
import functools

import jax
import jax.numpy as jnp
from jax import lax
from jax.experimental import pallas as pl
from jax.experimental.pallas import tpu as pltpu

F32 = jnp.float32
BF16 = jnp.bfloat16
I32 = jnp.int32
HIGHEST = lax.Precision.HIGHEST

EPS = 1e-6
GRID_W = 64
GLA_CHUNK = 64
GLA_STAGES = 6
GATE_NORMALIZER = 16.0
POOL_WINDOWS = (2, 4, 8, 16)
EC_CAPACITY = 2
LANES = 128
ROUTE_BLK = 256
SLOT_WIN = 64
SUB_ROWS = 256
VMEM_LIMIT = 56 * 1024 * 1024

_NT = (((1,), (1,)), ((), ()))
_TN = (((0,), (0,)), ((), ()))


def _cparams(sem):
    return pltpu.CompilerParams(dimension_semantics=sem, vmem_limit_bytes=VMEM_LIMIT)


def _silu(x):
    return x * jax.nn.sigmoid(x)


def _row_groups(n):
    step = min(n, SUB_ROWS)
    return [pl.ds(i, step) for i in range(0, n, step)]


def _staggered(gens, nstages):
    for t in range(nstages + len(gens) - 1):
        for g in reversed(range(len(gens))):
            if 0 <= t - g < nstages:
                next(gens[g])


def _ada_body(c_ref, w_ref, b_ref, o_ref):
    s = _silu(c_ref[...])
    rows = s.shape[0]
    s_hi = s.astype(BF16)
    s_lo = (s - s_hi.astype(F32)).astype(BF16)
    w = w_ref[...]
    w_hi = w.astype(BF16)
    w_lo = (w - w_hi.astype(F32)).astype(BF16)
    both = jnp.dot(jnp.concatenate([s_hi, s_lo], axis=0), w_hi, preferred_element_type=F32)
    o_ref[...] = (both[:rows] + both[rows:] + jnp.dot(s_hi, w_lo, preferred_element_type=F32)) + b_ref[...]


def _ada(cin, w, b):
    rows, d = cin.shape
    n = w.shape[1]
    nb = n // 4
    return pl.pallas_call(
        _ada_body,
        grid=(n // nb,),
        in_specs=[pl.BlockSpec((rows, d), lambda j: (0, 0)),
                  pl.BlockSpec((d, nb), lambda j: (0, j)),
                  pl.BlockSpec((1, nb), lambda j: (0, j))],
        out_specs=pl.BlockSpec((rows, nb), lambda j: (0, j)),
        out_shape=jax.ShapeDtypeStruct((rows, n), F32),
        compiler_params=_cparams(("arbitrary",)),
        name="ada",
    )(cin, w, b)


def _prep_body(w_ref, up_ref, w1_ref, wm_ref, *, o_r, o_q, o_g, o_m, qscale):
    w = w_ref[0]
    dk = o_g - o_q
    wz = jnp.dot(w[:, o_r:o_q], up_ref[...], precision=HIGHEST, preferred_element_type=F32)
    nz = wz.shape[1]
    w1_ref[:, :o_r] = w[:, :o_r].astype(BF16)
    w1_ref[:, o_r:o_r + dk] = (w[:, o_q:o_g] * qscale).astype(BF16)
    w1_ref[:, o_r + dk:o_r + dk + nz] = wz.astype(BF16)
    w1_ref[:, o_r + dk + nz:] = w[:, o_g:o_m].astype(BF16)
    wm_ref[...] = w[:, o_m:].astype(BF16)


def _prep(w_in, up, o_r, o_q, o_g, o_m, qscale):
    _, d, n = w_in.shape
    n1 = o_r + (o_g - o_q) + up.shape[1] + (o_m - o_g)
    tr = d // 4
    return pl.pallas_call(
        functools.partial(_prep_body, o_r=o_r, o_q=o_q, o_g=o_g, o_m=o_m, qscale=qscale),
        grid=(d // tr,),
        in_specs=[pl.BlockSpec((1, tr, n), lambda i: (0, i, 0)),
                  pl.BlockSpec(up.shape, lambda i: (0, 0))],
        out_specs=[pl.BlockSpec((tr, n1), lambda i: (i, 0)),
                   pl.BlockSpec((tr, n - o_m), lambda i: (i, 0))],
        out_shape=[jax.ShapeDtypeStruct((d, n1), BF16), jax.ShapeDtypeStruct((d, n - o_m), BF16)],
        compiler_params=_cparams(("parallel",)),
        name="prep",
    )(w_in, up)


def _modulated_norm(x, mult, shift):
    ms = jnp.mean(x * x, axis=-1, keepdims=True)
    return (x * lax.rsqrt(ms + EPS)) * mult + shift


def _log_sigmoid(z):
    return jnp.minimum(z, 0.0) - jnp.log1p(jnp.exp(-jnp.abs(z)))


def _inproj_body(x_ref, mult_ref, shift_ref, w_ref, bz_ref,
                 k_ref, v_ref, q_ref, laf_ref, lab_ref, sg_ref, p_ref, *, dk, dv, pw):
    def sub_tile(rows):
        h = _modulated_norm(x_ref[rows, :], mult_ref[0], shift_ref[0]).astype(BF16)
        yield
        u = jnp.dot(h, w_ref[...], preferred_element_type=F32)
        yield
        o = 0
        k_ref[rows, :] = u[:, o:o + dk].astype(BF16); o += dk
        v_ref[rows, :] = u[:, o:o + dv].astype(BF16); o += dv
        q_ref[rows, :] = u[:, o:o + dk].astype(BF16); o += dk
        zf = u[:, o:o + dk] + bz_ref[0:1, :]; o += dk
        zb = u[:, o:o + dk] + bz_ref[1:2, :]; o += dk
        laf_ref[rows, :] = _log_sigmoid(zf) * (1.0 / GATE_NORMALIZER)
        lab_ref[rows, :] = _log_sigmoid(zb) * (1.0 / GATE_NORMALIZER)
        sg_ref[rows, :] = _silu(u[:, o:o + dv]).astype(BF16); o += dv
        p_ref[rows, :] = u[:, o:o + pw].astype(BF16)
        yield

    _staggered([sub_tile(r) for r in _row_groups(x_ref.shape[0])], 3)


def _inproj(x2, mult, shift, w, bz, rows_per_sample, dk, dv, pw, tm):
    n, d = x2.shape
    tps = rows_per_sample // tm
    row = lambda i: (i, 0)
    vec = lambda i: (i // tps, 0, 0)
    outs = [(dk, BF16), (dv, BF16), (dk, BF16), (dk, F32), (dk, F32), (dv, BF16), (pw, BF16)]
    return pl.pallas_call(
        functools.partial(_inproj_body, dk=dk, dv=dv, pw=pw),
        grid=(n // tm,),
        in_specs=[pl.BlockSpec((tm, d), row),
                  pl.BlockSpec((1, 1, d), vec),
                  pl.BlockSpec((1, 1, d), vec),
                  pl.BlockSpec(w.shape, lambda i: (0, 0), pipeline_mode=pl.Buffered(1)),
                  pl.BlockSpec(bz.shape, lambda i: (0, 0))],
        out_specs=[pl.BlockSpec((tm, c), row) for c, _ in outs],
        out_shape=[jax.ShapeDtypeStruct((n, c), t) for c, t in outs],
        compiler_params=_cparams(("parallel",)),
        name="inproj",
    )(x2, mult, shift, w, bz)


def _gla_direction(k_ref, q_ref, v_ref, la_ref, o_ref, s_ref, reverse, nchunk):
    C = GLA_CHUNK
    hd = C
    cb = 4 * C
    lt = nchunk * C
    la = la_ref[0]
    r = lax.broadcasted_iota(I32, (cb, cb), 0)
    c = lax.broadcasted_iota(I32, (cb, cb), 1)
    same = (r // C) == (c // C)
    cum = jnp.where(same & ((c >= r) if reverse else (c <= r)), 1.0, 0.0).astype(BF16)
    la_hi = la.astype(BF16)
    la_lo = (la - la_hi.astype(F32)).astype(BF16)
    la2 = jnp.concatenate([la_hi, la_lo], axis=1)
    bcs = []
    for blk in range(lt // cb):
        part = jnp.dot(cum, la2[blk * cb:(blk + 1) * cb], preferred_element_type=F32)
        bcs.append(part[:, :LANES] + part[:, LANES:])
    ri = lax.broadcasted_iota(I32, (2 * C, C), 0) % C
    ci = lax.broadcasted_iota(I32, (2 * C, C), 1)
    tri = (ci >= ri) if reverse else (ci <= ri)
    lane = lax.broadcasted_iota(I32, (1, LANES), 1)
    m0 = (lane < hd).astype(F32)
    m1 = (lane >= hd).astype(F32)
    sr = lax.broadcasted_iota(I32, (LANES, 2 * LANES), 0)
    sl = lax.broadcasted_iota(I32, (LANES, 2 * LANES), 1)
    smask = ((sr < hd) == (sl < LANES)).astype(F32)
    kt = k_ref[0].astype(F32)
    qt = q_ref[0].astype(F32)
    zero_v = jnp.zeros((C, LANES), BF16)
    order = list(range(nchunk - 1, -1, -1) if reverse else range(nchunk))
    intra, qds, kvs, decs, q2s, kss, kws, scs = {}, {}, {}, {}, {}, {}, {}, {}
    for ch in order:
        lo = ch * C
        b = bcs[lo // cb][lo % cb:lo % cb + C]
        last = b[0:1] if reverse else b[C - 1:C]
        mid = b[C // 2:C // 2 + 1] if reverse else b[C // 2 - 1:C // 2]
        kc = kt[lo:lo + C]
        qc = qt[lo:lo + C]
        qs = qc * jnp.exp(b - mid)
        kss[ch] = (kc * jnp.exp(mid - b)).astype(BF16)
        qds[ch] = (qc * jnp.exp(b)).astype(BF16)
        kws[ch] = (kc * jnp.exp(last - b)).astype(BF16)
        decs[ch] = last
        q2s[ch] = jnp.concatenate([qs * m0, qs * m1], axis=0).astype(BF16)
    yield
    for ch in order:
        scs[ch] = lax.dot_general(q2s[ch], kss[ch], _NT, preferred_element_type=F32)
    yield
    for ch in order:
        v2 = v_ref[0, ch * C:(ch + 1) * C, :]
        kvs[ch] = lax.dot_general(kws[ch], v2, _TN, preferred_element_type=F32) * smask
    yield
    for ch in order:
        sc = jnp.where(tri, scs[ch], 0.0).astype(BF16)
        sc2 = jnp.concatenate([sc[:C], sc[C:]], axis=1)
        v2 = v_ref[0, ch * C:(ch + 1) * C, :]
        vbd = jnp.concatenate([jnp.concatenate([v2[:, :LANES], zero_v], axis=1),
                               jnp.concatenate([zero_v, v2[:, LANES:]], axis=1)], axis=0)
        intra[ch] = jnp.dot(sc2, vbd, preferred_element_type=F32)
    yield
    pad = jnp.zeros((LANES - nchunk, LANES), F32)
    dec_cols = jnp.exp(jnp.concatenate([decs[ch] for ch in range(nchunk)] + [pad], axis=0).T)
    st = s_ref[...]
    starts = {}
    for ch in order:
        starts[ch] = st.astype(BF16)
        st = st * dec_cols[:, ch:ch + 1] + kvs[ch]
    s_ref[...] = st
    yield
    for ch in order:
        inter = jnp.dot(qds[ch], starts[ch], preferred_element_type=F32)
        o_ref[0, ch * C:(ch + 1) * C, :] = inter + intra[ch]
    yield


def _gla_body(kf, qf, vf, laf, kb, qb, vb, lab, h0f, h0b, of, ob, hf_out, hb_out, sf, sb, *, nchunk):
    i = pl.program_id(2)

    @pl.when(i == 0)
    def _():
        sf[...] = h0f[0, 0]
        sb[...] = h0b[0, 0]

    sweeps = [_gla_direction(kf, qf, vf, laf, of, sf, False, nchunk),
              _gla_direction(kb, qb, vb, lab, ob, sb, True, nchunk)]
    for _ in range(GLA_STAGES):
        for sweep in sweeps:
            next(sweep)

    @pl.when(i == pl.num_programs(2) - 1)
    def _():
        hf_out[0, 0] = sf[...]
        hb_out[0, 0] = sb[...]


def _gla(k, q, v, laf, lab, h0f, h0b, lt):
    bsz, l, _ = k.shape
    pairs = h0f.shape[1]
    nt = l // lt
    fwd = lambda b, hp, i: (b, i, hp)
    bwd = lambda b, hp, i: (b, nt - 1 - i, hp)
    st = lambda b, hp, i: (b, hp, 0, 0)
    kq = lambda m: pl.BlockSpec((1, lt, LANES), m)
    vv = lambda m: pl.BlockSpec((1, lt, 2 * LANES), m)
    sspec = pl.BlockSpec((1, 1, LANES, 2 * LANES), st)
    return pl.pallas_call(
        functools.partial(_gla_body, nchunk=lt // GLA_CHUNK),
        grid=(bsz, pairs, nt),
        in_specs=[kq(fwd), kq(fwd), vv(fwd), kq(fwd), kq(bwd), kq(bwd), vv(bwd), kq(bwd), sspec, sspec],
        out_specs=[vv(fwd), vv(bwd), sspec, sspec],
        out_shape=[jax.ShapeDtypeStruct(v.shape, F32), jax.ShapeDtypeStruct(v.shape, F32),
                   jax.ShapeDtypeStruct(h0f.shape, F32), jax.ShapeDtypeStruct(h0b.shape, F32)],
        scratch_shapes=[pltpu.VMEM((LANES, 2 * LANES), F32), pltpu.VMEM((LANES, 2 * LANES), F32)],
        compiler_params=_cparams(("parallel", "parallel", "arbitrary")),
        name="gla",
    )(k, q, v, laf, k, q, v, lab, h0f, h0b)


def _pool_body(p_ref, w_ref, sc_ref, o_ref, s1_ref, *, half, rows):
    xb = p_ref[0]
    l, ch = xb.shape
    t = lax.broadcasted_iota(I32, (l, 1), 0)
    col = t % GRID_W
    row = t // GRID_W

    def counts(pos, n):
        return (jnp.minimum(pos + half, n) - jnp.maximum(pos - half, 0)).astype(F32)

    blk = 4 * GRID_W
    r = lax.broadcasted_iota(I32, (blk, blk), 0)
    c = lax.broadcasted_iota(I32, (blk, blk), 1)
    band = ((r // GRID_W == c // GRID_W) & (c - r >= -half) & (c - r <= half - 1)).astype(BF16)
    for i in range(0, l // blk, 2):
        pair = jnp.concatenate([xb[i * blk:(i + 1) * blk], xb[(i + 1) * blk:(i + 2) * blk]], axis=1)
        sums = jnp.dot(band, pair, preferred_element_type=F32)
        s1_ref[i * blk:(i + 1) * blk, :] = sums[:, :ch]
        s1_ref[(i + 1) * blk:(i + 2) * blk, :] = sums[:, ch:]
    s1 = s1_ref[...]

    fwd = s1
    bwd = jnp.where(row >= 1, pltpu.roll(s1, GRID_W, axis=0), 0.0)
    s = 1
    while s < half:
        fwd = fwd + jnp.where(row + s <= rows - 1, pltpu.roll(fwd, l - s * GRID_W, axis=0), 0.0)
        bwd = bwd + jnp.where(row >= s, pltpu.roll(bwd, s * GRID_W, axis=0), 0.0)
        s *= 2
    inv_count = 1.0 / (counts(row, rows) * counts(col, GRID_W))
    pooled = (fwd + bwd) * inv_count - xb.astype(F32)
    mixed = jnp.dot(pooled.astype(BF16), w_ref[0].astype(BF16), preferred_element_type=F32)
    o_ref[0] = (mixed * sc_ref[0]).astype(BF16)


def _pool_groups_body(p_ref, w_ref, sc_ref, o_ref, s1_ref, *, rows):
    for gi, window in enumerate(POOL_WINDOWS):
        @pl.when(pl.program_id(1) == gi)
        def _(half=window // 2):
            _pool_body(p_ref, w_ref, sc_ref, o_ref, s1_ref, half=half, rows=rows)


def _pool(pin, pool_w, pool_scale3):
    bsz, l, pw = pin.shape
    groups, ch = pool_w.shape[0], pool_w.shape[-1]
    return pl.pallas_call(
        functools.partial(_pool_groups_body, rows=l // GRID_W),
        grid=(bsz, groups),
        in_specs=[pl.BlockSpec((1, l, ch), lambda b, g: (b, 0, g)),
                  pl.BlockSpec((1, ch, ch), lambda b, g: (g, 0, 0)),
                  pl.BlockSpec((1, 1, ch), lambda b, g: (g, 0, 0))],
        out_specs=pl.BlockSpec((1, l, ch), lambda b, g: (b, 0, g)),
        out_shape=jax.ShapeDtypeStruct((bsz, l, pw), BF16),
        scratch_shapes=[pltpu.VMEM((l, ch), F32)],
        compiler_params=_cparams(("parallel", "parallel")),
        name="pool",
    )(pin, pool_w, pool_scale3)


def _merge_body(x_ref, m1_ref, s1_ref, g1_ref, m2_ref, s2_ref,
                of_ref, ob_ref, sg_ref, mx_ref,
                wm_ref, wgla_ref, wpool_ref, wout_ref, gn_ref, wr_ref,
                x1_ref, h2_ref, aff_ref, *, heads, ne):
    d = x_ref.shape[1]

    def sub_tile(rows):
        x = x_ref[rows, :]
        h = _modulated_norm(x, m1_ref[0], s1_ref[0]).astype(BF16)
        o = of_ref[rows, :] + ob_ref[rows, :]
        sg = sg_ref[rows, :].astype(F32)
        og = []
        for j in range(heads):
            oj = o[:, j * LANES:(j + 1) * LANES]
            oj = oj * lax.rsqrt(jnp.mean(oj * oj, axis=-1, keepdims=True) + EPS) * gn_ref[...]
            og.append((oj * sg[:, j * LANES:(j + 1) * LANES]).astype(BF16))
        og = jnp.concatenate(og, axis=1)
        mixed = mx_ref[rows, :]
        yield
        gates = jnp.dot(h, wm_ref[...], preferred_element_type=F32)
        bg = jnp.dot(og, wgla_ref[...], preferred_element_type=F32)
        bp = jnp.dot(mixed, wpool_ref[...], preferred_element_type=F32)
        yield
        gates = jax.nn.sigmoid(gates)
        z = (gates[:, :d] * bg + gates[:, d:] * bp).astype(BF16)
        yield
        y = jnp.dot(z, wout_ref[...], preferred_element_type=F32)
        yield
        x1 = x + g1_ref[0] * y
        x1_ref[rows, :] = x1
        h2 = _modulated_norm(x1, m2_ref[0], s2_ref[0])
        hi = h2.astype(BF16)
        h2_ref[rows, :] = hi
        lo = (h2 - hi.astype(F32)).astype(BF16)
        yield
        lg = (jnp.dot(hi, wr_ref[...], preferred_element_type=F32)
              + jnp.dot(lo, wr_ref[...], preferred_element_type=F32))
        yield
        lgt = lg.T
        logit = lgt[0:ne] + lgt[ne:2 * ne]
        mx = jnp.max(logit, axis=0, keepdims=True)
        ex = jnp.exp(logit - mx)
        aff_ref[0, :, rows] = ex / jnp.sum(ex, axis=0, keepdims=True)
        yield

    _staggered([sub_tile(r) for r in _row_groups(x_ref.shape[0])], 7)


def _merge(x2, vecs, of, ob, sg, mixed, wm, wgla, wpool, wout, gn, wr, rows_per_sample, heads, ne, tm):
    n, d = x2.shape
    bsz = n // rows_per_sample
    tps = rows_per_sample // tm
    row = lambda i: (i, 0)
    vec = lambda i: (i // tps, 0, 0)
    full = lambda a: pl.BlockSpec(a.shape, lambda i: (0,) * a.ndim, pipeline_mode=pl.Buffered(1))
    gv = of.shape[-1]
    return pl.pallas_call(
        functools.partial(_merge_body, heads=heads, ne=ne),
        grid=(n // tm,),
        in_specs=[pl.BlockSpec((tm, d), row)] + [pl.BlockSpec((1, 1, d), vec)] * 5
                 + [pl.BlockSpec((tm, gv), row)] * 4
                 + [full(wm), full(wgla), full(wpool), full(wout), full(gn), full(wr)],
        out_specs=[pl.BlockSpec((tm, d), row),
                   pl.BlockSpec((tm, d), row),
                   pl.BlockSpec((1, ne, tm), lambda i: (i // tps, 0, i % tps))],
        out_shape=[jax.ShapeDtypeStruct((n, d), F32),
                   jax.ShapeDtypeStruct((n, d), BF16),
                   jax.ShapeDtypeStruct((bsz, ne, rows_per_sample), F32)],
        compiler_params=_cparams(("parallel",)),
        name="merge",
    )(x2, *vecs, of, ob, sg, mixed, wm, wgla, wpool, wout, gn, wr)


def _route_body(aff_ref, pos_ref, off_ref, cnt_ref, *, cap, ntb):
    a = aff_ref[0]
    ne, l = a.shape
    blk = ROUTE_BLK

    def bisect(i, v):
        cand = v | jnp.left_shift(jnp.int32(1), 30 - i)
        cnt = jnp.sum((a >= lax.bitcast_convert_type(cand, F32)).astype(F32), axis=1, keepdims=True)
        return jnp.where(cnt >= cap, cand, v)

    thr = lax.bitcast_convert_type(lax.fori_loop(0, 31, bisect, jnp.zeros((ne, 1), I32)), F32)
    gt = a > thr
    tie = a == thr
    need = cap - jnp.sum(gt.astype(F32), axis=1, keepdims=True)

    r = lax.broadcasted_iota(I32, (blk, blk), 0)
    c = lax.broadcasted_iota(I32, (blk, blk), 1)
    upper = (r <= c).astype(BF16)
    lane = lax.broadcasted_iota(I32, (1, LANES), 1)

    def prefix(mask_f):
        run = jnp.zeros((ne, 1), F32)
        offs = jnp.zeros((ne, LANES), F32)
        for tb in range(ntb):
            m = mask_f[:, tb * blk:(tb + 1) * blk].astype(BF16)
            loc = jnp.dot(m, upper, preferred_element_type=F32)
            cnt_ref[:, tb * blk:(tb + 1) * blk] = loc + run
            offs = jnp.where(lane == tb, run, offs)
            run = run + loc[:, blk - 1:blk]
        return jnp.where(lane >= ntb, run, offs)

    tie_f = tie.astype(F32)
    prefix(tie_f)
    tie_excl = cnt_ref[...] - tie_f
    sel = gt | (tie & (tie_excl < need))
    offs = prefix(sel.astype(F32))
    pos_ref[0] = jnp.where(sel, cnt_ref[...] - 1.0, -1.0).astype(I32)
    off_ref[0] = offs.astype(I32)


def _route(aff, cap):
    bsz, ne, l = aff.shape
    spec = lambda s: pl.BlockSpec((1,) + s, lambda b: (b, 0, 0))
    return pl.pallas_call(
        functools.partial(_route_body, cap=cap, ntb=l // ROUTE_BLK),
        grid=(bsz,),
        in_specs=[spec((ne, l))],
        out_specs=[spec((ne, l)), spec((ne, LANES))],
        out_shape=[jax.ShapeDtypeStruct((bsz, ne, l), I32),
                   jax.ShapeDtypeStruct((bsz, ne, LANES), I32)],
        scratch_shapes=[pltpu.VMEM((ne, l), F32)],
        compiler_params=_cparams(("parallel",)),
        name="route",
    )(aff)


def _window_plan(off_ref, bb, tt, experts, ne):
    lows = [off_ref[(bb * ne + e) * LANES + tt] & -16 for e in experts]
    ends = [off_ref[(bb * ne + e) * LANES + tt + 1] for e in experts]
    return lows, ends


def _window_rounds(lows, ends):
    rounds = jnp.int32(0)
    for lo, hi in zip(lows, ends):
        rounds = jnp.maximum(rounds, lax.div(hi - lo + (SLOT_WIN - 1), jnp.int32(SLOT_WIN)))
    return rounds


def _gatherx_body(off_ref, h_ref, pos_ref, xs_ref, *, ne, cap, eg):
    b = pl.program_id(0)
    g = pl.program_id(1)
    tb = pl.program_id(2)
    win = SLOT_WIN

    @pl.when(tb == 0)
    def _():
        xs_ref[...] = jnp.zeros(xs_ref.shape, BF16)

    j_col = lax.broadcasted_iota(I32, (win, 1), 0)
    t = ROUTE_BLK
    for sub in range(h_ref.shape[0] // t):
        lows, ends = _window_plan(off_ref, b, tb * (h_ref.shape[0] // t) + sub, [g * eg + k for k in range(eg)], ne)
        rounds = _window_rounds(lows, ends)

        def one_round(r, carry, lows=lows, sub=sub):
            starts = [pl.multiple_of(jnp.minimum(lows[k] + r * win, cap - win), 16) for k in range(eg)]
            pieces = []
            for k in range(eg):
                p = pos_ref[0, k:k + 1, sub * t:(sub + 1) * t]
                hit = (p - starts[k] == j_col) & (p >= lows[k] + r * win)
                pieces.append(jnp.where(hit, 1.0, 0.0).astype(BF16))
            sel = jnp.concatenate(pieces, axis=0)
            rows = jnp.dot(sel, h_ref[sub * t:(sub + 1) * t, :],
                           preferred_element_type=F32).astype(BF16)
            for k in range(eg):
                dst = (0, k, pl.ds(starts[k], win), slice(None))
                xs_ref[dst] = xs_ref[dst] + rows[k * win:(k + 1) * win]
            return carry

        lax.fori_loop(0, rounds, one_round, 0)


def _gatherx(off_flat, h2, pos, cap):
    n, d = h2.shape
    bsz, ne, l = pos.shape
    t = min(l, 4 * ROUTE_BLK)
    ntb = l // t
    eg = 8
    grid_spec = pltpu.PrefetchScalarGridSpec(
        num_scalar_prefetch=1,
        grid=(bsz, ne // eg, ntb),
        in_specs=[pl.BlockSpec((t, d), lambda b, g, i, off: (b * ntb + i, 0)),
                  pl.BlockSpec((1, eg, t), lambda b, g, i, off: (b, g, i))],
        out_specs=pl.BlockSpec((1, eg, cap, d), lambda b, g, i, off: (b, g, 0, 0)),
    )
    return pl.pallas_call(
        functools.partial(_gatherx_body, ne=ne, cap=cap, eg=eg),
        grid_spec=grid_spec,
        out_shape=jax.ShapeDtypeStruct((bsz, ne, cap, d), BF16),
        compiler_params=_cparams(("parallel", "parallel", "arbitrary")),
        name="gatherx",
    )(off_flat, h2, pos)


def _moe_body(xs_ref, wg_ref, wu_ref, wd_ref, y_ref, *, rc):
    wg = wg_ref[0].astype(BF16)
    wu = wu_ref[0].astype(BF16)
    wd = wd_ref[0].astype(BF16)
    cap = xs_ref.shape[2]
    acts = []
    for ch in range(cap // rc):
        xs = xs_ref[0, 0, ch * rc:(ch + 1) * rc, :]
        gate = jnp.dot(xs, wg, preferred_element_type=F32)
        up = jnp.dot(xs, wu, preferred_element_type=F32)
        acts.append((_silu(gate) * up).astype(BF16))
    for ch in range(cap // rc):
        y_ref[0, 0, ch * rc:(ch + 1) * rc, :] = jnp.dot(acts[ch], wd, preferred_element_type=F32).astype(BF16)


def _moe(xs, wg, wu, wd):
    bsz, ne, cap, d = xs.shape
    de = wg.shape[2]
    slot = pl.BlockSpec((1, 1, cap, d), lambda b, e: (b, e, 0, 0))
    return pl.pallas_call(
        functools.partial(_moe_body, rc=min(cap, 512)),
        grid=(bsz, ne),
        in_specs=[slot,
                  pl.BlockSpec((1, d, de), lambda b, e: (e, 0, 0)),
                  pl.BlockSpec((1, d, de), lambda b, e: (e, 0, 0)),
                  pl.BlockSpec((1, de, d), lambda b, e: (e, 0, 0))],
        out_specs=slot,
        out_shape=jax.ShapeDtypeStruct((bsz, ne, cap, d), BF16),
        compiler_params=_cparams(("parallel", "arbitrary")),
        name="moe",
    )(xs, wg, wu, wd)


def _combine_body(off_ref, x1_ref, g2_ref, fg_ref, pos_ref, aff_ref, y_hbm, o_ref, ybuf, sem,
                  *, ne, cap, ntb):
    b = pl.program_id(0)
    tb = pl.program_id(1)
    step = b * ntb + tb
    nsteps = pl.num_programs(0) * ntb
    slot = step % 2
    spare = 2
    win = SLOT_WIN

    def block_lows(bb, tt):
        return _window_plan(off_ref, bb, tt, range(ne), ne)[0]

    def window_copy(bb, e, start, buf):
        return pltpu.make_async_copy(y_hbm.at[bb, e, pl.ds(start, win), :],
                                     ybuf.at[buf, pl.ds(e * win, win), :], sem.at[buf, e])

    def window_starts(lows, r):
        return [pl.multiple_of(jnp.minimum(lows[e] + r * win, cap - win), 16) for e in range(ne)]

    def start_first_round(bb, tt, buf):
        starts = window_starts(block_lows(bb, tt), 0)
        for e in range(ne):
            window_copy(bb, e, starts[e], buf).start()

    @pl.when(step == 0)
    def _():
        start_first_round(b, tb, 0)

    lows, ends = _window_plan(off_ref, b, tb, range(ne), ne)
    rounds = _window_rounds(lows, ends)
    j_col = lax.broadcasted_iota(I32, (win, 1), 0)

    def expand(r, starts, buf):
        pieces = []
        for e in range(ne):
            p = pos_ref[0, e:e + 1, :]
            valid = p >= lows[e] + r * win
            hit = (p - starts[e] == j_col) & valid
            pieces.append(jnp.where(hit, aff_ref[0, e:e + 1, :], 0.0).astype(BF16))
        pmat = jnp.concatenate(pieces, axis=0)
        return lax.dot_general(pmat, ybuf[buf], _TN, preferred_element_type=F32)

    nxt = jnp.minimum(step + 1, nsteps - 1)
    start_first_round(lax.div(nxt, jnp.int32(ntb)), lax.rem(nxt, jnp.int32(ntb)), 1 - slot)
    starts0 = window_starts(lows, 0)
    for e in range(ne):
        window_copy(b, e, starts0[e], slot).wait()
    moe = expand(0, starts0, slot)

    def extra_round(r, acc):
        starts = window_starts(lows, r)
        for e in range(ne):
            window_copy(b, e, starts[e], spare).start()
        for e in range(ne):
            window_copy(b, e, starts[e], spare).wait()
        return acc + expand(r, starts, spare)

    moe = lax.fori_loop(1, rounds, extra_round, moe)
    x2 = x1_ref[...] + g2_ref[0] * moe
    ms = jnp.mean(x2 * x2, axis=-1, keepdims=True)
    o_ref[...] = x2 * lax.rsqrt(ms + EPS) * fg_ref[...]

    @pl.when(step == nsteps - 1)
    def _():
        for e in range(ne):
            window_copy(b, e, starts0[e], 1 - slot).wait()


def _combine(off_flat, x1, g2, fg, pos, aff, y, rows_per_sample):
    n, d = x1.shape
    bsz, ne, cap, _ = y.shape
    t = ROUTE_BLK
    ntb = rows_per_sample // t
    grid_spec = pltpu.PrefetchScalarGridSpec(
        num_scalar_prefetch=1,
        grid=(bsz, ntb),
        in_specs=[pl.BlockSpec((t, d), lambda b, i, off: (b * ntb + i, 0)),
                  pl.BlockSpec((1, 1, d), lambda b, i, off: (b, 0, 0)),
                  pl.BlockSpec((1, d), lambda b, i, off: (0, 0)),
                  pl.BlockSpec((1, ne, t), lambda b, i, off: (b, 0, i)),
                  pl.BlockSpec((1, ne, t), lambda b, i, off: (b, 0, i)),
                  pl.BlockSpec(memory_space=pl.ANY)],
        out_specs=pl.BlockSpec((t, d), lambda b, i, off: (b * ntb + i, 0)),
        scratch_shapes=[pltpu.VMEM((3, ne * SLOT_WIN, d), BF16), pltpu.SemaphoreType.DMA((3, ne))],
    )
    return pl.pallas_call(
        functools.partial(_combine_body, ne=ne, cap=cap, ntb=ntb),
        grid_spec=grid_spec,
        out_shape=jax.ShapeDtypeStruct((n, d), F32),
        compiler_params=_cparams(("arbitrary", "arbitrary")),
        name="combine",
    )(off_flat, x1, g2, fg, pos, aff, y)


def kernel(x, c, ctx, c_ctx, ada_w, ada_b, norm1_g, norm2_g, w_in, w_decay_up, b_decay, gla_norm_g,
           w_gla_proj, pool_w, pool_scale, w_pool_proj, w_out, w_router, w_gate_e, w_up_e, w_down_e,
           final_norm_g):
    assert ada_w.shape[0] == 1, "single-layer block"
    bsz, l, d = x.shape
    lc = ctx.shape[1]
    rank, dk = w_decay_up.shape[2], w_decay_up.shape[3]
    dvh = gla_norm_g.shape[1]
    dv = w_gla_proj.shape[1]
    heads = dv // dvh
    groups, ch = pool_w.shape[1], pool_w.shape[2]
    pw = groups * ch
    ne = w_router.shape[2]
    cap = EC_CAPACITY * l // ne
    assert dk // heads == 64 and dvh == LANES and ch == LANES and heads % 2 == 0
    assert l % ROUTE_BLK == 0 and l // ROUTE_BLK < LANES and cap >= SLOT_WIN and cap % 16 == 0

    cin = jnp.zeros((8, d), F32).at[:bsz].set(c).at[bsz].set(c_ctx)
    mods = _ada(cin, ada_w[0], ada_b[0][None, :])
    sh1, sc1, gt1, sh2, sc2, gt2 = [mods[:, i * d:(i + 1) * d] for i in range(6)]
    vec3 = lambda a: a[:, None, :]
    mult1 = norm1_g[0][None, :] * (1.0 + sc1)
    mult2 = norm2_g[0][None, :] * (1.0 + sc2)

    o_r = dk + dv
    o_q = o_r + 2 * rank
    o_g = o_q + dk
    o_p = o_g + dv
    o_m = o_p + pw
    up = jnp.zeros((2 * rank, 2 * dk), F32)
    up = up.at[:rank, :dk].set(w_decay_up[0, 0]).at[rank:, dk:].set(w_decay_up[0, 1])
    w1, wm = _prep(w_in, up, o_r, o_q, o_g, o_m, float(dk // heads) ** -0.5)
    bz = b_decay[0]

    zero_state = jnp.zeros((bsz, heads // 2, LANES, 2 * LANES), F32)
    cm = jnp.broadcast_to(vec3(mult1[bsz:bsz + 1]), (bsz, 1, d))
    cs = jnp.broadcast_to(vec3(sh1[bsz:bsz + 1]), (bsz, 1, d))
    ck, cv, _, claf, clab, _, _ = _inproj(ctx.reshape(bsz * lc, d), cm, cs, w1, bz, lc, dk, dv, pw, min(lc, 256))
    r3 = lambda a, n: a.reshape(bsz, n, a.shape[-1])
    _, _, h_f, h_b = _gla(r3(ck, lc), r3(ck, lc), r3(cv, lc), r3(claf, lc), r3(clab, lc),
                          zero_state, zero_state, min(lc, 256))

    x2 = x.reshape(bsz * l, d)
    tm = min(l, 1024)
    k, v, q, laf, lab, sg, pin = _inproj(x2, vec3(mult1[:bsz]), vec3(sh1[:bsz]), w1, bz, l, dk, dv, pw, tm)
    of, ob, _, _ = _gla(r3(k, l), r3(q, l), r3(v, l), r3(laf, l), r3(lab, l), h_f, h_b, min(l, 1024))
    pin3 = r3(pin, l)
    psc = pool_scale[0].reshape(groups, 1, ch)
    mixed = _pool(pin3, pool_w[0], psc).reshape(bsz * l, pw)

    wr_hi = w_router[0].astype(BF16)
    wr_lo = (w_router[0] - wr_hi.astype(F32)).astype(BF16)
    wr = jnp.zeros((d, LANES), BF16).at[:, :ne].set(wr_hi).at[:, ne:2 * ne].set(wr_lo)
    vecs = [vec3(mult1[:bsz]), vec3(sh1[:bsz]), vec3(gt1[:bsz]), vec3(mult2[:bsz]), vec3(sh2[:bsz])]
    x1, h2, aff = _merge(x2, vecs, of.reshape(bsz * l, dv), ob.reshape(bsz * l, dv), sg, mixed,
                          wm, w_gla_proj[0].astype(BF16), w_pool_proj[0].astype(BF16), w_out[0].astype(BF16),
                          gla_norm_g[0][None, :], wr, l, heads, ne, tm)

    pos, offs = _route(aff, cap)
    off_flat = offs.reshape(-1)
    xs = _gatherx(off_flat, h2, pos, cap)
    y = _moe(xs, w_gate_e[0], w_up_e[0], w_down_e[0])
    out = _combine(off_flat, x1, vec3(gt2[:bsz]), final_norm_g[None, :], pos, aff, y, l)
    return out.reshape(bsz, l, d)
```

```python
import functools

import jax
import jax.numpy as jnp
from jax import lax
from jax.experimental import pallas as pl
from jax.experimental.pallas import tpu as pltpu

F32 = jnp.float32
BF16 = jnp.bfloat16
I32 = jnp.int32
HIGHEST = lax.Precision.HIGHEST

EPS = 1e-6
GRID_W = 64
GLA_CHUNK = 64
GLA_STAGES = 6
GATE_NORMALIZER = 16.0
POOL_WINDOWS = (2, 4, 8, 16)
EC_CAPACITY = 2
LANES = 128
ROUTE_BLK = 256
SLOT_WIN = 64
SUB_ROWS = 256
VMEM_LIMIT = 56 * 1024 * 1024

_NT = (((1,), (1,)), ((), ()))
_TN = (((0,), (0,)), ((), ()))


def _cparams(sem):
    return pltpu.CompilerParams(dimension_semantics=sem, vmem_limit_bytes=VMEM_LIMIT)


def _silu(x):
    return x * jax.nn.sigmoid(x)


def _row_groups(n):
    step = min(n, SUB_ROWS)
    return [pl.ds(i, step) for i in range(0, n, step)]


def _staggered(gens, nstages):
    for t in range(nstages + len(gens) - 1):
        for g in reversed(range(len(gens))):
            if 0 <= t - g < nstages:
                next(gens[g])


def _ada_body(c_ref, w_ref, b_ref, o_ref):
    s = _silu(c_ref[...])
    rows = s.shape[0]
    s_hi = s.astype(BF16)
    s_lo = (s - s_hi.astype(F32)).astype(BF16)
    w = w_ref[...]
    w_hi = w.astype(BF16)
    w_lo = (w - w_hi.astype(F32)).astype(BF16)
    both = jnp.dot(jnp.concatenate([s_hi, s_lo], axis=0), w_hi, preferred_element_type=F32)
    o_ref[...] = (both[:rows] + both[rows:] + jnp.dot(s_hi, w_lo, preferred_element_type=F32)) + b_ref[...]


def _ada(cin, w, b):
    rows, d = cin.shape
    n = w.shape[1]
    nb = n // 4
    return pl.pallas_call(
        _ada_body,
        grid=(n // nb,),
        in_specs=[pl.BlockSpec((rows, d), lambda j: (0, 0)),
                  pl.BlockSpec((d, nb), lambda j: (0, j)),
                  pl.BlockSpec((1, nb), lambda j: (0, j))],
        out_specs=pl.BlockSpec((rows, nb), lambda j: (0, j)),
        out_shape=jax.ShapeDtypeStruct((rows, n), F32),
        compiler_params=_cparams(("arbitrary",)),
        name="ada",
    )(cin, w, b)


def _prep_body(w_ref, up_ref, pw_ref, ps_ref, wpp_ref, w1_ref, wm_ref, wpf_ref, *, o_r, o_q, o_g, o_m, qscale):
    w = w_ref[0]
    dk = o_g - o_q
    wz = jnp.dot(w[:, o_r:o_q], up_ref[...], precision=HIGHEST, preferred_element_type=F32)
    nz = wz.shape[1]
    w1_ref[:, :o_r] = w[:, :o_r].astype(BF16)
    w1_ref[:, o_r:o_r + dk] = (w[:, o_q:o_g] * qscale).astype(BF16)
    w1_ref[:, o_r + dk:o_r + dk + nz] = wz.astype(BF16)
    w1_ref[:, o_r + dk + nz:] = w[:, o_g:o_m].astype(BF16)
    wm_ref[...] = w[:, o_m:].astype(BF16)
    wpf_ref[...] = jnp.dot(pw_ref[0] * ps_ref[0], wpp_ref[...], precision=HIGHEST,
                           preferred_element_type=F32).astype(BF16)


def _prep(w_in, up, pool_w, pool_scale3, w_pool_proj, o_r, o_q, o_g, o_m, qscale):
    _, d, n = w_in.shape
    n1 = o_r + (o_g - o_q) + up.shape[1] + (o_m - o_g)
    groups, ch, _ = pool_w.shape
    tr = d // groups
    dm = w_pool_proj.shape[1]
    return pl.pallas_call(
        functools.partial(_prep_body, o_r=o_r, o_q=o_q, o_g=o_g, o_m=o_m, qscale=qscale),
        grid=(d // tr,),
        in_specs=[pl.BlockSpec((1, tr, n), lambda i: (0, i, 0)),
                  pl.BlockSpec(up.shape, lambda i: (0, 0)),
                  pl.BlockSpec((1, ch, ch), lambda i: (i, 0, 0)),
                  pl.BlockSpec((1, 1, ch), lambda i: (i, 0, 0)),
                  pl.BlockSpec((ch, dm), lambda i: (i, 0))],
        out_specs=[pl.BlockSpec((tr, n1), lambda i: (i, 0)),
                   pl.BlockSpec((tr, n - o_m), lambda i: (i, 0)),
                   pl.BlockSpec((ch, dm), lambda i: (i, 0))],
        out_shape=[jax.ShapeDtypeStruct((d, n1), BF16), jax.ShapeDtypeStruct((d, n - o_m), BF16),
                   jax.ShapeDtypeStruct((groups * ch, dm), BF16)],
        compiler_params=_cparams(("parallel",)),
        name="prep",
    )(w_in, up, pool_w, pool_scale3, w_pool_proj)


def _modulated_norm(x, mult, shift):
    ms = jnp.mean(x * x, axis=-1, keepdims=True)
    return (x * lax.rsqrt(ms + EPS)) * mult + shift


def _log_sigmoid(z):
    return jnp.minimum(z, 0.0) - jnp.log1p(jnp.exp(-jnp.abs(z)))


def _inproj_body(x_ref, mult_ref, shift_ref, w_ref, bz_ref,
                 k_ref, v_ref, q_ref, laf_ref, lab_ref, sg_ref, p_ref, *, dk, dv, pw):
    def sub_tile(rows):
        h = _modulated_norm(x_ref[rows, :], mult_ref[0], shift_ref[0]).astype(BF16)
        yield
        u = jnp.dot(h, w_ref[...], preferred_element_type=F32)
        yield
        o = 0
        k_ref[rows, :] = u[:, o:o + dk].astype(BF16); o += dk
        v_ref[rows, :] = u[:, o:o + dv].astype(BF16); o += dv
        q_ref[rows, :] = u[:, o:o + dk].astype(BF16); o += dk
        zf = u[:, o:o + dk] + bz_ref[0:1, :]; o += dk
        zb = u[:, o:o + dk] + bz_ref[1:2, :]; o += dk
        laf_ref[rows, :] = _log_sigmoid(zf) * (1.0 / GATE_NORMALIZER)
        lab_ref[rows, :] = _log_sigmoid(zb) * (1.0 / GATE_NORMALIZER)
        sg_ref[rows, :] = _silu(u[:, o:o + dv]).astype(BF16); o += dv
        p_ref[rows, :] = u[:, o:o + pw].astype(BF16)
        yield

    _staggered([sub_tile(r) for r in _row_groups(x_ref.shape[0])], 3)


def _inproj(x2, mult, shift, w, bz, rows_per_sample, dk, dv, pw, tm):
    n, d = x2.shape
    tps = rows_per_sample // tm
    row = lambda i: (i, 0)
    vec = lambda i: (i // tps, 0, 0)
    outs = [(dk, BF16), (dv, BF16), (dk, BF16), (dk, F32), (dk, F32), (dv, BF16), (pw, BF16)]
    return pl.pallas_call(
        functools.partial(_inproj_body, dk=dk, dv=dv, pw=pw),
        grid=(n // tm,),
        in_specs=[pl.BlockSpec((tm, d), row),
                  pl.BlockSpec((1, 1, d), vec),
                  pl.BlockSpec((1, 1, d), vec),
                  pl.BlockSpec(w.shape, lambda i: (0, 0), pipeline_mode=pl.Buffered(1)),
                  pl.BlockSpec(bz.shape, lambda i: (0, 0))],
        out_specs=[pl.BlockSpec((tm, c), row) for c, _ in outs],
        out_shape=[jax.ShapeDtypeStruct((n, c), t) for c, t in outs],
        compiler_params=_cparams(("parallel",)),
        name="inproj",
    )(x2, mult, shift, w, bz)


def _gla_direction(k_ref, q_ref, v_ref, la_ref, o_ref, s_ref, reverse, nchunk):
    C = GLA_CHUNK
    hd = C
    cb = 4 * C
    lt = nchunk * C
    la = la_ref[0]
    r = lax.broadcasted_iota(I32, (cb, cb), 0)
    c = lax.broadcasted_iota(I32, (cb, cb), 1)
    same = (r // C) == (c // C)
    cum = jnp.where(same & ((c >= r) if reverse else (c <= r)), 1.0, 0.0).astype(BF16)
    la_hi = la.astype(BF16)
    la_lo = (la - la_hi.astype(F32)).astype(BF16)
    la2 = jnp.concatenate([la_hi, la_lo], axis=1)
    bcs = []
    for blk in range(lt // cb):
        part = jnp.dot(cum, la2[blk * cb:(blk + 1) * cb], preferred_element_type=F32)
        bcs.append(part[:, :LANES] + part[:, LANES:])
    ri = lax.broadcasted_iota(I32, (2 * C, C), 0) % C
    ci = lax.broadcasted_iota(I32, (2 * C, C), 1)
    tri = (ci >= ri) if reverse else (ci <= ri)
    lane = lax.broadcasted_iota(I32, (1, LANES), 1)
    m0 = (lane < hd).astype(F32)
    m1 = (lane >= hd).astype(F32)
    sr = lax.broadcasted_iota(I32, (LANES, 2 * LANES), 0)
    sl = lax.broadcasted_iota(I32, (LANES, 2 * LANES), 1)
    smask = ((sr < hd) == (sl < LANES)).astype(F32)
    kt = k_ref[0].astype(F32)
    qt = q_ref[0].astype(F32)
    zero_v = jnp.zeros((C, LANES), BF16)
    order = list(range(nchunk - 1, -1, -1) if reverse else range(nchunk))
    intra, qds, kvs, decs, q2s, kss, kws, scs = {}, {}, {}, {}, {}, {}, {}, {}
    for ch in order:
        lo = ch * C
        b = bcs[lo // cb][lo % cb:lo % cb + C]
        last = b[0:1] if reverse else b[C - 1:C]
        mid = b[C // 2:C // 2 + 1] if reverse else b[C // 2 - 1:C // 2]
        kc = kt[lo:lo + C]
        qc = qt[lo:lo + C]
        qs = qc * jnp.exp(b - mid)
        kss[ch] = (kc * jnp.exp(mid - b)).astype(BF16)
        qds[ch] = (qc * jnp.exp(b)).astype(BF16)
        kws[ch] = (kc * jnp.exp(last - b)).astype(BF16)
        decs[ch] = last
        q2s[ch] = jnp.concatenate([qs * m0, qs * m1], axis=0).astype(BF16)
    yield
    for ch in order:
        scs[ch] = lax.dot_general(q2s[ch], kss[ch], _NT, preferred_element_type=F32)
    yield
    for ch in order:
        v2 = v_ref[0, ch * C:(ch + 1) * C, :]
        kvs[ch] = lax.dot_general(kws[ch], v2, _TN, preferred_element_type=F32) * smask
    yield
    for ch in order:
        sc = jnp.where(tri, scs[ch], 0.0).astype(BF16)
        sc2 = jnp.concatenate([sc[:C], sc[C:]], axis=1)
        v2 = v_ref[0, ch * C:(ch + 1) * C, :]
        vbd = jnp.concatenate([jnp.concatenate([v2[:, :LANES], zero_v], axis=1),
                               jnp.concatenate([zero_v, v2[:, LANES:]], axis=1)], axis=0)
        intra[ch] = jnp.dot(sc2, vbd, preferred_element_type=F32)
    yield
    pad = jnp.zeros((LANES - nchunk, LANES), F32)
    dec_cols = jnp.exp(jnp.concatenate([decs[ch] for ch in range(nchunk)] + [pad], axis=0).T)
    st = s_ref[...]
    starts = {}
    for ch in order:
        starts[ch] = st.astype(BF16)
        st = st * dec_cols[:, ch:ch + 1] + kvs[ch]
    s_ref[...] = st
    yield
    for ch in order:
        inter = jnp.dot(qds[ch], starts[ch], preferred_element_type=F32)
        o_ref[0, ch * C:(ch + 1) * C, :] = inter + intra[ch]
    yield


def _gla_body(kf, qf, vf, laf, kb, qb, vb, lab, h0f, h0b, of, ob, hf_out, hb_out, sf, sb, *, nchunk):
    i = pl.program_id(2)

    @pl.when(i == 0)
    def _():
        sf[...] = h0f[0, 0]
        sb[...] = h0b[0, 0]

    sweeps = [_gla_direction(kf, qf, vf, laf, of, sf, False, nchunk),
              _gla_direction(kb, qb, vb, lab, ob, sb, True, nchunk)]
    for _ in range(GLA_STAGES):
        for sweep in sweeps:
            next(sweep)

    @pl.when(i == pl.num_programs(2) - 1)
    def _():
        hf_out[0, 0] = sf[...]
        hb_out[0, 0] = sb[...]


def _gla(k, q, v, laf, lab, h0f, h0b, lt):
    bsz, l, _ = k.shape
    pairs = h0f.shape[1]
    nt = l // lt
    fwd = lambda b, hp, i: (b, i, hp)
    bwd = lambda b, hp, i: (b, nt - 1 - i, hp)
    st = lambda b, hp, i: (b, hp, 0, 0)
    kq = lambda m: pl.BlockSpec((1, lt, LANES), m)
    vv = lambda m: pl.BlockSpec((1, lt, 2 * LANES), m)
    sspec = pl.BlockSpec((1, 1, LANES, 2 * LANES), st)
    return pl.pallas_call(
        functools.partial(_gla_body, nchunk=lt // GLA_CHUNK),
        grid=(bsz, pairs, nt),
        in_specs=[kq(fwd), kq(fwd), vv(fwd), kq(fwd), kq(bwd), kq(bwd), vv(bwd), kq(bwd), sspec, sspec],
        out_specs=[vv(fwd), vv(bwd), sspec, sspec],
        out_shape=[jax.ShapeDtypeStruct(v.shape, F32), jax.ShapeDtypeStruct(v.shape, F32),
                   jax.ShapeDtypeStruct(h0f.shape, F32), jax.ShapeDtypeStruct(h0b.shape, F32)],
        scratch_shapes=[pltpu.VMEM((LANES, 2 * LANES), F32), pltpu.VMEM((LANES, 2 * LANES), F32)],
        compiler_params=_cparams(("parallel", "parallel", "arbitrary")),
        name="gla",
    )(k, q, v, laf, k, q, v, lab, h0f, h0b)


def _pool_body(p_ref, o_ref, s1_ref, *, half, rows):
    xb = p_ref[0]
    l, ch = xb.shape
    t = lax.broadcasted_iota(I32, (l, 1), 0)
    col = t % GRID_W
    row = t // GRID_W

    def counts(pos, n):
        return (jnp.minimum(pos + half, n) - jnp.maximum(pos - half, 0)).astype(F32)

    blk = 4 * GRID_W
    r = lax.broadcasted_iota(I32, (blk, blk), 0)
    c = lax.broadcasted_iota(I32, (blk, blk), 1)
    band = ((r // GRID_W == c // GRID_W) & (c - r >= -half) & (c - r <= half - 1)).astype(BF16)
    for i in range(0, l // blk, 2):
        pair = jnp.concatenate([xb[i * blk:(i + 1) * blk], xb[(i + 1) * blk:(i + 2) * blk]], axis=1)
        sums = jnp.dot(band, pair, preferred_element_type=F32)
        s1_ref[i * blk:(i + 1) * blk, :] = sums[:, :ch]
        s1_ref[(i + 1) * blk:(i + 2) * blk, :] = sums[:, ch:]
    s1 = s1_ref[...]

    fwd = s1
    bwd = jnp.where(row >= 1, pltpu.roll(s1, GRID_W, axis=0), 0.0)
    s = 1
    while s < half:
        fwd = fwd + jnp.where(row + s <= rows - 1, pltpu.roll(fwd, l - s * GRID_W, axis=0), 0.0)
        bwd = bwd + jnp.where(row >= s, pltpu.roll(bwd, s * GRID_W, axis=0), 0.0)
        s *= 2
    inv_count = 1.0 / (counts(row, rows) * counts(col, GRID_W))
    o_ref[0] = ((fwd + bwd) * inv_count - xb.astype(F32)).astype(BF16)


def _pool_groups_body(p_ref, o_ref, s1_ref, *, rows):
    for gi, window in enumerate(POOL_WINDOWS):
        @pl.when(pl.program_id(1) == gi)
        def _(half=window // 2):
            _pool_body(p_ref, o_ref, s1_ref, half=half, rows=rows)


def _pool(pin, ch):
    bsz, l, pw = pin.shape
    return pl.pallas_call(
        functools.partial(_pool_groups_body, rows=l // GRID_W),
        grid=(bsz, pw // ch),
        in_specs=[pl.BlockSpec((1, l, ch), lambda b, g: (b, 0, g))],
        out_specs=pl.BlockSpec((1, l, ch), lambda b, g: (b, 0, g)),
        out_shape=jax.ShapeDtypeStruct((bsz, l, pw), BF16),
        scratch_shapes=[pltpu.VMEM((l, ch), F32)],
        compiler_params=_cparams(("parallel", "parallel")),
        name="pool",
    )(pin)


def _merge_body(x_ref, m1_ref, s1_ref, g1_ref, m2_ref, s2_ref,
                of_ref, ob_ref, sg_ref, mx_ref,
                wm_ref, wgla_ref, wpool_ref, wout_ref, gn_ref, wr_ref,
                x1_ref, h2_ref, aff_ref, *, heads, ne):
    d = x_ref.shape[1]

    def sub_tile(rows):
        x = x_ref[rows, :]
        h = _modulated_norm(x, m1_ref[0], s1_ref[0]).astype(BF16)
        o = of_ref[rows, :] + ob_ref[rows, :]
        sg = sg_ref[rows, :].astype(F32)
        og = []
        for j in range(heads):
            oj = o[:, j * LANES:(j + 1) * LANES]
            oj = oj * lax.rsqrt(jnp.mean(oj * oj, axis=-1, keepdims=True) + EPS) * gn_ref[...]
            og.append((oj * sg[:, j * LANES:(j + 1) * LANES]).astype(BF16))
        og = jnp.concatenate(og, axis=1)
        mixed = mx_ref[rows, :]
        yield
        gates = jnp.dot(h, wm_ref[...], preferred_element_type=F32)
        bg = jnp.dot(og, wgla_ref[...], preferred_element_type=F32)
        bp = jnp.dot(mixed, wpool_ref[...], preferred_element_type=F32)
        yield
        gates = jax.nn.sigmoid(gates)
        z = (gates[:, :d] * bg + gates[:, d:] * bp).astype(BF16)
        yield
        y = jnp.dot(z, wout_ref[...], preferred_element_type=F32)
        yield
        x1 = x + g1_ref[0] * y
        x1_ref[rows, :] = x1
        h2 = _modulated_norm(x1, m2_ref[0], s2_ref[0])
        hi = h2.astype(BF16)
        h2_ref[rows, :] = hi
        lo = (h2 - hi.astype(F32)).astype(BF16)
        yield
        lg = (jnp.dot(hi, wr_ref[...], preferred_element_type=F32)
              + jnp.dot(lo, wr_ref[...], preferred_element_type=F32))
        yield
        lgt = lg.T
        logit = lgt[0:ne] + lgt[ne:2 * ne]
        mx = jnp.max(logit, axis=0, keepdims=True)
        ex = jnp.exp(logit - mx)
        aff_ref[0, :, rows] = ex / jnp.sum(ex, axis=0, keepdims=True)
        yield

    _staggered([sub_tile(r) for r in _row_groups(x_ref.shape[0])], 7)


def _merge(x2, vecs, of, ob, sg, mixed, wm, wgla, wpool, wout, gn, wr, rows_per_sample, heads, ne, tm):
    n, d = x2.shape
    bsz = n // rows_per_sample
    tps = rows_per_sample // tm
    row = lambda i: (i, 0)
    vec = lambda i: (i // tps, 0, 0)
    full = lambda a: pl.BlockSpec(a.shape, lambda i: (0,) * a.ndim, pipeline_mode=pl.Buffered(1))
    gv = of.shape[-1]
    return pl.pallas_call(
        functools.partial(_merge_body, heads=heads, ne=ne),
        grid=(n // tm,),
        in_specs=[pl.BlockSpec((tm, d), row)] + [pl.BlockSpec((1, 1, d), vec)] * 5
                 + [pl.BlockSpec((tm, gv), row)] * 4
                 + [full(wm), full(wgla), full(wpool), full(wout), full(gn), full(wr)],
        out_specs=[pl.BlockSpec((tm, d), row),
                   pl.BlockSpec((tm, d), row),
                   pl.BlockSpec((1, ne, tm), lambda i: (i // tps, 0, i % tps))],
        out_shape=[jax.ShapeDtypeStruct((n, d), F32),
                   jax.ShapeDtypeStruct((n, d), BF16),
                   jax.ShapeDtypeStruct((bsz, ne, rows_per_sample), F32)],
        compiler_params=_cparams(("parallel",)),
        name="merge",
    )(x2, *vecs, of, ob, sg, mixed, wm, wgla, wpool, wout, gn, wr)


def _route_body(aff_ref, pos_ref, off_ref, cnt_ref, *, cap, ntb):
    a = aff_ref[0]
    ne, l = a.shape
    blk = ROUTE_BLK

    def bisect(i, v):
        cand = v | jnp.left_shift(jnp.int32(1), 30 - i)
        cnt = jnp.sum((a >= lax.bitcast_convert_type(cand, F32)).astype(F32), axis=1, keepdims=True)
        return jnp.where(cnt >= cap, cand, v)

    thr = lax.bitcast_convert_type(lax.fori_loop(0, 31, bisect, jnp.zeros((ne, 1), I32)), F32)
    gt = a > thr
    tie = a == thr
    need = cap - jnp.sum(gt.astype(F32), axis=1, keepdims=True)

    r = lax.broadcasted_iota(I32, (blk, blk), 0)
    c = lax.broadcasted_iota(I32, (blk, blk), 1)
    upper = (r <= c).astype(BF16)
    lane = lax.broadcasted_iota(I32, (1, LANES), 1)

    def prefix(mask_f):
        run = jnp.zeros((ne, 1), F32)
        offs = jnp.zeros((ne, LANES), F32)
        for tb in range(ntb):
            m = mask_f[:, tb * blk:(tb + 1) * blk].astype(BF16)
            loc = jnp.dot(m, upper, preferred_element_type=F32)
            cnt_ref[:, tb * blk:(tb + 1) * blk] = loc + run
            offs = jnp.where(lane == tb, run, offs)
            run = run + loc[:, blk - 1:blk]
        return jnp.where(lane >= ntb, run, offs)

    tie_f = tie.astype(F32)
    prefix(tie_f)
    tie_excl = cnt_ref[...] - tie_f
    sel = gt | (tie & (tie_excl < need))
    offs = prefix(sel.astype(F32))
    pos_ref[0] = jnp.where(sel, cnt_ref[...] - 1.0, -1.0).astype(I32)
    off_ref[0] = offs.astype(I32)


def _route(aff, cap):
    bsz, ne, l = aff.shape
    spec = lambda s: pl.BlockSpec((1,) + s, lambda b: (b, 0, 0))
    return pl.pallas_call(
        functools.partial(_route_body, cap=cap, ntb=l // ROUTE_BLK),
        grid=(bsz,),
        in_specs=[spec((ne, l))],
        out_specs=[spec((ne, l)), spec((ne, LANES))],
        out_shape=[jax.ShapeDtypeStruct((bsz, ne, l), I32),
                   jax.ShapeDtypeStruct((bsz, ne, LANES), I32)],
        scratch_shapes=[pltpu.VMEM((ne, l), F32)],
        compiler_params=_cparams(("parallel",)),
        name="route",
    )(aff)


def _window_plan(off_ref, bb, tt, experts, ne):
    lows = [off_ref[(bb * ne + e) * LANES + tt] & -16 for e in experts]
    ends = [off_ref[(bb * ne + e) * LANES + tt + 1] for e in experts]
    return lows, ends


def _window_rounds(lows, ends):
    rounds = jnp.int32(0)
    for lo, hi in zip(lows, ends):
        rounds = jnp.maximum(rounds, lax.div(hi - lo + (SLOT_WIN - 1), jnp.int32(SLOT_WIN)))
    return rounds


def _gatherx_body(off_ref, h_ref, pos_ref, xs_ref, *, ne, cap, eg):
    b = pl.program_id(0)
    g = pl.program_id(1)
    tb = pl.program_id(2)
    win = SLOT_WIN

    @pl.when(tb == 0)
    def _():
        xs_ref[...] = jnp.zeros(xs_ref.shape, BF16)

    j_col = lax.broadcasted_iota(I32, (win, 1), 0)
    t = ROUTE_BLK
    for sub in range(h_ref.shape[0] // t):
        lows, ends = _window_plan(off_ref, b, tb * (h_ref.shape[0] // t) + sub, [g * eg + k for k in range(eg)], ne)
        rounds = _window_rounds(lows, ends)

        def one_round(r, carry, lows=lows, sub=sub):
            starts = [pl.multiple_of(jnp.minimum(lows[k] + r * win, cap - win), 16) for k in range(eg)]
            pieces = []
            for k in range(eg):
                p = pos_ref[0, k:k + 1, sub * t:(sub + 1) * t]
                hit = (p - starts[k] == j_col) & (p >= lows[k] + r * win)
                pieces.append(jnp.where(hit, 1.0, 0.0).astype(BF16))
            sel = jnp.concatenate(pieces, axis=0)
            rows = jnp.dot(sel, h_ref[sub * t:(sub + 1) * t, :],
                           preferred_element_type=F32).astype(BF16)
            for k in range(eg):
                dst = (0, k, pl.ds(starts[k], win), slice(None))
                xs_ref[dst] = xs_ref[dst] + rows[k * win:(k + 1) * win]
            return carry

        lax.fori_loop(0, rounds, one_round, 0)


def _gatherx(off_flat, h2, pos, cap):
    n, d = h2.shape
    bsz, ne, l = pos.shape
    t = min(l, 4 * ROUTE_BLK)
    ntb = l // t
    eg = 8
    grid_spec = pltpu.PrefetchScalarGridSpec(
        num_scalar_prefetch=1,
        grid=(bsz, ne // eg, ntb),
        in_specs=[pl.BlockSpec((t, d), lambda b, g, i, off: (b * ntb + i, 0)),
                  pl.BlockSpec((1, eg, t), lambda b, g, i, off: (b, g, i))],
        out_specs=pl.BlockSpec((1, eg, cap, d), lambda b, g, i, off: (b, g, 0, 0)),
    )
    return pl.pallas_call(
        functools.partial(_gatherx_body, ne=ne, cap=cap, eg=eg),
        grid_spec=grid_spec,
        out_shape=jax.ShapeDtypeStruct((bsz, ne, cap, d), BF16),
        compiler_params=_cparams(("parallel", "parallel", "arbitrary")),
        name="gatherx",
    )(off_flat, h2, pos)


def _moe_body(xs_ref, wg_ref, wu_ref, wd_ref, y_ref, *, rc):
    wg = wg_ref[0].astype(BF16)
    wu = wu_ref[0].astype(BF16)
    wd = wd_ref[0].astype(BF16)
    cap = xs_ref.shape[2]
    acts = []
    for ch in range(cap // rc):
        xs = xs_ref[0, 0, ch * rc:(ch + 1) * rc, :]
        gate = jnp.dot(xs, wg, preferred_element_type=F32)
        up = jnp.dot(xs, wu, preferred_element_type=F32)
        acts.append((_silu(gate) * up).astype(BF16))
    for ch in range(cap // rc):
        y_ref[0, 0, ch * rc:(ch + 1) * rc, :] = jnp.dot(acts[ch], wd, preferred_element_type=F32).astype(BF16)


def _moe(xs, wg, wu, wd):
    bsz, ne, cap, d = xs.shape
    de = wg.shape[2]
    slot = pl.BlockSpec((1, 1, cap, d), lambda e, b: (b, e, 0, 0))
    return pl.pallas_call(
        functools.partial(_moe_body, rc=min(cap, 512)),
        grid=(ne, bsz),
        in_specs=[slot,
                  pl.BlockSpec((1, d, de), lambda e, b: (e, 0, 0)),
                  pl.BlockSpec((1, d, de), lambda e, b: (e, 0, 0)),
                  pl.BlockSpec((1, de, d), lambda e, b: (e, 0, 0))],
        out_specs=slot,
        out_shape=jax.ShapeDtypeStruct((bsz, ne, cap, d), BF16),
        compiler_params=_cparams(("parallel", "arbitrary")),
        name="moe",
    )(xs, wg, wu, wd)


def _combine_body(off_ref, x1_ref, g2_ref, fg_ref, pos_ref, aff_ref, y_hbm, o_ref, ybuf, sem,
                  *, ne, cap, ntb):
    b = pl.program_id(0)
    tb = pl.program_id(1)
    step = b * ntb + tb
    nsteps = pl.num_programs(0) * ntb
    slot = step % 2
    spare = 2
    win = SLOT_WIN

    def block_lows(bb, tt):
        return _window_plan(off_ref, bb, tt, range(ne), ne)[0]

    def window_copy(bb, e, start, buf):
        return pltpu.make_async_copy(y_hbm.at[bb, e, pl.ds(start, win), :],
                                     ybuf.at[buf, pl.ds(e * win, win), :], sem.at[buf, e])

    def window_starts(lows, r):
        return [pl.multiple_of(jnp.minimum(lows[e] + r * win, cap - win), 16) for e in range(ne)]

    def start_first_round(bb, tt, buf):
        starts = window_starts(block_lows(bb, tt), 0)
        for e in range(ne):
            window_copy(bb, e, starts[e], buf).start()

    @pl.when(step == 0)
    def _():
        start_first_round(b, tb, 0)

    lows, ends = _window_plan(off_ref, b, tb, range(ne), ne)
    rounds = _window_rounds(lows, ends)
    j_col = lax.broadcasted_iota(I32, (win, 1), 0)

    def expand(r, starts, buf):
        pieces = []
        for e in range(ne):
            p = pos_ref[0, e:e + 1, :]
            valid = p >= lows[e] + r * win
            hit = (p - starts[e] == j_col) & valid
            pieces.append(jnp.where(hit, aff_ref[0, e:e + 1, :], 0.0).astype(BF16))
        pmat = jnp.concatenate(pieces, axis=0)
        return lax.dot_general(pmat, ybuf[buf], _TN, preferred_element_type=F32)

    nxt = jnp.minimum(step + 1, nsteps - 1)
    start_first_round(lax.div(nxt, jnp.int32(ntb)), lax.rem(nxt, jnp.int32(ntb)), 1 - slot)
    starts0 = window_starts(lows, 0)
    for e in range(ne):
        window_copy(b, e, starts0[e], slot).wait()
    moe = expand(0, starts0, slot)

    def extra_round(r, acc):
        starts = window_starts(lows, r)
        for e in range(ne):
            window_copy(b, e, starts[e], spare).start()
        for e in range(ne):
            window_copy(b, e, starts[e], spare).wait()
        return acc + expand(r, starts, spare)

    moe = lax.fori_loop(1, rounds, extra_round, moe)
    x2 = x1_ref[...] + g2_ref[0] * moe
    ms = jnp.mean(x2 * x2, axis=-1, keepdims=True)
    o_ref[...] = x2 * lax.rsqrt(ms + EPS) * fg_ref[...]

    @pl.when(step == nsteps - 1)
    def _():
        for e in range(ne):
            window_copy(b, e, starts0[e], 1 - slot).wait()


def _combine(off_flat, x1, g2, fg, pos, aff, y, rows_per_sample):
    n, d = x1.shape
    bsz, ne, cap, _ = y.shape
    t = ROUTE_BLK
    ntb = rows_per_sample // t
    grid_spec = pltpu.PrefetchScalarGridSpec(
        num_scalar_prefetch=1,
        grid=(bsz, ntb),
        in_specs=[pl.BlockSpec((t, d), lambda b, i, off: (b * ntb + i, 0)),
                  pl.BlockSpec((1, 1, d), lambda b, i, off: (b, 0, 0)),
                  pl.BlockSpec((1, d), lambda b, i, off: (0, 0)),
                  pl.BlockSpec((1, ne, t), lambda b, i, off: (b, 0, i)),
                  pl.BlockSpec((1, ne, t), lambda b, i, off: (b, 0, i)),
                  pl.BlockSpec(memory_space=pl.ANY)],
        out_specs=pl.BlockSpec((t, d), lambda b, i, off: (b * ntb + i, 0)),
        scratch_shapes=[pltpu.VMEM((3, ne * SLOT_WIN, d), BF16), pltpu.SemaphoreType.DMA((3, ne))],
    )
    return pl.pallas_call(
        functools.partial(_combine_body, ne=ne, cap=cap, ntb=ntb),
        grid_spec=grid_spec,
        out_shape=jax.ShapeDtypeStruct((n, d), F32),
        compiler_params=_cparams(("arbitrary", "arbitrary")),
        name="combine",
    )(off_flat, x1, g2, fg, pos, aff, y)


def kernel(x, c, ctx, c_ctx, ada_w, ada_b, norm1_g, norm2_g, w_in, w_decay_up, b_decay, gla_norm_g,
           w_gla_proj, pool_w, pool_scale, w_pool_proj, w_out, w_router, w_gate_e, w_up_e, w_down_e,
           final_norm_g):
    assert ada_w.shape[0] == 1, "single-layer block"
    bsz, l, d = x.shape
    lc = ctx.shape[1]
    rank, dk = w_decay_up.shape[2], w_decay_up.shape[3]
    dvh = gla_norm_g.shape[1]
    dv = w_gla_proj.shape[1]
    heads = dv // dvh
    groups, ch = pool_w.shape[1], pool_w.shape[2]
    pw = groups * ch
    ne = w_router.shape[2]
    cap = EC_CAPACITY * l // ne
    assert dk // heads == 64 and dvh == LANES and ch == LANES and heads % 2 == 0
    assert l % ROUTE_BLK == 0 and l // ROUTE_BLK < LANES and cap >= SLOT_WIN and cap % 16 == 0

    cin = jnp.zeros((8, d), F32).at[:bsz].set(c).at[bsz].set(c_ctx)
    mods = _ada(cin, ada_w[0], ada_b[0][None, :])
    sh1, sc1, gt1, sh2, sc2, gt2 = [mods[:, i * d:(i + 1) * d] for i in range(6)]
    vec3 = lambda a: a[:, None, :]
    mult1 = norm1_g[0][None, :] * (1.0 + sc1)
    mult2 = norm2_g[0][None, :] * (1.0 + sc2)

    o_r = dk + dv
    o_q = o_r + 2 * rank
    o_g = o_q + dk
    o_p = o_g + dv
    o_m = o_p + pw
    up = jnp.zeros((2 * rank, 2 * dk), F32)
    up = up.at[:rank, :dk].set(w_decay_up[0, 0]).at[rank:, dk:].set(w_decay_up[0, 1])
    psc = pool_scale[0].reshape(groups, 1, ch)
    w1, wm, wpool = _prep(w_in, up, pool_w[0], psc, w_pool_proj[0], o_r, o_q, o_g, o_m,
                          float(dk // heads) ** -0.5)
    bz = b_decay[0]

    zero_state = jnp.zeros((bsz, heads // 2, LANES, 2 * LANES), F32)
    cm = jnp.broadcast_to(vec3(mult1[bsz:bsz + 1]), (bsz, 1, d))
    cs = jnp.broadcast_to(vec3(sh1[bsz:bsz + 1]), (bsz, 1, d))
    ck, cv, _, claf, clab, _, _ = _inproj(ctx.reshape(bsz * lc, d), cm, cs, w1, bz, lc, dk, dv, pw, min(lc, 256))
    r3 = lambda a, n: a.reshape(bsz, n, a.shape[-1])
    _, _, h_f, h_b = _gla(r3(ck, lc), r3(ck, lc), r3(cv, lc), r3(claf, lc), r3(clab, lc),
                          zero_state, zero_state, min(lc, 256))

    x2 = x.reshape(bsz * l, d)
    tm = min(l, 1024)
    k, v, q, laf, lab, sg, pin = _inproj(x2, vec3(mult1[:bsz]), vec3(sh1[:bsz]), w1, bz, l, dk, dv, pw, tm)
    of, ob, _, _ = _gla(r3(k, l), r3(q, l), r3(v, l), r3(laf, l), r3(lab, l), h_f, h_b, min(l, 1024))
    pooled = _pool(r3(pin, l), ch).reshape(bsz * l, pw)

    wr_hi = w_router[0].astype(BF16)
    wr_lo = (w_router[0] - wr_hi.astype(F32)).astype(BF16)
    wr = jnp.zeros((d, LANES), BF16).at[:, :ne].set(wr_hi).at[:, ne:2 * ne].set(wr_lo)
    vecs = [vec3(mult1[:bsz]), vec3(sh1[:bsz]), vec3(gt1[:bsz]), vec3(mult2[:bsz]), vec3(sh2[:bsz])]
    x1, h2, aff = _merge(x2, vecs, of.reshape(bsz * l, dv), ob.reshape(bsz * l, dv), sg, pooled,
                          wm, w_gla_proj[0].astype(BF16), wpool, w_out[0].astype(BF16),
                          gla_norm_g[0][None, :], wr, l, heads, ne, tm)

    pos, offs = _route(aff, cap)
    off_flat = offs.reshape(-1)
    xs = _gatherx(off_flat, h2, pos, cap)
    y = _moe(xs, w_gate_e[0], w_up_e[0], w_down_e[0])
    out = _combine(off_flat, x1, vec3(gt2[:bsz]), final_norm_g[None, :], pos, aff, y, l)
    return out.reshape(bsz, l, d)
```

```python
import functools

import jax
import jax.numpy as jnp
from jax import lax
from jax.experimental import pallas as pl
from jax.experimental.pallas import tpu as pltpu

F32 = jnp.float32
BF16 = jnp.bfloat16
I32 = jnp.int32
HIGHEST = lax.Precision.HIGHEST

EPS = 1e-6
GRID_W = 64
GLA_CHUNK = 64
GLA_STAGES = 6
GATE_NORMALIZER = 16.0
POOL_WINDOWS = (2, 4, 8, 16)
EC_CAPACITY = 2
LANES = 128
ROUTE_BLK = 256
SLOT_WIN = 64
SUB_ROWS = 256
VMEM_LIMIT = 56 * 1024 * 1024

_NT = (((1,), (1,)), ((), ()))
_TN = (((0,), (0,)), ((), ()))


def _cparams(sem):
    return pltpu.CompilerParams(dimension_semantics=sem, vmem_limit_bytes=VMEM_LIMIT)


def _silu(x):
    return x * jax.nn.sigmoid(x)


def _row_groups(n):
    step = min(n, SUB_ROWS)
    return [pl.ds(i, step) for i in range(0, n, step)]


def _staggered(gens, nstages):
    for t in range(nstages + len(gens) - 1):
        for g in reversed(range(len(gens))):
            if 0 <= t - g < nstages:
                next(gens[g])


def _ada_body(c_ref, w_ref, b_ref, o_ref):
    s = _silu(c_ref[...])
    rows = s.shape[0]
    s_hi = s.astype(BF16)
    s_lo = (s - s_hi.astype(F32)).astype(BF16)
    w = w_ref[...]
    w_hi = w.astype(BF16)
    w_lo = (w - w_hi.astype(F32)).astype(BF16)
    both = jnp.dot(jnp.concatenate([s_hi, s_lo], axis=0), w_hi, preferred_element_type=F32)
    o_ref[...] = (both[:rows] + both[rows:] + jnp.dot(s_hi, w_lo, preferred_element_type=F32)) + b_ref[...]


def _ada(cin, w, b):
    rows, d = cin.shape
    n = w.shape[1]
    nb = n // 4
    return pl.pallas_call(
        _ada_body,
        grid=(n // nb,),
        in_specs=[pl.BlockSpec((rows, d), lambda j: (0, 0)),
                  pl.BlockSpec((d, nb), lambda j: (0, j)),
                  pl.BlockSpec((1, nb), lambda j: (0, j))],
        out_specs=pl.BlockSpec((rows, nb), lambda j: (0, j)),
        out_shape=jax.ShapeDtypeStruct((rows, n), F32),
        compiler_params=_cparams(("arbitrary",)),
        name="ada",
    )(cin, w, b)


def _prep_body(w_ref, up_ref, pw_ref, ps_ref, wpp_ref, w1_ref, wm_ref, wpf_ref, *, o_r, o_q, o_g, o_m, qscale):
    dk = o_g - o_q
    wz = lax.dot_general(w_ref[o_r:o_q, :], up_ref[...], _TN, precision=HIGHEST, preferred_element_type=F32)
    nz = wz.shape[1]
    w1_ref[:, :o_r] = w_ref[:o_r, :].T.astype(BF16)
    w1_ref[:, o_r:o_r + dk] = (w_ref[o_q:o_g, :].T * qscale).astype(BF16)
    w1_ref[:, o_r + dk:o_r + dk + nz] = wz.astype(BF16)
    w1_ref[:, o_r + dk + nz:] = w_ref[o_g:o_m, :].T.astype(BF16)
    wm_ref[...] = w_ref[o_m:, :].T.astype(BF16)
    wpf_ref[...] = jnp.dot(pw_ref[0] * ps_ref[0], wpp_ref[...], precision=HIGHEST,
                           preferred_element_type=F32).astype(BF16)


def _prep(w_in_t, up, pool_w, pool_scale3, w_pool_proj, o_r, o_q, o_g, o_m, qscale):
    n, d = w_in_t.shape
    n1 = o_r + (o_g - o_q) + up.shape[1] + (o_m - o_g)
    groups, ch, _ = pool_w.shape
    tr = d // groups
    dm = w_pool_proj.shape[1]
    return pl.pallas_call(
        functools.partial(_prep_body, o_r=o_r, o_q=o_q, o_g=o_g, o_m=o_m, qscale=qscale),
        grid=(d // tr,),
        in_specs=[pl.BlockSpec((n, tr), lambda i: (0, i)),
                  pl.BlockSpec(up.shape, lambda i: (0, 0)),
                  pl.BlockSpec((1, ch, ch), lambda i: (i, 0, 0)),
                  pl.BlockSpec((1, 1, ch), lambda i: (i, 0, 0)),
                  pl.BlockSpec((ch, dm), lambda i: (i, 0))],
        out_specs=[pl.BlockSpec((tr, n1), lambda i: (i, 0)),
                   pl.BlockSpec((tr, n - o_m), lambda i: (i, 0)),
                   pl.BlockSpec((ch, dm), lambda i: (i, 0))],
        out_shape=[jax.ShapeDtypeStruct((d, n1), BF16), jax.ShapeDtypeStruct((d, n - o_m), BF16),
                   jax.ShapeDtypeStruct((groups * ch, dm), BF16)],
        compiler_params=_cparams(("parallel",)),
        name="prep",
    )(w_in_t, up, pool_w, pool_scale3, w_pool_proj)


def _modulated_norm(x, mult, shift):
    ms = jnp.mean(x * x, axis=-1, keepdims=True)
    return (x * lax.rsqrt(ms + EPS)) * mult + shift


def _log_sigmoid(z):
    return jnp.minimum(z, 0.0) - jnp.log1p(jnp.exp(-jnp.abs(z)))


def _inproj_body(x_ref, mult_ref, shift_ref, w_ref, bz_ref,
                 k_ref, v_ref, q_ref, laf_ref, lab_ref, sg_ref, p_ref, *, dk, dv, pw):
    def sub_tile(rows):
        h = _modulated_norm(x_ref[rows, :], mult_ref[0], shift_ref[0]).astype(BF16)
        yield
        u = jnp.dot(h, w_ref[...], preferred_element_type=F32)
        yield
        o = 0
        k_ref[rows, :] = u[:, o:o + dk].astype(BF16); o += dk
        v_ref[rows, :] = u[:, o:o + dv].astype(BF16); o += dv
        q_ref[rows, :] = u[:, o:o + dk].astype(BF16); o += dk
        zf = u[:, o:o + dk] + bz_ref[0:1, :]; o += dk
        zb = u[:, o:o + dk] + bz_ref[1:2, :]; o += dk
        laf_ref[rows, :] = _log_sigmoid(zf) * (1.0 / GATE_NORMALIZER)
        lab_ref[rows, :] = _log_sigmoid(zb) * (1.0 / GATE_NORMALIZER)
        sg_ref[rows, :] = _silu(u[:, o:o + dv]).astype(BF16); o += dv
        p_ref[rows, :] = u[:, o:o + pw].astype(BF16)
        yield

    _staggered([sub_tile(r) for r in _row_groups(x_ref.shape[0])], 3)


def _inproj(x2, mult, shift, w, bz, rows_per_sample, dk, dv, pw, tm):
    n, d = x2.shape
    tps = rows_per_sample // tm
    row = lambda i: (i, 0)
    vec = lambda i: (i // tps, 0, 0)
    outs = [(dk, BF16), (dv, BF16), (dk, BF16), (dk, F32), (dk, F32), (dv, BF16), (pw, BF16)]
    return pl.pallas_call(
        functools.partial(_inproj_body, dk=dk, dv=dv, pw=pw),
        grid=(n // tm,),
        in_specs=[pl.BlockSpec((tm, d), row),
                  pl.BlockSpec((1, 1, d), vec),
                  pl.BlockSpec((1, 1, d), vec),
                  pl.BlockSpec(w.shape, lambda i: (0, 0), pipeline_mode=pl.Buffered(1)),
                  pl.BlockSpec(bz.shape, lambda i: (0, 0))],
        out_specs=[pl.BlockSpec((tm, c), row) for c, _ in outs],
        out_shape=[jax.ShapeDtypeStruct((n, c), t) for c, t in outs],
        compiler_params=_cparams(("parallel",)),
        name="inproj",
    )(x2, mult, shift, w, bz)


def _gla_direction(k_ref, q_ref, v_ref, la_ref, o_ref, s_ref, reverse, nchunk):
    C = GLA_CHUNK
    hd = C
    cb = 4 * C
    lt = nchunk * C
    la = la_ref[0]
    r = lax.broadcasted_iota(I32, (cb, cb), 0)
    c = lax.broadcasted_iota(I32, (cb, cb), 1)
    same = (r // C) == (c // C)
    cum = jnp.where(same & ((c >= r) if reverse else (c <= r)), 1.0, 0.0).astype(BF16)
    la_hi = la.astype(BF16)
    la_lo = (la - la_hi.astype(F32)).astype(BF16)
    la2 = jnp.concatenate([la_hi, la_lo], axis=1)
    bcs = []
    for blk in range(lt // cb):
        part = jnp.dot(cum, la2[blk * cb:(blk + 1) * cb], preferred_element_type=F32)
        bcs.append(part[:, :LANES] + part[:, LANES:])
    ri = lax.broadcasted_iota(I32, (2 * C, C), 0) % C
    ci = lax.broadcasted_iota(I32, (2 * C, C), 1)
    tri = (ci >= ri) if reverse else (ci <= ri)
    lane = lax.broadcasted_iota(I32, (1, LANES), 1)
    m0 = (lane < hd).astype(F32)
    m1 = (lane >= hd).astype(F32)
    sr = lax.broadcasted_iota(I32, (LANES, 2 * LANES), 0)
    sl = lax.broadcasted_iota(I32, (LANES, 2 * LANES), 1)
    smask = ((sr < hd) == (sl < LANES)).astype(F32)
    kt = k_ref[0].astype(F32)
    qt = q_ref[0].astype(F32)
    zero_v = jnp.zeros((C, LANES), BF16)
    order = list(range(nchunk - 1, -1, -1) if reverse else range(nchunk))
    intra, qds, kvs, decs, q2s, kss, kws, scs = {}, {}, {}, {}, {}, {}, {}, {}
    for ch in order:
        lo = ch * C
        b = bcs[lo // cb][lo % cb:lo % cb + C]
        last = b[0:1] if reverse else b[C - 1:C]
        mid = b[C // 2:C // 2 + 1] if reverse else b[C // 2 - 1:C // 2]
        kc = kt[lo:lo + C]
        qc = qt[lo:lo + C]
        qs = qc * jnp.exp(b - mid)
        kss[ch] = (kc * jnp.exp(mid - b)).astype(BF16)
        qds[ch] = (qc * jnp.exp(b)).astype(BF16)
        kws[ch] = (kc * jnp.exp(last - b)).astype(BF16)
        decs[ch] = last
        q2s[ch] = jnp.concatenate([qs * m0, qs * m1], axis=0).astype(BF16)
    yield
    for ch in order:
        scs[ch] = lax.dot_general(q2s[ch], kss[ch], _NT, preferred_element_type=F32)
    yield
    for ch in order:
        v2 = v_ref[0, ch * C:(ch + 1) * C, :]
        kvs[ch] = lax.dot_general(kws[ch], v2, _TN, preferred_element_type=F32) * smask
    yield
    for ch in order:
        sc = jnp.where(tri, scs[ch], 0.0).astype(BF16)
        sc2 = jnp.concatenate([sc[:C], sc[C:]], axis=1)
        v2 = v_ref[0, ch * C:(ch + 1) * C, :]
        vbd = jnp.concatenate([jnp.concatenate([v2[:, :LANES], zero_v], axis=1),
                               jnp.concatenate([zero_v, v2[:, LANES:]], axis=1)], axis=0)
        intra[ch] = jnp.dot(sc2, vbd, preferred_element_type=F32)
    yield
    pad = jnp.zeros((LANES - nchunk, LANES), F32)
    dec_cols = jnp.exp(jnp.concatenate([decs[ch] for ch in range(nchunk)] + [pad], axis=0).T)
    st = s_ref[...]
    starts = {}
    for ch in order:
        starts[ch] = st.astype(BF16)
        st = st * dec_cols[:, ch:ch + 1] + kvs[ch]
    s_ref[...] = st
    yield
    for ch in order:
        inter = jnp.dot(qds[ch], starts[ch], preferred_element_type=F32)
        o_ref[0, ch * C:(ch + 1) * C, :] = inter + intra[ch]
    yield


def _gla_body(kf, qf, vf, laf, kb, qb, vb, lab, h0f, h0b, of, ob, hf_out, hb_out, sf, sb, *, nchunk):
    i = pl.program_id(2)

    @pl.when(i == 0)
    def _():
        sf[...] = h0f[0, 0]
        sb[...] = h0b[0, 0]

    sweeps = [_gla_direction(kf, qf, vf, laf, of, sf, False, nchunk),
              _gla_direction(kb, qb, vb, lab, ob, sb, True, nchunk)]
    for _ in range(GLA_STAGES):
        for sweep in sweeps:
            next(sweep)

    @pl.when(i == pl.num_programs(2) - 1)
    def _():
        hf_out[0, 0] = sf[...]
        hb_out[0, 0] = sb[...]


def _gla(k, q, v, laf, lab, h0f, h0b, lt):
    bsz, l, _ = k.shape
    pairs = h0f.shape[1]
    nt = l // lt
    fwd = lambda b, hp, i: (b, i, hp)
    bwd = lambda b, hp, i: (b, nt - 1 - i, hp)
    st = lambda b, hp, i: (b, hp, 0, 0)
    kq = lambda m: pl.BlockSpec((1, lt, LANES), m)
    vv = lambda m: pl.BlockSpec((1, lt, 2 * LANES), m)
    sspec = pl.BlockSpec((1, 1, LANES, 2 * LANES), st)
    return pl.pallas_call(
        functools.partial(_gla_body, nchunk=lt // GLA_CHUNK),
        grid=(bsz, pairs, nt),
        in_specs=[kq(fwd), kq(fwd), vv(fwd), kq(fwd), kq(bwd), kq(bwd), vv(bwd), kq(bwd), sspec, sspec],
        out_specs=[vv(fwd), vv(bwd), sspec, sspec],
        out_shape=[jax.ShapeDtypeStruct(v.shape, F32), jax.ShapeDtypeStruct(v.shape, F32),
                   jax.ShapeDtypeStruct(h0f.shape, F32), jax.ShapeDtypeStruct(h0b.shape, F32)],
        scratch_shapes=[pltpu.VMEM((LANES, 2 * LANES), F32), pltpu.VMEM((LANES, 2 * LANES), F32)],
        compiler_params=_cparams(("parallel", "parallel", "arbitrary")),
        name="gla",
    )(k, q, v, laf, k, q, v, lab, h0f, h0b)


def _pool_body(p_ref, o_ref, s1_ref, *, half, rows):
    xb = p_ref[0]
    l, ch = xb.shape
    t = lax.broadcasted_iota(I32, (l, 1), 0)
    col = t % GRID_W
    row = t // GRID_W

    def counts(pos, n):
        return (jnp.minimum(pos + half, n) - jnp.maximum(pos - half, 0)).astype(F32)

    blk = 4 * GRID_W
    r = lax.broadcasted_iota(I32, (blk, blk), 0)
    c = lax.broadcasted_iota(I32, (blk, blk), 1)
    band = ((r // GRID_W == c // GRID_W) & (c - r >= -half) & (c - r <= half - 1)).astype(BF16)
    for i in range(0, l // blk, 2):
        pair = jnp.concatenate([xb[i * blk:(i + 1) * blk], xb[(i + 1) * blk:(i + 2) * blk]], axis=1)
        sums = jnp.dot(band, pair, preferred_element_type=F32)
        s1_ref[i * blk:(i + 1) * blk, :] = sums[:, :ch]
        s1_ref[(i + 1) * blk:(i + 2) * blk, :] = sums[:, ch:]
    s1 = s1_ref[...]

    fwd = s1
    bwd = jnp.where(row >= 1, pltpu.roll(s1, GRID_W, axis=0), 0.0)
    s = 1
    while s < half:
        fwd = fwd + jnp.where(row + s <= rows - 1, pltpu.roll(fwd, l - s * GRID_W, axis=0), 0.0)
        bwd = bwd + jnp.where(row >= s, pltpu.roll(bwd, s * GRID_W, axis=0), 0.0)
        s *= 2
    inv_count = 1.0 / (counts(row, rows) * counts(col, GRID_W))
    o_ref[0] = ((fwd + bwd) * inv_count - xb.astype(F32)).astype(BF16)


def _pool_groups_body(p_ref, o_ref, s1_ref, *, rows):
    for gi, window in enumerate(POOL_WINDOWS):
        @pl.when(pl.program_id(1) == gi)
        def _(half=window // 2):
            _pool_body(p_ref, o_ref, s1_ref, half=half, rows=rows)


def _pool(pin, ch):
    bsz, l, pw = pin.shape
    return pl.pallas_call(
        functools.partial(_pool_groups_body, rows=l // GRID_W),
        grid=(bsz, pw // ch),
        in_specs=[pl.BlockSpec((1, l, ch), lambda b, g: (b, 0, g))],
        out_specs=pl.BlockSpec((1, l, ch), lambda b, g: (b, 0, g)),
        out_shape=jax.ShapeDtypeStruct((bsz, l, pw), BF16),
        scratch_shapes=[pltpu.VMEM((l, ch), F32)],
        compiler_params=_cparams(("parallel", "parallel")),
        name="pool",
    )(pin)


def _merge_body(x_ref, m1_ref, s1_ref, g1_ref, m2_ref, s2_ref,
                of_ref, ob_ref, sg_ref, mx_ref,
                wm_ref, wgla_ref, wpool_ref, wout_ref, gn_ref, wr_ref,
                x1_ref, h2_ref, aff_ref, *, heads, ne):
    d = x_ref.shape[1]

    def sub_tile(rows):
        x = x_ref[rows, :]
        h = _modulated_norm(x, m1_ref[0], s1_ref[0]).astype(BF16)
        o = of_ref[rows, :] + ob_ref[rows, :]
        sg = sg_ref[rows, :].astype(F32)
        og = []
        for j in range(heads):
            oj = o[:, j * LANES:(j + 1) * LANES]
            oj = oj * lax.rsqrt(jnp.mean(oj * oj, axis=-1, keepdims=True) + EPS) * gn_ref[...]
            og.append((oj * sg[:, j * LANES:(j + 1) * LANES]).astype(BF16))
        og = jnp.concatenate(og, axis=1)
        mixed = mx_ref[rows, :]
        yield
        gates = jnp.dot(h, wm_ref[...], preferred_element_type=F32)
        bg = jnp.dot(og, wgla_ref[...], preferred_element_type=F32)
        bp = jnp.dot(mixed, wpool_ref[...], preferred_element_type=F32)
        yield
        gates = jax.nn.sigmoid(gates)
        z = (gates[:, :d] * bg + gates[:, d:] * bp).astype(BF16)
        yield
        y = jnp.dot(z, wout_ref[...], preferred_element_type=F32)
        yield
        x1 = x + g1_ref[0] * y
        x1_ref[rows, :] = x1
        h2 = _modulated_norm(x1, m2_ref[0], s2_ref[0])
        hi = h2.astype(BF16)
        h2_ref[rows, :] = hi
        lo = (h2 - hi.astype(F32)).astype(BF16)
        yield
        lg = (jnp.dot(hi, wr_ref[...], preferred_element_type=F32)
              + jnp.dot(lo, wr_ref[...], preferred_element_type=F32))
        yield
        lgt = lg.T
        logit = lgt[0:ne] + lgt[ne:2 * ne]
        mx = jnp.max(logit, axis=0, keepdims=True)
        ex = jnp.exp(logit - mx)
        aff_ref[0, :, rows] = ex / jnp.sum(ex, axis=0, keepdims=True)
        yield

    _staggered([sub_tile(r) for r in _row_groups(x_ref.shape[0])], 7)


def _merge(x2, vecs, of, ob, sg, mixed, wm, wgla, wpool, wout, gn, wr, rows_per_sample, heads, ne, tm):
    n, d = x2.shape
    bsz = n // rows_per_sample
    tps = rows_per_sample // tm
    row = lambda i: (i, 0)
    vec = lambda i: (i // tps, 0, 0)
    full = lambda a: pl.BlockSpec(a.shape, lambda i: (0,) * a.ndim, pipeline_mode=pl.Buffered(1))
    gv = of.shape[-1]
    return pl.pallas_call(
        functools.partial(_merge_body, heads=heads, ne=ne),
        grid=(n // tm,),
        in_specs=[pl.BlockSpec((tm, d), row)] + [pl.BlockSpec((1, 1, d), vec)] * 5
                 + [pl.BlockSpec((tm, gv), row)] * 4
                 + [full(wm), full(wgla), full(wpool), full(wout), full(gn), full(wr)],
        out_specs=[pl.BlockSpec((tm, d), row),
                   pl.BlockSpec((tm, d), row),
                   pl.BlockSpec((1, ne, tm), lambda i: (i // tps, 0, i % tps))],
        out_shape=[jax.ShapeDtypeStruct((n, d), F32),
                   jax.ShapeDtypeStruct((n, d), BF16),
                   jax.ShapeDtypeStruct((bsz, ne, rows_per_sample), F32)],
        compiler_params=_cparams(("parallel",)),
        name="merge",
    )(x2, *vecs, of, ob, sg, mixed, wm, wgla, wpool, wout, gn, wr)


def _route_body(aff_ref, pos_ref, off_ref, cnt_ref, *, cap, ntb):
    a = aff_ref[0]
    ne, l = a.shape
    blk = ROUTE_BLK

    def bisect(i, v):
        cand = v | jnp.left_shift(jnp.int32(1), 30 - i)
        cnt = jnp.sum((a >= lax.bitcast_convert_type(cand, F32)).astype(F32), axis=1, keepdims=True)
        return jnp.where(cnt >= cap, cand, v)

    thr = lax.bitcast_convert_type(lax.fori_loop(0, 31, bisect, jnp.zeros((ne, 1), I32)), F32)
    gt = a > thr
    tie = a == thr
    need = cap - jnp.sum(gt.astype(F32), axis=1, keepdims=True)

    r = lax.broadcasted_iota(I32, (blk, blk), 0)
    c = lax.broadcasted_iota(I32, (blk, blk), 1)
    upper = (r <= c).astype(BF16)
    lane = lax.broadcasted_iota(I32, (1, LANES), 1)

    def prefix(mask_f):
        run = jnp.zeros((ne, 1), F32)
        offs = jnp.zeros((ne, LANES), F32)
        for tb in range(ntb):
            m = mask_f[:, tb * blk:(tb + 1) * blk].astype(BF16)
            loc = jnp.dot(m, upper, preferred_element_type=F32)
            cnt_ref[:, tb * blk:(tb + 1) * blk] = loc + run
            offs = jnp.where(lane == tb, run, offs)
            run = run + loc[:, blk - 1:blk]
        return jnp.where(lane >= ntb, run, offs)

    tie_f = tie.astype(F32)
    prefix(tie_f)
    tie_excl = cnt_ref[...] - tie_f
    sel = gt | (tie & (tie_excl < need))
    offs = prefix(sel.astype(F32))
    pos_ref[0] = jnp.where(sel, cnt_ref[...] - 1.0, -1.0).astype(I32)
    off_ref[0] = offs.astype(I32)


def _route(aff, cap):
    bsz, ne, l = aff.shape
    spec = lambda s: pl.BlockSpec((1,) + s, lambda b: (b, 0, 0))
    return pl.pallas_call(
        functools.partial(_route_body, cap=cap, ntb=l // ROUTE_BLK),
        grid=(bsz,),
        in_specs=[spec((ne, l))],
        out_specs=[spec((ne, l)), spec((ne, LANES))],
        out_shape=[jax.ShapeDtypeStruct((bsz, ne, l), I32),
                   jax.ShapeDtypeStruct((bsz, ne, LANES), I32)],
        scratch_shapes=[pltpu.VMEM((ne, l), F32)],
        compiler_params=_cparams(("parallel",)),
        name="route",
    )(aff)


def _window_plan(off_ref, bb, tt, experts, ne):
    lows = [off_ref[(bb * ne + e) * LANES + tt] & -16 for e in experts]
    ends = [off_ref[(bb * ne + e) * LANES + tt + 1] for e in experts]
    return lows, ends


def _window_rounds(lows, ends):
    rounds = jnp.int32(0)
    for lo, hi in zip(lows, ends):
        rounds = jnp.maximum(rounds, lax.div(hi - lo + (SLOT_WIN - 1), jnp.int32(SLOT_WIN)))
    return rounds


def _gatherx_body(off_ref, h_ref, pos_ref, xs_ref, *, ne, cap, eg):
    b = pl.program_id(0)
    g = pl.program_id(1)
    tb = pl.program_id(2)
    win = SLOT_WIN

    @pl.when(tb == 0)
    def _():
        xs_ref[...] = jnp.zeros(xs_ref.shape, BF16)

    j_col = lax.broadcasted_iota(I32, (win, 1), 0)
    t = ROUTE_BLK
    for sub in range(h_ref.shape[0] // t):
        lows, ends = _window_plan(off_ref, b, tb * (h_ref.shape[0] // t) + sub, [g * eg + k for k in range(eg)], ne)
        rounds = _window_rounds(lows, ends)

        def one_round(r, carry, lows=lows, sub=sub):
            starts = [pl.multiple_of(jnp.minimum(lows[k] + r * win, cap - win), 16) for k in range(eg)]
            pieces = []
            for k in range(eg):
                p = pos_ref[0, k:k + 1, sub * t:(sub + 1) * t]
                hit = (p - starts[k] == j_col) & (p >= lows[k] + r * win)
                pieces.append(jnp.where(hit, 1.0, 0.0).astype(BF16))
            sel = jnp.concatenate(pieces, axis=0)
            rows = jnp.dot(sel, h_ref[sub * t:(sub + 1) * t, :],
                           preferred_element_type=F32).astype(BF16)
            for k in range(eg):
                dst = (0, k, pl.ds(starts[k], win), slice(None))
                xs_ref[dst] = xs_ref[dst] + rows[k * win:(k + 1) * win]
            return carry

        lax.fori_loop(0, rounds, one_round, 0)


def _gatherx(off_flat, h2, pos, cap):
    n, d = h2.shape
    bsz, ne, l = pos.shape
    t = min(l, 4 * ROUTE_BLK)
    ntb = l // t
    eg = 8
    grid_spec = pltpu.PrefetchScalarGridSpec(
        num_scalar_prefetch=1,
        grid=(bsz, ne // eg, ntb),
        in_specs=[pl.BlockSpec((t, d), lambda b, g, i, off: (b * ntb + i, 0)),
                  pl.BlockSpec((1, eg, t), lambda b, g, i, off: (b, g, i))],
        out_specs=pl.BlockSpec((1, eg, cap, d), lambda b, g, i, off: (b, g, 0, 0)),
    )
    return pl.pallas_call(
        functools.partial(_gatherx_body, ne=ne, cap=cap, eg=eg),
        grid_spec=grid_spec,
        out_shape=jax.ShapeDtypeStruct((bsz, ne, cap, d), BF16),
        compiler_params=_cparams(("parallel", "parallel", "arbitrary")),
        name="gatherx",
    )(off_flat, h2, pos)


def _moe_body(xs_ref, wg_ref, wu_ref, wd_ref, y_ref, *, rc):
    wg = wg_ref[0].astype(BF16)
    wu = wu_ref[0].astype(BF16)
    wd = wd_ref[0].astype(BF16)
    cap = xs_ref.shape[2]
    acts = []
    for ch in range(cap // rc):
        xs = xs_ref[0, 0, ch * rc:(ch + 1) * rc, :]
        gate = jnp.dot(xs, wg, preferred_element_type=F32)
        up = jnp.dot(xs, wu, preferred_element_type=F32)
        acts.append((_silu(gate) * up).astype(BF16))
    for ch in range(cap // rc):
        y_ref[0, 0, ch * rc:(ch + 1) * rc, :] = jnp.dot(acts[ch], wd, preferred_element_type=F32).astype(BF16)


def _moe(xs, wg, wu, wd):
    bsz, ne, cap, d = xs.shape
    de = wg.shape[2]
    slot = pl.BlockSpec((1, 1, cap, d), lambda e, b: (b, e, 0, 0))
    return pl.pallas_call(
        functools.partial(_moe_body, rc=min(cap, 512)),
        grid=(ne, bsz),
        in_specs=[slot,
                  pl.BlockSpec((1, d, de), lambda e, b: (e, 0, 0)),
                  pl.BlockSpec((1, d, de), lambda e, b: (e, 0, 0)),
                  pl.BlockSpec((1, de, d), lambda e, b: (e, 0, 0))],
        out_specs=slot,
        out_shape=jax.ShapeDtypeStruct((bsz, ne, cap, d), BF16),
        compiler_params=_cparams(("parallel", "arbitrary")),
        name="moe",
    )(xs, wg, wu, wd)


def _combine_body(off_ref, x1_ref, g2_ref, fg_ref, pos_ref, aff_ref, y_hbm, o_ref, ybuf, sem,
                  *, ne, cap, ntb):
    b = pl.program_id(0)
    tb = pl.program_id(1)
    step = b * ntb + tb
    nsteps = pl.num_programs(0) * ntb
    slot = step % 2
    spare = 2
    win = SLOT_WIN

    def block_lows(bb, tt):
        return _window_plan(off_ref, bb, tt, range(ne), ne)[0]

    def window_copy(bb, e, start, buf):
        return pltpu.make_async_copy(y_hbm.at[bb, e, pl.ds(start, win), :],
                                     ybuf.at[buf, pl.ds(e * win, win), :], sem.at[buf, e])

    def window_starts(lows, r):
        return [pl.multiple_of(jnp.minimum(lows[e] + r * win, cap - win), 16) for e in range(ne)]

    def start_first_round(bb, tt, buf):
        starts = window_starts(block_lows(bb, tt), 0)
        for e in range(ne):
            window_copy(bb, e, starts[e], buf).start()

    @pl.when(step == 0)
    def _():
        start_first_round(b, tb, 0)

    lows, ends = _window_plan(off_ref, b, tb, range(ne), ne)
    rounds = _window_rounds(lows, ends)
    j_col = lax.broadcasted_iota(I32, (win, 1), 0)

    def expand(r, starts, buf):
        pieces = []
        for e in range(ne):
            p = pos_ref[0, e:e + 1, :]
            valid = p >= lows[e] + r * win
            hit = (p - starts[e] == j_col) & valid
            pieces.append(jnp.where(hit, aff_ref[0, e:e + 1, :], 0.0).astype(BF16))
        pmat = jnp.concatenate(pieces, axis=0)
        return lax.dot_general(pmat, ybuf[buf], _TN, preferred_element_type=F32)

    nxt = jnp.minimum(step + 1, nsteps - 1)
    start_first_round(lax.div(nxt, jnp.int32(ntb)), lax.rem(nxt, jnp.int32(ntb)), 1 - slot)
    starts0 = window_starts(lows, 0)
    for e in range(ne):
        window_copy(b, e, starts0[e], slot).wait()
    moe = expand(0, starts0, slot)

    def extra_round(r, acc):
        starts = window_starts(lows, r)
        for e in range(ne):
            window_copy(b, e, starts[e], spare).start()
        for e in range(ne):
            window_copy(b, e, starts[e], spare).wait()
        return acc + expand(r, starts, spare)

    moe = lax.fori_loop(1, rounds, extra_round, moe)
    x2 = x1_ref[...] + g2_ref[0] * moe
    ms = jnp.mean(x2 * x2, axis=-1, keepdims=True)
    o_ref[...] = x2 * lax.rsqrt(ms + EPS) * fg_ref[...]

    @pl.when(step == nsteps - 1)
    def _():
        for e in range(ne):
            window_copy(b, e, starts0[e], 1 - slot).wait()


def _combine(off_flat, x1, g2, fg, pos, aff, y, rows_per_sample):
    n, d = x1.shape
    bsz, ne, cap, _ = y.shape
    t = ROUTE_BLK
    ntb = rows_per_sample // t
    grid_spec = pltpu.PrefetchScalarGridSpec(
        num_scalar_prefetch=1,
        grid=(bsz, ntb),
        in_specs=[pl.BlockSpec((t, d), lambda b, i, off: (b * ntb + i, 0)),
                  pl.BlockSpec((1, 1, d), lambda b, i, off: (b, 0, 0)),
                  pl.BlockSpec((1, d), lambda b, i, off: (0, 0)),
                  pl.BlockSpec((1, ne, t), lambda b, i, off: (b, 0, i)),
                  pl.BlockSpec((1, ne, t), lambda b, i, off: (b, 0, i)),
                  pl.BlockSpec(memory_space=pl.ANY)],
        out_specs=pl.BlockSpec((t, d), lambda b, i, off: (b * ntb + i, 0)),
        scratch_shapes=[pltpu.VMEM((3, ne * SLOT_WIN, d), BF16), pltpu.SemaphoreType.DMA((3, ne))],
    )
    return pl.pallas_call(
        functools.partial(_combine_body, ne=ne, cap=cap, ntb=ntb),
        grid_spec=grid_spec,
        out_shape=jax.ShapeDtypeStruct((n, d), F32),
        compiler_params=_cparams(("arbitrary", "arbitrary")),
        name="combine",
    )(off_flat, x1, g2, fg, pos, aff, y)


def kernel(x, c, ctx, c_ctx, ada_w, ada_b, norm1_g, norm2_g, w_in, w_decay_up, b_decay, gla_norm_g,
           w_gla_proj, pool_w, pool_scale, w_pool_proj, w_out, w_router, w_gate_e, w_up_e, w_down_e,
           final_norm_g):
    assert ada_w.shape[0] == 1, "single-layer block"
    bsz, l, d = x.shape
    lc = ctx.shape[1]
    rank, dk = w_decay_up.shape[2], w_decay_up.shape[3]
    dvh = gla_norm_g.shape[1]
    dv = w_gla_proj.shape[1]
    heads = dv // dvh
    groups, ch = pool_w.shape[1], pool_w.shape[2]
    pw = groups * ch
    ne = w_router.shape[2]
    cap = EC_CAPACITY * l // ne
    assert dk // heads == 64 and dvh == LANES and ch == LANES and heads % 2 == 0
    assert l % ROUTE_BLK == 0 and l // ROUTE_BLK < LANES and cap >= SLOT_WIN and cap % 16 == 0

    cin = jnp.zeros((8, d), F32).at[:bsz].set(c).at[bsz].set(c_ctx)
    mods = _ada(cin, ada_w[0], ada_b[0][None, :])
    sh1, sc1, gt1, sh2, sc2, gt2 = [mods[:, i * d:(i + 1) * d] for i in range(6)]
    vec3 = lambda a: a[:, None, :]
    mult1 = norm1_g[0][None, :] * (1.0 + sc1)
    mult2 = norm2_g[0][None, :] * (1.0 + sc2)

    o_r = dk + dv
    o_q = o_r + 2 * rank
    o_g = o_q + dk
    o_p = o_g + dv
    o_m = o_p + pw
    up = jnp.zeros((2 * rank, 2 * dk), F32)
    up = up.at[:rank, :dk].set(w_decay_up[0, 0]).at[rank:, dk:].set(w_decay_up[0, 1])
    psc = pool_scale[0].reshape(groups, 1, ch)
    w1, wm, wpool = _prep(w_in[0].T, up, pool_w[0], psc, w_pool_proj[0], o_r, o_q, o_g, o_m,
                          float(dk // heads) ** -0.5)
    bz = b_decay[0]

    zero_state = jnp.zeros((bsz, heads // 2, LANES, 2 * LANES), F32)
    cm = jnp.broadcast_to(vec3(mult1[bsz:bsz + 1]), (bsz, 1, d))
    cs = jnp.broadcast_to(vec3(sh1[bsz:bsz + 1]), (bsz, 1, d))
    ck, cv, _, claf, clab, _, _ = _inproj(ctx.reshape(bsz * lc, d), cm, cs, w1, bz, lc, dk, dv, pw, min(lc, 256))
    r3 = lambda a, n: a.reshape(bsz, n, a.shape[-1])
    _, _, h_f, h_b = _gla(r3(ck, lc), r3(ck, lc), r3(cv, lc), r3(claf, lc), r3(clab, lc),
                          zero_state, zero_state, min(lc, 256))

    x2 = x.reshape(bsz * l, d)
    tm = min(l, 1024)
    k, v, q, laf, lab, sg, pin = _inproj(x2, vec3(mult1[:bsz]), vec3(sh1[:bsz]), w1, bz, l, dk, dv, pw, tm)
    of, ob, _, _ = _gla(r3(k, l), r3(q, l), r3(v, l), r3(laf, l), r3(lab, l), h_f, h_b, min(l, 1024))
    pooled = _pool(r3(pin, l), ch).reshape(bsz * l, pw)

    wr_hi = w_router[0].astype(BF16)
    wr_lo = (w_router[0] - wr_hi.astype(F32)).astype(BF16)
    wr = jnp.zeros((d, LANES), BF16).at[:, :ne].set(wr_hi).at[:, ne:2 * ne].set(wr_lo)
    vecs = [vec3(mult1[:bsz]), vec3(sh1[:bsz]), vec3(gt1[:bsz]), vec3(mult2[:bsz]), vec3(sh2[:bsz])]
    x1, h2, aff = _merge(x2, vecs, of.reshape(bsz * l, dv), ob.reshape(bsz * l, dv), sg, pooled,
                          wm, w_gla_proj[0].astype(BF16), wpool, w_out[0].astype(BF16),
                          gla_norm_g[0][None, :], wr, l, heads, ne, tm)

    pos, offs = _route(aff, cap)
    off_flat = offs.reshape(-1)
    xs = _gatherx(off_flat, h2, pos, cap)
    y = _moe(xs, w_gate_e[0], w_up_e[0], w_down_e[0])
    out = _combine(off_flat, x1, vec3(gt2[:bsz]), final_norm_g[None, :], pos, aff, y, l)
    return out.reshape(bsz, l, d)
```

```python
import functools

import jax
import jax.numpy as jnp
from jax import lax
from jax.experimental import pallas as pl
from jax.experimental.pallas import tpu as pltpu

F32 = jnp.float32
BF16 = jnp.bfloat16
I32 = jnp.int32
HIGHEST = lax.Precision.HIGHEST

EPS = 1e-6
GRID_W = 64
GLA_CHUNK = 64
GLA_STAGES = 6
GATE_NORMALIZER = 16.0
POOL_WINDOWS = (2, 4, 8, 16)
EC_CAPACITY = 2

LANES = 128
SUBLANES = 8
BF16_ROWS = 16
MXU_DIM = 256
VMEM_BYTES = 64 * 1024 * 1024
VMEM_LIMIT = VMEM_BYTES * 7 // 8

HEAD_DK = LANES // 2
ROUTE_BLK = MXU_DIM
SLOT_WIN = 64
SUB_ROWS = MXU_DIM
TOKEN_TILE = 1024
GLA_TILE = 2048
CTX_TILE = 256

_NT = (((1,), (1,)), ((), ()))
_TN = (((0,), (0,)), ((), ()))


def _cparams(sem):
    return pltpu.CompilerParams(dimension_semantics=sem, vmem_limit_bytes=VMEM_LIMIT)


def _silu(x):
    return x * jax.nn.sigmoid(x)


def _row_groups(n):
    step = min(n, SUB_ROWS)
    return [pl.ds(i, step) for i in range(0, n, step)]


def _staggered(gens, nstages):
    for t in range(nstages + len(gens) - 1):
        for g in reversed(range(len(gens))):
            if 0 <= t - g < nstages:
                next(gens[g])


def _ada_body(c_ref, w_ref, b_ref, o_ref):
    s = _silu(c_ref[...])
    rows = s.shape[0]
    s_hi = s.astype(BF16)
    s_lo = (s - s_hi.astype(F32)).astype(BF16)
    w = w_ref[...]
    w_hi = w.astype(BF16)
    w_lo = (w - w_hi.astype(F32)).astype(BF16)
    both = jnp.dot(jnp.concatenate([s_hi, s_lo], axis=0), w_hi, preferred_element_type=F32)
    o_ref[...] = (both[:rows] + both[rows:] + jnp.dot(s_hi, w_lo, preferred_element_type=F32)) + b_ref[...]


def _ada(cin, w, b):
    rows, d = cin.shape
    n = w.shape[1]
    nb = n // 4
    return pl.pallas_call(
        _ada_body,
        grid=(n // nb,),
        in_specs=[pl.BlockSpec((rows, d), lambda j: (0, 0)),
                  pl.BlockSpec((d, nb), lambda j: (0, j)),
                  pl.BlockSpec((1, nb), lambda j: (0, j))],
        out_specs=pl.BlockSpec((rows, nb), lambda j: (0, j)),
        out_shape=jax.ShapeDtypeStruct((rows, n), F32),
        compiler_params=_cparams(("arbitrary",)),
        name="ada",
    )(cin, w, b)


def _prep_body(w_ref, up_ref, pw_ref, ps_ref, wpp_ref, w1_ref, wm_ref, wpf_ref, *, o_r, o_q, o_g, o_m, qscale):
    dk = o_g - o_q
    wz = lax.dot_general(w_ref[o_r:o_q, :], up_ref[...], _TN, precision=HIGHEST, preferred_element_type=F32)
    nz = wz.shape[1]
    w1_ref[:, :o_r] = w_ref[:o_r, :].T.astype(BF16)
    w1_ref[:, o_r:o_r + dk] = (w_ref[o_q:o_g, :].T * qscale).astype(BF16)
    w1_ref[:, o_r + dk:o_r + dk + nz] = wz.astype(BF16)
    w1_ref[:, o_r + dk + nz:] = w_ref[o_g:o_m, :].T.astype(BF16)
    wm_ref[...] = w_ref[o_m:, :].T.astype(BF16)
    wpf_ref[...] = jnp.dot(pw_ref[0] * ps_ref[0], wpp_ref[...], precision=HIGHEST,
                           preferred_element_type=F32).astype(BF16)


def _prep(w_in_t, up, pool_w, pool_scale3, w_pool_proj, o_r, o_q, o_g, o_m, qscale):
    n, d = w_in_t.shape
    n1 = o_r + (o_g - o_q) + up.shape[1] + (o_m - o_g)
    groups, ch, _ = pool_w.shape
    tr = d // groups
    dm = w_pool_proj.shape[1]
    return pl.pallas_call(
        functools.partial(_prep_body, o_r=o_r, o_q=o_q, o_g=o_g, o_m=o_m, qscale=qscale),
        grid=(d // tr,),
        in_specs=[pl.BlockSpec((n, tr), lambda i: (0, i)),
                  pl.BlockSpec(up.shape, lambda i: (0, 0)),
                  pl.BlockSpec((1, ch, ch), lambda i: (i, 0, 0)),
                  pl.BlockSpec((1, 1, ch), lambda i: (i, 0, 0)),
                  pl.BlockSpec((ch, dm), lambda i: (i, 0))],
        out_specs=[pl.BlockSpec((tr, n1), lambda i: (i, 0)),
                   pl.BlockSpec((tr, n - o_m), lambda i: (i, 0)),
                   pl.BlockSpec((ch, dm), lambda i: (i, 0))],
        out_shape=[jax.ShapeDtypeStruct((d, n1), BF16), jax.ShapeDtypeStruct((d, n - o_m), BF16),
                   jax.ShapeDtypeStruct((groups * ch, dm), BF16)],
        compiler_params=_cparams(("parallel",)),
        name="prep",
    )(w_in_t, up, pool_w, pool_scale3, w_pool_proj)


def _modulated_norm(x, mult, shift):
    ms = jnp.mean(x * x, axis=-1, keepdims=True)
    return (x * lax.rsqrt(ms + EPS)) * mult + shift


def _log_sigmoid(z):
    return jnp.minimum(z, 0.0) - jnp.log1p(jnp.exp(-jnp.abs(z)))


def _inproj_body(x_ref, mult_ref, shift_ref, w_ref, bz_ref,
                 k_ref, v_ref, q_ref, laf_ref, lab_ref, sg_ref, p_ref, *, dk, dv, pw):
    def sub_tile(rows):
        h = _modulated_norm(x_ref[rows, :], mult_ref[0], shift_ref[0]).astype(BF16)
        yield
        u = jnp.dot(h, w_ref[...], preferred_element_type=F32)
        yield
        o = 0
        k_ref[rows, :] = u[:, o:o + dk].astype(BF16); o += dk
        v_ref[rows, :] = u[:, o:o + dv].astype(BF16); o += dv
        q_ref[rows, :] = u[:, o:o + dk].astype(BF16); o += dk
        zf = u[:, o:o + dk] + bz_ref[0:1, :]; o += dk
        zb = u[:, o:o + dk] + bz_ref[1:2, :]; o += dk
        laf_ref[rows, :] = _log_sigmoid(zf) * (1.0 / GATE_NORMALIZER)
        lab_ref[rows, :] = _log_sigmoid(zb) * (1.0 / GATE_NORMALIZER)
        sg_ref[rows, :] = _silu(u[:, o:o + dv]).astype(BF16); o += dv
        p_ref[rows, :] = u[:, o:o + pw].astype(BF16)
        yield

    _staggered([sub_tile(r) for r in _row_groups(x_ref.shape[0])], 3)


def _inproj(x2, mult, shift, w, bz, rows_per_sample, dk, dv, pw, tm):
    n, d = x2.shape
    tps = rows_per_sample // tm
    row = lambda i: (i, 0)
    vec = lambda i: (i // tps, 0, 0)
    outs = [(dk, BF16), (dv, BF16), (dk, BF16), (dk, F32), (dk, F32), (dv, BF16), (pw, BF16)]
    return pl.pallas_call(
        functools.partial(_inproj_body, dk=dk, dv=dv, pw=pw),
        grid=(n // tm,),
        in_specs=[pl.BlockSpec((tm, d), row),
                  pl.BlockSpec((1, 1, d), vec),
                  pl.BlockSpec((1, 1, d), vec),
                  pl.BlockSpec(w.shape, lambda i: (0, 0), pipeline_mode=pl.Buffered(1)),
                  pl.BlockSpec(bz.shape, lambda i: (0, 0))],
        out_specs=[pl.BlockSpec((tm, c), row) for c, _ in outs],
        out_shape=[jax.ShapeDtypeStruct((n, c), t) for c, t in outs],
        compiler_params=_cparams(("parallel",)),
        name="inproj",
    )(x2, mult, shift, w, bz)


def _gla_direction(k_ref, q_ref, v_ref, la_ref, o_ref, s_ref, reverse, nchunk):
    C = GLA_CHUNK
    hd = HEAD_DK
    cb = MXU_DIM
    lt = nchunk * C
    la = la_ref[0]
    r = lax.broadcasted_iota(I32, (cb, cb), 0)
    c = lax.broadcasted_iota(I32, (cb, cb), 1)
    same = (r // C) == (c // C)
    cum = jnp.where(same & ((c >= r) if reverse else (c <= r)), 1.0, 0.0).astype(BF16)
    la_hi = la.astype(BF16)
    la_lo = (la - la_hi.astype(F32)).astype(BF16)
    la2 = jnp.concatenate([la_hi, la_lo], axis=1)
    bcs = []
    for blk in range(lt // cb):
        part = jnp.dot(cum, la2[blk * cb:(blk + 1) * cb], preferred_element_type=F32)
        bcs.append(part[:, :LANES] + part[:, LANES:])
    ri = lax.broadcasted_iota(I32, (2 * C, C), 0) % C
    ci = lax.broadcasted_iota(I32, (2 * C, C), 1)
    tri = (ci >= ri) if reverse else (ci <= ri)
    lane = lax.broadcasted_iota(I32, (1, LANES), 1)
    m0 = (lane < hd).astype(F32)
    m1 = (lane >= hd).astype(F32)
    sr = lax.broadcasted_iota(I32, (LANES, 2 * LANES), 0)
    sl = lax.broadcasted_iota(I32, (LANES, 2 * LANES), 1)
    smask = ((sr < hd) == (sl < LANES)).astype(F32)
    kt = k_ref[0].astype(F32)
    qt = q_ref[0].astype(F32)
    zero_v = jnp.zeros((C, LANES), BF16)
    order = list(range(nchunk - 1, -1, -1) if reverse else range(nchunk))
    intra, qds, kvs, decs, q2s, kss, kws, scs = {}, {}, {}, {}, {}, {}, {}, {}
    for ch in order:
        lo = ch * C
        b = bcs[lo // cb][lo % cb:lo % cb + C]
        last = b[0:1] if reverse else b[C - 1:C]
        mid = b[C // 2:C // 2 + 1] if reverse else b[C // 2 - 1:C // 2]
        kc = kt[lo:lo + C]
        qc = qt[lo:lo + C]
        qs = qc * jnp.exp(b - mid)
        kss[ch] = (kc * jnp.exp(mid - b)).astype(BF16)
        qds[ch] = (qc * jnp.exp(b)).astype(BF16)
        kws[ch] = (kc * jnp.exp(last - b)).astype(BF16)
        decs[ch] = last
        q2s[ch] = jnp.concatenate([qs * m0, qs * m1], axis=0).astype(BF16)
    yield
    for ch in order:
        scs[ch] = lax.dot_general(q2s[ch], kss[ch], _NT, preferred_element_type=F32)
    yield
    for ch in order:
        v2 = v_ref[0, ch * C:(ch + 1) * C, :]
        kvs[ch] = lax.dot_general(kws[ch], v2, _TN, preferred_element_type=F32) * smask
    yield
    for ch in order:
        sc = jnp.where(tri, scs[ch], 0.0).astype(BF16)
        sc2 = jnp.concatenate([sc[:C], sc[C:]], axis=1)
        v2 = v_ref[0, ch * C:(ch + 1) * C, :]
        vbd = jnp.concatenate([jnp.concatenate([v2[:, :LANES], zero_v], axis=1),
                               jnp.concatenate([zero_v, v2[:, LANES:]], axis=1)], axis=0)
        intra[ch] = jnp.dot(sc2, vbd, preferred_element_type=F32)
    yield
    pad = jnp.zeros((LANES - nchunk, LANES), F32)
    dec_cols = jnp.exp(jnp.concatenate([decs[ch] for ch in range(nchunk)] + [pad], axis=0).T)
    st = s_ref[...]
    starts = {}
    for ch in order:
        starts[ch] = st.astype(BF16)
        st = st * dec_cols[:, ch:ch + 1] + kvs[ch]
    s_ref[...] = st
    yield
    for ch in order:
        inter = jnp.dot(qds[ch], starts[ch], preferred_element_type=F32)
        o_ref[0, ch * C:(ch + 1) * C, :] = inter + intra[ch]
    yield


def _gla_body(kf, qf, vf, laf, kb, qb, vb, lab, h0f, h0b, of, ob, hf_out, hb_out, sf, sb, *, nchunk):
    i = pl.program_id(2)

    @pl.when(i == 0)
    def _():
        sf[...] = h0f[0, 0]
        sb[...] = h0b[0, 0]

    sweeps = [_gla_direction(kf, qf, vf, laf, of, sf, False, nchunk),
              _gla_direction(kb, qb, vb, lab, ob, sb, True, nchunk)]
    for _ in range(GLA_STAGES):
        for sweep in sweeps:
            next(sweep)

    @pl.when(i == pl.num_programs(2) - 1)
    def _():
        hf_out[0, 0] = sf[...]
        hb_out[0, 0] = sb[...]


def _gla(k, q, v, laf, lab, h0f, h0b, lt):
    bsz, l, _ = k.shape
    pairs = h0f.shape[1]
    nt = l // lt
    fwd = lambda b, hp, i: (b, i, hp)
    bwd = lambda b, hp, i: (b, nt - 1 - i, hp)
    st = lambda b, hp, i: (b, hp, 0, 0)
    kq = lambda m: pl.BlockSpec((1, lt, LANES), m)
    vv = lambda m: pl.BlockSpec((1, lt, 2 * LANES), m)
    sspec = pl.BlockSpec((1, 1, LANES, 2 * LANES), st)
    return pl.pallas_call(
        functools.partial(_gla_body, nchunk=lt // GLA_CHUNK),
        grid=(bsz, pairs, nt),
        in_specs=[kq(fwd), kq(fwd), vv(fwd), kq(fwd), kq(bwd), kq(bwd), vv(bwd), kq(bwd), sspec, sspec],
        out_specs=[vv(fwd), vv(bwd), sspec, sspec],
        out_shape=[jax.ShapeDtypeStruct(v.shape, F32), jax.ShapeDtypeStruct(v.shape, F32),
                   jax.ShapeDtypeStruct(h0f.shape, F32), jax.ShapeDtypeStruct(h0b.shape, F32)],
        scratch_shapes=[pltpu.VMEM((LANES, 2 * LANES), F32), pltpu.VMEM((LANES, 2 * LANES), F32)],
        compiler_params=_cparams(("parallel", "parallel", "arbitrary")),
        name="gla",
    )(k, q, v, laf, k, q, v, lab, h0f, h0b)


def _pool_body(p_ref, o_ref, s1_ref, *, half, rows):
    xb = p_ref[0]
    l, ch = xb.shape

    def inv_counts(pos, n):
        return 1.0 / (jnp.minimum(pos + half, n) - jnp.maximum(pos - half, 0)).astype(F32)

    blk = MXU_DIM
    r = lax.broadcasted_iota(I32, (blk, blk), 0)
    c = lax.broadcasted_iota(I32, (blk, blk), 1)
    band = ((r // GRID_W == c // GRID_W) & (c - r >= -half) & (c - r <= half - 1)).astype(BF16)
    for i in range(0, l // blk, 2):
        pair = jnp.concatenate([xb[i * blk:(i + 1) * blk], xb[(i + 1) * blk:(i + 2) * blk]], axis=1)
        sums = jnp.dot(band, pair, preferred_element_type=F32)
        s1_ref[i * blk:(i + 1) * blk, :] = sums[:, :ch]
        s1_ref[(i + 1) * blk:(i + 2) * blk, :] = sums[:, ch:]
    s1 = s1_ref[...].reshape(rows, GRID_W, ch)

    def shifted(a, s):
        z = jnp.zeros((abs(s),) + a.shape[1:], a.dtype)
        return jnp.concatenate([a[s:], z], axis=0) if s > 0 else jnp.concatenate([z, a[:s]], axis=0)

    fwd = s1
    bwd = shifted(s1, -1)
    s = 1
    while s < half:
        fwd = fwd + shifted(fwd, s)
        bwd = bwd + shifted(bwd, -s)
        s *= 2
    inv_r = inv_counts(lax.broadcasted_iota(I32, (rows, 1, ch), 0), rows)
    inv_c = inv_counts(lax.broadcasted_iota(I32, (1, GRID_W, ch), 1), GRID_W)
    pooled = (fwd + bwd) * inv_r * inv_c - xb.astype(F32).reshape(rows, GRID_W, ch)
    o_ref[0] = pooled.reshape(l, ch).astype(BF16)


def _pool_groups_body(p_ref, o_ref, s1_ref, *, rows):
    for gi, window in enumerate(POOL_WINDOWS):
        @pl.when(pl.program_id(1) == gi)
        def _(half=window // 2):
            _pool_body(p_ref, o_ref, s1_ref, half=half, rows=rows)


def _pool(pin, ch):
    bsz, l, pw = pin.shape
    return pl.pallas_call(
        functools.partial(_pool_groups_body, rows=l // GRID_W),
        grid=(bsz, pw // ch),
        in_specs=[pl.BlockSpec((1, l, ch), lambda b, g: (b, 0, g))],
        out_specs=pl.BlockSpec((1, l, ch), lambda b, g: (b, 0, g)),
        out_shape=jax.ShapeDtypeStruct((bsz, l, pw), BF16),
        scratch_shapes=[pltpu.VMEM((l, ch), F32)],
        compiler_params=_cparams(("parallel", "parallel")),
        name="pool",
    )(pin)


def _merge_body(x_ref, m1_ref, s1_ref, g1_ref, m2_ref, s2_ref,
                of_ref, ob_ref, sg_ref, mx_ref,
                wm_ref, wgla_ref, wpool_ref, wout_ref, gn_ref, wr_ref,
                x1_ref, h2_ref, aff_ref, *, heads, ne):
    d = x_ref.shape[1]

    def sub_tile(rows):
        x = x_ref[rows, :]
        h = _modulated_norm(x, m1_ref[0], s1_ref[0]).astype(BF16)
        o = of_ref[rows, :] + ob_ref[rows, :]
        sg = sg_ref[rows, :].astype(F32)
        og = []
        for j in range(heads):
            oj = o[:, j * LANES:(j + 1) * LANES]
            oj = oj * lax.rsqrt(jnp.mean(oj * oj, axis=-1, keepdims=True) + EPS) * gn_ref[...]
            og.append((oj * sg[:, j * LANES:(j + 1) * LANES]).astype(BF16))
        og = jnp.concatenate(og, axis=1)
        mixed = mx_ref[rows, :]
        yield
        gates = jnp.dot(h, wm_ref[...], preferred_element_type=F32)
        bg = jnp.dot(og, wgla_ref[...], preferred_element_type=F32)
        bp = jnp.dot(mixed, wpool_ref[...], preferred_element_type=F32)
        yield
        gates = jax.nn.sigmoid(gates)
        z = (gates[:, :d] * bg + gates[:, d:] * bp).astype(BF16)
        yield
        y = jnp.dot(z, wout_ref[...], preferred_element_type=F32)
        yield
        x1 = x + g1_ref[0] * y
        x1_ref[rows, :] = x1
        h2 = _modulated_norm(x1, m2_ref[0], s2_ref[0])
        hi = h2.astype(BF16)
        h2_ref[rows, :] = hi
        lo = (h2 - hi.astype(F32)).astype(BF16)
        yield
        lg = (jnp.dot(hi, wr_ref[...], preferred_element_type=F32)
              + jnp.dot(lo, wr_ref[...], preferred_element_type=F32))
        yield
        lgt = lg.T
        logit = lgt[0:ne] + lgt[ne:2 * ne]
        mx = jnp.max(logit, axis=0, keepdims=True)
        ex = jnp.exp(logit - mx)
        aff_ref[0, :, rows] = ex / jnp.sum(ex, axis=0, keepdims=True)
        yield

    _staggered([sub_tile(r) for r in _row_groups(x_ref.shape[0])], 7)


def _merge(x2, vecs, of, ob, sg, mixed, wm, wgla, wpool, wout, gn, wr, rows_per_sample, heads, ne, tm):
    n, d = x2.shape
    bsz = n // rows_per_sample
    tps = rows_per_sample // tm
    row = lambda i: (i, 0)
    vec = lambda i: (i // tps, 0, 0)
    full = lambda a: pl.BlockSpec(a.shape, lambda i: (0,) * a.ndim, pipeline_mode=pl.Buffered(1))
    gv = of.shape[-1]
    return pl.pallas_call(
        functools.partial(_merge_body, heads=heads, ne=ne),
        grid=(n // tm,),
        in_specs=[pl.BlockSpec((tm, d), row)] + [pl.BlockSpec((1, 1, d), vec)] * 5
                 + [pl.BlockSpec((tm, gv), row)] * 4
                 + [full(wm), full(wgla), full(wpool), full(wout), full(gn), full(wr)],
        out_specs=[pl.BlockSpec((tm, d), row),
                   pl.BlockSpec((tm, d), row),
                   pl.BlockSpec((1, ne, tm), lambda i: (i // tps, 0, i % tps))],
        out_shape=[jax.ShapeDtypeStruct((n, d), F32),
                   jax.ShapeDtypeStruct((n, d), BF16),
                   jax.ShapeDtypeStruct((bsz, ne, rows_per_sample), F32)],
        compiler_params=_cparams(("parallel",)),
        name="merge",
    )(x2, *vecs, of, ob, sg, mixed, wm, wgla, wpool, wout, gn, wr)


def _route_body(aff_ref, pos_ref, off_ref, cnt_ref, *, cap, ntb):
    a = aff_ref[0]
    ne, l = a.shape
    blk = ROUTE_BLK

    def bisect(i, v):
        cand = v | jnp.left_shift(jnp.int32(1), 30 - i)
        cnt = jnp.sum((a >= lax.bitcast_convert_type(cand, F32)).astype(F32), axis=1, keepdims=True)
        return jnp.where(cnt >= cap, cand, v)

    thr = lax.bitcast_convert_type(lax.fori_loop(0, 31, bisect, jnp.zeros((ne, 1), I32)), F32)
    gt = a > thr
    tie = a == thr
    need = cap - jnp.sum(gt.astype(F32), axis=1, keepdims=True)

    r = lax.broadcasted_iota(I32, (blk, blk), 0)
    c = lax.broadcasted_iota(I32, (blk, blk), 1)
    upper = (r <= c).astype(BF16)
    lane = lax.broadcasted_iota(I32, (1, LANES), 1)

    def prefix(mask_f):
        run = jnp.zeros((ne, 1), F32)
        offs = jnp.zeros((ne, LANES), F32)
        for tb in range(ntb):
            m = mask_f[:, tb * blk:(tb + 1) * blk].astype(BF16)
            loc = jnp.dot(m, upper, preferred_element_type=F32)
            cnt_ref[:, tb * blk:(tb + 1) * blk] = loc + run
            offs = jnp.where(lane == tb, run, offs)
            run = run + loc[:, blk - 1:blk]
        return jnp.where(lane >= ntb, run, offs)

    tie_f = tie.astype(F32)
    prefix(tie_f)
    tie_excl = cnt_ref[...] - tie_f
    sel = gt | (tie & (tie_excl < need))
    offs = prefix(sel.astype(F32))
    pos_ref[0] = jnp.where(sel, cnt_ref[...] - 1.0, -1.0).astype(I32)
    off_ref[0] = offs.astype(I32)


def _route(aff, cap):
    bsz, ne, l = aff.shape
    spec = lambda s: pl.BlockSpec((1,) + s, lambda b: (b, 0, 0))
    return pl.pallas_call(
        functools.partial(_route_body, cap=cap, ntb=l // ROUTE_BLK),
        grid=(bsz,),
        in_specs=[spec((ne, l))],
        out_specs=[spec((ne, l)), spec((ne, LANES))],
        out_shape=[jax.ShapeDtypeStruct((bsz, ne, l), I32),
                   jax.ShapeDtypeStruct((bsz, ne, LANES), I32)],
        scratch_shapes=[pltpu.VMEM((ne, l), F32)],
        compiler_params=_cparams(("parallel",)),
        name="route",
    )(aff)


def _window_plan(off_ref, bb, tt, experts, ne):
    lows = [off_ref[(bb * ne + e) * LANES + tt] & -BF16_ROWS for e in experts]
    ends = [off_ref[(bb * ne + e) * LANES + tt + 1] for e in experts]
    return lows, ends


def _window_rounds(lows, ends):
    rounds = jnp.int32(0)
    for lo, hi in zip(lows, ends):
        rounds = jnp.maximum(rounds, lax.div(hi - lo + (SLOT_WIN - 1), jnp.int32(SLOT_WIN)))
    return rounds


def _gatherx_body(off_ref, h_ref, pos_ref, xs_ref, *, ne, cap, eg):
    b = pl.program_id(0)
    g = pl.program_id(1)
    tb = pl.program_id(2)
    win = SLOT_WIN

    @pl.when(tb == 0)
    def _():
        xs_ref[...] = jnp.zeros(xs_ref.shape, BF16)

    j_col = lax.broadcasted_iota(I32, (win, 1), 0)
    t = ROUTE_BLK
    for sub in range(h_ref.shape[0] // t):
        lows, ends = _window_plan(off_ref, b, tb * (h_ref.shape[0] // t) + sub, [g * eg + k for k in range(eg)], ne)
        rounds = _window_rounds(lows, ends)

        def one_round(r, carry, lows=lows, sub=sub):
            starts = [pl.multiple_of(jnp.minimum(lows[k] + r * win, cap - win), BF16_ROWS) for k in range(eg)]
            pieces = []
            for k in range(eg):
                p = pos_ref[0, k:k + 1, sub * t:(sub + 1) * t]
                hit = (p - starts[k] == j_col) & (p >= lows[k] + r * win)
                pieces.append(jnp.where(hit, 1.0, 0.0).astype(BF16))
            sel = jnp.concatenate(pieces, axis=0)
            rows = jnp.dot(sel, h_ref[sub * t:(sub + 1) * t, :],
                           preferred_element_type=F32).astype(BF16)
            for k in range(eg):
                dst = (0, k, pl.ds(starts[k], win), slice(None))
                xs_ref[dst] = xs_ref[dst] + rows[k * win:(k + 1) * win]
            return carry

        lax.fori_loop(0, rounds, one_round, 0)


def _gatherx(off_flat, h2, pos, cap):
    n, d = h2.shape
    bsz, ne, l = pos.shape
    t = min(l, TOKEN_TILE)
    ntb = l // t
    eg = SUBLANES
    grid_spec = pltpu.PrefetchScalarGridSpec(
        num_scalar_prefetch=1,
        grid=(bsz, ne // eg, ntb),
        in_specs=[pl.BlockSpec((t, d), lambda b, g, i, off: (b * ntb + i, 0)),
                  pl.BlockSpec((1, eg, t), lambda b, g, i, off: (b, g, i))],
        out_specs=pl.BlockSpec((1, eg, cap, d), lambda b, g, i, off: (b, g, 0, 0)),
    )
    return pl.pallas_call(
        functools.partial(_gatherx_body, ne=ne, cap=cap, eg=eg),
        grid_spec=grid_spec,
        out_shape=jax.ShapeDtypeStruct((bsz, ne, cap, d), BF16),
        compiler_params=_cparams(("parallel", "parallel", "arbitrary")),
        name="gatherx",
    )(off_flat, h2, pos)


def _moe_body(xs_ref, wg_ref, wu_ref, wd_ref, y_ref, *, rc):
    wg = wg_ref[0].astype(BF16)
    wu = wu_ref[0].astype(BF16)
    wd = wd_ref[0].astype(BF16)
    cap = xs_ref.shape[2]
    acts = []
    for ch in range(cap // rc):
        xs = xs_ref[0, 0, ch * rc:(ch + 1) * rc, :]
        gate = jnp.dot(xs, wg, preferred_element_type=F32)
        up = jnp.dot(xs, wu, preferred_element_type=F32)
        acts.append((_silu(gate) * up).astype(BF16))
    for ch in range(cap // rc):
        y_ref[0, 0, ch * rc:(ch + 1) * rc, :] = jnp.dot(acts[ch], wd, preferred_element_type=F32).astype(BF16)


def _moe(xs, wg, wu, wd):
    bsz, ne, cap, d = xs.shape
    de = wg.shape[2]
    slot = pl.BlockSpec((1, 1, cap, d), lambda e, b: (b, e, 0, 0))
    return pl.pallas_call(
        functools.partial(_moe_body, rc=min(cap, 2 * MXU_DIM)),
        grid=(ne, bsz),
        in_specs=[slot,
                  pl.BlockSpec((1, d, de), lambda e, b: (e, 0, 0)),
                  pl.BlockSpec((1, d, de), lambda e, b: (e, 0, 0)),
                  pl.BlockSpec((1, de, d), lambda e, b: (e, 0, 0))],
        out_specs=slot,
        out_shape=jax.ShapeDtypeStruct((bsz, ne, cap, d), BF16),
        compiler_params=_cparams(("parallel", "arbitrary")),
        name="moe",
    )(xs, wg, wu, wd)


def _combine_body(off_ref, x1_ref, g2_ref, fg_ref, pos_ref, aff_ref, y_hbm, o_ref, ybuf, sem,
                  *, ne, cap, ntb):
    b = pl.program_id(0)
    tb = pl.program_id(1)
    step = b * ntb + tb
    nsteps = pl.num_programs(0) * ntb
    slot = step % 2
    spare = 2
    win = SLOT_WIN

    def block_lows(bb, tt):
        return _window_plan(off_ref, bb, tt, range(ne), ne)[0]

    def window_copy(bb, e, start, buf):
        return pltpu.make_async_copy(y_hbm.at[bb, e, pl.ds(start, win), :],
                                     ybuf.at[buf, pl.ds(e * win, win), :], sem.at[buf, e])

    def window_starts(lows, r):
        return [pl.multiple_of(jnp.minimum(lows[e] + r * win, cap - win), BF16_ROWS) for e in range(ne)]

    def start_first_round(bb, tt, buf):
        starts = window_starts(block_lows(bb, tt), 0)
        for e in range(ne):
            window_copy(bb, e, starts[e], buf).start()

    @pl.when(step == 0)
    def _():
        start_first_round(b, tb, 0)

    lows, ends = _window_plan(off_ref, b, tb, range(ne), ne)
    rounds = _window_rounds(lows, ends)
    j_col = lax.broadcasted_iota(I32, (win, 1), 0)

    def expand(r, starts, buf):
        pieces = []
        for e in range(ne):
            p = pos_ref[0, e:e + 1, :]
            valid = p >= lows[e] + r * win
            hit = (p - starts[e] == j_col) & valid
            pieces.append(jnp.where(hit, aff_ref[0, e:e + 1, :], 0.0).astype(BF16))
        pmat = jnp.concatenate(pieces, axis=0)
        return lax.dot_general(pmat, ybuf[buf], _TN, preferred_element_type=F32)

    nxt = jnp.minimum(step + 1, nsteps - 1)
    start_first_round(lax.div(nxt, jnp.int32(ntb)), lax.rem(nxt, jnp.int32(ntb)), 1 - slot)
    starts0 = window_starts(lows, 0)
    for e in range(ne):
        window_copy(b, e, starts0[e], slot).wait()
    moe = expand(0, starts0, slot)

    def extra_round(r, acc):
        starts = window_starts(lows, r)
        for e in range(ne):
            window_copy(b, e, starts[e], spare).start()
        for e in range(ne):
            window_copy(b, e, starts[e], spare).wait()
        return acc + expand(r, starts, spare)

    moe = lax.fori_loop(1, rounds, extra_round, moe)
    x2 = x1_ref[...] + g2_ref[0] * moe
    ms = jnp.mean(x2 * x2, axis=-1, keepdims=True)
    o_ref[...] = x2 * lax.rsqrt(ms + EPS) * fg_ref[...]

    @pl.when(step == nsteps - 1)
    def _():
        for e in range(ne):
            window_copy(b, e, starts0[e], 1 - slot).wait()


def _combine(off_flat, x1, g2, fg, pos, aff, y, rows_per_sample):
    n, d = x1.shape
    bsz, ne, cap, _ = y.shape
    t = ROUTE_BLK
    ntb = rows_per_sample // t
    grid_spec = pltpu.PrefetchScalarGridSpec(
        num_scalar_prefetch=1,
        grid=(bsz, ntb),
        in_specs=[pl.BlockSpec((t, d), lambda b, i, off: (b * ntb + i, 0)),
                  pl.BlockSpec((1, 1, d), lambda b, i, off: (b, 0, 0)),
                  pl.BlockSpec((1, d), lambda b, i, off: (0, 0)),
                  pl.BlockSpec((1, ne, t), lambda b, i, off: (b, 0, i)),
                  pl.BlockSpec((1, ne, t), lambda b, i, off: (b, 0, i)),
                  pl.BlockSpec(memory_space=pl.ANY)],
        out_specs=pl.BlockSpec((t, d), lambda b, i, off: (b * ntb + i, 0)),
        scratch_shapes=[pltpu.VMEM((3, ne * SLOT_WIN, d), BF16), pltpu.SemaphoreType.DMA((3, ne))],
    )
    return pl.pallas_call(
        functools.partial(_combine_body, ne=ne, cap=cap, ntb=ntb),
        grid_spec=grid_spec,
        out_shape=jax.ShapeDtypeStruct((n, d), F32),
        compiler_params=_cparams(("arbitrary", "arbitrary")),
        name="combine",
    )(off_flat, x1, g2, fg, pos, aff, y)


def kernel(x, c, ctx, c_ctx, ada_w, ada_b, norm1_g, norm2_g, w_in, w_decay_up, b_decay, gla_norm_g,
           w_gla_proj, pool_w, pool_scale, w_pool_proj, w_out, w_router, w_gate_e, w_up_e, w_down_e,
           final_norm_g):
    assert ada_w.shape[0] == 1, "single-layer block"
    bsz, l, d = x.shape
    lc = ctx.shape[1]
    rank, dk = w_decay_up.shape[2], w_decay_up.shape[3]
    dvh = gla_norm_g.shape[1]
    dv = w_gla_proj.shape[1]
    heads = dv // dvh
    groups, ch = pool_w.shape[1], pool_w.shape[2]
    pw = groups * ch
    ne = w_router.shape[2]
    cap = EC_CAPACITY * l // ne
    assert dk // heads == HEAD_DK and dvh == LANES and ch == LANES and heads % 2 == 0
    assert l % ROUTE_BLK == 0 and l // ROUTE_BLK < LANES and cap >= SLOT_WIN and cap % BF16_ROWS == 0
    assert GRID_W == GLA_CHUNK and MXU_DIM % GRID_W == 0 and (l // MXU_DIM) % 2 == 0

    cin = jnp.zeros((8, d), F32).at[:bsz].set(c).at[bsz].set(c_ctx)
    mods = _ada(cin, ada_w[0], ada_b[0][None, :])
    sh1, sc1, gt1, sh2, sc2, gt2 = [mods[:, i * d:(i + 1) * d] for i in range(6)]
    vec3 = lambda a: a[:, None, :]
    mult1 = norm1_g[0][None, :] * (1.0 + sc1)
    mult2 = norm2_g[0][None, :] * (1.0 + sc2)

    o_r = dk + dv
    o_q = o_r + 2 * rank
    o_g = o_q + dk
    o_p = o_g + dv
    o_m = o_p + pw
    up = jnp.zeros((2 * rank, 2 * dk), F32)
    up = up.at[:rank, :dk].set(w_decay_up[0, 0]).at[rank:, dk:].set(w_decay_up[0, 1])
    psc = pool_scale[0].reshape(groups, 1, ch)
    w1, wm, wpool = _prep(w_in[0].T, up, pool_w[0], psc, w_pool_proj[0], o_r, o_q, o_g, o_m,
                          float(dk // heads) ** -0.5)
    bz = b_decay[0]

    zero_state = jnp.zeros((bsz, heads // 2, LANES, 2 * LANES), F32)
    cm = jnp.broadcast_to(vec3(mult1[bsz:bsz + 1]), (bsz, 1, d))
    cs = jnp.broadcast_to(vec3(sh1[bsz:bsz + 1]), (bsz, 1, d))
    ck, cv, _, claf, clab, _, _ = _inproj(ctx.reshape(bsz * lc, d), cm, cs, w1, bz, lc, dk, dv, pw, min(lc, CTX_TILE))
    r3 = lambda a, n: a.reshape(bsz, n, a.shape[-1])
    _, _, h_f, h_b = _gla(r3(ck, lc), r3(ck, lc), r3(cv, lc), r3(claf, lc), r3(clab, lc),
                          zero_state, zero_state, min(lc, CTX_TILE))

    x2 = x.reshape(bsz * l, d)
    tm = min(l, TOKEN_TILE)
    k, v, q, laf, lab, sg, pin = _inproj(x2, vec3(mult1[:bsz]), vec3(sh1[:bsz]), w1, bz, l, dk, dv, pw, tm)
    of, ob, _, _ = _gla(r3(k, l), r3(q, l), r3(v, l), r3(laf, l), r3(lab, l), h_f, h_b, min(l, GLA_TILE))
    pooled = _pool(r3(pin, l), ch).reshape(bsz * l, pw)

    wr_hi = w_router[0].astype(BF16)
    wr_lo = (w_router[0] - wr_hi.astype(F32)).astype(BF16)
    wr = jnp.zeros((d, LANES), BF16).at[:, :ne].set(wr_hi).at[:, ne:2 * ne].set(wr_lo)
    vecs = [vec3(mult1[:bsz]), vec3(sh1[:bsz]), vec3(gt1[:bsz]), vec3(mult2[:bsz]), vec3(sh2[:bsz])]
    x1, h2, aff = _merge(x2, vecs, of.reshape(bsz * l, dv), ob.reshape(bsz * l, dv), sg, pooled,
                          wm, w_gla_proj[0].astype(BF16), wpool, w_out[0].astype(BF16),
                          gla_norm_g[0][None, :], wr, l, heads, ne, tm)

    pos, offs = _route(aff, cap)
    off_flat = offs.reshape(-1)
    xs = _gatherx(off_flat, h2, pos, cap)
    y = _moe(xs, w_gate_e[0], w_up_e[0], w_down_e[0])
    out = _combine(off_flat, x1, vec3(gt2[:bsz]), final_norm_g[None, :], pos, aff, y, l)
    return out.reshape(bsz, l, d)
```

```python
import functools

import jax
import jax.numpy as jnp
from jax import lax
from jax.experimental import pallas as pl
from jax.experimental.pallas import tpu as pltpu

F32 = jnp.float32
BF16 = jnp.bfloat16
I32 = jnp.int32
HIGHEST = lax.Precision.HIGHEST

EPS = 1e-6
GRID_W = 64
GLA_CHUNK = 64
GLA_STAGES = 6
GLA_BATCH = 8
GATE_NORMALIZER = 16.0
POOL_WINDOWS = (2, 4, 8, 16)
EC_CAPACITY = 2

LANES = 128
SUBLANES = 8
BF16_ROWS = 16
MXU_DIM = 256
VMEM_BYTES = 64 * 1024 * 1024
VMEM_LIMIT = VMEM_BYTES * 7 // 8

HEAD_DK = LANES // 2
ROUTE_BLK = MXU_DIM
SLOT_WIN = 64
SUB_ROWS = MXU_DIM
TOKEN_TILE = 1024
GLA_TILE = 2048
CTX_TILE = 256

_NT = (((1,), (1,)), ((), ()))
_TN = (((0,), (0,)), ((), ()))


def _cparams(sem):
    return pltpu.CompilerParams(dimension_semantics=sem, vmem_limit_bytes=VMEM_LIMIT)


def _silu(x):
    return x * jax.nn.sigmoid(x)


def _row_groups(n):
    step = min(n, SUB_ROWS)
    return [pl.ds(i, step) for i in range(0, n, step)]


def _staggered(gens, nstages):
    for t in range(nstages + len(gens) - 1):
        for g in reversed(range(len(gens))):
            if 0 <= t - g < nstages:
                next(gens[g])


def _ada_body(c_ref, w_ref, b_ref, o_ref):
    s = _silu(c_ref[...])
    rows = s.shape[0]
    s_hi = s.astype(BF16)
    s_lo = (s - s_hi.astype(F32)).astype(BF16)
    w = w_ref[...]
    w_hi = w.astype(BF16)
    w_lo = (w - w_hi.astype(F32)).astype(BF16)
    both = jnp.dot(jnp.concatenate([s_hi, s_lo], axis=0), w_hi, preferred_element_type=F32)
    o_ref[...] = (both[:rows] + both[rows:] + jnp.dot(s_hi, w_lo, preferred_element_type=F32)) + b_ref[...]


def _ada(cin, w, b):
    rows, d = cin.shape
    n = w.shape[1]
    nb = n // 4
    return pl.pallas_call(
        _ada_body,
        grid=(n // nb,),
        in_specs=[pl.BlockSpec((rows, d), lambda j: (0, 0)),
                  pl.BlockSpec((d, nb), lambda j: (0, j)),
                  pl.BlockSpec((1, nb), lambda j: (0, j))],
        out_specs=pl.BlockSpec((rows, nb), lambda j: (0, j)),
        out_shape=jax.ShapeDtypeStruct((rows, n), F32),
        compiler_params=_cparams(("arbitrary",)),
        name="ada",
    )(cin, w, b)


def _prep_body(w_ref, up_ref, pw_ref, ps_ref, wpp_ref, w1_ref, wm_ref, wpf_ref, *, o_r, o_q, o_g, o_m, qscale):
    dk = o_g - o_q
    wz = lax.dot_general(w_ref[o_r:o_q, :], up_ref[...], _TN, precision=HIGHEST, preferred_element_type=F32)
    nz = wz.shape[1]
    w1_ref[:, :o_r] = w_ref[:o_r, :].T.astype(BF16)
    w1_ref[:, o_r:o_r + dk] = (w_ref[o_q:o_g, :].T * qscale).astype(BF16)
    w1_ref[:, o_r + dk:o_r + dk + nz] = wz.astype(BF16)
    w1_ref[:, o_r + dk + nz:] = w_ref[o_g:o_m, :].T.astype(BF16)
    wm_ref[...] = w_ref[o_m:, :].T.astype(BF16)
    wpf_ref[...] = jnp.dot(pw_ref[0] * ps_ref[0], wpp_ref[...], precision=HIGHEST,
                           preferred_element_type=F32).astype(BF16)


def _prep(w_in_t, up, pool_w, pool_scale3, w_pool_proj, o_r, o_q, o_g, o_m, qscale):
    n, d = w_in_t.shape
    n1 = o_r + (o_g - o_q) + up.shape[1] + (o_m - o_g)
    groups, ch, _ = pool_w.shape
    tr = d // groups
    dm = w_pool_proj.shape[1]
    return pl.pallas_call(
        functools.partial(_prep_body, o_r=o_r, o_q=o_q, o_g=o_g, o_m=o_m, qscale=qscale),
        grid=(d // tr,),
        in_specs=[pl.BlockSpec((n, tr), lambda i: (0, i)),
                  pl.BlockSpec(up.shape, lambda i: (0, 0)),
                  pl.BlockSpec((1, ch, ch), lambda i: (i, 0, 0)),
                  pl.BlockSpec((1, 1, ch), lambda i: (i, 0, 0)),
                  pl.BlockSpec((ch, dm), lambda i: (i, 0))],
        out_specs=[pl.BlockSpec((tr, n1), lambda i: (i, 0)),
                   pl.BlockSpec((tr, n - o_m), lambda i: (i, 0)),
                   pl.BlockSpec((ch, dm), lambda i: (i, 0))],
        out_shape=[jax.ShapeDtypeStruct((d, n1), BF16), jax.ShapeDtypeStruct((d, n - o_m), BF16),
                   jax.ShapeDtypeStruct((groups * ch, dm), BF16)],
        compiler_params=_cparams(("parallel",)),
        name="prep",
    )(w_in_t, up, pool_w, pool_scale3, w_pool_proj)


def _modulated_norm(x, mult, shift):
    ms = jnp.mean(x * x, axis=-1, keepdims=True)
    return (x * lax.rsqrt(ms + EPS)) * mult + shift


def _log_sigmoid(z):
    return jnp.minimum(z, 0.0) - jnp.log1p(jnp.exp(-jnp.abs(z)))


def _inproj_body(x_ref, mult_ref, shift_ref, w_ref, bz_ref,
                 k_ref, v_ref, q_ref, laf_ref, lab_ref, sg_ref, p_ref, *, dk, dv, pw):
    def sub_tile(rows):
        h = _modulated_norm(x_ref[rows, :], mult_ref[0], shift_ref[0]).astype(BF16)
        yield
        u = jnp.dot(h, w_ref[...], preferred_element_type=F32)
        yield
        o = 0
        k_ref[rows, :] = u[:, o:o + dk].astype(BF16); o += dk
        v_ref[rows, :] = u[:, o:o + dv].astype(BF16); o += dv
        q_ref[rows, :] = u[:, o:o + dk].astype(BF16); o += dk
        zf = u[:, o:o + dk] + bz_ref[0:1, :]; o += dk
        zb = u[:, o:o + dk] + bz_ref[1:2, :]; o += dk
        laf_ref[rows, :] = _log_sigmoid(zf) * (1.0 / GATE_NORMALIZER)
        lab_ref[rows, :] = _log_sigmoid(zb) * (1.0 / GATE_NORMALIZER)
        sg_ref[rows, :] = _silu(u[:, o:o + dv]).astype(BF16); o += dv
        p_ref[rows, :] = u[:, o:o + pw].astype(BF16)
        yield

    _staggered([sub_tile(r) for r in _row_groups(x_ref.shape[0])], 3)


def _inproj(x2, mult, shift, w, bz, rows_per_sample, dk, dv, pw, tm):
    n, d = x2.shape
    tps = rows_per_sample // tm
    row = lambda i: (i, 0)
    vec = lambda i: (i // tps, 0, 0)
    outs = [(dk, BF16), (dv, BF16), (dk, BF16), (dk, F32), (dk, F32), (dv, BF16), (pw, BF16)]
    return pl.pallas_call(
        functools.partial(_inproj_body, dk=dk, dv=dv, pw=pw),
        grid=(n // tm,),
        in_specs=[pl.BlockSpec((tm, d), row),
                  pl.BlockSpec((1, 1, d), vec),
                  pl.BlockSpec((1, 1, d), vec),
                  pl.BlockSpec(w.shape, lambda i: (0, 0), pipeline_mode=pl.Buffered(1)),
                  pl.BlockSpec(bz.shape, lambda i: (0, 0))],
        out_specs=[pl.BlockSpec((tm, c), row) for c, _ in outs],
        out_shape=[jax.ShapeDtypeStruct((n, c), t) for c, t in outs],
        compiler_params=_cparams(("parallel",)),
        name="inproj",
    )(x2, mult, shift, w, bz)


def _gla_direction(k_ref, q_ref, v_ref, la_ref, o_ref, s_ref, reverse, nchunk):
    C = GLA_CHUNK
    hd = HEAD_DK
    cb = MXU_DIM
    lt = nchunk * C
    la = la_ref[0]
    r = lax.broadcasted_iota(I32, (cb, cb), 0)
    c = lax.broadcasted_iota(I32, (cb, cb), 1)
    same = (r // C) == (c // C)
    cum = jnp.where(same & ((c >= r) if reverse else (c <= r)), 1.0, 0.0).astype(BF16)
    la_hi = la.astype(BF16)
    la_lo = (la - la_hi.astype(F32)).astype(BF16)
    la2 = jnp.concatenate([la_hi, la_lo], axis=1)
    bcs = []
    for blk in range(lt // cb):
        part = jnp.dot(cum, la2[blk * cb:(blk + 1) * cb], preferred_element_type=F32)
        bcs.append(part[:, :LANES] + part[:, LANES:])
    ri = lax.broadcasted_iota(I32, (C, 2 * C), 0)
    ci = lax.broadcasted_iota(I32, (C, 2 * C), 1) % C
    tri = (ci >= ri) if reverse else (ci <= ri)
    lane = lax.broadcasted_iota(I32, (1, LANES), 1)
    m0 = (lane < hd).astype(F32)
    m1 = (lane >= hd).astype(F32)
    sr = lax.broadcasted_iota(I32, (LANES, 2 * LANES), 0)
    sl = lax.broadcasted_iota(I32, (LANES, 2 * LANES), 1)
    smask = ((sr < hd) == (sl < LANES)).astype(F32)
    kt = k_ref[0].astype(F32)
    qt = q_ref[0].astype(F32)
    zero_v = jnp.zeros((C, LANES), BF16)
    sweep = list(range(nchunk - 1, -1, -1) if reverse else range(nchunk))
    for first in range(0, nchunk, GLA_BATCH):
        order = sweep[first:first + GLA_BATCH]
        intra, qds, kvs, decs, q2s, kss, kws, scs = {}, {}, {}, {}, {}, {}, {}, {}
        for ch in order:
            lo = ch * C
            b = bcs[lo // cb][lo % cb:lo % cb + C]
            last = b[0:1] if reverse else b[C - 1:C]
            mid = b[C // 2:C // 2 + 1] if reverse else b[C // 2 - 1:C // 2]
            kc = kt[lo:lo + C]
            qc = qt[lo:lo + C]
            q2s[ch] = (qc * jnp.exp(b - mid)).astype(BF16)
            ks = kc * jnp.exp(mid - b)
            kss[ch] = jnp.concatenate([ks * m0, ks * m1], axis=0).astype(BF16)
            qds[ch] = (qc * jnp.exp(b)).astype(BF16)
            kws[ch] = (kc * jnp.exp(last - b)).astype(BF16)
            decs[ch] = last
        yield
        for ch in order:
            scs[ch] = lax.dot_general(q2s[ch], kss[ch], _NT, preferred_element_type=F32)
        yield
        for ch in order:
            v2 = v_ref[0, ch * C:(ch + 1) * C, :]
            kvs[ch] = lax.dot_general(kws[ch], v2, _TN, preferred_element_type=F32) * smask
        yield
        for ch in order:
            sc2 = jnp.where(tri, scs[ch], 0.0).astype(BF16)
            v2 = v_ref[0, ch * C:(ch + 1) * C, :]
            vbd = jnp.concatenate([jnp.concatenate([v2[:, :LANES], zero_v], axis=1),
                                   jnp.concatenate([zero_v, v2[:, LANES:]], axis=1)], axis=0)
            intra[ch] = jnp.dot(sc2, vbd, preferred_element_type=F32)
        yield
        pad = jnp.zeros((LANES - len(order), LANES), F32)
        dec_cols = jnp.exp(jnp.concatenate([decs[ch] for ch in order] + [pad], axis=0).T)
        st = s_ref[...]
        starts = {}
        for i, ch in enumerate(order):
            starts[ch] = st.astype(BF16)
            st = st * dec_cols[:, i:i + 1] + kvs[ch]
        s_ref[...] = st
        yield
        for ch in order:
            inter = jnp.dot(qds[ch], starts[ch], preferred_element_type=F32)
            o_ref[0, ch * C:(ch + 1) * C, :] = inter + intra[ch]
        yield


def _gla_body(kf, qf, vf, laf, kb, qb, vb, lab, h0f, h0b, of, ob, hf_out, hb_out, sf, sb, *, nchunk):
    i = pl.program_id(2)

    @pl.when(i == 0)
    def _():
        sf[...] = h0f[0, 0]
        sb[...] = h0b[0, 0]

    sweeps = [_gla_direction(kf, qf, vf, laf, of, sf, False, nchunk),
              _gla_direction(kb, qb, vb, lab, ob, sb, True, nchunk)]
    for _ in range(GLA_STAGES * pl.cdiv(nchunk, GLA_BATCH)):
        for sweep in sweeps:
            next(sweep)

    @pl.when(i == pl.num_programs(2) - 1)
    def _():
        hf_out[0, 0] = sf[...]
        hb_out[0, 0] = sb[...]


def _gla(k, q, v, laf, lab, h0f, h0b, lt):
    bsz, l, _ = k.shape
    pairs = h0f.shape[1]
    nt = l // lt
    fwd = lambda b, hp, i: (b, i, hp)
    bwd = lambda b, hp, i: (b, nt - 1 - i, hp)
    st = lambda b, hp, i: (b, hp, 0, 0)
    kq = lambda m: pl.BlockSpec((1, lt, LANES), m)
    vv = lambda m: pl.BlockSpec((1, lt, 2 * LANES), m)
    sspec = pl.BlockSpec((1, 1, LANES, 2 * LANES), st)
    return pl.pallas_call(
        functools.partial(_gla_body, nchunk=lt // GLA_CHUNK),
        grid=(bsz, pairs, nt),
        in_specs=[kq(fwd), kq(fwd), vv(fwd), kq(fwd), kq(bwd), kq(bwd), vv(bwd), kq(bwd), sspec, sspec],
        out_specs=[vv(fwd), vv(bwd), sspec, sspec],
        out_shape=[jax.ShapeDtypeStruct(v.shape, F32), jax.ShapeDtypeStruct(v.shape, F32),
                   jax.ShapeDtypeStruct(h0f.shape, F32), jax.ShapeDtypeStruct(h0b.shape, F32)],
        scratch_shapes=[pltpu.VMEM((LANES, 2 * LANES), F32), pltpu.VMEM((LANES, 2 * LANES), F32)],
        compiler_params=_cparams(("parallel", "parallel", "arbitrary")),
        name="gla",
    )(k, q, v, laf, k, q, v, lab, h0f, h0b)


def _pool_body(p_ref, o_ref, s1_ref, *, half, rows):
    xb = p_ref[0]
    l, ch = xb.shape

    def inv_counts(pos, n):
        return 1.0 / (jnp.minimum(pos + half, n) - jnp.maximum(pos - half, 0)).astype(F32)

    blk = MXU_DIM
    r = lax.broadcasted_iota(I32, (blk, blk), 0)
    c = lax.broadcasted_iota(I32, (blk, blk), 1)
    band = ((r // GRID_W == c // GRID_W) & (c - r >= -half) & (c - r <= half - 1)).astype(BF16)
    for i in range(0, l // blk, 2):
        pair = jnp.concatenate([xb[i * blk:(i + 1) * blk], xb[(i + 1) * blk:(i + 2) * blk]], axis=1)
        sums = jnp.dot(band, pair, preferred_element_type=F32)
        s1_ref[i * blk:(i + 1) * blk, :] = sums[:, :ch]
        s1_ref[(i + 1) * blk:(i + 2) * blk, :] = sums[:, ch:]
    s1 = s1_ref[...].reshape(rows, GRID_W, ch)

    def shifted(a, s):
        z = jnp.zeros((abs(s),) + a.shape[1:], a.dtype)
        return jnp.concatenate([a[s:], z], axis=0) if s > 0 else jnp.concatenate([z, a[:s]], axis=0)

    fwd = s1
    bwd = shifted(s1, -1)
    s = 1
    while s < half:
        fwd = fwd + shifted(fwd, s)
        bwd = bwd + shifted(bwd, -s)
        s *= 2
    inv_r = inv_counts(lax.broadcasted_iota(I32, (rows, 1, ch), 0), rows)
    inv_c = inv_counts(lax.broadcasted_iota(I32, (1, GRID_W, ch), 1), GRID_W)
    pooled = (fwd + bwd) * inv_r * inv_c - xb.astype(F32).reshape(rows, GRID_W, ch)
    o_ref[0] = pooled.reshape(l, ch).astype(BF16)


def _pool_groups_body(p_ref, o_ref, s1_ref, *, rows):
    for gi, window in enumerate(POOL_WINDOWS):
        @pl.when(pl.program_id(1) == gi)
        def _(half=window // 2):
            _pool_body(p_ref, o_ref, s1_ref, half=half, rows=rows)


def _pool(pin, ch):
    bsz, l, pw = pin.shape
    return pl.pallas_call(
        functools.partial(_pool_groups_body, rows=l // GRID_W),
        grid=(bsz, pw // ch),
        in_specs=[pl.BlockSpec((1, l, ch), lambda b, g: (b, 0, g))],
        out_specs=pl.BlockSpec((1, l, ch), lambda b, g: (b, 0, g)),
        out_shape=jax.ShapeDtypeStruct((bsz, l, pw), BF16),
        scratch_shapes=[pltpu.VMEM((l, ch), F32)],
        compiler_params=_cparams(("parallel", "parallel")),
        name="pool",
    )(pin)


def _merge_body(x_ref, m1_ref, s1_ref, g1_ref, m2_ref, s2_ref,
                of_ref, ob_ref, sg_ref, mx_ref,
                wm_ref, wgla_ref, wpool_ref, wout_ref, gn_ref, wr_ref,
                x1_ref, h2_ref, aff_ref, *, heads, ne):
    d = x_ref.shape[1]

    def sub_tile(rows):
        x = x_ref[rows, :]
        h = _modulated_norm(x, m1_ref[0], s1_ref[0]).astype(BF16)
        o = of_ref[rows, :] + ob_ref[rows, :]
        sg = sg_ref[rows, :].astype(F32)
        og = []
        for j in range(heads):
            oj = o[:, j * LANES:(j + 1) * LANES]
            oj = oj * lax.rsqrt(jnp.mean(oj * oj, axis=-1, keepdims=True) + EPS) * gn_ref[...]
            og.append((oj * sg[:, j * LANES:(j + 1) * LANES]).astype(BF16))
        og = jnp.concatenate(og, axis=1)
        mixed = mx_ref[rows, :]
        yield
        gates = jnp.dot(h, wm_ref[...], preferred_element_type=F32)
        bg = jnp.dot(og, wgla_ref[...], preferred_element_type=F32)
        bp = jnp.dot(mixed, wpool_ref[...], preferred_element_type=F32)
        yield
        gates = jax.nn.sigmoid(gates)
        z = (gates[:, :d] * bg + gates[:, d:] * bp).astype(BF16)
        yield
        y = jnp.dot(z, wout_ref[...], preferred_element_type=F32)
        yield
        x1 = x + g1_ref[0] * y
        x1_ref[rows, :] = x1
        h2 = _modulated_norm(x1, m2_ref[0], s2_ref[0])
        hi = h2.astype(BF16)
        h2_ref[rows, :] = hi
        lo = (h2 - hi.astype(F32)).astype(BF16)
        yield
        lg = (jnp.dot(hi, wr_ref[...], preferred_element_type=F32)
              + jnp.dot(lo, wr_ref[...], preferred_element_type=F32))
        yield
        lgt = lg.T
        logit = lgt[0:ne] + lgt[ne:2 * ne]
        mx = jnp.max(logit, axis=0, keepdims=True)
        ex = jnp.exp(logit - mx)
        aff_ref[0, :, rows] = ex / jnp.sum(ex, axis=0, keepdims=True)
        yield

    _staggered([sub_tile(r) for r in _row_groups(x_ref.shape[0])], 7)


def _merge(x2, vecs, of, ob, sg, mixed, wm, wgla, wpool, wout, gn, wr, rows_per_sample, heads, ne, tm):
    n, d = x2.shape
    bsz = n // rows_per_sample
    tps = rows_per_sample // tm
    row = lambda i: (i, 0)
    vec = lambda i: (i // tps, 0, 0)
    full = lambda a: pl.BlockSpec(a.shape, lambda i: (0,) * a.ndim, pipeline_mode=pl.Buffered(1))
    gv = of.shape[-1]
    return pl.pallas_call(
        functools.partial(_merge_body, heads=heads, ne=ne),
        grid=(n // tm,),
        in_specs=[pl.BlockSpec((tm, d), row)] + [pl.BlockSpec((1, 1, d), vec)] * 5
                 + [pl.BlockSpec((tm, gv), row)] * 4
                 + [full(wm), full(wgla), full(wpool), full(wout), full(gn), full(wr)],
        out_specs=[pl.BlockSpec((tm, d), row),
                   pl.BlockSpec((tm, d), row),
                   pl.BlockSpec((1, ne, tm), lambda i: (i // tps, 0, i % tps))],
        out_shape=[jax.ShapeDtypeStruct((n, d), F32),
                   jax.ShapeDtypeStruct((n, d), BF16),
                   jax.ShapeDtypeStruct((bsz, ne, rows_per_sample), F32)],
        compiler_params=_cparams(("parallel",)),
        name="merge",
    )(x2, *vecs, of, ob, sg, mixed, wm, wgla, wpool, wout, gn, wr)


def _route_body(aff_ref, pos_ref, off_ref, cnt_ref, *, cap, ntb):
    a = aff_ref[0]
    ne, l = a.shape
    blk = ROUTE_BLK

    def bisect(i, v):
        cand = v | jnp.left_shift(jnp.int32(1), 30 - i)
        cnt = jnp.sum((a >= lax.bitcast_convert_type(cand, F32)).astype(F32), axis=1, keepdims=True)
        return jnp.where(cnt >= cap, cand, v)

    thr = lax.bitcast_convert_type(lax.fori_loop(0, 31, bisect, jnp.zeros((ne, 1), I32)), F32)
    gt = a > thr
    tie = a == thr
    need = cap - jnp.sum(gt.astype(F32), axis=1, keepdims=True)

    r = lax.broadcasted_iota(I32, (blk, blk), 0)
    c = lax.broadcasted_iota(I32, (blk, blk), 1)
    upper = (r <= c).astype(BF16)
    lane = lax.broadcasted_iota(I32, (1, LANES), 1)

    def prefix(mask_f):
        run = jnp.zeros((ne, 1), F32)
        offs = jnp.zeros((ne, LANES), F32)
        for tb in range(ntb):
            m = mask_f[:, tb * blk:(tb + 1) * blk].astype(BF16)
            loc = jnp.dot(m, upper, preferred_element_type=F32)
            cnt_ref[:, tb * blk:(tb + 1) * blk] = loc + run
            offs = jnp.where(lane == tb, run, offs)
            run = run + loc[:, blk - 1:blk]
        return jnp.where(lane >= ntb, run, offs)

    tie_f = tie.astype(F32)
    prefix(tie_f)
    tie_excl = cnt_ref[...] - tie_f
    sel = gt | (tie & (tie_excl < need))
    offs = prefix(sel.astype(F32))
    pos_ref[0] = jnp.where(sel, cnt_ref[...] - 1.0, -1.0).astype(I32)
    off_ref[0] = offs.astype(I32)


def _route(aff, cap):
    bsz, ne, l = aff.shape
    spec = lambda s: pl.BlockSpec((1,) + s, lambda b: (b, 0, 0))
    return pl.pallas_call(
        functools.partial(_route_body, cap=cap, ntb=l // ROUTE_BLK),
        grid=(bsz,),
        in_specs=[spec((ne, l))],
        out_specs=[spec((ne, l)), spec((ne, LANES))],
        out_shape=[jax.ShapeDtypeStruct((bsz, ne, l), I32),
                   jax.ShapeDtypeStruct((bsz, ne, LANES), I32)],
        scratch_shapes=[pltpu.VMEM((ne, l), F32)],
        compiler_params=_cparams(("parallel",)),
        name="route",
    )(aff)


def _window_plan(off_ref, bb, tt, experts, ne):
    lows = [off_ref[(bb * ne + e) * LANES + tt] & -BF16_ROWS for e in experts]
    ends = [off_ref[(bb * ne + e) * LANES + tt + 1] for e in experts]
    return lows, ends


def _window_rounds(lows, ends):
    rounds = jnp.int32(0)
    for lo, hi in zip(lows, ends):
        rounds = jnp.maximum(rounds, lax.div(hi - lo + (SLOT_WIN - 1), jnp.int32(SLOT_WIN)))
    return rounds


def _gatherx_body(off_ref, h_ref, pos_ref, xs_ref, *, ne, cap, eg):
    b = pl.program_id(0)
    g = pl.program_id(1)
    tb = pl.program_id(2)
    win = SLOT_WIN

    @pl.when(tb == 0)
    def _():
        xs_ref[...] = jnp.zeros(xs_ref.shape, BF16)

    j_col = lax.broadcasted_iota(I32, (win, 1), 0)
    t = ROUTE_BLK
    for sub in range(h_ref.shape[0] // t):
        lows, ends = _window_plan(off_ref, b, tb * (h_ref.shape[0] // t) + sub, [g * eg + k for k in range(eg)], ne)
        rounds = _window_rounds(lows, ends)

        def one_round(r, carry, lows=lows, sub=sub):
            starts = [pl.multiple_of(jnp.minimum(lows[k] + r * win, cap - win), BF16_ROWS) for k in range(eg)]
            pieces = []
            for k in range(eg):
                p = pos_ref[0, k:k + 1, sub * t:(sub + 1) * t]
                hit = (p - starts[k] == j_col) & (p >= lows[k] + r * win)
                pieces.append(jnp.where(hit, 1.0, 0.0).astype(BF16))
            sel = jnp.concatenate(pieces, axis=0)
            rows = jnp.dot(sel, h_ref[sub * t:(sub + 1) * t, :],
                           preferred_element_type=F32).astype(BF16)
            for k in range(eg):
                dst = (0, k, pl.ds(starts[k], win), slice(None))
                xs_ref[dst] = xs_ref[dst] + rows[k * win:(k + 1) * win]
            return carry

        lax.fori_loop(0, rounds, one_round, 0)


def _gatherx(off_flat, h2, pos, cap):
    n, d = h2.shape
    bsz, ne, l = pos.shape
    t = min(l, TOKEN_TILE)
    ntb = l // t
    eg = SUBLANES
    grid_spec = pltpu.PrefetchScalarGridSpec(
        num_scalar_prefetch=1,
        grid=(bsz, ne // eg, ntb),
        in_specs=[pl.BlockSpec((t, d), lambda b, g, i, off: (b * ntb + i, 0)),
                  pl.BlockSpec((1, eg, t), lambda b, g, i, off: (b, g, i))],
        out_specs=pl.BlockSpec((1, eg, cap, d), lambda b, g, i, off: (b, g, 0, 0)),
    )
    return pl.pallas_call(
        functools.partial(_gatherx_body, ne=ne, cap=cap, eg=eg),
        grid_spec=grid_spec,
        out_shape=jax.ShapeDtypeStruct((bsz, ne, cap, d), BF16),
        compiler_params=_cparams(("parallel", "parallel", "arbitrary")),
        name="gatherx",
    )(off_flat, h2, pos)


def _moe_body(xs_ref, wg_ref, wu_ref, wd_ref, y_ref, *, rc):
    wg = wg_ref[0].astype(BF16)
    wu = wu_ref[0].astype(BF16)
    wd = wd_ref[0].astype(BF16)
    cap = xs_ref.shape[2]
    acts = []
    for ch in range(cap // rc):
        xs = xs_ref[0, 0, ch * rc:(ch + 1) * rc, :]
        gate = jnp.dot(xs, wg, preferred_element_type=F32)
        up = jnp.dot(xs, wu, preferred_element_type=F32)
        acts.append((_silu(gate) * up).astype(BF16))
    for ch in range(cap // rc):
        y_ref[0, 0, ch * rc:(ch + 1) * rc, :] = jnp.dot(acts[ch], wd, preferred_element_type=F32).astype(BF16)


def _moe(xs, wg, wu, wd):
    bsz, ne, cap, d = xs.shape
    de = wg.shape[2]
    slot = pl.BlockSpec((1, 1, cap, d), lambda e, b: (b, e, 0, 0))
    return pl.pallas_call(
        functools.partial(_moe_body, rc=min(cap, 2 * MXU_DIM)),
        grid=(ne, bsz),
        in_specs=[slot,
                  pl.BlockSpec((1, d, de), lambda e, b: (e, 0, 0)),
                  pl.BlockSpec((1, d, de), lambda e, b: (e, 0, 0)),
                  pl.BlockSpec((1, de, d), lambda e, b: (e, 0, 0))],
        out_specs=slot,
        out_shape=jax.ShapeDtypeStruct((bsz, ne, cap, d), BF16),
        compiler_params=_cparams(("parallel", "arbitrary")),
        name="moe",
    )(xs, wg, wu, wd)


def _combine_body(off_ref, x1_ref, g2_ref, fg_ref, pos_ref, aff_ref, y_hbm, o_ref, ybuf, sem,
                  *, ne, cap, ntb):
    b = pl.program_id(0)
    tb = pl.program_id(1)
    step = b * ntb + tb
    nsteps = pl.num_programs(0) * ntb
    slot = step % 2
    spare = 2
    win = SLOT_WIN

    def block_lows(bb, tt):
        return _window_plan(off_ref, bb, tt, range(ne), ne)[0]

    def window_copy(bb, e, start, buf):
        return pltpu.make_async_copy(y_hbm.at[bb, e, pl.ds(start, win), :],
                                     ybuf.at[buf, pl.ds(e * win, win), :], sem.at[buf, e])

    def window_starts(lows, r):
        return [pl.multiple_of(jnp.minimum(lows[e] + r * win, cap - win), BF16_ROWS) for e in range(ne)]

    def start_first_round(bb, tt, buf):
        starts = window_starts(block_lows(bb, tt), 0)
        for e in range(ne):
            window_copy(bb, e, starts[e], buf).start()

    @pl.when(step == 0)
    def _():
        start_first_round(b, tb, 0)

    lows, ends = _window_plan(off_ref, b, tb, range(ne), ne)
    rounds = _window_rounds(lows, ends)
    j_col = lax.broadcasted_iota(I32, (win, 1), 0)

    def expand(r, starts, buf):
        pieces = []
        for e in range(ne):
            p = pos_ref[0, e:e + 1, :]
            valid = p >= lows[e] + r * win
            hit = (p - starts[e] == j_col) & valid
            pieces.append(jnp.where(hit, aff_ref[0, e:e + 1, :], 0.0).astype(BF16))
        pmat = jnp.concatenate(pieces, axis=0)
        return lax.dot_general(pmat, ybuf[buf], _TN, preferred_element_type=F32)

    nxt = jnp.minimum(step + 1, nsteps - 1)
    start_first_round(lax.div(nxt, jnp.int32(ntb)), lax.rem(nxt, jnp.int32(ntb)), 1 - slot)
    starts0 = window_starts(lows, 0)
    for e in range(ne):
        window_copy(b, e, starts0[e], slot).wait()
    moe = expand(0, starts0, slot)

    def extra_round(r, acc):
        starts = window_starts(lows, r)
        for e in range(ne):
            window_copy(b, e, starts[e], spare).start()
        for e in range(ne):
            window_copy(b, e, starts[e], spare).wait()
        return acc + expand(r, starts, spare)

    moe = lax.fori_loop(1, rounds, extra_round, moe)
    x2 = x1_ref[...] + g2_ref[0] * moe
    ms = jnp.mean(x2 * x2, axis=-1, keepdims=True)
    o_ref[...] = x2 * lax.rsqrt(ms + EPS) * fg_ref[...]

    @pl.when(step == nsteps - 1)
    def _():
        for e in range(ne):
            window_copy(b, e, starts0[e], 1 - slot).wait()


def _combine(off_flat, x1, g2, fg, pos, aff, y, rows_per_sample):
    n, d = x1.shape
    bsz, ne, cap, _ = y.shape
    t = ROUTE_BLK
    ntb = rows_per_sample // t
    grid_spec = pltpu.PrefetchScalarGridSpec(
        num_scalar_prefetch=1,
        grid=(bsz, ntb),
        in_specs=[pl.BlockSpec((t, d), lambda b, i, off: (b * ntb + i, 0)),
                  pl.BlockSpec((1, 1, d), lambda b, i, off: (b, 0, 0)),
                  pl.BlockSpec((1, d), lambda b, i, off: (0, 0)),
                  pl.BlockSpec((1, ne, t), lambda b, i, off: (b, 0, i)),
                  pl.BlockSpec((1, ne, t), lambda b, i, off: (b, 0, i)),
                  pl.BlockSpec(memory_space=pl.ANY)],
        out_specs=pl.BlockSpec((t, d), lambda b, i, off: (b * ntb + i, 0)),
        scratch_shapes=[pltpu.VMEM((3, ne * SLOT_WIN, d), BF16), pltpu.SemaphoreType.DMA((3, ne))],
    )
    return pl.pallas_call(
        functools.partial(_combine_body, ne=ne, cap=cap, ntb=ntb),
        grid_spec=grid_spec,
        out_shape=jax.ShapeDtypeStruct((n, d), F32),
        compiler_params=_cparams(("arbitrary", "arbitrary")),
        name="combine",
    )(off_flat, x1, g2, fg, pos, aff, y)


def kernel(x, c, ctx, c_ctx, ada_w, ada_b, norm1_g, norm2_g, w_in, w_decay_up, b_decay, gla_norm_g,
           w_gla_proj, pool_w, pool_scale, w_pool_proj, w_out, w_router, w_gate_e, w_up_e, w_down_e,
           final_norm_g):
    assert ada_w.shape[0] == 1, "single-layer block"
    bsz, l, d = x.shape
    lc = ctx.shape[1]
    rank, dk = w_decay_up.shape[2], w_decay_up.shape[3]
    dvh = gla_norm_g.shape[1]
    dv = w_gla_proj.shape[1]
    heads = dv // dvh
    groups, ch = pool_w.shape[1], pool_w.shape[2]
    pw = groups * ch
    ne = w_router.shape[2]
    cap = EC_CAPACITY * l // ne
    assert dk // heads == HEAD_DK and dvh == LANES and ch == LANES and heads % 2 == 0
    assert l % ROUTE_BLK == 0 and l // ROUTE_BLK < LANES and cap >= SLOT_WIN and cap % BF16_ROWS == 0
    assert GRID_W == GLA_CHUNK and MXU_DIM % GRID_W == 0 and (l // MXU_DIM) % 2 == 0

    cin = jnp.zeros((8, d), F32).at[:bsz].set(c).at[bsz].set(c_ctx)
    mods = _ada(cin, ada_w[0], ada_b[0][None, :])
    sh1, sc1, gt1, sh2, sc2, gt2 = [mods[:, i * d:(i + 1) * d] for i in range(6)]
    vec3 = lambda a: a[:, None, :]
    mult1 = norm1_g[0][None, :] * (1.0 + sc1)
    mult2 = norm2_g[0][None, :] * (1.0 + sc2)

    o_r = dk + dv
    o_q = o_r + 2 * rank
    o_g = o_q + dk
    o_p = o_g + dv
    o_m = o_p + pw
    up = jnp.zeros((2 * rank, 2 * dk), F32)
    up = up.at[:rank, :dk].set(w_decay_up[0, 0]).at[rank:, dk:].set(w_decay_up[0, 1])
    psc = pool_scale[0].reshape(groups, 1, ch)
    w1, wm, wpool = _prep(w_in[0].T, up, pool_w[0], psc, w_pool_proj[0], o_r, o_q, o_g, o_m,
                          float(dk // heads) ** -0.5)
    bz = b_decay[0]

    zero_state = jnp.zeros((bsz, heads // 2, LANES, 2 * LANES), F32)
    cm = jnp.broadcast_to(vec3(mult1[bsz:bsz + 1]), (bsz, 1, d))
    cs = jnp.broadcast_to(vec3(sh1[bsz:bsz + 1]), (bsz, 1, d))
    ck, cv, _, claf, clab, _, _ = _inproj(ctx.reshape(bsz * lc, d), cm, cs, w1, bz, lc, dk, dv, pw, min(lc, CTX_TILE))
    r3 = lambda a, n: a.reshape(bsz, n, a.shape[-1])
    _, _, h_f, h_b = _gla(r3(ck, lc), r3(ck, lc), r3(cv, lc), r3(claf, lc), r3(clab, lc),
                          zero_state, zero_state, min(lc, CTX_TILE))

    x2 = x.reshape(bsz * l, d)
    tm = min(l, TOKEN_TILE)
    k, v, q, laf, lab, sg, pin = _inproj(x2, vec3(mult1[:bsz]), vec3(sh1[:bsz]), w1, bz, l, dk, dv, pw, tm)
    of, ob, _, _ = _gla(r3(k, l), r3(q, l), r3(v, l), r3(laf, l), r3(lab, l), h_f, h_b, min(l, GLA_TILE))
    pooled = _pool(r3(pin, l), ch).reshape(bsz * l, pw)

    wr_hi = w_router[0].astype(BF16)
    wr_lo = (w_router[0] - wr_hi.astype(F32)).astype(BF16)
    wr = jnp.zeros((d, LANES), BF16).at[:, :ne].set(wr_hi).at[:, ne:2 * ne].set(wr_lo)
    vecs = [vec3(mult1[:bsz]), vec3(sh1[:bsz]), vec3(gt1[:bsz]), vec3(mult2[:bsz]), vec3(sh2[:bsz])]
    x1, h2, aff = _merge(x2, vecs, of.reshape(bsz * l, dv), ob.reshape(bsz * l, dv), sg, pooled,
                          wm, w_gla_proj[0].astype(BF16), wpool, w_out[0].astype(BF16),
                          gla_norm_g[0][None, :], wr, l, heads, ne, tm)

    pos, offs = _route(aff, cap)
    off_flat = offs.reshape(-1)
    xs = _gatherx(off_flat, h2, pos, cap)
    y = _moe(xs, w_gate_e[0], w_up_e[0], w_down_e[0])
    out = _combine(off_flat, x1, vec3(gt2[:bsz]), final_norm_g[None, :], pos, aff, y, l)
    return out.reshape(bsz, l, d)
```

```python
import functools

import jax
import jax.numpy as jnp
from jax import lax
from jax.experimental import pallas as pl
from jax.experimental.pallas import tpu as pltpu

F32 = jnp.float32
BF16 = jnp.bfloat16
I32 = jnp.int32
HIGHEST = lax.Precision.HIGHEST

EPS = 1e-6
GRID_W = 64
GLA_CHUNK = 64
GLA_STAGES = 6
GLA_BATCH = 8
GATE_NORMALIZER = 16.0
POOL_WINDOWS = (2, 4, 8, 16)
EC_CAPACITY = 2

LANES = 128
SUBLANES = 8
BF16_ROWS = 16
MXU_DIM = 256
VMEM_BYTES = 64 * 1024 * 1024
VMEM_LIMIT = VMEM_BYTES * 7 // 8

HEAD_DK = LANES // 2
ROUTE_BLK = MXU_DIM
SLOT_WIN = 64
SUB_ROWS = MXU_DIM
TOKEN_TILE = 1024
GLA_TILE = 2048
CTX_TILE = 256

_NT = (((1,), (1,)), ((), ()))
_TN = (((0,), (0,)), ((), ()))


def _cparams(sem):
    return pltpu.CompilerParams(dimension_semantics=sem, vmem_limit_bytes=VMEM_LIMIT)


def _silu(x):
    return x * jax.nn.sigmoid(x)


def _row_groups(n):
    step = min(n, SUB_ROWS)
    return [pl.ds(i, step) for i in range(0, n, step)]


def _staggered(gens, nstages):
    for t in range(nstages + len(gens) - 1):
        for g in reversed(range(len(gens))):
            if 0 <= t - g < nstages:
                next(gens[g])


def _ada_body(c_ref, w_ref, b_ref, o_ref):
    s = _silu(c_ref[...])
    rows = s.shape[0]
    s_hi = s.astype(BF16)
    s_lo = (s - s_hi.astype(F32)).astype(BF16)
    w = w_ref[...]
    w_hi = w.astype(BF16)
    w_lo = (w - w_hi.astype(F32)).astype(BF16)
    both = jnp.dot(jnp.concatenate([s_hi, s_lo], axis=0), w_hi, preferred_element_type=F32)
    o_ref[...] = (both[:rows] + both[rows:] + jnp.dot(s_hi, w_lo, preferred_element_type=F32)) + b_ref[...]


def _ada(cin, w, b):
    rows, d = cin.shape
    n = w.shape[1]
    nb = n // 4
    return pl.pallas_call(
        _ada_body,
        grid=(n // nb,),
        in_specs=[pl.BlockSpec((rows, d), lambda j: (0, 0)),
                  pl.BlockSpec((d, nb), lambda j: (0, j)),
                  pl.BlockSpec((1, nb), lambda j: (0, j))],
        out_specs=pl.BlockSpec((rows, nb), lambda j: (0, j)),
        out_shape=jax.ShapeDtypeStruct((rows, n), F32),
        compiler_params=_cparams(("arbitrary",)),
        name="ada",
    )(cin, w, b)


def _prep_body(w_ref, up_ref, pw_ref, ps_ref, wpp_ref, w1_ref, wm_ref, wpf_ref, *, o_r, o_q, o_g, o_m, qscale):
    dk = o_g - o_q
    wz = lax.dot_general(w_ref[o_r:o_q, :], up_ref[...], _TN, precision=HIGHEST, preferred_element_type=F32)
    nz = wz.shape[1]
    w1_ref[:, :o_r] = w_ref[:o_r, :].T.astype(BF16)
    w1_ref[:, o_r:o_r + dk] = (w_ref[o_q:o_g, :].T * qscale).astype(BF16)
    w1_ref[:, o_r + dk:o_r + dk + nz] = wz.astype(BF16)
    w1_ref[:, o_r + dk + nz:] = w_ref[o_g:o_m, :].T.astype(BF16)
    wm_ref[...] = w_ref[o_m:, :].T.astype(BF16)
    wpf_ref[...] = jnp.dot(pw_ref[0] * ps_ref[0], wpp_ref[...], precision=HIGHEST,
                           preferred_element_type=F32).astype(BF16)


def _prep(w_in_t, up, pool_w, pool_scale3, w_pool_proj, o_r, o_q, o_g, o_m, qscale):
    n, d = w_in_t.shape
    n1 = o_r + (o_g - o_q) + up.shape[1] + (o_m - o_g)
    groups, ch, _ = pool_w.shape
    tr = d // groups
    dm = w_pool_proj.shape[1]
    return pl.pallas_call(
        functools.partial(_prep_body, o_r=o_r, o_q=o_q, o_g=o_g, o_m=o_m, qscale=qscale),
        grid=(d // tr,),
        in_specs=[pl.BlockSpec((n, tr), lambda i: (0, i)),
                  pl.BlockSpec(up.shape, lambda i: (0, 0)),
                  pl.BlockSpec((1, ch, ch), lambda i: (i, 0, 0)),
                  pl.BlockSpec((1, 1, ch), lambda i: (i, 0, 0)),
                  pl.BlockSpec((ch, dm), lambda i: (i, 0))],
        out_specs=[pl.BlockSpec((tr, n1), lambda i: (i, 0)),
                   pl.BlockSpec((tr, n - o_m), lambda i: (i, 0)),
                   pl.BlockSpec((ch, dm), lambda i: (i, 0))],
        out_shape=[jax.ShapeDtypeStruct((d, n1), BF16), jax.ShapeDtypeStruct((d, n - o_m), BF16),
                   jax.ShapeDtypeStruct((groups * ch, dm), BF16)],
        compiler_params=_cparams(("parallel",)),
        name="prep",
    )(w_in_t, up, pool_w, pool_scale3, w_pool_proj)


def _modulated_norm(x, mult, shift):
    ms = jnp.mean(x * x, axis=-1, keepdims=True)
    return (x * lax.rsqrt(ms + EPS)) * mult + shift


def _log_sigmoid(z):
    return jnp.minimum(z, 0.0) - jnp.log1p(jnp.exp(-jnp.abs(z)))


def _inproj_body(x_ref, mult_ref, shift_ref, w_ref, bz_ref,
                 k_ref, v_ref, q_ref, laf_ref, lab_ref, sg_ref, p_ref, *, dk, dv, pw):
    def sub_tile(rows):
        h = _modulated_norm(x_ref[rows, :], mult_ref[0], shift_ref[0]).astype(BF16)
        yield
        u = jnp.dot(h, w_ref[...], preferred_element_type=F32)
        yield
        o = 0
        k_ref[rows, :] = u[:, o:o + dk].astype(BF16); o += dk
        v_ref[rows, :] = u[:, o:o + dv].astype(BF16); o += dv
        q_ref[rows, :] = u[:, o:o + dk].astype(BF16); o += dk
        zf = u[:, o:o + dk] + bz_ref[0:1, :]; o += dk
        zb = u[:, o:o + dk] + bz_ref[1:2, :]; o += dk
        laf_ref[rows, :] = _log_sigmoid(zf) * (1.0 / GATE_NORMALIZER)
        lab_ref[rows, :] = _log_sigmoid(zb) * (1.0 / GATE_NORMALIZER)
        sg_ref[rows, :] = _silu(u[:, o:o + dv]).astype(BF16); o += dv
        p_ref[rows, :] = u[:, o:o + pw].astype(BF16)
        yield

    _staggered([sub_tile(r) for r in _row_groups(x_ref.shape[0])], 3)


def _inproj(x2, mult, shift, w, bz, rows_per_sample, dk, dv, pw, tm):
    n, d = x2.shape
    tps = rows_per_sample // tm
    row = lambda i: (i, 0)
    vec = lambda i: (i // tps, 0, 0)
    outs = [(dk, BF16), (dv, BF16), (dk, BF16), (dk, F32), (dk, F32), (dv, BF16), (pw, BF16)]
    return pl.pallas_call(
        functools.partial(_inproj_body, dk=dk, dv=dv, pw=pw),
        grid=(n // tm,),
        in_specs=[pl.BlockSpec((tm, d), row),
                  pl.BlockSpec((1, 1, d), vec),
                  pl.BlockSpec((1, 1, d), vec),
                  pl.BlockSpec(w.shape, lambda i: (0, 0), pipeline_mode=pl.Buffered(1)),
                  pl.BlockSpec(bz.shape, lambda i: (0, 0))],
        out_specs=[pl.BlockSpec((tm, c), row) for c, _ in outs],
        out_shape=[jax.ShapeDtypeStruct((n, c), t) for c, t in outs],
        compiler_params=_cparams(("parallel",)),
        name="inproj",
    )(x2, mult, shift, w, bz)


def _gla_direction(k_ref, q_ref, v_ref, la_ref, o_ref, s_ref, reverse, nchunk):
    C = GLA_CHUNK
    hd = HEAD_DK
    cb = MXU_DIM
    lt = nchunk * C
    la = la_ref[0]
    r = lax.broadcasted_iota(I32, (cb, cb), 0)
    c = lax.broadcasted_iota(I32, (cb, cb), 1)
    same = (r // C) == (c // C)
    cum = jnp.where(same & ((c >= r) if reverse else (c <= r)), 1.0, 0.0).astype(BF16)
    la_hi = la.astype(BF16)
    la_lo = (la - la_hi.astype(F32)).astype(BF16)
    la2 = jnp.concatenate([la_hi, la_lo], axis=1)
    bcs = []
    for blk in range(lt // cb):
        part = jnp.dot(cum, la2[blk * cb:(blk + 1) * cb], preferred_element_type=F32)
        bcs.append(part[:, :LANES] + part[:, LANES:])
    ri = lax.broadcasted_iota(I32, (C, 2 * C), 0)
    ci = lax.broadcasted_iota(I32, (C, 2 * C), 1) % C
    tri = (ci >= ri) if reverse else (ci <= ri)
    lane = lax.broadcasted_iota(I32, (1, LANES), 1)
    m0 = (lane < hd).astype(F32)
    m1 = (lane >= hd).astype(F32)
    sr = lax.broadcasted_iota(I32, (LANES, 2 * LANES), 0)
    sl = lax.broadcasted_iota(I32, (LANES, 2 * LANES), 1)
    smask = ((sr < hd) == (sl < LANES)).astype(F32)
    kt = k_ref[0].astype(F32)
    qt = q_ref[0].astype(F32)
    zero_v = jnp.zeros((C, LANES), BF16)
    sweep = list(range(nchunk - 1, -1, -1) if reverse else range(nchunk))
    for first in range(0, nchunk, GLA_BATCH):
        order = sweep[first:first + GLA_BATCH]
        intra, qds, kvs, decs, q2s, kss, kws, scs = {}, {}, {}, {}, {}, {}, {}, {}
        for ch in order:
            lo = ch * C
            b = bcs[lo // cb][lo % cb:lo % cb + C]
            last = b[0:1] if reverse else b[C - 1:C]
            mid = b[C // 2:C // 2 + 1] if reverse else b[C // 2 - 1:C // 2]
            kc = kt[lo:lo + C]
            qc = qt[lo:lo + C]
            q2s[ch] = (qc * jnp.exp(b - mid)).astype(BF16)
            ks = kc * jnp.exp(mid - b)
            kss[ch] = jnp.concatenate([ks * m0, ks * m1], axis=0).astype(BF16)
            qds[ch] = (qc * jnp.exp(b)).astype(BF16)
            kws[ch] = (kc * jnp.exp(last - b)).astype(BF16)
            decs[ch] = last
        yield
        for ch in order:
            scs[ch] = lax.dot_general(q2s[ch], kss[ch], _NT, preferred_element_type=F32)
        yield
        for ch in order:
            v2 = v_ref[0, ch * C:(ch + 1) * C, :]
            kvs[ch] = lax.dot_general(kws[ch], v2, _TN, preferred_element_type=F32) * smask
        yield
        for ch in order:
            sc2 = jnp.where(tri, scs[ch], 0.0).astype(BF16)
            v2 = v_ref[0, ch * C:(ch + 1) * C, :]
            vbd = jnp.concatenate([jnp.concatenate([v2[:, :LANES], zero_v], axis=1),
                                   jnp.concatenate([zero_v, v2[:, LANES:]], axis=1)], axis=0)
            intra[ch] = jnp.dot(sc2, vbd, preferred_element_type=F32)
        yield
        pad = jnp.zeros((LANES - len(order), LANES), F32)
        dec_cols = jnp.exp(jnp.concatenate([decs[ch] for ch in order] + [pad], axis=0).T)
        st = s_ref[...]
        starts = {}
        for i, ch in enumerate(order):
            starts[ch] = st.astype(BF16)
            st = st * dec_cols[:, i:i + 1] + kvs[ch]
        s_ref[...] = st
        yield
        for ch in order:
            inter = jnp.dot(qds[ch], starts[ch], preferred_element_type=F32)
            o_ref[0, ch * C:(ch + 1) * C, :] = inter + intra[ch]
        yield


def _gla_body(kf, qf, vf, laf, kb, qb, vb, lab, h0f, h0b, of, ob, hf_out, hb_out, sf, sb, *, nchunk):
    i = pl.program_id(2)

    @pl.when(i == 0)
    def _():
        sf[...] = h0f[0, 0]
        sb[...] = h0b[0, 0]

    sweeps = [_gla_direction(kf, qf, vf, laf, of, sf, False, nchunk),
              _gla_direction(kb, qb, vb, lab, ob, sb, True, nchunk)]
    for _ in range(GLA_STAGES * pl.cdiv(nchunk, GLA_BATCH)):
        for sweep in sweeps:
            next(sweep)

    @pl.when(i == pl.num_programs(2) - 1)
    def _():
        hf_out[0, 0] = sf[...]
        hb_out[0, 0] = sb[...]


def _gla(k, q, v, laf, lab, h0f, h0b, lt):
    bsz, l, _ = k.shape
    pairs = h0f.shape[1]
    nt = l // lt
    fwd = lambda b, hp, i: (b, i, hp)
    bwd = lambda b, hp, i: (b, nt - 1 - i, hp)
    st = lambda b, hp, i: (b, hp, 0, 0)
    kq = lambda m: pl.BlockSpec((1, lt, LANES), m)
    vv = lambda m: pl.BlockSpec((1, lt, 2 * LANES), m)
    sspec = pl.BlockSpec((1, 1, LANES, 2 * LANES), st)
    return pl.pallas_call(
        functools.partial(_gla_body, nchunk=lt // GLA_CHUNK),
        grid=(bsz, pairs, nt),
        in_specs=[kq(fwd), kq(fwd), vv(fwd), kq(fwd), kq(bwd), kq(bwd), vv(bwd), kq(bwd), sspec, sspec],
        out_specs=[vv(fwd), vv(bwd), sspec, sspec],
        out_shape=[jax.ShapeDtypeStruct(v.shape, F32), jax.ShapeDtypeStruct(v.shape, F32),
                   jax.ShapeDtypeStruct(h0f.shape, F32), jax.ShapeDtypeStruct(h0b.shape, F32)],
        scratch_shapes=[pltpu.VMEM((LANES, 2 * LANES), F32), pltpu.VMEM((LANES, 2 * LANES), F32)],
        compiler_params=_cparams(("parallel", "parallel", "arbitrary")),
        name="gla",
    )(k, q, v, laf, k, q, v, lab, h0f, h0b)


def _pool_body(p_ref, o_ref, s1_ref, *, half, rows):
    xb = p_ref[0]
    l, ch = xb.shape

    def inv_counts(pos, n):
        return 1.0 / (jnp.minimum(pos + half, n) - jnp.maximum(pos - half, 0)).astype(F32)

    blk = MXU_DIM
    r = lax.broadcasted_iota(I32, (blk, blk), 0)
    c = lax.broadcasted_iota(I32, (blk, blk), 1)
    band = ((r // GRID_W == c // GRID_W) & (c - r >= -half) & (c - r <= half - 1)).astype(BF16)
    for i in range(0, l // blk, 2):
        pair = jnp.concatenate([xb[i * blk:(i + 1) * blk], xb[(i + 1) * blk:(i + 2) * blk]], axis=1)
        sums = jnp.dot(band, pair, preferred_element_type=F32)
        s1_ref[i * blk:(i + 1) * blk, :] = sums[:, :ch]
        s1_ref[(i + 1) * blk:(i + 2) * blk, :] = sums[:, ch:]
    s1 = s1_ref[...].reshape(rows, GRID_W, ch)

    def shifted(a, s):
        z = jnp.zeros((abs(s),) + a.shape[1:], a.dtype)
        return jnp.concatenate([a[s:], z], axis=0) if s > 0 else jnp.concatenate([z, a[:s]], axis=0)

    fwd = s1
    bwd = shifted(s1, -1)
    s = 1
    while s < half:
        fwd = fwd + shifted(fwd, s)
        bwd = bwd + shifted(bwd, -s)
        s *= 2
    inv_r = inv_counts(lax.broadcasted_iota(I32, (rows, 1, ch), 0), rows)
    inv_c = inv_counts(lax.broadcasted_iota(I32, (1, GRID_W, ch), 1), GRID_W)
    pooled = (fwd + bwd) * inv_r * inv_c - xb.astype(F32).reshape(rows, GRID_W, ch)
    o_ref[0] = pooled.reshape(l, ch).astype(BF16)


def _pool_groups_body(p_ref, o_ref, s1_ref, *, rows):
    for gi, window in enumerate(POOL_WINDOWS):
        @pl.when(pl.program_id(1) == gi)
        def _(half=window // 2):
            _pool_body(p_ref, o_ref, s1_ref, half=half, rows=rows)


def _pool(pin, ch):
    bsz, l, pw = pin.shape
    return pl.pallas_call(
        functools.partial(_pool_groups_body, rows=l // GRID_W),
        grid=(bsz, pw // ch),
        in_specs=[pl.BlockSpec((1, l, ch), lambda b, g: (b, 0, g))],
        out_specs=pl.BlockSpec((1, l, ch), lambda b, g: (b, 0, g)),
        out_shape=jax.ShapeDtypeStruct((bsz, l, pw), BF16),
        scratch_shapes=[pltpu.VMEM((l, ch), F32)],
        compiler_params=_cparams(("parallel", "parallel")),
        name="pool",
    )(pin)


def _merge_body(x_ref, m1_ref, s1_ref, g1_ref, m2_ref, s2_ref,
                of_ref, ob_ref, sg_ref, mx_ref,
                wm_ref, wgla_ref, wpool_ref, wout_ref, gn_ref, wr_ref,
                x1_ref, h2_ref, aff_ref, *, heads, ne):
    d = x_ref.shape[1]

    def sub_tile(rows):
        x = x_ref[rows, :]
        h = _modulated_norm(x, m1_ref[0], s1_ref[0]).astype(BF16)
        o = of_ref[rows, :] + ob_ref[rows, :]
        sg = sg_ref[rows, :].astype(F32)
        og = []
        for j in range(heads):
            oj = o[:, j * LANES:(j + 1) * LANES]
            oj = oj * lax.rsqrt(jnp.mean(oj * oj, axis=-1, keepdims=True) + EPS) * gn_ref[...]
            og.append((oj * sg[:, j * LANES:(j + 1) * LANES]).astype(BF16))
        og = jnp.concatenate(og, axis=1)
        mixed = mx_ref[rows, :]
        yield
        gates = jnp.dot(h, wm_ref[...], preferred_element_type=F32)
        bg = jnp.dot(og, wgla_ref[...], preferred_element_type=F32)
        bp = jnp.dot(mixed, wpool_ref[...], preferred_element_type=F32)
        yield
        gates = jax.nn.sigmoid(gates)
        z = (gates[:, :d] * bg + gates[:, d:] * bp).astype(BF16)
        yield
        y = jnp.dot(z, wout_ref[...], preferred_element_type=F32)
        yield
        x1 = x + g1_ref[0] * y
        x1_ref[rows, :] = x1
        h2 = _modulated_norm(x1, m2_ref[0], s2_ref[0])
        hi = h2.astype(BF16)
        h2_ref[rows, :] = hi
        lo = (h2 - hi.astype(F32)).astype(BF16)
        yield
        lg = (jnp.dot(hi, wr_ref[...], preferred_element_type=F32)
              + jnp.dot(lo, wr_ref[...], preferred_element_type=F32))
        yield
        lgt = lg.T
        logit = lgt[0:ne] + lgt[ne:2 * ne]
        mx = jnp.max(logit, axis=0, keepdims=True)
        ex = jnp.exp(logit - mx)
        aff_ref[0, :, rows] = ex / jnp.sum(ex, axis=0, keepdims=True)
        yield

    _staggered([sub_tile(r) for r in _row_groups(x_ref.shape[0])], 7)


def _merge(x2, vecs, of, ob, sg, mixed, wm, wgla, wpool, wout, gn, wr, rows_per_sample, heads, ne, tm):
    n, d = x2.shape
    bsz = n // rows_per_sample
    tps = rows_per_sample // tm
    row = lambda i: (i, 0)
    vec = lambda i: (i // tps, 0, 0)
    full = lambda a: pl.BlockSpec(a.shape, lambda i: (0,) * a.ndim, pipeline_mode=pl.Buffered(1))
    gv = of.shape[-1]
    return pl.pallas_call(
        functools.partial(_merge_body, heads=heads, ne=ne),
        grid=(n // tm,),
        in_specs=[pl.BlockSpec((tm, d), row)] + [pl.BlockSpec((1, 1, d), vec)] * 5
                 + [pl.BlockSpec((tm, gv), row)] * 4
                 + [full(wm), full(wgla), full(wpool), full(wout), full(gn), full(wr)],
        out_specs=[pl.BlockSpec((tm, d), row),
                   pl.BlockSpec((tm, d), row),
                   pl.BlockSpec((1, ne, tm), lambda i: (i // tps, 0, i % tps))],
        out_shape=[jax.ShapeDtypeStruct((n, d), F32),
                   jax.ShapeDtypeStruct((n, d), BF16),
                   jax.ShapeDtypeStruct((bsz, ne, rows_per_sample), F32)],
        compiler_params=_cparams(("parallel",)),
        name="merge",
    )(x2, *vecs, of, ob, sg, mixed, wm, wgla, wpool, wout, gn, wr)


def _route_body(aff_ref, pos_ref, off_ref, cnt_ref, *, cap, ntb):
    a = aff_ref[0]
    ne, l = a.shape
    blk = ROUTE_BLK

    def bisect(i, v):
        cand = v | jnp.left_shift(jnp.int32(1), 30 - i)
        cnt = jnp.sum((a >= lax.bitcast_convert_type(cand, F32)).astype(F32), axis=1, keepdims=True)
        return jnp.where(cnt >= cap, cand, v)

    thr = lax.bitcast_convert_type(lax.fori_loop(0, 31, bisect, jnp.zeros((ne, 1), I32)), F32)
    gt = a > thr
    tie = a == thr
    need = cap - jnp.sum(gt.astype(F32), axis=1, keepdims=True)

    r = lax.broadcasted_iota(I32, (blk, blk), 0)
    c = lax.broadcasted_iota(I32, (blk, blk), 1)
    upper = (r <= c).astype(BF16)
    lane = lax.broadcasted_iota(I32, (1, LANES), 1)

    def prefix(mask_f):
        run = jnp.zeros((ne, 1), F32)
        offs = jnp.zeros((ne, LANES), F32)
        for tb in range(ntb):
            m = mask_f[:, tb * blk:(tb + 1) * blk].astype(BF16)
            loc = jnp.dot(m, upper, preferred_element_type=F32)
            cnt_ref[:, tb * blk:(tb + 1) * blk] = loc + run
            offs = jnp.where(lane == tb, run, offs)
            run = run + loc[:, blk - 1:blk]
        return jnp.where(lane >= ntb, run, offs)

    tie_f = tie.astype(F32)
    prefix(tie_f)
    tie_excl = cnt_ref[...] - tie_f
    sel = gt | (tie & (tie_excl < need))
    offs = prefix(sel.astype(F32))
    pos_ref[0] = jnp.where(sel, cnt_ref[...] - 1.0, -1.0).astype(I32)
    off_ref[0] = offs.astype(I32)


def _route(aff, cap):
    bsz, ne, l = aff.shape
    spec = lambda s: pl.BlockSpec((1,) + s, lambda b: (b, 0, 0))
    return pl.pallas_call(
        functools.partial(_route_body, cap=cap, ntb=l // ROUTE_BLK),
        grid=(bsz,),
        in_specs=[spec((ne, l))],
        out_specs=[spec((ne, l)), spec((ne, LANES))],
        out_shape=[jax.ShapeDtypeStruct((bsz, ne, l), I32),
                   jax.ShapeDtypeStruct((bsz, ne, LANES), I32)],
        scratch_shapes=[pltpu.VMEM((ne, l), F32)],
        compiler_params=_cparams(("parallel",)),
        name="route",
    )(aff)


def _window_plan(off_ref, bb, tt, experts, ne):
    lows = [off_ref[(bb * ne + e) * LANES + tt] & -BF16_ROWS for e in experts]
    ends = [off_ref[(bb * ne + e) * LANES + tt + 1] for e in experts]
    return lows, ends


def _window_rounds(lows, ends):
    rounds = jnp.int32(0)
    for lo, hi in zip(lows, ends):
        rounds = jnp.maximum(rounds, lax.div(hi - lo + (SLOT_WIN - 1), jnp.int32(SLOT_WIN)))
    return rounds


def _gatherx_body(off_ref, h_ref, pos_ref, xs_ref, *, ne, cap, eg):
    b = pl.program_id(0)
    g = pl.program_id(1)
    tb = pl.program_id(2)
    win = SLOT_WIN

    @pl.when(tb == 0)
    def _():
        xs_ref[...] = jnp.zeros(xs_ref.shape, BF16)

    j_col = lax.broadcasted_iota(I32, (win, 1), 0)
    t = ROUTE_BLK
    nsub = h_ref.shape[0] // t
    experts = [g * eg + k for k in range(eg)]

    def select(sub, lows, r):
        starts = [pl.multiple_of(jnp.minimum(lows[k] + r * win, cap - win), BF16_ROWS) for k in range(eg)]
        pieces = []
        for k in range(eg):
            p = pos_ref[0, k:k + 1, sub * t:(sub + 1) * t]
            hit = (p - starts[k] == j_col) & (p >= lows[k] + r * win)
            pieces.append(jnp.where(hit, 1.0, 0.0).astype(BF16))
        sel = jnp.concatenate(pieces, axis=0)
        rows = jnp.dot(sel, h_ref[sub * t:(sub + 1) * t, :], preferred_element_type=F32).astype(BF16)
        return starts, rows

    def deposit(starts, rows):
        for k in range(eg):
            dst = (0, k, pl.ds(starts[k], win), slice(None))
            xs_ref[dst] = xs_ref[dst] + rows[k * win:(k + 1) * win]

    plans = [_window_plan(off_ref, b, tb * nsub + sub, experts, ne) for sub in range(nsub)]
    firsts = [select(sub, plans[sub][0], 0) for sub in range(nsub)]
    for starts, rows in firsts:
        deposit(starts, rows)
    for sub in range(nsub):
        lows, ends = plans[sub]

        def extra_round(r, carry, lows=lows, sub=sub):
            deposit(*select(sub, lows, r))
            return carry

        lax.fori_loop(1, _window_rounds(lows, ends), extra_round, 0)


def _gatherx(off_flat, h2, pos, cap):
    n, d = h2.shape
    bsz, ne, l = pos.shape
    t = min(l, TOKEN_TILE)
    ntb = l // t
    eg = SUBLANES
    grid_spec = pltpu.PrefetchScalarGridSpec(
        num_scalar_prefetch=1,
        grid=(bsz, ne // eg, ntb),
        in_specs=[pl.BlockSpec((t, d), lambda b, g, i, off: (b * ntb + i, 0)),
                  pl.BlockSpec((1, eg, t), lambda b, g, i, off: (b, g, i))],
        out_specs=pl.BlockSpec((1, eg, cap, d), lambda b, g, i, off: (b, g, 0, 0)),
    )
    return pl.pallas_call(
        functools.partial(_gatherx_body, ne=ne, cap=cap, eg=eg),
        grid_spec=grid_spec,
        out_shape=jax.ShapeDtypeStruct((bsz, ne, cap, d), BF16),
        compiler_params=_cparams(("parallel", "parallel", "arbitrary")),
        name="gatherx",
    )(off_flat, h2, pos)


def _moe_body(xs_ref, wg_ref, wu_ref, wd_ref, y_ref, *, rc):
    wg = wg_ref[0].astype(BF16)
    wu = wu_ref[0].astype(BF16)
    wd = wd_ref[0].astype(BF16)
    cap = xs_ref.shape[2]
    acts = []
    for ch in range(cap // rc):
        xs = xs_ref[0, 0, ch * rc:(ch + 1) * rc, :]
        gate = jnp.dot(xs, wg, preferred_element_type=F32)
        up = jnp.dot(xs, wu, preferred_element_type=F32)
        acts.append((_silu(gate) * up).astype(BF16))
    for ch in range(cap // rc):
        y_ref[0, 0, ch * rc:(ch + 1) * rc, :] = jnp.dot(acts[ch], wd, preferred_element_type=F32).astype(BF16)


def _moe(xs, wg, wu, wd):
    bsz, ne, cap, d = xs.shape
    de = wg.shape[2]
    slot = pl.BlockSpec((1, 1, cap, d), lambda e, b: (b, e, 0, 0))
    return pl.pallas_call(
        functools.partial(_moe_body, rc=min(cap, 2 * MXU_DIM)),
        grid=(ne, bsz),
        in_specs=[slot,
                  pl.BlockSpec((1, d, de), lambda e, b: (e, 0, 0)),
                  pl.BlockSpec((1, d, de), lambda e, b: (e, 0, 0)),
                  pl.BlockSpec((1, de, d), lambda e, b: (e, 0, 0))],
        out_specs=slot,
        out_shape=jax.ShapeDtypeStruct((bsz, ne, cap, d), BF16),
        compiler_params=_cparams(("parallel", "arbitrary")),
        name="moe",
    )(xs, wg, wu, wd)


def _combine_body(off_ref, x1_ref, g2_ref, fg_ref, pos_ref, aff_ref, y_hbm, o_ref, ybuf, spare, acc_ref, sem,
                  spare_sem, *, ne, cap, nsp, nsub):
    b = pl.program_id(0)
    i = pl.program_id(1)
    step = b * nsp + i
    nsteps = pl.num_programs(0) * nsp
    slot = step % 2
    win = SLOT_WIN
    t = ROUTE_BLK
    experts = range(ne)

    def window_starts(lows, r):
        return [pl.multiple_of(jnp.minimum(lows[e] + r * win, cap - win), BF16_ROWS) for e in experts]

    def first_copy(bb, sub, e, start, buf):
        return pltpu.make_async_copy(y_hbm.at[bb, e, pl.ds(start, win), :],
                                     ybuf.at[buf, sub, pl.ds(e * win, win), :], sem.at[buf, sub, e])

    def spare_copy(e, start):
        return pltpu.make_async_copy(y_hbm.at[b, e, pl.ds(start, win), :],
                                     spare.at[pl.ds(e * win, win), :], spare_sem.at[e])

    def start_step(bb, ii, buf):
        for sub in range(nsub):
            starts = window_starts(_window_plan(off_ref, bb, ii * nsub + sub, experts, ne)[0], 0)
            for e in experts:
                first_copy(bb, sub, e, starts[e], buf).start()

    @pl.when(step == 0)
    def _():
        start_step(b, i, 0)

    nxt = jnp.minimum(step + 1, nsteps - 1)
    start_step(lax.div(nxt, jnp.int32(nsp)), lax.rem(nxt, jnp.int32(nsp)), 1 - slot)

    plans = [_window_plan(off_ref, b, i * nsub + sub, experts, ne) for sub in range(nsub)]
    j_col = lax.broadcasted_iota(I32, (win, 1), 0)

    def expand(sub, r, starts, rows_ref):
        lows = plans[sub][0]
        pieces = []
        for e in experts:
            p = pos_ref[0, e:e + 1, sub * t:(sub + 1) * t]
            valid = p >= lows[e] + r * win
            hit = (p - starts[e] == j_col) & valid
            pieces.append(jnp.where(hit, aff_ref[0, e:e + 1, sub * t:(sub + 1) * t], 0.0).astype(BF16))
        pmat = jnp.concatenate(pieces, axis=0)
        return lax.dot_general(pmat, rows_ref[...], _TN, preferred_element_type=F32)

    firsts = [window_starts(plans[sub][0], 0) for sub in range(nsub)]
    for sub in range(nsub):
        for e in experts:
            first_copy(b, sub, e, firsts[sub][e], slot).wait()
    for sub in range(nsub):
        acc_ref[sub] = expand(sub, 0, firsts[sub], ybuf.at[slot, sub])

    for sub in range(nsub):
        lows, ends = plans[sub]

        def extra_round(r, carry, lows=lows, sub=sub):
            starts = window_starts(lows, r)
            for e in experts:
                spare_copy(e, starts[e]).start()
            for e in experts:
                spare_copy(e, starts[e]).wait()
            acc_ref[sub] += expand(sub, r, starts, spare)
            return carry

        lax.fori_loop(1, _window_rounds(lows, ends), extra_round, 0)

    for sub in range(nsub):
        rows = pl.ds(sub * t, t)
        x2 = x1_ref[rows, :] + g2_ref[0] * acc_ref[sub]
        ms = jnp.mean(x2 * x2, axis=-1, keepdims=True)
        o_ref[rows, :] = x2 * lax.rsqrt(ms + EPS) * fg_ref[...]

    @pl.when(step == nsteps - 1)
    def _():
        for sub in range(nsub):
            for e in experts:
                first_copy(b, sub, e, firsts[sub][e], 1 - slot).wait()


def _combine(off_flat, x1, g2, fg, pos, aff, y, rows_per_sample):
    n, d = x1.shape
    bsz, ne, cap, _ = y.shape
    t = min(rows_per_sample, TOKEN_TILE)
    nsub = t // ROUTE_BLK
    nsp = rows_per_sample // t
    rows = ne * SLOT_WIN
    grid_spec = pltpu.PrefetchScalarGridSpec(
        num_scalar_prefetch=1,
        grid=(bsz, nsp),
        in_specs=[pl.BlockSpec((t, d), lambda b, i, off: (b * nsp + i, 0)),
                  pl.BlockSpec((1, 1, d), lambda b, i, off: (b, 0, 0)),
                  pl.BlockSpec((1, d), lambda b, i, off: (0, 0)),
                  pl.BlockSpec((1, ne, t), lambda b, i, off: (b, 0, i)),
                  pl.BlockSpec((1, ne, t), lambda b, i, off: (b, 0, i)),
                  pl.BlockSpec(memory_space=pl.ANY)],
        out_specs=pl.BlockSpec((t, d), lambda b, i, off: (b * nsp + i, 0)),
        scratch_shapes=[pltpu.VMEM((2, nsub, rows, d), BF16),
                        pltpu.VMEM((rows, d), BF16),
                        pltpu.VMEM((nsub, ROUTE_BLK, d), F32),
                        pltpu.SemaphoreType.DMA((2, nsub, ne)),
                        pltpu.SemaphoreType.DMA((ne,))],
    )
    return pl.pallas_call(
        functools.partial(_combine_body, ne=ne, cap=cap, nsp=nsp, nsub=nsub),
        grid_spec=grid_spec,
        out_shape=jax.ShapeDtypeStruct((n, d), F32),
        compiler_params=_cparams(("arbitrary", "arbitrary")),
        name="combine",
    )(off_flat, x1, g2, fg, pos, aff, y)


def kernel(x, c, ctx, c_ctx, ada_w, ada_b, norm1_g, norm2_g, w_in, w_decay_up, b_decay, gla_norm_g,
           w_gla_proj, pool_w, pool_scale, w_pool_proj, w_out, w_router, w_gate_e, w_up_e, w_down_e,
           final_norm_g):
    assert ada_w.shape[0] == 1, "single-layer block"
    bsz, l, d = x.shape
    lc = ctx.shape[1]
    rank, dk = w_decay_up.shape[2], w_decay_up.shape[3]
    dvh = gla_norm_g.shape[1]
    dv = w_gla_proj.shape[1]
    heads = dv // dvh
    groups, ch = pool_w.shape[1], pool_w.shape[2]
    pw = groups * ch
    ne = w_router.shape[2]
    cap = EC_CAPACITY * l // ne
    assert dk // heads == HEAD_DK and dvh == LANES and ch == LANES and heads % 2 == 0
    assert l % ROUTE_BLK == 0 and l // ROUTE_BLK < LANES and cap >= SLOT_WIN and cap % BF16_ROWS == 0
    assert GRID_W == GLA_CHUNK and MXU_DIM % GRID_W == 0 and (l // MXU_DIM) % 2 == 0

    cin = jnp.zeros((8, d), F32).at[:bsz].set(c).at[bsz].set(c_ctx)
    mods = _ada(cin, ada_w[0], ada_b[0][None, :])
    sh1, sc1, gt1, sh2, sc2, gt2 = [mods[:, i * d:(i + 1) * d] for i in range(6)]
    vec3 = lambda a: a[:, None, :]
    mult1 = norm1_g[0][None, :] * (1.0 + sc1)
    mult2 = norm2_g[0][None, :] * (1.0 + sc2)

    o_r = dk + dv
    o_q = o_r + 2 * rank
    o_g = o_q + dk
    o_p = o_g + dv
    o_m = o_p + pw
    up = jnp.zeros((2 * rank, 2 * dk), F32)
    up = up.at[:rank, :dk].set(w_decay_up[0, 0]).at[rank:, dk:].set(w_decay_up[0, 1])
    psc = pool_scale[0].reshape(groups, 1, ch)
    w1, wm, wpool = _prep(w_in[0].T, up, pool_w[0], psc, w_pool_proj[0], o_r, o_q, o_g, o_m,
                          float(dk // heads) ** -0.5)
    bz = b_decay[0]

    zero_state = jnp.zeros((bsz, heads // 2, LANES, 2 * LANES), F32)
    cm = jnp.broadcast_to(vec3(mult1[bsz:bsz + 1]), (bsz, 1, d))
    cs = jnp.broadcast_to(vec3(sh1[bsz:bsz + 1]), (bsz, 1, d))
    ck, cv, _, claf, clab, _, _ = _inproj(ctx.reshape(bsz * lc, d), cm, cs, w1, bz, lc, dk, dv, pw, min(lc, CTX_TILE))
    r3 = lambda a, n: a.reshape(bsz, n, a.shape[-1])
    _, _, h_f, h_b = _gla(r3(ck, lc), r3(ck, lc), r3(cv, lc), r3(claf, lc), r3(clab, lc),
                          zero_state, zero_state, min(lc, CTX_TILE))

    x2 = x.reshape(bsz * l, d)
    tm = min(l, TOKEN_TILE)
    k, v, q, laf, lab, sg, pin = _inproj(x2, vec3(mult1[:bsz]), vec3(sh1[:bsz]), w1, bz, l, dk, dv, pw, tm)
    of, ob, _, _ = _gla(r3(k, l), r3(q, l), r3(v, l), r3(laf, l), r3(lab, l), h_f, h_b, min(l, GLA_TILE))
    pooled = _pool(r3(pin, l), ch).reshape(bsz * l, pw)

    wr_hi = w_router[0].astype(BF16)
    wr_lo = (w_router[0] - wr_hi.astype(F32)).astype(BF16)
    wr = jnp.zeros((d, LANES), BF16).at[:, :ne].set(wr_hi).at[:, ne:2 * ne].set(wr_lo)
    vecs = [vec3(mult1[:bsz]), vec3(sh1[:bsz]), vec3(gt1[:bsz]), vec3(mult2[:bsz]), vec3(sh2[:bsz])]
    x1, h2, aff = _merge(x2, vecs, of.reshape(bsz * l, dv), ob.reshape(bsz * l, dv), sg, pooled,
                          wm, w_gla_proj[0].astype(BF16), wpool, w_out[0].astype(BF16),
                          gla_norm_g[0][None, :], wr, l, heads, ne, tm)

    pos, offs = _route(aff.reshape(1, bsz * ne, l), cap)
    pos = pos.reshape(bsz, ne, l)
    off_flat = offs.reshape(-1)
    xs = _gatherx(off_flat, h2, pos, cap)
    y = _moe(xs, w_gate_e[0], w_up_e[0], w_down_e[0])
    out = _combine(off_flat, x1, vec3(gt2[:bsz]), final_norm_g[None, :], pos, aff, y, l)
    return out.reshape(bsz, l, d)
```

```python
import functools

import jax
import jax.numpy as jnp
from jax import lax
from jax.experimental import pallas as pl
from jax.experimental.pallas import tpu as pltpu

F32 = jnp.float32
BF16 = jnp.bfloat16
I32 = jnp.int32
HIGHEST = lax.Precision.HIGHEST

EPS = 1e-6
GRID_W = 64
GLA_CHUNK = 64
GLA_STAGES = 6
GLA_BATCH = 8
GATE_NORMALIZER = 16.0
POOL_WINDOWS = (2, 4, 8, 16)
EC_CAPACITY = 2

LANES = 128
SUBLANES = 8
BF16_ROWS = 16
MXU_DIM = 256
VMEM_BYTES = 64 * 1024 * 1024
VMEM_LIMIT = VMEM_BYTES * 7 // 8

HEAD_DK = LANES // 2
ROUTE_BLK = MXU_DIM
SLOT_WIN = 64
SUB_ROWS = MXU_DIM
TOKEN_TILE = 1024
GLA_TILE = 2048
CTX_TILE = 256

_NT = (((1,), (1,)), ((), ()))
_TN = (((0,), (0,)), ((), ()))


def _cparams(sem):
    return pltpu.CompilerParams(dimension_semantics=sem, vmem_limit_bytes=VMEM_LIMIT)


def _silu(x):
    return x * jax.nn.sigmoid(x)


def _row_groups(n):
    step = min(n, SUB_ROWS)
    return [pl.ds(i, step) for i in range(0, n, step)]


def _staggered(gens, nstages):
    for t in range(nstages + len(gens) - 1):
        for g in reversed(range(len(gens))):
            if 0 <= t - g < nstages:
                next(gens[g])


def _ada_body(c_ref, w_ref, b_ref, o_ref):
    s = _silu(c_ref[...])
    rows = s.shape[0]
    s_hi = s.astype(BF16)
    s_lo = (s - s_hi.astype(F32)).astype(BF16)
    w = w_ref[...]
    w_hi = w.astype(BF16)
    w_lo = (w - w_hi.astype(F32)).astype(BF16)
    both = jnp.dot(jnp.concatenate([s_hi, s_lo], axis=0), w_hi, preferred_element_type=F32)
    o_ref[...] = (both[:rows] + both[rows:] + jnp.dot(s_hi, w_lo, preferred_element_type=F32)) + b_ref[...]


def _ada(cin, w, b):
    rows, d = cin.shape
    n = w.shape[1]
    nb = n // 4
    return pl.pallas_call(
        _ada_body,
        grid=(n // nb,),
        in_specs=[pl.BlockSpec((rows, d), lambda j: (0, 0)),
                  pl.BlockSpec((d, nb), lambda j: (0, j)),
                  pl.BlockSpec((1, nb), lambda j: (0, j))],
        out_specs=pl.BlockSpec((rows, nb), lambda j: (0, j)),
        out_shape=jax.ShapeDtypeStruct((rows, n), F32),
        compiler_params=_cparams(("arbitrary",)),
        name="ada",
    )(cin, w, b)


def _prep_body(w_ref, up_ref, pw_ref, ps_ref, wpp_ref, w1_ref, wm_ref, wpf_ref, *, o_r, o_q, o_g, o_m, qscale):
    dk = o_g - o_q
    wz = lax.dot_general(w_ref[o_r:o_q, :], up_ref[...], _TN, precision=HIGHEST, preferred_element_type=F32)
    nz = wz.shape[1]
    w1_ref[:, :o_r] = w_ref[:o_r, :].T.astype(BF16)
    w1_ref[:, o_r:o_r + dk] = (w_ref[o_q:o_g, :].T * qscale).astype(BF16)
    w1_ref[:, o_r + dk:o_r + dk + nz] = wz.astype(BF16)
    w1_ref[:, o_r + dk + nz:] = w_ref[o_g:o_m, :].T.astype(BF16)
    wm_ref[...] = w_ref[o_m:, :].T.astype(BF16)
    wpf_ref[...] = jnp.dot(pw_ref[0] * ps_ref[0], wpp_ref[...], precision=HIGHEST,
                           preferred_element_type=F32).astype(BF16)


def _prep(w_in_t, up, pool_w, pool_scale3, w_pool_proj, o_r, o_q, o_g, o_m, qscale):
    n, d = w_in_t.shape
    n1 = o_r + (o_g - o_q) + up.shape[1] + (o_m - o_g)
    groups, ch, _ = pool_w.shape
    tr = d // groups
    dm = w_pool_proj.shape[1]
    return pl.pallas_call(
        functools.partial(_prep_body, o_r=o_r, o_q=o_q, o_g=o_g, o_m=o_m, qscale=qscale),
        grid=(d // tr,),
        in_specs=[pl.BlockSpec((n, tr), lambda i: (0, i)),
                  pl.BlockSpec(up.shape, lambda i: (0, 0)),
                  pl.BlockSpec((1, ch, ch), lambda i: (i, 0, 0)),
                  pl.BlockSpec((1, 1, ch), lambda i: (i, 0, 0)),
                  pl.BlockSpec((ch, dm), lambda i: (i, 0))],
        out_specs=[pl.BlockSpec((tr, n1), lambda i: (i, 0)),
                   pl.BlockSpec((tr, n - o_m), lambda i: (i, 0)),
                   pl.BlockSpec((ch, dm), lambda i: (i, 0))],
        out_shape=[jax.ShapeDtypeStruct((d, n1), BF16), jax.ShapeDtypeStruct((d, n - o_m), BF16),
                   jax.ShapeDtypeStruct((groups * ch, dm), BF16)],
        compiler_params=_cparams(("parallel",)),
        name="prep",
    )(w_in_t, up, pool_w, pool_scale3, w_pool_proj)


def _modulated_norm(x, mult, shift):
    ms = jnp.mean(x * x, axis=-1, keepdims=True)
    return (x * lax.rsqrt(ms + EPS)) * mult + shift


def _log_sigmoid(z):
    return jnp.minimum(z, 0.0) - jnp.log1p(jnp.exp(-jnp.abs(z)))


def _inproj_body(x_ref, mult_ref, shift_ref, w_ref, bz_ref,
                 k_ref, v_ref, q_ref, laf_ref, lab_ref, sg_ref, p_ref, *, dk, dv, pw):
    def sub_tile(rows):
        h = _modulated_norm(x_ref[rows, :], mult_ref[0], shift_ref[0]).astype(BF16)
        yield
        u = jnp.dot(h, w_ref[...], preferred_element_type=F32)
        yield
        o = 0
        k_ref[rows, :] = u[:, o:o + dk].astype(BF16); o += dk
        v_ref[rows, :] = u[:, o:o + dv].astype(BF16); o += dv
        q_ref[rows, :] = u[:, o:o + dk].astype(BF16); o += dk
        zf = u[:, o:o + dk] + bz_ref[0:1, :]; o += dk
        zb = u[:, o:o + dk] + bz_ref[1:2, :]; o += dk
        laf_ref[rows, :] = _log_sigmoid(zf) * (1.0 / GATE_NORMALIZER)
        lab_ref[rows, :] = _log_sigmoid(zb) * (1.0 / GATE_NORMALIZER)
        sg_ref[rows, :] = _silu(u[:, o:o + dv]).astype(BF16); o += dv
        p_ref[rows, :] = u[:, o:o + pw].astype(BF16)
        yield

    _staggered([sub_tile(r) for r in _row_groups(x_ref.shape[0])], 3)


def _inproj(x2, mult, shift, w, bz, rows_per_sample, dk, dv, pw, tm):
    n, d = x2.shape
    tps = rows_per_sample // tm
    row = lambda i: (i, 0)
    vec = lambda i: (i // tps, 0, 0)
    outs = [(dk, BF16), (dv, BF16), (dk, BF16), (dk, F32), (dk, F32), (dv, BF16), (pw, BF16)]
    return pl.pallas_call(
        functools.partial(_inproj_body, dk=dk, dv=dv, pw=pw),
        grid=(n // tm,),
        in_specs=[pl.BlockSpec((tm, d), row),
                  pl.BlockSpec((1, 1, d), vec),
                  pl.BlockSpec((1, 1, d), vec),
                  pl.BlockSpec(w.shape, lambda i: (0, 0), pipeline_mode=pl.Buffered(1)),
                  pl.BlockSpec(bz.shape, lambda i: (0, 0))],
        out_specs=[pl.BlockSpec((tm, c), row) for c, _ in outs],
        out_shape=[jax.ShapeDtypeStruct((n, c), t) for c, t in outs],
        compiler_params=_cparams(("parallel",)),
        name="inproj",
    )(x2, mult, shift, w, bz)


def _gla_direction(k_ref, q_ref, v_ref, la_ref, o_ref, s_ref, reverse, nchunk):
    C = GLA_CHUNK
    hd = HEAD_DK
    cb = MXU_DIM
    lt = nchunk * C
    la = la_ref[0]
    r = lax.broadcasted_iota(I32, (cb, cb), 0)
    c = lax.broadcasted_iota(I32, (cb, cb), 1)
    same = (r // C) == (c // C)
    cum = jnp.where(same & ((c >= r) if reverse else (c <= r)), 1.0, 0.0).astype(BF16)
    la_hi = la.astype(BF16)
    la_lo = (la - la_hi.astype(F32)).astype(BF16)
    la2 = jnp.concatenate([la_hi, la_lo], axis=1)
    bcs = []
    for blk in range(lt // cb):
        part = jnp.dot(cum, la2[blk * cb:(blk + 1) * cb], preferred_element_type=F32)
        bcs.append(part[:, :LANES] + part[:, LANES:])
    ri = lax.broadcasted_iota(I32, (C, 2 * C), 0)
    ci = lax.broadcasted_iota(I32, (C, 2 * C), 1) % C
    tri = (ci >= ri) if reverse else (ci <= ri)
    lane = lax.broadcasted_iota(I32, (1, LANES), 1)
    m0 = (lane < hd).astype(F32)
    m1 = (lane >= hd).astype(F32)
    sr = lax.broadcasted_iota(I32, (LANES, 2 * LANES), 0)
    sl = lax.broadcasted_iota(I32, (LANES, 2 * LANES), 1)
    smask = ((sr < hd) == (sl < LANES)).astype(F32)
    kt = k_ref[0].astype(F32)
    qt = q_ref[0].astype(F32)
    zero_v = jnp.zeros((C, LANES), BF16)
    sweep = list(range(nchunk - 1, -1, -1) if reverse else range(nchunk))
    for first in range(0, nchunk, GLA_BATCH):
        order = sweep[first:first + GLA_BATCH]
        intra, qds, kvs, decs, q2s, kss, kws, scs = {}, {}, {}, {}, {}, {}, {}, {}
        for ch in order:
            lo = ch * C
            b = bcs[lo // cb][lo % cb:lo % cb + C]
            last = b[0:1] if reverse else b[C - 1:C]
            mid = b[C // 2:C // 2 + 1] if reverse else b[C // 2 - 1:C // 2]
            kc = kt[lo:lo + C]
            qc = qt[lo:lo + C]
            q2s[ch] = (qc * jnp.exp(b - mid)).astype(BF16)
            ks = kc * jnp.exp(mid - b)
            kss[ch] = jnp.concatenate([ks * m0, ks * m1], axis=0).astype(BF16)
            qds[ch] = (qc * jnp.exp(b)).astype(BF16)
            kws[ch] = (kc * jnp.exp(last - b)).astype(BF16)
            decs[ch] = last
        yield
        for ch in order:
            scs[ch] = lax.dot_general(q2s[ch], kss[ch], _NT, preferred_element_type=F32)
        yield
        for ch in order:
            v2 = v_ref[0, ch * C:(ch + 1) * C, :]
            kvs[ch] = lax.dot_general(kws[ch], v2, _TN, preferred_element_type=F32) * smask
        yield
        for ch in order:
            sc2 = jnp.where(tri, scs[ch], 0.0).astype(BF16)
            v2 = v_ref[0, ch * C:(ch + 1) * C, :]
            vbd = jnp.concatenate([jnp.concatenate([v2[:, :LANES], zero_v], axis=1),
                                   jnp.concatenate([zero_v, v2[:, LANES:]], axis=1)], axis=0)
            intra[ch] = jnp.dot(sc2, vbd, preferred_element_type=F32)
        yield
        pad = jnp.zeros((LANES - len(order), LANES), F32)
        dec_cols = jnp.exp(jnp.concatenate([decs[ch] for ch in order] + [pad], axis=0).T)
        st = s_ref[...]
        starts = {}
        for i, ch in enumerate(order):
            starts[ch] = st.astype(BF16)
            st = st * dec_cols[:, i:i + 1] + kvs[ch]
        s_ref[...] = st
        yield
        for ch in order:
            inter = jnp.dot(qds[ch], starts[ch], preferred_element_type=F32)
            o_ref[0, ch * C:(ch + 1) * C, :] = inter + intra[ch]
        yield


def _gla_body(kf, qf, vf, laf, kb, qb, vb, lab, h0f, h0b, of, ob, hf_out, hb_out, sf, sb, *, nchunk):
    i = pl.program_id(2)

    @pl.when(i == 0)
    def _():
        sf[...] = h0f[0, 0]
        sb[...] = h0b[0, 0]

    sweeps = [_gla_direction(kf, qf, vf, laf, of, sf, False, nchunk),
              _gla_direction(kb, qb, vb, lab, ob, sb, True, nchunk)]
    for _ in range(GLA_STAGES * pl.cdiv(nchunk, GLA_BATCH)):
        for sweep in sweeps:
            next(sweep)

    @pl.when(i == pl.num_programs(2) - 1)
    def _():
        hf_out[0, 0] = sf[...]
        hb_out[0, 0] = sb[...]


def _gla(k, q, v, laf, lab, h0f, h0b, lt):
    bsz, l, _ = k.shape
    pairs = h0f.shape[1]
    nt = l // lt
    fwd = lambda b, hp, i: (b, i, hp)
    bwd = lambda b, hp, i: (b, nt - 1 - i, hp)
    st = lambda b, hp, i: (b, hp, 0, 0)
    kq = lambda m: pl.BlockSpec((1, lt, LANES), m)
    vv = lambda m: pl.BlockSpec((1, lt, 2 * LANES), m)
    sspec = pl.BlockSpec((1, 1, LANES, 2 * LANES), st)
    return pl.pallas_call(
        functools.partial(_gla_body, nchunk=lt // GLA_CHUNK),
        grid=(bsz, pairs, nt),
        in_specs=[kq(fwd), kq(fwd), vv(fwd), kq(fwd), kq(bwd), kq(bwd), vv(bwd), kq(bwd), sspec, sspec],
        out_specs=[vv(fwd), vv(bwd), sspec, sspec],
        out_shape=[jax.ShapeDtypeStruct(v.shape, F32), jax.ShapeDtypeStruct(v.shape, F32),
                   jax.ShapeDtypeStruct(h0f.shape, F32), jax.ShapeDtypeStruct(h0b.shape, F32)],
        scratch_shapes=[pltpu.VMEM((LANES, 2 * LANES), F32), pltpu.VMEM((LANES, 2 * LANES), F32)],
        compiler_params=_cparams(("parallel", "parallel", "arbitrary")),
        name="gla",
    )(k, q, v, laf, k, q, v, lab, h0f, h0b)


def _pool_body(p_ref, o_ref, s1_ref, *, half, rows):
    xb = p_ref[0]
    l, ch = xb.shape

    def inv_counts(pos, n):
        return 1.0 / (jnp.minimum(pos + half, n) - jnp.maximum(pos - half, 0)).astype(F32)

    blk = MXU_DIM
    r = lax.broadcasted_iota(I32, (blk, blk), 0)
    c = lax.broadcasted_iota(I32, (blk, blk), 1)
    band = ((r // GRID_W == c // GRID_W) & (c - r >= -half) & (c - r <= half - 1)).astype(BF16)
    for i in range(0, l // blk, 2):
        pair = jnp.concatenate([xb[i * blk:(i + 1) * blk], xb[(i + 1) * blk:(i + 2) * blk]], axis=1)
        sums = jnp.dot(band, pair, preferred_element_type=F32)
        s1_ref[i * blk:(i + 1) * blk, :] = sums[:, :ch]
        s1_ref[(i + 1) * blk:(i + 2) * blk, :] = sums[:, ch:]
    s1 = s1_ref[...].reshape(rows, GRID_W, ch)

    def shifted(a, s):
        z = jnp.zeros((abs(s),) + a.shape[1:], a.dtype)
        return jnp.concatenate([a[s:], z], axis=0) if s > 0 else jnp.concatenate([z, a[:s]], axis=0)

    fwd = s1
    bwd = shifted(s1, -1)
    s = 1
    while s < half:
        fwd = fwd + shifted(fwd, s)
        bwd = bwd + shifted(bwd, -s)
        s *= 2
    inv_r = inv_counts(lax.broadcasted_iota(I32, (rows, 1, ch), 0), rows)
    inv_c = inv_counts(lax.broadcasted_iota(I32, (1, GRID_W, ch), 1), GRID_W)
    pooled = (fwd + bwd) * inv_r * inv_c - xb.astype(F32).reshape(rows, GRID_W, ch)
    o_ref[0] = pooled.reshape(l, ch).astype(BF16)


def _pool_groups_body(p_ref, o_ref, s1_ref, *, rows):
    for gi, window in enumerate(POOL_WINDOWS):
        @pl.when(pl.program_id(1) == gi)
        def _(half=window // 2):
            _pool_body(p_ref, o_ref, s1_ref, half=half, rows=rows)


def _pool(pin, ch):
    bsz, l, pw = pin.shape
    return pl.pallas_call(
        functools.partial(_pool_groups_body, rows=l // GRID_W),
        grid=(bsz, pw // ch),
        in_specs=[pl.BlockSpec((1, l, ch), lambda b, g: (b, 0, g))],
        out_specs=pl.BlockSpec((1, l, ch), lambda b, g: (b, 0, g)),
        out_shape=jax.ShapeDtypeStruct((bsz, l, pw), BF16),
        scratch_shapes=[pltpu.VMEM((l, ch), F32)],
        compiler_params=_cparams(("parallel", "parallel")),
        name="pool",
    )(pin)


def _merge_body(x_ref, m1_ref, s1_ref, g1_ref, m2_ref, s2_ref,
                of_ref, ob_ref, sg_ref, mx_ref,
                wm_ref, wgla_ref, wpool_ref, wout_ref, gn_ref, wr_ref,
                x1_ref, h2_ref, aff_ref, *, heads, ne):
    d = x_ref.shape[1]

    def sub_tile(rows):
        bp = jnp.dot(mx_ref[rows, :], wpool_ref[...], preferred_element_type=F32)
        x = x_ref[rows, :]
        h = _modulated_norm(x, m1_ref[0], s1_ref[0]).astype(BF16)
        o = of_ref[rows, :] + ob_ref[rows, :]
        sg = sg_ref[rows, :].astype(F32)
        og = []
        for j in range(heads):
            oj = o[:, j * LANES:(j + 1) * LANES]
            oj = oj * lax.rsqrt(jnp.mean(oj * oj, axis=-1, keepdims=True) + EPS) * gn_ref[...]
            og.append((oj * sg[:, j * LANES:(j + 1) * LANES]).astype(BF16))
        og = jnp.concatenate(og, axis=1)
        yield
        gates = jnp.dot(h, wm_ref[...], preferred_element_type=F32)
        bg = jnp.dot(og, wgla_ref[...], preferred_element_type=F32)
        yield
        gates = jax.nn.sigmoid(gates)
        z = (gates[:, :d] * bg + gates[:, d:] * bp).astype(BF16)
        yield
        y = jnp.dot(z, wout_ref[...], preferred_element_type=F32)
        yield
        x1 = x + g1_ref[0] * y
        x1_ref[rows, :] = x1
        h2 = _modulated_norm(x1, m2_ref[0], s2_ref[0])
        hi = h2.astype(BF16)
        h2_ref[rows, :] = hi
        lo = (h2 - hi.astype(F32)).astype(BF16)
        yield
        lg = (jnp.dot(hi, wr_ref[...], preferred_element_type=F32)
              + jnp.dot(lo, wr_ref[...], preferred_element_type=F32))
        yield
        lgt = lg.T
        logit = lgt[0:ne] + lgt[ne:2 * ne]
        mx = jnp.max(logit, axis=0, keepdims=True)
        ex = jnp.exp(logit - mx)
        aff_ref[0, :, rows] = ex / jnp.sum(ex, axis=0, keepdims=True)
        yield

    _staggered([sub_tile(r) for r in _row_groups(x_ref.shape[0])], 7)


def _merge(x2, vecs, of, ob, sg, mixed, wm, wgla, wpool, wout, gn, wr, rows_per_sample, heads, ne, tm):
    n, d = x2.shape
    bsz = n // rows_per_sample
    tps = rows_per_sample // tm
    row = lambda i: (i, 0)
    vec = lambda i: (i // tps, 0, 0)
    full = lambda a: pl.BlockSpec(a.shape, lambda i: (0,) * a.ndim, pipeline_mode=pl.Buffered(1))
    gv = of.shape[-1]
    return pl.pallas_call(
        functools.partial(_merge_body, heads=heads, ne=ne),
        grid=(n // tm,),
        in_specs=[pl.BlockSpec((tm, d), row)] + [pl.BlockSpec((1, 1, d), vec)] * 5
                 + [pl.BlockSpec((tm, gv), row)] * 4
                 + [full(wm), full(wgla), full(wpool), full(wout), full(gn), full(wr)],
        out_specs=[pl.BlockSpec((tm, d), row),
                   pl.BlockSpec((tm, d), row),
                   pl.BlockSpec((1, ne, tm), lambda i: (i // tps, 0, i % tps))],
        out_shape=[jax.ShapeDtypeStruct((n, d), F32),
                   jax.ShapeDtypeStruct((n, d), BF16),
                   jax.ShapeDtypeStruct((bsz, ne, rows_per_sample), F32)],
        compiler_params=_cparams(("parallel",)),
        name="merge",
    )(x2, *vecs, of, ob, sg, mixed, wm, wgla, wpool, wout, gn, wr)


def _route_body(aff_ref, pos_ref, off_ref, cnt_ref, *, cap, ntb):
    a = aff_ref[0]
    ne, l = a.shape
    blk = ROUTE_BLK

    def bisect(i, v):
        cand = v | jnp.left_shift(jnp.int32(1), 30 - i)
        cnt = jnp.sum((a >= lax.bitcast_convert_type(cand, F32)).astype(F32), axis=1, keepdims=True)
        return jnp.where(cnt >= cap, cand, v)

    thr = lax.bitcast_convert_type(lax.fori_loop(0, 31, bisect, jnp.zeros((ne, 1), I32)), F32)
    gt = a > thr
    tie = a == thr
    need = cap - jnp.sum(gt.astype(F32), axis=1, keepdims=True)

    r = lax.broadcasted_iota(I32, (blk, blk), 0)
    c = lax.broadcasted_iota(I32, (blk, blk), 1)
    upper = (r <= c).astype(BF16)
    lane = lax.broadcasted_iota(I32, (1, LANES), 1)

    def prefix(mask_f):
        run = jnp.zeros((ne, 1), F32)
        offs = jnp.zeros((ne, LANES), F32)
        for tb in range(ntb):
            m = mask_f[:, tb * blk:(tb + 1) * blk].astype(BF16)
            loc = jnp.dot(m, upper, preferred_element_type=F32)
            cnt_ref[:, tb * blk:(tb + 1) * blk] = loc + run
            offs = jnp.where(lane == tb, run, offs)
            run = run + loc[:, blk - 1:blk]
        return jnp.where(lane >= ntb, run, offs)

    tie_f = tie.astype(F32)
    prefix(tie_f)
    tie_excl = cnt_ref[...] - tie_f
    sel = gt | (tie & (tie_excl < need))
    offs = prefix(sel.astype(F32))
    pos_ref[0] = jnp.where(sel, cnt_ref[...] - 1.0, -1.0).astype(I32)
    off_ref[0] = offs.astype(I32)


def _route(aff, cap):
    bsz, ne, l = aff.shape
    spec = lambda s: pl.BlockSpec((1,) + s, lambda b: (b, 0, 0))
    return pl.pallas_call(
        functools.partial(_route_body, cap=cap, ntb=l // ROUTE_BLK),
        grid=(bsz,),
        in_specs=[spec((ne, l))],
        out_specs=[spec((ne, l)), spec((ne, LANES))],
        out_shape=[jax.ShapeDtypeStruct((bsz, ne, l), I32),
                   jax.ShapeDtypeStruct((bsz, ne, LANES), I32)],
        scratch_shapes=[pltpu.VMEM((ne, l), F32)],
        compiler_params=_cparams(("parallel",)),
        name="route",
    )(aff)


def _window_plan(off_ref, bb, tt, experts, ne):
    lows = [off_ref[(bb * ne + e) * LANES + tt] & -BF16_ROWS for e in experts]
    ends = [off_ref[(bb * ne + e) * LANES + tt + 1] for e in experts]
    return lows, ends


def _window_rounds(lows, ends):
    rounds = jnp.int32(0)
    for lo, hi in zip(lows, ends):
        rounds = jnp.maximum(rounds, lax.div(hi - lo + (SLOT_WIN - 1), jnp.int32(SLOT_WIN)))
    return rounds


def _gatherx_body(off_ref, h_ref, pos_ref, xs_ref, *, ne, cap, eg):
    b = pl.program_id(0)
    g = pl.program_id(1)
    tb = pl.program_id(2)
    win = SLOT_WIN

    @pl.when(tb == 0)
    def _():
        xs_ref[...] = jnp.zeros(xs_ref.shape, BF16)

    j_col = lax.broadcasted_iota(I32, (win, 1), 0)
    t = ROUTE_BLK
    nsub = h_ref.shape[0] // t
    experts = [g * eg + k for k in range(eg)]

    def select(sub, lows, r):
        starts = [pl.multiple_of(jnp.minimum(lows[k] + r * win, cap - win), BF16_ROWS) for k in range(eg)]
        pieces = []
        for k in range(eg):
            p = pos_ref[0, k:k + 1, sub * t:(sub + 1) * t]
            hit = (p - starts[k] == j_col) & (p >= lows[k] + r * win)
            pieces.append(jnp.where(hit, 1.0, 0.0).astype(BF16))
        sel = jnp.concatenate(pieces, axis=0)
        rows = jnp.dot(sel, h_ref[sub * t:(sub + 1) * t, :], preferred_element_type=F32).astype(BF16)
        return starts, rows

    def deposit(starts, rows):
        for k in range(eg):
            dst = (0, k, pl.ds(starts[k], win), slice(None))
            xs_ref[dst] = xs_ref[dst] + rows[k * win:(k + 1) * win]

    plans = [_window_plan(off_ref, b, tb * nsub + sub, experts, ne) for sub in range(nsub)]
    firsts = [select(sub, plans[sub][0], 0) for sub in range(nsub)]
    for starts, rows in firsts:
        deposit(starts, rows)
    for sub in range(nsub):
        lows, ends = plans[sub]

        def extra_round(r, carry, lows=lows, sub=sub):
            deposit(*select(sub, lows, r))
            return carry

        lax.fori_loop(1, _window_rounds(lows, ends), extra_round, 0)


def _gatherx(off_flat, h2, pos, cap):
    n, d = h2.shape
    bsz, ne, l = pos.shape
    t = min(l, 2 * TOKEN_TILE)
    ntb = l // t
    eg = SUBLANES
    grid_spec = pltpu.PrefetchScalarGridSpec(
        num_scalar_prefetch=1,
        grid=(bsz, ne // eg, ntb),
        in_specs=[pl.BlockSpec((t, d), lambda b, g, i, off: (b * ntb + i, 0)),
                  pl.BlockSpec((1, eg, t), lambda b, g, i, off: (b, g, i))],
        out_specs=pl.BlockSpec((1, eg, cap, d), lambda b, g, i, off: (b, g, 0, 0)),
    )
    return pl.pallas_call(
        functools.partial(_gatherx_body, ne=ne, cap=cap, eg=eg),
        grid_spec=grid_spec,
        out_shape=jax.ShapeDtypeStruct((bsz, ne, cap, d), BF16),
        compiler_params=_cparams(("parallel", "parallel", "arbitrary")),
        name="gatherx",
    )(off_flat, h2, pos)


def _moe_body(xs_ref, wg_ref, wu_ref, wd_ref, y_ref, *, rc):
    wg = wg_ref[0].astype(BF16)
    wu = wu_ref[0].astype(BF16)
    wd = wd_ref[0].astype(BF16)
    cap = xs_ref.shape[2]
    acts = []
    for ch in range(cap // rc):
        xs = xs_ref[0, 0, ch * rc:(ch + 1) * rc, :]
        gate = jnp.dot(xs, wg, preferred_element_type=F32)
        up = jnp.dot(xs, wu, preferred_element_type=F32)
        acts.append((_silu(gate) * up).astype(BF16))
    for ch in range(cap // rc):
        y_ref[0, 0, ch * rc:(ch + 1) * rc, :] = jnp.dot(acts[ch], wd, preferred_element_type=F32).astype(BF16)


def _moe(xs, wg, wu, wd):
    bsz, ne, cap, d = xs.shape
    de = wg.shape[2]
    slot = pl.BlockSpec((1, 1, cap, d), lambda e, b: (b, e, 0, 0))
    return pl.pallas_call(
        functools.partial(_moe_body, rc=min(cap, 2 * MXU_DIM)),
        grid=(ne, bsz),
        in_specs=[slot,
                  pl.BlockSpec((1, d, de), lambda e, b: (e, 0, 0)),
                  pl.BlockSpec((1, d, de), lambda e, b: (e, 0, 0)),
                  pl.BlockSpec((1, de, d), lambda e, b: (e, 0, 0))],
        out_specs=slot,
        out_shape=jax.ShapeDtypeStruct((bsz, ne, cap, d), BF16),
        compiler_params=_cparams(("parallel", "arbitrary")),
        name="moe",
    )(xs, wg, wu, wd)


def _combine_body(off_ref, x1_ref, g2_ref, fg_ref, pos_ref, aff_ref, y_hbm, o_ref, ybuf, spare, acc_ref, sem,
                  spare_sem, *, ne, cap, nsp, nsub):
    b = pl.program_id(0)
    i = pl.program_id(1)
    step = b * nsp + i
    nsteps = pl.num_programs(0) * nsp
    slot = step % 2
    win = SLOT_WIN
    t = ROUTE_BLK
    experts = range(ne)

    def window_starts(lows, r):
        return [pl.multiple_of(jnp.minimum(lows[e] + r * win, cap - win), BF16_ROWS) for e in experts]

    def first_copy(bb, sub, e, start, buf):
        return pltpu.make_async_copy(y_hbm.at[bb, e, pl.ds(start, win), :],
                                     ybuf.at[buf, sub, pl.ds(e * win, win), :], sem.at[buf, sub, e])

    def spare_copy(e, start):
        return pltpu.make_async_copy(y_hbm.at[b, e, pl.ds(start, win), :],
                                     spare.at[pl.ds(e * win, win), :], spare_sem.at[e])

    def start_step(bb, ii, buf):
        for sub in range(nsub):
            starts = window_starts(_window_plan(off_ref, bb, ii * nsub + sub, experts, ne)[0], 0)
            for e in experts:
                first_copy(bb, sub, e, starts[e], buf).start()

    @pl.when(step == 0)
    def _():
        start_step(b, i, 0)

    nxt = jnp.minimum(step + 1, nsteps - 1)
    start_step(lax.div(nxt, jnp.int32(nsp)), lax.rem(nxt, jnp.int32(nsp)), 1 - slot)

    plans = [_window_plan(off_ref, b, i * nsub + sub, experts, ne) for sub in range(nsub)]
    j_col = lax.broadcasted_iota(I32, (win, 1), 0)

    def expand(sub, r, starts, rows_ref):
        lows = plans[sub][0]
        pieces = []
        for e in experts:
            p = pos_ref[0, e:e + 1, sub * t:(sub + 1) * t]
            valid = p >= lows[e] + r * win
            hit = (p - starts[e] == j_col) & valid
            pieces.append(jnp.where(hit, aff_ref[0, e:e + 1, sub * t:(sub + 1) * t], 0.0).astype(BF16))
        pmat = jnp.concatenate(pieces, axis=0)
        return lax.dot_general(pmat, rows_ref[...], _TN, preferred_element_type=F32)

    firsts = [window_starts(plans[sub][0], 0) for sub in range(nsub)]
    for sub in range(nsub):
        for e in experts:
            first_copy(b, sub, e, firsts[sub][e], slot).wait()
    for sub in range(nsub):
        acc_ref[sub] = expand(sub, 0, firsts[sub], ybuf.at[slot, sub])

    for sub in range(nsub):
        lows, ends = plans[sub]

        def extra_round(r, carry, lows=lows, sub=sub):
            starts = window_starts(lows, r)
            for e in experts:
                spare_copy(e, starts[e]).start()
            for e in experts:
                spare_copy(e, starts[e]).wait()
            acc_ref[sub] += expand(sub, r, starts, spare)
            return carry

        lax.fori_loop(1, _window_rounds(lows, ends), extra_round, 0)

    for sub in range(nsub):
        rows = pl.ds(sub * t, t)
        x2 = x1_ref[rows, :] + g2_ref[0] * acc_ref[sub]
        ms = jnp.mean(x2 * x2, axis=-1, keepdims=True)
        o_ref[rows, :] = x2 * lax.rsqrt(ms + EPS) * fg_ref[...]

    @pl.when(step == nsteps - 1)
    def _():
        for sub in range(nsub):
            for e in experts:
                first_copy(b, sub, e, firsts[sub][e], 1 - slot).wait()


def _combine(off_flat, x1, g2, fg, pos, aff, y, rows_per_sample):
    n, d = x1.shape
    bsz, ne, cap, _ = y.shape
    t = min(rows_per_sample, TOKEN_TILE)
    nsub = t // ROUTE_BLK
    nsp = rows_per_sample // t
    rows = ne * SLOT_WIN
    grid_spec = pltpu.PrefetchScalarGridSpec(
        num_scalar_prefetch=1,
        grid=(bsz, nsp),
        in_specs=[pl.BlockSpec((t, d), lambda b, i, off: (b * nsp + i, 0)),
                  pl.BlockSpec((1, 1, d), lambda b, i, off: (b, 0, 0)),
                  pl.BlockSpec((1, d), lambda b, i, off: (0, 0)),
                  pl.BlockSpec((1, ne, t), lambda b, i, off: (b, 0, i)),
                  pl.BlockSpec((1, ne, t), lambda b, i, off: (b, 0, i)),
                  pl.BlockSpec(memory_space=pl.ANY)],
        out_specs=pl.BlockSpec((t, d), lambda b, i, off: (b * nsp + i, 0)),
        scratch_shapes=[pltpu.VMEM((2, nsub, rows, d), BF16),
                        pltpu.VMEM((rows, d), BF16),
                        pltpu.VMEM((nsub, ROUTE_BLK, d), F32),
                        pltpu.SemaphoreType.DMA((2, nsub, ne)),
                        pltpu.SemaphoreType.DMA((ne,))],
    )
    return pl.pallas_call(
        functools.partial(_combine_body, ne=ne, cap=cap, nsp=nsp, nsub=nsub),
        grid_spec=grid_spec,
        out_shape=jax.ShapeDtypeStruct((n, d), F32),
        compiler_params=_cparams(("arbitrary", "arbitrary")),
        name="combine",
    )(off_flat, x1, g2, fg, pos, aff, y)


def kernel(x, c, ctx, c_ctx, ada_w, ada_b, norm1_g, norm2_g, w_in, w_decay_up, b_decay, gla_norm_g,
           w_gla_proj, pool_w, pool_scale, w_pool_proj, w_out, w_router, w_gate_e, w_up_e, w_down_e,
           final_norm_g):
    assert ada_w.shape[0] == 1, "single-layer block"
    bsz, l, d = x.shape
    lc = ctx.shape[1]
    rank, dk = w_decay_up.shape[2], w_decay_up.shape[3]
    dvh = gla_norm_g.shape[1]
    dv = w_gla_proj.shape[1]
    heads = dv // dvh
    groups, ch = pool_w.shape[1], pool_w.shape[2]
    pw = groups * ch
    ne = w_router.shape[2]
    cap = EC_CAPACITY * l // ne
    assert dk // heads == HEAD_DK and dvh == LANES and ch == LANES and heads % 2 == 0
    assert l % ROUTE_BLK == 0 and l // ROUTE_BLK < LANES and cap >= SLOT_WIN and cap % BF16_ROWS == 0
    assert GRID_W == GLA_CHUNK and MXU_DIM % GRID_W == 0 and (l // MXU_DIM) % 2 == 0

    cin = jnp.zeros((8, d), F32).at[:bsz].set(c).at[bsz].set(c_ctx)
    mods = _ada(cin, ada_w[0], ada_b[0][None, :])
    sh1, sc1, gt1, sh2, sc2, gt2 = [mods[:, i * d:(i + 1) * d] for i in range(6)]
    vec3 = lambda a: a[:, None, :]
    mult1 = norm1_g[0][None, :] * (1.0 + sc1)
    mult2 = norm2_g[0][None, :] * (1.0 + sc2)

    o_r = dk + dv
    o_q = o_r + 2 * rank
    o_g = o_q + dk
    o_p = o_g + dv
    o_m = o_p + pw
    up = jnp.zeros((2 * rank, 2 * dk), F32)
    up = up.at[:rank, :dk].set(w_decay_up[0, 0]).at[rank:, dk:].set(w_decay_up[0, 1])
    psc = pool_scale[0].reshape(groups, 1, ch)
    w1, wm, wpool = _prep(w_in[0].T, up, pool_w[0], psc, w_pool_proj[0], o_r, o_q, o_g, o_m,
                          float(dk // heads) ** -0.5)
    bz = b_decay[0]

    zero_state = jnp.zeros((bsz, heads // 2, LANES, 2 * LANES), F32)
    cm = jnp.broadcast_to(vec3(mult1[bsz:bsz + 1]), (bsz, 1, d))
    cs = jnp.broadcast_to(vec3(sh1[bsz:bsz + 1]), (bsz, 1, d))
    ck, cv, _, claf, clab, _, _ = _inproj(ctx.reshape(bsz * lc, d), cm, cs, w1, bz, lc, dk, dv, pw, min(lc, CTX_TILE))
    r3 = lambda a, n: a.reshape(bsz, n, a.shape[-1])
    _, _, h_f, h_b = _gla(r3(ck, lc), r3(ck, lc), r3(cv, lc), r3(claf, lc), r3(clab, lc),
                          zero_state, zero_state, min(lc, CTX_TILE))

    x2 = x.reshape(bsz * l, d)
    tm = min(l, TOKEN_TILE)
    k, v, q, laf, lab, sg, pin = _inproj(x2, vec3(mult1[:bsz]), vec3(sh1[:bsz]), w1, bz, l, dk, dv, pw, tm)
    of, ob, _, _ = _gla(r3(k, l), r3(q, l), r3(v, l), r3(laf, l), r3(lab, l), h_f, h_b, min(l, GLA_TILE))
    pooled = _pool(r3(pin, l), ch).reshape(bsz * l, pw)

    wr_hi = w_router[0].astype(BF16)
    wr_lo = (w_router[0] - wr_hi.astype(F32)).astype(BF16)
    wr = jnp.zeros((d, LANES), BF16).at[:, :ne].set(wr_hi).at[:, ne:2 * ne].set(wr_lo)
    vecs = [vec3(mult1[:bsz]), vec3(sh1[:bsz]), vec3(gt1[:bsz]), vec3(mult2[:bsz]), vec3(sh2[:bsz])]
    x1, h2, aff = _merge(x2, vecs, of.reshape(bsz * l, dv), ob.reshape(bsz * l, dv), sg, pooled,
                          wm, w_gla_proj[0].astype(BF16), wpool, w_out[0].astype(BF16),
                          gla_norm_g[0][None, :], wr, l, heads, ne, tm)

    pos, offs = _route(aff.reshape(1, bsz * ne, l), cap)
    pos = pos.reshape(bsz, ne, l)
    off_flat = offs.reshape(-1)
    xs = _gatherx(off_flat, h2, pos, cap)
    y = _moe(xs, w_gate_e[0], w_up_e[0], w_down_e[0])
    out = _combine(off_flat, x1, vec3(gt2[:bsz]), final_norm_g[None, :], pos, aff, y, l)
    return out.reshape(bsz, l, d)
```

```python
import functools

import jax
import jax.numpy as jnp
from jax import lax
from jax.experimental import pallas as pl
from jax.experimental.pallas import tpu as pltpu

F32 = jnp.float32
BF16 = jnp.bfloat16
I32 = jnp.int32
HIGHEST = lax.Precision.HIGHEST

EPS = 1e-6
GRID_W = 64
GLA_CHUNK = 64
GLA_STAGES = 6
GLA_BATCH = 8
GATE_NORMALIZER = 16.0
POOL_WINDOWS = (2, 4, 8, 16)
EC_CAPACITY = 2

LANES = 128
SUBLANES = 8
BF16_ROWS = 16
MXU_DIM = 256
VMEM_BYTES = 64 * 1024 * 1024
VMEM_LIMIT = VMEM_BYTES * 7 // 8

HEAD_DK = LANES // 2
ROUTE_BLK = MXU_DIM
SLOT_WIN = 64
SUB_ROWS = MXU_DIM
TOKEN_TILE = 1024
GLA_TILE = 2048
CTX_TILE = 256

_NT = (((1,), (1,)), ((), ()))
_TN = (((0,), (0,)), ((), ()))


def _cparams(sem):
    return pltpu.CompilerParams(dimension_semantics=sem, vmem_limit_bytes=VMEM_LIMIT)


def _silu(x):
    return x * jax.nn.sigmoid(x)


def _row_groups(n):
    step = min(n, SUB_ROWS)
    return [pl.ds(i, step) for i in range(0, n, step)]


def _staggered(gens, nstages):
    for t in range(nstages + len(gens) - 1):
        for g in reversed(range(len(gens))):
            if 0 <= t - g < nstages:
                next(gens[g])


def _ada_body(c_ref, w_ref, b_ref, o_ref):
    s = _silu(c_ref[...])
    rows = s.shape[0]
    s_hi = s.astype(BF16)
    s_lo = (s - s_hi.astype(F32)).astype(BF16)
    w = w_ref[...]
    w_hi = w.astype(BF16)
    w_lo = (w - w_hi.astype(F32)).astype(BF16)
    both = jnp.dot(jnp.concatenate([s_hi, s_lo], axis=0), w_hi, preferred_element_type=F32)
    o_ref[...] = (both[:rows] + both[rows:] + jnp.dot(s_hi, w_lo, preferred_element_type=F32)) + b_ref[...]


def _ada(cin, w, b):
    rows, d = cin.shape
    n = w.shape[1]
    nb = n // 4
    return pl.pallas_call(
        _ada_body,
        grid=(n // nb,),
        in_specs=[pl.BlockSpec((rows, d), lambda j: (0, 0)),
                  pl.BlockSpec((d, nb), lambda j: (0, j)),
                  pl.BlockSpec((1, nb), lambda j: (0, j))],
        out_specs=pl.BlockSpec((rows, nb), lambda j: (0, j)),
        out_shape=jax.ShapeDtypeStruct((rows, n), F32),
        compiler_params=_cparams(("arbitrary",)),
        name="ada",
    )(cin, w, b)


def _prep_body(w_ref, up_ref, pw_ref, ps_ref, wpp_ref, w1_ref, wm_ref, wpf_ref, *, o_r, o_q, o_g, o_m, qscale):
    dk = o_g - o_q
    wz = lax.dot_general(w_ref[o_r:o_q, :], up_ref[...], _TN, precision=HIGHEST, preferred_element_type=F32)
    nz = wz.shape[1]
    w1_ref[:, :o_r] = w_ref[:o_r, :].T.astype(BF16)
    w1_ref[:, o_r:o_r + dk] = (w_ref[o_q:o_g, :].T * qscale).astype(BF16)
    w1_ref[:, o_r + dk:o_r + dk + nz] = wz.astype(BF16)
    w1_ref[:, o_r + dk + nz:] = w_ref[o_g:o_m, :].T.astype(BF16)
    wm_ref[...] = w_ref[o_m:, :].T.astype(BF16)
    wpf_ref[...] = jnp.dot(pw_ref[0] * ps_ref[0], wpp_ref[...], precision=HIGHEST,
                           preferred_element_type=F32).astype(BF16)


def _prep(w_in_t, up, pool_w, pool_scale3, w_pool_proj, o_r, o_q, o_g, o_m, qscale):
    n, d = w_in_t.shape
    n1 = o_r + (o_g - o_q) + up.shape[1] + (o_m - o_g)
    groups, ch, _ = pool_w.shape
    tr = d // groups
    dm = w_pool_proj.shape[1]
    return pl.pallas_call(
        functools.partial(_prep_body, o_r=o_r, o_q=o_q, o_g=o_g, o_m=o_m, qscale=qscale),
        grid=(d // tr,),
        in_specs=[pl.BlockSpec((n, tr), lambda i: (0, i)),
                  pl.BlockSpec(up.shape, lambda i: (0, 0)),
                  pl.BlockSpec((1, ch, ch), lambda i: (i, 0, 0)),
                  pl.BlockSpec((1, 1, ch), lambda i: (i, 0, 0)),
                  pl.BlockSpec((ch, dm), lambda i: (i, 0))],
        out_specs=[pl.BlockSpec((tr, n1), lambda i: (i, 0)),
                   pl.BlockSpec((tr, n - o_m), lambda i: (i, 0)),
                   pl.BlockSpec((ch, dm), lambda i: (i, 0))],
        out_shape=[jax.ShapeDtypeStruct((d, n1), BF16), jax.ShapeDtypeStruct((d, n - o_m), BF16),
                   jax.ShapeDtypeStruct((groups * ch, dm), BF16)],
        compiler_params=_cparams(("parallel",)),
        name="prep",
    )(w_in_t, up, pool_w, pool_scale3, w_pool_proj)


def _modulated_norm(x, mult, shift):
    ms = jnp.mean(x * x, axis=-1, keepdims=True)
    return (x * lax.rsqrt(ms + EPS)) * mult + shift


def _log_sigmoid(z):
    return jnp.minimum(z, 0.0) - jnp.log1p(jnp.exp(-jnp.abs(z)))


def _inproj_body(x_ref, mult_ref, shift_ref, w_ref, bz_ref,
                 k_ref, v_ref, q_ref, laf_ref, lab_ref, sg_ref, p_ref, *, dk, dv, pw):
    def sub_tile(rows):
        h = _modulated_norm(x_ref[rows, :], mult_ref[0], shift_ref[0]).astype(BF16)
        yield
        u = jnp.dot(h, w_ref[...], preferred_element_type=F32)
        yield
        o = 0
        k_ref[rows, :] = u[:, o:o + dk].astype(BF16); o += dk
        v_ref[rows, :] = u[:, o:o + dv].astype(BF16); o += dv
        q_ref[rows, :] = u[:, o:o + dk].astype(BF16); o += dk
        zf = u[:, o:o + dk] + bz_ref[0:1, :]; o += dk
        zb = u[:, o:o + dk] + bz_ref[1:2, :]; o += dk
        laf_ref[rows, :] = _log_sigmoid(zf) * (1.0 / GATE_NORMALIZER)
        lab_ref[rows, :] = _log_sigmoid(zb) * (1.0 / GATE_NORMALIZER)
        sg_ref[rows, :] = _silu(u[:, o:o + dv]).astype(BF16); o += dv
        p_ref[rows, :] = u[:, o:o + pw].astype(BF16)
        yield

    _staggered([sub_tile(r) for r in _row_groups(x_ref.shape[0])], 3)


def _inproj(x2, mult, shift, w, bz, rows_per_sample, dk, dv, pw, tm):
    n, d = x2.shape
    tps = rows_per_sample // tm
    row = lambda i: (i, 0)
    vec = lambda i: (i // tps, 0, 0)
    outs = [(dk, BF16), (dv, BF16), (dk, BF16), (dk, F32), (dk, F32), (dv, BF16), (pw, BF16)]
    return pl.pallas_call(
        functools.partial(_inproj_body, dk=dk, dv=dv, pw=pw),
        grid=(n // tm,),
        in_specs=[pl.BlockSpec((tm, d), row),
                  pl.BlockSpec((1, 1, d), vec),
                  pl.BlockSpec((1, 1, d), vec),
                  pl.BlockSpec(w.shape, lambda i: (0, 0), pipeline_mode=pl.Buffered(1)),
                  pl.BlockSpec(bz.shape, lambda i: (0, 0))],
        out_specs=[pl.BlockSpec((tm, c), row) for c, _ in outs],
        out_shape=[jax.ShapeDtypeStruct((n, c), t) for c, t in outs],
        compiler_params=_cparams(("parallel",)),
        name="inproj",
    )(x2, mult, shift, w, bz)


def _gla_direction(k_ref, q_ref, v_ref, la_ref, o_ref, s_ref, reverse, nchunk):
    C = GLA_CHUNK
    hd = HEAD_DK
    cb = MXU_DIM
    lt = nchunk * C
    la = la_ref[0]
    r = lax.broadcasted_iota(I32, (cb, cb), 0)
    c = lax.broadcasted_iota(I32, (cb, cb), 1)
    same = (r // C) == (c // C)
    cum = jnp.where(same & ((c >= r) if reverse else (c <= r)), 1.0, 0.0).astype(BF16)
    la_hi = la.astype(BF16)
    la_lo = (la - la_hi.astype(F32)).astype(BF16)
    la2 = jnp.concatenate([la_hi, la_lo], axis=1)
    bcs = []
    for blk in range(lt // cb):
        part = jnp.dot(cum, la2[blk * cb:(blk + 1) * cb], preferred_element_type=F32)
        bcs.append(part[:, :LANES] + part[:, LANES:])
    ri = lax.broadcasted_iota(I32, (C, 2 * C), 0)
    ci = lax.broadcasted_iota(I32, (C, 2 * C), 1) % C
    tri = (ci >= ri) if reverse else (ci <= ri)
    lane = lax.broadcasted_iota(I32, (1, LANES), 1)
    m0 = (lane < hd).astype(F32)
    m1 = (lane >= hd).astype(F32)
    sr = lax.broadcasted_iota(I32, (LANES, 2 * LANES), 0)
    sl = lax.broadcasted_iota(I32, (LANES, 2 * LANES), 1)
    smask = ((sr < hd) == (sl < LANES)).astype(F32)
    kt = k_ref[0].astype(F32)
    qt = q_ref[0].astype(F32)
    zero_v = jnp.zeros((C, LANES), BF16)
    sweep = list(range(nchunk - 1, -1, -1) if reverse else range(nchunk))
    for first in range(0, nchunk, GLA_BATCH):
        order = sweep[first:first + GLA_BATCH]
        intra, qds, kvs, decs, q2s, kss, kws, scs = {}, {}, {}, {}, {}, {}, {}, {}
        for ch in order:
            lo = ch * C
            b = bcs[lo // cb][lo % cb:lo % cb + C]
            last = b[0:1] if reverse else b[C - 1:C]
            mid = b[C // 2:C // 2 + 1] if reverse else b[C // 2 - 1:C // 2]
            kc = kt[lo:lo + C]
            qc = qt[lo:lo + C]
            q2s[ch] = (qc * jnp.exp(b - mid)).astype(BF16)
            ks = kc * jnp.exp(mid - b)
            kss[ch] = jnp.concatenate([ks * m0, ks * m1], axis=0).astype(BF16)
            qds[ch] = (qc * jnp.exp(b)).astype(BF16)
            kws[ch] = (kc * jnp.exp(last - b)).astype(BF16)
            decs[ch] = last
        yield
        for ch in order:
            scs[ch] = lax.dot_general(q2s[ch], kss[ch], _NT, preferred_element_type=F32)
        yield
        for ch in order:
            v2 = v_ref[0, ch * C:(ch + 1) * C, :]
            kvs[ch] = lax.dot_general(kws[ch], v2, _TN, preferred_element_type=F32) * smask
        yield
        for ch in order:
            sc2 = jnp.where(tri, scs[ch], 0.0).astype(BF16)
            v2 = v_ref[0, ch * C:(ch + 1) * C, :]
            vbd = jnp.concatenate([jnp.concatenate([v2[:, :LANES], zero_v], axis=1),
                                   jnp.concatenate([zero_v, v2[:, LANES:]], axis=1)], axis=0)
            intra[ch] = jnp.dot(sc2, vbd, preferred_element_type=F32)
        yield
        pad = jnp.zeros((LANES - len(order), LANES), F32)
        dec_cols = jnp.exp(jnp.concatenate([decs[ch] for ch in order] + [pad], axis=0).T)
        st = s_ref[...]
        starts = {}
        for i, ch in enumerate(order):
            starts[ch] = st.astype(BF16)
            st = st * dec_cols[:, i:i + 1] + kvs[ch]
        s_ref[...] = st
        yield
        for ch in order:
            inter = jnp.dot(qds[ch], starts[ch], preferred_element_type=F32)
            o_ref[0, ch * C:(ch + 1) * C, :] = inter + intra[ch]
        yield


def _gla_body(kf, qf, vf, laf, kb, qb, vb, lab, h0f, h0b, of, ob, hf_out, hb_out, sf, sb, *, nchunk):
    i = pl.program_id(2)

    @pl.when(i == 0)
    def _():
        sf[...] = h0f[0, 0]
        sb[...] = h0b[0, 0]

    sweeps = [_gla_direction(kf, qf, vf, laf, of, sf, False, nchunk),
              _gla_direction(kb, qb, vb, lab, ob, sb, True, nchunk)]
    for _ in range(GLA_STAGES * pl.cdiv(nchunk, GLA_BATCH)):
        for sweep in sweeps:
            next(sweep)

    @pl.when(i == pl.num_programs(2) - 1)
    def _():
        hf_out[0, 0] = sf[...]
        hb_out[0, 0] = sb[...]


def _gla(k, q, v, laf, lab, h0f, h0b, lt):
    bsz, l, _ = k.shape
    pairs = h0f.shape[1]
    nt = l // lt
    fwd = lambda b, hp, i: (b, i, hp)
    bwd = lambda b, hp, i: (b, nt - 1 - i, hp)
    st = lambda b, hp, i: (b, hp, 0, 0)
    kq = lambda m: pl.BlockSpec((1, lt, LANES), m)
    vv = lambda m: pl.BlockSpec((1, lt, 2 * LANES), m)
    sspec = pl.BlockSpec((1, 1, LANES, 2 * LANES), st)
    return pl.pallas_call(
        functools.partial(_gla_body, nchunk=lt // GLA_CHUNK),
        grid=(bsz, pairs, nt),
        in_specs=[kq(fwd), kq(fwd), vv(fwd), kq(fwd), kq(bwd), kq(bwd), vv(bwd), kq(bwd), sspec, sspec],
        out_specs=[vv(fwd), vv(bwd), sspec, sspec],
        out_shape=[jax.ShapeDtypeStruct(v.shape, F32), jax.ShapeDtypeStruct(v.shape, F32),
                   jax.ShapeDtypeStruct(h0f.shape, F32), jax.ShapeDtypeStruct(h0b.shape, F32)],
        scratch_shapes=[pltpu.VMEM((LANES, 2 * LANES), F32), pltpu.VMEM((LANES, 2 * LANES), F32)],
        compiler_params=_cparams(("parallel", "parallel", "arbitrary")),
        name="gla",
    )(k, q, v, laf, k, q, v, lab, h0f, h0b)


def _pool_body(p_ref, o_ref, s1_ref, *, half, rows):
    xb = p_ref[0]
    l, ch = xb.shape

    def inv_counts(pos, n):
        return 1.0 / (jnp.minimum(pos + half, n) - jnp.maximum(pos - half, 0)).astype(F32)

    blk = MXU_DIM
    r = lax.broadcasted_iota(I32, (blk, blk), 0)
    c = lax.broadcasted_iota(I32, (blk, blk), 1)
    band = ((r // GRID_W == c // GRID_W) & (c - r >= -half) & (c - r <= half - 1)).astype(BF16)
    for i in range(0, l // blk, 2):
        pair = jnp.concatenate([xb[i * blk:(i + 1) * blk], xb[(i + 1) * blk:(i + 2) * blk]], axis=1)
        sums = jnp.dot(band, pair, preferred_element_type=F32)
        s1_ref[i * blk:(i + 1) * blk, :] = sums[:, :ch]
        s1_ref[(i + 1) * blk:(i + 2) * blk, :] = sums[:, ch:]
    s1 = s1_ref[...].reshape(rows, GRID_W, ch)

    def shifted(a, s):
        z = jnp.zeros((abs(s),) + a.shape[1:], a.dtype)
        return jnp.concatenate([a[s:], z], axis=0) if s > 0 else jnp.concatenate([z, a[:s]], axis=0)

    fwd = s1
    bwd = shifted(s1, -1)
    s = 1
    while s < half:
        fwd = fwd + shifted(fwd, s)
        bwd = bwd + shifted(bwd, -s)
        s *= 2
    inv_r = inv_counts(lax.broadcasted_iota(I32, (rows, 1, ch), 0), rows)
    inv_c = inv_counts(lax.broadcasted_iota(I32, (1, GRID_W, ch), 1), GRID_W)
    pooled = (fwd + bwd) * inv_r * inv_c - xb.astype(F32).reshape(rows, GRID_W, ch)
    o_ref[0] = pooled.reshape(l, ch).astype(BF16)


def _pool_groups_body(p_ref, o_ref, s1_ref, *, rows):
    for gi, window in enumerate(POOL_WINDOWS):
        @pl.when(pl.program_id(1) == gi)
        def _(half=window // 2):
            _pool_body(p_ref, o_ref, s1_ref, half=half, rows=rows)


def _pool(pin, ch):
    bsz, l, pw = pin.shape
    return pl.pallas_call(
        functools.partial(_pool_groups_body, rows=l // GRID_W),
        grid=(bsz, pw // ch),
        in_specs=[pl.BlockSpec((1, l, ch), lambda b, g: (b, 0, g))],
        out_specs=pl.BlockSpec((1, l, ch), lambda b, g: (b, 0, g)),
        out_shape=jax.ShapeDtypeStruct((bsz, l, pw), BF16),
        scratch_shapes=[pltpu.VMEM((l, ch), F32)],
        compiler_params=_cparams(("parallel", "parallel")),
        name="pool",
    )(pin)


def _merge_body(x_ref, m1_ref, s1_ref, g1_ref, m2_ref, s2_ref,
                of_ref, ob_ref, sg_ref, mx_ref,
                wm_ref, wgla_ref, wpool_ref, wout_ref, gn_ref, wr_ref,
                x1_ref, h2_ref, aff_ref, *, heads, ne):
    d = x_ref.shape[1]

    def sub_tile(rows):
        bp = jnp.dot(mx_ref[rows, :], wpool_ref[...], preferred_element_type=F32)
        x = x_ref[rows, :]
        h = _modulated_norm(x, m1_ref[0], s1_ref[0]).astype(BF16)
        o = of_ref[rows, :] + ob_ref[rows, :]
        sg = sg_ref[rows, :].astype(F32)
        og = []
        for j in range(heads):
            oj = o[:, j * LANES:(j + 1) * LANES]
            oj = oj * lax.rsqrt(jnp.mean(oj * oj, axis=-1, keepdims=True) + EPS) * gn_ref[...]
            og.append((oj * sg[:, j * LANES:(j + 1) * LANES]).astype(BF16))
        og = jnp.concatenate(og, axis=1)
        yield
        gates = jnp.dot(h, wm_ref[...], preferred_element_type=F32)
        bg = jnp.dot(og, wgla_ref[...], preferred_element_type=F32)
        yield
        gates = jax.nn.sigmoid(gates)
        z = (gates[:, :d] * bg + gates[:, d:] * bp).astype(BF16)
        yield
        y = jnp.dot(z, wout_ref[...], preferred_element_type=F32)
        yield
        x1 = x + g1_ref[0] * y
        x1_ref[rows, :] = x1
        h2 = _modulated_norm(x1, m2_ref[0], s2_ref[0])
        hi = h2.astype(BF16)
        h2_ref[rows, :] = hi
        lo = (h2 - hi.astype(F32)).astype(BF16)
        yield
        lg = (jnp.dot(hi, wr_ref[...], preferred_element_type=F32)
              + jnp.dot(lo, wr_ref[...], preferred_element_type=F32))
        yield
        lgt = lg.T
        logit = lgt[0:ne] + lgt[ne:2 * ne]
        mx = jnp.max(logit, axis=0, keepdims=True)
        ex = jnp.exp(logit - mx)
        aff_ref[0, :, rows] = ex / jnp.sum(ex, axis=0, keepdims=True)
        yield

    _staggered([sub_tile(r) for r in _row_groups(x_ref.shape[0])], 7)


def _merge(x2, vecs, of, ob, sg, mixed, wm, wgla, wpool, wout, gn, wr, rows_per_sample, heads, ne, tm):
    n, d = x2.shape
    bsz = n // rows_per_sample
    tps = rows_per_sample // tm
    row = lambda i: (i, 0)
    vec = lambda i: (i // tps, 0, 0)
    full = lambda a: pl.BlockSpec(a.shape, lambda i: (0,) * a.ndim, pipeline_mode=pl.Buffered(1))
    gv = of.shape[-1]
    return pl.pallas_call(
        functools.partial(_merge_body, heads=heads, ne=ne),
        grid=(n // tm,),
        in_specs=[pl.BlockSpec((tm, d), row)] + [pl.BlockSpec((1, 1, d), vec)] * 5
                 + [pl.BlockSpec((tm, gv), row)] * 4
                 + [full(wm), full(wgla), full(wpool), full(wout), full(gn), full(wr)],
        out_specs=[pl.BlockSpec((tm, d), row),
                   pl.BlockSpec((tm, d), row),
                   pl.BlockSpec((1, ne, tm), lambda i: (i // tps, 0, i % tps))],
        out_shape=[jax.ShapeDtypeStruct((n, d), F32),
                   jax.ShapeDtypeStruct((n, d), BF16),
                   jax.ShapeDtypeStruct((bsz, ne, rows_per_sample), F32)],
        compiler_params=_cparams(("parallel",)),
        name="merge",
    )(x2, *vecs, of, ob, sg, mixed, wm, wgla, wpool, wout, gn, wr)


def _route_body(aff_ref, pos_ref, off_ref, cnt_ref, *, cap, ntb):
    a = aff_ref[0]
    ne, l = a.shape
    blk = ROUTE_BLK

    def bisect(i, v):
        cand = v | jnp.left_shift(jnp.int32(1), 30 - i)
        cnt = jnp.sum((a >= lax.bitcast_convert_type(cand, F32)).astype(F32), axis=1, keepdims=True)
        return jnp.where(cnt >= cap, cand, v)

    thr = lax.bitcast_convert_type(lax.fori_loop(0, 31, bisect, jnp.zeros((ne, 1), I32)), F32)
    gt = a > thr
    tie = a == thr
    need = cap - jnp.sum(gt.astype(F32), axis=1, keepdims=True)

    r = lax.broadcasted_iota(I32, (blk, blk), 0)
    c = lax.broadcasted_iota(I32, (blk, blk), 1)
    upper = (r <= c).astype(BF16)
    lane = lax.broadcasted_iota(I32, (1, LANES), 1)

    def prefix(mask_f):
        run = jnp.zeros((ne, 1), F32)
        offs = jnp.zeros((ne, LANES), F32)
        for tb in range(ntb):
            m = mask_f[:, tb * blk:(tb + 1) * blk].astype(BF16)
            loc = jnp.dot(m, upper, preferred_element_type=F32)
            cnt_ref[:, tb * blk:(tb + 1) * blk] = loc + run
            offs = jnp.where(lane == tb, run, offs)
            run = run + loc[:, blk - 1:blk]
        return jnp.where(lane >= ntb, run, offs)

    tie_f = tie.astype(F32)
    prefix(tie_f)
    tie_excl = cnt_ref[...] - tie_f
    sel = gt | (tie & (tie_excl < need))
    offs = prefix(sel.astype(F32))
    pos_ref[0] = jnp.where(sel, cnt_ref[...] - 1.0, -1.0).astype(I32)
    off_ref[0] = offs.astype(I32)


def _route(aff, cap):
    bsz, ne, l = aff.shape
    spec = lambda s: pl.BlockSpec((1,) + s, lambda b: (b, 0, 0))
    return pl.pallas_call(
        functools.partial(_route_body, cap=cap, ntb=l // ROUTE_BLK),
        grid=(bsz,),
        in_specs=[spec((ne, l))],
        out_specs=[spec((ne, l)), spec((ne, LANES))],
        out_shape=[jax.ShapeDtypeStruct((bsz, ne, l), I32),
                   jax.ShapeDtypeStruct((bsz, ne, LANES), I32)],
        scratch_shapes=[pltpu.VMEM((ne, l), F32)],
        compiler_params=_cparams(("parallel",)),
        name="route",
    )(aff)


def _window_plan(off_ref, bb, tt, experts, ne):
    lows = [off_ref[(bb * ne + e) * LANES + tt] & -BF16_ROWS for e in experts]
    ends = [off_ref[(bb * ne + e) * LANES + tt + 1] for e in experts]
    return lows, ends


def _window_rounds(lows, ends):
    rounds = jnp.int32(0)
    for lo, hi in zip(lows, ends):
        rounds = jnp.maximum(rounds, lax.div(hi - lo + (SLOT_WIN - 1), jnp.int32(SLOT_WIN)))
    return rounds


def _gatherx_body(off_ref, h_ref, pos_ref, xs_ref, *, ne, cap, eg):
    b = pl.program_id(0)
    g = pl.program_id(1)
    tb = pl.program_id(2)
    win = SLOT_WIN

    @pl.when(tb == 0)
    def _():
        xs_ref[...] = jnp.zeros(xs_ref.shape, BF16)

    j_col = lax.broadcasted_iota(I32, (win, 1), 0)
    t = ROUTE_BLK
    nsub = h_ref.shape[0] // t
    experts = [g * eg + k for k in range(eg)]

    def select(sub, lows, r):
        starts = [pl.multiple_of(jnp.minimum(lows[k] + r * win, cap - win), BF16_ROWS) for k in range(eg)]
        pieces = []
        for k in range(eg):
            p = pos_ref[0, k:k + 1, sub * t:(sub + 1) * t]
            hit = (p - starts[k] == j_col) & (p >= lows[k] + r * win)
            pieces.append(jnp.where(hit, 1.0, 0.0).astype(BF16))
        sel = jnp.concatenate(pieces, axis=0)
        rows = jnp.dot(sel, h_ref[sub * t:(sub + 1) * t, :], preferred_element_type=F32).astype(BF16)
        return starts, rows

    def deposit(starts, rows):
        for k in range(eg):
            dst = (0, k, pl.ds(starts[k], win), slice(None))
            xs_ref[dst] = xs_ref[dst] + rows[k * win:(k + 1) * win]

    plans = [_window_plan(off_ref, b, tb * nsub + sub, experts, ne) for sub in range(nsub)]
    firsts = [select(sub, plans[sub][0], 0) for sub in range(nsub)]
    for starts, rows in firsts:
        deposit(starts, rows)
    for sub in range(nsub):
        lows, ends = plans[sub]

        def extra_round(r, carry, lows=lows, sub=sub):
            deposit(*select(sub, lows, r))
            return carry

        lax.fori_loop(1, _window_rounds(lows, ends), extra_round, 0)


def _gatherx(off_flat, h2, pos, cap):
    n, d = h2.shape
    bsz, ne, l = pos.shape
    t = min(l, 2 * TOKEN_TILE)
    ntb = l // t
    eg = SUBLANES
    grid_spec = pltpu.PrefetchScalarGridSpec(
        num_scalar_prefetch=1,
        grid=(bsz, ne // eg, ntb),
        in_specs=[pl.BlockSpec((t, d), lambda b, g, i, off: (b * ntb + i, 0)),
                  pl.BlockSpec((1, eg, t), lambda b, g, i, off: (b, g, i))],
        out_specs=pl.BlockSpec((1, eg, cap, d), lambda b, g, i, off: (b, g, 0, 0)),
    )
    return pl.pallas_call(
        functools.partial(_gatherx_body, ne=ne, cap=cap, eg=eg),
        grid_spec=grid_spec,
        out_shape=jax.ShapeDtypeStruct((bsz, ne, cap, d), BF16),
        compiler_params=_cparams(("parallel", "parallel", "arbitrary")),
        name="gatherx",
    )(off_flat, h2, pos)


def _moe_body(xs_ref, wg_ref, wu_ref, wd_ref, y_ref, *, rc):
    wg = wg_ref[0].astype(BF16)
    wu = wu_ref[0].astype(BF16)
    wd = wd_ref[0].astype(BF16)
    cap = xs_ref.shape[2]
    acts = []
    for ch in range(cap // rc):
        xs = xs_ref[0, 0, ch * rc:(ch + 1) * rc, :]
        gate = jnp.dot(xs, wg, preferred_element_type=F32)
        up = jnp.dot(xs, wu, preferred_element_type=F32)
        acts.append((_silu(gate) * up).astype(BF16))
    for ch in range(cap // rc):
        y_ref[0, 0, ch * rc:(ch + 1) * rc, :] = jnp.dot(acts[ch], wd, preferred_element_type=F32).astype(BF16)


def _moe(xs, wg, wu, wd):
    bsz, ne, cap, d = xs.shape
    de = wg.shape[2]
    slot = pl.BlockSpec((1, 1, cap, d), lambda e, b: (b, e, 0, 0))
    return pl.pallas_call(
        functools.partial(_moe_body, rc=min(cap, 2 * MXU_DIM)),
        grid=(ne, bsz),
        in_specs=[slot,
                  pl.BlockSpec((1, d, de), lambda e, b: (e, 0, 0)),
                  pl.BlockSpec((1, d, de), lambda e, b: (e, 0, 0)),
                  pl.BlockSpec((1, de, d), lambda e, b: (e, 0, 0))],
        out_specs=slot,
        out_shape=jax.ShapeDtypeStruct((bsz, ne, cap, d), BF16),
        compiler_params=_cparams(("parallel", "arbitrary")),
        name="moe",
    )(xs, wg, wu, wd)


def _combine_body(off_ref, x1_ref, g2_ref, fg_ref, pos_ref, aff_ref, y_hbm, o_ref, ybuf, spare, acc_ref, sem,
                  spare_sem, *, ne, cap, nsp, nsub):
    b = pl.program_id(0)
    i = pl.program_id(1)
    step = b * nsp + i
    nsteps = pl.num_programs(0) * nsp
    slot = step % 2
    win = SLOT_WIN
    t = ROUTE_BLK
    experts = range(ne)

    def window_starts(lows, r):
        return [pl.multiple_of(jnp.minimum(lows[e] + r * win, cap - win), BF16_ROWS) for e in experts]

    def first_copy(bb, sub, e, start, buf):
        return pltpu.make_async_copy(y_hbm.at[bb, e, pl.ds(start, win), :],
                                     ybuf.at[buf, sub, pl.ds(e * win, win), :], sem.at[buf, sub, e])

    def spare_copy(e, start):
        return pltpu.make_async_copy(y_hbm.at[b, e, pl.ds(start, win), :],
                                     spare.at[pl.ds(e * win, win), :], spare_sem.at[e])

    def start_step(bb, ii, buf):
        for sub in range(nsub):
            starts = window_starts(_window_plan(off_ref, bb, ii * nsub + sub, experts, ne)[0], 0)
            for e in experts:
                first_copy(bb, sub, e, starts[e], buf).start()

    @pl.when(step == 0)
    def _():
        start_step(b, i, 0)

    nxt = jnp.minimum(step + 1, nsteps - 1)
    start_step(lax.div(nxt, jnp.int32(nsp)), lax.rem(nxt, jnp.int32(nsp)), 1 - slot)

    plans = [_window_plan(off_ref, b, i * nsub + sub, experts, ne) for sub in range(nsub)]
    j_col = lax.broadcasted_iota(I32, (win, 1), 0)

    def expand(sub, r, starts, rows_ref):
        lows = plans[sub][0]
        pieces = []
        for e in experts:
            p = pos_ref[0, e:e + 1, sub * t:(sub + 1) * t]
            valid = p >= lows[e] + r * win
            hit = (p - starts[e] == j_col) & valid
            pieces.append(jnp.where(hit, aff_ref[0, e:e + 1, sub * t:(sub + 1) * t], 0.0).astype(BF16))
        pmat = jnp.concatenate(pieces, axis=0)
        return lax.dot_general(pmat, rows_ref[...], _TN, preferred_element_type=F32)

    acc_ref[...] = jnp.zeros(acc_ref.shape, F32)
    for sub in range(nsub):
        lows, ends = plans[sub]

        def extra_round(r, carry, lows=lows, sub=sub):
            starts = window_starts(lows, r)
            for e in experts:
                spare_copy(e, starts[e]).start()
            for e in experts:
                spare_copy(e, starts[e]).wait()
            acc_ref[sub] += expand(sub, r, starts, spare)
            return carry

        lax.fori_loop(1, _window_rounds(lows, ends), extra_round, 0)

    firsts = [window_starts(plans[sub][0], 0) for sub in range(nsub)]
    for sub in range(nsub):
        for e in experts:
            first_copy(b, sub, e, firsts[sub][e], slot).wait()
    for sub in range(nsub):
        rows = pl.ds(sub * t, t)
        moe = expand(sub, 0, firsts[sub], ybuf.at[slot, sub]) + acc_ref[sub]
        x2 = x1_ref[rows, :] + g2_ref[0] * moe
        ms = jnp.mean(x2 * x2, axis=-1, keepdims=True)
        o_ref[rows, :] = x2 * lax.rsqrt(ms + EPS) * fg_ref[...]

    @pl.when(step == nsteps - 1)
    def _():
        for sub in range(nsub):
            for e in experts:
                first_copy(b, sub, e, firsts[sub][e], 1 - slot).wait()


def _combine(off_flat, x1, g2, fg, pos, aff, y, rows_per_sample):
    n, d = x1.shape
    bsz, ne, cap, _ = y.shape
    t = min(rows_per_sample, TOKEN_TILE)
    nsub = t // ROUTE_BLK
    nsp = rows_per_sample // t
    rows = ne * SLOT_WIN
    grid_spec = pltpu.PrefetchScalarGridSpec(
        num_scalar_prefetch=1,
        grid=(bsz, nsp),
        in_specs=[pl.BlockSpec((t, d), lambda b, i, off: (b * nsp + i, 0)),
                  pl.BlockSpec((1, 1, d), lambda b, i, off: (b, 0, 0)),
                  pl.BlockSpec((1, d), lambda b, i, off: (0, 0)),
                  pl.BlockSpec((1, ne, t), lambda b, i, off: (b, 0, i)),
                  pl.BlockSpec((1, ne, t), lambda b, i, off: (b, 0, i)),
                  pl.BlockSpec(memory_space=pl.ANY)],
        out_specs=pl.BlockSpec((t, d), lambda b, i, off: (b * nsp + i, 0)),
        scratch_shapes=[pltpu.VMEM((2, nsub, rows, d), BF16),
                        pltpu.VMEM((rows, d), BF16),
                        pltpu.VMEM((nsub, ROUTE_BLK, d), F32),
                        pltpu.SemaphoreType.DMA((2, nsub, ne)),
                        pltpu.SemaphoreType.DMA((ne,))],
    )
    return pl.pallas_call(
        functools.partial(_combine_body, ne=ne, cap=cap, nsp=nsp, nsub=nsub),
        grid_spec=grid_spec,
        out_shape=jax.ShapeDtypeStruct((n, d), F32),
        compiler_params=_cparams(("arbitrary", "arbitrary")),
        name="combine",
    )(off_flat, x1, g2, fg, pos, aff, y)


def kernel(x, c, ctx, c_ctx, ada_w, ada_b, norm1_g, norm2_g, w_in, w_decay_up, b_decay, gla_norm_g,
           w_gla_proj, pool_w, pool_scale, w_pool_proj, w_out, w_router, w_gate_e, w_up_e, w_down_e,
           final_norm_g):
    assert ada_w.shape[0] == 1, "single-layer block"
    bsz, l, d = x.shape
    lc = ctx.shape[1]
    rank, dk = w_decay_up.shape[2], w_decay_up.shape[3]
    dvh = gla_norm_g.shape[1]
    dv = w_gla_proj.shape[1]
    heads = dv // dvh
    groups, ch = pool_w.shape[1], pool_w.shape[2]
    pw = groups * ch
    ne = w_router.shape[2]
    cap = EC_CAPACITY * l // ne
    assert dk // heads == HEAD_DK and dvh == LANES and ch == LANES and heads % 2 == 0
    assert l % ROUTE_BLK == 0 and l // ROUTE_BLK < LANES and cap >= SLOT_WIN and cap % BF16_ROWS == 0
    assert GRID_W == GLA_CHUNK and MXU_DIM % GRID_W == 0 and (l // MXU_DIM) % 2 == 0

    cin = jnp.zeros((8, d), F32).at[:bsz].set(c).at[bsz].set(c_ctx)
    mods = _ada(cin, ada_w[0], ada_b[0][None, :])
    sh1, sc1, gt1, sh2, sc2, gt2 = [mods[:, i * d:(i + 1) * d] for i in range(6)]
    vec3 = lambda a: a[:, None, :]
    mult1 = norm1_g[0][None, :] * (1.0 + sc1)
    mult2 = norm2_g[0][None, :] * (1.0 + sc2)

    o_r = dk + dv
    o_q = o_r + 2 * rank
    o_g = o_q + dk
    o_p = o_g + dv
    o_m = o_p + pw
    up = jnp.zeros((2 * rank, 2 * dk), F32)
    up = up.at[:rank, :dk].set(w_decay_up[0, 0]).at[rank:, dk:].set(w_decay_up[0, 1])
    psc = pool_scale[0].reshape(groups, 1, ch)
    w1, wm, wpool = _prep(w_in[0].T, up, pool_w[0], psc, w_pool_proj[0], o_r, o_q, o_g, o_m,
                          float(dk // heads) ** -0.5)
    bz = b_decay[0]

    zero_state = jnp.zeros((bsz, heads // 2, LANES, 2 * LANES), F32)
    cm = jnp.broadcast_to(vec3(mult1[bsz:bsz + 1]), (bsz, 1, d))
    cs = jnp.broadcast_to(vec3(sh1[bsz:bsz + 1]), (bsz, 1, d))
    ck, cv, _, claf, clab, _, _ = _inproj(ctx.reshape(bsz * lc, d), cm, cs, w1, bz, lc, dk, dv, pw, min(lc, CTX_TILE))
    r3 = lambda a, n: a.reshape(bsz, n, a.shape[-1])
    _, _, h_f, h_b = _gla(r3(ck, lc), r3(ck, lc), r3(cv, lc), r3(claf, lc), r3(clab, lc),
                          zero_state, zero_state, min(lc, CTX_TILE))

    x2 = x.reshape(bsz * l, d)
    tm = min(l, TOKEN_TILE)
    k, v, q, laf, lab, sg, pin = _inproj(x2, vec3(mult1[:bsz]), vec3(sh1[:bsz]), w1, bz, l, dk, dv, pw, tm)
    of, ob, _, _ = _gla(r3(k, l), r3(q, l), r3(v, l), r3(laf, l), r3(lab, l), h_f, h_b, min(l, GLA_TILE))
    pooled = _pool(r3(pin, l), ch).reshape(bsz * l, pw)

    wr_hi = w_router[0].astype(BF16)
    wr_lo = (w_router[0] - wr_hi.astype(F32)).astype(BF16)
    wr = jnp.zeros((d, LANES), BF16).at[:, :ne].set(wr_hi).at[:, ne:2 * ne].set(wr_lo)
    vecs = [vec3(mult1[:bsz]), vec3(sh1[:bsz]), vec3(gt1[:bsz]), vec3(mult2[:bsz]), vec3(sh2[:bsz])]
    x1, h2, aff = _merge(x2, vecs, of.reshape(bsz * l, dv), ob.reshape(bsz * l, dv), sg, pooled,
                          wm, w_gla_proj[0].astype(BF16), wpool, w_out[0].astype(BF16),
                          gla_norm_g[0][None, :], wr, l, heads, ne, tm)

    pos, offs = _route(aff.reshape(1, bsz * ne, l), cap)
    pos = pos.reshape(bsz, ne, l)
    off_flat = offs.reshape(-1)
    xs = _gatherx(off_flat, h2, pos, cap)
    y = _moe(xs, w_gate_e[0], w_up_e[0], w_down_e[0])
    out = _combine(off_flat, x1, vec3(gt2[:bsz]), final_norm_g[None, :], pos, aff, y, l)
    return out.reshape(bsz, l, d)
```

```python
import functools

import jax
import jax.numpy as jnp
from jax import lax
from jax.experimental import pallas as pl
from jax.experimental.pallas import tpu as pltpu

F32 = jnp.float32
BF16 = jnp.bfloat16
I32 = jnp.int32
HIGHEST = lax.Precision.HIGHEST

EPS = 1e-6
GRID_W = 64
GLA_CHUNK = 64
GLA_STAGES = 6
GLA_BATCH = 8
GATE_NORMALIZER = 16.0
POOL_WINDOWS = (2, 4, 8, 16)
EC_CAPACITY = 2

LANES = 128
SUBLANES = 8
BF16_ROWS = 16
MXU_DIM = 256
VMEM_BYTES = 64 * 1024 * 1024
VMEM_LIMIT = VMEM_BYTES * 7 // 8

HEAD_DK = LANES // 2
ROUTE_BLK = MXU_DIM
SLOT_WIN = 64
SUB_ROWS = MXU_DIM
TOKEN_TILE = 1024
GLA_TILE = 2048
CTX_TILE = 256

_NT = (((1,), (1,)), ((), ()))
_TN = (((0,), (0,)), ((), ()))


def _cparams(sem):
    return pltpu.CompilerParams(dimension_semantics=sem, vmem_limit_bytes=VMEM_LIMIT)


def _silu(x):
    return x * jax.nn.sigmoid(x)


def _row_groups(n):
    step = min(n, SUB_ROWS)
    return [pl.ds(i, step) for i in range(0, n, step)]


def _staggered(gens, nstages):
    for t in range(nstages + len(gens) - 1):
        for g in reversed(range(len(gens))):
            if 0 <= t - g < nstages:
                next(gens[g])


def _ada_body(c_ref, w_ref, b_ref, o_ref):
    s = _silu(c_ref[...])
    rows = s.shape[0]
    s_hi = s.astype(BF16)
    s_lo = (s - s_hi.astype(F32)).astype(BF16)
    w = w_ref[...]
    w_hi = w.astype(BF16)
    w_lo = (w - w_hi.astype(F32)).astype(BF16)
    both = jnp.dot(jnp.concatenate([s_hi, s_lo], axis=0), w_hi, preferred_element_type=F32)
    o_ref[...] = (both[:rows] + both[rows:] + jnp.dot(s_hi, w_lo, preferred_element_type=F32)) + b_ref[...]


def _ada(cin, w, b):
    rows, d = cin.shape
    n = w.shape[1]
    nb = n // 4
    return pl.pallas_call(
        _ada_body,
        grid=(n // nb,),
        in_specs=[pl.BlockSpec((rows, d), lambda j: (0, 0)),
                  pl.BlockSpec((d, nb), lambda j: (0, j)),
                  pl.BlockSpec((1, nb), lambda j: (0, j))],
        out_specs=pl.BlockSpec((rows, nb), lambda j: (0, j)),
        out_shape=jax.ShapeDtypeStruct((rows, n), F32),
        compiler_params=_cparams(("arbitrary",)),
        name="ada",
    )(cin, w, b)


def _prep_body(w_ref, up_ref, pw_ref, ps_ref, wpp_ref, w1_ref, wm_ref, wpf_ref, *, o_r, o_q, o_g, o_m, qscale):
    dk = o_g - o_q
    wz = lax.dot_general(w_ref[o_r:o_q, :], up_ref[...], _TN, precision=HIGHEST, preferred_element_type=F32)
    nz = wz.shape[1]
    w1_ref[:, :o_r] = w_ref[:o_r, :].T.astype(BF16)
    w1_ref[:, o_r:o_r + dk] = (w_ref[o_q:o_g, :].T * qscale).astype(BF16)
    w1_ref[:, o_r + dk:o_r + dk + nz] = wz.astype(BF16)
    w1_ref[:, o_r + dk + nz:] = w_ref[o_g:o_m, :].T.astype(BF16)
    wm_ref[...] = w_ref[o_m:, :].T.astype(BF16)
    wpf_ref[...] = jnp.dot(pw_ref[0] * ps_ref[0], wpp_ref[...], precision=HIGHEST,
                           preferred_element_type=F32).astype(BF16)


def _prep(w_in_t, up, pool_w, pool_scale3, w_pool_proj, o_r, o_q, o_g, o_m, qscale):
    n, d = w_in_t.shape
    n1 = o_r + (o_g - o_q) + up.shape[1] + (o_m - o_g)
    groups, ch, _ = pool_w.shape
    tr = d // groups
    dm = w_pool_proj.shape[1]
    return pl.pallas_call(
        functools.partial(_prep_body, o_r=o_r, o_q=o_q, o_g=o_g, o_m=o_m, qscale=qscale),
        grid=(d // tr,),
        in_specs=[pl.BlockSpec((n, tr), lambda i: (0, i)),
                  pl.BlockSpec(up.shape, lambda i: (0, 0)),
                  pl.BlockSpec((1, ch, ch), lambda i: (i, 0, 0)),
                  pl.BlockSpec((1, 1, ch), lambda i: (i, 0, 0)),
                  pl.BlockSpec((ch, dm), lambda i: (i, 0))],
        out_specs=[pl.BlockSpec((tr, n1), lambda i: (i, 0)),
                   pl.BlockSpec((tr, n - o_m), lambda i: (i, 0)),
                   pl.BlockSpec((ch, dm), lambda i: (i, 0))],
        out_shape=[jax.ShapeDtypeStruct((d, n1), BF16), jax.ShapeDtypeStruct((d, n - o_m), BF16),
                   jax.ShapeDtypeStruct((groups * ch, dm), BF16)],
        compiler_params=_cparams(("parallel",)),
        name="prep",
    )(w_in_t, up, pool_w, pool_scale3, w_pool_proj)


def _modulated_norm(x, mult, shift):
    ms = jnp.mean(x * x, axis=-1, keepdims=True)
    return (x * lax.rsqrt(ms + EPS)) * mult + shift


def _log_sigmoid(z):
    return jnp.minimum(z, 0.0) - jnp.log1p(jnp.exp(-jnp.abs(z)))


def _inproj_body(x_ref, mult_ref, shift_ref, w_ref, bz_ref,
                 k_ref, v_ref, q_ref, laf_ref, lab_ref, sg_ref, p_ref, *, dk, dv, pw):
    def sub_tile(rows):
        h = _modulated_norm(x_ref[rows, :], mult_ref[0], shift_ref[0]).astype(BF16)
        yield
        u = jnp.dot(h, w_ref[...], preferred_element_type=F32)
        yield
        o = 0
        k_ref[rows, :] = u[:, o:o + dk].astype(BF16); o += dk
        v_ref[rows, :] = u[:, o:o + dv].astype(BF16); o += dv
        q_ref[rows, :] = u[:, o:o + dk].astype(BF16); o += dk
        zf = u[:, o:o + dk] + bz_ref[0:1, :]; o += dk
        zb = u[:, o:o + dk] + bz_ref[1:2, :]; o += dk
        laf_ref[rows, :] = _log_sigmoid(zf) * (1.0 / GATE_NORMALIZER)
        lab_ref[rows, :] = _log_sigmoid(zb) * (1.0 / GATE_NORMALIZER)
        sg_ref[rows, :] = _silu(u[:, o:o + dv]).astype(BF16); o += dv
        p_ref[rows, :] = u[:, o:o + pw].astype(BF16)
        yield

    _staggered([sub_tile(r) for r in _row_groups(x_ref.shape[0])], 3)


def _inproj(x2, mult, shift, w, bz, rows_per_sample, dk, dv, pw, tm):
    n, d = x2.shape
    tps = rows_per_sample // tm
    row = lambda i: (i, 0)
    vec = lambda i: (i // tps, 0, 0)
    outs = [(dk, BF16), (dv, BF16), (dk, BF16), (dk, F32), (dk, F32), (dv, BF16), (pw, BF16)]
    return pl.pallas_call(
        functools.partial(_inproj_body, dk=dk, dv=dv, pw=pw),
        grid=(n // tm,),
        in_specs=[pl.BlockSpec((tm, d), row),
                  pl.BlockSpec((1, 1, d), vec),
                  pl.BlockSpec((1, 1, d), vec),
                  pl.BlockSpec(w.shape, lambda i: (0, 0), pipeline_mode=pl.Buffered(1)),
                  pl.BlockSpec(bz.shape, lambda i: (0, 0))],
        out_specs=[pl.BlockSpec((tm, c), row) for c, _ in outs],
        out_shape=[jax.ShapeDtypeStruct((n, c), t) for c, t in outs],
        compiler_params=_cparams(("parallel",)),
        name="inproj",
    )(x2, mult, shift, w, bz)


def _gla_direction(k_ref, q_ref, v_ref, la_ref, o_ref, s_ref, reverse, nchunk):
    C = GLA_CHUNK
    hd = HEAD_DK
    cb = MXU_DIM
    lt = nchunk * C
    la = la_ref[0]
    r = lax.broadcasted_iota(I32, (cb, cb), 0)
    c = lax.broadcasted_iota(I32, (cb, cb), 1)
    same = (r // C) == (c // C)
    cum = jnp.where(same & ((c >= r) if reverse else (c <= r)), 1.0, 0.0).astype(BF16)
    la_hi = la.astype(BF16)
    la_lo = (la - la_hi.astype(F32)).astype(BF16)
    la2 = jnp.concatenate([la_hi, la_lo], axis=1)
    bcs = []
    for blk in range(lt // cb):
        part = jnp.dot(cum, la2[blk * cb:(blk + 1) * cb], preferred_element_type=F32)
        bcs.append(part[:, :LANES] + part[:, LANES:])
    ri = lax.broadcasted_iota(I32, (C, 2 * C), 0)
    ci = lax.broadcasted_iota(I32, (C, 2 * C), 1) % C
    tri = (ci >= ri) if reverse else (ci <= ri)
    lane = lax.broadcasted_iota(I32, (1, LANES), 1)
    m0 = (lane < hd).astype(F32)
    m1 = (lane >= hd).astype(F32)
    sr = lax.broadcasted_iota(I32, (LANES, 2 * LANES), 0)
    sl = lax.broadcasted_iota(I32, (LANES, 2 * LANES), 1)
    smask = ((sr < hd) == (sl < LANES)).astype(F32)
    kt = k_ref[0].astype(F32)
    qt = q_ref[0].astype(F32)
    zero_v = jnp.zeros((C, LANES), BF16)
    sweep = list(range(nchunk - 1, -1, -1) if reverse else range(nchunk))
    for first in range(0, nchunk, GLA_BATCH):
        order = sweep[first:first + GLA_BATCH]
        intra, qds, kvs, decs, q2s, kss, kws, scs = {}, {}, {}, {}, {}, {}, {}, {}
        for ch in order:
            lo = ch * C
            b = bcs[lo // cb][lo % cb:lo % cb + C]
            last = b[0:1] if reverse else b[C - 1:C]
            mid = b[C // 2:C // 2 + 1] if reverse else b[C // 2 - 1:C // 2]
            kc = kt[lo:lo + C]
            qc = qt[lo:lo + C]
            q2s[ch] = (qc * jnp.exp(b - mid)).astype(BF16)
            ks = kc * jnp.exp(mid - b)
            kss[ch] = jnp.concatenate([ks * m0, ks * m1], axis=0).astype(BF16)
            qds[ch] = (qc * jnp.exp(b)).astype(BF16)
            kws[ch] = (kc * jnp.exp(last - b)).astype(BF16)
            decs[ch] = last
        yield
        for ch in order:
            scs[ch] = lax.dot_general(q2s[ch], kss[ch], _NT, preferred_element_type=F32)
        yield
        for ch in order:
            v2 = v_ref[0, ch * C:(ch + 1) * C, :]
            kvs[ch] = lax.dot_general(kws[ch], v2, _TN, preferred_element_type=F32) * smask
        yield
        for ch in order:
            sc2 = jnp.where(tri, scs[ch], 0.0).astype(BF16)
            v2 = v_ref[0, ch * C:(ch + 1) * C, :]
            vbd = jnp.concatenate([jnp.concatenate([v2[:, :LANES], zero_v], axis=1),
                                   jnp.concatenate([zero_v, v2[:, LANES:]], axis=1)], axis=0)
            intra[ch] = jnp.dot(sc2, vbd, preferred_element_type=F32)
        yield
        pad = jnp.zeros((LANES - len(order), LANES), F32)
        dec_cols = jnp.exp(jnp.concatenate([decs[ch] for ch in order] + [pad], axis=0).T)
        st = s_ref[...]
        starts = {}
        for i, ch in enumerate(order):
            starts[ch] = st.astype(BF16)
            st = st * dec_cols[:, i:i + 1] + kvs[ch]
        s_ref[...] = st
        yield
        for ch in order:
            inter = jnp.dot(qds[ch], starts[ch], preferred_element_type=F32)
            o_ref[0, ch * C:(ch + 1) * C, :] = inter + intra[ch]
        yield


def _gla_body(kf, qf, vf, laf, kb, qb, vb, lab, h0f, h0b, of, ob, hf_out, hb_out, sf, sb, *, nchunk):
    i = pl.program_id(2)

    @pl.when(i == 0)
    def _():
        sf[...] = h0f[0, 0]
        sb[...] = h0b[0, 0]

    sweeps = [_gla_direction(kf, qf, vf, laf, of, sf, False, nchunk),
              _gla_direction(kb, qb, vb, lab, ob, sb, True, nchunk)]
    for _ in range(GLA_STAGES * pl.cdiv(nchunk, GLA_BATCH)):
        for sweep in sweeps:
            next(sweep)

    @pl.when(i == pl.num_programs(2) - 1)
    def _():
        hf_out[0, 0] = sf[...]
        hb_out[0, 0] = sb[...]


def _gla(k, q, v, laf, lab, h0f, h0b, lt):
    bsz, l, _ = k.shape
    pairs = h0f.shape[1]
    nt = l // lt
    fwd = lambda b, hp, i: (b, i, hp)
    bwd = lambda b, hp, i: (b, nt - 1 - i, hp)
    st = lambda b, hp, i: (b, hp, 0, 0)
    kq = lambda m: pl.BlockSpec((1, lt, LANES), m)
    vv = lambda m: pl.BlockSpec((1, lt, 2 * LANES), m)
    sspec = pl.BlockSpec((1, 1, LANES, 2 * LANES), st)
    return pl.pallas_call(
        functools.partial(_gla_body, nchunk=lt // GLA_CHUNK),
        grid=(bsz, pairs, nt),
        in_specs=[kq(fwd), kq(fwd), vv(fwd), kq(fwd), kq(bwd), kq(bwd), vv(bwd), kq(bwd), sspec, sspec],
        out_specs=[vv(fwd), vv(bwd), sspec, sspec],
        out_shape=[jax.ShapeDtypeStruct(v.shape, F32), jax.ShapeDtypeStruct(v.shape, F32),
                   jax.ShapeDtypeStruct(h0f.shape, F32), jax.ShapeDtypeStruct(h0b.shape, F32)],
        scratch_shapes=[pltpu.VMEM((LANES, 2 * LANES), F32), pltpu.VMEM((LANES, 2 * LANES), F32)],
        compiler_params=_cparams(("parallel", "parallel", "arbitrary")),
        name="gla",
    )(k, q, v, laf, k, q, v, lab, h0f, h0b)


def _pool_body(p_ref, o_ref, s1_ref, *, half, rows):
    xb = p_ref[0]
    l, ch = xb.shape

    def inv_counts(pos, n):
        return 1.0 / (jnp.minimum(pos + half, n) - jnp.maximum(pos - half, 0)).astype(F32)

    blk = MXU_DIM
    r = lax.broadcasted_iota(I32, (blk, blk), 0)
    c = lax.broadcasted_iota(I32, (blk, blk), 1)
    band = ((r // GRID_W == c // GRID_W) & (c - r >= -half) & (c - r <= half - 1)).astype(BF16)
    for i in range(0, l // blk, 2):
        pair = jnp.concatenate([xb[i * blk:(i + 1) * blk], xb[(i + 1) * blk:(i + 2) * blk]], axis=1)
        sums = jnp.dot(band, pair, preferred_element_type=F32)
        s1_ref[i * blk:(i + 1) * blk, :] = sums[:, :ch]
        s1_ref[(i + 1) * blk:(i + 2) * blk, :] = sums[:, ch:]
    s1 = s1_ref[...].reshape(rows, GRID_W, ch)

    def shifted(a, s):
        z = jnp.zeros((abs(s),) + a.shape[1:], a.dtype)
        return jnp.concatenate([a[s:], z], axis=0) if s > 0 else jnp.concatenate([z, a[:s]], axis=0)

    fwd = s1
    bwd = shifted(s1, -1)
    s = 1
    while s < half:
        fwd = fwd + shifted(fwd, s)
        bwd = bwd + shifted(bwd, -s)
        s *= 2
    inv_r = inv_counts(lax.broadcasted_iota(I32, (rows, 1, ch), 0), rows)
    inv_c = inv_counts(lax.broadcasted_iota(I32, (1, GRID_W, ch), 1), GRID_W)
    pooled = (fwd + bwd) * inv_r * inv_c - xb.astype(F32).reshape(rows, GRID_W, ch)
    o_ref[0] = pooled.reshape(l, ch).astype(BF16)


def _pool_groups_body(p_ref, o_ref, s1_ref, *, rows):
    for gi, window in enumerate(POOL_WINDOWS):
        @pl.when(pl.program_id(1) == gi)
        def _(half=window // 2):
            _pool_body(p_ref, o_ref, s1_ref, half=half, rows=rows)


def _pool(pin, ch):
    bsz, l, pw = pin.shape
    return pl.pallas_call(
        functools.partial(_pool_groups_body, rows=l // GRID_W),
        grid=(bsz, pw // ch),
        in_specs=[pl.BlockSpec((1, l, ch), lambda b, g: (b, 0, g))],
        out_specs=pl.BlockSpec((1, l, ch), lambda b, g: (b, 0, g)),
        out_shape=jax.ShapeDtypeStruct((bsz, l, pw), BF16),
        scratch_shapes=[pltpu.VMEM((l, ch), F32)],
        compiler_params=_cparams(("parallel", "parallel")),
        name="pool",
    )(pin)


def _merge_body(x_ref, m1_ref, s1_ref, g1_ref, m2_ref, s2_ref,
                of_ref, ob_ref, sg_ref, mx_ref,
                wm_ref, wgla_ref, wpool_ref, wout_ref, gn_ref, wr_ref,
                x1_ref, h2_ref, aff_ref, *, heads, ne):
    d = x_ref.shape[1]

    def sub_tile(rows):
        bp = jnp.dot(mx_ref[rows, :], wpool_ref[...], preferred_element_type=F32)
        x = x_ref[rows, :]
        h = _modulated_norm(x, m1_ref[0], s1_ref[0]).astype(BF16)
        o = of_ref[rows, :] + ob_ref[rows, :]
        sg = sg_ref[rows, :].astype(F32)
        og = []
        for j in range(heads):
            oj = o[:, j * LANES:(j + 1) * LANES]
            oj = oj * lax.rsqrt(jnp.mean(oj * oj, axis=-1, keepdims=True) + EPS) * gn_ref[...]
            og.append((oj * sg[:, j * LANES:(j + 1) * LANES]).astype(BF16))
        og = jnp.concatenate(og, axis=1)
        yield
        gates = jnp.dot(h, wm_ref[...], preferred_element_type=F32)
        bg = jnp.dot(og, wgla_ref[...], preferred_element_type=F32)
        yield
        gates = jax.nn.sigmoid(gates)
        z = (gates[:, :d] * bg + gates[:, d:] * bp).astype(BF16)
        yield
        y = jnp.dot(z, wout_ref[...], preferred_element_type=F32)
        yield
        x1 = x + g1_ref[0] * y
        x1_ref[rows, :] = x1
        h2 = _modulated_norm(x1, m2_ref[0], s2_ref[0])
        hi = h2.astype(BF16)
        h2_ref[rows, :] = hi
        lo = (h2 - hi.astype(F32)).astype(BF16)
        yield
        lg = (jnp.dot(hi, wr_ref[...], preferred_element_type=F32)
              + jnp.dot(lo, wr_ref[...], preferred_element_type=F32))
        yield
        lgt = lg.T
        logit = lgt[0:ne] + lgt[ne:2 * ne]
        mx = jnp.max(logit, axis=0, keepdims=True)
        ex = jnp.exp(logit - mx)
        aff_ref[0, :, rows] = ex / jnp.sum(ex, axis=0, keepdims=True)
        yield

    _staggered([sub_tile(r) for r in _row_groups(x_ref.shape[0])], 7)


def _merge(x2, vecs, of, ob, sg, mixed, wm, wgla, wpool, wout, gn, wr, rows_per_sample, heads, ne, tm):
    n, d = x2.shape
    bsz = n // rows_per_sample
    tps = rows_per_sample // tm
    row = lambda i: (i, 0)
    vec = lambda i: (i // tps, 0, 0)
    full = lambda a: pl.BlockSpec(a.shape, lambda i: (0,) * a.ndim, pipeline_mode=pl.Buffered(1))
    gv = of.shape[-1]
    return pl.pallas_call(
        functools.partial(_merge_body, heads=heads, ne=ne),
        grid=(n // tm,),
        in_specs=[pl.BlockSpec((tm, d), row)] + [pl.BlockSpec((1, 1, d), vec)] * 5
                 + [pl.BlockSpec((tm, gv), row)] * 4
                 + [full(wm), full(wgla), full(wpool), full(wout), full(gn), full(wr)],
        out_specs=[pl.BlockSpec((tm, d), row),
                   pl.BlockSpec((tm, d), row),
                   pl.BlockSpec((1, ne, tm), lambda i: (i // tps, 0, i % tps))],
        out_shape=[jax.ShapeDtypeStruct((n, d), F32),
                   jax.ShapeDtypeStruct((n, d), BF16),
                   jax.ShapeDtypeStruct((bsz, ne, rows_per_sample), F32)],
        compiler_params=_cparams(("parallel",)),
        name="merge",
    )(x2, *vecs, of, ob, sg, mixed, wm, wgla, wpool, wout, gn, wr)


def _route_body(aff_ref, pos_ref, off_ref, cnt_ref, *, cap, ntb):
    a = aff_ref[0]
    ne, l = a.shape
    blk = ROUTE_BLK

    def bisect(i, v):
        cand = v | jnp.left_shift(jnp.int32(1), 30 - i)
        cnt = jnp.sum((a >= lax.bitcast_convert_type(cand, F32)).astype(F32), axis=1, keepdims=True)
        return jnp.where(cnt >= cap, cand, v)

    thr = lax.bitcast_convert_type(lax.fori_loop(0, 31, bisect, jnp.zeros((ne, 1), I32)), F32)
    gt = a > thr
    tie = a == thr
    need = cap - jnp.sum(gt.astype(F32), axis=1, keepdims=True)

    r = lax.broadcasted_iota(I32, (blk, blk), 0)
    c = lax.broadcasted_iota(I32, (blk, blk), 1)
    upper = (r <= c).astype(BF16)
    lane = lax.broadcasted_iota(I32, (1, LANES), 1)

    def prefix(mask_f):
        run = jnp.zeros((ne, 1), F32)
        offs = jnp.zeros((ne, LANES), F32)
        for tb in range(ntb):
            m = mask_f[:, tb * blk:(tb + 1) * blk].astype(BF16)
            loc = jnp.dot(m, upper, preferred_element_type=F32)
            cnt_ref[:, tb * blk:(tb + 1) * blk] = loc + run
            offs = jnp.where(lane == tb, run, offs)
            run = run + loc[:, blk - 1:blk]
        return jnp.where(lane >= ntb, run, offs)

    tie_f = tie.astype(F32)
    prefix(tie_f)
    tie_excl = cnt_ref[...] - tie_f
    sel = gt | (tie & (tie_excl < need))
    offs = prefix(sel.astype(F32))
    pos_ref[0] = jnp.where(sel, cnt_ref[...] - 1.0, -1.0).astype(I32)
    off_ref[0] = offs.astype(I32)


def _route(aff, cap):
    bsz, ne, l = aff.shape
    spec = lambda s: pl.BlockSpec((1,) + s, lambda b: (b, 0, 0))
    return pl.pallas_call(
        functools.partial(_route_body, cap=cap, ntb=l // ROUTE_BLK),
        grid=(bsz,),
        in_specs=[spec((ne, l))],
        out_specs=[spec((ne, l)), spec((ne, LANES))],
        out_shape=[jax.ShapeDtypeStruct((bsz, ne, l), I32),
                   jax.ShapeDtypeStruct((bsz, ne, LANES), I32)],
        scratch_shapes=[pltpu.VMEM((ne, l), F32)],
        compiler_params=_cparams(("parallel",)),
        name="route",
    )(aff)


def _window_plan(off_ref, bb, tt, experts, ne):
    lows = [off_ref[(bb * ne + e) * LANES + tt] & -BF16_ROWS for e in experts]
    ends = [off_ref[(bb * ne + e) * LANES + tt + 1] for e in experts]
    return lows, ends


def _window_rounds(lows, ends):
    rounds = jnp.int32(0)
    for lo, hi in zip(lows, ends):
        rounds = jnp.maximum(rounds, lax.div(hi - lo + (SLOT_WIN - 1), jnp.int32(SLOT_WIN)))
    return rounds


def _gatherx_body(off_ref, h_ref, pos_ref, xs_ref, *, ne, cap, eg):
    b = pl.program_id(0)
    g = pl.program_id(1)
    tb = pl.program_id(2)
    win = SLOT_WIN

    @pl.when(tb == 0)
    def _():
        xs_ref[...] = jnp.zeros(xs_ref.shape, BF16)

    j_col = lax.broadcasted_iota(I32, (win, 1), 0)
    t = ROUTE_BLK
    nsub = h_ref.shape[0] // t
    experts = [g * eg + k for k in range(eg)]

    def select(sub, lows, r):
        starts = [pl.multiple_of(jnp.minimum(lows[k] + r * win, cap - win), BF16_ROWS) for k in range(eg)]
        pieces = []
        for k in range(eg):
            p = pos_ref[0, k:k + 1, sub * t:(sub + 1) * t]
            hit = (p - starts[k] == j_col) & (p >= lows[k] + r * win)
            pieces.append(jnp.where(hit, 1.0, 0.0).astype(BF16))
        sel = jnp.concatenate(pieces, axis=0)
        rows = jnp.dot(sel, h_ref[sub * t:(sub + 1) * t, :], preferred_element_type=F32).astype(BF16)
        return starts, rows

    def deposit(starts, rows):
        for k in range(eg):
            dst = (0, k, pl.ds(starts[k], win), slice(None))
            xs_ref[dst] = xs_ref[dst] + rows[k * win:(k + 1) * win]

    plans = [_window_plan(off_ref, b, tb * nsub + sub, experts, ne) for sub in range(nsub)]
    firsts = [select(sub, plans[sub][0], 0) for sub in range(nsub)]
    for starts, rows in firsts:
        deposit(starts, rows)
    for sub in range(nsub):
        lows, ends = plans[sub]

        def extra_round(r, carry, lows=lows, sub=sub):
            deposit(*select(sub, lows, r))
            return carry

        lax.fori_loop(1, _window_rounds(lows, ends), extra_round, 0)


def _gatherx(off_flat, h2, pos, cap):
    n, d = h2.shape
    bsz, ne, l = pos.shape
    t = min(l, 2 * TOKEN_TILE)
    ntb = l // t
    eg = SUBLANES
    grid_spec = pltpu.PrefetchScalarGridSpec(
        num_scalar_prefetch=1,
        grid=(bsz, ne // eg, ntb),
        in_specs=[pl.BlockSpec((t, d), lambda b, g, i, off: (b * ntb + i, 0)),
                  pl.BlockSpec((1, eg, t), lambda b, g, i, off: (b, g, i))],
        out_specs=pl.BlockSpec((1, eg, cap, d), lambda b, g, i, off: (b, g, 0, 0)),
    )
    return pl.pallas_call(
        functools.partial(_gatherx_body, ne=ne, cap=cap, eg=eg),
        grid_spec=grid_spec,
        out_shape=jax.ShapeDtypeStruct((bsz, ne, cap, d), BF16),
        compiler_params=_cparams(("parallel", "parallel", "arbitrary")),
        name="gatherx",
    )(off_flat, h2, pos)


def _moe_body(xs_ref, wg_ref, wu_ref, wd_ref, y_ref, *, rc):
    wg = wg_ref[0].astype(BF16)
    wu = wu_ref[0].astype(BF16)
    wd = wd_ref[0].astype(BF16)
    cap = xs_ref.shape[2]
    acts = []
    for ch in range(cap // rc):
        xs = xs_ref[0, 0, ch * rc:(ch + 1) * rc, :]
        gate = jnp.dot(xs, wg, preferred_element_type=F32)
        up = jnp.dot(xs, wu, preferred_element_type=F32)
        acts.append((_silu(gate) * up).astype(BF16))
    for ch in range(cap // rc):
        y_ref[0, 0, ch * rc:(ch + 1) * rc, :] = jnp.dot(acts[ch], wd, preferred_element_type=F32).astype(BF16)


def _moe(xs, wg, wu, wd):
    bsz, ne, cap, d = xs.shape
    de = wg.shape[2]
    slot = pl.BlockSpec((1, 1, cap, d), lambda e, b: (b, e, 0, 0))
    return pl.pallas_call(
        functools.partial(_moe_body, rc=min(cap, 2 * MXU_DIM)),
        grid=(ne, bsz),
        in_specs=[slot,
                  pl.BlockSpec((1, d, de), lambda e, b: (e, 0, 0)),
                  pl.BlockSpec((1, d, de), lambda e, b: (e, 0, 0)),
                  pl.BlockSpec((1, de, d), lambda e, b: (e, 0, 0))],
        out_specs=slot,
        out_shape=jax.ShapeDtypeStruct((bsz, ne, cap, d), BF16),
        compiler_params=_cparams(("parallel", "arbitrary")),
        name="moe",
    )(xs, wg, wu, wd)


def _combine_body(off_ref, x1_ref, g2_ref, fg_ref, pos_ref, aff_ref, y_hbm, o_ref, ybuf, spare, acc_ref, sem,
                  spare_sem, *, ne, cap, nsp, nsub, span):
    b = pl.program_id(0)
    i = pl.program_id(1)
    step = b * nsp + i
    nsteps = pl.num_programs(0) * nsp
    slot = step % 2
    win = SLOT_WIN
    t = ROUTE_BLK
    experts = range(ne)

    def window_starts(lows, r):
        return [pl.multiple_of(jnp.minimum(lows[e] + r * win, cap - win), BF16_ROWS) for e in experts]

    def span_starts(bb, ii):
        lows = _window_plan(off_ref, bb, ii * nsub, experts, ne)[0]
        return [pl.multiple_of(jnp.minimum(lows[e], cap - span), BF16_ROWS) for e in experts]

    def span_copy(bb, e, start, buf):
        return pltpu.make_async_copy(y_hbm.at[bb, e, pl.ds(start, span), :], ybuf.at[buf, e], sem.at[buf, e])

    def spare_copy(e, start):
        return pltpu.make_async_copy(y_hbm.at[b, e, pl.ds(start, win), :],
                                     spare.at[pl.ds(e * win, win), :], spare_sem.at[e])

    def start_step(bb, ii, buf):
        starts = span_starts(bb, ii)
        for e in experts:
            span_copy(bb, e, starts[e], buf).start()

    @pl.when(step == 0)
    def _():
        start_step(b, i, 0)

    nxt = jnp.minimum(step + 1, nsteps - 1)
    start_step(lax.div(nxt, jnp.int32(nsp)), lax.rem(nxt, jnp.int32(nsp)), 1 - slot)

    plans = [_window_plan(off_ref, b, i * nsub + sub, experts, ne) for sub in range(nsub)]
    mine = span_starts(b, i)
    firsts = [window_starts(plans[sub][0], 0) for sub in range(nsub)]
    inside = []
    for sub in range(nsub):
        ok = jnp.bool_(True)
        for e in experts:
            ok = ok & (firsts[sub][e] >= mine[e]) & (firsts[sub][e] + win <= mine[e] + span)
        inside.append(ok)
    j_col = lax.broadcasted_iota(I32, (win, 1), 0)

    def expand(sub, r, starts, rows, enabled=None):
        lows = plans[sub][0]
        pieces = []
        for e in experts:
            p = pos_ref[0, e:e + 1, sub * t:(sub + 1) * t]
            valid = p >= lows[e] + r * win
            hit = (p - starts[e] == j_col) & valid
            if enabled is not None:
                hit = hit & enabled
            pieces.append(jnp.where(hit, aff_ref[0, e:e + 1, sub * t:(sub + 1) * t], 0.0).astype(BF16))
        pmat = jnp.concatenate(pieces, axis=0)
        return lax.dot_general(pmat, rows, _TN, preferred_element_type=F32)

    acc_ref[...] = jnp.zeros(acc_ref.shape, F32)
    for sub in range(nsub):
        lows, ends = plans[sub]

        def own_round(r, carry, lows=lows, sub=sub):
            starts = window_starts(lows, r)
            for e in experts:
                spare_copy(e, starts[e]).start()
            for e in experts:
                spare_copy(e, starts[e]).wait()
            acc_ref[sub] += expand(sub, r, starts, spare[...])
            return carry

        lax.fori_loop(jnp.where(inside[sub], 1, 0), _window_rounds(lows, ends), own_round, 0)

    for e in experts:
        span_copy(b, e, mine[e], slot).wait()
    for sub in range(nsub):
        rows = pl.ds(sub * t, t)
        picked = []
        for e in experts:
            local = pl.multiple_of(jnp.clip(firsts[sub][e] - mine[e], 0, span - win), BF16_ROWS)
            picked.append(ybuf[slot, e, pl.ds(local, win), :])
        moe = expand(sub, 0, firsts[sub], jnp.concatenate(picked, axis=0), inside[sub]) + acc_ref[sub]
        x2 = x1_ref[rows, :] + g2_ref[0] * moe
        ms = jnp.mean(x2 * x2, axis=-1, keepdims=True)
        o_ref[rows, :] = x2 * lax.rsqrt(ms + EPS) * fg_ref[...]

    @pl.when(step == nsteps - 1)
    def _():
        for e in experts:
            span_copy(b, e, mine[e], 1 - slot).wait()


def _combine(off_flat, x1, g2, fg, pos, aff, y, rows_per_sample):
    n, d = x1.shape
    bsz, ne, cap, _ = y.shape
    t = min(rows_per_sample, TOKEN_TILE)
    nsub = t // ROUTE_BLK
    nsp = rows_per_sample // t
    rows = ne * SLOT_WIN
    span = min(cap, nsub * SLOT_WIN * 3 // 4)
    grid_spec = pltpu.PrefetchScalarGridSpec(
        num_scalar_prefetch=1,
        grid=(bsz, nsp),
        in_specs=[pl.BlockSpec((t, d), lambda b, i, off: (b * nsp + i, 0)),
                  pl.BlockSpec((1, 1, d), lambda b, i, off: (b, 0, 0)),
                  pl.BlockSpec((1, d), lambda b, i, off: (0, 0)),
                  pl.BlockSpec((1, ne, t), lambda b, i, off: (b, 0, i)),
                  pl.BlockSpec((1, ne, t), lambda b, i, off: (b, 0, i)),
                  pl.BlockSpec(memory_space=pl.ANY)],
        out_specs=pl.BlockSpec((t, d), lambda b, i, off: (b * nsp + i, 0)),
        scratch_shapes=[pltpu.VMEM((2, ne, span, d), BF16),
                        pltpu.VMEM((rows, d), BF16),
                        pltpu.VMEM((nsub, ROUTE_BLK, d), F32),
                        pltpu.SemaphoreType.DMA((2, ne)),
                        pltpu.SemaphoreType.DMA((ne,))],
    )
    return pl.pallas_call(
        functools.partial(_combine_body, ne=ne, cap=cap, nsp=nsp, nsub=nsub, span=span),
        grid_spec=grid_spec,
        out_shape=jax.ShapeDtypeStruct((n, d), F32),
        compiler_params=_cparams(("arbitrary", "arbitrary")),
        name="combine",
    )(off_flat, x1, g2, fg, pos, aff, y)


def kernel(x, c, ctx, c_ctx, ada_w, ada_b, norm1_g, norm2_g, w_in, w_decay_up, b_decay, gla_norm_g,
           w_gla_proj, pool_w, pool_scale, w_pool_proj, w_out, w_router, w_gate_e, w_up_e, w_down_e,
           final_norm_g):
    assert ada_w.shape[0] == 1, "single-layer block"
    bsz, l, d = x.shape
    lc = ctx.shape[1]
    rank, dk = w_decay_up.shape[2], w_decay_up.shape[3]
    dvh = gla_norm_g.shape[1]
    dv = w_gla_proj.shape[1]
    heads = dv // dvh
    groups, ch = pool_w.shape[1], pool_w.shape[2]
    pw = groups * ch
    ne = w_router.shape[2]
    cap = EC_CAPACITY * l // ne
    assert dk // heads == HEAD_DK and dvh == LANES and ch == LANES and heads % 2 == 0
    assert l % ROUTE_BLK == 0 and l // ROUTE_BLK < LANES and cap >= SLOT_WIN and cap % BF16_ROWS == 0
    assert GRID_W == GLA_CHUNK and MXU_DIM % GRID_W == 0 and (l // MXU_DIM) % 2 == 0

    cin = jnp.zeros((8, d), F32).at[:bsz].set(c).at[bsz].set(c_ctx)
    mods = _ada(cin, ada_w[0], ada_b[0][None, :])
    sh1, sc1, gt1, sh2, sc2, gt2 = [mods[:, i * d:(i + 1) * d] for i in range(6)]
    vec3 = lambda a: a[:, None, :]
    mult1 = norm1_g[0][None, :] * (1.0 + sc1)
    mult2 = norm2_g[0][None, :] * (1.0 + sc2)

    o_r = dk + dv
    o_q = o_r + 2 * rank
    o_g = o_q + dk
    o_p = o_g + dv
    o_m = o_p + pw
    up = jnp.zeros((2 * rank, 2 * dk), F32)
    up = up.at[:rank, :dk].set(w_decay_up[0, 0]).at[rank:, dk:].set(w_decay_up[0, 1])
    psc = pool_scale[0].reshape(groups, 1, ch)
    w1, wm, wpool = _prep(w_in[0].T, up, pool_w[0], psc, w_pool_proj[0], o_r, o_q, o_g, o_m,
                          float(dk // heads) ** -0.5)
    bz = b_decay[0]

    zero_state = jnp.zeros((bsz, heads // 2, LANES, 2 * LANES), F32)
    cm = jnp.broadcast_to(vec3(mult1[bsz:bsz + 1]), (bsz, 1, d))
    cs = jnp.broadcast_to(vec3(sh1[bsz:bsz + 1]), (bsz, 1, d))
    ck, cv, _, claf, clab, _, _ = _inproj(ctx.reshape(bsz * lc, d), cm, cs, w1, bz, lc, dk, dv, pw, min(lc, CTX_TILE))
    r3 = lambda a, n: a.reshape(bsz, n, a.shape[-1])
    _, _, h_f, h_b = _gla(r3(ck, lc), r3(ck, lc), r3(cv, lc), r3(claf, lc), r3(clab, lc),
                          zero_state, zero_state, min(lc, CTX_TILE))

    x2 = x.reshape(bsz * l, d)
    tm = min(l, TOKEN_TILE)
    k, v, q, laf, lab, sg, pin = _inproj(x2, vec3(mult1[:bsz]), vec3(sh1[:bsz]), w1, bz, l, dk, dv, pw, tm)
    of, ob, _, _ = _gla(r3(k, l), r3(q, l), r3(v, l), r3(laf, l), r3(lab, l), h_f, h_b, min(l, GLA_TILE))
    pooled = _pool(r3(pin, l), ch).reshape(bsz * l, pw)

    wr_hi = w_router[0].astype(BF16)
    wr_lo = (w_router[0] - wr_hi.astype(F32)).astype(BF16)
    wr = jnp.zeros((d, LANES), BF16).at[:, :ne].set(wr_hi).at[:, ne:2 * ne].set(wr_lo)
    vecs = [vec3(mult1[:bsz]), vec3(sh1[:bsz]), vec3(gt1[:bsz]), vec3(mult2[:bsz]), vec3(sh2[:bsz])]
    x1, h2, aff = _merge(x2, vecs, of.reshape(bsz * l, dv), ob.reshape(bsz * l, dv), sg, pooled,
                          wm, w_gla_proj[0].astype(BF16), wpool, w_out[0].astype(BF16),
                          gla_norm_g[0][None, :], wr, l, heads, ne, tm)

    pos, offs = _route(aff.reshape(1, bsz * ne, l), cap)
    pos = pos.reshape(bsz, ne, l)
    off_flat = offs.reshape(-1)
    xs = _gatherx(off_flat, h2, pos, cap)
    y = _moe(xs, w_gate_e[0], w_up_e[0], w_down_e[0])
    out = _combine(off_flat, x1, vec3(gt2[:bsz]), final_norm_g[None, :], pos, aff, y, l)
    return out.reshape(bsz, l, d)
```

```python
import functools

import jax
import jax.numpy as jnp
from jax import lax
from jax.experimental import pallas as pl
from jax.experimental.pallas import tpu as pltpu

F32 = jnp.float32
BF16 = jnp.bfloat16
I32 = jnp.int32
HIGHEST = lax.Precision.HIGHEST

EPS = 1e-6
GRID_W = 64
GLA_CHUNK = 64
GLA_STAGES = 6
GLA_BATCH = 8
GATE_NORMALIZER = 16.0
POOL_WINDOWS = (2, 4, 8, 16)
EC_CAPACITY = 2

LANES = 128
SUBLANES = 8
BF16_ROWS = 16
MXU_DIM = 256
VMEM_BYTES = 64 * 1024 * 1024
VMEM_LIMIT = VMEM_BYTES * 7 // 8

HEAD_DK = LANES // 2
ROUTE_BLK = MXU_DIM
SLOT_WIN = 64
SUB_ROWS = MXU_DIM
TOKEN_TILE = 1024
GLA_TILE = 2048
CTX_TILE = 256

_NT = (((1,), (1,)), ((), ()))
_TN = (((0,), (0,)), ((), ()))


def _cparams(sem):
    return pltpu.CompilerParams(dimension_semantics=sem, vmem_limit_bytes=VMEM_LIMIT)


def _silu(x):
    return x * jax.nn.sigmoid(x)


def _row_groups(n):
    step = min(n, SUB_ROWS)
    return [pl.ds(i, step) for i in range(0, n, step)]


def _staggered(gens, nstages):
    for t in range(nstages + len(gens) - 1):
        for g in reversed(range(len(gens))):
            if 0 <= t - g < nstages:
                next(gens[g])


def _ada_body(c_ref, w_ref, b_ref, o_ref):
    s = _silu(c_ref[...])
    rows = s.shape[0]
    s_hi = s.astype(BF16)
    s_lo = (s - s_hi.astype(F32)).astype(BF16)
    w = w_ref[...]
    w_hi = w.astype(BF16)
    w_lo = (w - w_hi.astype(F32)).astype(BF16)
    both = jnp.dot(jnp.concatenate([s_hi, s_lo], axis=0), w_hi, preferred_element_type=F32)
    o_ref[...] = (both[:rows] + both[rows:] + jnp.dot(s_hi, w_lo, preferred_element_type=F32)) + b_ref[...]


def _ada(cin, w, b):
    rows, d = cin.shape
    n = w.shape[1]
    nb = n // 4
    return pl.pallas_call(
        _ada_body,
        grid=(n // nb,),
        in_specs=[pl.BlockSpec((rows, d), lambda j: (0, 0)),
                  pl.BlockSpec((d, nb), lambda j: (0, j)),
                  pl.BlockSpec((1, nb), lambda j: (0, j))],
        out_specs=pl.BlockSpec((rows, nb), lambda j: (0, j)),
        out_shape=jax.ShapeDtypeStruct((rows, n), F32),
        compiler_params=_cparams(("arbitrary",)),
        name="ada",
    )(cin, w, b)


def _prep_body(w_ref, up_ref, pw_ref, ps_ref, wpp_ref, w1_ref, wm_ref, wpf_ref, *, o_r, o_q, o_g, o_m, qscale):
    dk = o_g - o_q
    wz = lax.dot_general(w_ref[o_r:o_q, :], up_ref[...], _TN, precision=HIGHEST, preferred_element_type=F32)
    nz = wz.shape[1]
    w1_ref[:, :o_r] = w_ref[:o_r, :].T.astype(BF16)
    w1_ref[:, o_r:o_r + dk] = (w_ref[o_q:o_g, :].T * qscale).astype(BF16)
    w1_ref[:, o_r + dk:o_r + dk + nz] = wz.astype(BF16)
    w1_ref[:, o_r + dk + nz:] = w_ref[o_g:o_m, :].T.astype(BF16)
    wm_ref[...] = w_ref[o_m:, :].T.astype(BF16)
    wpf_ref[...] = jnp.dot(pw_ref[0] * ps_ref[0], wpp_ref[...], precision=HIGHEST,
                           preferred_element_type=F32).astype(BF16)


def _prep(w_in_t, up, pool_w, pool_scale3, w_pool_proj, o_r, o_q, o_g, o_m, qscale):
    n, d = w_in_t.shape
    n1 = o_r + (o_g - o_q) + up.shape[1] + (o_m - o_g)
    groups, ch, _ = pool_w.shape
    tr = d // groups
    dm = w_pool_proj.shape[1]
    return pl.pallas_call(
        functools.partial(_prep_body, o_r=o_r, o_q=o_q, o_g=o_g, o_m=o_m, qscale=qscale),
        grid=(d // tr,),
        in_specs=[pl.BlockSpec((n, tr), lambda i: (0, i)),
                  pl.BlockSpec(up.shape, lambda i: (0, 0)),
                  pl.BlockSpec((1, ch, ch), lambda i: (i, 0, 0)),
                  pl.BlockSpec((1, 1, ch), lambda i: (i, 0, 0)),
                  pl.BlockSpec((ch, dm), lambda i: (i, 0))],
        out_specs=[pl.BlockSpec((tr, n1), lambda i: (i, 0)),
                   pl.BlockSpec((tr, n - o_m), lambda i: (i, 0)),
                   pl.BlockSpec((ch, dm), lambda i: (i, 0))],
        out_shape=[jax.ShapeDtypeStruct((d, n1), BF16), jax.ShapeDtypeStruct((d, n - o_m), BF16),
                   jax.ShapeDtypeStruct((groups * ch, dm), BF16)],
        compiler_params=_cparams(("parallel",)),
        name="prep",
    )(w_in_t, up, pool_w, pool_scale3, w_pool_proj)


def _modulated_norm(x, mult, shift):
    ms = jnp.mean(x * x, axis=-1, keepdims=True)
    return (x * lax.rsqrt(ms + EPS)) * mult + shift


def _log_sigmoid(z):
    return jnp.minimum(z, 0.0) - jnp.log1p(jnp.exp(-jnp.abs(z)))


def _inproj_body(x_ref, mult_ref, shift_ref, w_ref, bz_ref,
                 k_ref, v_ref, q_ref, laf_ref, lab_ref, sg_ref, p_ref, *, dk, dv, pw):
    def sub_tile(rows):
        h = _modulated_norm(x_ref[rows, :], mult_ref[0], shift_ref[0]).astype(BF16)
        yield
        u = jnp.dot(h, w_ref[...], preferred_element_type=F32)
        yield
        o = 0
        k_ref[rows, :] = u[:, o:o + dk].astype(BF16); o += dk
        v_ref[rows, :] = u[:, o:o + dv].astype(BF16); o += dv
        q_ref[rows, :] = u[:, o:o + dk].astype(BF16); o += dk
        zf = u[:, o:o + dk] + bz_ref[0:1, :]; o += dk
        zb = u[:, o:o + dk] + bz_ref[1:2, :]; o += dk
        laf_ref[rows, :] = _log_sigmoid(zf) * (1.0 / GATE_NORMALIZER)
        lab_ref[rows, :] = _log_sigmoid(zb) * (1.0 / GATE_NORMALIZER)
        sg_ref[rows, :] = _silu(u[:, o:o + dv]).astype(BF16); o += dv
        for g in range(p_ref.shape[0]):
            p_ref[g, rows, :] = u[:, o + g * LANES:o + (g + 1) * LANES].astype(BF16)
        yield

    _staggered([sub_tile(r) for r in _row_groups(x_ref.shape[0])], 3)


def _inproj(x2, mult, shift, w, bz, rows_per_sample, dk, dv, pw, tm):
    n, d = x2.shape
    tps = rows_per_sample // tm
    row = lambda i: (i, 0)
    vec = lambda i: (i // tps, 0, 0)
    outs = [(dk, BF16), (dv, BF16), (dk, BF16), (dk, F32), (dk, F32), (dv, BF16)]
    groups = pw // LANES
    return pl.pallas_call(
        functools.partial(_inproj_body, dk=dk, dv=dv, pw=pw),
        grid=(n // tm,),
        in_specs=[pl.BlockSpec((tm, d), row),
                  pl.BlockSpec((1, 1, d), vec),
                  pl.BlockSpec((1, 1, d), vec),
                  pl.BlockSpec(w.shape, lambda i: (0, 0), pipeline_mode=pl.Buffered(1)),
                  pl.BlockSpec(bz.shape, lambda i: (0, 0))],
        out_specs=[pl.BlockSpec((tm, c), row) for c, _ in outs]
                  + [pl.BlockSpec((groups, tm, LANES), lambda i: (0, i, 0))],
        out_shape=[jax.ShapeDtypeStruct((n, c), t) for c, t in outs]
                  + [jax.ShapeDtypeStruct((groups, n, LANES), BF16)],
        compiler_params=_cparams(("parallel",)),
        name="inproj",
    )(x2, mult, shift, w, bz)


def _gla_direction(k_ref, q_ref, v_ref, la_ref, o_ref, s_ref, reverse, nchunk):
    C = GLA_CHUNK
    hd = HEAD_DK
    cb = MXU_DIM
    lt = nchunk * C
    la = la_ref[0]
    r = lax.broadcasted_iota(I32, (cb, cb), 0)
    c = lax.broadcasted_iota(I32, (cb, cb), 1)
    same = (r // C) == (c // C)
    cum = jnp.where(same & ((c >= r) if reverse else (c <= r)), 1.0, 0.0).astype(BF16)
    la_hi = la.astype(BF16)
    la_lo = (la - la_hi.astype(F32)).astype(BF16)
    la2 = jnp.concatenate([la_hi, la_lo], axis=1)
    bcs = []
    for blk in range(lt // cb):
        part = jnp.dot(cum, la2[blk * cb:(blk + 1) * cb], preferred_element_type=F32)
        bcs.append(part[:, :LANES] + part[:, LANES:])
    ri = lax.broadcasted_iota(I32, (C, 2 * C), 0)
    ci = lax.broadcasted_iota(I32, (C, 2 * C), 1) % C
    tri = (ci >= ri) if reverse else (ci <= ri)
    lane = lax.broadcasted_iota(I32, (1, LANES), 1)
    m0 = (lane < hd).astype(F32)
    m1 = (lane >= hd).astype(F32)
    sr = lax.broadcasted_iota(I32, (LANES, 2 * LANES), 0)
    sl = lax.broadcasted_iota(I32, (LANES, 2 * LANES), 1)
    smask = ((sr < hd) == (sl < LANES)).astype(F32)
    kt = k_ref[0].astype(F32)
    qt = q_ref[0].astype(F32)
    zero_v = jnp.zeros((C, LANES), BF16)
    sweep = list(range(nchunk - 1, -1, -1) if reverse else range(nchunk))
    for first in range(0, nchunk, GLA_BATCH):
        order = sweep[first:first + GLA_BATCH]
        intra, qds, kvs, decs, q2s, kss, kws, scs = {}, {}, {}, {}, {}, {}, {}, {}
        for ch in order:
            lo = ch * C
            b = bcs[lo // cb][lo % cb:lo % cb + C]
            last = b[0:1] if reverse else b[C - 1:C]
            mid = b[C // 2:C // 2 + 1] if reverse else b[C // 2 - 1:C // 2]
            kc = kt[lo:lo + C]
            qc = qt[lo:lo + C]
            q2s[ch] = (qc * jnp.exp(b - mid)).astype(BF16)
            ks = kc * jnp.exp(mid - b)
            kss[ch] = jnp.concatenate([ks * m0, ks * m1], axis=0).astype(BF16)
            qds[ch] = (qc * jnp.exp(b)).astype(BF16)
            kws[ch] = (kc * jnp.exp(last - b)).astype(BF16)
            decs[ch] = last
        yield
        for ch in order:
            scs[ch] = lax.dot_general(q2s[ch], kss[ch], _NT, preferred_element_type=F32)
        yield
        for ch in order:
            v2 = v_ref[0, ch * C:(ch + 1) * C, :]
            kvs[ch] = lax.dot_general(kws[ch], v2, _TN, preferred_element_type=F32) * smask
        yield
        for ch in order:
            sc2 = jnp.where(tri, scs[ch], 0.0).astype(BF16)
            v2 = v_ref[0, ch * C:(ch + 1) * C, :]
            vbd = jnp.concatenate([jnp.concatenate([v2[:, :LANES], zero_v], axis=1),
                                   jnp.concatenate([zero_v, v2[:, LANES:]], axis=1)], axis=0)
            intra[ch] = jnp.dot(sc2, vbd, preferred_element_type=F32)
        yield
        pad = jnp.zeros((LANES - len(order), LANES), F32)
        dec_cols = jnp.exp(jnp.concatenate([decs[ch] for ch in order] + [pad], axis=0).T)
        st = s_ref[...]
        starts = {}
        for i, ch in enumerate(order):
            starts[ch] = st.astype(BF16)
            st = st * dec_cols[:, i:i + 1] + kvs[ch]
        s_ref[...] = st
        yield
        for ch in order:
            inter = jnp.dot(qds[ch], starts[ch], preferred_element_type=F32)
            o_ref[0, ch * C:(ch + 1) * C, :] = inter + intra[ch]
        yield


def _gla_body(kf, qf, vf, laf, kb, qb, vb, lab, h0f, h0b, of, ob, hf_out, hb_out, sf, sb, *, nchunk):
    i = pl.program_id(2)

    @pl.when(i == 0)
    def _():
        sf[...] = h0f[0, 0]
        sb[...] = h0b[0, 0]

    sweeps = [_gla_direction(kf, qf, vf, laf, of, sf, False, nchunk),
              _gla_direction(kb, qb, vb, lab, ob, sb, True, nchunk)]
    for _ in range(GLA_STAGES * pl.cdiv(nchunk, GLA_BATCH)):
        for sweep in sweeps:
            next(sweep)

    @pl.when(i == pl.num_programs(2) - 1)
    def _():
        hf_out[0, 0] = sf[...]
        hb_out[0, 0] = sb[...]


def _gla(k, q, v, laf, lab, h0f, h0b, lt):
    bsz, l, _ = k.shape
    pairs = h0f.shape[1]
    nt = l // lt
    fwd = lambda b, hp, i: (b, i, hp)
    bwd = lambda b, hp, i: (b, nt - 1 - i, hp)
    st = lambda b, hp, i: (b, hp, 0, 0)
    kq = lambda m: pl.BlockSpec((1, lt, LANES), m)
    vv = lambda m: pl.BlockSpec((1, lt, 2 * LANES), m)
    sspec = pl.BlockSpec((1, 1, LANES, 2 * LANES), st)
    return pl.pallas_call(
        functools.partial(_gla_body, nchunk=lt // GLA_CHUNK),
        grid=(bsz, pairs, nt),
        in_specs=[kq(fwd), kq(fwd), vv(fwd), kq(fwd), kq(bwd), kq(bwd), vv(bwd), kq(bwd), sspec, sspec],
        out_specs=[vv(fwd), vv(bwd), sspec, sspec],
        out_shape=[jax.ShapeDtypeStruct(v.shape, F32), jax.ShapeDtypeStruct(v.shape, F32),
                   jax.ShapeDtypeStruct(h0f.shape, F32), jax.ShapeDtypeStruct(h0b.shape, F32)],
        scratch_shapes=[pltpu.VMEM((LANES, 2 * LANES), F32), pltpu.VMEM((LANES, 2 * LANES), F32)],
        compiler_params=_cparams(("parallel", "parallel", "arbitrary")),
        name="gla",
    )(k, q, v, laf, k, q, v, lab, h0f, h0b)


def _pool_body(p_ref, o_ref, s1_ref, *, half, rows):
    xb = p_ref[0, 0]
    l, ch = xb.shape

    def inv_counts(pos, n):
        return 1.0 / (jnp.minimum(pos + half, n) - jnp.maximum(pos - half, 0)).astype(F32)

    blk = MXU_DIM
    r = lax.broadcasted_iota(I32, (blk, blk), 0)
    c = lax.broadcasted_iota(I32, (blk, blk), 1)
    band = ((r // GRID_W == c // GRID_W) & (c - r >= -half) & (c - r <= half - 1)).astype(BF16)
    for i in range(0, l // blk, 2):
        pair = jnp.concatenate([xb[i * blk:(i + 1) * blk], xb[(i + 1) * blk:(i + 2) * blk]], axis=1)
        sums = jnp.dot(band, pair, preferred_element_type=F32)
        s1_ref[i * blk:(i + 1) * blk, :] = sums[:, :ch]
        s1_ref[(i + 1) * blk:(i + 2) * blk, :] = sums[:, ch:]
    s1 = s1_ref[...].reshape(rows, GRID_W, ch)

    def shifted(a, s):
        z = jnp.zeros((abs(s),) + a.shape[1:], a.dtype)
        return jnp.concatenate([a[s:], z], axis=0) if s > 0 else jnp.concatenate([z, a[:s]], axis=0)

    fwd = s1
    bwd = shifted(s1, -1)
    s = 1
    while s < half:
        fwd = fwd + shifted(fwd, s)
        bwd = bwd + shifted(bwd, -s)
        s *= 2
    inv_r = inv_counts(lax.broadcasted_iota(I32, (rows, 1, ch), 0), rows)
    inv_c = inv_counts(lax.broadcasted_iota(I32, (1, GRID_W, ch), 1), GRID_W)
    pooled = (fwd + bwd) * inv_r * inv_c - xb.astype(F32).reshape(rows, GRID_W, ch)
    o_ref[0, 0] = pooled.reshape(l, ch).astype(BF16)


def _pool_groups_body(p_ref, o_ref, s1_ref, *, rows):
    for gi, window in enumerate(POOL_WINDOWS):
        @pl.when(pl.program_id(1) == gi)
        def _(half=window // 2):
            _pool_body(p_ref, o_ref, s1_ref, half=half, rows=rows)


def _pool(pin):
    groups, bsz, l, ch = pin.shape
    blk = pl.BlockSpec((1, 1, l, ch), lambda b, g: (g, b, 0, 0))
    return pl.pallas_call(
        functools.partial(_pool_groups_body, rows=l // GRID_W),
        grid=(bsz, groups),
        in_specs=[blk],
        out_specs=blk,
        out_shape=jax.ShapeDtypeStruct(pin.shape, BF16),
        scratch_shapes=[pltpu.VMEM((l, ch), F32)],
        compiler_params=_cparams(("parallel", "parallel")),
        name="pool",
    )(pin)


def _merge_body(x_ref, m1_ref, s1_ref, g1_ref, m2_ref, s2_ref,
                of_ref, ob_ref, sg_ref, mx_ref,
                wm_ref, wgla_ref, wpool_ref, wout_ref, gn_ref, wr_ref,
                x1_ref, h2_ref, aff_ref, *, heads, ne):
    d = x_ref.shape[1]

    def sub_tile(rows):
        pooled = jnp.concatenate([mx_ref[g, rows, :] for g in range(mx_ref.shape[0])], axis=1)
        bp = jnp.dot(pooled, wpool_ref[...], preferred_element_type=F32)
        x = x_ref[rows, :]
        h = _modulated_norm(x, m1_ref[0], s1_ref[0]).astype(BF16)
        o = of_ref[rows, :] + ob_ref[rows, :]
        sg = sg_ref[rows, :].astype(F32)
        og = []
        for j in range(heads):
            oj = o[:, j * LANES:(j + 1) * LANES]
            oj = oj * lax.rsqrt(jnp.mean(oj * oj, axis=-1, keepdims=True) + EPS) * gn_ref[...]
            og.append((oj * sg[:, j * LANES:(j + 1) * LANES]).astype(BF16))
        og = jnp.concatenate(og, axis=1)
        yield
        gates = jnp.dot(h, wm_ref[...], preferred_element_type=F32)
        bg = jnp.dot(og, wgla_ref[...], preferred_element_type=F32)
        yield
        gates = jax.nn.sigmoid(gates)
        z = (gates[:, :d] * bg + gates[:, d:] * bp).astype(BF16)
        yield
        y = jnp.dot(z, wout_ref[...], preferred_element_type=F32)
        yield
        x1 = x + g1_ref[0] * y
        x1_ref[rows, :] = x1
        h2 = _modulated_norm(x1, m2_ref[0], s2_ref[0])
        hi = h2.astype(BF16)
        h2_ref[rows, :] = hi
        lo = (h2 - hi.astype(F32)).astype(BF16)
        yield
        lg = (jnp.dot(hi, wr_ref[...], preferred_element_type=F32)
              + jnp.dot(lo, wr_ref[...], preferred_element_type=F32))
        yield
        lgt = lg.T
        logit = lgt[0:ne] + lgt[ne:2 * ne]
        mx = jnp.max(logit, axis=0, keepdims=True)
        ex = jnp.exp(logit - mx)
        aff_ref[0, :, rows] = ex / jnp.sum(ex, axis=0, keepdims=True)
        yield

    _staggered([sub_tile(r) for r in _row_groups(x_ref.shape[0])], 7)


def _merge(x2, vecs, of, ob, sg, mixed, wm, wgla, wpool, wout, gn, wr, rows_per_sample, heads, ne, tm):
    n, d = x2.shape
    bsz = n // rows_per_sample
    tps = rows_per_sample // tm
    row = lambda i: (i, 0)
    vec = lambda i: (i // tps, 0, 0)
    full = lambda a: pl.BlockSpec(a.shape, lambda i: (0,) * a.ndim, pipeline_mode=pl.Buffered(1))
    gv = of.shape[-1]
    return pl.pallas_call(
        functools.partial(_merge_body, heads=heads, ne=ne),
        grid=(n // tm,),
        in_specs=[pl.BlockSpec((tm, d), row)] + [pl.BlockSpec((1, 1, d), vec)] * 5
                 + [pl.BlockSpec((tm, gv), row)] * 3 + [pl.BlockSpec((mixed.shape[0], tm, LANES), lambda i: (0, i, 0))]
                 + [full(wm), full(wgla), full(wpool), full(wout), full(gn), full(wr)],
        out_specs=[pl.BlockSpec((tm, d), row),
                   pl.BlockSpec((tm, d), row),
                   pl.BlockSpec((1, ne, tm), lambda i: (i // tps, 0, i % tps))],
        out_shape=[jax.ShapeDtypeStruct((n, d), F32),
                   jax.ShapeDtypeStruct((n, d), BF16),
                   jax.ShapeDtypeStruct((bsz, ne, rows_per_sample), F32)],
        compiler_params=_cparams(("parallel",)),
        name="merge",
    )(x2, *vecs, of, ob, sg, mixed, wm, wgla, wpool, wout, gn, wr)


def _route_body(aff_ref, pos_ref, off_ref, cnt_ref, *, cap, ntb):
    a = aff_ref[0]
    ne, l = a.shape
    blk = ROUTE_BLK

    def bisect(i, v):
        cand = v | jnp.left_shift(jnp.int32(1), 30 - i)
        cnt = jnp.sum((a >= lax.bitcast_convert_type(cand, F32)).astype(F32), axis=1, keepdims=True)
        return jnp.where(cnt >= cap, cand, v)

    thr = lax.bitcast_convert_type(lax.fori_loop(0, 31, bisect, jnp.zeros((ne, 1), I32)), F32)
    gt = a > thr
    tie = a == thr
    need = cap - jnp.sum(gt.astype(F32), axis=1, keepdims=True)

    r = lax.broadcasted_iota(I32, (blk, blk), 0)
    c = lax.broadcasted_iota(I32, (blk, blk), 1)
    upper = (r <= c).astype(BF16)
    lane = lax.broadcasted_iota(I32, (1, LANES), 1)

    def prefix(mask_f):
        run = jnp.zeros((ne, 1), F32)
        offs = jnp.zeros((ne, LANES), F32)
        for tb in range(ntb):
            m = mask_f[:, tb * blk:(tb + 1) * blk].astype(BF16)
            loc = jnp.dot(m, upper, preferred_element_type=F32)
            cnt_ref[:, tb * blk:(tb + 1) * blk] = loc + run
            offs = jnp.where(lane == tb, run, offs)
            run = run + loc[:, blk - 1:blk]
        return jnp.where(lane >= ntb, run, offs)

    tie_f = tie.astype(F32)
    prefix(tie_f)
    tie_excl = cnt_ref[...] - tie_f
    sel = gt | (tie & (tie_excl < need))
    offs = prefix(sel.astype(F32))
    pos_ref[0] = jnp.where(sel, cnt_ref[...] - 1.0, -1.0).astype(I32)
    off_ref[0] = offs.astype(I32)


def _route(aff, cap):
    bsz, ne, l = aff.shape
    spec = lambda s: pl.BlockSpec((1,) + s, lambda b: (b, 0, 0))
    return pl.pallas_call(
        functools.partial(_route_body, cap=cap, ntb=l // ROUTE_BLK),
        grid=(bsz,),
        in_specs=[spec((ne, l))],
        out_specs=[spec((ne, l)), spec((ne, LANES))],
        out_shape=[jax.ShapeDtypeStruct((bsz, ne, l), I32),
                   jax.ShapeDtypeStruct((bsz, ne, LANES), I32)],
        scratch_shapes=[pltpu.VMEM((ne, l), F32)],
        compiler_params=_cparams(("parallel",)),
        name="route",
    )(aff)


def _window_plan(off_ref, bb, tt, experts, ne):
    lows = [off_ref[(bb * ne + e) * LANES + tt] & -BF16_ROWS for e in experts]
    ends = [off_ref[(bb * ne + e) * LANES + tt + 1] for e in experts]
    return lows, ends


def _window_rounds(lows, ends):
    rounds = jnp.int32(0)
    for lo, hi in zip(lows, ends):
        rounds = jnp.maximum(rounds, lax.div(hi - lo + (SLOT_WIN - 1), jnp.int32(SLOT_WIN)))
    return rounds


def _gatherx_body(off_ref, h_ref, pos_ref, xs_ref, *, ne, cap, eg):
    b = pl.program_id(0)
    g = pl.program_id(1)
    tb = pl.program_id(2)
    win = SLOT_WIN

    @pl.when(tb == 0)
    def _():
        xs_ref[...] = jnp.zeros(xs_ref.shape, BF16)

    j_col = lax.broadcasted_iota(I32, (win, 1), 0)
    t = ROUTE_BLK
    nsub = h_ref.shape[0] // t
    experts = [g * eg + k for k in range(eg)]

    def select(sub, lows, r):
        starts = [pl.multiple_of(jnp.minimum(lows[k] + r * win, cap - win), BF16_ROWS) for k in range(eg)]
        pieces = []
        for k in range(eg):
            p = pos_ref[0, k:k + 1, sub * t:(sub + 1) * t]
            hit = (p - starts[k] == j_col) & (p >= lows[k] + r * win)
            pieces.append(jnp.where(hit, 1.0, 0.0).astype(BF16))
        sel = jnp.concatenate(pieces, axis=0)
        rows = jnp.dot(sel, h_ref[sub * t:(sub + 1) * t, :], preferred_element_type=F32).astype(BF16)
        return starts, rows

    def deposit(starts, rows):
        for k in range(eg):
            dst = (0, k, pl.ds(starts[k], win), slice(None))
            xs_ref[dst] = xs_ref[dst] + rows[k * win:(k + 1) * win]

    plans = [_window_plan(off_ref, b, tb * nsub + sub, experts, ne) for sub in range(nsub)]
    firsts = [select(sub, plans[sub][0], 0) for sub in range(nsub)]
    for starts, rows in firsts:
        deposit(starts, rows)
    for sub in range(nsub):
        lows, ends = plans[sub]

        def extra_round(r, carry, lows=lows, sub=sub):
            deposit(*select(sub, lows, r))
            return carry

        lax.fori_loop(1, _window_rounds(lows, ends), extra_round, 0)


def _gatherx(off_flat, h2, pos, cap):
    n, d = h2.shape
    bsz, ne, l = pos.shape
    t = min(l, 2 * TOKEN_TILE)
    ntb = l // t
    eg = SUBLANES
    grid_spec = pltpu.PrefetchScalarGridSpec(
        num_scalar_prefetch=1,
        grid=(bsz, ne // eg, ntb),
        in_specs=[pl.BlockSpec((t, d), lambda b, g, i, off: (b * ntb + i, 0)),
                  pl.BlockSpec((1, eg, t), lambda b, g, i, off: (b, g, i))],
        out_specs=pl.BlockSpec((1, eg, cap, d), lambda b, g, i, off: (b, g, 0, 0)),
    )
    return pl.pallas_call(
        functools.partial(_gatherx_body, ne=ne, cap=cap, eg=eg),
        grid_spec=grid_spec,
        out_shape=jax.ShapeDtypeStruct((bsz, ne, cap, d), BF16),
        compiler_params=_cparams(("parallel", "parallel", "arbitrary")),
        name="gatherx",
    )(off_flat, h2, pos)


def _moe_body(xs_ref, wg_ref, wu_ref, wd_ref, y_ref, *, rc):
    wg = wg_ref[0].astype(BF16)
    wu = wu_ref[0].astype(BF16)
    wd = wd_ref[0].astype(BF16)
    cap = xs_ref.shape[2]
    acts = []
    for ch in range(cap // rc):
        xs = xs_ref[0, 0, ch * rc:(ch + 1) * rc, :]
        gate = jnp.dot(xs, wg, preferred_element_type=F32)
        up = jnp.dot(xs, wu, preferred_element_type=F32)
        acts.append((_silu(gate) * up).astype(BF16))
    for ch in range(cap // rc):
        y_ref[0, 0, ch * rc:(ch + 1) * rc, :] = jnp.dot(acts[ch], wd, preferred_element_type=F32).astype(BF16)


def _moe(xs, wg, wu, wd):
    bsz, ne, cap, d = xs.shape
    de = wg.shape[2]
    slot = pl.BlockSpec((1, 1, cap, d), lambda e, b: (b, e, 0, 0))
    return pl.pallas_call(
        functools.partial(_moe_body, rc=min(cap, 2 * MXU_DIM)),
        grid=(ne, bsz),
        in_specs=[slot,
                  pl.BlockSpec((1, d, de), lambda e, b: (e, 0, 0)),
                  pl.BlockSpec((1, d, de), lambda e, b: (e, 0, 0)),
                  pl.BlockSpec((1, de, d), lambda e, b: (e, 0, 0))],
        out_specs=slot,
        out_shape=jax.ShapeDtypeStruct((bsz, ne, cap, d), BF16),
        compiler_params=_cparams(("parallel", "arbitrary")),
        name="moe",
    )(xs, wg, wu, wd)


def _combine_body(off_ref, x1_ref, g2_ref, fg_ref, pos_ref, aff_ref, y_hbm, o_ref, ybuf, spare, acc_ref, sem,
                  spare_sem, *, ne, cap, nsp, nsub, span):
    b = pl.program_id(0)
    i = pl.program_id(1)
    step = b * nsp + i
    nsteps = pl.num_programs(0) * nsp
    slot = step % 2
    win = SLOT_WIN
    t = ROUTE_BLK
    experts = range(ne)

    def window_starts(lows, r):
        return [pl.multiple_of(jnp.minimum(lows[e] + r * win, cap - win), BF16_ROWS) for e in experts]

    def span_starts(bb, ii):
        lows = _window_plan(off_ref, bb, ii * nsub, experts, ne)[0]
        return [pl.multiple_of(jnp.minimum(lows[e], cap - span), BF16_ROWS) for e in experts]

    def span_copy(bb, e, start, buf):
        return pltpu.make_async_copy(y_hbm.at[bb, e, pl.ds(start, span), :], ybuf.at[buf, e], sem.at[buf, e])

    def spare_copy(e, start):
        return pltpu.make_async_copy(y_hbm.at[b, e, pl.ds(start, win), :],
                                     spare.at[pl.ds(e * win, win), :], spare_sem.at[e])

    def start_step(bb, ii, buf):
        starts = span_starts(bb, ii)
        for e in experts:
            span_copy(bb, e, starts[e], buf).start()

    @pl.when(step == 0)
    def _():
        start_step(b, i, 0)

    nxt = jnp.minimum(step + 1, nsteps - 1)
    start_step(lax.div(nxt, jnp.int32(nsp)), lax.rem(nxt, jnp.int32(nsp)), 1 - slot)

    plans = [_window_plan(off_ref, b, i * nsub + sub, experts, ne) for sub in range(nsub)]
    mine = span_starts(b, i)
    firsts = [window_starts(plans[sub][0], 0) for sub in range(nsub)]
    inside = []
    for sub in range(nsub):
        ok = jnp.bool_(True)
        for e in experts:
            ok = ok & (firsts[sub][e] >= mine[e]) & (firsts[sub][e] + win <= mine[e] + span)
        inside.append(ok)
    j_col = lax.broadcasted_iota(I32, (win, 1), 0)

    def expand(sub, r, starts, rows, enabled=None):
        lows = plans[sub][0]
        pieces = []
        for e in experts:
            p = pos_ref[0, e:e + 1, sub * t:(sub + 1) * t]
            valid = p >= lows[e] + r * win
            hit = (p - starts[e] == j_col) & valid
            if enabled is not None:
                hit = hit & enabled
            pieces.append(jnp.where(hit, aff_ref[0, e:e + 1, sub * t:(sub + 1) * t], 0.0).astype(BF16))
        pmat = jnp.concatenate(pieces, axis=0)
        return lax.dot_general(pmat, rows, _TN, preferred_element_type=F32)

    acc_ref[...] = jnp.zeros(acc_ref.shape, F32)
    for sub in range(nsub):
        lows, ends = plans[sub]

        def own_round(r, carry, lows=lows, sub=sub):
            starts = window_starts(lows, r)
            for e in experts:
                spare_copy(e, starts[e]).start()
            for e in experts:
                spare_copy(e, starts[e]).wait()
            acc_ref[sub] += expand(sub, r, starts, spare[...])
            return carry

        lax.fori_loop(jnp.where(inside[sub], 1, 0), _window_rounds(lows, ends), own_round, 0)

    for e in experts:
        span_copy(b, e, mine[e], slot).wait()
    for sub in range(nsub):
        rows = pl.ds(sub * t, t)
        picked = []
        for e in experts:
            local = pl.multiple_of(jnp.clip(firsts[sub][e] - mine[e], 0, span - win), BF16_ROWS)
            picked.append(ybuf[slot, e, pl.ds(local, win), :])
        moe = expand(sub, 0, firsts[sub], jnp.concatenate(picked, axis=0), inside[sub]) + acc_ref[sub]
        x2 = x1_ref[rows, :] + g2_ref[0] * moe
        ms = jnp.mean(x2 * x2, axis=-1, keepdims=True)
        o_ref[rows, :] = x2 * lax.rsqrt(ms + EPS) * fg_ref[...]

    @pl.when(step == nsteps - 1)
    def _():
        for e in experts:
            span_copy(b, e, mine[e], 1 - slot).wait()


def _combine(off_flat, x1, g2, fg, pos, aff, y, rows_per_sample):
    n, d = x1.shape
    bsz, ne, cap, _ = y.shape
    t = min(rows_per_sample, TOKEN_TILE)
    nsub = t // ROUTE_BLK
    nsp = rows_per_sample // t
    rows = ne * SLOT_WIN
    span = min(cap, nsub * SLOT_WIN * 3 // 4)
    grid_spec = pltpu.PrefetchScalarGridSpec(
        num_scalar_prefetch=1,
        grid=(bsz, nsp),
        in_specs=[pl.BlockSpec((t, d), lambda b, i, off: (b * nsp + i, 0)),
                  pl.BlockSpec((1, 1, d), lambda b, i, off: (b, 0, 0)),
                  pl.BlockSpec((1, d), lambda b, i, off: (0, 0)),
                  pl.BlockSpec((1, ne, t), lambda b, i, off: (b, 0, i)),
                  pl.BlockSpec((1, ne, t), lambda b, i, off: (b, 0, i)),
                  pl.BlockSpec(memory_space=pl.ANY)],
        out_specs=pl.BlockSpec((t, d), lambda b, i, off: (b * nsp + i, 0)),
        scratch_shapes=[pltpu.VMEM((2, ne, span, d), BF16),
                        pltpu.VMEM((rows, d), BF16),
                        pltpu.VMEM((nsub, ROUTE_BLK, d), F32),
                        pltpu.SemaphoreType.DMA((2, ne)),
                        pltpu.SemaphoreType.DMA((ne,))],
    )
    return pl.pallas_call(
        functools.partial(_combine_body, ne=ne, cap=cap, nsp=nsp, nsub=nsub, span=span),
        grid_spec=grid_spec,
        out_shape=jax.ShapeDtypeStruct((n, d), F32),
        compiler_params=_cparams(("arbitrary", "arbitrary")),
        name="combine",
    )(off_flat, x1, g2, fg, pos, aff, y)


def kernel(x, c, ctx, c_ctx, ada_w, ada_b, norm1_g, norm2_g, w_in, w_decay_up, b_decay, gla_norm_g,
           w_gla_proj, pool_w, pool_scale, w_pool_proj, w_out, w_router, w_gate_e, w_up_e, w_down_e,
           final_norm_g):
    assert ada_w.shape[0] == 1, "single-layer block"
    bsz, l, d = x.shape
    lc = ctx.shape[1]
    rank, dk = w_decay_up.shape[2], w_decay_up.shape[3]
    dvh = gla_norm_g.shape[1]
    dv = w_gla_proj.shape[1]
    heads = dv // dvh
    groups, ch = pool_w.shape[1], pool_w.shape[2]
    pw = groups * ch
    ne = w_router.shape[2]
    cap = EC_CAPACITY * l // ne
    assert dk // heads == HEAD_DK and dvh == LANES and ch == LANES and heads % 2 == 0
    assert l % ROUTE_BLK == 0 and l // ROUTE_BLK < LANES and cap >= SLOT_WIN and cap % BF16_ROWS == 0
    assert GRID_W == GLA_CHUNK and MXU_DIM % GRID_W == 0 and (l // MXU_DIM) % 2 == 0

    cin = jnp.zeros((8, d), F32).at[:bsz].set(c).at[bsz].set(c_ctx)
    mods = _ada(cin, ada_w[0], ada_b[0][None, :])
    sh1, sc1, gt1, sh2, sc2, gt2 = [mods[:, i * d:(i + 1) * d] for i in range(6)]
    vec3 = lambda a: a[:, None, :]
    mult1 = norm1_g[0][None, :] * (1.0 + sc1)
    mult2 = norm2_g[0][None, :] * (1.0 + sc2)

    o_r = dk + dv
    o_q = o_r + 2 * rank
    o_g = o_q + dk
    o_p = o_g + dv
    o_m = o_p + pw
    up = jnp.zeros((2 * rank, 2 * dk), F32)
    up = up.at[:rank, :dk].set(w_decay_up[0, 0]).at[rank:, dk:].set(w_decay_up[0, 1])
    psc = pool_scale[0].reshape(groups, 1, ch)
    w1, wm, wpool = _prep(w_in[0].T, up, pool_w[0], psc, w_pool_proj[0], o_r, o_q, o_g, o_m,
                          float(dk // heads) ** -0.5)
    bz = b_decay[0]

    zero_state = jnp.zeros((bsz, heads // 2, LANES, 2 * LANES), F32)
    cm = jnp.broadcast_to(vec3(mult1[bsz:bsz + 1]), (bsz, 1, d))
    cs = jnp.broadcast_to(vec3(sh1[bsz:bsz + 1]), (bsz, 1, d))
    ck, cv, _, claf, clab, _, _ = _inproj(ctx.reshape(bsz * lc, d), cm, cs, w1, bz, lc, dk, dv, pw, min(lc, CTX_TILE))
    r3 = lambda a, n: a.reshape(bsz, n, a.shape[-1])
    _, _, h_f, h_b = _gla(r3(ck, lc), r3(ck, lc), r3(cv, lc), r3(claf, lc), r3(clab, lc),
                          zero_state, zero_state, min(lc, CTX_TILE))

    x2 = x.reshape(bsz * l, d)
    tm = min(l, TOKEN_TILE)
    k, v, q, laf, lab, sg, pin = _inproj(x2, vec3(mult1[:bsz]), vec3(sh1[:bsz]), w1, bz, l, dk, dv, pw, tm)
    of, ob, _, _ = _gla(r3(k, l), r3(q, l), r3(v, l), r3(laf, l), r3(lab, l), h_f, h_b, min(l, GLA_TILE))
    pooled = _pool(pin.reshape(groups, bsz, l, ch)).reshape(groups, bsz * l, ch)

    wr_hi = w_router[0].astype(BF16)
    wr_lo = (w_router[0] - wr_hi.astype(F32)).astype(BF16)
    wr = jnp.zeros((d, LANES), BF16).at[:, :ne].set(wr_hi).at[:, ne:2 * ne].set(wr_lo)
    vecs = [vec3(mult1[:bsz]), vec3(sh1[:bsz]), vec3(gt1[:bsz]), vec3(mult2[:bsz]), vec3(sh2[:bsz])]
    x1, h2, aff = _merge(x2, vecs, of.reshape(bsz * l, dv), ob.reshape(bsz * l, dv), sg, pooled,
                          wm, w_gla_proj[0].astype(BF16), wpool, w_out[0].astype(BF16),
                          gla_norm_g[0][None, :], wr, l, heads, ne, tm)

    pos, offs = _route(aff.reshape(1, bsz * ne, l), cap)
    pos = pos.reshape(bsz, ne, l)
    off_flat = offs.reshape(-1)
    xs = _gatherx(off_flat, h2, pos, cap)
    y = _moe(xs, w_gate_e[0], w_up_e[0], w_down_e[0])
    out = _combine(off_flat, x1, vec3(gt2[:bsz]), final_norm_g[None, :], pos, aff, y, l)
    return out.reshape(bsz, l, d)
```

```python
import functools

import jax
import jax.numpy as jnp
from jax import lax
from jax.experimental import pallas as pl
from jax.experimental.pallas import tpu as pltpu

F32 = jnp.float32
BF16 = jnp.bfloat16
I32 = jnp.int32
HIGHEST = lax.Precision.HIGHEST

EPS = 1e-6
GRID_W = 64
GLA_CHUNK = 64
GLA_STAGES = 6
GLA_BATCH = 8
GATE_NORMALIZER = 16.0
POOL_WINDOWS = (2, 4, 8, 16)
EC_CAPACITY = 2

LANES = 128
SUBLANES = 8
BF16_ROWS = 16
MXU_DIM = 256
VMEM_BYTES = 64 * 1024 * 1024
VMEM_LIMIT = VMEM_BYTES * 7 // 8

HEAD_DK = LANES // 2
ROUTE_BLK = MXU_DIM
SLOT_WIN = 64
SUB_ROWS = MXU_DIM
TOKEN_TILE = 1024
GLA_TILE = 2048
CTX_TILE = 256

_NT = (((1,), (1,)), ((), ()))
_TN = (((0,), (0,)), ((), ()))


def _cparams(sem):
    return pltpu.CompilerParams(dimension_semantics=sem, vmem_limit_bytes=VMEM_LIMIT)


def _sigmoid(x):
    return 0.5 * jnp.tanh(0.5 * x) + 0.5


def _silu(x):
    return x * _sigmoid(x)


def _row_groups(n):
    step = min(n, SUB_ROWS)
    return [pl.ds(i, step) for i in range(0, n, step)]


def _staggered(gens, nstages):
    for t in range(nstages + len(gens) - 1):
        for g in reversed(range(len(gens))):
            if 0 <= t - g < nstages:
                next(gens[g])


def _ada_body(c_ref, w_ref, b_ref, o_ref):
    s = _silu(c_ref[...])
    rows = s.shape[0]
    s_hi = s.astype(BF16)
    s_lo = (s - s_hi.astype(F32)).astype(BF16)
    w = w_ref[...]
    w_hi = w.astype(BF16)
    w_lo = (w - w_hi.astype(F32)).astype(BF16)
    both = jnp.dot(jnp.concatenate([s_hi, s_lo], axis=0), w_hi, preferred_element_type=F32)
    o_ref[...] = (both[:rows] + both[rows:] + jnp.dot(s_hi, w_lo, preferred_element_type=F32)) + b_ref[...]


def _ada(cin, w, b):
    rows, d = cin.shape
    n = w.shape[1]
    nb = n // 4
    return pl.pallas_call(
        _ada_body,
        grid=(n // nb,),
        in_specs=[pl.BlockSpec((rows, d), lambda j: (0, 0)),
                  pl.BlockSpec((d, nb), lambda j: (0, j)),
                  pl.BlockSpec((1, nb), lambda j: (0, j))],
        out_specs=pl.BlockSpec((rows, nb), lambda j: (0, j)),
        out_shape=jax.ShapeDtypeStruct((rows, n), F32),
        compiler_params=_cparams(("arbitrary",)),
        name="ada",
    )(cin, w, b)


def _prep_body(w_ref, up_ref, pw_ref, ps_ref, wpp_ref, w1_ref, wm_ref, wpf_ref, *, o_r, o_q, o_g, o_m, qscale):
    dk = o_g - o_q
    wz = lax.dot_general(w_ref[o_r:o_q, :], up_ref[...], _TN, precision=HIGHEST, preferred_element_type=F32)
    nz = wz.shape[1]
    w1_ref[:, :o_r] = w_ref[:o_r, :].T.astype(BF16)
    w1_ref[:, o_r:o_r + dk] = (w_ref[o_q:o_g, :].T * qscale).astype(BF16)
    w1_ref[:, o_r + dk:o_r + dk + nz] = wz.astype(BF16)
    w1_ref[:, o_r + dk + nz:] = w_ref[o_g:o_m, :].T.astype(BF16)
    wm_ref[...] = w_ref[o_m:, :].T.astype(BF16)
    wpf_ref[...] = jnp.dot(pw_ref[0] * ps_ref[0], wpp_ref[...], precision=HIGHEST,
                           preferred_element_type=F32).astype(BF16)


def _prep(w_in_t, up, pool_w, pool_scale3, w_pool_proj, o_r, o_q, o_g, o_m, qscale):
    n, d = w_in_t.shape
    n1 = o_r + (o_g - o_q) + up.shape[1] + (o_m - o_g)
    groups, ch, _ = pool_w.shape
    tr = d // groups
    dm = w_pool_proj.shape[1]
    return pl.pallas_call(
        functools.partial(_prep_body, o_r=o_r, o_q=o_q, o_g=o_g, o_m=o_m, qscale=qscale),
        grid=(d // tr,),
        in_specs=[pl.BlockSpec((n, tr), lambda i: (0, i)),
                  pl.BlockSpec(up.shape, lambda i: (0, 0)),
                  pl.BlockSpec((1, ch, ch), lambda i: (i, 0, 0)),
                  pl.BlockSpec((1, 1, ch), lambda i: (i, 0, 0)),
                  pl.BlockSpec((ch, dm), lambda i: (i, 0))],
        out_specs=[pl.BlockSpec((tr, n1), lambda i: (i, 0)),
                   pl.BlockSpec((tr, n - o_m), lambda i: (i, 0)),
                   pl.BlockSpec((ch, dm), lambda i: (i, 0))],
        out_shape=[jax.ShapeDtypeStruct((d, n1), BF16), jax.ShapeDtypeStruct((d, n - o_m), BF16),
                   jax.ShapeDtypeStruct((groups * ch, dm), BF16)],
        compiler_params=_cparams(("parallel",)),
        name="prep",
    )(w_in_t, up, pool_w, pool_scale3, w_pool_proj)


def _modulated_norm(x, mult, shift):
    ms = jnp.mean(x * x, axis=-1, keepdims=True)
    return (x * lax.rsqrt(ms + EPS)) * mult + shift


def _log_sigmoid(z):
    return jnp.minimum(z, 0.0) - jnp.log1p(jnp.exp(-jnp.abs(z)))


def _inproj_body(x_ref, mult_ref, shift_ref, w_ref, bz_ref,
                 k_ref, v_ref, q_ref, laf_ref, lab_ref, sg_ref, p_ref, *, dk, dv, pw):
    def sub_tile(rows):
        h = _modulated_norm(x_ref[rows, :], mult_ref[0], shift_ref[0]).astype(BF16)
        yield
        u = jnp.dot(h, w_ref[...], preferred_element_type=F32)
        yield
        o = 0
        k_ref[rows, :] = u[:, o:o + dk].astype(BF16); o += dk
        v_ref[rows, :] = u[:, o:o + dv].astype(BF16); o += dv
        q_ref[rows, :] = u[:, o:o + dk].astype(BF16); o += dk
        zf = u[:, o:o + dk] + bz_ref[0:1, :]; o += dk
        zb = u[:, o:o + dk] + bz_ref[1:2, :]; o += dk
        laf_ref[rows, :] = _log_sigmoid(zf) * (1.0 / GATE_NORMALIZER)
        lab_ref[rows, :] = _log_sigmoid(zb) * (1.0 / GATE_NORMALIZER)
        sg_ref[rows, :] = _silu(u[:, o:o + dv]).astype(BF16); o += dv
        p_ref[rows, :] = u[:, o:o + pw].astype(BF16)
        yield

    _staggered([sub_tile(r) for r in _row_groups(x_ref.shape[0])], 3)


def _inproj(x2, mult, shift, w, bz, rows_per_sample, dk, dv, pw, tm):
    n, d = x2.shape
    tps = rows_per_sample // tm
    row = lambda i: (i, 0)
    vec = lambda i: (i // tps, 0, 0)
    outs = [(dk, BF16), (dv, BF16), (dk, BF16), (dk, F32), (dk, F32), (dv, BF16), (pw, BF16)]
    return pl.pallas_call(
        functools.partial(_inproj_body, dk=dk, dv=dv, pw=pw),
        grid=(n // tm,),
        in_specs=[pl.BlockSpec((tm, d), row),
                  pl.BlockSpec((1, 1, d), vec),
                  pl.BlockSpec((1, 1, d), vec),
                  pl.BlockSpec(w.shape, lambda i: (0, 0), pipeline_mode=pl.Buffered(1)),
                  pl.BlockSpec(bz.shape, lambda i: (0, 0))],
        out_specs=[pl.BlockSpec((tm, c), row) for c, _ in outs],
        out_shape=[jax.ShapeDtypeStruct((n, c), t) for c, t in outs],
        compiler_params=_cparams(("parallel",)),
        name="inproj",
    )(x2, mult, shift, w, bz)


def _gla_direction(k_ref, q_ref, v_ref, la_ref, o_ref, s_ref, reverse, nchunk):
    C = GLA_CHUNK
    hd = HEAD_DK
    cb = MXU_DIM
    lt = nchunk * C
    la = la_ref[0]
    r = lax.broadcasted_iota(I32, (cb, cb), 0)
    c = lax.broadcasted_iota(I32, (cb, cb), 1)
    same = (r // C) == (c // C)
    cum = jnp.where(same & ((c >= r) if reverse else (c <= r)), 1.0, 0.0).astype(BF16)
    la_hi = la.astype(BF16)
    la_lo = (la - la_hi.astype(F32)).astype(BF16)
    la2 = jnp.concatenate([la_hi, la_lo], axis=1)
    bcs = []
    for blk in range(lt // cb):
        part = jnp.dot(cum, la2[blk * cb:(blk + 1) * cb], preferred_element_type=F32)
        bcs.append(part[:, :LANES] + part[:, LANES:])
    ri = lax.broadcasted_iota(I32, (C, 2 * C), 0)
    ci = lax.broadcasted_iota(I32, (C, 2 * C), 1) % C
    tri = (ci >= ri) if reverse else (ci <= ri)
    lane = lax.broadcasted_iota(I32, (1, LANES), 1)
    m0 = (lane < hd).astype(F32)
    m1 = (lane >= hd).astype(F32)
    sr = lax.broadcasted_iota(I32, (LANES, 2 * LANES), 0)
    sl = lax.broadcasted_iota(I32, (LANES, 2 * LANES), 1)
    smask = ((sr < hd) == (sl < LANES)).astype(F32)
    kt = k_ref[0].astype(F32)
    qt = q_ref[0].astype(F32)
    zero_v = jnp.zeros((C, LANES), BF16)
    sweep = list(range(nchunk - 1, -1, -1) if reverse else range(nchunk))
    for first in range(0, nchunk, GLA_BATCH):
        order = sweep[first:first + GLA_BATCH]
        intra, qds, kvs, decs, q2s, kss, kws, scs = {}, {}, {}, {}, {}, {}, {}, {}
        for ch in order:
            lo = ch * C
            b = bcs[lo // cb][lo % cb:lo % cb + C]
            last = b[0:1] if reverse else b[C - 1:C]
            mid = b[C // 2:C // 2 + 1] if reverse else b[C // 2 - 1:C // 2]
            kc = kt[lo:lo + C]
            qc = qt[lo:lo + C]
            q2s[ch] = (qc * jnp.exp(b - mid)).astype(BF16)
            ks = kc * jnp.exp(mid - b)
            kss[ch] = jnp.concatenate([ks * m0, ks * m1], axis=0).astype(BF16)
            qds[ch] = (qc * jnp.exp(b)).astype(BF16)
            kws[ch] = (kc * jnp.exp(last - b)).astype(BF16)
            decs[ch] = last
        yield
        for ch in order:
            scs[ch] = lax.dot_general(q2s[ch], kss[ch], _NT, preferred_element_type=F32)
        yield
        for ch in order:
            v2 = v_ref[0, ch * C:(ch + 1) * C, :]
            kvs[ch] = lax.dot_general(kws[ch], v2, _TN, preferred_element_type=F32) * smask
        yield
        for ch in order:
            sc2 = jnp.where(tri, scs[ch], 0.0).astype(BF16)
            v2 = v_ref[0, ch * C:(ch + 1) * C, :]
            vbd = jnp.concatenate([jnp.concatenate([v2[:, :LANES], zero_v], axis=1),
                                   jnp.concatenate([zero_v, v2[:, LANES:]], axis=1)], axis=0)
            intra[ch] = jnp.dot(sc2, vbd, preferred_element_type=F32)
        yield
        pad = jnp.zeros((LANES - len(order), LANES), F32)
        dec_cols = jnp.exp(jnp.concatenate([decs[ch] for ch in order] + [pad], axis=0).T)
        st = s_ref[...]
        starts = {}
        for i, ch in enumerate(order):
            starts[ch] = st.astype(BF16)
            st = st * dec_cols[:, i:i + 1] + kvs[ch]
        s_ref[...] = st
        yield
        for ch in order:
            inter = jnp.dot(qds[ch], starts[ch], preferred_element_type=F32)
            o_ref[0, ch * C:(ch + 1) * C, :] = inter + intra[ch]
        yield


def _gla_body(kf, qf, vf, laf, kb, qb, vb, lab, h0f, h0b, of, ob, hf_out, hb_out, sf, sb, *, nchunk):
    i = pl.program_id(2)

    @pl.when(i == 0)
    def _():
        sf[...] = h0f[0, 0]
        sb[...] = h0b[0, 0]

    sweeps = [_gla_direction(kf, qf, vf, laf, of, sf, False, nchunk),
              _gla_direction(kb, qb, vb, lab, ob, sb, True, nchunk)]
    for _ in range(GLA_STAGES * pl.cdiv(nchunk, GLA_BATCH)):
        for sweep in sweeps:
            next(sweep)

    @pl.when(i == pl.num_programs(2) - 1)
    def _():
        hf_out[0, 0] = sf[...]
        hb_out[0, 0] = sb[...]


def _gla(k, q, v, laf, lab, h0f, h0b, lt):
    bsz, l, _ = k.shape
    pairs = h0f.shape[1]
    nt = l // lt
    fwd = lambda b, hp, i: (b, i, hp)
    bwd = lambda b, hp, i: (b, nt - 1 - i, hp)
    st = lambda b, hp, i: (b, hp, 0, 0)
    kq = lambda m: pl.BlockSpec((1, lt, LANES), m)
    vv = lambda m: pl.BlockSpec((1, lt, 2 * LANES), m)
    sspec = pl.BlockSpec((1, 1, LANES, 2 * LANES), st)
    return pl.pallas_call(
        functools.partial(_gla_body, nchunk=lt // GLA_CHUNK),
        grid=(bsz, pairs, nt),
        in_specs=[kq(fwd), kq(fwd), vv(fwd), kq(fwd), kq(bwd), kq(bwd), vv(bwd), kq(bwd), sspec, sspec],
        out_specs=[vv(fwd), vv(bwd), sspec, sspec],
        out_shape=[jax.ShapeDtypeStruct(v.shape, F32), jax.ShapeDtypeStruct(v.shape, F32),
                   jax.ShapeDtypeStruct(h0f.shape, F32), jax.ShapeDtypeStruct(h0b.shape, F32)],
        scratch_shapes=[pltpu.VMEM((LANES, 2 * LANES), F32), pltpu.VMEM((LANES, 2 * LANES), F32)],
        compiler_params=_cparams(("parallel", "parallel", "arbitrary")),
        name="gla",
    )(k, q, v, laf, k, q, v, lab, h0f, h0b)


def _pool_body(p_ref, o_ref, s1_ref, *, half, rows):
    xb = p_ref[0]
    l, ch = xb.shape

    def inv_counts(pos, n):
        return 1.0 / (jnp.minimum(pos + half, n) - jnp.maximum(pos - half, 0)).astype(F32)

    blk = MXU_DIM
    r = lax.broadcasted_iota(I32, (blk, blk), 0)
    c = lax.broadcasted_iota(I32, (blk, blk), 1)
    band = ((r // GRID_W == c // GRID_W) & (c - r >= -half) & (c - r <= half - 1)).astype(BF16)
    for i in range(0, l // blk, 2):
        pair = jnp.concatenate([xb[i * blk:(i + 1) * blk], xb[(i + 1) * blk:(i + 2) * blk]], axis=1)
        sums = jnp.dot(band, pair, preferred_element_type=F32)
        s1_ref[i * blk:(i + 1) * blk, :] = sums[:, :ch]
        s1_ref[(i + 1) * blk:(i + 2) * blk, :] = sums[:, ch:]
    s1 = s1_ref[...].reshape(rows, GRID_W, ch)

    def shifted(a, s):
        z = jnp.zeros((abs(s),) + a.shape[1:], a.dtype)
        return jnp.concatenate([a[s:], z], axis=0) if s > 0 else jnp.concatenate([z, a[:s]], axis=0)

    fwd = s1
    bwd = shifted(s1, -1)
    s = 1
    while s < half:
        fwd = fwd + shifted(fwd, s)
        bwd = bwd + shifted(bwd, -s)
        s *= 2
    inv_r = inv_counts(lax.broadcasted_iota(I32, (rows, 1, ch), 0), rows)
    inv_c = inv_counts(lax.broadcasted_iota(I32, (1, GRID_W, ch), 1), GRID_W)
    pooled = (fwd + bwd) * inv_r * inv_c - xb.astype(F32).reshape(rows, GRID_W, ch)
    o_ref[0] = pooled.reshape(l, ch).astype(BF16)


def _pool_groups_body(p_ref, o_ref, s1_ref, *, rows):
    for gi, window in enumerate(POOL_WINDOWS):
        @pl.when(pl.program_id(1) == gi)
        def _(half=window // 2):
            _pool_body(p_ref, o_ref, s1_ref, half=half, rows=rows)


def _pool(pin, ch):
    bsz, l, pw = pin.shape
    return pl.pallas_call(
        functools.partial(_pool_groups_body, rows=l // GRID_W),
        grid=(bsz, pw // ch),
        in_specs=[pl.BlockSpec((1, l, ch), lambda b, g: (b, 0, g))],
        out_specs=pl.BlockSpec((1, l, ch), lambda b, g: (b, 0, g)),
        out_shape=jax.ShapeDtypeStruct((bsz, l, pw), BF16),
        scratch_shapes=[pltpu.VMEM((l, ch), F32)],
        compiler_params=_cparams(("parallel", "parallel")),
        name="pool",
    )(pin)


def _merge_body(x_ref, m1_ref, s1_ref, g1_ref, m2_ref, s2_ref,
                of_ref, ob_ref, sg_ref, mx_ref,
                wm_ref, wgla_ref, wpool_ref, wout_ref, gn_ref, wr_ref,
                x1_ref, h2_ref, aff_ref, *, heads, ne):
    d = x_ref.shape[1]

    def sub_tile(rows):
        bp = jnp.dot(mx_ref[rows, :], wpool_ref[...], preferred_element_type=F32)
        x = x_ref[rows, :]
        h = _modulated_norm(x, m1_ref[0], s1_ref[0]).astype(BF16)
        o = of_ref[rows, :] + ob_ref[rows, :]
        sg = sg_ref[rows, :].astype(F32)
        og = []
        for j in range(heads):
            oj = o[:, j * LANES:(j + 1) * LANES]
            oj = oj * lax.rsqrt(jnp.mean(oj * oj, axis=-1, keepdims=True) + EPS) * gn_ref[...]
            og.append((oj * sg[:, j * LANES:(j + 1) * LANES]).astype(BF16))
        og = jnp.concatenate(og, axis=1)
        yield
        gates = jnp.dot(h, wm_ref[...], preferred_element_type=F32)
        bg = jnp.dot(og, wgla_ref[...], preferred_element_type=F32)
        yield
        gates = _sigmoid(gates)
        z = (gates[:, :d] * bg + gates[:, d:] * bp).astype(BF16)
        yield
        y = jnp.dot(z, wout_ref[...], preferred_element_type=F32)
        yield
        x1 = x + g1_ref[0] * y
        x1_ref[rows, :] = x1
        h2 = _modulated_norm(x1, m2_ref[0], s2_ref[0])
        hi = h2.astype(BF16)
        h2_ref[rows, :] = hi
        lo = (h2 - hi.astype(F32)).astype(BF16)
        yield
        lg = (jnp.dot(hi, wr_ref[...], preferred_element_type=F32)
              + jnp.dot(lo, wr_ref[...], preferred_element_type=F32))
        yield
        lgt = lg.T
        logit = lgt[0:ne] + lgt[ne:2 * ne]
        mx = jnp.max(logit, axis=0, keepdims=True)
        ex = jnp.exp(logit - mx)
        aff_ref[0, :, rows] = ex / jnp.sum(ex, axis=0, keepdims=True)
        yield

    _staggered([sub_tile(r) for r in _row_groups(x_ref.shape[0])], 7)


def _merge(x2, vecs, of, ob, sg, mixed, wm, wgla, wpool, wout, gn, wr, rows_per_sample, heads, ne, tm):
    n, d = x2.shape
    bsz = n // rows_per_sample
    tps = rows_per_sample // tm
    row = lambda i: (i, 0)
    vec = lambda i: (i // tps, 0, 0)
    full = lambda a: pl.BlockSpec(a.shape, lambda i: (0,) * a.ndim, pipeline_mode=pl.Buffered(1))
    gv = of.shape[-1]
    return pl.pallas_call(
        functools.partial(_merge_body, heads=heads, ne=ne),
        grid=(n // tm,),
        in_specs=[pl.BlockSpec((tm, d), row)] + [pl.BlockSpec((1, 1, d), vec)] * 5
                 + [pl.BlockSpec((tm, gv), row)] * 4
                 + [full(wm), full(wgla), full(wpool), full(wout), full(gn), full(wr)],
        out_specs=[pl.BlockSpec((tm, d), row),
                   pl.BlockSpec((tm, d), row),
                   pl.BlockSpec((1, ne, tm), lambda i: (i // tps, 0, i % tps))],
        out_shape=[jax.ShapeDtypeStruct((n, d), F32),
                   jax.ShapeDtypeStruct((n, d), BF16),
                   jax.ShapeDtypeStruct((bsz, ne, rows_per_sample), F32)],
        compiler_params=_cparams(("parallel",)),
        name="merge",
    )(x2, *vecs, of, ob, sg, mixed, wm, wgla, wpool, wout, gn, wr)


def _route_body(aff_ref, pos_ref, off_ref, cnt_ref, *, cap, ntb):
    a = aff_ref[0]
    ne, l = a.shape
    blk = ROUTE_BLK

    def bisect(i, v):
        cand = v | jnp.left_shift(jnp.int32(1), 30 - i)
        cnt = jnp.sum((a >= lax.bitcast_convert_type(cand, F32)).astype(F32), axis=1, keepdims=True)
        return jnp.where(cnt >= cap, cand, v)

    thr = lax.bitcast_convert_type(lax.fori_loop(0, 31, bisect, jnp.zeros((ne, 1), I32)), F32)
    gt = a > thr
    tie = a == thr
    need = cap - jnp.sum(gt.astype(F32), axis=1, keepdims=True)

    r = lax.broadcasted_iota(I32, (blk, blk), 0)
    c = lax.broadcasted_iota(I32, (blk, blk), 1)
    upper = (r <= c).astype(BF16)
    lane = lax.broadcasted_iota(I32, (1, LANES), 1)

    def prefix(mask_f):
        run = jnp.zeros((ne, 1), F32)
        offs = jnp.zeros((ne, LANES), F32)
        for tb in range(ntb):
            m = mask_f[:, tb * blk:(tb + 1) * blk].astype(BF16)
            loc = jnp.dot(m, upper, preferred_element_type=F32)
            cnt_ref[:, tb * blk:(tb + 1) * blk] = loc + run
            offs = jnp.where(lane == tb, run, offs)
            run = run + loc[:, blk - 1:blk]
        return jnp.where(lane >= ntb, run, offs)

    tie_f = tie.astype(F32)
    prefix(tie_f)
    tie_excl = cnt_ref[...] - tie_f
    sel = gt | (tie & (tie_excl < need))
    offs = prefix(sel.astype(F32))
    pos_ref[0] = jnp.where(sel, cnt_ref[...] - 1.0, -1.0).astype(I32)
    off_ref[0] = offs.astype(I32)


def _route(aff, cap):
    bsz, ne, l = aff.shape
    spec = lambda s: pl.BlockSpec((1,) + s, lambda b: (b, 0, 0))
    return pl.pallas_call(
        functools.partial(_route_body, cap=cap, ntb=l // ROUTE_BLK),
        grid=(bsz,),
        in_specs=[spec((ne, l))],
        out_specs=[spec((ne, l)), spec((ne, LANES))],
        out_shape=[jax.ShapeDtypeStruct((bsz, ne, l), I32),
                   jax.ShapeDtypeStruct((bsz, ne, LANES), I32)],
        scratch_shapes=[pltpu.VMEM((ne, l), F32)],
        compiler_params=_cparams(("parallel",)),
        name="route",
    )(aff)


def _window_plan(off_ref, bb, tt, experts, ne):
    lows = [off_ref[(bb * ne + e) * LANES + tt] & -BF16_ROWS for e in experts]
    ends = [off_ref[(bb * ne + e) * LANES + tt + 1] for e in experts]
    return lows, ends


def _window_rounds(lows, ends):
    rounds = jnp.int32(0)
    for lo, hi in zip(lows, ends):
        rounds = jnp.maximum(rounds, lax.div(hi - lo + (SLOT_WIN - 1), jnp.int32(SLOT_WIN)))
    return rounds


def _gatherx_body(off_ref, h_ref, pos_ref, xs_ref, *, ne, cap, eg):
    b = pl.program_id(0)
    g = pl.program_id(1)
    tb = pl.program_id(2)
    win = SLOT_WIN

    @pl.when(tb == 0)
    def _():
        xs_ref[...] = jnp.zeros(xs_ref.shape, BF16)

    j_col = lax.broadcasted_iota(I32, (win, 1), 0)
    t = ROUTE_BLK
    nsub = h_ref.shape[0] // t
    experts = [g * eg + k for k in range(eg)]

    def select(sub, lows, r):
        starts = [pl.multiple_of(jnp.minimum(lows[k] + r * win, cap - win), BF16_ROWS) for k in range(eg)]
        pieces = []
        for k in range(eg):
            p = pos_ref[0, k:k + 1, sub * t:(sub + 1) * t]
            hit = (p - starts[k] == j_col) & (p >= lows[k] + r * win)
            pieces.append(jnp.where(hit, 1.0, 0.0).astype(BF16))
        sel = jnp.concatenate(pieces, axis=0)
        rows = jnp.dot(sel, h_ref[sub * t:(sub + 1) * t, :], preferred_element_type=F32).astype(BF16)
        return starts, rows

    def deposit(starts, rows):
        for k in range(eg):
            dst = (0, k, pl.ds(starts[k], win), slice(None))
            xs_ref[dst] = xs_ref[dst] + rows[k * win:(k + 1) * win]

    plans = [_window_plan(off_ref, b, tb * nsub + sub, experts, ne) for sub in range(nsub)]
    firsts = [select(sub, plans[sub][0], 0) for sub in range(nsub)]
    for starts, rows in firsts:
        deposit(starts, rows)
    for sub in range(nsub):
        lows, ends = plans[sub]

        def extra_round(r, carry, lows=lows, sub=sub):
            deposit(*select(sub, lows, r))
            return carry

        lax.fori_loop(1, _window_rounds(lows, ends), extra_round, 0)


def _gatherx(off_flat, h2, pos, cap):
    n, d = h2.shape
    bsz, ne, l = pos.shape
    t = min(l, 2 * TOKEN_TILE)
    ntb = l // t
    eg = SUBLANES
    grid_spec = pltpu.PrefetchScalarGridSpec(
        num_scalar_prefetch=1,
        grid=(bsz, ne // eg, ntb),
        in_specs=[pl.BlockSpec((t, d), lambda b, g, i, off: (b * ntb + i, 0)),
                  pl.BlockSpec((1, eg, t), lambda b, g, i, off: (b, g, i))],
        out_specs=pl.BlockSpec((1, eg, cap, d), lambda b, g, i, off: (b, g, 0, 0)),
    )
    return pl.pallas_call(
        functools.partial(_gatherx_body, ne=ne, cap=cap, eg=eg),
        grid_spec=grid_spec,
        out_shape=jax.ShapeDtypeStruct((bsz, ne, cap, d), BF16),
        compiler_params=_cparams(("parallel", "parallel", "arbitrary")),
        name="gatherx",
    )(off_flat, h2, pos)


def _moe_body(xs_ref, wg_ref, wu_ref, wd_ref, y_ref, *, rc):
    wg = wg_ref[0].astype(BF16)
    wu = wu_ref[0].astype(BF16)
    wd = wd_ref[0].astype(BF16)
    cap = xs_ref.shape[2]
    acts = []
    for ch in range(cap // rc):
        xs = xs_ref[0, 0, ch * rc:(ch + 1) * rc, :]
        gate = jnp.dot(xs, wg, preferred_element_type=F32)
        up = jnp.dot(xs, wu, preferred_element_type=F32)
        acts.append((_silu(gate) * up).astype(BF16))
    for ch in range(cap // rc):
        y_ref[0, 0, ch * rc:(ch + 1) * rc, :] = jnp.dot(acts[ch], wd, preferred_element_type=F32).astype(BF16)


def _moe(xs, wg, wu, wd):
    bsz, ne, cap, d = xs.shape
    de = wg.shape[2]
    slot = pl.BlockSpec((1, 1, cap, d), lambda e, b: (b, e, 0, 0))
    return pl.pallas_call(
        functools.partial(_moe_body, rc=min(cap, 2 * MXU_DIM)),
        grid=(ne, bsz),
        in_specs=[slot,
                  pl.BlockSpec((1, d, de), lambda e, b: (e, 0, 0)),
                  pl.BlockSpec((1, d, de), lambda e, b: (e, 0, 0)),
                  pl.BlockSpec((1, de, d), lambda e, b: (e, 0, 0))],
        out_specs=slot,
        out_shape=jax.ShapeDtypeStruct((bsz, ne, cap, d), BF16),
        compiler_params=_cparams(("parallel", "arbitrary")),
        name="moe",
    )(xs, wg, wu, wd)


def _combine_body(off_ref, x1_ref, g2_ref, fg_ref, pos_ref, aff_ref, y_hbm, o_ref, ybuf, spare, acc_ref, sem,
                  spare_sem, *, ne, cap, nsp, nsub, span):
    b = pl.program_id(0)
    i = pl.program_id(1)
    step = b * nsp + i
    nsteps = pl.num_programs(0) * nsp
    slot = step % 2
    win = SLOT_WIN
    t = ROUTE_BLK
    experts = range(ne)

    def window_starts(lows, r):
        return [pl.multiple_of(jnp.minimum(lows[e] + r * win, cap - win), BF16_ROWS) for e in experts]

    def span_starts(bb, ii):
        lows = _window_plan(off_ref, bb, ii * nsub, experts, ne)[0]
        return [pl.multiple_of(jnp.minimum(lows[e], cap - span), BF16_ROWS) for e in experts]

    def span_copy(bb, e, start, buf):
        return pltpu.make_async_copy(y_hbm.at[bb, e, pl.ds(start, span), :], ybuf.at[buf, e], sem.at[buf, e])

    def spare_copy(e, start):
        return pltpu.make_async_copy(y_hbm.at[b, e, pl.ds(start, win), :],
                                     spare.at[pl.ds(e * win, win), :], spare_sem.at[e])

    def start_step(bb, ii, buf):
        starts = span_starts(bb, ii)
        for e in experts:
            span_copy(bb, e, starts[e], buf).start()

    @pl.when(step == 0)
    def _():
        start_step(b, i, 0)

    nxt = jnp.minimum(step + 1, nsteps - 1)
    start_step(lax.div(nxt, jnp.int32(nsp)), lax.rem(nxt, jnp.int32(nsp)), 1 - slot)

    plans = [_window_plan(off_ref, b, i * nsub + sub, experts, ne) for sub in range(nsub)]
    mine = span_starts(b, i)
    firsts = [window_starts(plans[sub][0], 0) for sub in range(nsub)]
    inside = []
    for sub in range(nsub):
        ok = jnp.bool_(True)
        for e in experts:
            ok = ok & (firsts[sub][e] >= mine[e]) & (firsts[sub][e] + win <= mine[e] + span)
        inside.append(ok)
    j_col = lax.broadcasted_iota(I32, (win, 1), 0)

    def expand(sub, r, starts, rows, enabled=None):
        lows = plans[sub][0]
        pieces = []
        for e in experts:
            p = pos_ref[0, e:e + 1, sub * t:(sub + 1) * t]
            valid = p >= lows[e] + r * win
            hit = (p - starts[e] == j_col) & valid
            if enabled is not None:
                hit = hit & enabled
            pieces.append(jnp.where(hit, aff_ref[0, e:e + 1, sub * t:(sub + 1) * t], 0.0).astype(BF16))
        pmat = jnp.concatenate(pieces, axis=0)
        return lax.dot_general(pmat, rows, _TN, preferred_element_type=F32)

    acc_ref[...] = jnp.zeros(acc_ref.shape, F32)
    for sub in range(nsub):
        lows, ends = plans[sub]

        def own_round(r, carry, lows=lows, sub=sub):
            starts = window_starts(lows, r)
            for e in experts:
                spare_copy(e, starts[e]).start()
            for e in experts:
                spare_copy(e, starts[e]).wait()
            acc_ref[sub] += expand(sub, r, starts, spare[...])
            return carry

        lax.fori_loop(jnp.where(inside[sub], 1, 0), _window_rounds(lows, ends), own_round, 0)

    for e in experts:
        span_copy(b, e, mine[e], slot).wait()
    for sub in range(nsub):
        rows = pl.ds(sub * t, t)
        picked = []
        for e in experts:
            local = pl.multiple_of(jnp.clip(firsts[sub][e] - mine[e], 0, span - win), BF16_ROWS)
            picked.append(ybuf[slot, e, pl.ds(local, win), :])
        moe = expand(sub, 0, firsts[sub], jnp.concatenate(picked, axis=0), inside[sub]) + acc_ref[sub]
        x2 = x1_ref[rows, :] + g2_ref[0] * moe
        ms = jnp.mean(x2 * x2, axis=-1, keepdims=True)
        o_ref[rows, :] = x2 * lax.rsqrt(ms + EPS) * fg_ref[...]

    @pl.when(step == nsteps - 1)
    def _():
        for e in experts:
            span_copy(b, e, mine[e], 1 - slot).wait()


def _combine(off_flat, x1, g2, fg, pos, aff, y, rows_per_sample):
    n, d = x1.shape
    bsz, ne, cap, _ = y.shape
    t = min(rows_per_sample, TOKEN_TILE)
    nsub = t // ROUTE_BLK
    nsp = rows_per_sample // t
    rows = ne * SLOT_WIN
    span = min(cap, nsub * SLOT_WIN * 3 // 4)
    grid_spec = pltpu.PrefetchScalarGridSpec(
        num_scalar_prefetch=1,
        grid=(bsz, nsp),
        in_specs=[pl.BlockSpec((t, d), lambda b, i, off: (b * nsp + i, 0)),
                  pl.BlockSpec((1, 1, d), lambda b, i, off: (b, 0, 0)),
                  pl.BlockSpec((1, d), lambda b, i, off: (0, 0)),
                  pl.BlockSpec((1, ne, t), lambda b, i, off: (b, 0, i)),
                  pl.BlockSpec((1, ne, t), lambda b, i, off: (b, 0, i)),
                  pl.BlockSpec(memory_space=pl.ANY)],
        out_specs=pl.BlockSpec((t, d), lambda b, i, off: (b * nsp + i, 0)),
        scratch_shapes=[pltpu.VMEM((2, ne, span, d), BF16),
                        pltpu.VMEM((rows, d), BF16),
                        pltpu.VMEM((nsub, ROUTE_BLK, d), F32),
                        pltpu.SemaphoreType.DMA((2, ne)),
                        pltpu.SemaphoreType.DMA((ne,))],
    )
    return pl.pallas_call(
        functools.partial(_combine_body, ne=ne, cap=cap, nsp=nsp, nsub=nsub, span=span),
        grid_spec=grid_spec,
        out_shape=jax.ShapeDtypeStruct((n, d), F32),
        compiler_params=_cparams(("arbitrary", "arbitrary")),
        name="combine",
    )(off_flat, x1, g2, fg, pos, aff, y)


def kernel(x, c, ctx, c_ctx, ada_w, ada_b, norm1_g, norm2_g, w_in, w_decay_up, b_decay, gla_norm_g,
           w_gla_proj, pool_w, pool_scale, w_pool_proj, w_out, w_router, w_gate_e, w_up_e, w_down_e,
           final_norm_g):
    assert ada_w.shape[0] == 1, "single-layer block"
    bsz, l, d = x.shape
    lc = ctx.shape[1]
    rank, dk = w_decay_up.shape[2], w_decay_up.shape[3]
    dvh = gla_norm_g.shape[1]
    dv = w_gla_proj.shape[1]
    heads = dv // dvh
    groups, ch = pool_w.shape[1], pool_w.shape[2]
    pw = groups * ch
    ne = w_router.shape[2]
    cap = EC_CAPACITY * l // ne
    assert dk // heads == HEAD_DK and dvh == LANES and ch == LANES and heads % 2 == 0
    assert l % ROUTE_BLK == 0 and l // ROUTE_BLK < LANES and cap >= SLOT_WIN and cap % BF16_ROWS == 0
    assert GRID_W == GLA_CHUNK and MXU_DIM % GRID_W == 0 and (l // MXU_DIM) % 2 == 0

    cin = jnp.zeros((8, d), F32).at[:bsz].set(c).at[bsz].set(c_ctx)
    mods = _ada(cin, ada_w[0], ada_b[0][None, :])
    sh1, sc1, gt1, sh2, sc2, gt2 = [mods[:, i * d:(i + 1) * d] for i in range(6)]
    vec3 = lambda a: a[:, None, :]
    mult1 = norm1_g[0][None, :] * (1.0 + sc1)
    mult2 = norm2_g[0][None, :] * (1.0 + sc2)

    o_r = dk + dv
    o_q = o_r + 2 * rank
    o_g = o_q + dk
    o_p = o_g + dv
    o_m = o_p + pw
    up = jnp.zeros((2 * rank, 2 * dk), F32)
    up = up.at[:rank, :dk].set(w_decay_up[0, 0]).at[rank:, dk:].set(w_decay_up[0, 1])
    psc = pool_scale[0].reshape(groups, 1, ch)
    w1, wm, wpool = _prep(w_in[0].T, up, pool_w[0], psc, w_pool_proj[0], o_r, o_q, o_g, o_m,
                          float(dk // heads) ** -0.5)
    bz = b_decay[0]

    zero_state = jnp.zeros((bsz, heads // 2, LANES, 2 * LANES), F32)
    cm = jnp.broadcast_to(vec3(mult1[bsz:bsz + 1]), (bsz, 1, d))
    cs = jnp.broadcast_to(vec3(sh1[bsz:bsz + 1]), (bsz, 1, d))
    ck, cv, _, claf, clab, _, _ = _inproj(ctx.reshape(bsz * lc, d), cm, cs, w1, bz, lc, dk, dv, pw, min(lc, CTX_TILE))
    r3 = lambda a, n: a.reshape(bsz, n, a.shape[-1])
    _, _, h_f, h_b = _gla(r3(ck, lc), r3(ck, lc), r3(cv, lc), r3(claf, lc), r3(clab, lc),
                          zero_state, zero_state, min(lc, CTX_TILE))

    x2 = x.reshape(bsz * l, d)
    tm = min(l, TOKEN_TILE)
    k, v, q, laf, lab, sg, pin = _inproj(x2, vec3(mult1[:bsz]), vec3(sh1[:bsz]), w1, bz, l, dk, dv, pw, tm)
    of, ob, _, _ = _gla(r3(k, l), r3(q, l), r3(v, l), r3(laf, l), r3(lab, l), h_f, h_b, min(l, GLA_TILE))
    pooled = _pool(r3(pin, l), ch).reshape(bsz * l, pw)

    wr_hi = w_router[0].astype(BF16)
    wr_lo = (w_router[0] - wr_hi.astype(F32)).astype(BF16)
    wr = jnp.zeros((d, LANES), BF16).at[:, :ne].set(wr_hi).at[:, ne:2 * ne].set(wr_lo)
    vecs = [vec3(mult1[:bsz]), vec3(sh1[:bsz]), vec3(gt1[:bsz]), vec3(mult2[:bsz]), vec3(sh2[:bsz])]
    x1, h2, aff = _merge(x2, vecs, of.reshape(bsz * l, dv), ob.reshape(bsz * l, dv), sg, pooled,
                          wm, w_gla_proj[0].astype(BF16), wpool, w_out[0].astype(BF16),
                          gla_norm_g[0][None, :], wr, l, heads, ne, tm)

    pos, offs = _route(aff.reshape(1, bsz * ne, l), cap)
    pos = pos.reshape(bsz, ne, l)
    off_flat = offs.reshape(-1)
    xs = _gatherx(off_flat, h2, pos, cap)
    y = _moe(xs, w_gate_e[0], w_up_e[0], w_down_e[0])
    out = _combine(off_flat, x1, vec3(gt2[:bsz]), final_norm_g[None, :], pos, aff, y, l)
    return out.reshape(bsz, l, d)
```

```python
import functools

import jax
import jax.numpy as jnp
from jax import lax
from jax.experimental import pallas as pl
from jax.experimental.pallas import tpu as pltpu

F32 = jnp.float32
BF16 = jnp.bfloat16
I32 = jnp.int32
HIGHEST = lax.Precision.HIGHEST

EPS = 1e-6
GRID_W = 64
GLA_CHUNK = 64
GLA_STAGES = 6
GLA_BATCH = 8
GATE_NORMALIZER = 16.0
POOL_WINDOWS = (2, 4, 8, 16)
EC_CAPACITY = 2

LANES = 128
SUBLANES = 8
BF16_ROWS = 16
MXU_DIM = 256
VMEM_BYTES = 64 * 1024 * 1024
VMEM_LIMIT = VMEM_BYTES * 7 // 8

HEAD_DK = LANES // 2
ROUTE_BLK = MXU_DIM
SLOT_WIN = 64
SUB_ROWS = MXU_DIM
TOKEN_TILE = 1024
GLA_TILE = 2048
CTX_TILE = 256
COND_ROWS = SUBLANES
MOD_SHIFT1, MOD_MULT1, MOD_GATE1, MOD_SHIFT2, MOD_MULT2, MOD_GATE2 = range(6)


def _mod_spec(d, which, sample_of):
    return pl.BlockSpec((1, 1, d), lambda *g: (which * COND_ROWS + sample_of(*g[:2]), 0, 0))

_NT = (((1,), (1,)), ((), ()))
_TN = (((0,), (0,)), ((), ()))


def _cparams(sem):
    return pltpu.CompilerParams(dimension_semantics=sem, vmem_limit_bytes=VMEM_LIMIT)


def _sigmoid(x):
    return 0.5 * jnp.tanh(0.5 * x) + 0.5


def _silu(x):
    return x * _sigmoid(x)


def _row_groups(n):
    step = min(n, SUB_ROWS)
    return [pl.ds(i, step) for i in range(0, n, step)]


def _staggered(gens, nstages):
    for t in range(nstages + len(gens) - 1):
        for g in reversed(range(len(gens))):
            if 0 <= t - g < nstages:
                next(gens[g])


def _ada_body(c_ref, w_ref, b_ref, g_ref, o_ref):
    s = _silu(c_ref[...])
    rows = s.shape[0]
    s_hi = s.astype(BF16)
    s_lo = (s - s_hi.astype(F32)).astype(BF16)
    w = w_ref[...]
    w_hi = w.astype(BF16)
    w_lo = (w - w_hi.astype(F32)).astype(BF16)
    both = jnp.dot(jnp.concatenate([s_hi, s_lo], axis=0), w_hi, preferred_element_type=F32)
    out = (both[:rows] + both[rows:] + jnp.dot(s_hi, w_lo, preferred_element_type=F32)) + b_ref[...]
    j = pl.program_id(0)
    gain = jnp.where(j == MOD_MULT1, g_ref[0:1, :], g_ref[1:2, :])
    o_ref[...] = jnp.where((j == MOD_MULT1) | (j == MOD_MULT2), gain * (1.0 + out), out)


def _ada(cin, w, b, gains):
    rows, d = cin.shape
    nvec = w.shape[1] // d
    mods = pl.pallas_call(
        _ada_body,
        grid=(nvec,),
        in_specs=[pl.BlockSpec((rows, d), lambda j: (0, 0)),
                  pl.BlockSpec((d, d), lambda j: (0, j)),
                  pl.BlockSpec((1, d), lambda j: (0, j)),
                  pl.BlockSpec(gains.shape, lambda j: (0, 0))],
        out_specs=pl.BlockSpec((rows, d), lambda j: (j, 0)),
        out_shape=jax.ShapeDtypeStruct((nvec * rows, d), F32),
        compiler_params=_cparams(("arbitrary",)),
        name="ada",
    )(cin, w, b, gains)
    return mods.reshape(nvec * rows, 1, d)


def _prep_body(w_ref, up_ref, pw_ref, ps_ref, wpp_ref, w1_ref, wm_ref, wpf_ref, *, o_r, o_q, o_g, o_m, qscale):
    dk = o_g - o_q
    wz = lax.dot_general(w_ref[o_r:o_q, :], up_ref[...], _TN, precision=HIGHEST, preferred_element_type=F32)
    nz = wz.shape[1]
    w1_ref[:, :o_r] = w_ref[:o_r, :].T.astype(BF16)
    w1_ref[:, o_r:o_r + dk] = (w_ref[o_q:o_g, :].T * qscale).astype(BF16)
    w1_ref[:, o_r + dk:o_r + dk + nz] = wz.astype(BF16)
    w1_ref[:, o_r + dk + nz:] = w_ref[o_g:o_m, :].T.astype(BF16)
    wm_ref[...] = w_ref[o_m:, :].T.astype(BF16)
    wpf_ref[...] = jnp.dot(pw_ref[0] * ps_ref[0], wpp_ref[...], precision=HIGHEST,
                           preferred_element_type=F32).astype(BF16)


def _prep(w_in_t, up, pool_w, pool_scale3, w_pool_proj, o_r, o_q, o_g, o_m, qscale):
    n, d = w_in_t.shape
    n1 = o_r + (o_g - o_q) + up.shape[1] + (o_m - o_g)
    groups, ch, _ = pool_w.shape
    tr = d // groups
    dm = w_pool_proj.shape[1]
    return pl.pallas_call(
        functools.partial(_prep_body, o_r=o_r, o_q=o_q, o_g=o_g, o_m=o_m, qscale=qscale),
        grid=(d // tr,),
        in_specs=[pl.BlockSpec((n, tr), lambda i: (0, i)),
                  pl.BlockSpec(up.shape, lambda i: (0, 0)),
                  pl.BlockSpec((1, ch, ch), lambda i: (i, 0, 0)),
                  pl.BlockSpec((1, 1, ch), lambda i: (i, 0, 0)),
                  pl.BlockSpec((ch, dm), lambda i: (i, 0))],
        out_specs=[pl.BlockSpec((tr, n1), lambda i: (i, 0)),
                   pl.BlockSpec((tr, n - o_m), lambda i: (i, 0)),
                   pl.BlockSpec((ch, dm), lambda i: (i, 0))],
        out_shape=[jax.ShapeDtypeStruct((d, n1), BF16), jax.ShapeDtypeStruct((d, n - o_m), BF16),
                   jax.ShapeDtypeStruct((groups * ch, dm), BF16)],
        compiler_params=_cparams(("parallel",)),
        name="prep",
    )(w_in_t, up, pool_w, pool_scale3, w_pool_proj)


def _modulated_norm(x, mult, shift):
    ms = jnp.mean(x * x, axis=-1, keepdims=True)
    return (x * lax.rsqrt(ms + EPS)) * mult + shift


def _log_sigmoid(z):
    return jnp.minimum(z, 0.0) - jnp.log1p(jnp.exp(-jnp.abs(z)))


def _inproj_body(x_ref, mult_ref, shift_ref, w_ref, bz_ref,
                 k_ref, v_ref, q_ref, laf_ref, lab_ref, sg_ref, p_ref, *, dk, dv, pw):
    def sub_tile(rows):
        h = _modulated_norm(x_ref[rows, :], mult_ref[0], shift_ref[0]).astype(BF16)
        yield
        u = jnp.dot(h, w_ref[...], preferred_element_type=F32)
        yield
        o = 0
        k_ref[rows, :] = u[:, o:o + dk].astype(BF16); o += dk
        v_ref[rows, :] = u[:, o:o + dv].astype(BF16); o += dv
        q_ref[rows, :] = u[:, o:o + dk].astype(BF16); o += dk
        zf = u[:, o:o + dk] + bz_ref[0:1, :]; o += dk
        zb = u[:, o:o + dk] + bz_ref[1:2, :]; o += dk
        laf_ref[rows, :] = _log_sigmoid(zf) * (1.0 / GATE_NORMALIZER)
        lab_ref[rows, :] = _log_sigmoid(zb) * (1.0 / GATE_NORMALIZER)
        sg_ref[rows, :] = _silu(u[:, o:o + dv]).astype(BF16); o += dv
        p_ref[rows, :] = u[:, o:o + pw].astype(BF16)
        yield

    _staggered([sub_tile(r) for r in _row_groups(x_ref.shape[0])], 3)


def _inproj(x2, mods, cond_row, w, bz, dk, dv, pw, tm):
    n, d = x2.shape
    row = lambda i: (i, 0)
    sample_of = lambda i, *_: cond_row(i)
    outs = [(dk, BF16), (dv, BF16), (dk, BF16), (dk, F32), (dk, F32), (dv, BF16), (pw, BF16)]
    return pl.pallas_call(
        functools.partial(_inproj_body, dk=dk, dv=dv, pw=pw),
        grid=(n // tm,),
        in_specs=[pl.BlockSpec((tm, d), row),
                  _mod_spec(d, MOD_MULT1, sample_of),
                  _mod_spec(d, MOD_SHIFT1, sample_of),
                  pl.BlockSpec(w.shape, lambda i: (0, 0), pipeline_mode=pl.Buffered(1)),
                  pl.BlockSpec(bz.shape, lambda i: (0, 0))],
        out_specs=[pl.BlockSpec((tm, c), row) for c, _ in outs],
        out_shape=[jax.ShapeDtypeStruct((n, c), t) for c, t in outs],
        compiler_params=_cparams(("parallel",)),
        name="inproj",
    )(x2, mods, mods, w, bz)


def _gla_direction(k_ref, q_ref, v_ref, la_ref, o_ref, s_ref, reverse, nchunk):
    C = GLA_CHUNK
    hd = HEAD_DK
    cb = MXU_DIM
    lt = nchunk * C
    la = la_ref[0]
    r = lax.broadcasted_iota(I32, (cb, cb), 0)
    c = lax.broadcasted_iota(I32, (cb, cb), 1)
    same = (r // C) == (c // C)
    cum = jnp.where(same & ((c >= r) if reverse else (c <= r)), 1.0, 0.0).astype(BF16)
    la_hi = la.astype(BF16)
    la_lo = (la - la_hi.astype(F32)).astype(BF16)
    la2 = jnp.concatenate([la_hi, la_lo], axis=1)
    bcs = []
    for blk in range(lt // cb):
        part = jnp.dot(cum, la2[blk * cb:(blk + 1) * cb], preferred_element_type=F32)
        bcs.append(part[:, :LANES] + part[:, LANES:])
    ri = lax.broadcasted_iota(I32, (C, 2 * C), 0)
    ci = lax.broadcasted_iota(I32, (C, 2 * C), 1) % C
    tri = (ci >= ri) if reverse else (ci <= ri)
    lane = lax.broadcasted_iota(I32, (1, LANES), 1)
    m0 = (lane < hd).astype(F32)
    m1 = (lane >= hd).astype(F32)
    sr = lax.broadcasted_iota(I32, (LANES, 2 * LANES), 0)
    sl = lax.broadcasted_iota(I32, (LANES, 2 * LANES), 1)
    smask = ((sr < hd) == (sl < LANES)).astype(F32)
    kt = k_ref[0].astype(F32)
    qt = q_ref[0].astype(F32)
    zero_v = jnp.zeros((C, LANES), BF16)
    sweep = list(range(nchunk - 1, -1, -1) if reverse else range(nchunk))
    for first in range(0, nchunk, GLA_BATCH):
        order = sweep[first:first + GLA_BATCH]
        intra, qds, kvs, decs, q2s, kss, kws, scs = {}, {}, {}, {}, {}, {}, {}, {}
        for ch in order:
            lo = ch * C
            b = bcs[lo // cb][lo % cb:lo % cb + C]
            last = b[0:1] if reverse else b[C - 1:C]
            mid = b[C // 2:C // 2 + 1] if reverse else b[C // 2 - 1:C // 2]
            kc = kt[lo:lo + C]
            qc = qt[lo:lo + C]
            q2s[ch] = (qc * jnp.exp(b - mid)).astype(BF16)
            ks = kc * jnp.exp(mid - b)
            kss[ch] = jnp.concatenate([ks * m0, ks * m1], axis=0).astype(BF16)
            qds[ch] = (qc * jnp.exp(b)).astype(BF16)
            kws[ch] = (kc * jnp.exp(last - b)).astype(BF16)
            decs[ch] = last
        yield
        for ch in order:
            scs[ch] = lax.dot_general(q2s[ch], kss[ch], _NT, preferred_element_type=F32)
        yield
        for ch in order:
            v2 = v_ref[0, ch * C:(ch + 1) * C, :]
            kvs[ch] = lax.dot_general(kws[ch], v2, _TN, preferred_element_type=F32) * smask
        yield
        for ch in order:
            sc2 = jnp.where(tri, scs[ch], 0.0).astype(BF16)
            v2 = v_ref[0, ch * C:(ch + 1) * C, :]
            vbd = jnp.concatenate([jnp.concatenate([v2[:, :LANES], zero_v], axis=1),
                                   jnp.concatenate([zero_v, v2[:, LANES:]], axis=1)], axis=0)
            intra[ch] = jnp.dot(sc2, vbd, preferred_element_type=F32)
        yield
        pad = jnp.zeros((LANES - len(order), LANES), F32)
        dec_cols = jnp.exp(jnp.concatenate([decs[ch] for ch in order] + [pad], axis=0).T)
        st = s_ref[...]
        starts = {}
        for i, ch in enumerate(order):
            starts[ch] = st.astype(BF16)
            st = st * dec_cols[:, i:i + 1] + kvs[ch]
        s_ref[...] = st
        yield
        for ch in order:
            inter = jnp.dot(qds[ch], starts[ch], preferred_element_type=F32)
            o_ref[0, ch * C:(ch + 1) * C, :] = inter + intra[ch]
        yield


def _gla_body(kf, qf, vf, laf, kb, qb, vb, lab, h0f, h0b, of, ob, hf_out, hb_out, sf, sb, *, nchunk):
    i = pl.program_id(2)

    @pl.when(i == 0)
    def _():
        sf[...] = h0f[0, 0]
        sb[...] = h0b[0, 0]

    sweeps = [_gla_direction(kf, qf, vf, laf, of, sf, False, nchunk),
              _gla_direction(kb, qb, vb, lab, ob, sb, True, nchunk)]
    for _ in range(GLA_STAGES * pl.cdiv(nchunk, GLA_BATCH)):
        for sweep in sweeps:
            next(sweep)

    @pl.when(i == pl.num_programs(2) - 1)
    def _():
        hf_out[0, 0] = sf[...]
        hb_out[0, 0] = sb[...]


def _gla(k, q, v, laf, lab, h0f, h0b, lt):
    bsz, l, _ = k.shape
    pairs = h0f.shape[1]
    nt = l // lt
    fwd = lambda b, hp, i: (b, i, hp)
    bwd = lambda b, hp, i: (b, nt - 1 - i, hp)
    st = lambda b, hp, i: (b, hp, 0, 0)
    kq = lambda m: pl.BlockSpec((1, lt, LANES), m)
    vv = lambda m: pl.BlockSpec((1, lt, 2 * LANES), m)
    sspec = pl.BlockSpec((1, 1, LANES, 2 * LANES), st)
    return pl.pallas_call(
        functools.partial(_gla_body, nchunk=lt // GLA_CHUNK),
        grid=(bsz, pairs, nt),
        in_specs=[kq(fwd), kq(fwd), vv(fwd), kq(fwd), kq(bwd), kq(bwd), vv(bwd), kq(bwd), sspec, sspec],
        out_specs=[vv(fwd), vv(bwd), sspec, sspec],
        out_shape=[jax.ShapeDtypeStruct(v.shape, F32), jax.ShapeDtypeStruct(v.shape, F32),
                   jax.ShapeDtypeStruct(h0f.shape, F32), jax.ShapeDtypeStruct(h0b.shape, F32)],
        scratch_shapes=[pltpu.VMEM((LANES, 2 * LANES), F32), pltpu.VMEM((LANES, 2 * LANES), F32)],
        compiler_params=_cparams(("parallel", "parallel", "arbitrary")),
        name="gla",
    )(k, q, v, laf, k, q, v, lab, h0f, h0b)


def _pool_body(p_ref, o_ref, s1_ref, *, half, rows):
    xb = p_ref[0]
    l, ch = xb.shape

    def inv_counts(pos, n):
        return 1.0 / (jnp.minimum(pos + half, n) - jnp.maximum(pos - half, 0)).astype(F32)

    blk = MXU_DIM
    r = lax.broadcasted_iota(I32, (blk, blk), 0)
    c = lax.broadcasted_iota(I32, (blk, blk), 1)
    band = ((r // GRID_W == c // GRID_W) & (c - r >= -half) & (c - r <= half - 1)).astype(BF16)
    for i in range(0, l // blk, 2):
        pair = jnp.concatenate([xb[i * blk:(i + 1) * blk], xb[(i + 1) * blk:(i + 2) * blk]], axis=1)
        sums = jnp.dot(band, pair, preferred_element_type=F32)
        s1_ref[i * blk:(i + 1) * blk, :] = sums[:, :ch]
        s1_ref[(i + 1) * blk:(i + 2) * blk, :] = sums[:, ch:]
    s1 = s1_ref[...].reshape(rows, GRID_W, ch)

    def shifted(a, s):
        z = jnp.zeros((abs(s),) + a.shape[1:], a.dtype)
        return jnp.concatenate([a[s:], z], axis=0) if s > 0 else jnp.concatenate([z, a[:s]], axis=0)

    fwd = s1
    bwd = shifted(s1, -1)
    s = 1
    while s < half:
        fwd = fwd + shifted(fwd, s)
        bwd = bwd + shifted(bwd, -s)
        s *= 2
    inv_r = inv_counts(lax.broadcasted_iota(I32, (rows, 1, ch), 0), rows)
    inv_c = inv_counts(lax.broadcasted_iota(I32, (1, GRID_W, ch), 1), GRID_W)
    pooled = (fwd + bwd) * inv_r * inv_c - xb.astype(F32).reshape(rows, GRID_W, ch)
    o_ref[0] = pooled.reshape(l, ch).astype(BF16)


def _pool_groups_body(p_ref, o_ref, s1_ref, *, rows):
    for gi, window in enumerate(POOL_WINDOWS):
        @pl.when(pl.program_id(1) == gi)
        def _(half=window // 2):
            _pool_body(p_ref, o_ref, s1_ref, half=half, rows=rows)


def _pool(pin, ch):
    bsz, l, pw = pin.shape
    return pl.pallas_call(
        functools.partial(_pool_groups_body, rows=l // GRID_W),
        grid=(bsz, pw // ch),
        in_specs=[pl.BlockSpec((1, l, ch), lambda b, g: (b, 0, g))],
        out_specs=pl.BlockSpec((1, l, ch), lambda b, g: (b, 0, g)),
        out_shape=jax.ShapeDtypeStruct((bsz, l, pw), BF16),
        scratch_shapes=[pltpu.VMEM((l, ch), F32)],
        compiler_params=_cparams(("parallel", "parallel")),
        name="pool",
    )(pin)


def _merge_body(x_ref, m1_ref, s1_ref, g1_ref, m2_ref, s2_ref,
                of_ref, ob_ref, sg_ref, mx_ref,
                wm_ref, wgla_ref, wpool_ref, wout_ref, gn_ref, wr_ref,
                x1_ref, h2_ref, aff_ref, *, heads, ne):
    d = x_ref.shape[1]

    def sub_tile(rows):
        bp = jnp.dot(mx_ref[rows, :], wpool_ref[...], preferred_element_type=F32)
        x = x_ref[rows, :]
        h = _modulated_norm(x, m1_ref[0], s1_ref[0]).astype(BF16)
        o = of_ref[rows, :] + ob_ref[rows, :]
        sg = sg_ref[rows, :].astype(F32)
        og = []
        for j in range(heads):
            oj = o[:, j * LANES:(j + 1) * LANES]
            oj = oj * lax.rsqrt(jnp.mean(oj * oj, axis=-1, keepdims=True) + EPS) * gn_ref[...]
            og.append((oj * sg[:, j * LANES:(j + 1) * LANES]).astype(BF16))
        og = jnp.concatenate(og, axis=1)
        yield
        gates = jnp.dot(h, wm_ref[...], preferred_element_type=F32)
        bg = jnp.dot(og, wgla_ref[...], preferred_element_type=F32)
        yield
        gates = _sigmoid(gates)
        z = (gates[:, :d] * bg + gates[:, d:] * bp).astype(BF16)
        yield
        y = jnp.dot(z, wout_ref[...], preferred_element_type=F32)
        yield
        x1 = x + g1_ref[0] * y
        x1_ref[rows, :] = x1
        h2 = _modulated_norm(x1, m2_ref[0], s2_ref[0])
        hi = h2.astype(BF16)
        h2_ref[rows, :] = hi
        lo = (h2 - hi.astype(F32)).astype(BF16)
        yield
        lg = (jnp.dot(hi, wr_ref[...], preferred_element_type=F32)
              + jnp.dot(lo, wr_ref[...], preferred_element_type=F32))
        yield
        lgt = lg.T
        logit = lgt[0:ne] + lgt[ne:2 * ne]
        mx = jnp.max(logit, axis=0, keepdims=True)
        ex = jnp.exp(logit - mx)
        aff_ref[0, :, rows] = ex / jnp.sum(ex, axis=0, keepdims=True)
        yield

    _staggered([sub_tile(r) for r in _row_groups(x_ref.shape[0])], 7)


def _merge(x2, mods, of, ob, sg, mixed, wm, wgla, wpool, wout, gn, wr, rows_per_sample, heads, ne, tm):
    n, d = x2.shape
    bsz = n // rows_per_sample
    tps = rows_per_sample // tm
    row = lambda i: (i, 0)
    sample_of = lambda i, *_: i // tps
    vecs = [MOD_MULT1, MOD_SHIFT1, MOD_GATE1, MOD_MULT2, MOD_SHIFT2]
    full = lambda a: pl.BlockSpec(a.shape, lambda i: (0,) * a.ndim, pipeline_mode=pl.Buffered(1))
    gv = of.shape[-1]
    return pl.pallas_call(
        functools.partial(_merge_body, heads=heads, ne=ne),
        grid=(n // tm,),
        in_specs=[pl.BlockSpec((tm, d), row)] + [_mod_spec(d, v, sample_of) for v in vecs]
                 + [pl.BlockSpec((tm, gv), row)] * 4
                 + [full(wm), full(wgla), full(wpool), full(wout), full(gn), full(wr)],
        out_specs=[pl.BlockSpec((tm, d), row),
                   pl.BlockSpec((tm, d), row),
                   pl.BlockSpec((1, ne, tm), lambda i: (i // tps, 0, i % tps))],
        out_shape=[jax.ShapeDtypeStruct((n, d), F32),
                   jax.ShapeDtypeStruct((n, d), BF16),
                   jax.ShapeDtypeStruct((bsz, ne, rows_per_sample), F32)],
        compiler_params=_cparams(("parallel",)),
        name="merge",
    )(x2, *[mods] * len(vecs), of, ob, sg, mixed, wm, wgla, wpool, wout, gn, wr)


def _route_body(aff_ref, pos_ref, off_ref, cnt_ref, *, cap, ntb):
    a = aff_ref[0]
    ne, l = a.shape
    blk = ROUTE_BLK

    def bisect(i, v):
        cand = v | jnp.left_shift(jnp.int32(1), 30 - i)
        cnt = jnp.sum((a >= lax.bitcast_convert_type(cand, F32)).astype(F32), axis=1, keepdims=True)
        return jnp.where(cnt >= cap, cand, v)

    thr = lax.bitcast_convert_type(lax.fori_loop(0, 31, bisect, jnp.zeros((ne, 1), I32)), F32)
    gt = a > thr
    tie = a == thr
    need = cap - jnp.sum(gt.astype(F32), axis=1, keepdims=True)

    r = lax.broadcasted_iota(I32, (blk, blk), 0)
    c = lax.broadcasted_iota(I32, (blk, blk), 1)
    upper = (r <= c).astype(BF16)
    lane = lax.broadcasted_iota(I32, (1, LANES), 1)

    def prefix(mask_f):
        run = jnp.zeros((ne, 1), F32)
        offs = jnp.zeros((ne, LANES), F32)
        for tb in range(ntb):
            m = mask_f[:, tb * blk:(tb + 1) * blk].astype(BF16)
            loc = jnp.dot(m, upper, preferred_element_type=F32)
            cnt_ref[:, tb * blk:(tb + 1) * blk] = loc + run
            offs = jnp.where(lane == tb, run, offs)
            run = run + loc[:, blk - 1:blk]
        return jnp.where(lane >= ntb, run, offs)

    tie_f = tie.astype(F32)
    prefix(tie_f)
    tie_excl = cnt_ref[...] - tie_f
    sel = gt | (tie & (tie_excl < need))
    offs = prefix(sel.astype(F32))
    pos_ref[0] = jnp.where(sel, cnt_ref[...] - 1.0, -1.0).astype(I32)
    off_ref[0] = offs.astype(I32)


def _route(aff, cap):
    bsz, ne, l = aff.shape
    spec = lambda s: pl.BlockSpec((1,) + s, lambda b: (b, 0, 0))
    return pl.pallas_call(
        functools.partial(_route_body, cap=cap, ntb=l // ROUTE_BLK),
        grid=(bsz,),
        in_specs=[spec((ne, l))],
        out_specs=[spec((ne, l)), spec((ne, LANES))],
        out_shape=[jax.ShapeDtypeStruct((bsz, ne, l), I32),
                   jax.ShapeDtypeStruct((bsz, ne, LANES), I32)],
        scratch_shapes=[pltpu.VMEM((ne, l), F32)],
        compiler_params=_cparams(("parallel",)),
        name="route",
    )(aff)


def _window_plan(off_ref, bb, tt, experts, ne):
    lows = [off_ref[(bb * ne + e) * LANES + tt] & -BF16_ROWS for e in experts]
    ends = [off_ref[(bb * ne + e) * LANES + tt + 1] for e in experts]
    return lows, ends


def _window_rounds(lows, ends):
    rounds = jnp.int32(0)
    for lo, hi in zip(lows, ends):
        rounds = jnp.maximum(rounds, lax.div(hi - lo + (SLOT_WIN - 1), jnp.int32(SLOT_WIN)))
    return rounds


def _gatherx_body(off_ref, h_ref, pos_ref, xs_ref, *, ne, cap, eg):
    b = pl.program_id(0)
    g = pl.program_id(1)
    tb = pl.program_id(2)
    win = SLOT_WIN

    @pl.when(tb == 0)
    def _():
        xs_ref[...] = jnp.zeros(xs_ref.shape, BF16)

    j_col = lax.broadcasted_iota(I32, (win, 1), 0)
    t = ROUTE_BLK
    nsub = h_ref.shape[0] // t
    experts = [g * eg + k for k in range(eg)]

    def select(sub, lows, r):
        starts = [pl.multiple_of(jnp.minimum(lows[k] + r * win, cap - win), BF16_ROWS) for k in range(eg)]
        pieces = []
        for k in range(eg):
            p = pos_ref[0, k:k + 1, sub * t:(sub + 1) * t]
            hit = (p - starts[k] == j_col) & (p >= lows[k] + r * win)
            pieces.append(jnp.where(hit, 1.0, 0.0).astype(BF16))
        sel = jnp.concatenate(pieces, axis=0)
        rows = jnp.dot(sel, h_ref[sub * t:(sub + 1) * t, :], preferred_element_type=F32).astype(BF16)
        return starts, rows

    def deposit(starts, rows):
        for k in range(eg):
            dst = (0, k, pl.ds(starts[k], win), slice(None))
            xs_ref[dst] = xs_ref[dst] + rows[k * win:(k + 1) * win]

    plans = [_window_plan(off_ref, b, tb * nsub + sub, experts, ne) for sub in range(nsub)]
    firsts = [select(sub, plans[sub][0], 0) for sub in range(nsub)]
    for starts, rows in firsts:
        deposit(starts, rows)
    for sub in range(nsub):
        lows, ends = plans[sub]

        def extra_round(r, carry, lows=lows, sub=sub):
            deposit(*select(sub, lows, r))
            return carry

        lax.fori_loop(1, _window_rounds(lows, ends), extra_round, 0)


def _gatherx(off_flat, h2, pos, cap):
    n, d = h2.shape
    bsz, ne, l = pos.shape
    t = min(l, 2 * TOKEN_TILE)
    ntb = l // t
    eg = SUBLANES
    grid_spec = pltpu.PrefetchScalarGridSpec(
        num_scalar_prefetch=1,
        grid=(bsz, ne // eg, ntb),
        in_specs=[pl.BlockSpec((t, d), lambda b, g, i, off: (b * ntb + i, 0)),
                  pl.BlockSpec((1, eg, t), lambda b, g, i, off: (b, g, i))],
        out_specs=pl.BlockSpec((1, eg, cap, d), lambda b, g, i, off: (b, g, 0, 0)),
    )
    return pl.pallas_call(
        functools.partial(_gatherx_body, ne=ne, cap=cap, eg=eg),
        grid_spec=grid_spec,
        out_shape=jax.ShapeDtypeStruct((bsz, ne, cap, d), BF16),
        compiler_params=_cparams(("parallel", "parallel", "arbitrary")),
        name="gatherx",
    )(off_flat, h2, pos)


def _moe_body(xs_ref, wg_ref, wu_ref, wd_ref, y_ref, *, rc):
    wg = wg_ref[0].astype(BF16)
    wu = wu_ref[0].astype(BF16)
    wd = wd_ref[0].astype(BF16)
    cap = xs_ref.shape[2]
    acts = []
    for ch in range(cap // rc):
        xs = xs_ref[0, 0, ch * rc:(ch + 1) * rc, :]
        gate = jnp.dot(xs, wg, preferred_element_type=F32)
        up = jnp.dot(xs, wu, preferred_element_type=F32)
        acts.append((_silu(gate) * up).astype(BF16))
    for ch in range(cap // rc):
        y_ref[0, 0, ch * rc:(ch + 1) * rc, :] = jnp.dot(acts[ch], wd, preferred_element_type=F32).astype(BF16)


def _moe(xs, wg, wu, wd):
    bsz, ne, cap, d = xs.shape
    de = wg.shape[2]
    slot = pl.BlockSpec((1, 1, cap, d), lambda e, b: (b, e, 0, 0))
    return pl.pallas_call(
        functools.partial(_moe_body, rc=min(cap, 2 * MXU_DIM)),
        grid=(ne, bsz),
        in_specs=[slot,
                  pl.BlockSpec((1, d, de), lambda e, b: (e, 0, 0)),
                  pl.BlockSpec((1, d, de), lambda e, b: (e, 0, 0)),
                  pl.BlockSpec((1, de, d), lambda e, b: (e, 0, 0))],
        out_specs=slot,
        out_shape=jax.ShapeDtypeStruct((bsz, ne, cap, d), BF16),
        compiler_params=_cparams(("parallel", "arbitrary")),
        name="moe",
    )(xs, wg, wu, wd)


def _combine_body(off_ref, x1_ref, g2_ref, fg_ref, pos_ref, aff_ref, y_hbm, o_ref, ybuf, spare, acc_ref, sem,
                  spare_sem, *, ne, cap, nsp, nsub, span):
    b = pl.program_id(0)
    i = pl.program_id(1)
    step = b * nsp + i
    nsteps = pl.num_programs(0) * nsp
    slot = step % 2
    win = SLOT_WIN
    t = ROUTE_BLK
    experts = range(ne)

    def window_starts(lows, r):
        return [pl.multiple_of(jnp.minimum(lows[e] + r * win, cap - win), BF16_ROWS) for e in experts]

    def span_starts(bb, ii):
        lows = _window_plan(off_ref, bb, ii * nsub, experts, ne)[0]
        return [pl.multiple_of(jnp.minimum(lows[e], cap - span), BF16_ROWS) for e in experts]

    def span_copy(bb, e, start, buf):
        return pltpu.make_async_copy(y_hbm.at[bb, e, pl.ds(start, span), :], ybuf.at[buf, e], sem.at[buf, e])

    def spare_copy(e, start):
        return pltpu.make_async_copy(y_hbm.at[b, e, pl.ds(start, win), :],
                                     spare.at[pl.ds(e * win, win), :], spare_sem.at[e])

    def start_step(bb, ii, buf):
        starts = span_starts(bb, ii)
        for e in experts:
            span_copy(bb, e, starts[e], buf).start()

    @pl.when(step == 0)
    def _():
        start_step(b, i, 0)

    nxt = jnp.minimum(step + 1, nsteps - 1)
    start_step(lax.div(nxt, jnp.int32(nsp)), lax.rem(nxt, jnp.int32(nsp)), 1 - slot)

    plans = [_window_plan(off_ref, b, i * nsub + sub, experts, ne) for sub in range(nsub)]
    mine = span_starts(b, i)
    firsts = [window_starts(plans[sub][0], 0) for sub in range(nsub)]
    inside = []
    for sub in range(nsub):
        ok = jnp.bool_(True)
        for e in experts:
            ok = ok & (firsts[sub][e] >= mine[e]) & (firsts[sub][e] + win <= mine[e] + span)
        inside.append(ok)
    j_col = lax.broadcasted_iota(I32, (win, 1), 0)

    def expand(sub, r, starts, rows, enabled=None):
        lows = plans[sub][0]
        pieces = []
        for e in experts:
            p = pos_ref[0, e:e + 1, sub * t:(sub + 1) * t]
            valid = p >= lows[e] + r * win
            hit = (p - starts[e] == j_col) & valid
            if enabled is not None:
                hit = hit & enabled
            pieces.append(jnp.where(hit, aff_ref[0, e:e + 1, sub * t:(sub + 1) * t], 0.0).astype(BF16))
        pmat = jnp.concatenate(pieces, axis=0)
        return lax.dot_general(pmat, rows, _TN, preferred_element_type=F32)

    acc_ref[...] = jnp.zeros(acc_ref.shape, F32)
    for sub in range(nsub):
        lows, ends = plans[sub]

        def own_round(r, carry, lows=lows, sub=sub):
            starts = window_starts(lows, r)
            for e in experts:
                spare_copy(e, starts[e]).start()
            for e in experts:
                spare_copy(e, starts[e]).wait()
            acc_ref[sub] += expand(sub, r, starts, spare[...])
            return carry

        lax.fori_loop(jnp.where(inside[sub], 1, 0), _window_rounds(lows, ends), own_round, 0)

    for e in experts:
        span_copy(b, e, mine[e], slot).wait()
    for sub in range(nsub):
        rows = pl.ds(sub * t, t)
        picked = []
        for e in experts:
            local = pl.multiple_of(jnp.clip(firsts[sub][e] - mine[e], 0, span - win), BF16_ROWS)
            picked.append(ybuf[slot, e, pl.ds(local, win), :])
        moe = expand(sub, 0, firsts[sub], jnp.concatenate(picked, axis=0), inside[sub]) + acc_ref[sub]
        x2 = x1_ref[rows, :] + g2_ref[0] * moe
        ms = jnp.mean(x2 * x2, axis=-1, keepdims=True)
        o_ref[rows, :] = x2 * lax.rsqrt(ms + EPS) * fg_ref[...]

    @pl.when(step == nsteps - 1)
    def _():
        for e in experts:
            span_copy(b, e, mine[e], 1 - slot).wait()


def _combine(off_flat, x1, mods, fg, pos, aff, y, rows_per_sample):
    n, d = x1.shape
    bsz, ne, cap, _ = y.shape
    t = min(rows_per_sample, TOKEN_TILE)
    nsub = t // ROUTE_BLK
    nsp = rows_per_sample // t
    rows = ne * SLOT_WIN
    span = min(cap, nsub * SLOT_WIN * 3 // 4)
    grid_spec = pltpu.PrefetchScalarGridSpec(
        num_scalar_prefetch=1,
        grid=(bsz, nsp),
        in_specs=[pl.BlockSpec((t, d), lambda b, i, off: (b * nsp + i, 0)),
                  _mod_spec(d, MOD_GATE2, lambda b, i: b),
                  pl.BlockSpec((1, d), lambda b, i, off: (0, 0)),
                  pl.BlockSpec((1, ne, t), lambda b, i, off: (b, 0, i)),
                  pl.BlockSpec((1, ne, t), lambda b, i, off: (b, 0, i)),
                  pl.BlockSpec(memory_space=pl.ANY)],
        out_specs=pl.BlockSpec((t, d), lambda b, i, off: (b * nsp + i, 0)),
        scratch_shapes=[pltpu.VMEM((2, ne, span, d), BF16),
                        pltpu.VMEM((rows, d), BF16),
                        pltpu.VMEM((nsub, ROUTE_BLK, d), F32),
                        pltpu.SemaphoreType.DMA((2, ne)),
                        pltpu.SemaphoreType.DMA((ne,))],
    )
    return pl.pallas_call(
        functools.partial(_combine_body, ne=ne, cap=cap, nsp=nsp, nsub=nsub, span=span),
        grid_spec=grid_spec,
        out_shape=jax.ShapeDtypeStruct((n, d), F32),
        compiler_params=_cparams(("arbitrary", "arbitrary")),
        name="combine",
    )(off_flat, x1, mods, fg, pos, aff, y)


def kernel(x, c, ctx, c_ctx, ada_w, ada_b, norm1_g, norm2_g, w_in, w_decay_up, b_decay, gla_norm_g,
           w_gla_proj, pool_w, pool_scale, w_pool_proj, w_out, w_router, w_gate_e, w_up_e, w_down_e,
           final_norm_g):
    assert ada_w.shape[0] == 1, "single-layer block"
    bsz, l, d = x.shape
    lc = ctx.shape[1]
    rank, dk = w_decay_up.shape[2], w_decay_up.shape[3]
    dvh = gla_norm_g.shape[1]
    dv = w_gla_proj.shape[1]
    heads = dv // dvh
    groups, ch = pool_w.shape[1], pool_w.shape[2]
    pw = groups * ch
    ne = w_router.shape[2]
    cap = EC_CAPACITY * l // ne
    assert dk // heads == HEAD_DK and dvh == LANES and ch == LANES and heads % 2 == 0
    assert l % ROUTE_BLK == 0 and l // ROUTE_BLK < LANES and cap >= SLOT_WIN and cap % BF16_ROWS == 0
    assert GRID_W == GLA_CHUNK and MXU_DIM % GRID_W == 0 and (l // MXU_DIM) % 2 == 0

    assert bsz < COND_ROWS
    cin = jnp.zeros((COND_ROWS, d), F32).at[:bsz].set(c).at[bsz].set(c_ctx)
    mods = _ada(cin, ada_w[0], ada_b[0][None, :], jnp.concatenate([norm1_g, norm2_g], axis=0))

    o_r = dk + dv
    o_q = o_r + 2 * rank
    o_g = o_q + dk
    o_p = o_g + dv
    o_m = o_p + pw
    up = jnp.zeros((2 * rank, 2 * dk), F32)
    up = up.at[:rank, :dk].set(w_decay_up[0, 0]).at[rank:, dk:].set(w_decay_up[0, 1])
    psc = pool_scale[0].reshape(groups, 1, ch)
    w1, wm, wpool = _prep(w_in[0].T, up, pool_w[0], psc, w_pool_proj[0], o_r, o_q, o_g, o_m,
                          float(dk // heads) ** -0.5)
    bz = b_decay[0]

    zero_state = jnp.zeros((bsz, heads // 2, LANES, 2 * LANES), F32)
    ck, cv, _, claf, clab, _, _ = _inproj(ctx.reshape(bsz * lc, d), mods, lambda i: bsz, w1, bz, dk, dv, pw,
                                          min(lc, CTX_TILE))
    r3 = lambda a, n: a.reshape(bsz, n, a.shape[-1])
    _, _, h_f, h_b = _gla(r3(ck, lc), r3(ck, lc), r3(cv, lc), r3(claf, lc), r3(clab, lc),
                          zero_state, zero_state, min(lc, CTX_TILE))

    x2 = x.reshape(bsz * l, d)
    tm = min(l, TOKEN_TILE)
    k, v, q, laf, lab, sg, pin = _inproj(x2, mods, lambda i: i // (l // tm), w1, bz, dk, dv, pw, tm)
    of, ob, _, _ = _gla(r3(k, l), r3(q, l), r3(v, l), r3(laf, l), r3(lab, l), h_f, h_b, min(l, GLA_TILE))
    pooled = _pool(r3(pin, l), ch).reshape(bsz * l, pw)

    wr_hi = w_router[0].astype(BF16)
    wr_lo = (w_router[0] - wr_hi.astype(F32)).astype(BF16)
    wr = jnp.zeros((d, LANES), BF16).at[:, :ne].set(wr_hi).at[:, ne:2 * ne].set(wr_lo)
    x1, h2, aff = _merge(x2, mods, of.reshape(bsz * l, dv), ob.reshape(bsz * l, dv), sg, pooled,
                          wm, w_gla_proj[0].astype(BF16), wpool, w_out[0].astype(BF16),
                          gla_norm_g[0][None, :], wr, l, heads, ne, tm)

    pos, offs = _route(aff.reshape(1, bsz * ne, l), cap)
    pos = pos.reshape(bsz, ne, l)
    off_flat = offs.reshape(-1)
    xs = _gatherx(off_flat, h2, pos, cap)
    y = _moe(xs, w_gate_e[0], w_up_e[0], w_down_e[0])
    out = _combine(off_flat, x1, mods, final_norm_g[None, :], pos, aff, y, l)
    return out.reshape(bsz, l, d)
```

```python
import functools

import jax
import jax.numpy as jnp
from jax import lax
from jax.experimental import pallas as pl
from jax.experimental.pallas import tpu as pltpu

F32 = jnp.float32
BF16 = jnp.bfloat16
I32 = jnp.int32
HIGHEST = lax.Precision.HIGHEST

EPS = 1e-6
GRID_W = 64
GLA_CHUNK = 64
GLA_STAGES = 6
GLA_BATCH = 8
GATE_NORMALIZER = 16.0
POOL_WINDOWS = (2, 4, 8, 16)
EC_CAPACITY = 2

LANES = 128
SUBLANES = 8
BF16_ROWS = 16
MXU_DIM = 256
VMEM_BYTES = 64 * 1024 * 1024
VMEM_LIMIT = VMEM_BYTES * 7 // 8

HEAD_DK = LANES // 2
ROUTE_BLK = MXU_DIM
SLOT_WIN = 64
SUB_ROWS = MXU_DIM
TOKEN_TILE = 1024
GLA_TILE = 2048
CTX_TILE = 256
COND_ROWS = SUBLANES
MOD_SHIFT1, MOD_MULT1, MOD_GATE1, MOD_SHIFT2, MOD_MULT2, MOD_GATE2 = range(6)


def _mod_spec(d, which, sample_of):
    return pl.BlockSpec((1, 1, d), lambda *g: (which * COND_ROWS + sample_of(*g[:2]), 0, 0))

_NT = (((1,), (1,)), ((), ()))
_TN = (((0,), (0,)), ((), ()))


def _cparams(sem):
    return pltpu.CompilerParams(dimension_semantics=sem, vmem_limit_bytes=VMEM_LIMIT)


def _sigmoid(x):
    return 0.5 * jnp.tanh(0.5 * x) + 0.5


def _silu(x):
    return x * _sigmoid(x)


def _row_groups(n):
    step = min(n, SUB_ROWS)
    return [pl.ds(i, step) for i in range(0, n, step)]


def _staggered(gens, nstages):
    for t in range(nstages + len(gens) - 1):
        for g in reversed(range(len(gens))):
            if 0 <= t - g < nstages:
                next(gens[g])


def _ada_body(c_ref, w_ref, b_ref, g_ref, o_ref):
    s = _silu(c_ref[...])
    rows = s.shape[0]
    s_hi = s.astype(BF16)
    s_lo = (s - s_hi.astype(F32)).astype(BF16)
    w = w_ref[...]
    w_hi = w.astype(BF16)
    w_lo = (w - w_hi.astype(F32)).astype(BF16)
    both = jnp.dot(jnp.concatenate([s_hi, s_lo], axis=0), w_hi, preferred_element_type=F32)
    out = (both[:rows] + both[rows:] + jnp.dot(s_hi, w_lo, preferred_element_type=F32)) + b_ref[...]
    j = pl.program_id(0)
    gain = jnp.where(j == MOD_MULT1, g_ref[0:1, :], g_ref[1:2, :])
    o_ref[...] = jnp.where((j == MOD_MULT1) | (j == MOD_MULT2), gain * (1.0 + out), out)[:, None, :]


def _ada(cin, w, b, gains):
    rows, d = cin.shape
    nvec = w.shape[1] // d
    return pl.pallas_call(
        _ada_body,
        grid=(nvec,),
        in_specs=[pl.BlockSpec((rows, d), lambda j: (0, 0)),
                  pl.BlockSpec((d, d), lambda j: (0, j)),
                  pl.BlockSpec((1, d), lambda j: (0, j)),
                  pl.BlockSpec(gains.shape, lambda j: (0, 0))],
        out_specs=pl.BlockSpec((rows, 1, d), lambda j: (j, 0, 0)),
        out_shape=jax.ShapeDtypeStruct((nvec * rows, 1, d), F32),
        compiler_params=_cparams(("arbitrary",)),
        name="ada",
    )(cin, w, b, gains)


def _prep_body(w_ref, up_ref, pw_ref, ps_ref, wpp_ref, w1_ref, wm_ref, wpf_ref, *, o_r, o_q, o_g, o_m, qscale):
    dk = o_g - o_q
    wz = lax.dot_general(w_ref[o_r:o_q, :], up_ref[...], _TN, precision=HIGHEST, preferred_element_type=F32)
    nz = wz.shape[1]
    w1_ref[:, :o_r] = w_ref[:o_r, :].T.astype(BF16)
    w1_ref[:, o_r:o_r + dk] = (w_ref[o_q:o_g, :].T * qscale).astype(BF16)
    w1_ref[:, o_r + dk:o_r + dk + nz] = wz.astype(BF16)
    w1_ref[:, o_r + dk + nz:] = w_ref[o_g:o_m, :].T.astype(BF16)
    wm_ref[...] = w_ref[o_m:, :].T.astype(BF16)
    wpf_ref[...] = jnp.dot(pw_ref[0] * ps_ref[0], wpp_ref[...], precision=HIGHEST,
                           preferred_element_type=F32).astype(BF16)


def _prep(w_in_t, up, pool_w, pool_scale3, w_pool_proj, o_r, o_q, o_g, o_m, qscale):
    n, d = w_in_t.shape
    n1 = o_r + (o_g - o_q) + up.shape[1] + (o_m - o_g)
    groups, ch, _ = pool_w.shape
    tr = d // groups
    dm = w_pool_proj.shape[1]
    return pl.pallas_call(
        functools.partial(_prep_body, o_r=o_r, o_q=o_q, o_g=o_g, o_m=o_m, qscale=qscale),
        grid=(d // tr,),
        in_specs=[pl.BlockSpec((n, tr), lambda i: (0, i)),
                  pl.BlockSpec(up.shape, lambda i: (0, 0)),
                  pl.BlockSpec((1, ch, ch), lambda i: (i, 0, 0)),
                  pl.BlockSpec((1, 1, ch), lambda i: (i, 0, 0)),
                  pl.BlockSpec((ch, dm), lambda i: (i, 0))],
        out_specs=[pl.BlockSpec((tr, n1), lambda i: (i, 0)),
                   pl.BlockSpec((tr, n - o_m), lambda i: (i, 0)),
                   pl.BlockSpec((ch, dm), lambda i: (i, 0))],
        out_shape=[jax.ShapeDtypeStruct((d, n1), BF16), jax.ShapeDtypeStruct((d, n - o_m), BF16),
                   jax.ShapeDtypeStruct((groups * ch, dm), BF16)],
        compiler_params=_cparams(("parallel",)),
        name="prep",
    )(w_in_t, up, pool_w, pool_scale3, w_pool_proj)


def _modulated_norm(x, mult, shift):
    ms = jnp.mean(x * x, axis=-1, keepdims=True)
    return (x * lax.rsqrt(ms + EPS)) * mult + shift


def _log_sigmoid(z):
    return jnp.minimum(z, 0.0) - jnp.log1p(jnp.exp(-jnp.abs(z)))


def _inproj_body(x_ref, mult_ref, shift_ref, w_ref, bz_ref,
                 k_ref, v_ref, q_ref, laf_ref, lab_ref, sg_ref, p_ref, *, dk, dv, pw):
    def sub_tile(rows):
        h = _modulated_norm(x_ref[rows, :], mult_ref[0], shift_ref[0]).astype(BF16)
        yield
        u = jnp.dot(h, w_ref[...], preferred_element_type=F32)
        yield
        o = 0
        k_ref[rows, :] = u[:, o:o + dk].astype(BF16); o += dk
        v_ref[rows, :] = u[:, o:o + dv].astype(BF16); o += dv
        q_ref[rows, :] = u[:, o:o + dk].astype(BF16); o += dk
        zf = u[:, o:o + dk] + bz_ref[0:1, :]; o += dk
        zb = u[:, o:o + dk] + bz_ref[1:2, :]; o += dk
        laf_ref[rows, :] = _log_sigmoid(zf) * (1.0 / GATE_NORMALIZER)
        lab_ref[rows, :] = _log_sigmoid(zb) * (1.0 / GATE_NORMALIZER)
        sg_ref[rows, :] = _silu(u[:, o:o + dv]).astype(BF16); o += dv
        p_ref[rows, :] = u[:, o:o + pw].astype(BF16)
        yield

    _staggered([sub_tile(r) for r in _row_groups(x_ref.shape[0])], 3)


def _inproj(x2, mods, cond_row, w, bz, dk, dv, pw, tm):
    n, d = x2.shape
    row = lambda i: (i, 0)
    sample_of = lambda i, *_: cond_row(i)
    outs = [(dk, BF16), (dv, BF16), (dk, BF16), (dk, F32), (dk, F32), (dv, BF16), (pw, BF16)]
    return pl.pallas_call(
        functools.partial(_inproj_body, dk=dk, dv=dv, pw=pw),
        grid=(n // tm,),
        in_specs=[pl.BlockSpec((tm, d), row),
                  _mod_spec(d, MOD_MULT1, sample_of),
                  _mod_spec(d, MOD_SHIFT1, sample_of),
                  pl.BlockSpec(w.shape, lambda i: (0, 0), pipeline_mode=pl.Buffered(1)),
                  pl.BlockSpec(bz.shape, lambda i: (0, 0))],
        out_specs=[pl.BlockSpec((tm, c), row) for c, _ in outs],
        out_shape=[jax.ShapeDtypeStruct((n, c), t) for c, t in outs],
        compiler_params=_cparams(("parallel",)),
        name="inproj",
    )(x2, mods, mods, w, bz)


def _gla_direction(k_ref, q_ref, v_ref, la_ref, o_ref, s_ref, reverse, nchunk):
    C = GLA_CHUNK
    hd = HEAD_DK
    cb = MXU_DIM
    lt = nchunk * C
    la = la_ref[0]
    r = lax.broadcasted_iota(I32, (cb, cb), 0)
    c = lax.broadcasted_iota(I32, (cb, cb), 1)
    same = (r // C) == (c // C)
    cum = jnp.where(same & ((c >= r) if reverse else (c <= r)), 1.0, 0.0).astype(BF16)
    la_hi = la.astype(BF16)
    la_lo = (la - la_hi.astype(F32)).astype(BF16)
    la2 = jnp.concatenate([la_hi, la_lo], axis=1)
    bcs = []
    for blk in range(lt // cb):
        part = jnp.dot(cum, la2[blk * cb:(blk + 1) * cb], preferred_element_type=F32)
        bcs.append(part[:, :LANES] + part[:, LANES:])
    ri = lax.broadcasted_iota(I32, (C, 2 * C), 0)
    ci = lax.broadcasted_iota(I32, (C, 2 * C), 1) % C
    tri = (ci >= ri) if reverse else (ci <= ri)
    lane = lax.broadcasted_iota(I32, (1, LANES), 1)
    m0 = (lane < hd).astype(F32)
    m1 = (lane >= hd).astype(F32)
    sr = lax.broadcasted_iota(I32, (LANES, 2 * LANES), 0)
    sl = lax.broadcasted_iota(I32, (LANES, 2 * LANES), 1)
    smask = ((sr < hd) == (sl < LANES)).astype(F32)
    kt = k_ref[0].astype(F32)
    qt = q_ref[0].astype(F32)
    zero_v = jnp.zeros((C, LANES), BF16)
    sweep = list(range(nchunk - 1, -1, -1) if reverse else range(nchunk))
    for first in range(0, nchunk, GLA_BATCH):
        order = sweep[first:first + GLA_BATCH]
        intra, qds, kvs, decs, q2s, kss, kws, scs = {}, {}, {}, {}, {}, {}, {}, {}
        for ch in order:
            lo = ch * C
            b = bcs[lo // cb][lo % cb:lo % cb + C]
            last = b[0:1] if reverse else b[C - 1:C]
            mid = b[C // 2:C // 2 + 1] if reverse else b[C // 2 - 1:C // 2]
            kc = kt[lo:lo + C]
            qc = qt[lo:lo + C]
            q2s[ch] = (qc * jnp.exp(b - mid)).astype(BF16)
            ks = kc * jnp.exp(mid - b)
            kss[ch] = jnp.concatenate([ks * m0, ks * m1], axis=0).astype(BF16)
            qds[ch] = (qc * jnp.exp(b)).astype(BF16)
            kws[ch] = (kc * jnp.exp(last - b)).astype(BF16)
            decs[ch] = last
        yield
        for ch in order:
            scs[ch] = lax.dot_general(q2s[ch], kss[ch], _NT, preferred_element_type=F32)
        yield
        for ch in order:
            v2 = v_ref[0, ch * C:(ch + 1) * C, :]
            kvs[ch] = lax.dot_general(kws[ch], v2, _TN, preferred_element_type=F32) * smask
        yield
        for ch in order:
            sc2 = jnp.where(tri, scs[ch], 0.0).astype(BF16)
            v2 = v_ref[0, ch * C:(ch + 1) * C, :]
            vbd = jnp.concatenate([jnp.concatenate([v2[:, :LANES], zero_v], axis=1),
                                   jnp.concatenate([zero_v, v2[:, LANES:]], axis=1)], axis=0)
            intra[ch] = jnp.dot(sc2, vbd, preferred_element_type=F32)
        yield
        pad = jnp.zeros((LANES - len(order), LANES), F32)
        dec_cols = jnp.exp(jnp.concatenate([decs[ch] for ch in order] + [pad], axis=0).T)
        st = s_ref[...]
        starts = {}
        for i, ch in enumerate(order):
            starts[ch] = st.astype(BF16)
            st = st * dec_cols[:, i:i + 1] + kvs[ch]
        s_ref[...] = st
        yield
        for ch in order:
            inter = jnp.dot(qds[ch], starts[ch], preferred_element_type=F32)
            o_ref[0, ch * C:(ch + 1) * C, :] = inter + intra[ch]
        yield


def _gla_body(kf, qf, vf, laf, kb, qb, vb, lab, h0f, h0b, of, ob, hf_out, hb_out, sf, sb, *, nchunk):
    i = pl.program_id(2)

    @pl.when(i == 0)
    def _():
        sf[...] = h0f[0, 0]
        sb[...] = h0b[0, 0]

    sweeps = [_gla_direction(kf, qf, vf, laf, of, sf, False, nchunk),
              _gla_direction(kb, qb, vb, lab, ob, sb, True, nchunk)]
    for _ in range(GLA_STAGES * pl.cdiv(nchunk, GLA_BATCH)):
        for sweep in sweeps:
            next(sweep)

    @pl.when(i == pl.num_programs(2) - 1)
    def _():
        hf_out[0, 0] = sf[...]
        hb_out[0, 0] = sb[...]


def _gla(k, q, v, laf, lab, h0f, h0b, lt):
    bsz, l, _ = k.shape
    pairs = h0f.shape[1]
    nt = l // lt
    fwd = lambda b, hp, i: (b, i, hp)
    bwd = lambda b, hp, i: (b, nt - 1 - i, hp)
    st = lambda b, hp, i: (b, hp, 0, 0)
    kq = lambda m: pl.BlockSpec((1, lt, LANES), m)
    vv = lambda m: pl.BlockSpec((1, lt, 2 * LANES), m)
    sspec = pl.BlockSpec((1, 1, LANES, 2 * LANES), st)
    return pl.pallas_call(
        functools.partial(_gla_body, nchunk=lt // GLA_CHUNK),
        grid=(bsz, pairs, nt),
        in_specs=[kq(fwd), kq(fwd), vv(fwd), kq(fwd), kq(bwd), kq(bwd), vv(bwd), kq(bwd), sspec, sspec],
        out_specs=[vv(fwd), vv(bwd), sspec, sspec],
        out_shape=[jax.ShapeDtypeStruct(v.shape, F32), jax.ShapeDtypeStruct(v.shape, F32),
                   jax.ShapeDtypeStruct(h0f.shape, F32), jax.ShapeDtypeStruct(h0b.shape, F32)],
        scratch_shapes=[pltpu.VMEM((LANES, 2 * LANES), F32), pltpu.VMEM((LANES, 2 * LANES), F32)],
        compiler_params=_cparams(("parallel", "parallel", "arbitrary")),
        name="gla",
    )(k, q, v, laf, k, q, v, lab, h0f, h0b)


def _pool_body(p_ref, o_ref, s1_ref, *, half, rows):
    xb = p_ref[0]
    l, ch = xb.shape

    def inv_counts(pos, n):
        return 1.0 / (jnp.minimum(pos + half, n) - jnp.maximum(pos - half, 0)).astype(F32)

    blk = MXU_DIM
    r = lax.broadcasted_iota(I32, (blk, blk), 0)
    c = lax.broadcasted_iota(I32, (blk, blk), 1)
    band = ((r // GRID_W == c // GRID_W) & (c - r >= -half) & (c - r <= half - 1)).astype(BF16)
    for i in range(0, l // blk, 2):
        pair = jnp.concatenate([xb[i * blk:(i + 1) * blk], xb[(i + 1) * blk:(i + 2) * blk]], axis=1)
        sums = jnp.dot(band, pair, preferred_element_type=F32)
        s1_ref[i * blk:(i + 1) * blk, :] = sums[:, :ch]
        s1_ref[(i + 1) * blk:(i + 2) * blk, :] = sums[:, ch:]
    s1 = s1_ref[...].reshape(rows, GRID_W, ch)

    def shifted(a, s):
        z = jnp.zeros((abs(s),) + a.shape[1:], a.dtype)
        return jnp.concatenate([a[s:], z], axis=0) if s > 0 else jnp.concatenate([z, a[:s]], axis=0)

    fwd = s1
    bwd = shifted(s1, -1)
    s = 1
    while s < half:
        fwd = fwd + shifted(fwd, s)
        bwd = bwd + shifted(bwd, -s)
        s *= 2
    inv_r = inv_counts(lax.broadcasted_iota(I32, (rows, 1, ch), 0), rows)
    inv_c = inv_counts(lax.broadcasted_iota(I32, (1, GRID_W, ch), 1), GRID_W)
    pooled = (fwd + bwd) * inv_r * inv_c - xb.astype(F32).reshape(rows, GRID_W, ch)
    o_ref[0] = pooled.reshape(l, ch).astype(BF16)


def _pool_groups_body(p_ref, o_ref, s1_ref, *, rows):
    for gi, window in enumerate(POOL_WINDOWS):
        @pl.when(pl.program_id(1) == gi)
        def _(half=window // 2):
            _pool_body(p_ref, o_ref, s1_ref, half=half, rows=rows)


def _pool(pin, ch):
    bsz, l, pw = pin.shape
    return pl.pallas_call(
        functools.partial(_pool_groups_body, rows=l // GRID_W),
        grid=(bsz, pw // ch),
        in_specs=[pl.BlockSpec((1, l, ch), lambda b, g: (b, 0, g))],
        out_specs=pl.BlockSpec((1, l, ch), lambda b, g: (b, 0, g)),
        out_shape=jax.ShapeDtypeStruct((bsz, l, pw), BF16),
        scratch_shapes=[pltpu.VMEM((l, ch), F32)],
        compiler_params=_cparams(("parallel", "parallel")),
        name="pool",
    )(pin)


def _merge_body(x_ref, m1_ref, s1_ref, g1_ref, m2_ref, s2_ref,
                of_ref, ob_ref, sg_ref, mx_ref,
                wm_ref, wgla_ref, wpool_ref, wout_ref, gn_ref, wr_ref,
                x1_ref, h2_ref, aff_ref, *, heads, ne):
    d = x_ref.shape[1]

    def sub_tile(rows):
        bp = jnp.dot(mx_ref[rows, :], wpool_ref[...], preferred_element_type=F32)
        x = x_ref[rows, :]
        h = _modulated_norm(x, m1_ref[0], s1_ref[0]).astype(BF16)
        o = of_ref[rows, :] + ob_ref[rows, :]
        sg = sg_ref[rows, :].astype(F32)
        og = []
        for j in range(heads):
            oj = o[:, j * LANES:(j + 1) * LANES]
            oj = oj * lax.rsqrt(jnp.mean(oj * oj, axis=-1, keepdims=True) + EPS) * gn_ref[...]
            og.append((oj * sg[:, j * LANES:(j + 1) * LANES]).astype(BF16))
        og = jnp.concatenate(og, axis=1)
        yield
        gates = jnp.dot(h, wm_ref[...], preferred_element_type=F32)
        bg = jnp.dot(og, wgla_ref[...], preferred_element_type=F32)
        yield
        gates = _sigmoid(gates)
        z = (gates[:, :d] * bg + gates[:, d:] * bp).astype(BF16)
        yield
        y = jnp.dot(z, wout_ref[...], preferred_element_type=F32)
        yield
        x1 = x + g1_ref[0] * y
        x1_ref[rows, :] = x1
        h2 = _modulated_norm(x1, m2_ref[0], s2_ref[0])
        hi = h2.astype(BF16)
        h2_ref[rows, :] = hi
        lo = (h2 - hi.astype(F32)).astype(BF16)
        yield
        lg = (jnp.dot(hi, wr_ref[...], preferred_element_type=F32)
              + jnp.dot(lo, wr_ref[...], preferred_element_type=F32))
        yield
        lgt = lg.T
        logit = lgt[0:ne] + lgt[ne:2 * ne]
        mx = jnp.max(logit, axis=0, keepdims=True)
        ex = jnp.exp(logit - mx)
        aff_ref[0, :, rows] = ex / jnp.sum(ex, axis=0, keepdims=True)
        yield

    _staggered([sub_tile(r) for r in _row_groups(x_ref.shape[0])], 7)


def _merge(x2, mods, of, ob, sg, mixed, wm, wgla, wpool, wout, gn, wr, rows_per_sample, heads, ne, tm):
    n, d = x2.shape
    bsz = n // rows_per_sample
    tps = rows_per_sample // tm
    row = lambda i: (i, 0)
    sample_of = lambda i, *_: i // tps
    vecs = [MOD_MULT1, MOD_SHIFT1, MOD_GATE1, MOD_MULT2, MOD_SHIFT2]
    full = lambda a: pl.BlockSpec(a.shape, lambda i: (0,) * a.ndim, pipeline_mode=pl.Buffered(1))
    gv = of.shape[-1]
    return pl.pallas_call(
        functools.partial(_merge_body, heads=heads, ne=ne),
        grid=(n // tm,),
        in_specs=[pl.BlockSpec((tm, d), row)] + [_mod_spec(d, v, sample_of) for v in vecs]
                 + [pl.BlockSpec((tm, gv), row)] * 4
                 + [full(wm), full(wgla), full(wpool), full(wout), full(gn), full(wr)],
        out_specs=[pl.BlockSpec((tm, d), row),
                   pl.BlockSpec((tm, d), row),
                   pl.BlockSpec((1, ne, tm), lambda i: (i // tps, 0, i % tps))],
        out_shape=[jax.ShapeDtypeStruct((n, d), F32),
                   jax.ShapeDtypeStruct((n, d), BF16),
                   jax.ShapeDtypeStruct((bsz, ne, rows_per_sample), F32)],
        compiler_params=_cparams(("parallel",)),
        name="merge",
    )(x2, *[mods] * len(vecs), of, ob, sg, mixed, wm, wgla, wpool, wout, gn, wr)


def _route_body(aff_ref, pos_ref, off_ref, cnt_ref, *, cap, ntb):
    a = aff_ref[0]
    ne, l = a.shape
    blk = ROUTE_BLK

    def bisect(i, v):
        cand = v | jnp.left_shift(jnp.int32(1), 30 - i)
        cnt = jnp.sum((a >= lax.bitcast_convert_type(cand, F32)).astype(F32), axis=1, keepdims=True)
        return jnp.where(cnt >= cap, cand, v)

    thr = lax.bitcast_convert_type(lax.fori_loop(0, 31, bisect, jnp.zeros((ne, 1), I32)), F32)
    gt = a > thr
    tie = a == thr
    need = cap - jnp.sum(gt.astype(F32), axis=1, keepdims=True)

    r = lax.broadcasted_iota(I32, (blk, blk), 0)
    c = lax.broadcasted_iota(I32, (blk, blk), 1)
    upper = (r <= c).astype(BF16)
    lane = lax.broadcasted_iota(I32, (1, LANES), 1)

    def prefix(mask_f):
        run = jnp.zeros((ne, 1), F32)
        offs = jnp.zeros((ne, LANES), F32)
        for tb in range(ntb):
            m = mask_f[:, tb * blk:(tb + 1) * blk].astype(BF16)
            loc = jnp.dot(m, upper, preferred_element_type=F32)
            cnt_ref[:, tb * blk:(tb + 1) * blk] = loc + run
            offs = jnp.where(lane == tb, run, offs)
            run = run + loc[:, blk - 1:blk]
        return jnp.where(lane >= ntb, run, offs)

    tie_f = tie.astype(F32)
    prefix(tie_f)
    tie_excl = cnt_ref[...] - tie_f
    sel = gt | (tie & (tie_excl < need))
    offs = prefix(sel.astype(F32))
    pos_ref[0] = jnp.where(sel, cnt_ref[...] - 1.0, -1.0).astype(I32)
    off_ref[0] = offs.astype(I32)


def _route(aff, cap):
    bsz, ne, l = aff.shape
    spec = lambda s: pl.BlockSpec((1,) + s, lambda b: (b, 0, 0))
    return pl.pallas_call(
        functools.partial(_route_body, cap=cap, ntb=l // ROUTE_BLK),
        grid=(bsz,),
        in_specs=[spec((ne, l))],
        out_specs=[spec((ne, l)), spec((ne, LANES))],
        out_shape=[jax.ShapeDtypeStruct((bsz, ne, l), I32),
                   jax.ShapeDtypeStruct((bsz, ne, LANES), I32)],
        scratch_shapes=[pltpu.VMEM((ne, l), F32)],
        compiler_params=_cparams(("parallel",)),
        name="route",
    )(aff)


def _window_plan(off_ref, bb, tt, experts, ne):
    lows = [off_ref[(bb * ne + e) * LANES + tt] & -BF16_ROWS for e in experts]
    ends = [off_ref[(bb * ne + e) * LANES + tt + 1] for e in experts]
    return lows, ends


def _window_rounds(lows, ends):
    rounds = jnp.int32(0)
    for lo, hi in zip(lows, ends):
        rounds = jnp.maximum(rounds, lax.div(hi - lo + (SLOT_WIN - 1), jnp.int32(SLOT_WIN)))
    return rounds


def _gatherx_body(off_ref, h_ref, pos_ref, xs_ref, *, ne, cap, eg):
    b = pl.program_id(0)
    g = pl.program_id(1)
    tb = pl.program_id(2)
    win = SLOT_WIN

    @pl.when(tb == 0)
    def _():
        xs_ref[...] = jnp.zeros(xs_ref.shape, BF16)

    j_col = lax.broadcasted_iota(I32, (win, 1), 0)
    t = ROUTE_BLK
    nsub = h_ref.shape[0] // t
    experts = [g * eg + k for k in range(eg)]

    def select(sub, lows, r):
        starts = [pl.multiple_of(jnp.minimum(lows[k] + r * win, cap - win), BF16_ROWS) for k in range(eg)]
        pieces = []
        for k in range(eg):
            p = pos_ref[0, k:k + 1, sub * t:(sub + 1) * t]
            hit = (p - starts[k] == j_col) & (p >= lows[k] + r * win)
            pieces.append(hit.astype(BF16))
        sel = jnp.concatenate(pieces, axis=0)
        rows = jnp.dot(sel, h_ref[sub * t:(sub + 1) * t, :], preferred_element_type=F32).astype(BF16)
        return starts, rows

    def deposit(starts, rows):
        for k in range(eg):
            dst = (0, k, pl.ds(starts[k], win), slice(None))
            xs_ref[dst] = xs_ref[dst] + rows[k * win:(k + 1) * win]

    plans = [_window_plan(off_ref, b, tb * nsub + sub, experts, ne) for sub in range(nsub)]
    firsts = [select(sub, plans[sub][0], 0) for sub in range(nsub)]
    for starts, rows in firsts:
        deposit(starts, rows)
    for sub in range(nsub):
        lows, ends = plans[sub]

        def extra_round(r, carry, lows=lows, sub=sub):
            deposit(*select(sub, lows, r))
            return carry

        lax.fori_loop(1, _window_rounds(lows, ends), extra_round, 0)


def _gatherx(off_flat, h2, pos, cap):
    n, d = h2.shape
    bsz, ne, l = pos.shape
    t = min(l, 2 * TOKEN_TILE)
    ntb = l // t
    eg = SUBLANES
    grid_spec = pltpu.PrefetchScalarGridSpec(
        num_scalar_prefetch=1,
        grid=(bsz, ne // eg, ntb),
        in_specs=[pl.BlockSpec((t, d), lambda b, g, i, off: (b * ntb + i, 0)),
                  pl.BlockSpec((1, eg, t), lambda b, g, i, off: (b, g, i))],
        out_specs=pl.BlockSpec((1, eg, cap, d), lambda b, g, i, off: (b, g, 0, 0)),
    )
    return pl.pallas_call(
        functools.partial(_gatherx_body, ne=ne, cap=cap, eg=eg),
        grid_spec=grid_spec,
        out_shape=jax.ShapeDtypeStruct((bsz, ne, cap, d), BF16),
        compiler_params=_cparams(("parallel", "parallel", "arbitrary")),
        name="gatherx",
    )(off_flat, h2, pos)


def _moe_body(xs_ref, wg_ref, wu_ref, wd_ref, y_ref, *, rc):
    wg = wg_ref[0].astype(BF16)
    wu = wu_ref[0].astype(BF16)
    wd = wd_ref[0].astype(BF16)
    cap = xs_ref.shape[2]
    acts = []
    for ch in range(cap // rc):
        xs = xs_ref[0, 0, ch * rc:(ch + 1) * rc, :]
        gate = jnp.dot(xs, wg, preferred_element_type=F32)
        up = jnp.dot(xs, wu, preferred_element_type=F32)
        acts.append((_silu(gate) * up).astype(BF16))
    for ch in range(cap // rc):
        y_ref[0, 0, ch * rc:(ch + 1) * rc, :] = jnp.dot(acts[ch], wd, preferred_element_type=F32).astype(BF16)


def _moe(xs, wg, wu, wd):
    bsz, ne, cap, d = xs.shape
    de = wg.shape[2]
    slot = pl.BlockSpec((1, 1, cap, d), lambda e, b: (b, e, 0, 0))
    return pl.pallas_call(
        functools.partial(_moe_body, rc=min(cap, 2 * MXU_DIM)),
        grid=(ne, bsz),
        in_specs=[slot,
                  pl.BlockSpec((1, d, de), lambda e, b: (e, 0, 0)),
                  pl.BlockSpec((1, d, de), lambda e, b: (e, 0, 0)),
                  pl.BlockSpec((1, de, d), lambda e, b: (e, 0, 0))],
        out_specs=slot,
        out_shape=jax.ShapeDtypeStruct((bsz, ne, cap, d), BF16),
        compiler_params=_cparams(("parallel", "arbitrary")),
        name="moe",
    )(xs, wg, wu, wd)


def _combine_body(off_ref, x1_ref, g2_ref, fg_ref, pos_ref, aff_ref, y_hbm, o_ref, ybuf, spare, acc_ref, sem,
                  spare_sem, *, ne, cap, nsp, nsub, span):
    b = pl.program_id(0)
    i = pl.program_id(1)
    step = b * nsp + i
    nsteps = pl.num_programs(0) * nsp
    slot = step % 2
    win = SLOT_WIN
    t = ROUTE_BLK
    experts = range(ne)

    def window_starts(lows, r):
        return [pl.multiple_of(jnp.minimum(lows[e] + r * win, cap - win), BF16_ROWS) for e in experts]

    def span_starts(bb, ii):
        lows = _window_plan(off_ref, bb, ii * nsub, experts, ne)[0]
        return [pl.multiple_of(jnp.minimum(lows[e], cap - span), BF16_ROWS) for e in experts]

    def span_copy(bb, e, start, buf):
        return pltpu.make_async_copy(y_hbm.at[bb, e, pl.ds(start, span), :], ybuf.at[buf, e], sem.at[buf, e])

    def spare_copy(e, start):
        return pltpu.make_async_copy(y_hbm.at[b, e, pl.ds(start, win), :],
                                     spare.at[pl.ds(e * win, win), :], spare_sem.at[e])

    def start_step(bb, ii, buf):
        starts = span_starts(bb, ii)
        for e in experts:
            span_copy(bb, e, starts[e], buf).start()

    @pl.when(step == 0)
    def _():
        start_step(b, i, 0)

    nxt = jnp.minimum(step + 1, nsteps - 1)
    start_step(lax.div(nxt, jnp.int32(nsp)), lax.rem(nxt, jnp.int32(nsp)), 1 - slot)

    plans = [_window_plan(off_ref, b, i * nsub + sub, experts, ne) for sub in range(nsub)]
    mine = span_starts(b, i)
    firsts = [window_starts(plans[sub][0], 0) for sub in range(nsub)]
    inside = []
    for sub in range(nsub):
        ok = jnp.bool_(True)
        for e in experts:
            ok = ok & (firsts[sub][e] >= mine[e]) & (firsts[sub][e] + win <= mine[e] + span)
        inside.append(ok)
    j_col = lax.broadcasted_iota(I32, (win, 1), 0)

    def expand(sub, r, starts, rows, enabled=None):
        lows = plans[sub][0]
        pieces = []
        for e in experts:
            p = pos_ref[0, e:e + 1, sub * t:(sub + 1) * t]
            valid = p >= lows[e] + r * win
            hit = (p - starts[e] == j_col) & valid
            if enabled is not None:
                hit = hit & enabled
            pieces.append(jnp.where(hit, aff_ref[0, e:e + 1, sub * t:(sub + 1) * t], 0.0).astype(BF16))
        pmat = jnp.concatenate(pieces, axis=0)
        return lax.dot_general(pmat, rows, _TN, preferred_element_type=F32)

    acc_ref[...] = jnp.zeros(acc_ref.shape, F32)
    for sub in range(nsub):
        lows, ends = plans[sub]

        def own_round(r, carry, lows=lows, sub=sub):
            starts = window_starts(lows, r)
            for e in experts:
                spare_copy(e, starts[e]).start()
            for e in experts:
                spare_copy(e, starts[e]).wait()
            acc_ref[sub] += expand(sub, r, starts, spare[...])
            return carry

        lax.fori_loop(jnp.where(inside[sub], 1, 0), _window_rounds(lows, ends), own_round, 0)

    for e in experts:
        span_copy(b, e, mine[e], slot).wait()
    for sub in range(nsub):
        rows = pl.ds(sub * t, t)
        picked = []
        for e in experts:
            local = pl.multiple_of(jnp.clip(firsts[sub][e] - mine[e], 0, span - win), BF16_ROWS)
            picked.append(ybuf[slot, e, pl.ds(local, win), :])
        moe = expand(sub, 0, firsts[sub], jnp.concatenate(picked, axis=0), inside[sub]) + acc_ref[sub]
        x2 = x1_ref[rows, :] + g2_ref[0] * moe
        ms = jnp.mean(x2 * x2, axis=-1, keepdims=True)
        o_ref[rows, :] = x2 * lax.rsqrt(ms + EPS) * fg_ref[...]

    @pl.when(step == nsteps - 1)
    def _():
        for e in experts:
            span_copy(b, e, mine[e], 1 - slot).wait()


def _combine(off_flat, x1, mods, fg, pos, aff, y, rows_per_sample):
    n, d = x1.shape
    bsz, ne, cap, _ = y.shape
    t = min(rows_per_sample, TOKEN_TILE)
    nsub = t // ROUTE_BLK
    nsp = rows_per_sample // t
    rows = ne * SLOT_WIN
    span = min(cap, nsub * SLOT_WIN * 11 // 16)
    grid_spec = pltpu.PrefetchScalarGridSpec(
        num_scalar_prefetch=1,
        grid=(bsz, nsp),
        in_specs=[pl.BlockSpec((t, d), lambda b, i, off: (b * nsp + i, 0)),
                  _mod_spec(d, MOD_GATE2, lambda b, i: b),
                  pl.BlockSpec((1, d), lambda b, i, off: (0, 0)),
                  pl.BlockSpec((1, ne, t), lambda b, i, off: (b, 0, i)),
                  pl.BlockSpec((1, ne, t), lambda b, i, off: (b, 0, i)),
                  pl.BlockSpec(memory_space=pl.ANY)],
        out_specs=pl.BlockSpec((t, d), lambda b, i, off: (b * nsp + i, 0)),
        scratch_shapes=[pltpu.VMEM((2, ne, span, d), BF16),
                        pltpu.VMEM((rows, d), BF16),
                        pltpu.VMEM((nsub, ROUTE_BLK, d), F32),
                        pltpu.SemaphoreType.DMA((2, ne)),
                        pltpu.SemaphoreType.DMA((ne,))],
    )
    return pl.pallas_call(
        functools.partial(_combine_body, ne=ne, cap=cap, nsp=nsp, nsub=nsub, span=span),
        grid_spec=grid_spec,
        out_shape=jax.ShapeDtypeStruct((n, d), F32),
        compiler_params=_cparams(("arbitrary", "arbitrary")),
        name="combine",
    )(off_flat, x1, mods, fg, pos, aff, y)


def kernel(x, c, ctx, c_ctx, ada_w, ada_b, norm1_g, norm2_g, w_in, w_decay_up, b_decay, gla_norm_g,
           w_gla_proj, pool_w, pool_scale, w_pool_proj, w_out, w_router, w_gate_e, w_up_e, w_down_e,
           final_norm_g):
    assert ada_w.shape[0] == 1, "single-layer block"
    bsz, l, d = x.shape
    lc = ctx.shape[1]
    rank, dk = w_decay_up.shape[2], w_decay_up.shape[3]
    dvh = gla_norm_g.shape[1]
    dv = w_gla_proj.shape[1]
    heads = dv // dvh
    groups, ch = pool_w.shape[1], pool_w.shape[2]
    pw = groups * ch
    ne = w_router.shape[2]
    cap = EC_CAPACITY * l // ne
    assert dk // heads == HEAD_DK and dvh == LANES and ch == LANES and heads % 2 == 0
    assert l % ROUTE_BLK == 0 and l // ROUTE_BLK < LANES and cap >= SLOT_WIN and cap % BF16_ROWS == 0
    assert GRID_W == GLA_CHUNK and MXU_DIM % GRID_W == 0 and (l // MXU_DIM) % 2 == 0

    assert bsz < COND_ROWS
    cin = jnp.zeros((COND_ROWS, d), F32).at[:bsz].set(c).at[bsz].set(c_ctx)
    mods = _ada(cin, ada_w[0], ada_b[0][None, :], jnp.concatenate([norm1_g, norm2_g], axis=0))

    o_r = dk + dv
    o_q = o_r + 2 * rank
    o_g = o_q + dk
    o_p = o_g + dv
    o_m = o_p + pw
    up = jnp.zeros((2 * rank, 2 * dk), F32)
    up = up.at[:rank, :dk].set(w_decay_up[0, 0]).at[rank:, dk:].set(w_decay_up[0, 1])
    psc = pool_scale[0].reshape(groups, 1, ch)
    w1, wm, wpool = _prep(w_in[0].T, up, pool_w[0], psc, w_pool_proj[0], o_r, o_q, o_g, o_m,
                          float(dk // heads) ** -0.5)
    bz = b_decay[0]

    zero_state = jnp.zeros((bsz, heads // 2, LANES, 2 * LANES), F32)
    ck, cv, _, claf, clab, _, _ = _inproj(ctx.reshape(bsz * lc, d), mods, lambda i: bsz, w1, bz, dk, dv, pw,
                                          min(bsz * lc, TOKEN_TILE))
    r3 = lambda a, n: a.reshape(bsz, n, a.shape[-1])
    _, _, h_f, h_b = _gla(r3(ck, lc), r3(ck, lc), r3(cv, lc), r3(claf, lc), r3(clab, lc),
                          zero_state, zero_state, min(lc, CTX_TILE))

    x2 = x.reshape(bsz * l, d)
    tm = min(l, TOKEN_TILE)
    k, v, q, laf, lab, sg, pin = _inproj(x2, mods, lambda i: i // (l // tm), w1, bz, dk, dv, pw, tm)
    of, ob, _, _ = _gla(r3(k, l), r3(q, l), r3(v, l), r3(laf, l), r3(lab, l), h_f, h_b, min(l, GLA_TILE))
    pooled = _pool(r3(pin, l), ch).reshape(bsz * l, pw)

    wr_hi = w_router[0].astype(BF16)
    wr_lo = (w_router[0] - wr_hi.astype(F32)).astype(BF16)
    wr = jnp.zeros((d, LANES), BF16).at[:, :ne].set(wr_hi).at[:, ne:2 * ne].set(wr_lo)
    x1, h2, aff = _merge(x2, mods, of.reshape(bsz * l, dv), ob.reshape(bsz * l, dv), sg, pooled,
                          wm, w_gla_proj[0].astype(BF16), wpool, w_out[0].astype(BF16),
                          gla_norm_g[0][None, :], wr, l, heads, ne, tm)

    pos, offs = _route(aff.reshape(1, bsz * ne, l), cap)
    pos = pos.reshape(bsz, ne, l)
    off_flat = offs.reshape(-1)
    xs = _gatherx(off_flat, h2, pos, cap)
    y = _moe(xs, w_gate_e[0], w_up_e[0], w_down_e[0])
    out = _combine(off_flat, x1, mods, final_norm_g[None, :], pos, aff, y, l)
    return out.reshape(bsz, l, d)
```

```python
import functools

import jax
import jax.numpy as jnp
from jax import lax
from jax.experimental import pallas as pl
from jax.experimental.pallas import tpu as pltpu

F32 = jnp.float32
BF16 = jnp.bfloat16
I32 = jnp.int32
HIGHEST = lax.Precision.HIGHEST

EPS = 1e-6
GRID_W = 64
GLA_CHUNK = 64
GLA_STAGES = 6
GLA_BATCH = 8
GATE_NORMALIZER = 16.0
POOL_WINDOWS = (2, 4, 8, 16)
EC_CAPACITY = 2

LANES = 128
SUBLANES = 8
BF16_ROWS = 16
MXU_DIM = 256
VMEM_BYTES = 64 * 1024 * 1024
VMEM_LIMIT = VMEM_BYTES * 7 // 8

HEAD_DK = LANES // 2
ROUTE_BLK = MXU_DIM
SLOT_WIN = 64
SUB_ROWS = MXU_DIM
TOKEN_TILE = 1024
GLA_TILE = 2048
CTX_TILE = 256
COND_ROWS = SUBLANES
MOD_SHIFT1, MOD_MULT1, MOD_GATE1, MOD_SHIFT2, MOD_MULT2, MOD_GATE2 = range(6)


def _mod_spec(d, which, sample_of):
    return pl.BlockSpec((1, 1, d), lambda *g: (which * COND_ROWS + sample_of(*g[:2]), 0, 0))

_NT = (((1,), (1,)), ((), ()))
_TN = (((0,), (0,)), ((), ()))


def _cparams(sem):
    return pltpu.CompilerParams(dimension_semantics=sem, vmem_limit_bytes=VMEM_LIMIT)


def _sigmoid(x):
    return 0.5 * jnp.tanh(0.5 * x) + 0.5


def _silu(x):
    return x * _sigmoid(x)


def _row_groups(n):
    step = min(n, SUB_ROWS)
    return [pl.ds(i, step) for i in range(0, n, step)]


def _staggered(gens, nstages):
    for t in range(nstages + len(gens) - 1):
        for g in reversed(range(len(gens))):
            if 0 <= t - g < nstages:
                next(gens[g])


def _ada_body(c_ref, w_ref, b_ref, g_ref, o_ref):
    s = _silu(c_ref[...])
    rows = s.shape[0]
    s_hi = s.astype(BF16)
    s_lo = (s - s_hi.astype(F32)).astype(BF16)
    w = w_ref[...]
    w_hi = w.astype(BF16)
    w_lo = (w - w_hi.astype(F32)).astype(BF16)
    both = jnp.dot(jnp.concatenate([s_hi, s_lo], axis=0), w_hi, preferred_element_type=F32)
    out = (both[:rows] + both[rows:] + jnp.dot(s_hi, w_lo, preferred_element_type=F32)) + b_ref[...]
    j = pl.program_id(0)
    gain = jnp.where(j == MOD_MULT1, g_ref[0:1, :], g_ref[1:2, :])
    o_ref[...] = jnp.where((j == MOD_MULT1) | (j == MOD_MULT2), gain * (1.0 + out), out)


def _ada(cin, w, b, gains):
    rows, d = cin.shape
    nvec = w.shape[1] // d
    mods = pl.pallas_call(
        _ada_body,
        grid=(nvec,),
        in_specs=[pl.BlockSpec((rows, d), lambda j: (0, 0)),
                  pl.BlockSpec((d, d), lambda j: (0, j)),
                  pl.BlockSpec((1, d), lambda j: (0, j)),
                  pl.BlockSpec(gains.shape, lambda j: (0, 0))],
        out_specs=pl.BlockSpec((rows, d), lambda j: (j, 0)),
        out_shape=jax.ShapeDtypeStruct((nvec * rows, d), F32),
        compiler_params=_cparams(("arbitrary",)),
        name="ada",
    )(cin, w, b, gains)
    return mods.reshape(nvec * rows, 1, d)


def _prep_body(w_ref, up_ref, pw_ref, ps_ref, wpp_ref, w1_ref, wm_ref, wpf_ref, *, o_r, o_q, o_g, o_m, qscale):
    dk = o_g - o_q
    wz = lax.dot_general(w_ref[o_r:o_q, :], up_ref[...], _TN, precision=HIGHEST, preferred_element_type=F32)
    nz = wz.shape[1]
    w1_ref[:, :o_r] = w_ref[:o_r, :].T.astype(BF16)
    w1_ref[:, o_r:o_r + dk] = (w_ref[o_q:o_g, :].T * qscale).astype(BF16)
    w1_ref[:, o_r + dk:o_r + dk + nz] = wz.astype(BF16)
    w1_ref[:, o_r + dk + nz:] = w_ref[o_g:o_m, :].T.astype(BF16)
    wm_ref[...] = w_ref[o_m:, :].T.astype(BF16)
    wpf_ref[...] = jnp.dot(pw_ref[0] * ps_ref[0], wpp_ref[...], precision=HIGHEST,
                           preferred_element_type=F32).astype(BF16)


def _prep(w_in_t, up, pool_w, pool_scale3, w_pool_proj, o_r, o_q, o_g, o_m, qscale):
    n, d = w_in_t.shape
    n1 = o_r + (o_g - o_q) + up.shape[1] + (o_m - o_g)
    groups, ch, _ = pool_w.shape
    tr = d // groups
    dm = w_pool_proj.shape[1]
    return pl.pallas_call(
        functools.partial(_prep_body, o_r=o_r, o_q=o_q, o_g=o_g, o_m=o_m, qscale=qscale),
        grid=(d // tr,),
        in_specs=[pl.BlockSpec((n, tr), lambda i: (0, i)),
                  pl.BlockSpec(up.shape, lambda i: (0, 0)),
                  pl.BlockSpec((1, ch, ch), lambda i: (i, 0, 0)),
                  pl.BlockSpec((1, 1, ch), lambda i: (i, 0, 0)),
                  pl.BlockSpec((ch, dm), lambda i: (i, 0))],
        out_specs=[pl.BlockSpec((tr, n1), lambda i: (i, 0)),
                   pl.BlockSpec((tr, n - o_m), lambda i: (i, 0)),
                   pl.BlockSpec((ch, dm), lambda i: (i, 0))],
        out_shape=[jax.ShapeDtypeStruct((d, n1), BF16), jax.ShapeDtypeStruct((d, n - o_m), BF16),
                   jax.ShapeDtypeStruct((groups * ch, dm), BF16)],
        compiler_params=_cparams(("parallel",)),
        name="prep",
    )(w_in_t, up, pool_w, pool_scale3, w_pool_proj)


def _modulated_norm(x, mult, shift):
    ms = jnp.mean(x * x, axis=-1, keepdims=True)
    return (x * lax.rsqrt(ms + EPS)) * mult + shift


def _log_sigmoid(z):
    return jnp.minimum(z, 0.0) - jnp.log1p(jnp.exp(-jnp.abs(z)))


def _inproj_body(x_ref, mult_ref, shift_ref, w_ref, bz_ref,
                 k_ref, v_ref, q_ref, laf_ref, lab_ref, sg_ref, p_ref, *, dk, dv, pw):
    def sub_tile(rows):
        h = _modulated_norm(x_ref[rows, :], mult_ref[0], shift_ref[0]).astype(BF16)
        yield
        u = jnp.dot(h, w_ref[...], preferred_element_type=F32)
        yield
        o = 0
        k_ref[rows, :] = u[:, o:o + dk].astype(BF16); o += dk
        v_ref[rows, :] = u[:, o:o + dv].astype(BF16); o += dv
        q_ref[rows, :] = u[:, o:o + dk].astype(BF16); o += dk
        zf = u[:, o:o + dk] + bz_ref[0:1, :]; o += dk
        zb = u[:, o:o + dk] + bz_ref[1:2, :]; o += dk
        laf_ref[rows, :] = _log_sigmoid(zf) * (1.0 / GATE_NORMALIZER)
        lab_ref[rows, :] = _log_sigmoid(zb) * (1.0 / GATE_NORMALIZER)
        sg_ref[rows, :] = _silu(u[:, o:o + dv]).astype(BF16); o += dv
        p_ref[rows, :] = u[:, o:o + pw].astype(BF16)
        yield

    _staggered([sub_tile(r) for r in _row_groups(x_ref.shape[0])], 3)


def _inproj(x2, mods, cond_row, w, bz, dk, dv, pw, tm):
    n, d = x2.shape
    row = lambda i: (i, 0)
    sample_of = lambda i, *_: cond_row(i)
    outs = [(dk, BF16), (dv, BF16), (dk, BF16), (dk, F32), (dk, F32), (dv, BF16), (pw, BF16)]
    return pl.pallas_call(
        functools.partial(_inproj_body, dk=dk, dv=dv, pw=pw),
        grid=(n // tm,),
        in_specs=[pl.BlockSpec((tm, d), row),
                  _mod_spec(d, MOD_MULT1, sample_of),
                  _mod_spec(d, MOD_SHIFT1, sample_of),
                  pl.BlockSpec(w.shape, lambda i: (0, 0), pipeline_mode=pl.Buffered(1)),
                  pl.BlockSpec(bz.shape, lambda i: (0, 0))],
        out_specs=[pl.BlockSpec((tm, c), row) for c, _ in outs],
        out_shape=[jax.ShapeDtypeStruct((n, c), t) for c, t in outs],
        compiler_params=_cparams(("parallel",)),
        name="inproj",
    )(x2, mods, mods, w, bz)


def _gla_direction(k_ref, q_ref, v_ref, la_ref, o_ref, s_ref, reverse, nchunk):
    C = GLA_CHUNK
    hd = HEAD_DK
    cb = MXU_DIM
    lt = nchunk * C
    la = la_ref[0]
    r = lax.broadcasted_iota(I32, (cb, cb), 0)
    c = lax.broadcasted_iota(I32, (cb, cb), 1)
    same = (r // C) == (c // C)
    cum = jnp.where(same & ((c >= r) if reverse else (c <= r)), 1.0, 0.0).astype(BF16)
    la_hi = la.astype(BF16)
    la_lo = (la - la_hi.astype(F32)).astype(BF16)
    la2 = jnp.concatenate([la_hi, la_lo], axis=1)
    bcs = []
    for blk in range(lt // cb):
        part = jnp.dot(cum, la2[blk * cb:(blk + 1) * cb], preferred_element_type=F32)
        bcs.append(part[:, :LANES] + part[:, LANES:])
    ri = lax.broadcasted_iota(I32, (C, 2 * C), 0)
    ci = lax.broadcasted_iota(I32, (C, 2 * C), 1) % C
    tri = (ci >= ri) if reverse else (ci <= ri)
    lane = lax.broadcasted_iota(I32, (1, LANES), 1)
    m0 = (lane < hd).astype(F32)
    m1 = (lane >= hd).astype(F32)
    sr = lax.broadcasted_iota(I32, (LANES, 2 * LANES), 0)
    sl = lax.broadcasted_iota(I32, (LANES, 2 * LANES), 1)
    smask = ((sr < hd) == (sl < LANES)).astype(F32)
    kt = k_ref[0].astype(F32)
    qt = q_ref[0].astype(F32)
    zero_v = jnp.zeros((C, LANES), BF16)
    sweep = list(range(nchunk - 1, -1, -1) if reverse else range(nchunk))
    for first in range(0, nchunk, GLA_BATCH):
        order = sweep[first:first + GLA_BATCH]
        intra, qds, kvs, decs, q2s, kss, kws, scs = {}, {}, {}, {}, {}, {}, {}, {}
        for ch in order:
            lo = ch * C
            b = bcs[lo // cb][lo % cb:lo % cb + C]
            last = b[0:1] if reverse else b[C - 1:C]
            mid = b[C // 2:C // 2 + 1] if reverse else b[C // 2 - 1:C // 2]
            kc = kt[lo:lo + C]
            qc = qt[lo:lo + C]
            q2s[ch] = (qc * jnp.exp(b - mid)).astype(BF16)
            ks = kc * jnp.exp(mid - b)
            kss[ch] = jnp.concatenate([ks * m0, ks * m1], axis=0).astype(BF16)
            qds[ch] = (qc * jnp.exp(b)).astype(BF16)
            kws[ch] = (kc * jnp.exp(last - b)).astype(BF16)
            decs[ch] = last
        yield
        for ch in order:
            scs[ch] = lax.dot_general(q2s[ch], kss[ch], _NT, preferred_element_type=F32)
        yield
        for ch in order:
            v2 = v_ref[0, ch * C:(ch + 1) * C, :]
            kvs[ch] = lax.dot_general(kws[ch], v2, _TN, preferred_element_type=F32) * smask
        yield
        for ch in order:
            sc2 = jnp.where(tri, scs[ch], 0.0).astype(BF16)
            v2 = v_ref[0, ch * C:(ch + 1) * C, :]
            vbd = jnp.concatenate([jnp.concatenate([v2[:, :LANES], zero_v], axis=1),
                                   jnp.concatenate([zero_v, v2[:, LANES:]], axis=1)], axis=0)
            intra[ch] = jnp.dot(sc2, vbd, preferred_element_type=F32)
        yield
        pad = jnp.zeros((LANES - len(order), LANES), F32)
        dec_cols = jnp.exp(jnp.concatenate([decs[ch] for ch in order] + [pad], axis=0).T)
        st = s_ref[...]
        starts = {}
        for i, ch in enumerate(order):
            starts[ch] = st.astype(BF16)
            st = st * dec_cols[:, i:i + 1] + kvs[ch]
        s_ref[...] = st
        yield
        for ch in order:
            inter = jnp.dot(qds[ch], starts[ch], preferred_element_type=F32)
            o_ref[0, ch * C:(ch + 1) * C, :] = inter + intra[ch]
        yield


def _gla_body(kf, qf, vf, laf, kb, qb, vb, lab, h0f, h0b, of, ob, hf_out, hb_out, sf, sb, *, nchunk):
    i = pl.program_id(2)

    @pl.when(i == 0)
    def _():
        sf[...] = h0f[0, 0]
        sb[...] = h0b[0, 0]

    sweeps = [_gla_direction(kf, qf, vf, laf, of, sf, False, nchunk),
              _gla_direction(kb, qb, vb, lab, ob, sb, True, nchunk)]
    for _ in range(GLA_STAGES * pl.cdiv(nchunk, GLA_BATCH)):
        for sweep in sweeps:
            next(sweep)

    @pl.when(i == pl.num_programs(2) - 1)
    def _():
        hf_out[0, 0] = sf[...]
        hb_out[0, 0] = sb[...]


def _gla(k, q, v, laf, lab, h0f, h0b, lt):
    bsz, l, _ = k.shape
    pairs = h0f.shape[1]
    nt = l // lt
    fwd = lambda b, hp, i: (b, i, hp)
    bwd = lambda b, hp, i: (b, nt - 1 - i, hp)
    st = lambda b, hp, i: (b, hp, 0, 0)
    kq = lambda m: pl.BlockSpec((1, lt, LANES), m)
    vv = lambda m: pl.BlockSpec((1, lt, 2 * LANES), m)
    sspec = pl.BlockSpec((1, 1, LANES, 2 * LANES), st)
    return pl.pallas_call(
        functools.partial(_gla_body, nchunk=lt // GLA_CHUNK),
        grid=(bsz, pairs, nt),
        in_specs=[kq(fwd), kq(fwd), vv(fwd), kq(fwd), kq(bwd), kq(bwd), vv(bwd), kq(bwd), sspec, sspec],
        out_specs=[vv(fwd), vv(bwd), sspec, sspec],
        out_shape=[jax.ShapeDtypeStruct(v.shape, F32), jax.ShapeDtypeStruct(v.shape, F32),
                   jax.ShapeDtypeStruct(h0f.shape, F32), jax.ShapeDtypeStruct(h0b.shape, F32)],
        scratch_shapes=[pltpu.VMEM((LANES, 2 * LANES), F32), pltpu.VMEM((LANES, 2 * LANES), F32)],
        compiler_params=_cparams(("parallel", "parallel", "arbitrary")),
        name="gla",
    )(k, q, v, laf, k, q, v, lab, h0f, h0b)


def _pool_body(p_ref, o_ref, s1_ref, *, half, rows):
    xb = p_ref[0]
    l, ch = xb.shape

    def inv_counts(pos, n):
        return 1.0 / (jnp.minimum(pos + half, n) - jnp.maximum(pos - half, 0)).astype(F32)

    blk = MXU_DIM
    r = lax.broadcasted_iota(I32, (blk, blk), 0)
    c = lax.broadcasted_iota(I32, (blk, blk), 1)
    band = ((r // GRID_W == c // GRID_W) & (c - r >= -half) & (c - r <= half - 1)).astype(BF16)
    for i in range(0, l // blk, 2):
        pair = jnp.concatenate([xb[i * blk:(i + 1) * blk], xb[(i + 1) * blk:(i + 2) * blk]], axis=1)
        sums = jnp.dot(band, pair, preferred_element_type=F32)
        s1_ref[i * blk:(i + 1) * blk, :] = sums[:, :ch]
        s1_ref[(i + 1) * blk:(i + 2) * blk, :] = sums[:, ch:]
    s1 = s1_ref[...].reshape(rows, GRID_W, ch)

    def shifted(a, s):
        z = jnp.zeros((abs(s),) + a.shape[1:], a.dtype)
        return jnp.concatenate([a[s:], z], axis=0) if s > 0 else jnp.concatenate([z, a[:s]], axis=0)

    fwd = s1
    bwd = shifted(s1, -1)
    s = 1
    while s < half:
        fwd = fwd + shifted(fwd, s)
        bwd = bwd + shifted(bwd, -s)
        s *= 2
    inv_r = inv_counts(lax.broadcasted_iota(I32, (rows, 1, ch), 0), rows)
    inv_c = inv_counts(lax.broadcasted_iota(I32, (1, GRID_W, ch), 1), GRID_W)
    pooled = (fwd + bwd) * inv_r * inv_c - xb.astype(F32).reshape(rows, GRID_W, ch)
    o_ref[0] = pooled.reshape(l, ch).astype(BF16)


def _pool_groups_body(p_ref, o_ref, s1_ref, *, rows):
    for gi, window in enumerate(POOL_WINDOWS):
        @pl.when(pl.program_id(1) == gi)
        def _(half=window // 2):
            _pool_body(p_ref, o_ref, s1_ref, half=half, rows=rows)


def _pool(pin, ch):
    bsz, l, pw = pin.shape
    return pl.pallas_call(
        functools.partial(_pool_groups_body, rows=l // GRID_W),
        grid=(bsz, pw // ch),
        in_specs=[pl.BlockSpec((1, l, ch), lambda b, g: (b, 0, g))],
        out_specs=pl.BlockSpec((1, l, ch), lambda b, g: (b, 0, g)),
        out_shape=jax.ShapeDtypeStruct((bsz, l, pw), BF16),
        scratch_shapes=[pltpu.VMEM((l, ch), F32)],
        compiler_params=_cparams(("parallel", "parallel")),
        name="pool",
    )(pin)


def _merge_body(x_ref, m1_ref, s1_ref, g1_ref, m2_ref, s2_ref,
                of_ref, ob_ref, sg_ref, mx_ref,
                wm_ref, wgla_ref, wpool_ref, wout_ref, gn_ref, wr_ref,
                x1_ref, h2_ref, aff_ref, *, heads, ne):
    d = x_ref.shape[1]

    def sub_tile(rows):
        bp = jnp.dot(mx_ref[rows, :], wpool_ref[...], preferred_element_type=F32)
        x = x_ref[rows, :]
        h = _modulated_norm(x, m1_ref[0], s1_ref[0]).astype(BF16)
        o = of_ref[rows, :] + ob_ref[rows, :]
        sg = sg_ref[rows, :].astype(F32)
        og = []
        for j in range(heads):
            oj = o[:, j * LANES:(j + 1) * LANES]
            oj = oj * lax.rsqrt(jnp.mean(oj * oj, axis=-1, keepdims=True) + EPS) * gn_ref[...]
            og.append((oj * sg[:, j * LANES:(j + 1) * LANES]).astype(BF16))
        og = jnp.concatenate(og, axis=1)
        yield
        gates = jnp.dot(h, wm_ref[...], preferred_element_type=F32)
        bg = jnp.dot(og, wgla_ref[...], preferred_element_type=F32)
        yield
        gates = _sigmoid(gates)
        z = (gates[:, :d] * bg + gates[:, d:] * bp).astype(BF16)
        yield
        y = jnp.dot(z, wout_ref[...], preferred_element_type=F32)
        yield
        x1 = x + g1_ref[0] * y
        x1_ref[rows, :] = x1
        h2 = _modulated_norm(x1, m2_ref[0], s2_ref[0])
        hi = h2.astype(BF16)
        h2_ref[rows, :] = hi
        lo = (h2 - hi.astype(F32)).astype(BF16)
        yield
        lg = (jnp.dot(hi, wr_ref[...], preferred_element_type=F32)
              + jnp.dot(lo, wr_ref[...], preferred_element_type=F32))
        yield
        lgt = lg.T
        logit = lgt[0:ne] + lgt[ne:2 * ne]
        mx = jnp.max(logit, axis=0, keepdims=True)
        ex = jnp.exp(logit - mx)
        aff_ref[0, :, rows] = ex / jnp.sum(ex, axis=0, keepdims=True)
        yield

    _staggered([sub_tile(r) for r in _row_groups(x_ref.shape[0])], 7)


def _merge(x2, mods, of, ob, sg, mixed, wm, wgla, wpool, wout, gn, wr, rows_per_sample, heads, ne, tm):
    n, d = x2.shape
    bsz = n // rows_per_sample
    tps = rows_per_sample // tm
    row = lambda i: (i, 0)
    sample_of = lambda i, *_: i // tps
    vecs = [MOD_MULT1, MOD_SHIFT1, MOD_GATE1, MOD_MULT2, MOD_SHIFT2]
    full = lambda a: pl.BlockSpec(a.shape, lambda i: (0,) * a.ndim, pipeline_mode=pl.Buffered(1))
    gv = of.shape[-1]
    return pl.pallas_call(
        functools.partial(_merge_body, heads=heads, ne=ne),
        grid=(n // tm,),
        in_specs=[pl.BlockSpec((tm, d), row)] + [_mod_spec(d, v, sample_of) for v in vecs]
                 + [pl.BlockSpec((tm, gv), row)] * 4
                 + [full(wm), full(wgla), full(wpool), full(wout), full(gn), full(wr)],
        out_specs=[pl.BlockSpec((tm, d), row),
                   pl.BlockSpec((tm, d), row),
                   pl.BlockSpec((1, ne, tm), lambda i: (i // tps, 0, i % tps))],
        out_shape=[jax.ShapeDtypeStruct((n, d), F32),
                   jax.ShapeDtypeStruct((n, d), BF16),
                   jax.ShapeDtypeStruct((bsz, ne, rows_per_sample), F32)],
        compiler_params=_cparams(("parallel",)),
        name="merge",
    )(x2, *[mods] * len(vecs), of, ob, sg, mixed, wm, wgla, wpool, wout, gn, wr)


def _route_body(aff_ref, pos_ref, off_ref, cnt_ref, *, cap, ntb):
    a = aff_ref[0]
    ne, l = a.shape
    blk = ROUTE_BLK

    def bisect(i, v):
        cand = v | jnp.left_shift(jnp.int32(1), 30 - i)
        cnt = jnp.sum((a >= lax.bitcast_convert_type(cand, F32)).astype(F32), axis=1, keepdims=True)
        return jnp.where(cnt >= cap, cand, v)

    thr = lax.bitcast_convert_type(lax.fori_loop(0, 31, bisect, jnp.zeros((ne, 1), I32)), F32)
    gt = a > thr
    tie = a == thr
    need = cap - jnp.sum(gt.astype(F32), axis=1, keepdims=True)

    r = lax.broadcasted_iota(I32, (blk, blk), 0)
    c = lax.broadcasted_iota(I32, (blk, blk), 1)
    upper = (r <= c).astype(BF16)
    lane = lax.broadcasted_iota(I32, (1, LANES), 1)

    def prefix(mask_f):
        run = jnp.zeros((ne, 1), F32)
        offs = jnp.zeros((ne, LANES), F32)
        for tb in range(ntb):
            m = mask_f[:, tb * blk:(tb + 1) * blk].astype(BF16)
            loc = jnp.dot(m, upper, preferred_element_type=F32)
            cnt_ref[:, tb * blk:(tb + 1) * blk] = loc + run
            offs = jnp.where(lane == tb, run, offs)
            run = run + loc[:, blk - 1:blk]
        return jnp.where(lane >= ntb, run, offs)

    tie_f = tie.astype(F32)
    prefix(tie_f)
    tie_excl = cnt_ref[...] - tie_f
    sel = gt | (tie & (tie_excl < need))
    offs = prefix(sel.astype(F32))
    pos_ref[0] = jnp.where(sel, cnt_ref[...] - 1.0, -1.0).astype(I32)
    off_ref[0] = offs.astype(I32)


def _route(aff, cap):
    bsz, ne, l = aff.shape
    spec = lambda s: pl.BlockSpec((1,) + s, lambda b: (b, 0, 0))
    return pl.pallas_call(
        functools.partial(_route_body, cap=cap, ntb=l // ROUTE_BLK),
        grid=(bsz,),
        in_specs=[spec((ne, l))],
        out_specs=[spec((ne, l)), spec((ne, LANES))],
        out_shape=[jax.ShapeDtypeStruct((bsz, ne, l), I32),
                   jax.ShapeDtypeStruct((bsz, ne, LANES), I32)],
        scratch_shapes=[pltpu.VMEM((ne, l), F32)],
        compiler_params=_cparams(("parallel",)),
        name="route",
    )(aff)


def _window_plan(off_ref, bb, tt, experts, ne):
    lows = [off_ref[(bb * ne + e) * LANES + tt] & -BF16_ROWS for e in experts]
    ends = [off_ref[(bb * ne + e) * LANES + tt + 1] for e in experts]
    return lows, ends


def _window_rounds(lows, ends):
    rounds = jnp.int32(0)
    for lo, hi in zip(lows, ends):
        rounds = jnp.maximum(rounds, lax.div(hi - lo + (SLOT_WIN - 1), jnp.int32(SLOT_WIN)))
    return rounds


def _gatherx_body(off_ref, h_ref, pos_ref, xs_ref, *, ne, cap, eg):
    b = pl.program_id(0)
    g = pl.program_id(1)
    tb = pl.program_id(2)
    win = SLOT_WIN

    @pl.when(tb == 0)
    def _():
        xs_ref[...] = jnp.zeros(xs_ref.shape, BF16)

    j_col = lax.broadcasted_iota(I32, (win, 1), 0)
    t = ROUTE_BLK
    nsub = h_ref.shape[0] // t
    experts = [g * eg + k for k in range(eg)]

    def select(sub, lows, r):
        starts = [pl.multiple_of(jnp.minimum(lows[k] + r * win, cap - win), BF16_ROWS) for k in range(eg)]
        pieces = []
        for k in range(eg):
            p = pos_ref[0, k:k + 1, sub * t:(sub + 1) * t]
            hit = (p - starts[k] == j_col) & (p >= lows[k] + r * win)
            pieces.append(jnp.where(hit, 1.0, 0.0).astype(BF16))
        sel = jnp.concatenate(pieces, axis=0)
        rows = jnp.dot(sel, h_ref[sub * t:(sub + 1) * t, :], preferred_element_type=F32).astype(BF16)
        return starts, rows

    def deposit(starts, rows):
        for k in range(eg):
            dst = (0, k, pl.ds(starts[k], win), slice(None))
            xs_ref[dst] = xs_ref[dst] + rows[k * win:(k + 1) * win]

    plans = [_window_plan(off_ref, b, tb * nsub + sub, experts, ne) for sub in range(nsub)]
    firsts = [select(sub, plans[sub][0], 0) for sub in range(nsub)]
    for starts, rows in firsts:
        deposit(starts, rows)
    for sub in range(nsub):
        lows, ends = plans[sub]

        def extra_round(r, carry, lows=lows, sub=sub):
            deposit(*select(sub, lows, r))
            return carry

        lax.fori_loop(1, _window_rounds(lows, ends), extra_round, 0)


def _gatherx(off_flat, h2, pos, cap):
    n, d = h2.shape
    bsz, ne, l = pos.shape
    t = min(l, 2 * TOKEN_TILE)
    ntb = l // t
    eg = SUBLANES
    grid_spec = pltpu.PrefetchScalarGridSpec(
        num_scalar_prefetch=1,
        grid=(bsz, ne // eg, ntb),
        in_specs=[pl.BlockSpec((t, d), lambda b, g, i, off: (b * ntb + i, 0)),
                  pl.BlockSpec((1, eg, t), lambda b, g, i, off: (b, g, i))],
        out_specs=pl.BlockSpec((1, eg, cap, d), lambda b, g, i, off: (b, g, 0, 0)),
    )
    return pl.pallas_call(
        functools.partial(_gatherx_body, ne=ne, cap=cap, eg=eg),
        grid_spec=grid_spec,
        out_shape=jax.ShapeDtypeStruct((bsz, ne, cap, d), BF16),
        compiler_params=_cparams(("parallel", "parallel", "arbitrary")),
        name="gatherx",
    )(off_flat, h2, pos)


def _moe_body(xs_ref, wg_ref, wu_ref, wd_ref, y_ref, *, rc):
    wg = wg_ref[0].astype(BF16)
    wu = wu_ref[0].astype(BF16)
    wd = wd_ref[0].astype(BF16)
    cap = xs_ref.shape[2]
    acts = []
    for ch in range(cap // rc):
        xs = xs_ref[0, 0, ch * rc:(ch + 1) * rc, :]
        gate = jnp.dot(xs, wg, preferred_element_type=F32)
        up = jnp.dot(xs, wu, preferred_element_type=F32)
        acts.append((_silu(gate) * up).astype(BF16))
    for ch in range(cap // rc):
        y_ref[0, 0, ch * rc:(ch + 1) * rc, :] = jnp.dot(acts[ch], wd, preferred_element_type=F32).astype(BF16)


def _moe(xs, wg, wu, wd):
    bsz, ne, cap, d = xs.shape
    de = wg.shape[2]
    slot = pl.BlockSpec((1, 1, cap, d), lambda e, b: (b, e, 0, 0))
    return pl.pallas_call(
        functools.partial(_moe_body, rc=min(cap, 2 * MXU_DIM)),
        grid=(ne, bsz),
        in_specs=[slot,
                  pl.BlockSpec((1, d, de), lambda e, b: (e, 0, 0)),
                  pl.BlockSpec((1, d, de), lambda e, b: (e, 0, 0)),
                  pl.BlockSpec((1, de, d), lambda e, b: (e, 0, 0))],
        out_specs=slot,
        out_shape=jax.ShapeDtypeStruct((bsz, ne, cap, d), BF16),
        compiler_params=_cparams(("parallel", "arbitrary")),
        name="moe",
    )(xs, wg, wu, wd)


def _combine_body(off_ref, x1_ref, g2_ref, fg_ref, pos_ref, aff_ref, y_hbm, o_ref, ybuf, spare, acc_ref, sem,
                  spare_sem, *, ne, cap, nsp, nsub, span):
    b = pl.program_id(0)
    i = pl.program_id(1)
    step = b * nsp + i
    nsteps = pl.num_programs(0) * nsp
    slot = step % 2
    win = SLOT_WIN
    t = ROUTE_BLK
    experts = range(ne)

    def window_starts(lows, r):
        return [pl.multiple_of(jnp.minimum(lows[e] + r * win, cap - win), BF16_ROWS) for e in experts]

    def span_starts(bb, ii):
        lows = _window_plan(off_ref, bb, ii * nsub, experts, ne)[0]
        return [pl.multiple_of(jnp.minimum(lows[e], cap - span), BF16_ROWS) for e in experts]

    def span_copy(bb, e, start, buf):
        return pltpu.make_async_copy(y_hbm.at[bb, e, pl.ds(start, span), :], ybuf.at[buf, e], sem.at[buf, e])

    def spare_copy(e, start):
        return pltpu.make_async_copy(y_hbm.at[b, e, pl.ds(start, win), :],
                                     spare.at[pl.ds(e * win, win), :], spare_sem.at[e])

    def start_step(bb, ii, buf):
        starts = span_starts(bb, ii)
        for e in experts:
            span_copy(bb, e, starts[e], buf).start()

    @pl.when(step == 0)
    def _():
        start_step(b, i, 0)

    nxt = jnp.minimum(step + 1, nsteps - 1)
    start_step(lax.div(nxt, jnp.int32(nsp)), lax.rem(nxt, jnp.int32(nsp)), 1 - slot)

    plans = [_window_plan(off_ref, b, i * nsub + sub, experts, ne) for sub in range(nsub)]
    mine = span_starts(b, i)
    firsts = [window_starts(plans[sub][0], 0) for sub in range(nsub)]
    inside = []
    for sub in range(nsub):
        ok = jnp.bool_(True)
        for e in experts:
            ok = ok & (firsts[sub][e] >= mine[e]) & (firsts[sub][e] + win <= mine[e] + span)
        inside.append(ok)
    j_col = lax.broadcasted_iota(I32, (win, 1), 0)

    def expand(sub, r, starts, rows, enabled=None):
        lows = plans[sub][0]
        pieces = []
        for e in experts:
            p = pos_ref[0, e:e + 1, sub * t:(sub + 1) * t]
            valid = p >= lows[e] + r * win
            hit = (p - starts[e] == j_col) & valid
            if enabled is not None:
                hit = hit & enabled
            pieces.append(jnp.where(hit, aff_ref[0, e:e + 1, sub * t:(sub + 1) * t], 0.0).astype(BF16))
        pmat = jnp.concatenate(pieces, axis=0)
        return lax.dot_general(pmat, rows, _TN, preferred_element_type=F32)

    acc_ref[...] = jnp.zeros(acc_ref.shape, F32)
    for sub in range(nsub):
        lows, ends = plans[sub]

        def own_round(r, carry, lows=lows, sub=sub):
            starts = window_starts(lows, r)
            for e in experts:
                spare_copy(e, starts[e]).start()
            for e in experts:
                spare_copy(e, starts[e]).wait()
            acc_ref[sub] += expand(sub, r, starts, spare[...])
            return carry

        lax.fori_loop(jnp.where(inside[sub], 1, 0), _window_rounds(lows, ends), own_round, 0)

    for e in experts:
        span_copy(b, e, mine[e], slot).wait()
    for sub in range(nsub):
        rows = pl.ds(sub * t, t)
        picked = []
        for e in experts:
            local = pl.multiple_of(jnp.clip(firsts[sub][e] - mine[e], 0, span - win), BF16_ROWS)
            picked.append(ybuf[slot, e, pl.ds(local, win), :])
        moe = expand(sub, 0, firsts[sub], jnp.concatenate(picked, axis=0), inside[sub]) + acc_ref[sub]
        x2 = x1_ref[rows, :] + g2_ref[0] * moe
        ms = jnp.mean(x2 * x2, axis=-1, keepdims=True)
        o_ref[rows, :] = x2 * lax.rsqrt(ms + EPS) * fg_ref[...]

    @pl.when(step == nsteps - 1)
    def _():
        for e in experts:
            span_copy(b, e, mine[e], 1 - slot).wait()


def _combine(off_flat, x1, mods, fg, pos, aff, y, rows_per_sample):
    n, d = x1.shape
    bsz, ne, cap, _ = y.shape
    t = min(rows_per_sample, TOKEN_TILE)
    nsub = t // ROUTE_BLK
    nsp = rows_per_sample // t
    rows = ne * SLOT_WIN
    span = min(cap, nsub * SLOT_WIN * 7 // 8)
    grid_spec = pltpu.PrefetchScalarGridSpec(
        num_scalar_prefetch=1,
        grid=(bsz, nsp),
        in_specs=[pl.BlockSpec((t, d), lambda b, i, off: (b * nsp + i, 0)),
                  _mod_spec(d, MOD_GATE2, lambda b, i: b),
                  pl.BlockSpec((1, d), lambda b, i, off: (0, 0)),
                  pl.BlockSpec((1, ne, t), lambda b, i, off: (b, 0, i)),
                  pl.BlockSpec((1, ne, t), lambda b, i, off: (b, 0, i)),
                  pl.BlockSpec(memory_space=pl.ANY)],
        out_specs=pl.BlockSpec((t, d), lambda b, i, off: (b * nsp + i, 0)),
        scratch_shapes=[pltpu.VMEM((2, ne, span, d), BF16),
                        pltpu.VMEM((rows, d), BF16),
                        pltpu.VMEM((nsub, ROUTE_BLK, d), F32),
                        pltpu.SemaphoreType.DMA((2, ne)),
                        pltpu.SemaphoreType.DMA((ne,))],
    )
    return pl.pallas_call(
        functools.partial(_combine_body, ne=ne, cap=cap, nsp=nsp, nsub=nsub, span=span),
        grid_spec=grid_spec,
        out_shape=jax.ShapeDtypeStruct((n, d), F32),
        compiler_params=_cparams(("arbitrary", "arbitrary")),
        name="combine",
    )(off_flat, x1, mods, fg, pos, aff, y)


def kernel(x, c, ctx, c_ctx, ada_w, ada_b, norm1_g, norm2_g, w_in, w_decay_up, b_decay, gla_norm_g,
           w_gla_proj, pool_w, pool_scale, w_pool_proj, w_out, w_router, w_gate_e, w_up_e, w_down_e,
           final_norm_g):
    assert ada_w.shape[0] == 1, "single-layer block"
    bsz, l, d = x.shape
    lc = ctx.shape[1]
    rank, dk = w_decay_up.shape[2], w_decay_up.shape[3]
    dvh = gla_norm_g.shape[1]
    dv = w_gla_proj.shape[1]
    heads = dv // dvh
    groups, ch = pool_w.shape[1], pool_w.shape[2]
    pw = groups * ch
    ne = w_router.shape[2]
    cap = EC_CAPACITY * l // ne
    assert dk // heads == HEAD_DK and dvh == LANES and ch == LANES and heads % 2 == 0
    assert l % ROUTE_BLK == 0 and l // ROUTE_BLK < LANES and cap >= SLOT_WIN and cap % BF16_ROWS == 0
    assert GRID_W == GLA_CHUNK and MXU_DIM % GRID_W == 0 and (l // MXU_DIM) % 2 == 0

    assert bsz < COND_ROWS
    cin = jnp.zeros((COND_ROWS, d), F32).at[:bsz].set(c).at[bsz].set(c_ctx)
    mods = _ada(cin, ada_w[0], ada_b[0][None, :], jnp.concatenate([norm1_g, norm2_g], axis=0))

    o_r = dk + dv
    o_q = o_r + 2 * rank
    o_g = o_q + dk
    o_p = o_g + dv
    o_m = o_p + pw
    up = jnp.zeros((2 * rank, 2 * dk), F32)
    up = up.at[:rank, :dk].set(w_decay_up[0, 0]).at[rank:, dk:].set(w_decay_up[0, 1])
    psc = pool_scale[0].reshape(groups, 1, ch)
    w1, wm, wpool = _prep(w_in[0].T, up, pool_w[0], psc, w_pool_proj[0], o_r, o_q, o_g, o_m,
                          float(dk // heads) ** -0.5)
    bz = b_decay[0]

    zero_state = jnp.zeros((bsz, heads // 2, LANES, 2 * LANES), F32)
    ck, cv, _, claf, clab, _, _ = _inproj(ctx.reshape(bsz * lc, d), mods, lambda i: bsz, w1, bz, dk, dv, pw,
                                          min(lc, CTX_TILE))
    r3 = lambda a, n: a.reshape(bsz, n, a.shape[-1])
    _, _, h_f, h_b = _gla(r3(ck, lc), r3(ck, lc), r3(cv, lc), r3(claf, lc), r3(clab, lc),
                          zero_state, zero_state, min(lc, CTX_TILE))

    x2 = x.reshape(bsz * l, d)
    tm = min(l, TOKEN_TILE)
    k, v, q, laf, lab, sg, pin = _inproj(x2, mods, lambda i: i // (l // tm), w1, bz, dk, dv, pw, tm)
    of, ob, _, _ = _gla(r3(k, l), r3(q, l), r3(v, l), r3(laf, l), r3(lab, l), h_f, h_b, min(l, GLA_TILE))
    pooled = _pool(r3(pin, l), ch).reshape(bsz * l, pw)

    wr_hi = w_router[0].astype(BF16)
    wr_lo = (w_router[0] - wr_hi.astype(F32)).astype(BF16)
    wr = jnp.zeros((d, LANES), BF16).at[:, :ne].set(wr_hi).at[:, ne:2 * ne].set(wr_lo)
    x1, h2, aff = _merge(x2, mods, of.reshape(bsz * l, dv), ob.reshape(bsz * l, dv), sg, pooled,
                          wm, w_gla_proj[0].astype(BF16), wpool, w_out[0].astype(BF16),
                          gla_norm_g[0][None, :], wr, l, heads, ne, tm)

    pos, offs = _route(aff.reshape(1, bsz * ne, l), cap)
    pos = pos.reshape(bsz, ne, l)
    off_flat = offs.reshape(-1)
    xs = _gatherx(off_flat, h2, pos, cap)
    y = _moe(xs, w_gate_e[0], w_up_e[0], w_down_e[0])
    out = _combine(off_flat, x1, mods, final_norm_g[None, :], pos, aff, y, l)
    return out.reshape(bsz, l, d)
```

```python
import functools

import jax
import jax.numpy as jnp
from jax import lax
from jax.experimental import pallas as pl
from jax.experimental.pallas import tpu as pltpu

F32 = jnp.float32
BF16 = jnp.bfloat16
I32 = jnp.int32
HIGHEST = lax.Precision.HIGHEST

EPS = 1e-6
GRID_W = 64
GLA_CHUNK = 64
GLA_STAGES = 6
GLA_BATCH = 8
GATE_NORMALIZER = 16.0
POOL_WINDOWS = (2, 4, 8, 16)
EC_CAPACITY = 2

LANES = 128
SUBLANES = 8
BF16_ROWS = 16
MXU_DIM = 256
VMEM_BYTES = 64 * 1024 * 1024
VMEM_LIMIT = VMEM_BYTES * 7 // 8

HEAD_DK = LANES // 2
ROUTE_BLK = MXU_DIM
SLOT_WIN = 64
SUB_ROWS = MXU_DIM
TOKEN_TILE = 1024
GLA_TILE = 2048
CTX_TILE = 256
COND_ROWS = SUBLANES
MOD_SHIFT1, MOD_MULT1, MOD_GATE1, MOD_SHIFT2, MOD_MULT2, MOD_GATE2 = range(6)


def _mod_spec(d, which, sample_of):
    return pl.BlockSpec((1, 1, d), lambda *g: (which * COND_ROWS + sample_of(*g[:2]), 0, 0))

_NT = (((1,), (1,)), ((), ()))
_TN = (((0,), (0,)), ((), ()))


def _cparams(sem):
    return pltpu.CompilerParams(dimension_semantics=sem, vmem_limit_bytes=VMEM_LIMIT)


def _sigmoid(x):
    return 0.5 * jnp.tanh(0.5 * x) + 0.5


def _silu(x):
    return x * _sigmoid(x)


def _row_groups(n):
    step = min(n, SUB_ROWS)
    return [pl.ds(i, step) for i in range(0, n, step)]


def _staggered(gens, nstages):
    for t in range(nstages + len(gens) - 1):
        for g in reversed(range(len(gens))):
            if 0 <= t - g < nstages:
                next(gens[g])


def _ada_body(c_ref, w_ref, b_ref, g_ref, o_ref):
    s = _silu(c_ref[...])
    rows = s.shape[0]
    s_hi = s.astype(BF16)
    s_lo = (s - s_hi.astype(F32)).astype(BF16)
    w = w_ref[...]
    w_hi = w.astype(BF16)
    w_lo = (w - w_hi.astype(F32)).astype(BF16)
    both = jnp.dot(jnp.concatenate([s_hi, s_lo], axis=0), w_hi, preferred_element_type=F32)
    out = (both[:rows] + both[rows:] + jnp.dot(s_hi, w_lo, preferred_element_type=F32)) + b_ref[...]
    j = pl.program_id(0)
    gain = jnp.where(j == MOD_MULT1, g_ref[0:1, :], g_ref[1:2, :])
    o_ref[...] = jnp.where((j == MOD_MULT1) | (j == MOD_MULT2), gain * (1.0 + out), out)


def _ada(cin, w, b, gains):
    rows, d = cin.shape
    nvec = w.shape[1] // d
    mods = pl.pallas_call(
        _ada_body,
        grid=(nvec,),
        in_specs=[pl.BlockSpec((rows, d), lambda j: (0, 0)),
                  pl.BlockSpec((d, d), lambda j: (0, j)),
                  pl.BlockSpec((1, d), lambda j: (0, j)),
                  pl.BlockSpec(gains.shape, lambda j: (0, 0))],
        out_specs=pl.BlockSpec((rows, d), lambda j: (j, 0)),
        out_shape=jax.ShapeDtypeStruct((nvec * rows, d), F32),
        compiler_params=_cparams(("arbitrary",)),
        name="ada",
    )(cin, w, b, gains)
    return mods.reshape(nvec * rows, 1, d)


def _prep_body(w_ref, up_ref, pw_ref, ps_ref, wpp_ref, w1_ref, wm_ref, wpf_ref, *, o_r, o_q, o_g, o_m, qscale):
    dk = o_g - o_q
    wz = lax.dot_general(w_ref[o_r:o_q, :], up_ref[...], _TN, precision=HIGHEST, preferred_element_type=F32)
    nz = wz.shape[1]
    w1_ref[:, :o_r] = w_ref[:o_r, :].T.astype(BF16)
    w1_ref[:, o_r:o_r + dk] = (w_ref[o_q:o_g, :].T * qscale).astype(BF16)
    w1_ref[:, o_r + dk:o_r + dk + nz] = wz.astype(BF16)
    w1_ref[:, o_r + dk + nz:] = w_ref[o_g:o_m, :].T.astype(BF16)
    wm_ref[...] = w_ref[o_m:, :].T.astype(BF16)
    wpf_ref[...] = jnp.dot(pw_ref[0] * ps_ref[0], wpp_ref[...], precision=HIGHEST,
                           preferred_element_type=F32).astype(BF16)


def _prep(w_in_t, up, pool_w, pool_scale3, w_pool_proj, o_r, o_q, o_g, o_m, qscale):
    n, d = w_in_t.shape
    n1 = o_r + (o_g - o_q) + up.shape[1] + (o_m - o_g)
    groups, ch, _ = pool_w.shape
    tr = d // groups
    dm = w_pool_proj.shape[1]
    return pl.pallas_call(
        functools.partial(_prep_body, o_r=o_r, o_q=o_q, o_g=o_g, o_m=o_m, qscale=qscale),
        grid=(d // tr,),
        in_specs=[pl.BlockSpec((n, tr), lambda i: (0, i)),
                  pl.BlockSpec(up.shape, lambda i: (0, 0)),
                  pl.BlockSpec((1, ch, ch), lambda i: (i, 0, 0)),
                  pl.BlockSpec((1, 1, ch), lambda i: (i, 0, 0)),
                  pl.BlockSpec((ch, dm), lambda i: (i, 0))],
        out_specs=[pl.BlockSpec((tr, n1), lambda i: (i, 0)),
                   pl.BlockSpec((tr, n - o_m), lambda i: (i, 0)),
                   pl.BlockSpec((ch, dm), lambda i: (i, 0))],
        out_shape=[jax.ShapeDtypeStruct((d, n1), BF16), jax.ShapeDtypeStruct((d, n - o_m), BF16),
                   jax.ShapeDtypeStruct((groups * ch, dm), BF16)],
        compiler_params=_cparams(("parallel",)),
        name="prep",
    )(w_in_t, up, pool_w, pool_scale3, w_pool_proj)


def _modulated_norm(x, mult, shift):
    ms = jnp.mean(x * x, axis=-1, keepdims=True)
    return (x * lax.rsqrt(ms + EPS)) * mult + shift


def _log_sigmoid(z):
    return jnp.minimum(z, 0.0) - jnp.log1p(jnp.exp(-jnp.abs(z)))


def _inproj_body(x_ref, mult_ref, shift_ref, w_ref, bz_ref,
                 k_ref, v_ref, q_ref, laf_ref, lab_ref, sg_ref, p_ref, *, dk, dv, pw):
    def sub_tile(rows):
        h = _modulated_norm(x_ref[rows, :], mult_ref[0], shift_ref[0]).astype(BF16)
        yield
        u = jnp.dot(h, w_ref[...], preferred_element_type=F32)
        yield
        o = 0
        k_ref[rows, :] = u[:, o:o + dk].astype(BF16); o += dk
        v_ref[rows, :] = u[:, o:o + dv].astype(BF16); o += dv
        q_ref[rows, :] = u[:, o:o + dk].astype(BF16); o += dk
        zf = u[:, o:o + dk] + bz_ref[0:1, :]; o += dk
        zb = u[:, o:o + dk] + bz_ref[1:2, :]; o += dk
        laf_ref[rows, :] = _log_sigmoid(zf) * (1.0 / GATE_NORMALIZER)
        lab_ref[rows, :] = _log_sigmoid(zb) * (1.0 / GATE_NORMALIZER)
        sg_ref[rows, :] = _silu(u[:, o:o + dv]).astype(BF16); o += dv
        p_ref[rows, :] = u[:, o:o + pw].astype(BF16)
        yield

    _staggered([sub_tile(r) for r in _row_groups(x_ref.shape[0])], 3)


def _inproj(x2, mods, cond_row, w, bz, dk, dv, pw, tm):
    n, d = x2.shape
    row = lambda i: (i, 0)
    sample_of = lambda i, *_: cond_row(i)
    outs = [(dk, BF16), (dv, BF16), (dk, BF16), (dk, F32), (dk, F32), (dv, BF16), (pw, BF16)]
    return pl.pallas_call(
        functools.partial(_inproj_body, dk=dk, dv=dv, pw=pw),
        grid=(n // tm,),
        in_specs=[pl.BlockSpec((tm, d), row),
                  _mod_spec(d, MOD_MULT1, sample_of),
                  _mod_spec(d, MOD_SHIFT1, sample_of),
                  pl.BlockSpec(w.shape, lambda i: (0, 0), pipeline_mode=pl.Buffered(1)),
                  pl.BlockSpec(bz.shape, lambda i: (0, 0))],
        out_specs=[pl.BlockSpec((tm, c), row) for c, _ in outs],
        out_shape=[jax.ShapeDtypeStruct((n, c), t) for c, t in outs],
        compiler_params=_cparams(("parallel",)),
        name="inproj",
    )(x2, mods, mods, w, bz)


def _gla_direction(k_ref, q_ref, v_ref, la_ref, o_ref, s_ref, reverse, nchunk):
    C = GLA_CHUNK
    hd = HEAD_DK
    cb = MXU_DIM
    lt = nchunk * C
    la = la_ref[0]
    r = lax.broadcasted_iota(I32, (cb, cb), 0)
    c = lax.broadcasted_iota(I32, (cb, cb), 1)
    same = (r // C) == (c // C)
    cum = jnp.where(same & ((c >= r) if reverse else (c <= r)), 1.0, 0.0).astype(BF16)
    la_hi = la.astype(BF16)
    la_lo = (la - la_hi.astype(F32)).astype(BF16)
    la2 = jnp.concatenate([la_hi, la_lo], axis=1)
    bcs = []
    for blk in range(lt // cb):
        part = jnp.dot(cum, la2[blk * cb:(blk + 1) * cb], preferred_element_type=F32)
        bcs.append(part[:, :LANES] + part[:, LANES:])
    ri = lax.broadcasted_iota(I32, (C, 2 * C), 0)
    ci = lax.broadcasted_iota(I32, (C, 2 * C), 1) % C
    tri = (ci >= ri) if reverse else (ci <= ri)
    lane = lax.broadcasted_iota(I32, (1, LANES), 1)
    m0 = (lane < hd).astype(F32)
    m1 = (lane >= hd).astype(F32)
    sr = lax.broadcasted_iota(I32, (LANES, 2 * LANES), 0)
    sl = lax.broadcasted_iota(I32, (LANES, 2 * LANES), 1)
    smask = ((sr < hd) == (sl < LANES)).astype(F32)
    kt = k_ref[0].astype(F32)
    qt = q_ref[0].astype(F32)
    zero_v = jnp.zeros((C, LANES), BF16)
    sweep = list(range(nchunk - 1, -1, -1) if reverse else range(nchunk))
    for first in range(0, nchunk, GLA_BATCH):
        order = sweep[first:first + GLA_BATCH]
        intra, qds, kvs, decs, q2s, kss, kws, scs = {}, {}, {}, {}, {}, {}, {}, {}
        for ch in order:
            lo = ch * C
            b = bcs[lo // cb][lo % cb:lo % cb + C]
            last = b[0:1] if reverse else b[C - 1:C]
            mid = b[C // 2:C // 2 + 1] if reverse else b[C // 2 - 1:C // 2]
            kc = kt[lo:lo + C]
            qc = qt[lo:lo + C]
            q2s[ch] = (qc * jnp.exp(b - mid)).astype(BF16)
            ks = kc * jnp.exp(mid - b)
            kss[ch] = jnp.concatenate([ks * m0, ks * m1], axis=0).astype(BF16)
            qds[ch] = (qc * jnp.exp(b)).astype(BF16)
            kws[ch] = (kc * jnp.exp(last - b)).astype(BF16)
            decs[ch] = last
        yield
        for ch in order:
            scs[ch] = lax.dot_general(q2s[ch], kss[ch], _NT, preferred_element_type=F32)
        yield
        for ch in order:
            v2 = v_ref[0, ch * C:(ch + 1) * C, :]
            kvs[ch] = lax.dot_general(kws[ch], v2, _TN, preferred_element_type=F32) * smask
        yield
        for ch in order:
            sc2 = jnp.where(tri, scs[ch], 0.0).astype(BF16)
            v2 = v_ref[0, ch * C:(ch + 1) * C, :]
            vbd = jnp.concatenate([jnp.concatenate([v2[:, :LANES], zero_v], axis=1),
                                   jnp.concatenate([zero_v, v2[:, LANES:]], axis=1)], axis=0)
            intra[ch] = jnp.dot(sc2, vbd, preferred_element_type=F32)
        yield
        pad = jnp.zeros((LANES - len(order), LANES), F32)
        dec_cols = jnp.exp(jnp.concatenate([decs[ch] for ch in order] + [pad], axis=0).T)
        st = s_ref[...]
        starts = {}
        for i, ch in enumerate(order):
            starts[ch] = st.astype(BF16)
            st = st * dec_cols[:, i:i + 1] + kvs[ch]
        s_ref[...] = st
        yield
        for ch in order:
            inter = jnp.dot(qds[ch], starts[ch], preferred_element_type=F32)
            o_ref[0, ch * C:(ch + 1) * C, :] = inter + intra[ch]
        yield


def _gla_body(kf, qf, vf, laf, kb, qb, vb, lab, h0f, h0b, of, ob, hf_out, hb_out, sf, sb, *, nchunk):
    i = pl.program_id(2)

    @pl.when(i == 0)
    def _():
        sf[...] = h0f[0, 0]
        sb[...] = h0b[0, 0]

    sweeps = [_gla_direction(kf, qf, vf, laf, of, sf, False, nchunk),
              _gla_direction(kb, qb, vb, lab, ob, sb, True, nchunk)]
    for _ in range(GLA_STAGES * pl.cdiv(nchunk, GLA_BATCH)):
        for sweep in sweeps:
            next(sweep)

    @pl.when(i == pl.num_programs(2) - 1)
    def _():
        hf_out[0, 0] = sf[...]
        hb_out[0, 0] = sb[...]


def _gla(k, q, v, laf, lab, h0f, h0b, lt):
    bsz, l, _ = k.shape
    pairs = h0f.shape[1]
    nt = l // lt
    fwd = lambda b, hp, i: (b, i, hp)
    bwd = lambda b, hp, i: (b, nt - 1 - i, hp)
    st = lambda b, hp, i: (b, hp, 0, 0)
    kq = lambda m: pl.BlockSpec((1, lt, LANES), m)
    vv = lambda m: pl.BlockSpec((1, lt, 2 * LANES), m)
    sspec = pl.BlockSpec((1, 1, LANES, 2 * LANES), st)
    return pl.pallas_call(
        functools.partial(_gla_body, nchunk=lt // GLA_CHUNK),
        grid=(bsz, pairs, nt),
        in_specs=[kq(fwd), kq(fwd), vv(fwd), kq(fwd), kq(bwd), kq(bwd), vv(bwd), kq(bwd), sspec, sspec],
        out_specs=[vv(fwd), vv(bwd), sspec, sspec],
        out_shape=[jax.ShapeDtypeStruct(v.shape, F32), jax.ShapeDtypeStruct(v.shape, F32),
                   jax.ShapeDtypeStruct(h0f.shape, F32), jax.ShapeDtypeStruct(h0b.shape, F32)],
        scratch_shapes=[pltpu.VMEM((LANES, 2 * LANES), F32), pltpu.VMEM((LANES, 2 * LANES), F32)],
        compiler_params=_cparams(("parallel", "parallel", "arbitrary")),
        name="gla",
    )(k, q, v, laf, k, q, v, lab, h0f, h0b)


def _pool_body(p_ref, o_ref, s1_ref, *, half, rows):
    xb = p_ref[0]
    l, ch = xb.shape

    def inv_counts(pos, n):
        return 1.0 / (jnp.minimum(pos + half, n) - jnp.maximum(pos - half, 0)).astype(F32)

    blk = MXU_DIM
    r = lax.broadcasted_iota(I32, (blk, blk), 0)
    c = lax.broadcasted_iota(I32, (blk, blk), 1)
    band = ((r // GRID_W == c // GRID_W) & (c - r >= -half) & (c - r <= half - 1)).astype(BF16)
    for i in range(0, l // blk, 2):
        pair = jnp.concatenate([xb[i * blk:(i + 1) * blk], xb[(i + 1) * blk:(i + 2) * blk]], axis=1)
        sums = jnp.dot(band, pair, preferred_element_type=F32)
        s1_ref[i * blk:(i + 1) * blk, :] = sums[:, :ch]
        s1_ref[(i + 1) * blk:(i + 2) * blk, :] = sums[:, ch:]
    s1 = s1_ref[...].reshape(rows, GRID_W, ch)

    def shifted(a, s):
        z = jnp.zeros((abs(s),) + a.shape[1:], a.dtype)
        return jnp.concatenate([a[s:], z], axis=0) if s > 0 else jnp.concatenate([z, a[:s]], axis=0)

    fwd = s1
    bwd = shifted(s1, -1)
    s = 1
    while s < half:
        fwd = fwd + shifted(fwd, s)
        bwd = bwd + shifted(bwd, -s)
        s *= 2
    inv_r = inv_counts(lax.broadcasted_iota(I32, (rows, 1, ch), 0), rows)
    inv_c = inv_counts(lax.broadcasted_iota(I32, (1, GRID_W, ch), 1), GRID_W)
    pooled = (fwd + bwd) * inv_r * inv_c - xb.astype(F32).reshape(rows, GRID_W, ch)
    o_ref[0] = pooled.reshape(l, ch).astype(BF16)


def _pool_groups_body(p_ref, o_ref, s1_ref, *, rows):
    for gi, window in enumerate(POOL_WINDOWS):
        @pl.when(pl.program_id(1) == gi)
        def _(half=window // 2):
            _pool_body(p_ref, o_ref, s1_ref, half=half, rows=rows)


def _pool(pin, ch):
    bsz, l, pw = pin.shape
    return pl.pallas_call(
        functools.partial(_pool_groups_body, rows=l // GRID_W),
        grid=(bsz, pw // ch),
        in_specs=[pl.BlockSpec((1, l, ch), lambda b, g: (b, 0, g))],
        out_specs=pl.BlockSpec((1, l, ch), lambda b, g: (b, 0, g)),
        out_shape=jax.ShapeDtypeStruct((bsz, l, pw), BF16),
        scratch_shapes=[pltpu.VMEM((l, ch), F32)],
        compiler_params=_cparams(("parallel", "parallel")),
        name="pool",
    )(pin)


def _merge_body(x_ref, m1_ref, s1_ref, g1_ref, m2_ref, s2_ref,
                of_ref, ob_ref, sg_ref, mx_ref,
                wm_ref, wgla_ref, wpool_ref, wout_ref, gn_ref, wr_ref,
                x1_ref, h2_ref, aff_ref, *, heads, ne):
    d = x_ref.shape[1]

    def sub_tile(rows):
        bp = jnp.dot(mx_ref[rows, :], wpool_ref[...], preferred_element_type=F32)
        x = x_ref[rows, :]
        h = _modulated_norm(x, m1_ref[0], s1_ref[0]).astype(BF16)
        o = of_ref[rows, :] + ob_ref[rows, :]
        sg = sg_ref[rows, :].astype(F32)
        og = []
        for j in range(heads):
            oj = o[:, j * LANES:(j + 1) * LANES]
            oj = oj * lax.rsqrt(jnp.mean(oj * oj, axis=-1, keepdims=True) + EPS) * gn_ref[...]
            og.append((oj * sg[:, j * LANES:(j + 1) * LANES]).astype(BF16))
        og = jnp.concatenate(og, axis=1)
        yield
        gates = jnp.dot(h, wm_ref[...], preferred_element_type=F32)
        bg = jnp.dot(og, wgla_ref[...], preferred_element_type=F32)
        yield
        gates = _sigmoid(gates)
        z = (gates[:, :d] * bg + gates[:, d:] * bp).astype(BF16)
        yield
        y = jnp.dot(z, wout_ref[...], preferred_element_type=F32)
        yield
        x1 = x + g1_ref[0] * y
        x1_ref[rows, :] = x1
        h2 = _modulated_norm(x1, m2_ref[0], s2_ref[0])
        hi = h2.astype(BF16)
        h2_ref[rows, :] = hi
        lo = (h2 - hi.astype(F32)).astype(BF16)
        yield
        lg = (jnp.dot(hi, wr_ref[...], preferred_element_type=F32)
              + jnp.dot(lo, wr_ref[...], preferred_element_type=F32))
        yield
        lgt = lg.T
        logit = lgt[0:ne] + lgt[ne:2 * ne]
        mx = jnp.max(logit, axis=0, keepdims=True)
        ex = jnp.exp(logit - mx)
        aff_ref[0, :, rows] = ex / jnp.sum(ex, axis=0, keepdims=True)
        yield

    _staggered([sub_tile(r) for r in _row_groups(x_ref.shape[0])], 7)


def _merge(x2, mods, of, ob, sg, mixed, wm, wgla, wpool, wout, gn, wr, rows_per_sample, heads, ne, tm):
    n, d = x2.shape
    bsz = n // rows_per_sample
    tps = rows_per_sample // tm
    row = lambda i: (i, 0)
    sample_of = lambda i, *_: i // tps
    vecs = [MOD_MULT1, MOD_SHIFT1, MOD_GATE1, MOD_MULT2, MOD_SHIFT2]
    full = lambda a: pl.BlockSpec(a.shape, lambda i: (0,) * a.ndim, pipeline_mode=pl.Buffered(1))
    gv = of.shape[-1]
    return pl.pallas_call(
        functools.partial(_merge_body, heads=heads, ne=ne),
        grid=(n // tm,),
        in_specs=[pl.BlockSpec((tm, d), row)] + [_mod_spec(d, v, sample_of) for v in vecs]
                 + [pl.BlockSpec((tm, gv), row)] * 4
                 + [full(wm), full(wgla), full(wpool), full(wout), full(gn), full(wr)],
        out_specs=[pl.BlockSpec((tm, d), row),
                   pl.BlockSpec((tm, d), row),
                   pl.BlockSpec((1, ne, tm), lambda i: (i // tps, 0, i % tps))],
        out_shape=[jax.ShapeDtypeStruct((n, d), F32),
                   jax.ShapeDtypeStruct((n, d), BF16),
                   jax.ShapeDtypeStruct((bsz, ne, rows_per_sample), F32)],
        compiler_params=_cparams(("parallel",)),
        name="merge",
    )(x2, *[mods] * len(vecs), of, ob, sg, mixed, wm, wgla, wpool, wout, gn, wr)


def _route_body(aff_ref, pos_ref, off_ref, cnt_ref, *, cap, ntb):
    a = aff_ref[0]
    ne, l = a.shape
    blk = ROUTE_BLK

    def bisect(i, v):
        cand = v | jnp.left_shift(jnp.int32(1), 30 - i)
        cnt = jnp.sum((a >= lax.bitcast_convert_type(cand, F32)).astype(F32), axis=1, keepdims=True)
        return jnp.where(cnt >= cap, cand, v)

    thr = lax.bitcast_convert_type(lax.fori_loop(0, 31, bisect, jnp.zeros((ne, 1), I32)), F32)
    gt = a > thr
    tie = a == thr
    need = cap - jnp.sum(gt.astype(F32), axis=1, keepdims=True)

    r = lax.broadcasted_iota(I32, (blk, blk), 0)
    c = lax.broadcasted_iota(I32, (blk, blk), 1)
    upper = (r <= c).astype(BF16)
    lane = lax.broadcasted_iota(I32, (1, LANES), 1)

    def prefix(mask_f):
        run = jnp.zeros((ne, 1), F32)
        offs = jnp.zeros((ne, LANES), F32)
        for tb in range(ntb):
            m = mask_f[:, tb * blk:(tb + 1) * blk].astype(BF16)
            loc = jnp.dot(m, upper, preferred_element_type=F32)
            cnt_ref[:, tb * blk:(tb + 1) * blk] = loc + run
            offs = jnp.where(lane == tb, run, offs)
            run = run + loc[:, blk - 1:blk]
        return jnp.where(lane >= ntb, run, offs)

    tie_f = tie.astype(F32)
    prefix(tie_f)
    tie_excl = cnt_ref[...] - tie_f
    sel = gt | (tie & (tie_excl < need))
    offs = prefix(sel.astype(F32))
    pos_ref[0] = jnp.where(sel, cnt_ref[...] - 1.0, -1.0).astype(I32)
    off_ref[0] = offs.astype(I32)


def _route(aff, cap):
    bsz, ne, l = aff.shape
    spec = lambda s: pl.BlockSpec((1,) + s, lambda b: (b, 0, 0))
    return pl.pallas_call(
        functools.partial(_route_body, cap=cap, ntb=l // ROUTE_BLK),
        grid=(bsz,),
        in_specs=[spec((ne, l))],
        out_specs=[spec((ne, l)), spec((ne, LANES))],
        out_shape=[jax.ShapeDtypeStruct((bsz, ne, l), I32),
                   jax.ShapeDtypeStruct((bsz, ne, LANES), I32)],
        scratch_shapes=[pltpu.VMEM((ne, l), F32)],
        compiler_params=_cparams(("parallel",)),
        name="route",
    )(aff)


def _window_plan(off_ref, bb, tt, experts, ne):
    lows = [off_ref[(bb * ne + e) * LANES + tt] & -BF16_ROWS for e in experts]
    ends = [off_ref[(bb * ne + e) * LANES + tt + 1] for e in experts]
    return lows, ends


def _window_rounds(lows, ends):
    rounds = jnp.int32(0)
    for lo, hi in zip(lows, ends):
        rounds = jnp.maximum(rounds, lax.div(hi - lo + (SLOT_WIN - 1), jnp.int32(SLOT_WIN)))
    return rounds


def _gatherx_body(off_ref, h_ref, pos_ref, xs_ref, *, ne, cap, eg):
    b = pl.program_id(0)
    g = pl.program_id(1)
    tb = pl.program_id(2)
    win = SLOT_WIN

    @pl.when(tb == 0)
    def _():
        xs_ref[...] = jnp.zeros(xs_ref.shape, BF16)

    j_col = lax.broadcasted_iota(I32, (win, 1), 0)
    t = ROUTE_BLK
    nsub = h_ref.shape[0] // t
    experts = [g * eg + k for k in range(eg)]

    def select(sub, lows, r):
        starts = [pl.multiple_of(jnp.minimum(lows[k] + r * win, cap - win), BF16_ROWS) for k in range(eg)]
        pieces = []
        for k in range(eg):
            p = pos_ref[0, k:k + 1, sub * t:(sub + 1) * t]
            hit = (p - starts[k] == j_col) & (p >= lows[k] + r * win)
            pieces.append(jnp.where(hit, 1.0, 0.0).astype(BF16))
        sel = jnp.concatenate(pieces, axis=0)
        rows = jnp.dot(sel, h_ref[sub * t:(sub + 1) * t, :], preferred_element_type=F32).astype(BF16)
        return starts, rows

    def deposit(starts, rows):
        for k in range(eg):
            dst = (0, k, pl.ds(starts[k], win), slice(None))
            xs_ref[dst] = xs_ref[dst] + rows[k * win:(k + 1) * win]

    plans = [_window_plan(off_ref, b, tb * nsub + sub, experts, ne) for sub in range(nsub)]
    firsts = [select(sub, plans[sub][0], 0) for sub in range(nsub)]
    for starts, rows in firsts:
        deposit(starts, rows)
    for sub in range(nsub):
        lows, ends = plans[sub]

        def extra_round(r, carry, lows=lows, sub=sub):
            deposit(*select(sub, lows, r))
            return carry

        lax.fori_loop(1, _window_rounds(lows, ends), extra_round, 0)


def _gatherx(off_flat, h2, pos, cap):
    n, d = h2.shape
    bsz, ne, l = pos.shape
    t = min(l, 2 * TOKEN_TILE)
    ntb = l // t
    eg = SUBLANES
    grid_spec = pltpu.PrefetchScalarGridSpec(
        num_scalar_prefetch=1,
        grid=(bsz, ne // eg, ntb),
        in_specs=[pl.BlockSpec((t, d), lambda b, g, i, off: (b * ntb + i, 0)),
                  pl.BlockSpec((1, eg, t), lambda b, g, i, off: (b, g, i))],
        out_specs=pl.BlockSpec((1, eg, cap, d), lambda b, g, i, off: (b, g, 0, 0)),
    )
    return pl.pallas_call(
        functools.partial(_gatherx_body, ne=ne, cap=cap, eg=eg),
        grid_spec=grid_spec,
        out_shape=jax.ShapeDtypeStruct((bsz, ne, cap, d), BF16),
        compiler_params=_cparams(("parallel", "parallel", "arbitrary")),
        name="gatherx",
    )(off_flat, h2, pos)


def _moe_body(xs_ref, wg_ref, wu_ref, wd_ref, y_ref, *, rc):
    wg = wg_ref[0].astype(BF16)
    wu = wu_ref[0].astype(BF16)
    wd = wd_ref[0].astype(BF16)
    cap = xs_ref.shape[2]
    acts = []
    for ch in range(cap // rc):
        xs = xs_ref[0, 0, ch * rc:(ch + 1) * rc, :]
        gate = jnp.dot(xs, wg, preferred_element_type=F32)
        up = jnp.dot(xs, wu, preferred_element_type=F32)
        acts.append((_silu(gate) * up).astype(BF16))
    for ch in range(cap // rc):
        y_ref[0, 0, ch * rc:(ch + 1) * rc, :] = jnp.dot(acts[ch], wd, preferred_element_type=F32).astype(BF16)


def _moe(xs, wg, wu, wd):
    bsz, ne, cap, d = xs.shape
    de = wg.shape[2]
    slot = pl.BlockSpec((1, 1, cap, d), lambda e, b: (b, e, 0, 0))
    return pl.pallas_call(
        functools.partial(_moe_body, rc=min(cap, 2 * MXU_DIM)),
        grid=(ne, bsz),
        in_specs=[slot,
                  pl.BlockSpec((1, d, de), lambda e, b: (e, 0, 0)),
                  pl.BlockSpec((1, d, de), lambda e, b: (e, 0, 0)),
                  pl.BlockSpec((1, de, d), lambda e, b: (e, 0, 0))],
        out_specs=slot,
        out_shape=jax.ShapeDtypeStruct((bsz, ne, cap, d), BF16),
        compiler_params=_cparams(("parallel", "arbitrary")),
        name="moe",
    )(xs, wg, wu, wd)


def _combine_body(off_ref, x1_hbm, g2_ref, fg_ref, pos_ref, aff_ref, y_hbm, o_ref, ybuf, spare, acc_ref, x1buf,
                  sem, spare_sem, x1sem, *, ne, cap, nsp, nsub, span, nsteps):
    b = pl.program_id(0)
    i = pl.program_id(1)
    step = b * nsp + i
    slot = step % 2
    win = SLOT_WIN
    t = ROUTE_BLK
    experts = range(ne)

    def x1_copy(s):
        rows = pl.ds(pl.multiple_of(s * (nsub * t), nsub * t), nsub * t)
        return pltpu.make_async_copy(x1_hbm.at[rows, :], x1buf.at[s % 3], x1sem.at[s % 3])

    @pl.when(step == 0)
    def _():
        for s in range(min(2, nsteps)):
            x1_copy(s).start()

    @pl.when(step + 2 < nsteps)
    def _():
        x1_copy(step + 2).start()

    def window_starts(lows, r):
        return [pl.multiple_of(jnp.minimum(lows[e] + r * win, cap - win), BF16_ROWS) for e in experts]

    def span_starts(bb, ii):
        lows = _window_plan(off_ref, bb, ii * nsub, experts, ne)[0]
        return [pl.multiple_of(jnp.minimum(lows[e], cap - span), BF16_ROWS) for e in experts]

    def span_copy(bb, e, start, buf):
        return pltpu.make_async_copy(y_hbm.at[bb, e, pl.ds(start, span), :], ybuf.at[buf, e], sem.at[buf, e])

    def spare_copy(e, start):
        return pltpu.make_async_copy(y_hbm.at[b, e, pl.ds(start, win), :],
                                     spare.at[pl.ds(e * win, win), :], spare_sem.at[e])

    def start_step(bb, ii, buf):
        starts = span_starts(bb, ii)
        for e in experts:
            span_copy(bb, e, starts[e], buf).start()

    @pl.when(step == 0)
    def _():
        start_step(b, i, 0)

    nxt = jnp.minimum(step + 1, nsteps - 1)
    start_step(lax.div(nxt, jnp.int32(nsp)), lax.rem(nxt, jnp.int32(nsp)), 1 - slot)

    plans = [_window_plan(off_ref, b, i * nsub + sub, experts, ne) for sub in range(nsub)]
    mine = span_starts(b, i)
    firsts = [window_starts(plans[sub][0], 0) for sub in range(nsub)]
    inside = []
    for sub in range(nsub):
        ok = jnp.bool_(True)
        for e in experts:
            ok = ok & (firsts[sub][e] >= mine[e]) & (firsts[sub][e] + win <= mine[e] + span)
        inside.append(ok)
    j_col = lax.broadcasted_iota(I32, (win, 1), 0)

    def expand(sub, r, starts, rows, enabled=None):
        lows = plans[sub][0]
        pieces = []
        for e in experts:
            p = pos_ref[0, e:e + 1, sub * t:(sub + 1) * t]
            valid = p >= lows[e] + r * win
            hit = (p - starts[e] == j_col) & valid
            if enabled is not None:
                hit = hit & enabled
            pieces.append(jnp.where(hit, aff_ref[0, e:e + 1, sub * t:(sub + 1) * t], 0.0).astype(BF16))
        pmat = jnp.concatenate(pieces, axis=0)
        return lax.dot_general(pmat, rows, _TN, preferred_element_type=F32)

    acc_ref[...] = jnp.zeros(acc_ref.shape, F32)
    for sub in range(nsub):
        lows, ends = plans[sub]

        def own_round(r, carry, lows=lows, sub=sub):
            starts = window_starts(lows, r)
            for e in experts:
                spare_copy(e, starts[e]).start()
            for e in experts:
                spare_copy(e, starts[e]).wait()
            acc_ref[sub] += expand(sub, r, starts, spare[...])
            return carry

        lax.fori_loop(jnp.where(inside[sub], 1, 0), _window_rounds(lows, ends), own_round, 0)

    for e in experts:
        span_copy(b, e, mine[e], slot).wait()
    x1_copy(step).wait()
    for sub in range(nsub):
        rows = pl.ds(sub * t, t)
        picked = []
        for e in experts:
            local = pl.multiple_of(jnp.clip(firsts[sub][e] - mine[e], 0, span - win), BF16_ROWS)
            picked.append(ybuf[slot, e, pl.ds(local, win), :])
        moe = expand(sub, 0, firsts[sub], jnp.concatenate(picked, axis=0), inside[sub]) + acc_ref[sub]
        x2 = x1buf[step % 3, rows, :] + g2_ref[0] * moe
        ms = jnp.mean(x2 * x2, axis=-1, keepdims=True)
        o_ref[rows, :] = x2 * lax.rsqrt(ms + EPS) * fg_ref[...]

    @pl.when(step == nsteps - 1)
    def _():
        for e in experts:
            span_copy(b, e, mine[e], 1 - slot).wait()


def _combine(off_flat, x1, mods, fg, pos, aff, y, rows_per_sample):
    n, d = x1.shape
    bsz, ne, cap, _ = y.shape
    t = min(rows_per_sample, TOKEN_TILE)
    nsub = t // ROUTE_BLK
    nsp = rows_per_sample // t
    rows = ne * SLOT_WIN
    span = min(cap, nsub * SLOT_WIN * 3 // 4)
    grid_spec = pltpu.PrefetchScalarGridSpec(
        num_scalar_prefetch=1,
        grid=(bsz, nsp),
        in_specs=[pl.BlockSpec(memory_space=pl.ANY),
                  _mod_spec(d, MOD_GATE2, lambda b, i: b),
                  pl.BlockSpec((1, d), lambda b, i, off: (0, 0)),
                  pl.BlockSpec((1, ne, t), lambda b, i, off: (b, 0, i)),
                  pl.BlockSpec((1, ne, t), lambda b, i, off: (b, 0, i)),
                  pl.BlockSpec(memory_space=pl.ANY)],
        out_specs=pl.BlockSpec((t, d), lambda b, i, off: (b * nsp + i, 0)),
        scratch_shapes=[pltpu.VMEM((2, ne, span, d), BF16),
                        pltpu.VMEM((rows, d), BF16),
                        pltpu.VMEM((nsub, ROUTE_BLK, d), F32),
                        pltpu.VMEM((3, t, d), F32),
                        pltpu.SemaphoreType.DMA((2, ne)),
                        pltpu.SemaphoreType.DMA((ne,)),
                        pltpu.SemaphoreType.DMA((3,))],
    )
    return pl.pallas_call(
        functools.partial(_combine_body, ne=ne, cap=cap, nsp=nsp, nsub=nsub, span=span, nsteps=bsz * nsp),
        grid_spec=grid_spec,
        out_shape=jax.ShapeDtypeStruct((n, d), F32),
        compiler_params=_cparams(("arbitrary", "arbitrary")),
        name="combine",
    )(off_flat, x1, mods, fg, pos, aff, y)


def kernel(x, c, ctx, c_ctx, ada_w, ada_b, norm1_g, norm2_g, w_in, w_decay_up, b_decay, gla_norm_g,
           w_gla_proj, pool_w, pool_scale, w_pool_proj, w_out, w_router, w_gate_e, w_up_e, w_down_e,
           final_norm_g):
    assert ada_w.shape[0] == 1, "single-layer block"
    bsz, l, d = x.shape
    lc = ctx.shape[1]
    rank, dk = w_decay_up.shape[2], w_decay_up.shape[3]
    dvh = gla_norm_g.shape[1]
    dv = w_gla_proj.shape[1]
    heads = dv // dvh
    groups, ch = pool_w.shape[1], pool_w.shape[2]
    pw = groups * ch
    ne = w_router.shape[2]
    cap = EC_CAPACITY * l // ne
    assert dk // heads == HEAD_DK and dvh == LANES and ch == LANES and heads % 2 == 0
    assert l % ROUTE_BLK == 0 and l // ROUTE_BLK < LANES and cap >= SLOT_WIN and cap % BF16_ROWS == 0
    assert GRID_W == GLA_CHUNK and MXU_DIM % GRID_W == 0 and (l // MXU_DIM) % 2 == 0

    assert bsz < COND_ROWS
    cin = jnp.zeros((COND_ROWS, d), F32).at[:bsz].set(c).at[bsz].set(c_ctx)
    mods = _ada(cin, ada_w[0], ada_b[0][None, :], jnp.concatenate([norm1_g, norm2_g], axis=0))

    o_r = dk + dv
    o_q = o_r + 2 * rank
    o_g = o_q + dk
    o_p = o_g + dv
    o_m = o_p + pw
    up = jnp.zeros((2 * rank, 2 * dk), F32)
    up = up.at[:rank, :dk].set(w_decay_up[0, 0]).at[rank:, dk:].set(w_decay_up[0, 1])
    psc = pool_scale[0].reshape(groups, 1, ch)
    w1, wm, wpool = _prep(w_in[0].T, up, pool_w[0], psc, w_pool_proj[0], o_r, o_q, o_g, o_m,
                          float(dk // heads) ** -0.5)
    bz = b_decay[0]

    zero_state = jnp.zeros((bsz, heads // 2, LANES, 2 * LANES), F32)
    ck, cv, _, claf, clab, _, _ = _inproj(ctx.reshape(bsz * lc, d), mods, lambda i: bsz, w1, bz, dk, dv, pw,
                                          min(lc, CTX_TILE))
    r3 = lambda a, n: a.reshape(bsz, n, a.shape[-1])
    _, _, h_f, h_b = _gla(r3(ck, lc), r3(ck, lc), r3(cv, lc), r3(claf, lc), r3(clab, lc),
                          zero_state, zero_state, min(lc, CTX_TILE))

    x2 = x.reshape(bsz * l, d)
    tm = min(l, TOKEN_TILE)
    k, v, q, laf, lab, sg, pin = _inproj(x2, mods, lambda i: i // (l // tm), w1, bz, dk, dv, pw, tm)
    of, ob, _, _ = _gla(r3(k, l), r3(q, l), r3(v, l), r3(laf, l), r3(lab, l), h_f, h_b, min(l, GLA_TILE))
    pooled = _pool(r3(pin, l), ch).reshape(bsz * l, pw)

    wr_hi = w_router[0].astype(BF16)
    wr_lo = (w_router[0] - wr_hi.astype(F32)).astype(BF16)
    wr = jnp.zeros((d, LANES), BF16).at[:, :ne].set(wr_hi).at[:, ne:2 * ne].set(wr_lo)
    x1, h2, aff = _merge(x2, mods, of.reshape(bsz * l, dv), ob.reshape(bsz * l, dv), sg, pooled,
                          wm, w_gla_proj[0].astype(BF16), wpool, w_out[0].astype(BF16),
                          gla_norm_g[0][None, :], wr, l, heads, ne, tm)

    pos, offs = _route(aff.reshape(1, bsz * ne, l), cap)
    pos = pos.reshape(bsz, ne, l)
    off_flat = offs.reshape(-1)
    xs = _gatherx(off_flat, h2, pos, cap)
    y = _moe(xs, w_gate_e[0], w_up_e[0], w_down_e[0])
    out = _combine(off_flat, x1, mods, final_norm_g[None, :], pos, aff, y, l)
    return out.reshape(bsz, l, d)
```

```python
import functools

import jax
import jax.numpy as jnp
from jax import lax
from jax.experimental import pallas as pl
from jax.experimental.pallas import tpu as pltpu

F32 = jnp.float32
BF16 = jnp.bfloat16
I32 = jnp.int32
HIGHEST = lax.Precision.HIGHEST

EPS = 1e-6
GRID_W = 64
GLA_CHUNK = 64
GLA_STAGES = 6
GLA_BATCH = 8
GATE_NORMALIZER = 16.0
POOL_WINDOWS = (2, 4, 8, 16)
EC_CAPACITY = 2

LANES = 128
SUBLANES = 8
BF16_ROWS = 16
MXU_DIM = 256
VMEM_BYTES = 64 * 1024 * 1024
VMEM_LIMIT = VMEM_BYTES * 7 // 8

HEAD_DK = LANES // 2
ROUTE_BLK = MXU_DIM
SLOT_WIN = 64
SUB_ROWS = MXU_DIM
TOKEN_TILE = 1024
GLA_TILE = 2048
CTX_TILE = 256
COND_ROWS = SUBLANES
MOD_SHIFT1, MOD_MULT1, MOD_GATE1, MOD_SHIFT2, MOD_MULT2, MOD_GATE2 = range(6)


def _mod_spec(d, which, sample_of):
    return pl.BlockSpec((1, 1, d), lambda *g: (which * COND_ROWS + sample_of(*g[:2]), 0, 0))

_NT = (((1,), (1,)), ((), ()))
_TN = (((0,), (0,)), ((), ()))


def _cparams(sem):
    return pltpu.CompilerParams(dimension_semantics=sem, vmem_limit_bytes=VMEM_LIMIT)


def _sigmoid(x):
    return 0.5 * jnp.tanh(0.5 * x) + 0.5


def _silu(x):
    return x * _sigmoid(x)


def _row_groups(n):
    step = min(n, SUB_ROWS)
    return [pl.ds(i, step) for i in range(0, n, step)]


def _staggered(gens, nstages):
    for t in range(nstages + len(gens) - 1):
        for g in reversed(range(len(gens))):
            if 0 <= t - g < nstages:
                next(gens[g])


def _ada_body(c_ref, w_ref, b_ref, g_ref, o_ref):
    s = _silu(c_ref[...])
    rows = s.shape[0]
    s_hi = s.astype(BF16)
    s_lo = (s - s_hi.astype(F32)).astype(BF16)
    w = w_ref[...]
    w_hi = w.astype(BF16)
    w_lo = (w - w_hi.astype(F32)).astype(BF16)
    both = jnp.dot(jnp.concatenate([s_hi, s_lo], axis=0), w_hi, preferred_element_type=F32)
    out = (both[:rows] + both[rows:] + jnp.dot(s_hi, w_lo, preferred_element_type=F32)) + b_ref[...]
    j = pl.program_id(0)
    gain = jnp.where(j == MOD_MULT1, g_ref[0:1, :], g_ref[1:2, :])
    o_ref[...] = jnp.where((j == MOD_MULT1) | (j == MOD_MULT2), gain * (1.0 + out), out)


def _ada(cin, w, b, gains):
    rows, d = cin.shape
    nvec = w.shape[1] // d
    mods = pl.pallas_call(
        _ada_body,
        grid=(nvec,),
        in_specs=[pl.BlockSpec((rows, d), lambda j: (0, 0)),
                  pl.BlockSpec((d, d), lambda j: (0, j)),
                  pl.BlockSpec((1, d), lambda j: (0, j)),
                  pl.BlockSpec(gains.shape, lambda j: (0, 0))],
        out_specs=pl.BlockSpec((rows, d), lambda j: (j, 0)),
        out_shape=jax.ShapeDtypeStruct((nvec * rows, d), F32),
        compiler_params=_cparams(("arbitrary",)),
        name="ada",
    )(cin, w, b, gains)
    return mods.reshape(nvec * rows, 1, d)


def _prep_body(w_ref, up_ref, pw_ref, ps_ref, wpp_ref, w1_ref, wm_ref, wpf_ref, *, o_r, o_q, o_g, o_m, qscale):
    dk = o_g - o_q
    wz = lax.dot_general(w_ref[o_r:o_q, :], up_ref[...], _TN, precision=HIGHEST, preferred_element_type=F32)
    nz = wz.shape[1]
    w1_ref[:, :o_r] = w_ref[:o_r, :].T.astype(BF16)
    w1_ref[:, o_r:o_r + dk] = (w_ref[o_q:o_g, :].T * qscale).astype(BF16)
    w1_ref[:, o_r + dk:o_r + dk + nz] = wz.astype(BF16)
    w1_ref[:, o_r + dk + nz:] = w_ref[o_g:o_m, :].T.astype(BF16)
    wm_ref[...] = w_ref[o_m:, :].T.astype(BF16)
    wpf_ref[...] = jnp.dot(pw_ref[0] * ps_ref[0], wpp_ref[...], precision=HIGHEST,
                           preferred_element_type=F32).astype(BF16)


def _prep(w_in_t, up, pool_w, pool_scale3, w_pool_proj, o_r, o_q, o_g, o_m, qscale):
    n, d = w_in_t.shape
    n1 = o_r + (o_g - o_q) + up.shape[1] + (o_m - o_g)
    groups, ch, _ = pool_w.shape
    tr = d // groups
    dm = w_pool_proj.shape[1]
    return pl.pallas_call(
        functools.partial(_prep_body, o_r=o_r, o_q=o_q, o_g=o_g, o_m=o_m, qscale=qscale),
        grid=(d // tr,),
        in_specs=[pl.BlockSpec((n, tr), lambda i: (0, i)),
                  pl.BlockSpec(up.shape, lambda i: (0, 0)),
                  pl.BlockSpec((1, ch, ch), lambda i: (i, 0, 0)),
                  pl.BlockSpec((1, 1, ch), lambda i: (i, 0, 0)),
                  pl.BlockSpec((ch, dm), lambda i: (i, 0))],
        out_specs=[pl.BlockSpec((tr, n1), lambda i: (i, 0)),
                   pl.BlockSpec((tr, n - o_m), lambda i: (i, 0)),
                   pl.BlockSpec((ch, dm), lambda i: (i, 0))],
        out_shape=[jax.ShapeDtypeStruct((d, n1), BF16), jax.ShapeDtypeStruct((d, n - o_m), BF16),
                   jax.ShapeDtypeStruct((groups * ch, dm), BF16)],
        compiler_params=_cparams(("parallel",)),
        name="prep",
    )(w_in_t, up, pool_w, pool_scale3, w_pool_proj)


def _modulated_norm(x, mult, shift):
    ms = jnp.mean(x * x, axis=-1, keepdims=True)
    return (x * lax.rsqrt(ms + EPS)) * mult + shift


def _log_sigmoid(z):
    return jnp.minimum(z, 0.0) - jnp.log1p(jnp.exp(-jnp.abs(z)))


def _inproj_body(x_ref, mult_ref, shift_ref, w_ref, bz_ref,
                 k_ref, v_ref, q_ref, laf_ref, lab_ref, sg_ref, p_ref, *, dk, dv, pw):
    def sub_tile(rows):
        h = _modulated_norm(x_ref[rows, :], mult_ref[0], shift_ref[0]).astype(BF16)
        yield
        u = jnp.dot(h, w_ref[...], preferred_element_type=F32)
        yield
        o = 0
        k_ref[rows, :] = u[:, o:o + dk].astype(BF16); o += dk
        v_ref[rows, :] = u[:, o:o + dv].astype(BF16); o += dv
        q_ref[rows, :] = u[:, o:o + dk].astype(BF16); o += dk
        zf = u[:, o:o + dk] + bz_ref[0:1, :]; o += dk
        zb = u[:, o:o + dk] + bz_ref[1:2, :]; o += dk
        laf_ref[rows, :] = _log_sigmoid(zf) * (1.0 / GATE_NORMALIZER)
        lab_ref[rows, :] = _log_sigmoid(zb) * (1.0 / GATE_NORMALIZER)
        sg_ref[rows, :] = _silu(u[:, o:o + dv]).astype(BF16); o += dv
        p_ref[rows, :] = u[:, o:o + pw].astype(BF16)
        yield

    _staggered([sub_tile(r) for r in _row_groups(x_ref.shape[0])], 3)


def _inproj(x2, mods, cond_row, w, bz, dk, dv, pw, tm):
    n, d = x2.shape
    row = lambda i: (i, 0)
    sample_of = lambda i, *_: cond_row(i)
    outs = [(dk, BF16), (dv, BF16), (dk, BF16), (dk, F32), (dk, F32), (dv, BF16), (pw, BF16)]
    return pl.pallas_call(
        functools.partial(_inproj_body, dk=dk, dv=dv, pw=pw),
        grid=(n // tm,),
        in_specs=[pl.BlockSpec((tm, d), row),
                  _mod_spec(d, MOD_MULT1, sample_of),
                  _mod_spec(d, MOD_SHIFT1, sample_of),
                  pl.BlockSpec(w.shape, lambda i: (0, 0), pipeline_mode=pl.Buffered(1)),
                  pl.BlockSpec(bz.shape, lambda i: (0, 0))],
        out_specs=[pl.BlockSpec((tm, c), row) for c, _ in outs],
        out_shape=[jax.ShapeDtypeStruct((n, c), t) for c, t in outs],
        compiler_params=_cparams(("parallel",)),
        name="inproj",
    )(x2, mods, mods, w, bz)


def _gla_direction(k_ref, q_ref, v_ref, la_ref, o_ref, s_ref, reverse, nchunk):
    C = GLA_CHUNK
    hd = HEAD_DK
    cb = MXU_DIM
    lt = nchunk * C
    la = la_ref[0]
    r = lax.broadcasted_iota(I32, (cb, cb), 0)
    c = lax.broadcasted_iota(I32, (cb, cb), 1)
    same = (r // C) == (c // C)
    cum = jnp.where(same & ((c >= r) if reverse else (c <= r)), 1.0, 0.0).astype(BF16)
    la_hi = la.astype(BF16)
    la_lo = (la - la_hi.astype(F32)).astype(BF16)
    la2 = jnp.concatenate([la_hi, la_lo], axis=1)
    bcs = []
    for blk in range(lt // cb):
        part = jnp.dot(cum, la2[blk * cb:(blk + 1) * cb], preferred_element_type=F32)
        bcs.append(part[:, :LANES] + part[:, LANES:])
    ri = lax.broadcasted_iota(I32, (C, 2 * C), 0)
    ci = lax.broadcasted_iota(I32, (C, 2 * C), 1) % C
    tri = (ci >= ri) if reverse else (ci <= ri)
    lane = lax.broadcasted_iota(I32, (1, LANES), 1)
    m0 = (lane < hd).astype(F32)
    m1 = (lane >= hd).astype(F32)
    sr = lax.broadcasted_iota(I32, (LANES, 2 * LANES), 0)
    sl = lax.broadcasted_iota(I32, (LANES, 2 * LANES), 1)
    smask = ((sr < hd) == (sl < LANES)).astype(F32)
    kt = k_ref[0].astype(F32)
    qt = q_ref[0].astype(F32)
    zero_v = jnp.zeros((C, LANES), BF16)
    sweep = list(range(nchunk - 1, -1, -1) if reverse else range(nchunk))
    for first in range(0, nchunk, GLA_BATCH):
        order = sweep[first:first + GLA_BATCH]
        intra, qds, kvs, decs, q2s, kss, kws, scs = {}, {}, {}, {}, {}, {}, {}, {}
        for ch in order:
            lo = ch * C
            b = bcs[lo // cb][lo % cb:lo % cb + C]
            last = b[0:1] if reverse else b[C - 1:C]
            mid = b[C // 2:C // 2 + 1] if reverse else b[C // 2 - 1:C // 2]
            kc = kt[lo:lo + C]
            qc = qt[lo:lo + C]
            q2s[ch] = (qc * jnp.exp(b - mid)).astype(BF16)
            ks = kc * jnp.exp(mid - b)
            kss[ch] = jnp.concatenate([ks * m0, ks * m1], axis=0).astype(BF16)
            qds[ch] = (qc * jnp.exp(b)).astype(BF16)
            kws[ch] = (kc * jnp.exp(last - b)).astype(BF16)
            decs[ch] = last
        yield
        for ch in order:
            scs[ch] = lax.dot_general(q2s[ch], kss[ch], _NT, preferred_element_type=F32)
        yield
        for ch in order:
            v2 = v_ref[0, ch * C:(ch + 1) * C, :]
            kvs[ch] = lax.dot_general(kws[ch], v2, _TN, preferred_element_type=F32) * smask
        yield
        for ch in order:
            sc2 = jnp.where(tri, scs[ch], 0.0).astype(BF16)
            v2 = v_ref[0, ch * C:(ch + 1) * C, :]
            vbd = jnp.concatenate([jnp.concatenate([v2[:, :LANES], zero_v], axis=1),
                                   jnp.concatenate([zero_v, v2[:, LANES:]], axis=1)], axis=0)
            intra[ch] = jnp.dot(sc2, vbd, preferred_element_type=F32)
        yield
        pad = jnp.zeros((LANES - len(order), LANES), F32)
        dec_cols = jnp.exp(jnp.concatenate([decs[ch] for ch in order] + [pad], axis=0).T)
        st = s_ref[...]
        starts = {}
        for i, ch in enumerate(order):
            starts[ch] = st.astype(BF16)
            st = st * dec_cols[:, i:i + 1] + kvs[ch]
        s_ref[...] = st
        yield
        for ch in order:
            inter = jnp.dot(qds[ch], starts[ch], preferred_element_type=F32)
            o_ref[0, ch * C:(ch + 1) * C, :] = inter + intra[ch]
        yield


def _gla_body(kf, qf, vf, laf, kb, qb, vb, lab, h0f, h0b, of, ob, hf_out, hb_out, sf, sb, *, nchunk):
    i = pl.program_id(2)

    @pl.when(i == 0)
    def _():
        sf[...] = h0f[0, 0]
        sb[...] = h0b[0, 0]

    sweeps = [_gla_direction(kf, qf, vf, laf, of, sf, False, nchunk),
              _gla_direction(kb, qb, vb, lab, ob, sb, True, nchunk)]
    for _ in range(GLA_STAGES * pl.cdiv(nchunk, GLA_BATCH)):
        for sweep in sweeps:
            next(sweep)

    @pl.when(i == pl.num_programs(2) - 1)
    def _():
        hf_out[0, 0] = sf[...]
        hb_out[0, 0] = sb[...]


def _gla(k, q, v, laf, lab, h0f, h0b, lt):
    bsz, l, _ = k.shape
    pairs = h0f.shape[1]
    nt = l // lt
    fwd = lambda b, hp, i: (b, i, hp)
    bwd = lambda b, hp, i: (b, nt - 1 - i, hp)
    st = lambda b, hp, i: (b, hp, 0, 0)
    kq = lambda m: pl.BlockSpec((1, lt, LANES), m)
    vv = lambda m: pl.BlockSpec((1, lt, 2 * LANES), m)
    sspec = pl.BlockSpec((1, 1, LANES, 2 * LANES), st)
    return pl.pallas_call(
        functools.partial(_gla_body, nchunk=lt // GLA_CHUNK),
        grid=(bsz, pairs, nt),
        in_specs=[kq(fwd), kq(fwd), vv(fwd), kq(fwd), kq(bwd), kq(bwd), vv(bwd), kq(bwd), sspec, sspec],
        out_specs=[vv(fwd), vv(bwd), sspec, sspec],
        out_shape=[jax.ShapeDtypeStruct(v.shape, F32), jax.ShapeDtypeStruct(v.shape, F32),
                   jax.ShapeDtypeStruct(h0f.shape, F32), jax.ShapeDtypeStruct(h0b.shape, F32)],
        scratch_shapes=[pltpu.VMEM((LANES, 2 * LANES), F32), pltpu.VMEM((LANES, 2 * LANES), F32)],
        compiler_params=_cparams(("parallel", "parallel", "arbitrary")),
        name="gla",
    )(k, q, v, laf, k, q, v, lab, h0f, h0b)


def _pool_body(p_ref, o_ref, s1_ref, *, half, rows):
    xb = p_ref[0]
    l, ch = xb.shape

    def inv_counts(pos, n):
        return 1.0 / (jnp.minimum(pos + half, n) - jnp.maximum(pos - half, 0)).astype(F32)

    blk = MXU_DIM
    r = lax.broadcasted_iota(I32, (blk, blk), 0)
    c = lax.broadcasted_iota(I32, (blk, blk), 1)
    band = ((r // GRID_W == c // GRID_W) & (c - r >= -half) & (c - r <= half - 1)).astype(BF16)
    for i in range(0, l // blk, 2):
        pair = jnp.concatenate([xb[i * blk:(i + 1) * blk], xb[(i + 1) * blk:(i + 2) * blk]], axis=1)
        sums = jnp.dot(band, pair, preferred_element_type=F32)
        s1_ref[i * blk:(i + 1) * blk, :] = sums[:, :ch]
        s1_ref[(i + 1) * blk:(i + 2) * blk, :] = sums[:, ch:]
    s1 = s1_ref[...].reshape(rows, GRID_W, ch)

    def shifted(a, s):
        z = jnp.zeros((abs(s),) + a.shape[1:], a.dtype)
        return jnp.concatenate([a[s:], z], axis=0) if s > 0 else jnp.concatenate([z, a[:s]], axis=0)

    fwd = s1
    bwd = shifted(s1, -1)
    s = 1
    while s < half:
        fwd = fwd + shifted(fwd, s)
        bwd = bwd + shifted(bwd, -s)
        s *= 2
    inv_r = inv_counts(lax.broadcasted_iota(I32, (rows, 1, ch), 0), rows)
    inv_c = inv_counts(lax.broadcasted_iota(I32, (1, GRID_W, ch), 1), GRID_W)
    pooled = (fwd + bwd) * inv_r * inv_c - xb.astype(F32).reshape(rows, GRID_W, ch)
    o_ref[0] = pooled.reshape(l, ch).astype(BF16)


def _pool_groups_body(p_ref, o_ref, s1_ref, *, rows):
    for gi, window in enumerate(POOL_WINDOWS):
        @pl.when(pl.program_id(1) == gi)
        def _(half=window // 2):
            _pool_body(p_ref, o_ref, s1_ref, half=half, rows=rows)


def _pool(pin, ch):
    bsz, l, pw = pin.shape
    return pl.pallas_call(
        functools.partial(_pool_groups_body, rows=l // GRID_W),
        grid=(bsz, pw // ch),
        in_specs=[pl.BlockSpec((1, l, ch), lambda b, g: (b, 0, g))],
        out_specs=pl.BlockSpec((1, l, ch), lambda b, g: (b, 0, g)),
        out_shape=jax.ShapeDtypeStruct((bsz, l, pw), BF16),
        scratch_shapes=[pltpu.VMEM((l, ch), F32)],
        compiler_params=_cparams(("parallel", "parallel")),
        name="pool",
    )(pin)


def _merge_body(x_ref, m1_ref, s1_ref, g1_ref, m2_ref, s2_ref,
                of_ref, ob_ref, sg_ref, mx_ref,
                wm_ref, wgla_ref, wpool_ref, wout_ref, gn_ref, wr_ref,
                x1_ref, h2_ref, aff_ref, *, heads, ne):
    d = x_ref.shape[1]

    def sub_tile(rows):
        bp = jnp.dot(mx_ref[rows, :], wpool_ref[...], preferred_element_type=F32)
        x = x_ref[rows, :]
        h = _modulated_norm(x, m1_ref[0], s1_ref[0]).astype(BF16)
        o = of_ref[rows, :] + ob_ref[rows, :]
        sg = sg_ref[rows, :].astype(F32)
        og = []
        for j in range(heads):
            oj = o[:, j * LANES:(j + 1) * LANES]
            oj = oj * lax.rsqrt(jnp.mean(oj * oj, axis=-1, keepdims=True) + EPS) * gn_ref[...]
            og.append((oj * sg[:, j * LANES:(j + 1) * LANES]).astype(BF16))
        og = jnp.concatenate(og, axis=1)
        yield
        gates = jnp.dot(h, wm_ref[...], preferred_element_type=F32)
        bg = jnp.dot(og, wgla_ref[...], preferred_element_type=F32)
        yield
        gates = _sigmoid(gates)
        z = (gates[:, :d] * bg + gates[:, d:] * bp).astype(BF16)
        yield
        y = jnp.dot(z, wout_ref[...], preferred_element_type=F32)
        yield
        x1 = x + g1_ref[0] * y
        x1_ref[rows, :] = x1
        h2 = _modulated_norm(x1, m2_ref[0], s2_ref[0])
        hi = h2.astype(BF16)
        h2_ref[rows, :] = hi
        lo = (h2 - hi.astype(F32)).astype(BF16)
        yield
        lg = (jnp.dot(hi, wr_ref[...], preferred_element_type=F32)
              + jnp.dot(lo, wr_ref[...], preferred_element_type=F32))
        yield
        lgt = lg.T
        logit = lgt[0:ne] + lgt[ne:2 * ne]
        mx = jnp.max(logit, axis=0, keepdims=True)
        ex = jnp.exp(logit - mx)
        aff_ref[0, :, rows] = ex / jnp.sum(ex, axis=0, keepdims=True)
        yield

    _staggered([sub_tile(r) for r in _row_groups(x_ref.shape[0])], 7)


def _merge(x2, mods, of, ob, sg, mixed, wm, wgla, wpool, wout, gn, wr, rows_per_sample, heads, ne, tm):
    n, d = x2.shape
    bsz = n // rows_per_sample
    tps = rows_per_sample // tm
    row = lambda i: (i, 0)
    sample_of = lambda i, *_: i // tps
    vecs = [MOD_MULT1, MOD_SHIFT1, MOD_GATE1, MOD_MULT2, MOD_SHIFT2]
    full = lambda a: pl.BlockSpec(a.shape, lambda i: (0,) * a.ndim, pipeline_mode=pl.Buffered(1))
    gv = of.shape[-1]
    return pl.pallas_call(
        functools.partial(_merge_body, heads=heads, ne=ne),
        grid=(n // tm,),
        in_specs=[pl.BlockSpec((tm, d), row)] + [_mod_spec(d, v, sample_of) for v in vecs]
                 + [pl.BlockSpec((tm, gv), row)] * 4
                 + [full(wm), full(wgla), full(wpool), full(wout), full(gn), full(wr)],
        out_specs=[pl.BlockSpec((tm, d), row),
                   pl.BlockSpec((tm, d), row),
                   pl.BlockSpec((1, ne, tm), lambda i: (i // tps, 0, i % tps))],
        out_shape=[jax.ShapeDtypeStruct((n, d), F32),
                   jax.ShapeDtypeStruct((n, d), BF16),
                   jax.ShapeDtypeStruct((bsz, ne, rows_per_sample), F32)],
        compiler_params=_cparams(("parallel",)),
        name="merge",
    )(x2, *[mods] * len(vecs), of, ob, sg, mixed, wm, wgla, wpool, wout, gn, wr)


def _route_body(aff_ref, pos_ref, off_ref, cnt_ref, *, cap, ntb):
    a = aff_ref[0]
    ne, l = a.shape
    blk = ROUTE_BLK

    def bisect(i, v):
        cand = v | jnp.left_shift(jnp.int32(1), 30 - i)
        cnt = jnp.sum((a >= lax.bitcast_convert_type(cand, F32)).astype(F32), axis=1, keepdims=True)
        return jnp.where(cnt >= cap, cand, v)

    thr = lax.bitcast_convert_type(lax.fori_loop(0, 31, bisect, jnp.zeros((ne, 1), I32)), F32)
    gt = a > thr
    tie = a == thr
    need = cap - jnp.sum(gt.astype(F32), axis=1, keepdims=True)

    r = lax.broadcasted_iota(I32, (blk, blk), 0)
    c = lax.broadcasted_iota(I32, (blk, blk), 1)
    upper = (r <= c).astype(BF16)
    lane = lax.broadcasted_iota(I32, (1, LANES), 1)

    def prefix(mask_f):
        run = jnp.zeros((ne, 1), F32)
        offs = jnp.zeros((ne, LANES), F32)
        for tb in range(ntb):
            m = mask_f[:, tb * blk:(tb + 1) * blk].astype(BF16)
            loc = jnp.dot(m, upper, preferred_element_type=F32)
            cnt_ref[:, tb * blk:(tb + 1) * blk] = loc + run
            offs = jnp.where(lane == tb, run, offs)
            run = run + loc[:, blk - 1:blk]
        return jnp.where(lane >= ntb, run, offs)

    tie_f = tie.astype(F32)
    prefix(tie_f)
    tie_excl = cnt_ref[...] - tie_f
    sel = gt | (tie & (tie_excl < need))
    offs = prefix(sel.astype(F32))
    pos_ref[0] = jnp.where(sel, cnt_ref[...] - 1.0, -1.0).astype(I32)
    off_ref[0] = offs.astype(I32)


def _route(aff, cap):
    bsz, ne, l = aff.shape
    spec = lambda s: pl.BlockSpec((1,) + s, lambda b: (b, 0, 0))
    return pl.pallas_call(
        functools.partial(_route_body, cap=cap, ntb=l // ROUTE_BLK),
        grid=(bsz,),
        in_specs=[spec((ne, l))],
        out_specs=[spec((ne, l)), spec((ne, LANES))],
        out_shape=[jax.ShapeDtypeStruct((bsz, ne, l), I32),
                   jax.ShapeDtypeStruct((bsz, ne, LANES), I32)],
        scratch_shapes=[pltpu.VMEM((ne, l), F32)],
        compiler_params=_cparams(("parallel",)),
        name="route",
    )(aff)


def _window_plan(off_ref, bb, tt, experts, ne):
    lows = [off_ref[(bb * ne + e) * LANES + tt] & -BF16_ROWS for e in experts]
    ends = [off_ref[(bb * ne + e) * LANES + tt + 1] for e in experts]
    return lows, ends


def _window_rounds(lows, ends):
    rounds = jnp.int32(0)
    for lo, hi in zip(lows, ends):
        rounds = jnp.maximum(rounds, lax.div(hi - lo + (SLOT_WIN - 1), jnp.int32(SLOT_WIN)))
    return rounds


def _gatherx_body(off_ref, h_ref, pos_ref, xs_ref, *, ne, cap, eg):
    b = pl.program_id(0)
    g = pl.program_id(1)
    tb = pl.program_id(2)
    win = SLOT_WIN

    @pl.when(tb == 0)
    def _():
        xs_ref[...] = jnp.zeros(xs_ref.shape, BF16)

    j_col = lax.broadcasted_iota(I32, (win, 1), 0)
    t = ROUTE_BLK
    nsub = h_ref.shape[0] // t
    experts = [g * eg + k for k in range(eg)]

    def select(sub, lows, r):
        starts = [pl.multiple_of(jnp.minimum(lows[k] + r * win, cap - win), BF16_ROWS) for k in range(eg)]
        pieces = []
        for k in range(eg):
            p = pos_ref[0, k:k + 1, sub * t:(sub + 1) * t]
            hit = (p - starts[k] == j_col) & (p >= lows[k] + r * win)
            pieces.append(jnp.where(hit, 1.0, 0.0).astype(BF16))
        sel = jnp.concatenate(pieces, axis=0)
        rows = jnp.dot(sel, h_ref[sub * t:(sub + 1) * t, :], preferred_element_type=F32).astype(BF16)
        return starts, rows

    def deposit(starts, rows):
        for k in range(eg):
            dst = (0, k, pl.ds(starts[k], win), slice(None))
            xs_ref[dst] = xs_ref[dst] + rows[k * win:(k + 1) * win]

    plans = [_window_plan(off_ref, b, tb * nsub + sub, experts, ne) for sub in range(nsub)]
    firsts = [select(sub, plans[sub][0], 0) for sub in range(nsub)]
    for starts, rows in firsts:
        deposit(starts, rows)
    for sub in range(nsub):
        lows, ends = plans[sub]

        def extra_round(r, carry, lows=lows, sub=sub):
            deposit(*select(sub, lows, r))
            return carry

        lax.fori_loop(1, _window_rounds(lows, ends), extra_round, 0)


def _gatherx(off_flat, h2, pos, cap):
    n, d = h2.shape
    bsz, ne, l = pos.shape
    t = min(l, 2 * TOKEN_TILE)
    ntb = l // t
    eg = SUBLANES
    grid_spec = pltpu.PrefetchScalarGridSpec(
        num_scalar_prefetch=1,
        grid=(bsz, ne // eg, ntb),
        in_specs=[pl.BlockSpec((t, d), lambda b, g, i, off: (b * ntb + i, 0)),
                  pl.BlockSpec((1, eg, t), lambda b, g, i, off: (b, g, i))],
        out_specs=pl.BlockSpec((1, eg, cap, d), lambda b, g, i, off: (b, g, 0, 0)),
    )
    return pl.pallas_call(
        functools.partial(_gatherx_body, ne=ne, cap=cap, eg=eg),
        grid_spec=grid_spec,
        out_shape=jax.ShapeDtypeStruct((bsz, ne, cap, d), BF16),
        compiler_params=_cparams(("parallel", "parallel", "arbitrary")),
        name="gatherx",
    )(off_flat, h2, pos)


def _moe_body(xs_ref, wg_ref, wu_ref, wd_ref, y_ref, *, rc):
    wg = wg_ref[0].astype(BF16)
    wu = wu_ref[0].astype(BF16)
    wd = wd_ref[0].astype(BF16)
    cap = xs_ref.shape[2]
    acts = []
    for ch in range(cap // rc):
        xs = xs_ref[0, 0, ch * rc:(ch + 1) * rc, :]
        gate = jnp.dot(xs, wg, preferred_element_type=F32)
        up = jnp.dot(xs, wu, preferred_element_type=F32)
        acts.append((_silu(gate) * up).astype(BF16))
    for ch in range(cap // rc):
        y_ref[0, 0, ch * rc:(ch + 1) * rc, :] = jnp.dot(acts[ch], wd, preferred_element_type=F32).astype(BF16)


def _moe(xs, wg, wu, wd):
    bsz, ne, cap, d = xs.shape
    de = wg.shape[2]
    slot = pl.BlockSpec((1, 1, cap, d), lambda e, b: (b, e, 0, 0))
    return pl.pallas_call(
        functools.partial(_moe_body, rc=min(cap, 2 * MXU_DIM)),
        grid=(ne, bsz),
        in_specs=[slot,
                  pl.BlockSpec((1, d, de), lambda e, b: (e, 0, 0)),
                  pl.BlockSpec((1, d, de), lambda e, b: (e, 0, 0)),
                  pl.BlockSpec((1, de, d), lambda e, b: (e, 0, 0))],
        out_specs=slot,
        out_shape=jax.ShapeDtypeStruct((bsz, ne, cap, d), BF16),
        compiler_params=_cparams(("parallel", "arbitrary")),
        name="moe",
    )(xs, wg, wu, wd)


def _combine_body(off_ref, x1_ref, g2_ref, fg_ref, pos_ref, aff_ref, y_hbm, o_ref, ybuf, spare, acc_ref, sem,
                  spare_sem, *, ne, cap, nsp, nsub, span):
    b = pl.program_id(0)
    i = pl.program_id(1)
    step = b * nsp + i
    nsteps = pl.num_programs(0) * nsp
    slot = step % 2
    win = SLOT_WIN
    t = ROUTE_BLK
    experts = range(ne)

    def window_starts(lows, r):
        return [pl.multiple_of(jnp.minimum(lows[e] + r * win, cap - win), BF16_ROWS) for e in experts]

    def span_starts(bb, ii):
        lows = _window_plan(off_ref, bb, ii * nsub, experts, ne)[0]
        return [pl.multiple_of(jnp.minimum(lows[e], cap - span), BF16_ROWS) for e in experts]

    def span_copy(bb, e, start, buf):
        return pltpu.make_async_copy(y_hbm.at[bb, e, pl.ds(start, span), :], ybuf.at[buf, e], sem.at[buf, e])

    def spare_copy(e, start):
        return pltpu.make_async_copy(y_hbm.at[b, e, pl.ds(start, win), :],
                                     spare.at[pl.ds(e * win, win), :], spare_sem.at[e])

    def start_step(bb, ii, buf):
        starts = span_starts(bb, ii)
        for e in experts:
            span_copy(bb, e, starts[e], buf).start(priority=1)

    @pl.when(step == 0)
    def _():
        start_step(b, i, 0)

    nxt = jnp.minimum(step + 1, nsteps - 1)
    start_step(lax.div(nxt, jnp.int32(nsp)), lax.rem(nxt, jnp.int32(nsp)), 1 - slot)

    plans = [_window_plan(off_ref, b, i * nsub + sub, experts, ne) for sub in range(nsub)]
    mine = span_starts(b, i)
    firsts = [window_starts(plans[sub][0], 0) for sub in range(nsub)]
    inside = []
    for sub in range(nsub):
        ok = jnp.bool_(True)
        for e in experts:
            ok = ok & (firsts[sub][e] >= mine[e]) & (firsts[sub][e] + win <= mine[e] + span)
        inside.append(ok)
    j_col = lax.broadcasted_iota(I32, (win, 1), 0)

    def expand(sub, r, starts, rows, enabled=None):
        lows = plans[sub][0]
        pieces = []
        for e in experts:
            p = pos_ref[0, e:e + 1, sub * t:(sub + 1) * t]
            valid = p >= lows[e] + r * win
            hit = (p - starts[e] == j_col) & valid
            if enabled is not None:
                hit = hit & enabled
            pieces.append(jnp.where(hit, aff_ref[0, e:e + 1, sub * t:(sub + 1) * t], 0.0).astype(BF16))
        pmat = jnp.concatenate(pieces, axis=0)
        return lax.dot_general(pmat, rows, _TN, preferred_element_type=F32)

    acc_ref[...] = jnp.zeros(acc_ref.shape, F32)
    for sub in range(nsub):
        lows, ends = plans[sub]

        def own_round(r, carry, lows=lows, sub=sub):
            starts = window_starts(lows, r)
            for e in experts:
                spare_copy(e, starts[e]).start()
            for e in experts:
                spare_copy(e, starts[e]).wait()
            acc_ref[sub] += expand(sub, r, starts, spare[...])
            return carry

        lax.fori_loop(jnp.where(inside[sub], 1, 0), _window_rounds(lows, ends), own_round, 0)

    for e in experts:
        span_copy(b, e, mine[e], slot).wait()
    for sub in range(nsub):
        rows = pl.ds(sub * t, t)
        picked = []
        for e in experts:
            local = pl.multiple_of(jnp.clip(firsts[sub][e] - mine[e], 0, span - win), BF16_ROWS)
            picked.append(ybuf[slot, e, pl.ds(local, win), :])
        moe = expand(sub, 0, firsts[sub], jnp.concatenate(picked, axis=0), inside[sub]) + acc_ref[sub]
        x2 = x1_ref[rows, :] + g2_ref[0] * moe
        ms = jnp.mean(x2 * x2, axis=-1, keepdims=True)
        o_ref[rows, :] = x2 * lax.rsqrt(ms + EPS) * fg_ref[...]

    @pl.when(step == nsteps - 1)
    def _():
        for e in experts:
            span_copy(b, e, mine[e], 1 - slot).wait()


def _combine(off_flat, x1, mods, fg, pos, aff, y, rows_per_sample):
    n, d = x1.shape
    bsz, ne, cap, _ = y.shape
    t = min(rows_per_sample, TOKEN_TILE)
    nsub = t // ROUTE_BLK
    nsp = rows_per_sample // t
    rows = ne * SLOT_WIN
    span = min(cap, nsub * SLOT_WIN * 3 // 4)
    grid_spec = pltpu.PrefetchScalarGridSpec(
        num_scalar_prefetch=1,
        grid=(bsz, nsp),
        in_specs=[pl.BlockSpec((t, d), lambda b, i, off: (b * nsp + i, 0)),
                  _mod_spec(d, MOD_GATE2, lambda b, i: b),
                  pl.BlockSpec((1, d), lambda b, i, off: (0, 0)),
                  pl.BlockSpec((1, ne, t), lambda b, i, off: (b, 0, i)),
                  pl.BlockSpec((1, ne, t), lambda b, i, off: (b, 0, i)),
                  pl.BlockSpec(memory_space=pl.ANY)],
        out_specs=pl.BlockSpec((t, d), lambda b, i, off: (b * nsp + i, 0)),
        scratch_shapes=[pltpu.VMEM((2, ne, span, d), BF16),
                        pltpu.VMEM((rows, d), BF16),
                        pltpu.VMEM((nsub, ROUTE_BLK, d), F32),
                        pltpu.SemaphoreType.DMA((2, ne)),
                        pltpu.SemaphoreType.DMA((ne,))],
    )
    return pl.pallas_call(
        functools.partial(_combine_body, ne=ne, cap=cap, nsp=nsp, nsub=nsub, span=span),
        grid_spec=grid_spec,
        out_shape=jax.ShapeDtypeStruct((n, d), F32),
        compiler_params=_cparams(("arbitrary", "arbitrary")),
        name="combine",
    )(off_flat, x1, mods, fg, pos, aff, y)


def kernel(x, c, ctx, c_ctx, ada_w, ada_b, norm1_g, norm2_g, w_in, w_decay_up, b_decay, gla_norm_g,
           w_gla_proj, pool_w, pool_scale, w_pool_proj, w_out, w_router, w_gate_e, w_up_e, w_down_e,
           final_norm_g):
    assert ada_w.shape[0] == 1, "single-layer block"
    bsz, l, d = x.shape
    lc = ctx.shape[1]
    rank, dk = w_decay_up.shape[2], w_decay_up.shape[3]
    dvh = gla_norm_g.shape[1]
    dv = w_gla_proj.shape[1]
    heads = dv // dvh
    groups, ch = pool_w.shape[1], pool_w.shape[2]
    pw = groups * ch
    ne = w_router.shape[2]
    cap = EC_CAPACITY * l // ne
    assert dk // heads == HEAD_DK and dvh == LANES and ch == LANES and heads % 2 == 0
    assert l % ROUTE_BLK == 0 and l // ROUTE_BLK < LANES and cap >= SLOT_WIN and cap % BF16_ROWS == 0
    assert GRID_W == GLA_CHUNK and MXU_DIM % GRID_W == 0 and (l // MXU_DIM) % 2 == 0

    assert bsz < COND_ROWS
    cin = jnp.zeros((COND_ROWS, d), F32).at[:bsz].set(c).at[bsz].set(c_ctx)
    mods = _ada(cin, ada_w[0], ada_b[0][None, :], jnp.concatenate([norm1_g, norm2_g], axis=0))

    o_r = dk + dv
    o_q = o_r + 2 * rank
    o_g = o_q + dk
    o_p = o_g + dv
    o_m = o_p + pw
    up = jnp.zeros((2 * rank, 2 * dk), F32)
    up = up.at[:rank, :dk].set(w_decay_up[0, 0]).at[rank:, dk:].set(w_decay_up[0, 1])
    psc = pool_scale[0].reshape(groups, 1, ch)
    w1, wm, wpool = _prep(w_in[0].T, up, pool_w[0], psc, w_pool_proj[0], o_r, o_q, o_g, o_m,
                          float(dk // heads) ** -0.5)
    bz = b_decay[0]

    zero_state = jnp.zeros((bsz, heads // 2, LANES, 2 * LANES), F32)
    ck, cv, _, claf, clab, _, _ = _inproj(ctx.reshape(bsz * lc, d), mods, lambda i: bsz, w1, bz, dk, dv, pw,
                                          min(lc, CTX_TILE))
    r3 = lambda a, n: a.reshape(bsz, n, a.shape[-1])
    _, _, h_f, h_b = _gla(r3(ck, lc), r3(ck, lc), r3(cv, lc), r3(claf, lc), r3(clab, lc),
                          zero_state, zero_state, min(lc, CTX_TILE))

    x2 = x.reshape(bsz * l, d)
    tm = min(l, TOKEN_TILE)
    k, v, q, laf, lab, sg, pin = _inproj(x2, mods, lambda i: i // (l // tm), w1, bz, dk, dv, pw, tm)
    of, ob, _, _ = _gla(r3(k, l), r3(q, l), r3(v, l), r3(laf, l), r3(lab, l), h_f, h_b, min(l, GLA_TILE))
    pooled = _pool(r3(pin, l), ch).reshape(bsz * l, pw)

    wr_hi = w_router[0].astype(BF16)
    wr_lo = (w_router[0] - wr_hi.astype(F32)).astype(BF16)
    wr = jnp.zeros((d, LANES), BF16).at[:, :ne].set(wr_hi).at[:, ne:2 * ne].set(wr_lo)
    x1, h2, aff = _merge(x2, mods, of.reshape(bsz * l, dv), ob.reshape(bsz * l, dv), sg, pooled,
                          wm, w_gla_proj[0].astype(BF16), wpool, w_out[0].astype(BF16),
                          gla_norm_g[0][None, :], wr, l, heads, ne, tm)

    pos, offs = _route(aff.reshape(1, bsz * ne, l), cap)
    pos = pos.reshape(bsz, ne, l)
    off_flat = offs.reshape(-1)
    xs = _gatherx(off_flat, h2, pos, cap)
    y = _moe(xs, w_gate_e[0], w_up_e[0], w_down_e[0])
    out = _combine(off_flat, x1, mods, final_norm_g[None, :], pos, aff, y, l)
    return out.reshape(bsz, l, d)
```

```python
import functools

import jax
import jax.numpy as jnp
from jax import lax
from jax.experimental import pallas as pl
from jax.experimental.pallas import tpu as pltpu

F32 = jnp.float32
BF16 = jnp.bfloat16
I32 = jnp.int32
HIGHEST = lax.Precision.HIGHEST

EPS = 1e-6
GRID_W = 64
GLA_CHUNK = 64
GLA_STAGES = 6
GLA_BATCH = 8
GATE_NORMALIZER = 16.0
POOL_WINDOWS = (2, 4, 8, 16)
EC_CAPACITY = 2

LANES = 128
SUBLANES = 8
BF16_ROWS = 16
MXU_DIM = 256
VMEM_BYTES = 64 * 1024 * 1024
VMEM_LIMIT = VMEM_BYTES * 7 // 8

HEAD_DK = LANES // 2
ROUTE_BLK = MXU_DIM
SLOT_WIN = 64
SUB_ROWS = MXU_DIM
TOKEN_TILE = 1024
GLA_TILE = 2048
CTX_TILE = 256
COND_ROWS = SUBLANES
MOD_SHIFT1, MOD_MULT1, MOD_GATE1, MOD_SHIFT2, MOD_MULT2, MOD_GATE2 = range(6)


def _mod_spec(d, which, sample_of):
    return pl.BlockSpec((1, 1, d), lambda *g: (which * COND_ROWS + sample_of(*g[:2]), 0, 0))

_NT = (((1,), (1,)), ((), ()))
_TN = (((0,), (0,)), ((), ()))


def _cparams(sem):
    return pltpu.CompilerParams(dimension_semantics=sem, vmem_limit_bytes=VMEM_LIMIT)


def _sigmoid(x):
    return 0.5 * jnp.tanh(0.5 * x) + 0.5


def _silu(x):
    return x * _sigmoid(x)


def _row_groups(n):
    step = min(n, SUB_ROWS)
    return [pl.ds(i, step) for i in range(0, n, step)]


def _staggered(gens, nstages):
    for t in range(nstages + len(gens) - 1):
        for g in reversed(range(len(gens))):
            if 0 <= t - g < nstages:
                next(gens[g])


def _ada_body(c_ref, w_ref, b_ref, g_ref, o_ref):
    s = _silu(c_ref[...])
    rows = s.shape[0]
    s_hi = s.astype(BF16)
    s_lo = (s - s_hi.astype(F32)).astype(BF16)
    w = w_ref[...]
    w_hi = w.astype(BF16)
    w_lo = (w - w_hi.astype(F32)).astype(BF16)
    both = jnp.dot(jnp.concatenate([s_hi, s_lo], axis=0), w_hi, preferred_element_type=F32)
    out = (both[:rows] + both[rows:] + jnp.dot(s_hi, w_lo, preferred_element_type=F32)) + b_ref[...]
    j = pl.program_id(0)
    gain = jnp.where(j == MOD_MULT1, g_ref[0:1, :], g_ref[1:2, :])
    o_ref[...] = jnp.where((j == MOD_MULT1) | (j == MOD_MULT2), gain * (1.0 + out), out)


def _ada(cin, w, b, gains):
    rows, d = cin.shape
    nvec = w.shape[1] // d
    mods = pl.pallas_call(
        _ada_body,
        grid=(nvec,),
        in_specs=[pl.BlockSpec((rows, d), lambda j: (0, 0)),
                  pl.BlockSpec((d, d), lambda j: (0, j)),
                  pl.BlockSpec((1, d), lambda j: (0, j)),
                  pl.BlockSpec(gains.shape, lambda j: (0, 0))],
        out_specs=pl.BlockSpec((rows, d), lambda j: (j, 0)),
        out_shape=jax.ShapeDtypeStruct((nvec * rows, d), F32),
        compiler_params=_cparams(("arbitrary",)),
        name="ada",
    )(cin, w, b, gains)
    return mods.reshape(nvec * rows, 1, d)


def _prep_body(w_ref, up_ref, pw_ref, ps_ref, wpp_ref, w1_ref, wm_ref, wpf_ref, *, o_r, o_q, o_g, o_m, qscale):
    dk = o_g - o_q
    wz = lax.dot_general(w_ref[o_r:o_q, :], up_ref[...], _TN, precision=HIGHEST, preferred_element_type=F32)
    nz = wz.shape[1]
    w1_ref[:, :o_r] = w_ref[:o_r, :].T.astype(BF16)
    w1_ref[:, o_r:o_r + dk] = (w_ref[o_q:o_g, :].T * qscale).astype(BF16)
    w1_ref[:, o_r + dk:o_r + dk + nz] = wz.astype(BF16)
    w1_ref[:, o_r + dk + nz:] = w_ref[o_g:o_m, :].T.astype(BF16)
    wm_ref[...] = w_ref[o_m:, :].T.astype(BF16)
    wpf_ref[...] = jnp.dot(pw_ref[0] * ps_ref[0], wpp_ref[...], precision=HIGHEST,
                           preferred_element_type=F32).astype(BF16)


def _prep(w_in_t, up, pool_w, pool_scale3, w_pool_proj, o_r, o_q, o_g, o_m, qscale):
    n, d = w_in_t.shape
    n1 = o_r + (o_g - o_q) + up.shape[1] + (o_m - o_g)
    groups, ch, _ = pool_w.shape
    tr = d // groups
    dm = w_pool_proj.shape[1]
    return pl.pallas_call(
        functools.partial(_prep_body, o_r=o_r, o_q=o_q, o_g=o_g, o_m=o_m, qscale=qscale),
        grid=(d // tr,),
        in_specs=[pl.BlockSpec((n, tr), lambda i: (0, i)),
                  pl.BlockSpec(up.shape, lambda i: (0, 0)),
                  pl.BlockSpec((1, ch, ch), lambda i: (i, 0, 0)),
                  pl.BlockSpec((1, 1, ch), lambda i: (i, 0, 0)),
                  pl.BlockSpec((ch, dm), lambda i: (i, 0))],
        out_specs=[pl.BlockSpec((tr, n1), lambda i: (i, 0)),
                   pl.BlockSpec((tr, n - o_m), lambda i: (i, 0)),
                   pl.BlockSpec((ch, dm), lambda i: (i, 0))],
        out_shape=[jax.ShapeDtypeStruct((d, n1), BF16), jax.ShapeDtypeStruct((d, n - o_m), BF16),
                   jax.ShapeDtypeStruct((groups * ch, dm), BF16)],
        compiler_params=_cparams(("parallel",)),
        name="prep",
    )(w_in_t, up, pool_w, pool_scale3, w_pool_proj)


def _modulated_norm(x, mult, shift):
    ms = jnp.mean(x * x, axis=-1, keepdims=True)
    return (x * lax.rsqrt(ms + EPS)) * mult + shift


def _log_sigmoid(z):
    return jnp.minimum(z, 0.0) - jnp.log1p(jnp.exp(-jnp.abs(z)))


def _inproj_body(x_ref, mult_ref, shift_ref, w_ref, bz_ref,
                 k_ref, v_ref, q_ref, laf_ref, lab_ref, sg_ref, p_ref, *, dk, dv, pw):
    def sub_tile(rows):
        h = _modulated_norm(x_ref[rows, :], mult_ref[0], shift_ref[0]).astype(BF16)
        yield
        u = jnp.dot(h, w_ref[...], preferred_element_type=F32)
        yield
        o = 0
        k_ref[rows, :] = u[:, o:o + dk].astype(BF16); o += dk
        v_ref[rows, :] = u[:, o:o + dv].astype(BF16); o += dv
        q_ref[rows, :] = u[:, o:o + dk].astype(BF16); o += dk
        zf = u[:, o:o + dk] + bz_ref[0:1, :]; o += dk
        zb = u[:, o:o + dk] + bz_ref[1:2, :]; o += dk
        laf_ref[rows, :] = _log_sigmoid(zf) * (1.0 / GATE_NORMALIZER)
        lab_ref[rows, :] = _log_sigmoid(zb) * (1.0 / GATE_NORMALIZER)
        sg_ref[rows, :] = _silu(u[:, o:o + dv]).astype(BF16); o += dv
        p_ref[rows, :] = u[:, o:o + pw].astype(BF16)
        yield

    _staggered([sub_tile(r) for r in _row_groups(x_ref.shape[0])], 3)


def _inproj(x2, mods, cond_row, w, bz, dk, dv, pw, tm):
    n, d = x2.shape
    row = lambda i: (i, 0)
    sample_of = lambda i, *_: cond_row(i)
    outs = [(dk, BF16), (dv, BF16), (dk, BF16), (dk, F32), (dk, F32), (dv, BF16), (pw, BF16)]
    return pl.pallas_call(
        functools.partial(_inproj_body, dk=dk, dv=dv, pw=pw),
        grid=(n // tm,),
        in_specs=[pl.BlockSpec((tm, d), row),
                  _mod_spec(d, MOD_MULT1, sample_of),
                  _mod_spec(d, MOD_SHIFT1, sample_of),
                  pl.BlockSpec(w.shape, lambda i: (0, 0), pipeline_mode=pl.Buffered(1)),
                  pl.BlockSpec(bz.shape, lambda i: (0, 0))],
        out_specs=[pl.BlockSpec((tm, c), row) for c, _ in outs],
        out_shape=[jax.ShapeDtypeStruct((n, c), t) for c, t in outs],
        compiler_params=_cparams(("parallel",)),
        name="inproj",
    )(x2, mods, mods, w, bz)


def _gla_direction(k_ref, q_ref, v_ref, la_ref, o_ref, s_ref, reverse, nchunk):
    C = GLA_CHUNK
    hd = HEAD_DK
    cb = MXU_DIM
    lt = nchunk * C
    la = la_ref[0]
    r = lax.broadcasted_iota(I32, (cb, cb), 0)
    c = lax.broadcasted_iota(I32, (cb, cb), 1)
    same = (r // C) == (c // C)
    cum = jnp.where(same & ((c >= r) if reverse else (c <= r)), 1.0, 0.0).astype(BF16)
    la_hi = la.astype(BF16)
    la_lo = (la - la_hi.astype(F32)).astype(BF16)
    la2 = jnp.concatenate([la_hi, la_lo], axis=1)
    bcs = []
    for blk in range(lt // cb):
        part = jnp.dot(cum, la2[blk * cb:(blk + 1) * cb], preferred_element_type=F32)
        bcs.append(part[:, :LANES] + part[:, LANES:])
    ri = lax.broadcasted_iota(I32, (C, 2 * C), 0)
    ci = lax.broadcasted_iota(I32, (C, 2 * C), 1) % C
    tri = (ci >= ri) if reverse else (ci <= ri)
    lane = lax.broadcasted_iota(I32, (1, LANES), 1)
    m0 = (lane < hd).astype(F32)
    m1 = (lane >= hd).astype(F32)
    sr = lax.broadcasted_iota(I32, (LANES, 2 * LANES), 0)
    sl = lax.broadcasted_iota(I32, (LANES, 2 * LANES), 1)
    smask = ((sr < hd) == (sl < LANES)).astype(F32)
    kt = k_ref[0].astype(F32)
    qt = q_ref[0].astype(F32)
    zero_v = jnp.zeros((C, LANES), BF16)
    sweep = list(range(nchunk - 1, -1, -1) if reverse else range(nchunk))
    for first in range(0, nchunk, GLA_BATCH):
        order = sweep[first:first + GLA_BATCH]
        intra, qds, kvs, decs, q2s, kss, kws, scs = {}, {}, {}, {}, {}, {}, {}, {}
        for ch in order:
            lo = ch * C
            b = bcs[lo // cb][lo % cb:lo % cb + C]
            last = b[0:1] if reverse else b[C - 1:C]
            mid = b[C // 2:C // 2 + 1] if reverse else b[C // 2 - 1:C // 2]
            kc = kt[lo:lo + C]
            qc = qt[lo:lo + C]
            q2s[ch] = (qc * jnp.exp(b - mid)).astype(BF16)
            ks = kc * jnp.exp(mid - b)
            kss[ch] = jnp.concatenate([ks * m0, ks * m1], axis=0).astype(BF16)
            qds[ch] = (qc * jnp.exp(b)).astype(BF16)
            kws[ch] = (kc * jnp.exp(last - b)).astype(BF16)
            decs[ch] = last
        yield
        for ch in order:
            scs[ch] = lax.dot_general(q2s[ch], kss[ch], _NT, preferred_element_type=F32)
        yield
        for ch in order:
            v2 = v_ref[0, ch * C:(ch + 1) * C, :]
            kvs[ch] = lax.dot_general(kws[ch], v2, _TN, preferred_element_type=F32) * smask
        yield
        for ch in order:
            sc2 = jnp.where(tri, scs[ch], 0.0).astype(BF16)
            v2 = v_ref[0, ch * C:(ch + 1) * C, :]
            vbd = jnp.concatenate([jnp.concatenate([v2[:, :LANES], zero_v], axis=1),
                                   jnp.concatenate([zero_v, v2[:, LANES:]], axis=1)], axis=0)
            intra[ch] = jnp.dot(sc2, vbd, preferred_element_type=F32)
        yield
        pad = jnp.zeros((LANES - len(order), LANES), F32)
        dec_cols = jnp.exp(jnp.concatenate([decs[ch] for ch in order] + [pad], axis=0).T)
        st = s_ref[...]
        starts = {}
        for i, ch in enumerate(order):
            starts[ch] = st.astype(BF16)
            st = st * dec_cols[:, i:i + 1] + kvs[ch]
        s_ref[...] = st
        yield
        for ch in order:
            inter = jnp.dot(qds[ch], starts[ch], preferred_element_type=F32)
            o_ref[0, ch * C:(ch + 1) * C, :] = inter + intra[ch]
        yield


def _gla_body(kf, qf, vf, laf, kb, qb, vb, lab, h0f, h0b, of, ob, hf_out, hb_out, sf, sb, *, nchunk):
    i = pl.program_id(2)

    @pl.when(i == 0)
    def _():
        sf[...] = h0f[0, 0]
        sb[...] = h0b[0, 0]

    sweeps = [_gla_direction(kf, qf, vf, laf, of, sf, False, nchunk),
              _gla_direction(kb, qb, vb, lab, ob, sb, True, nchunk)]
    for _ in range(GLA_STAGES * pl.cdiv(nchunk, GLA_BATCH)):
        for sweep in sweeps:
            next(sweep)

    @pl.when(i == pl.num_programs(2) - 1)
    def _():
        hf_out[0, 0] = sf[...]
        hb_out[0, 0] = sb[...]


def _gla(k, q, v, laf, lab, h0f, h0b, lt):
    bsz, l, _ = k.shape
    pairs = h0f.shape[1]
    nt = l // lt
    fwd = lambda b, hp, i: (b, i, hp)
    bwd = lambda b, hp, i: (b, nt - 1 - i, hp)
    st = lambda b, hp, i: (b, hp, 0, 0)
    kq = lambda m: pl.BlockSpec((1, lt, LANES), m)
    vv = lambda m: pl.BlockSpec((1, lt, 2 * LANES), m)
    sspec = pl.BlockSpec((1, 1, LANES, 2 * LANES), st)
    return pl.pallas_call(
        functools.partial(_gla_body, nchunk=lt // GLA_CHUNK),
        grid=(bsz, pairs, nt),
        in_specs=[kq(fwd), kq(fwd), vv(fwd), kq(fwd), kq(bwd), kq(bwd), vv(bwd), kq(bwd), sspec, sspec],
        out_specs=[vv(fwd), vv(bwd), sspec, sspec],
        out_shape=[jax.ShapeDtypeStruct(v.shape, F32), jax.ShapeDtypeStruct(v.shape, F32),
                   jax.ShapeDtypeStruct(h0f.shape, F32), jax.ShapeDtypeStruct(h0b.shape, F32)],
        scratch_shapes=[pltpu.VMEM((LANES, 2 * LANES), F32), pltpu.VMEM((LANES, 2 * LANES), F32)],
        compiler_params=_cparams(("parallel", "parallel", "arbitrary")),
        name="gla",
    )(k, q, v, laf, k, q, v, lab, h0f, h0b)


def _pool_body(p_ref, o_ref, s1_ref, *, half, rows):
    xb = p_ref[0]
    l, ch = xb.shape

    def inv_counts(pos, n):
        return 1.0 / (jnp.minimum(pos + half, n) - jnp.maximum(pos - half, 0)).astype(F32)

    blk = MXU_DIM
    r = lax.broadcasted_iota(I32, (blk, blk), 0)
    c = lax.broadcasted_iota(I32, (blk, blk), 1)
    band = ((r // GRID_W == c // GRID_W) & (c - r >= -half) & (c - r <= half - 1)).astype(BF16)
    for i in range(0, l // blk, 2):
        pair = jnp.concatenate([xb[i * blk:(i + 1) * blk], xb[(i + 1) * blk:(i + 2) * blk]], axis=1)
        sums = jnp.dot(band, pair, preferred_element_type=F32)
        s1_ref[i * blk:(i + 1) * blk, :] = sums[:, :ch]
        s1_ref[(i + 1) * blk:(i + 2) * blk, :] = sums[:, ch:]
    s1 = s1_ref[...].reshape(rows, GRID_W, ch)

    def shifted(a, s):
        z = jnp.zeros((abs(s),) + a.shape[1:], a.dtype)
        return jnp.concatenate([a[s:], z], axis=0) if s > 0 else jnp.concatenate([z, a[:s]], axis=0)

    fwd = s1
    bwd = shifted(s1, -1)
    s = 1
    while s < half:
        fwd = fwd + shifted(fwd, s)
        bwd = bwd + shifted(bwd, -s)
        s *= 2
    inv_r = inv_counts(lax.broadcasted_iota(I32, (rows, 1, ch), 0), rows)
    inv_c = inv_counts(lax.broadcasted_iota(I32, (1, GRID_W, ch), 1), GRID_W)
    pooled = (fwd + bwd) * inv_r * inv_c - xb.astype(F32).reshape(rows, GRID_W, ch)
    o_ref[0] = pooled.reshape(l, ch).astype(BF16)


def _pool_groups_body(p_ref, o_ref, s1_ref, *, rows):
    for gi, window in enumerate(POOL_WINDOWS):
        @pl.when(pl.program_id(1) == gi)
        def _(half=window // 2):
            _pool_body(p_ref, o_ref, s1_ref, half=half, rows=rows)


def _pool(pin, ch):
    bsz, l, pw = pin.shape
    return pl.pallas_call(
        functools.partial(_pool_groups_body, rows=l // GRID_W),
        grid=(bsz, pw // ch),
        in_specs=[pl.BlockSpec((1, l, ch), lambda b, g: (b, 0, g))],
        out_specs=pl.BlockSpec((1, l, ch), lambda b, g: (b, 0, g)),
        out_shape=jax.ShapeDtypeStruct((bsz, l, pw), BF16),
        scratch_shapes=[pltpu.VMEM((l, ch), F32)],
        compiler_params=_cparams(("parallel", "parallel")),
        name="pool",
    )(pin)


def _merge_body(x_ref, m1_ref, s1_ref, g1_ref, m2_ref, s2_ref,
                of_ref, ob_ref, sg_ref, mx_ref,
                wm_ref, wgla_ref, wpool_ref, wout_ref, gn_ref, wr_ref,
                x1_ref, h2_ref, aff_ref, *, heads, ne):
    d = x_ref.shape[1]

    def sub_tile(rows):
        bp = jnp.dot(mx_ref[rows, :], wpool_ref[...], preferred_element_type=F32)
        x = x_ref[rows, :]
        h = _modulated_norm(x, m1_ref[0], s1_ref[0]).astype(BF16)
        o = of_ref[rows, :] + ob_ref[rows, :]
        sg = sg_ref[rows, :].astype(F32)
        og = []
        for j in range(heads):
            oj = o[:, j * LANES:(j + 1) * LANES]
            oj = oj * lax.rsqrt(jnp.mean(oj * oj, axis=-1, keepdims=True) + EPS) * gn_ref[...]
            og.append((oj * sg[:, j * LANES:(j + 1) * LANES]).astype(BF16))
        og = jnp.concatenate(og, axis=1)
        yield
        gates = jnp.dot(h, wm_ref[...], preferred_element_type=F32)
        bg = jnp.dot(og, wgla_ref[...], preferred_element_type=F32)
        yield
        gates = _sigmoid(gates)
        z = (gates[:, :d] * bg + gates[:, d:] * bp).astype(BF16)
        yield
        y = jnp.dot(z, wout_ref[...], preferred_element_type=F32)
        yield
        x1 = x + g1_ref[0] * y
        x1_ref[rows, :] = x1
        h2 = _modulated_norm(x1, m2_ref[0], s2_ref[0])
        hi = h2.astype(BF16)
        h2_ref[rows, :] = hi
        lo = (h2 - hi.astype(F32)).astype(BF16)
        yield
        lg = (jnp.dot(hi, wr_ref[...], preferred_element_type=F32)
              + jnp.dot(lo, wr_ref[...], preferred_element_type=F32))
        yield
        lgt = lg.T
        logit = lgt[0:ne] + lgt[ne:2 * ne]
        mx = jnp.max(logit, axis=0, keepdims=True)
        ex = jnp.exp(logit - mx)
        aff_ref[0, :, rows] = ex / jnp.sum(ex, axis=0, keepdims=True)
        yield

    _staggered([sub_tile(r) for r in _row_groups(x_ref.shape[0])], 7)


def _merge(x2, mods, of, ob, sg, mixed, wm, wgla, wpool, wout, gn, wr, rows_per_sample, heads, ne, tm):
    n, d = x2.shape
    bsz = n // rows_per_sample
    tps = rows_per_sample // tm
    row = lambda i: (i, 0)
    sample_of = lambda i, *_: i // tps
    vecs = [MOD_MULT1, MOD_SHIFT1, MOD_GATE1, MOD_MULT2, MOD_SHIFT2]
    full = lambda a: pl.BlockSpec(a.shape, lambda i: (0,) * a.ndim, pipeline_mode=pl.Buffered(1))
    gv = of.shape[-1]
    return pl.pallas_call(
        functools.partial(_merge_body, heads=heads, ne=ne),
        grid=(n // tm,),
        in_specs=[pl.BlockSpec((tm, d), row)] + [_mod_spec(d, v, sample_of) for v in vecs]
                 + [pl.BlockSpec((tm, gv), row)] * 4
                 + [full(wm), full(wgla), full(wpool), full(wout), full(gn), full(wr)],
        out_specs=[pl.BlockSpec((tm, d), row),
                   pl.BlockSpec((tm, d), row),
                   pl.BlockSpec((1, ne, tm), lambda i: (i // tps, 0, i % tps))],
        out_shape=[jax.ShapeDtypeStruct((n, d), F32),
                   jax.ShapeDtypeStruct((n, d), BF16),
                   jax.ShapeDtypeStruct((bsz, ne, rows_per_sample), F32)],
        compiler_params=_cparams(("parallel",)),
        name="merge",
    )(x2, *[mods] * len(vecs), of, ob, sg, mixed, wm, wgla, wpool, wout, gn, wr)


def _route_body(aff_ref, pos_ref, off_ref, cnt_ref, *, cap, ntb):
    a = aff_ref[0]
    ne, l = a.shape
    blk = ROUTE_BLK

    def bisect(i, v):
        cand = v | jnp.left_shift(jnp.int32(1), 30 - i)
        cnt = jnp.sum((a >= lax.bitcast_convert_type(cand, F32)).astype(F32), axis=1, keepdims=True)
        return jnp.where(cnt >= cap, cand, v)

    thr = lax.bitcast_convert_type(lax.fori_loop(0, 31, bisect, jnp.zeros((ne, 1), I32)), F32)
    gt = a > thr
    tie = a == thr
    need = cap - jnp.sum(gt.astype(F32), axis=1, keepdims=True)

    r = lax.broadcasted_iota(I32, (blk, blk), 0)
    c = lax.broadcasted_iota(I32, (blk, blk), 1)
    upper = (r <= c).astype(BF16)
    lane = lax.broadcasted_iota(I32, (1, LANES), 1)

    def prefix(mask_f):
        run = jnp.zeros((ne, 1), F32)
        offs = jnp.zeros((ne, LANES), F32)
        for tb in range(ntb):
            m = mask_f[:, tb * blk:(tb + 1) * blk].astype(BF16)
            loc = jnp.dot(m, upper, preferred_element_type=F32)
            cnt_ref[:, tb * blk:(tb + 1) * blk] = loc + run
            offs = jnp.where(lane == tb, run, offs)
            run = run + loc[:, blk - 1:blk]
        return jnp.where(lane >= ntb, run, offs)

    tie_f = tie.astype(F32)
    prefix(tie_f)
    tie_excl = cnt_ref[...] - tie_f
    sel = gt | (tie & (tie_excl < need))
    offs = prefix(sel.astype(F32))
    pos_ref[0] = jnp.where(sel, cnt_ref[...] - 1.0, -1.0).astype(I32)
    off_ref[0] = offs.astype(I32)


def _route(aff, cap):
    bsz, ne, l = aff.shape
    spec = lambda s: pl.BlockSpec((1,) + s, lambda b: (b, 0, 0))
    return pl.pallas_call(
        functools.partial(_route_body, cap=cap, ntb=l // ROUTE_BLK),
        grid=(bsz,),
        in_specs=[spec((ne, l))],
        out_specs=[spec((ne, l)), spec((ne, LANES))],
        out_shape=[jax.ShapeDtypeStruct((bsz, ne, l), I32),
                   jax.ShapeDtypeStruct((bsz, ne, LANES), I32)],
        scratch_shapes=[pltpu.VMEM((ne, l), F32)],
        compiler_params=_cparams(("parallel",)),
        name="route",
    )(aff)


def _window_plan(off_ref, bb, tt, experts, ne):
    lows = [off_ref[(bb * ne + e) * LANES + tt] & -BF16_ROWS for e in experts]
    ends = [off_ref[(bb * ne + e) * LANES + tt + 1] for e in experts]
    return lows, ends


def _window_rounds(lows, ends):
    rounds = jnp.int32(0)
    for lo, hi in zip(lows, ends):
        rounds = jnp.maximum(rounds, lax.div(hi - lo + (SLOT_WIN - 1), jnp.int32(SLOT_WIN)))
    return rounds


def _gatherx_body(off_ref, h_ref, pos_ref, xs_ref, *, ne, cap, eg):
    b = pl.program_id(0)
    g = pl.program_id(1)
    tb = pl.program_id(2)
    win = SLOT_WIN

    @pl.when(tb == 0)
    def _():
        xs_ref[...] = jnp.zeros(xs_ref.shape, BF16)

    j_col = lax.broadcasted_iota(I32, (win, 1), 0)
    t = ROUTE_BLK
    nsub = h_ref.shape[0] // t
    experts = [g * eg + k for k in range(eg)]

    def select(sub, lows, r):
        starts = [pl.multiple_of(jnp.minimum(lows[k] + r * win, cap - win), BF16_ROWS) for k in range(eg)]
        pieces = []
        for k in range(eg):
            p = pos_ref[0, k:k + 1, sub * t:(sub + 1) * t]
            hit = (p - starts[k] == j_col) & (p >= lows[k] + r * win)
            pieces.append(jnp.where(hit, 1.0, 0.0).astype(BF16))
        sel = jnp.concatenate(pieces, axis=0)
        rows = jnp.dot(sel, h_ref[sub * t:(sub + 1) * t, :], preferred_element_type=F32).astype(BF16)
        return starts, rows

    def deposit(starts, rows):
        for k in range(eg):
            dst = (0, k, pl.ds(starts[k], win), slice(None))
            xs_ref[dst] = xs_ref[dst] + rows[k * win:(k + 1) * win]

    plans = [_window_plan(off_ref, b, tb * nsub + sub, experts, ne) for sub in range(nsub)]
    firsts = [select(sub, plans[sub][0], 0) for sub in range(nsub)]
    for starts, rows in firsts:
        deposit(starts, rows)
    for sub in range(nsub):
        lows, ends = plans[sub]

        def extra_round(r, carry, lows=lows, sub=sub):
            deposit(*select(sub, lows, r))
            return carry

        lax.fori_loop(1, _window_rounds(lows, ends), extra_round, 0)


def _gatherx(off_flat, h2, pos, cap):
    n, d = h2.shape
    bsz, ne, l = pos.shape
    t = min(l, 2 * TOKEN_TILE)
    ntb = l // t
    eg = SUBLANES
    grid_spec = pltpu.PrefetchScalarGridSpec(
        num_scalar_prefetch=1,
        grid=(bsz, ne // eg, ntb),
        in_specs=[pl.BlockSpec((t, d), lambda b, g, i, off: (b * ntb + i, 0)),
                  pl.BlockSpec((1, eg, t), lambda b, g, i, off: (b, g, i))],
        out_specs=pl.BlockSpec((1, eg, cap, d), lambda b, g, i, off: (b, g, 0, 0)),
    )
    return pl.pallas_call(
        functools.partial(_gatherx_body, ne=ne, cap=cap, eg=eg),
        grid_spec=grid_spec,
        out_shape=jax.ShapeDtypeStruct((bsz, ne, cap, d), BF16),
        compiler_params=_cparams(("parallel", "parallel", "arbitrary")),
        name="gatherx",
    )(off_flat, h2, pos)


def _moe_body(xs_ref, wg_ref, wu_ref, wd_ref, y_ref, *, rc):
    wg = wg_ref[0].astype(BF16)
    wu = wu_ref[0].astype(BF16)
    wd = wd_ref[0].astype(BF16)
    cap = xs_ref.shape[2]
    acts = []
    for ch in range(cap // rc):
        xs = xs_ref[0, 0, ch * rc:(ch + 1) * rc, :]
        gate = jnp.dot(xs, wg, preferred_element_type=F32)
        up = jnp.dot(xs, wu, preferred_element_type=F32)
        acts.append((_silu(gate) * up).astype(BF16))
    for ch in range(cap // rc):
        y_ref[0, 0, ch * rc:(ch + 1) * rc, :] = jnp.dot(acts[ch], wd, preferred_element_type=F32).astype(BF16)


def _moe(xs, wg, wu, wd):
    bsz, ne, cap, d = xs.shape
    de = wg.shape[2]
    slot = pl.BlockSpec((1, 1, cap, d), lambda e, b: (b, e, 0, 0))
    return pl.pallas_call(
        functools.partial(_moe_body, rc=min(cap, 2 * MXU_DIM)),
        grid=(ne, bsz),
        in_specs=[slot,
                  pl.BlockSpec((1, d, de), lambda e, b: (e, 0, 0)),
                  pl.BlockSpec((1, d, de), lambda e, b: (e, 0, 0)),
                  pl.BlockSpec((1, de, d), lambda e, b: (e, 0, 0))],
        out_specs=slot,
        out_shape=jax.ShapeDtypeStruct((bsz, ne, cap, d), BF16),
        compiler_params=_cparams(("parallel", "arbitrary")),
        name="moe",
    )(xs, wg, wu, wd)


def _combine_body(off_ref, x1_ref, g2_ref, fg_ref, pos_ref, aff_ref, y_hbm, o_ref, ybuf, spare, acc_ref, sem,
                  spare_sem, *, ne, cap, nsp, nsub, span):
    b = pl.program_id(0)
    i = pl.program_id(1)
    step = b * nsp + i
    nsteps = pl.num_programs(0) * nsp
    slot = step % 2
    win = SLOT_WIN
    t = ROUTE_BLK
    experts = range(ne)

    def window_starts(lows, r):
        return [pl.multiple_of(jnp.minimum(lows[e] + r * win, cap - win), BF16_ROWS) for e in experts]

    def span_starts(bb, ii):
        lows = _window_plan(off_ref, bb, ii * nsub, experts, ne)[0]
        return [pl.multiple_of(jnp.minimum(lows[e], cap - span), BF16_ROWS) for e in experts]

    def span_copy(bb, e, start, buf):
        return pltpu.make_async_copy(y_hbm.at[bb, e, pl.ds(start, span), :], ybuf.at[buf, e], sem.at[buf, e])

    def spare_copy(e, start):
        return pltpu.make_async_copy(y_hbm.at[b, e, pl.ds(start, win), :],
                                     spare.at[pl.ds(e * win, win), :], spare_sem.at[e])

    def start_step(bb, ii, buf):
        starts = span_starts(bb, ii)
        for e in experts:
            span_copy(bb, e, starts[e], buf).start(priority=e % 2)

    @pl.when(step == 0)
    def _():
        start_step(b, i, 0)

    nxt = jnp.minimum(step + 1, nsteps - 1)
    start_step(lax.div(nxt, jnp.int32(nsp)), lax.rem(nxt, jnp.int32(nsp)), 1 - slot)

    plans = [_window_plan(off_ref, b, i * nsub + sub, experts, ne) for sub in range(nsub)]
    mine = span_starts(b, i)
    firsts = [window_starts(plans[sub][0], 0) for sub in range(nsub)]
    inside = []
    for sub in range(nsub):
        ok = jnp.bool_(True)
        for e in experts:
            ok = ok & (firsts[sub][e] >= mine[e]) & (firsts[sub][e] + win <= mine[e] + span)
        inside.append(ok)
    j_col = lax.broadcasted_iota(I32, (win, 1), 0)

    def expand(sub, r, starts, rows, enabled=None):
        lows = plans[sub][0]
        pieces = []
        for e in experts:
            p = pos_ref[0, e:e + 1, sub * t:(sub + 1) * t]
            valid = p >= lows[e] + r * win
            hit = (p - starts[e] == j_col) & valid
            if enabled is not None:
                hit = hit & enabled
            pieces.append(jnp.where(hit, aff_ref[0, e:e + 1, sub * t:(sub + 1) * t], 0.0).astype(BF16))
        pmat = jnp.concatenate(pieces, axis=0)
        return lax.dot_general(pmat, rows, _TN, preferred_element_type=F32)

    acc_ref[...] = jnp.zeros(acc_ref.shape, F32)
    for sub in range(nsub):
        lows, ends = plans[sub]

        def own_round(r, carry, lows=lows, sub=sub):
            starts = window_starts(lows, r)
            for e in experts:
                spare_copy(e, starts[e]).start()
            for e in experts:
                spare_copy(e, starts[e]).wait()
            acc_ref[sub] += expand(sub, r, starts, spare[...])
            return carry

        lax.fori_loop(jnp.where(inside[sub], 1, 0), _window_rounds(lows, ends), own_round, 0)

    for e in experts:
        span_copy(b, e, mine[e], slot).wait()
    for sub in range(nsub):
        rows = pl.ds(sub * t, t)
        picked = []
        for e in experts:
            local = pl.multiple_of(jnp.clip(firsts[sub][e] - mine[e], 0, span - win), BF16_ROWS)
            picked.append(ybuf[slot, e, pl.ds(local, win), :])
        moe = expand(sub, 0, firsts[sub], jnp.concatenate(picked, axis=0), inside[sub]) + acc_ref[sub]
        x2 = x1_ref[rows, :] + g2_ref[0] * moe
        ms = jnp.mean(x2 * x2, axis=-1, keepdims=True)
        o_ref[rows, :] = x2 * lax.rsqrt(ms + EPS) * fg_ref[...]

    @pl.when(step == nsteps - 1)
    def _():
        for e in experts:
            span_copy(b, e, mine[e], 1 - slot).wait()


def _combine(off_flat, x1, mods, fg, pos, aff, y, rows_per_sample):
    n, d = x1.shape
    bsz, ne, cap, _ = y.shape
    t = min(rows_per_sample, TOKEN_TILE)
    nsub = t // ROUTE_BLK
    nsp = rows_per_sample // t
    rows = ne * SLOT_WIN
    span = min(cap, nsub * SLOT_WIN * 3 // 4)
    grid_spec = pltpu.PrefetchScalarGridSpec(
        num_scalar_prefetch=1,
        grid=(bsz, nsp),
        in_specs=[pl.BlockSpec((t, d), lambda b, i, off: (b * nsp + i, 0)),
                  _mod_spec(d, MOD_GATE2, lambda b, i: b),
                  pl.BlockSpec((1, d), lambda b, i, off: (0, 0)),
                  pl.BlockSpec((1, ne, t), lambda b, i, off: (b, 0, i)),
                  pl.BlockSpec((1, ne, t), lambda b, i, off: (b, 0, i)),
                  pl.BlockSpec(memory_space=pl.ANY)],
        out_specs=pl.BlockSpec((t, d), lambda b, i, off: (b * nsp + i, 0)),
        scratch_shapes=[pltpu.VMEM((2, ne, span, d), BF16),
                        pltpu.VMEM((rows, d), BF16),
                        pltpu.VMEM((nsub, ROUTE_BLK, d), F32),
                        pltpu.SemaphoreType.DMA((2, ne)),
                        pltpu.SemaphoreType.DMA((ne,))],
    )
    return pl.pallas_call(
        functools.partial(_combine_body, ne=ne, cap=cap, nsp=nsp, nsub=nsub, span=span),
        grid_spec=grid_spec,
        out_shape=jax.ShapeDtypeStruct((n, d), F32),
        compiler_params=_cparams(("arbitrary", "arbitrary")),
        name="combine",
    )(off_flat, x1, mods, fg, pos, aff, y)


def kernel(x, c, ctx, c_ctx, ada_w, ada_b, norm1_g, norm2_g, w_in, w_decay_up, b_decay, gla_norm_g,
           w_gla_proj, pool_w, pool_scale, w_pool_proj, w_out, w_router, w_gate_e, w_up_e, w_down_e,
           final_norm_g):
    assert ada_w.shape[0] == 1, "single-layer block"
    bsz, l, d = x.shape
    lc = ctx.shape[1]
    rank, dk = w_decay_up.shape[2], w_decay_up.shape[3]
    dvh = gla_norm_g.shape[1]
    dv = w_gla_proj.shape[1]
    heads = dv // dvh
    groups, ch = pool_w.shape[1], pool_w.shape[2]
    pw = groups * ch
    ne = w_router.shape[2]
    cap = EC_CAPACITY * l // ne
    assert dk // heads == HEAD_DK and dvh == LANES and ch == LANES and heads % 2 == 0
    assert l % ROUTE_BLK == 0 and l // ROUTE_BLK < LANES and cap >= SLOT_WIN and cap % BF16_ROWS == 0
    assert GRID_W == GLA_CHUNK and MXU_DIM % GRID_W == 0 and (l // MXU_DIM) % 2 == 0

    assert bsz < COND_ROWS
    cin = jnp.zeros((COND_ROWS, d), F32).at[:bsz].set(c).at[bsz].set(c_ctx)
    mods = _ada(cin, ada_w[0], ada_b[0][None, :], jnp.concatenate([norm1_g, norm2_g], axis=0))

    o_r = dk + dv
    o_q = o_r + 2 * rank
    o_g = o_q + dk
    o_p = o_g + dv
    o_m = o_p + pw
    up = jnp.zeros((2 * rank, 2 * dk), F32)
    up = up.at[:rank, :dk].set(w_decay_up[0, 0]).at[rank:, dk:].set(w_decay_up[0, 1])
    psc = pool_scale[0].reshape(groups, 1, ch)
    w1, wm, wpool = _prep(w_in[0].T, up, pool_w[0], psc, w_pool_proj[0], o_r, o_q, o_g, o_m,
                          float(dk // heads) ** -0.5)
    bz = b_decay[0]

    zero_state = jnp.zeros((bsz, heads // 2, LANES, 2 * LANES), F32)
    ck, cv, _, claf, clab, _, _ = _inproj(ctx.reshape(bsz * lc, d), mods, lambda i: bsz, w1, bz, dk, dv, pw,
                                          min(lc, CTX_TILE))
    r3 = lambda a, n: a.reshape(bsz, n, a.shape[-1])
    _, _, h_f, h_b = _gla(r3(ck, lc), r3(ck, lc), r3(cv, lc), r3(claf, lc), r3(clab, lc),
                          zero_state, zero_state, min(lc, CTX_TILE))

    x2 = x.reshape(bsz * l, d)
    tm = min(l, TOKEN_TILE)
    k, v, q, laf, lab, sg, pin = _inproj(x2, mods, lambda i: i // (l // tm), w1, bz, dk, dv, pw, tm)
    of, ob, _, _ = _gla(r3(k, l), r3(q, l), r3(v, l), r3(laf, l), r3(lab, l), h_f, h_b, min(l, GLA_TILE))
    pooled = _pool(r3(pin, l), ch).reshape(bsz * l, pw)

    wr_hi = w_router[0].astype(BF16)
    wr_lo = (w_router[0] - wr_hi.astype(F32)).astype(BF16)
    wr = jnp.zeros((d, LANES), BF16).at[:, :ne].set(wr_hi).at[:, ne:2 * ne].set(wr_lo)
    x1, h2, aff = _merge(x2, mods, of.reshape(bsz * l, dv), ob.reshape(bsz * l, dv), sg, pooled,
                          wm, w_gla_proj[0].astype(BF16), wpool, w_out[0].astype(BF16),
                          gla_norm_g[0][None, :], wr, l, heads, ne, tm)

    pos, offs = _route(aff.reshape(1, bsz * ne, l), cap)
    pos = pos.reshape(bsz, ne, l)
    off_flat = offs.reshape(-1)
    xs = _gatherx(off_flat, h2, pos, cap)
    y = _moe(xs, w_gate_e[0], w_up_e[0], w_down_e[0])
    out = _combine(off_flat, x1, mods, final_norm_g[None, :], pos, aff, y, l)
    return out.reshape(bsz, l, d)
```

```python
import functools

import jax
import jax.numpy as jnp
from jax import lax
from jax.experimental import pallas as pl
from jax.experimental.pallas import tpu as pltpu

F32 = jnp.float32
BF16 = jnp.bfloat16
I32 = jnp.int32
HIGHEST = lax.Precision.HIGHEST

EPS = 1e-6
GRID_W = 64
GLA_CHUNK = 64
GLA_STAGES = 6
GLA_BATCH = 8
GATE_NORMALIZER = 16.0
POOL_WINDOWS = (2, 4, 8, 16)
EC_CAPACITY = 2

LANES = 128
SUBLANES = 8
BF16_ROWS = 16
MXU_DIM = 256
VMEM_BYTES = 64 * 1024 * 1024
VMEM_LIMIT = VMEM_BYTES * 7 // 8

HEAD_DK = LANES // 2
ROUTE_BLK = MXU_DIM
SLOT_WIN = 64
SUB_ROWS = MXU_DIM
TOKEN_TILE = 1024
GLA_TILE = 2048
CTX_TILE = 256
COND_ROWS = SUBLANES
MOD_SHIFT1, MOD_MULT1, MOD_GATE1, MOD_SHIFT2, MOD_MULT2, MOD_GATE2 = range(6)


def _mod_spec(d, which, sample_of):
    return pl.BlockSpec((1, 1, d), lambda *g: (which * COND_ROWS + sample_of(*g[:2]), 0, 0))

_NT = (((1,), (1,)), ((), ()))
_TN = (((0,), (0,)), ((), ()))


def _cparams(sem):
    return pltpu.CompilerParams(dimension_semantics=sem, vmem_limit_bytes=VMEM_LIMIT)


def _sigmoid(x):
    return 0.5 * jnp.tanh(0.5 * x) + 0.5


def _silu(x):
    return x * _sigmoid(x)


def _row_groups(n):
    step = min(n, SUB_ROWS)
    return [pl.ds(i, step) for i in range(0, n, step)]


def _staggered(gens, nstages):
    for t in range(nstages + len(gens) - 1):
        for g in reversed(range(len(gens))):
            if 0 <= t - g < nstages:
                next(gens[g])


def _ada_body(c_ref, w_ref, b_ref, g_ref, o_ref):
    s = _silu(c_ref[...])
    rows = s.shape[0]
    s_hi = s.astype(BF16)
    s_lo = (s - s_hi.astype(F32)).astype(BF16)
    w = w_ref[...]
    w_hi = w.astype(BF16)
    w_lo = (w - w_hi.astype(F32)).astype(BF16)
    both = jnp.dot(jnp.concatenate([s_hi, s_lo], axis=0), w_hi, preferred_element_type=F32)
    out = (both[:rows] + both[rows:] + jnp.dot(s_hi, w_lo, preferred_element_type=F32)) + b_ref[...]
    j = pl.program_id(0)
    gain = jnp.where(j == MOD_MULT1, g_ref[0:1, :], g_ref[1:2, :])
    o_ref[...] = jnp.where((j == MOD_MULT1) | (j == MOD_MULT2), gain * (1.0 + out), out)


def _ada(cin, w, b, gains):
    rows, d = cin.shape
    nvec = w.shape[1] // d
    mods = pl.pallas_call(
        _ada_body,
        grid=(nvec,),
        in_specs=[pl.BlockSpec((rows, d), lambda j: (0, 0)),
                  pl.BlockSpec((d, d), lambda j: (0, j)),
                  pl.BlockSpec((1, d), lambda j: (0, j)),
                  pl.BlockSpec(gains.shape, lambda j: (0, 0))],
        out_specs=pl.BlockSpec((rows, d), lambda j: (j, 0)),
        out_shape=jax.ShapeDtypeStruct((nvec * rows, d), F32),
        compiler_params=_cparams(("arbitrary",)),
        name="ada",
    )(cin, w, b, gains)
    return mods.reshape(nvec * rows, 1, d)


def _prep_body(w_ref, up_ref, pw_ref, ps_ref, wpp_ref, w1_ref, wm_ref, wpf_ref, *, o_r, o_q, o_g, o_m, qscale):
    dk = o_g - o_q
    wz = lax.dot_general(w_ref[o_r:o_q, :], up_ref[...], _TN, precision=HIGHEST, preferred_element_type=F32)
    nz = wz.shape[1]
    w1_ref[:, :o_r] = w_ref[:o_r, :].T.astype(BF16)
    w1_ref[:, o_r:o_r + dk] = (w_ref[o_q:o_g, :].T * qscale).astype(BF16)
    w1_ref[:, o_r + dk:o_r + dk + nz] = wz.astype(BF16)
    w1_ref[:, o_r + dk + nz:] = w_ref[o_g:o_m, :].T.astype(BF16)
    wm_ref[...] = w_ref[o_m:, :].T.astype(BF16)
    wpf_ref[...] = jnp.dot(pw_ref[0] * ps_ref[0], wpp_ref[...], precision=HIGHEST,
                           preferred_element_type=F32).astype(BF16)


def _prep(w_in_t, up, pool_w, pool_scale3, w_pool_proj, o_r, o_q, o_g, o_m, qscale):
    n, d = w_in_t.shape
    n1 = o_r + (o_g - o_q) + up.shape[1] + (o_m - o_g)
    groups, ch, _ = pool_w.shape
    tr = d // groups
    dm = w_pool_proj.shape[1]
    return pl.pallas_call(
        functools.partial(_prep_body, o_r=o_r, o_q=o_q, o_g=o_g, o_m=o_m, qscale=qscale),
        grid=(d // tr,),
        in_specs=[pl.BlockSpec((n, tr), lambda i: (0, i)),
                  pl.BlockSpec(up.shape, lambda i: (0, 0)),
                  pl.BlockSpec((1, ch, ch), lambda i: (i, 0, 0)),
                  pl.BlockSpec((1, 1, ch), lambda i: (i, 0, 0)),
                  pl.BlockSpec((ch, dm), lambda i: (i, 0))],
        out_specs=[pl.BlockSpec((tr, n1), lambda i: (i, 0)),
                   pl.BlockSpec((tr, n - o_m), lambda i: (i, 0)),
                   pl.BlockSpec((ch, dm), lambda i: (i, 0))],
        out_shape=[jax.ShapeDtypeStruct((d, n1), BF16), jax.ShapeDtypeStruct((d, n - o_m), BF16),
                   jax.ShapeDtypeStruct((groups * ch, dm), BF16)],
        compiler_params=_cparams(("parallel",)),
        name="prep",
    )(w_in_t, up, pool_w, pool_scale3, w_pool_proj)


def _modulated_norm(x, mult, shift):
    ms = jnp.mean(x * x, axis=-1, keepdims=True)
    return (x * lax.rsqrt(ms + EPS)) * mult + shift


def _log_sigmoid(z):
    return jnp.minimum(z, 0.0) - jnp.log1p(jnp.exp(-jnp.abs(z)))


def _inproj_body(x_ref, mult_ref, shift_ref, w_ref, bz_ref,
                 k_ref, v_ref, q_ref, laf_ref, lab_ref, sg_ref, p_ref, *, dk, dv, pw):
    def sub_tile(rows):
        h = _modulated_norm(x_ref[rows, :], mult_ref[0], shift_ref[0]).astype(BF16)
        yield
        u = jnp.dot(h, w_ref[...], preferred_element_type=F32)
        yield
        o = 0
        k_ref[rows, :] = u[:, o:o + dk].astype(BF16); o += dk
        v_ref[rows, :] = u[:, o:o + dv].astype(BF16); o += dv
        q_ref[rows, :] = u[:, o:o + dk].astype(BF16); o += dk
        zf = u[:, o:o + dk] + bz_ref[0:1, :]; o += dk
        zb = u[:, o:o + dk] + bz_ref[1:2, :]; o += dk
        laf_ref[rows, :] = _log_sigmoid(zf) * (1.0 / GATE_NORMALIZER)
        lab_ref[rows, :] = _log_sigmoid(zb) * (1.0 / GATE_NORMALIZER)
        sg_ref[rows, :] = _silu(u[:, o:o + dv]).astype(BF16); o += dv
        p_ref[rows, :] = u[:, o:o + pw].astype(BF16)
        yield

    _staggered([sub_tile(r) for r in _row_groups(x_ref.shape[0])], 3)


def _inproj(x2, mods, cond_row, w, bz, dk, dv, pw, tm):
    n, d = x2.shape
    row = lambda i: (i, 0)
    sample_of = lambda i, *_: cond_row(i)
    outs = [(dk, BF16), (dv, BF16), (dk, BF16), (dk, F32), (dk, F32), (dv, BF16), (pw, BF16)]
    return pl.pallas_call(
        functools.partial(_inproj_body, dk=dk, dv=dv, pw=pw),
        grid=(n // tm,),
        in_specs=[pl.BlockSpec((tm, d), row),
                  _mod_spec(d, MOD_MULT1, sample_of),
                  _mod_spec(d, MOD_SHIFT1, sample_of),
                  pl.BlockSpec(w.shape, lambda i: (0, 0), pipeline_mode=pl.Buffered(1)),
                  pl.BlockSpec(bz.shape, lambda i: (0, 0))],
        out_specs=[pl.BlockSpec((tm, c), row) for c, _ in outs],
        out_shape=[jax.ShapeDtypeStruct((n, c), t) for c, t in outs],
        compiler_params=_cparams(("parallel",)),
        name="inproj",
    )(x2, mods, mods, w, bz)


def _gla_direction(k_ref, q_ref, v_ref, la_ref, o_ref, s_ref, reverse, nchunk):
    C = GLA_CHUNK
    hd = HEAD_DK
    cb = MXU_DIM
    lt = nchunk * C
    la = la_ref[0]
    r = lax.broadcasted_iota(I32, (cb, cb), 0)
    c = lax.broadcasted_iota(I32, (cb, cb), 1)
    same = (r // C) == (c // C)
    cum = jnp.where(same & ((c >= r) if reverse else (c <= r)), 1.0, 0.0).astype(BF16)
    la_hi = la.astype(BF16)
    la_lo = (la - la_hi.astype(F32)).astype(BF16)
    la2 = jnp.concatenate([la_hi, la_lo], axis=1)
    bcs = []
    for blk in range(lt // cb):
        part = jnp.dot(cum, la2[blk * cb:(blk + 1) * cb], preferred_element_type=F32)
        bcs.append(part[:, :LANES] + part[:, LANES:])
    ri = lax.broadcasted_iota(I32, (C, 2 * C), 0)
    ci = lax.broadcasted_iota(I32, (C, 2 * C), 1) % C
    tri = (ci >= ri) if reverse else (ci <= ri)
    lane = lax.broadcasted_iota(I32, (1, LANES), 1)
    m0 = (lane < hd).astype(F32)
    m1 = (lane >= hd).astype(F32)
    sr = lax.broadcasted_iota(I32, (LANES, 2 * LANES), 0)
    sl = lax.broadcasted_iota(I32, (LANES, 2 * LANES), 1)
    smask = ((sr < hd) == (sl < LANES)).astype(F32)
    kt = k_ref[0].astype(F32)
    qt = q_ref[0].astype(F32)
    zero_v = jnp.zeros((C, LANES), BF16)
    sweep = list(range(nchunk - 1, -1, -1) if reverse else range(nchunk))
    for first in range(0, nchunk, GLA_BATCH):
        order = sweep[first:first + GLA_BATCH]
        intra, qds, kvs, decs, q2s, kss, kws, scs = {}, {}, {}, {}, {}, {}, {}, {}
        for ch in order:
            lo = ch * C
            b = bcs[lo // cb][lo % cb:lo % cb + C]
            last = b[0:1] if reverse else b[C - 1:C]
            mid = b[C // 2:C // 2 + 1] if reverse else b[C // 2 - 1:C // 2]
            kc = kt[lo:lo + C]
            qc = qt[lo:lo + C]
            q2s[ch] = (qc * jnp.exp(b - mid)).astype(BF16)
            ks = kc * jnp.exp(mid - b)
            kss[ch] = jnp.concatenate([ks * m0, ks * m1], axis=0).astype(BF16)
            qds[ch] = (qc * jnp.exp(b)).astype(BF16)
            kws[ch] = (kc * jnp.exp(last - b)).astype(BF16)
            decs[ch] = last
        yield
        for ch in order:
            scs[ch] = lax.dot_general(q2s[ch], kss[ch], _NT, preferred_element_type=F32)
        yield
        for ch in order:
            v2 = v_ref[0, ch * C:(ch + 1) * C, :]
            kvs[ch] = lax.dot_general(kws[ch], v2, _TN, preferred_element_type=F32) * smask
        yield
        for ch in order:
            sc2 = jnp.where(tri, scs[ch], 0.0).astype(BF16)
            v2 = v_ref[0, ch * C:(ch + 1) * C, :]
            vbd = jnp.concatenate([jnp.concatenate([v2[:, :LANES], zero_v], axis=1),
                                   jnp.concatenate([zero_v, v2[:, LANES:]], axis=1)], axis=0)
            intra[ch] = jnp.dot(sc2, vbd, preferred_element_type=F32)
        yield
        pad = jnp.zeros((LANES - len(order), LANES), F32)
        dec_cols = jnp.exp(jnp.concatenate([decs[ch] for ch in order] + [pad], axis=0).T)
        st = s_ref[...]
        starts = {}
        for i, ch in enumerate(order):
            starts[ch] = st.astype(BF16)
            st = st * dec_cols[:, i:i + 1] + kvs[ch]
        s_ref[...] = st
        yield
        for ch in order:
            inter = jnp.dot(qds[ch], starts[ch], preferred_element_type=F32)
            o_ref[0, ch * C:(ch + 1) * C, :] = inter + intra[ch]
        yield


def _gla_body(kf, qf, vf, laf, kb, qb, vb, lab, h0f, h0b, of, ob, hf_out, hb_out, sf, sb, *, nchunk):
    i = pl.program_id(2)

    @pl.when(i == 0)
    def _():
        sf[...] = h0f[0, 0]
        sb[...] = h0b[0, 0]

    sweeps = [_gla_direction(kf, qf, vf, laf, of, sf, False, nchunk),
              _gla_direction(kb, qb, vb, lab, ob, sb, True, nchunk)]
    for _ in range(GLA_STAGES * pl.cdiv(nchunk, GLA_BATCH)):
        for sweep in sweeps:
            next(sweep)

    @pl.when(i == pl.num_programs(2) - 1)
    def _():
        hf_out[0, 0] = sf[...]
        hb_out[0, 0] = sb[...]


def _gla(k, q, v, laf, lab, h0f, h0b, lt):
    bsz, l, _ = k.shape
    pairs = h0f.shape[1]
    nt = l // lt
    fwd = lambda b, hp, i: (b, i, hp)
    bwd = lambda b, hp, i: (b, nt - 1 - i, hp)
    st = lambda b, hp, i: (b, hp, 0, 0)
    kq = lambda m: pl.BlockSpec((1, lt, LANES), m)
    vv = lambda m: pl.BlockSpec((1, lt, 2 * LANES), m)
    sspec = pl.BlockSpec((1, 1, LANES, 2 * LANES), st)
    return pl.pallas_call(
        functools.partial(_gla_body, nchunk=lt // GLA_CHUNK),
        grid=(bsz, pairs, nt),
        in_specs=[kq(fwd), kq(fwd), vv(fwd), kq(fwd), kq(bwd), kq(bwd), vv(bwd), kq(bwd), sspec, sspec],
        out_specs=[vv(fwd), vv(bwd), sspec, sspec],
        out_shape=[jax.ShapeDtypeStruct(v.shape, F32), jax.ShapeDtypeStruct(v.shape, F32),
                   jax.ShapeDtypeStruct(h0f.shape, F32), jax.ShapeDtypeStruct(h0b.shape, F32)],
        scratch_shapes=[pltpu.VMEM((LANES, 2 * LANES), F32), pltpu.VMEM((LANES, 2 * LANES), F32)],
        compiler_params=_cparams(("parallel", "parallel", "arbitrary")),
        name="gla",
    )(k, q, v, laf, k, q, v, lab, h0f, h0b)


def _pool_body(p_ref, o_ref, s1_ref, *, half, rows):
    xb = p_ref[0]
    l, ch = xb.shape

    def inv_counts(pos, n):
        return 1.0 / (jnp.minimum(pos + half, n) - jnp.maximum(pos - half, 0)).astype(F32)

    blk = MXU_DIM
    r = lax.broadcasted_iota(I32, (blk, blk), 0)
    c = lax.broadcasted_iota(I32, (blk, blk), 1)
    band = ((r // GRID_W == c // GRID_W) & (c - r >= -half) & (c - r <= half - 1)).astype(BF16)
    for i in range(0, l // blk, 2):
        pair = jnp.concatenate([xb[i * blk:(i + 1) * blk], xb[(i + 1) * blk:(i + 2) * blk]], axis=1)
        sums = jnp.dot(band, pair, preferred_element_type=F32)
        s1_ref[i * blk:(i + 1) * blk, :] = sums[:, :ch]
        s1_ref[(i + 1) * blk:(i + 2) * blk, :] = sums[:, ch:]
    s1 = s1_ref[...].reshape(rows, GRID_W, ch)

    def shifted(a, s):
        z = jnp.zeros((abs(s),) + a.shape[1:], a.dtype)
        return jnp.concatenate([a[s:], z], axis=0) if s > 0 else jnp.concatenate([z, a[:s]], axis=0)

    fwd = s1
    bwd = shifted(s1, -1)
    s = 1
    while s < half:
        fwd = fwd + shifted(fwd, s)
        bwd = bwd + shifted(bwd, -s)
        s *= 2
    inv_r = inv_counts(lax.broadcasted_iota(I32, (rows, 1, ch), 0), rows)
    inv_c = inv_counts(lax.broadcasted_iota(I32, (1, GRID_W, ch), 1), GRID_W)
    pooled = (fwd + bwd) * inv_r * inv_c - xb.astype(F32).reshape(rows, GRID_W, ch)
    o_ref[0] = pooled.reshape(l, ch).astype(BF16)


def _pool_groups_body(p_ref, o_ref, s1_ref, *, rows):
    for gi, window in enumerate(POOL_WINDOWS):
        @pl.when(pl.program_id(1) == gi)
        def _(half=window // 2):
            _pool_body(p_ref, o_ref, s1_ref, half=half, rows=rows)


def _pool(pin, ch):
    bsz, l, pw = pin.shape
    return pl.pallas_call(
        functools.partial(_pool_groups_body, rows=l // GRID_W),
        grid=(bsz, pw // ch),
        in_specs=[pl.BlockSpec((1, l, ch), lambda b, g: (b, 0, g))],
        out_specs=pl.BlockSpec((1, l, ch), lambda b, g: (b, 0, g)),
        out_shape=jax.ShapeDtypeStruct((bsz, l, pw), BF16),
        scratch_shapes=[pltpu.VMEM((l, ch), F32)],
        compiler_params=_cparams(("parallel", "parallel")),
        name="pool",
    )(pin)


def _merge_body(x_ref, m1_ref, s1_ref, g1_ref, m2_ref, s2_ref,
                of_ref, ob_ref, sg_ref, mx_ref,
                wm_ref, wgla_ref, wpool_ref, wout_ref, gn_ref, wr_ref,
                x1_ref, h2_ref, aff_ref, *, heads, ne):
    d = x_ref.shape[1]

    def sub_tile(rows):
        bp = jnp.dot(mx_ref[rows, :], wpool_ref[...], preferred_element_type=F32)
        x = x_ref[rows, :]
        h = _modulated_norm(x, m1_ref[0], s1_ref[0]).astype(BF16)
        o = of_ref[rows, :] + ob_ref[rows, :]
        sg = sg_ref[rows, :].astype(F32)
        og = []
        for j in range(heads):
            oj = o[:, j * LANES:(j + 1) * LANES]
            oj = oj * lax.rsqrt(jnp.mean(oj * oj, axis=-1, keepdims=True) + EPS) * gn_ref[...]
            og.append((oj * sg[:, j * LANES:(j + 1) * LANES]).astype(BF16))
        og = jnp.concatenate(og, axis=1)
        yield
        gates = jnp.dot(h, wm_ref[...], preferred_element_type=F32)
        bg = jnp.dot(og, wgla_ref[...], preferred_element_type=F32)
        yield
        gates = _sigmoid(gates)
        z = (gates[:, :d] * bg + gates[:, d:] * bp).astype(BF16)
        yield
        y = jnp.dot(z, wout_ref[...], preferred_element_type=F32)
        yield
        x1 = x + g1_ref[0] * y
        x1_ref[rows, :] = x1
        h2 = _modulated_norm(x1, m2_ref[0], s2_ref[0])
        hi = h2.astype(BF16)
        h2_ref[rows, :] = hi
        lo = (h2 - hi.astype(F32)).astype(BF16)
        yield
        lg = (jnp.dot(hi, wr_ref[...], preferred_element_type=F32)
              + jnp.dot(lo, wr_ref[...], preferred_element_type=F32))
        yield
        lgt = lg.T
        logit = lgt[0:ne] + lgt[ne:2 * ne]
        mx = jnp.max(logit, axis=0, keepdims=True)
        ex = jnp.exp(logit - mx)
        aff_ref[0, :, rows] = ex / jnp.sum(ex, axis=0, keepdims=True)
        yield

    _staggered([sub_tile(r) for r in _row_groups(x_ref.shape[0])], 7)


def _merge(x2, mods, of, ob, sg, mixed, wm, wgla, wpool, wout, gn, wr, rows_per_sample, heads, ne, tm):
    n, d = x2.shape
    bsz = n // rows_per_sample
    tps = rows_per_sample // tm
    row = lambda i: (i, 0)
    sample_of = lambda i, *_: i // tps
    vecs = [MOD_MULT1, MOD_SHIFT1, MOD_GATE1, MOD_MULT2, MOD_SHIFT2]
    full = lambda a: pl.BlockSpec(a.shape, lambda i: (0,) * a.ndim, pipeline_mode=pl.Buffered(1))
    gv = of.shape[-1]
    return pl.pallas_call(
        functools.partial(_merge_body, heads=heads, ne=ne),
        grid=(n // tm,),
        in_specs=[pl.BlockSpec((tm, d), row)] + [_mod_spec(d, v, sample_of) for v in vecs]
                 + [pl.BlockSpec((tm, gv), row)] * 4
                 + [full(wm), full(wgla), full(wpool), full(wout), full(gn), full(wr)],
        out_specs=[pl.BlockSpec((tm, d), row),
                   pl.BlockSpec((tm, d), row),
                   pl.BlockSpec((1, ne, tm), lambda i: (i // tps, 0, i % tps))],
        out_shape=[jax.ShapeDtypeStruct((n, d), F32),
                   jax.ShapeDtypeStruct((n, d), BF16),
                   jax.ShapeDtypeStruct((bsz, ne, rows_per_sample), F32)],
        compiler_params=_cparams(("parallel",)),
        name="merge",
    )(x2, *[mods] * len(vecs), of, ob, sg, mixed, wm, wgla, wpool, wout, gn, wr)


def _route_body(aff_ref, pos_ref, off_ref, cnt_ref, *, cap, ntb):
    a = aff_ref[0]
    ne, l = a.shape
    blk = ROUTE_BLK

    def bisect(i, v):
        cand = v | jnp.left_shift(jnp.int32(1), 30 - i)
        cnt = jnp.sum((a >= lax.bitcast_convert_type(cand, F32)).astype(F32), axis=1, keepdims=True)
        return jnp.where(cnt >= cap, cand, v)

    thr = lax.bitcast_convert_type(lax.fori_loop(0, 31, bisect, jnp.zeros((ne, 1), I32)), F32)
    gt = a > thr
    tie = a == thr
    need = cap - jnp.sum(gt.astype(F32), axis=1, keepdims=True)

    r = lax.broadcasted_iota(I32, (blk, blk), 0)
    c = lax.broadcasted_iota(I32, (blk, blk), 1)
    upper = (r <= c).astype(BF16)
    lane = lax.broadcasted_iota(I32, (1, LANES), 1)

    def prefix(mask_f):
        run = jnp.zeros((ne, 1), F32)
        offs = jnp.zeros((ne, LANES), F32)
        for tb in range(ntb):
            m = mask_f[:, tb * blk:(tb + 1) * blk].astype(BF16)
            loc = jnp.dot(m, upper, preferred_element_type=F32)
            cnt_ref[:, tb * blk:(tb + 1) * blk] = loc + run
            offs = jnp.where(lane == tb, run, offs)
            run = run + loc[:, blk - 1:blk]
        return jnp.where(lane >= ntb, run, offs)

    tie_f = tie.astype(F32)
    prefix(tie_f)
    tie_excl = cnt_ref[...] - tie_f
    sel = gt | (tie & (tie_excl < need))
    offs = prefix(sel.astype(F32))
    pos_ref[0] = jnp.where(sel, cnt_ref[...] - 1.0, -1.0).astype(I32)
    off_ref[0] = offs.astype(I32)


def _route(aff, cap):
    bsz, ne, l = aff.shape
    spec = lambda s: pl.BlockSpec((1,) + s, lambda b: (b, 0, 0))
    return pl.pallas_call(
        functools.partial(_route_body, cap=cap, ntb=l // ROUTE_BLK),
        grid=(bsz,),
        in_specs=[spec((ne, l))],
        out_specs=[spec((ne, l)), spec((ne, LANES))],
        out_shape=[jax.ShapeDtypeStruct((bsz, ne, l), I32),
                   jax.ShapeDtypeStruct((bsz, ne, LANES), I32)],
        scratch_shapes=[pltpu.VMEM((ne, l), F32)],
        compiler_params=_cparams(("parallel",)),
        name="route",
    )(aff)


def _window_plan(off_ref, bb, tt, experts, ne):
    lows = [off_ref[(bb * ne + e) * LANES + tt] & -BF16_ROWS for e in experts]
    ends = [off_ref[(bb * ne + e) * LANES + tt + 1] for e in experts]
    return lows, ends


def _window_rounds(lows, ends):
    rounds = jnp.int32(0)
    for lo, hi in zip(lows, ends):
        rounds = jnp.maximum(rounds, lax.div(hi - lo + (SLOT_WIN - 1), jnp.int32(SLOT_WIN)))
    return rounds


def _gatherx_body(off_ref, h_ref, pos_ref, xs_ref, *, ne, cap, eg):
    b = pl.program_id(0)
    g = pl.program_id(1)
    tb = pl.program_id(2)
    win = SLOT_WIN

    @pl.when(tb == 0)
    def _():
        xs_ref[...] = jnp.zeros(xs_ref.shape, BF16)

    j_col = lax.broadcasted_iota(I32, (win, 1), 0)
    t = ROUTE_BLK
    nsub = h_ref.shape[0] // t
    experts = [g * eg + k for k in range(eg)]

    def select(sub, lows, r):
        starts = [pl.multiple_of(jnp.minimum(lows[k] + r * win, cap - win), BF16_ROWS) for k in range(eg)]
        pieces = []
        for k in range(eg):
            p = pos_ref[0, k:k + 1, sub * t:(sub + 1) * t]
            hit = (p - starts[k] == j_col) & (p >= lows[k] + r * win)
            pieces.append(jnp.where(hit, 1.0, 0.0).astype(BF16))
        sel = jnp.concatenate(pieces, axis=0)
        rows = jnp.dot(sel, h_ref[sub * t:(sub + 1) * t, :], preferred_element_type=F32).astype(BF16)
        return starts, rows

    def deposit(starts, rows):
        for k in range(eg):
            dst = (0, k, pl.ds(starts[k], win), slice(None))
            xs_ref[dst] = xs_ref[dst] + rows[k * win:(k + 1) * win]

    plans = [_window_plan(off_ref, b, tb * nsub + sub, experts, ne) for sub in range(nsub)]
    firsts = [select(sub, plans[sub][0], 0) for sub in range(nsub)]
    for starts, rows in firsts:
        deposit(starts, rows)
    for sub in range(nsub):
        lows, ends = plans[sub]

        def extra_round(r, carry, lows=lows, sub=sub):
            deposit(*select(sub, lows, r))
            return carry

        lax.fori_loop(1, _window_rounds(lows, ends), extra_round, 0)


def _gatherx(off_flat, h2, pos, cap):
    n, d = h2.shape
    bsz, ne, l = pos.shape
    t = min(l, 2 * TOKEN_TILE)
    ntb = l // t
    eg = SUBLANES
    grid_spec = pltpu.PrefetchScalarGridSpec(
        num_scalar_prefetch=1,
        grid=(bsz, ne // eg, ntb),
        in_specs=[pl.BlockSpec((t, d), lambda b, g, i, off: (b * ntb + i, 0)),
                  pl.BlockSpec((1, eg, t), lambda b, g, i, off: (b, g, i))],
        out_specs=pl.BlockSpec((1, eg, cap, d), lambda b, g, i, off: (b, g, 0, 0)),
    )
    return pl.pallas_call(
        functools.partial(_gatherx_body, ne=ne, cap=cap, eg=eg),
        grid_spec=grid_spec,
        out_shape=jax.ShapeDtypeStruct((bsz, ne, cap, d), BF16),
        compiler_params=_cparams(("parallel", "parallel", "arbitrary")),
        name="gatherx",
    )(off_flat, h2, pos)


def _moe_body(xs_ref, wg_ref, wu_ref, wd_ref, y_ref, *, rc):
    wg = wg_ref[0].astype(BF16)
    wu = wu_ref[0].astype(BF16)
    wd = wd_ref[0].astype(BF16)
    cap = xs_ref.shape[2]
    acts = []
    for ch in range(cap // rc):
        xs = xs_ref[0, 0, ch * rc:(ch + 1) * rc, :]
        gate = jnp.dot(xs, wg, preferred_element_type=F32)
        up = jnp.dot(xs, wu, preferred_element_type=F32)
        acts.append((_silu(gate) * up).astype(BF16))
    for ch in range(cap // rc):
        y_ref[0, 0, ch * rc:(ch + 1) * rc, :] = jnp.dot(acts[ch], wd, preferred_element_type=F32).astype(BF16)


def _moe(xs, wg, wu, wd):
    bsz, ne, cap, d = xs.shape
    de = wg.shape[2]
    slot = pl.BlockSpec((1, 1, cap, d), lambda e, b: (b, e, 0, 0))
    return pl.pallas_call(
        functools.partial(_moe_body, rc=min(cap, 2 * MXU_DIM)),
        grid=(ne, bsz),
        in_specs=[slot,
                  pl.BlockSpec((1, d, de), lambda e, b: (e, 0, 0)),
                  pl.BlockSpec((1, d, de), lambda e, b: (e, 0, 0)),
                  pl.BlockSpec((1, de, d), lambda e, b: (e, 0, 0))],
        out_specs=slot,
        out_shape=jax.ShapeDtypeStruct((bsz, ne, cap, d), BF16),
        compiler_params=_cparams(("parallel", "arbitrary")),
        name="moe",
    )(xs, wg, wu, wd)


def _combine_body(off_ref, x1_ref, g2_ref, fg_ref, pos_ref, aff_ref, y_hbm, o_ref, ybuf, spare, acc_ref, sem,
                  spare_sem, *, ne, cap, nsp, nsub, span):
    b = pl.program_id(0)
    i = pl.program_id(1)
    step = b * nsp + i
    nsteps = pl.num_programs(0) * nsp
    slot = step % 2
    win = SLOT_WIN
    t = ROUTE_BLK
    experts = range(ne)

    def window_starts(lows, r):
        return [pl.multiple_of(jnp.minimum(lows[e] + r * win, cap - win), BF16_ROWS) for e in experts]

    def span_starts(bb, ii):
        lows = _window_plan(off_ref, bb, ii * nsub, experts, ne)[0]
        return [pl.multiple_of(jnp.minimum(lows[e], cap - span), BF16_ROWS) for e in experts]

    def span_copy(bb, e, start, buf):
        return pltpu.make_async_copy(y_hbm.at[bb, e, pl.ds(start, span), :], ybuf.at[buf, e], sem.at[buf, e])

    def spare_copy(e, start):
        return pltpu.make_async_copy(y_hbm.at[b, e, pl.ds(start, win), :],
                                     spare.at[pl.ds(e * win, win), :], spare_sem.at[e])

    def start_step(bb, ii, buf):
        starts = span_starts(bb, ii)
        for e in experts:
            span_copy(bb, e, starts[e], buf).start(priority=1)

    acc_ref[...] = jnp.zeros(acc_ref.shape, F32)

    @pl.when(step == 0)
    def _():
        start_step(b, i, 0)

    nxt = jnp.minimum(step + 1, nsteps - 1)
    start_step(lax.div(nxt, jnp.int32(nsp)), lax.rem(nxt, jnp.int32(nsp)), 1 - slot)

    plans = [_window_plan(off_ref, b, i * nsub + sub, experts, ne) for sub in range(nsub)]
    mine = span_starts(b, i)
    firsts = [window_starts(plans[sub][0], 0) for sub in range(nsub)]
    inside = []
    for sub in range(nsub):
        ok = jnp.bool_(True)
        for e in experts:
            ok = ok & (firsts[sub][e] >= mine[e]) & (firsts[sub][e] + win <= mine[e] + span)
        inside.append(ok)
    j_col = lax.broadcasted_iota(I32, (win, 1), 0)

    def expand(sub, r, starts, rows, enabled=None):
        lows = plans[sub][0]
        pieces = []
        for e in experts:
            p = pos_ref[0, e:e + 1, sub * t:(sub + 1) * t]
            valid = p >= lows[e] + r * win
            hit = (p - starts[e] == j_col) & valid
            if enabled is not None:
                hit = hit & enabled
            pieces.append(jnp.where(hit, aff_ref[0, e:e + 1, sub * t:(sub + 1) * t], 0.0).astype(BF16))
        pmat = jnp.concatenate(pieces, axis=0)
        return lax.dot_general(pmat, rows, _TN, preferred_element_type=F32)

    for sub in range(nsub):
        lows, ends = plans[sub]

        def own_round(r, carry, lows=lows, sub=sub):
            starts = window_starts(lows, r)
            for e in experts:
                spare_copy(e, starts[e]).start()
            for e in experts:
                spare_copy(e, starts[e]).wait()
            acc_ref[sub] += expand(sub, r, starts, spare[...])
            return carry

        lax.fori_loop(jnp.where(inside[sub], 1, 0), _window_rounds(lows, ends), own_round, 0)

    for e in experts:
        span_copy(b, e, mine[e], slot).wait()
    for sub in range(nsub):
        rows = pl.ds(sub * t, t)
        picked = []
        for e in experts:
            local = pl.multiple_of(jnp.clip(firsts[sub][e] - mine[e], 0, span - win), BF16_ROWS)
            picked.append(ybuf[slot, e, pl.ds(local, win), :])
        moe = expand(sub, 0, firsts[sub], jnp.concatenate(picked, axis=0), inside[sub]) + acc_ref[sub]
        x2 = x1_ref[rows, :] + g2_ref[0] * moe
        ms = jnp.mean(x2 * x2, axis=-1, keepdims=True)
        o_ref[rows, :] = x2 * lax.rsqrt(ms + EPS) * fg_ref[...]

    @pl.when(step == nsteps - 1)
    def _():
        for e in experts:
            span_copy(b, e, mine[e], 1 - slot).wait()


def _combine(off_flat, x1, mods, fg, pos, aff, y, rows_per_sample):
    n, d = x1.shape
    bsz, ne, cap, _ = y.shape
    t = min(rows_per_sample, TOKEN_TILE)
    nsub = t // ROUTE_BLK
    nsp = rows_per_sample // t
    rows = ne * SLOT_WIN
    span = min(cap, nsub * SLOT_WIN * 3 // 4)
    grid_spec = pltpu.PrefetchScalarGridSpec(
        num_scalar_prefetch=1,
        grid=(bsz, nsp),
        in_specs=[pl.BlockSpec((t, d), lambda b, i, off: (b * nsp + i, 0)),
                  _mod_spec(d, MOD_GATE2, lambda b, i: b),
                  pl.BlockSpec((1, d), lambda b, i, off: (0, 0)),
                  pl.BlockSpec((1, ne, t), lambda b, i, off: (b, 0, i)),
                  pl.BlockSpec((1, ne, t), lambda b, i, off: (b, 0, i)),
                  pl.BlockSpec(memory_space=pl.ANY)],
        out_specs=pl.BlockSpec((t, d), lambda b, i, off: (b * nsp + i, 0)),
        scratch_shapes=[pltpu.VMEM((2, ne, span, d), BF16),
                        pltpu.VMEM((rows, d), BF16),
                        pltpu.VMEM((nsub, ROUTE_BLK, d), F32),
                        pltpu.SemaphoreType.DMA((2, ne)),
                        pltpu.SemaphoreType.DMA((ne,))],
    )
    return pl.pallas_call(
        functools.partial(_combine_body, ne=ne, cap=cap, nsp=nsp, nsub=nsub, span=span),
        grid_spec=grid_spec,
        out_shape=jax.ShapeDtypeStruct((n, d), F32),
        compiler_params=_cparams(("arbitrary", "arbitrary")),
        name="combine",
    )(off_flat, x1, mods, fg, pos, aff, y)


def kernel(x, c, ctx, c_ctx, ada_w, ada_b, norm1_g, norm2_g, w_in, w_decay_up, b_decay, gla_norm_g,
           w_gla_proj, pool_w, pool_scale, w_pool_proj, w_out, w_router, w_gate_e, w_up_e, w_down_e,
           final_norm_g):
    assert ada_w.shape[0] == 1, "single-layer block"
    bsz, l, d = x.shape
    lc = ctx.shape[1]
    rank, dk = w_decay_up.shape[2], w_decay_up.shape[3]
    dvh = gla_norm_g.shape[1]
    dv = w_gla_proj.shape[1]
    heads = dv // dvh
    groups, ch = pool_w.shape[1], pool_w.shape[2]
    pw = groups * ch
    ne = w_router.shape[2]
    cap = EC_CAPACITY * l // ne
    assert dk // heads == HEAD_DK and dvh == LANES and ch == LANES and heads % 2 == 0
    assert l % ROUTE_BLK == 0 and l // ROUTE_BLK < LANES and cap >= SLOT_WIN and cap % BF16_ROWS == 0
    assert GRID_W == GLA_CHUNK and MXU_DIM % GRID_W == 0 and (l // MXU_DIM) % 2 == 0

    assert bsz < COND_ROWS
    cin = jnp.zeros((COND_ROWS, d), F32).at[:bsz].set(c).at[bsz].set(c_ctx)
    mods = _ada(cin, ada_w[0], ada_b[0][None, :], jnp.concatenate([norm1_g, norm2_g], axis=0))

    o_r = dk + dv
    o_q = o_r + 2 * rank
    o_g = o_q + dk
    o_p = o_g + dv
    o_m = o_p + pw
    up = jnp.zeros((2 * rank, 2 * dk), F32)
    up = up.at[:rank, :dk].set(w_decay_up[0, 0]).at[rank:, dk:].set(w_decay_up[0, 1])
    psc = pool_scale[0].reshape(groups, 1, ch)
    w1, wm, wpool = _prep(w_in[0].T, up, pool_w[0], psc, w_pool_proj[0], o_r, o_q, o_g, o_m,
                          float(dk // heads) ** -0.5)
    bz = b_decay[0]

    zero_state = jnp.zeros((bsz, heads // 2, LANES, 2 * LANES), F32)
    ck, cv, _, claf, clab, _, _ = _inproj(ctx.reshape(bsz * lc, d), mods, lambda i: bsz, w1, bz, dk, dv, pw,
                                          min(lc, CTX_TILE))
    r3 = lambda a, n: a.reshape(bsz, n, a.shape[-1])
    _, _, h_f, h_b = _gla(r3(ck, lc), r3(ck, lc), r3(cv, lc), r3(claf, lc), r3(clab, lc),
                          zero_state, zero_state, min(lc, CTX_TILE))

    x2 = x.reshape(bsz * l, d)
    tm = min(l, TOKEN_TILE)
    k, v, q, laf, lab, sg, pin = _inproj(x2, mods, lambda i: i // (l // tm), w1, bz, dk, dv, pw, tm)
    of, ob, _, _ = _gla(r3(k, l), r3(q, l), r3(v, l), r3(laf, l), r3(lab, l), h_f, h_b, min(l, GLA_TILE))
    pooled = _pool(r3(pin, l), ch).reshape(bsz * l, pw)

    wr_hi = w_router[0].astype(BF16)
    wr_lo = (w_router[0] - wr_hi.astype(F32)).astype(BF16)
    wr = jnp.zeros((d, LANES), BF16).at[:, :ne].set(wr_hi).at[:, ne:2 * ne].set(wr_lo)
    x1, h2, aff = _merge(x2, mods, of.reshape(bsz * l, dv), ob.reshape(bsz * l, dv), sg, pooled,
                          wm, w_gla_proj[0].astype(BF16), wpool, w_out[0].astype(BF16),
                          gla_norm_g[0][None, :], wr, l, heads, ne, tm)

    pos, offs = _route(aff.reshape(1, bsz * ne, l), cap)
    pos = pos.reshape(bsz, ne, l)
    off_flat = offs.reshape(-1)
    xs = _gatherx(off_flat, h2, pos, cap)
    y = _moe(xs, w_gate_e[0], w_up_e[0], w_down_e[0])
    out = _combine(off_flat, x1, mods, final_norm_g[None, :], pos, aff, y, l)
    return out.reshape(bsz, l, d)
```
